```python
import jax, jax.numpy as jnp
from jax import lax
import numpy as np

D_MODEL = 1024
BATCH = 16
SEQ = 256
DEPTH = 2
DEC_BATCH = 2
DEC_SEQ = 4096
PAST_LEN = 256

GRID_W = 64
N_MIXERS = 2
N_HGRN = (DEPTH + 1) // 2
N_CONV = DEPTH // 2
HG_DK = 128
HG_HEADS = D_MODEL // HG_DK
HG_DV = D_MODEL // HG_HEADS
HG_DIM = HG_HEADS * HG_DK
CHUNK = 64
N_EXPERTS = 16
N_GROUPS = 4
EXPERTS_PER_GROUP = N_EXPERTS // N_GROUPS
TOP_K = 2
D_FF_EXPERT = 512
EPS = 1e-6

kernel_name = "hybrid_hgrn2_shortconv_moe_diffusion_step"


def rmsnorm(x, g):
    xf = x.astype(jnp.float32)
    y = xf * lax.rsqrt(jnp.mean(xf * xf, axis=-1, keepdims=True) + EPS)
    return (y * g.astype(jnp.float32)).astype(x.dtype)


def modulate(x, g, shift, scale):
    return rmsnorm(x, g) * (1.0 + scale) + shift


def adaln_params(cond, w_mod, b_mod):
    mod = (cond @ w_mod + b_mod).reshape(cond.shape[0], 6, 1, D_MODEL)
    return [mod[:, k] for k in range(6)]


def gla_chunked(q, k, v, logf, s0):
    b_, t_, h_, dk = q.shape
    dv = v.shape[-1]
    nc = t_ // CHUNK
    def chunks(a):
        return a.reshape(b_, nc, CHUNK, h_, a.shape[-1])
    q, k, v, logf = chunks(q), chunks(k), chunks(v), chunks(logf)
    cum = jnp.cumsum(logf, axis=2)
    ref = cum[:, :, CHUNK // 2:CHUNK // 2 + 1]
    last = cum[:, :, -1:]
    scores = jnp.einsum('bclhd,bcmhd->bchlm', q * jnp.exp(cum - ref), k * jnp.exp(ref - cum))
    causal = jnp.tril(jnp.ones((CHUNK, CHUNK), dtype=bool))
    scores = jnp.where(causal, scores, 0.0)
    o_intra = jnp.einsum('bchlm,bcmhv->bclhv', scores, v)
    d_state = jnp.einsum('bclhd,bclhv->bchdv', k * jnp.exp(last - cum), v)
    decay = jnp.exp(last[:, :, 0])

    def step(s, inp):
        dec, ds = inp
        return dec[..., None] * s + ds, s

    s_final, s_prev = lax.scan(step, s0.astype(d_state.dtype),
                               (jnp.moveaxis(decay, 1, 0), jnp.moveaxis(d_state, 1, 0)))
    s_prev = jnp.moveaxis(s_prev, 0, 1)
    o_inter = jnp.einsum('bclhd,bchdv->bclhv', q * jnp.exp(cum), s_prev)
    return (o_intra + o_inter).reshape(b_, t_, h_, dv), s_final


def hgrn2_mix(h, w_in, w_out, norm_g, lb, s0_f, s0_b):
    bsz, t_, _ = h.shape
    q, v, f_fwd, f_bwd, g = jnp.split(h @ w_in, 5, axis=-1)
    def heads(a):
        return a.reshape(bsz, t_, HG_HEADS, a.shape[-1] // HG_HEADS)
    def gates(raw):
        f = lb + (1.0 - lb) * jax.nn.sigmoid(raw.astype(jnp.float32))
        return heads(1.0 - f), heads(jnp.log(f))
    k_f, lf_f = gates(f_fwd)
    k_b, lf_b = gates(f_bwd)
    q = heads(q.astype(jnp.float32)) * (HG_DK ** -0.5)
    v = heads(v.astype(jnp.float32))
    o_f, s_f = gla_chunked(q, k_f, v, lf_f, s0_f)
    o_b, s_b = gla_chunked(q[:, ::-1], k_b[:, ::-1], v[:, ::-1], lf_b[:, ::-1], s0_b)
    o = o_f + o_b[:, ::-1]
    o = rmsnorm(o, norm_g) * jax.nn.silu(heads(g.astype(jnp.float32)))
    out = o.reshape(bsz, t_, HG_HEADS * HG_DV).astype(h.dtype) @ w_out
    return out, s_f, s_b


def dwconv3(u, w):
    pad = [(0, 0)] * (u.ndim - 2) + [(1, 1), (0, 0)]
    up = jnp.pad(u, pad)
    return w[0] * up[..., :-2, :] + w[1] * up[..., 1:-1, :] + w[2] * up[..., 2:, :]


def shortconv_mix(h, w_in, conv_w, w_out, grid):
    bsz, t_, d = h.shape
    bg, cg, x_in = jnp.split(h @ w_in, 3, axis=-1)
    u = cg * x_in
    if grid:
        rows = t_ // GRID_W
        conv = dwconv3(u.reshape(bsz, rows, GRID_W, d), conv_w).reshape(bsz, t_, d)
    else:
        conv = dwconv3(u, conv_w)
    return (bg * conv) @ w_out


def moe(h, router_w, router_b, w_gate, w_up, w_down):
    bsz, t_, d = h.shape
    hf = h.reshape(-1, d)
    scores = jax.nn.sigmoid((hf @ router_w).astype(jnp.float32))
    sel = (scores + router_b.astype(jnp.float32)).reshape(-1, N_GROUPS, EXPERTS_PER_GROUP)
    group_score = lax.top_k(sel, 2)[0].sum(-1)
    g_idx = jnp.argmax(group_score, axis=-1)
    sel_in = jnp.take_along_axis(sel, g_idx[:, None, None], axis=1)[:, 0]
    _, local = lax.top_k(sel_in, TOP_K)
    expert = g_idx[:, None] * EXPERTS_PER_GROUP + local
    gate = jnp.take_along_axis(scores, expert, axis=-1)
    gate = gate / jnp.sum(gate, axis=-1, keepdims=True)
    combine = jnp.sum(jax.nn.one_hot(expert, N_EXPERTS, dtype=jnp.float32) * gate[..., None], axis=1)
    act = jax.nn.silu(jnp.einsum('nd,edf->nef', hf, w_gate)) * jnp.einsum('nd,edf->nef', hf, w_up)
    act = act * combine[:, :, None].astype(act.dtype)
    y = jnp.einsum('nef,efd->nd', act, w_down)
    return y.reshape(bsz, t_, d)


def setup_inputs(seed: int = 0) -> dict:
    key = jax.random.key(seed)
    ks = jax.random.split(key, 24)
    nrm = jax.random.normal
    d = D_MODEL
    return {
        "x_prompt": nrm(ks[0], (BATCH, SEQ, d), jnp.float32),
        "x_sample": nrm(ks[1], (DEC_BATCH, DEC_SEQ, d), jnp.float32),
        "state_hgrn": 0.5 * nrm(ks[2], (DEC_BATCH, N_HGRN, 2, HG_HEADS, HG_DK, HG_DV), jnp.float32),
        "c": nrm(ks[3], (DEC_BATCH, d), jnp.float32),
        "c_ctx": nrm(ks[4], (d,), jnp.float32),
        "w_mod": 0.5 * d ** -0.5 * nrm(ks[5], (DEPTH, d, 6 * d), jnp.float32),
        "b_mod": 0.02 * nrm(ks[6], (DEPTH, 6 * d), jnp.float32),
        "norm_mix_g": 1.0 + 0.02 * nrm(ks[7], (DEPTH, d), jnp.float32),
        "norm_ffn_g": 1.0 + 0.02 * nrm(ks[8], (DEPTH, d), jnp.float32),
        "norm_final_g": 1.0 + 0.02 * nrm(ks[9], (d,), jnp.float32),
        "hgrn_w_in": d ** -0.5 * nrm(ks[10], (N_HGRN, d, 5 * HG_DIM), jnp.float32),
        "hgrn_lb_logits": 0.1 * nrm(ks[11], (DEPTH + 1, HG_DIM), jnp.float32),
        "hgrn_norm_g": 1.0 + 0.02 * nrm(ks[12], (N_HGRN, HG_DV), jnp.float32),
        "hgrn_w_out": HG_DIM ** -0.5 * nrm(ks[13], (N_HGRN, HG_DIM, d), jnp.float32),
        "conv_w_in": d ** -0.5 * nrm(ks[14], (N_CONV, d, 3 * d), jnp.float32),
        "conv_w": 3 ** -0.5 * nrm(ks[15], (N_CONV, 3, d), jnp.float32),
        "conv_w_out": d ** -0.5 * nrm(ks[16], (N_CONV, d, d), jnp.float32),
        "router_w": d ** -0.5 * nrm(ks[17], (d, N_EXPERTS), jnp.float32),
        "router_b": 0.01 * nrm(ks[18], (N_EXPERTS,), jnp.float32),
        "moe_w_gate": d ** -0.5 * nrm(ks[19], (DEPTH, N_EXPERTS, d, D_FF_EXPERT), jnp.float32),
        "moe_w_up": d ** -0.5 * nrm(ks[20], (DEPTH, N_EXPERTS, d, D_FF_EXPERT), jnp.float32),
        "moe_w_down": D_FF_EXPERT ** -0.5 * nrm(ks[21], (DEPTH, N_EXPERTS, D_FF_EXPERT, d), jnp.float32),
    }


def reference(x_prompt, x_sample, state_hgrn, c, c_ctx, w_mod, b_mod, norm_mix_g, norm_ffn_g,
              norm_final_g, hgrn_w_in, hgrn_lb_logits, hgrn_norm_g, hgrn_w_out, conv_w_in, conv_w,
              conv_w_out, router_w, router_b, moe_w_gate, moe_w_up, moe_w_down):
    lower_bounds = jnp.cumsum(jax.nn.softmax(hgrn_lb_logits.astype(jnp.float32), axis=0), axis=0)
    cond_ctx = jax.nn.silu(c_ctx)[None]
    cond_lat = jax.nn.silu(c)
    xp, xs = x_prompt, x_sample
    new_states = []
    for i in range(DEPTH):
        sp1, sc1, gp1, sp2, sc2, gp2 = adaln_params(cond_ctx, w_mod[i], b_mod[i])
        ss1, scs1, gs1, ss2, scs2, gs2 = adaln_params(cond_lat, w_mod[i], b_mod[i])
        hp = modulate(xp, norm_mix_g[i], sp1, sc1)
        hs = modulate(xs, norm_mix_g[i], ss1, scs1)
        j = i // N_MIXERS
        if i % N_MIXERS == 0:
            zero = jnp.zeros((xp.shape[0], HG_HEADS, HG_DK, HG_DV), jnp.float32)
            mp, s_f, s_b = hgrn2_mix(hp, hgrn_w_in[j], hgrn_w_out[j], hgrn_norm_g[j],
                                     lower_bounds[i], zero, zero)
            new_states.append(jnp.stack([s_f, s_b], axis=1))
            ms, _, _ = hgrn2_mix(hs, hgrn_w_in[j], hgrn_w_out[j], hgrn_norm_g[j],
                                 lower_bounds[i], state_hgrn[:, j, 0], state_hgrn[:, j, 1])
        else:
            mp = shortconv_mix(hp, conv_w_in[j], conv_w[j], conv_w_out[j], grid=False)
            ms = shortconv_mix(hs, conv_w_in[j], conv_w[j], conv_w_out[j], grid=True)
        xp = xp + gp1 * mp
        xs = xs + gs1 * ms
        hp = modulate(xp, norm_ffn_g[i], sp2, sc2)
        hs = modulate(xs, norm_ffn_g[i], ss2, scs2)
        xp = xp + gp2 * moe(hp, router_w, router_b, moe_w_gate[i], moe_w_up[i], moe_w_down[i])
        xs = xs + gs2 * moe(hs, router_w, router_b, moe_w_gate[i], moe_w_up[i], moe_w_down[i])
    y_prompt = rmsnorm(xp, norm_final_g)
    y_sample = rmsnorm(xs, norm_final_g)
    new_state_hgrn = jnp.stack(new_states, axis=1)
    return (y_prompt, y_sample, new_state_hgrn)
```

```python
import functools

import jax
import jax.numpy as jnp
from jax import lax
from jax.experimental import pallas as pl
from jax.experimental.pallas import tpu as pltpu

F32 = jnp.float32
BF16 = jnp.bfloat16

D = 1024
N_PROMPT_SEQ = 16
PROMPT_LEN = 256
N_SAMPLE_SEQ = 2
SAMPLE_LEN = 4096
GRID_W = 64
T_PROMPT = N_PROMPT_SEQ * PROMPT_LEN
T_SAMPLE = N_SAMPLE_SEQ * SAMPLE_LEN
T = T_PROMPT + T_SAMPLE
N_COND = 1 + N_SAMPLE_SEQ
COND_ROWS = 8

HEADS = 8
DK = 128
DV = 128
CHUNK = 64
N_EXPERTS = 16
N_GROUPS = 4
EPG = 4
N_PAIRS = 6
N_BUCKETS = N_GROUPS * N_PAIRS
D_FF = 512
EPS = 1e-6

TM = 256
N_TILES = T // TM
PROMPT_TILES = T_PROMPT // TM
TILES_PER_SAMPLE = SAMPLE_LEN // TM
CHUNKS_PER_TILE = TM // CHUNK
TM_E = 256
N_TILES_E = T // TM_E + N_BUCKETS
S_ROWS = N_TILES_E * TM_E
XW = D + 128
LANES = 128
MOD_TN = 1536

VMEM_LIMIT = 56 * 1024 * 1024

_PAIR_A = (0, 0, 0, 1, 1, 2)
_PAIR_B = (1, 2, 3, 2, 3, 3)


def _cparams():
    return pltpu.CompilerParams(dimension_semantics=("arbitrary",), vmem_limit_bytes=VMEM_LIMIT)


def _cond_of_tile(t):
    return jnp.where(t < PROMPT_TILES, 0, 1 + (t - PROMPT_TILES) // TILES_PER_SAMPLE)


def _sample_of_tile(t):
    return jnp.clip((t - PROMPT_TILES) // TILES_PER_SAMPLE, 0, N_SAMPLE_SEQ - 1)


def _rmsnorm(x, g):
    ms = jnp.mean(x * x, axis=-1, keepdims=True)
    return (x * lax.rsqrt(ms + EPS)) * g


def _modulate(x, g, shift, scale):
    return _rmsnorm(x, g) * (1.0 + scale) + shift


def _silu(x):
    return x * jax.nn.sigmoid(x)


def _dot(a, b):
    return jnp.dot(a, b, preferred_element_type=F32)


def _dot_nt(a, b):
    return lax.dot_general(a, b, (((1,), (1,)), ((), ())), preferred_element_type=F32)


def _dot_tn(a, b):
    return lax.dot_general(a, b, (((0,), (0,)), ((), ())), preferred_element_type=F32)


def _split_bf16(x):
    hi = x.astype(BF16)
    lo = (x - hi.astype(F32)).astype(BF16)
    return hi, lo


def _mod_kernel(cond_ref, w_ref, b_ref, o_ref):
    s = _silu(cond_ref[...])
    o_ref[0] = _dot(s.astype(BF16), w_ref[0].astype(BF16)) + b_ref[0]


def _adaln(cond, w_mod, b_mod):
    depth = w_mod.shape[0]
    return pl.pallas_call(
        _mod_kernel,
        grid=(depth, 6 * D // MOD_TN),
        in_specs=[
            pl.BlockSpec((COND_ROWS, D), lambda i, j: (0, 0)),
            pl.BlockSpec((1, D, MOD_TN), lambda i, j: (i, 0, j)),
            pl.BlockSpec((1, 1, MOD_TN), lambda i, j: (i, 0, j)),
        ],
        out_specs=pl.BlockSpec((1, COND_ROWS, MOD_TN), lambda i, j: (i, 0, j)),
        out_shape=jax.ShapeDtypeStruct((depth, COND_ROWS, 6 * D), F32),
        compiler_params=pltpu.CompilerParams(
            dimension_semantics=("arbitrary", "arbitrary"), vmem_limit_bytes=VMEM_LIMIT),
        name="adaln",
    )(cond, w_mod, b_mod.reshape(depth, 1, 6 * D))


_MOD_SPEC = pl.BlockSpec((1, 6, 1, D), lambda i: (_cond_of_tile(i), 0, 0, 0))
_ROW_SPEC = pl.BlockSpec((1, D), lambda i: (0, 0))
_TILE_SPEC = pl.BlockSpec((TM, D), lambda i: (i, 0))


def _hgrn_proj_kernel(x_ref, mod_ref, g_ref, w_ref, q_ref, v_ref, ff_ref, fb_ref, gg_ref):
    h = _modulate(x_ref[...], g_ref[...], mod_ref[0, 0], mod_ref[0, 1]).astype(BF16)
    q_ref[...] = (_dot(h, w_ref[:, 0 * D:1 * D]) * (DK ** -0.5)).astype(BF16)
    v_ref[...] = _dot(h, w_ref[:, 1 * D:2 * D]).astype(BF16)
    ff_ref[...] = _dot(h, w_ref[:, 2 * D:3 * D])
    fb_ref[...] = _dot(h, w_ref[:, 3 * D:4 * D])
    gg_ref[...] = _dot(h, w_ref[:, 4 * D:5 * D]).astype(BF16)


def _hgrn_proj(x, mod, norm_g, w_in):
    bf = jax.ShapeDtypeStruct((T, D), BF16)
    f32 = jax.ShapeDtypeStruct((T, D), F32)
    return pl.pallas_call(
        _hgrn_proj_kernel,
        grid=(N_TILES,),
        in_specs=[_TILE_SPEC, _MOD_SPEC, _ROW_SPEC, pl.BlockSpec((D, 5 * D), lambda i: (0, 0))],
        out_specs=[_TILE_SPEC] * 5,
        out_shape=[bf, bf, f32, f32, bf],
        compiler_params=_cparams(),
        name="hgrn_proj",
    )(x, mod, norm_g, w_in)


def _gla_tile(t, q_ref, v_ref, f_ref, lbl_ref, s0_ref, st_ref, o_ref, *, reverse):
    is_prompt = t < PROMPT_TILES
    rel = t - PROMPT_TILES
    edge = (TILES_PER_SAMPLE - 1) if reverse else 0
    sample_start = jnp.logical_and(rel >= 0, rel % TILES_PER_SAMPLE == edge)

    @pl.when(is_prompt)
    def _():
        st_ref[...] = jnp.zeros_like(st_ref)

    @pl.when(sample_start)
    def _():
        for h in range(HEADS):
            st_ref[h] = s0_ref[0, 0, h].T

    logits = lbl_ref[...]
    e = jnp.exp(logits - jnp.max(logits, axis=0, keepdims=True))
    lb = e[0:1] / jnp.sum(e, axis=0, keepdims=True)

    row = lax.broadcasted_iota(jnp.int32, (CHUNK, CHUNK), 0)
    col = lax.broadcasted_iota(jnp.int32, (CHUNK, CHUNK), 1)
    keep = (row <= col) if reverse else (row >= col)
    tri = keep.astype(BF16)
    ref_idx = (CHUNK - 1 - CHUNK // 2) if reverse else CHUNK // 2
    last_idx = 0 if reverse else CHUNK - 1

    for cc in range(CHUNKS_PER_TILE):
        c = (CHUNKS_PER_TILE - 1 - cc) if reverse else cc
        rows = slice(c * CHUNK, (c + 1) * CHUNK)
        f = lb + (1.0 - lb) * jax.nn.sigmoid(f_ref[rows, :])
        k = 1.0 - f
        lf_hi, lf_lo = _split_bf16(jnp.log(f))
        cum = _dot(tri, lf_hi) + _dot(tri, lf_lo)
        ref = cum[ref_idx:ref_idx + 1]
        last = cum[last_idx:last_idx + 1]
        a = jnp.exp(cum - ref)
        b = jnp.exp(ref - cum)
        qa32 = q_ref[rows, :].astype(F32) * a
        kb32 = k * b
        qa = qa32.astype(BF16)
        kb = kb32.astype(BF16)
        qe = (qa32 * jnp.exp(ref)).astype(BF16)
        kd = (kb32 * jnp.exp(last - ref)).astype(BF16)
        decay = jnp.exp(last)
        v = v_ref[rows, :]
        for h in range(HEADS):
            hs = slice(h * DK, (h + 1) * DK)
            scores = jnp.where(keep, _dot_nt(qa[:, hs], kb[:, hs]), 0.0).astype(BF16)
            st = st_ref[h]
            o = _dot(scores, v[:, hs]) + _dot_nt(qe[:, hs], st.astype(BF16))
            o_ref[rows, hs] = o
            st_ref[h] = st * decay[:, hs] + _dot_tn(v[:, hs], kd[:, hs])


def _store_prompt_state(t, st_ref, snew_ref):
    @pl.when(t < PROMPT_TILES)
    def _():
        for h in range(HEADS):
            snew_ref[0, h] = st_ref[h].T


def _gla_fwd_kernel(q_ref, v_ref, f_ref, lbl_ref, s0_ref, o_ref, snew_ref, st_ref):
    t = pl.program_id(0)
    _gla_tile(t, q_ref, v_ref, f_ref, lbl_ref, s0_ref, st_ref, o_ref, reverse=False)
    _store_prompt_state(t, st_ref, snew_ref)


def _gla_bwd_kernel(q_ref, v_ref, f_ref, lbl_ref, s0_ref, of_ref, gg_ref, ng_ref,
                    a_ref, snew_ref, st_ref, ob_ref):
    t = N_TILES - 1 - pl.program_id(0)
    _gla_tile(t, q_ref, v_ref, f_ref, lbl_ref, s0_ref, st_ref, ob_ref, reverse=True)
    _store_prompt_state(t, st_ref, snew_ref)
    ng = ng_ref[...]
    for h in range(HEADS):
        hs = slice(h * DV, (h + 1) * DV)
        o = of_ref[:, hs] + ob_ref[:, hs]
        a_ref[:, hs] = (_rmsnorm(o, ng) * _silu(gg_ref[:, hs].astype(F32))).astype(BF16)


def _state_specs(direction, tile_of_step):
    s0 = pl.BlockSpec((1, 1, HEADS, DK, DV),
                      lambda i: (_sample_of_tile(tile_of_step(i)), direction, 0, 0, 0))
    snew = pl.BlockSpec((1, HEADS, DK, DV),
                        lambda i: (jnp.minimum(tile_of_step(i), PROMPT_TILES - 1), 0, 0, 0))
    return s0, snew


def _gla_fwd(q, v, ff, lb_logits, s0):
    s0_spec, snew_spec = _state_specs(0, lambda i: i)
    return pl.pallas_call(
        _gla_fwd_kernel,
        grid=(N_TILES,),
        in_specs=[_TILE_SPEC, _TILE_SPEC, _TILE_SPEC,
                  pl.BlockSpec(lb_logits.shape, lambda i: (0, 0)), s0_spec],
        out_specs=[_TILE_SPEC, snew_spec],
        out_shape=[jax.ShapeDtypeStruct((T, D), F32),
                   jax.ShapeDtypeStruct((N_PROMPT_SEQ, HEADS, DK, DV), F32)],
        scratch_shapes=[pltpu.VMEM((HEADS, DV, DK), F32)],
        compiler_params=_cparams(),
        name="gla_fwd",
    )(q, v, ff, lb_logits, s0)


def _gla_bwd(q, v, fb, lb_logits, s0, o_f, gg, head_norm_g):
    rev = lambda i: N_TILES - 1 - i
    tile = pl.BlockSpec((TM, D), lambda i: (rev(i), 0))
    s0_spec, snew_spec = _state_specs(1, rev)
    return pl.pallas_call(
        _gla_bwd_kernel,
        grid=(N_TILES,),
        in_specs=[tile, tile, tile, pl.BlockSpec(lb_logits.shape, lambda i: (0, 0)), s0_spec,
                  tile, tile, pl.BlockSpec((1, DV), lambda i: (0, 0))],
        out_specs=[tile, snew_spec],
        out_shape=[jax.ShapeDtypeStruct((T, D), BF16),
                   jax.ShapeDtypeStruct((N_PROMPT_SEQ, HEADS, DK, DV), F32)],
        scratch_shapes=[pltpu.VMEM((HEADS, DV, DK), F32), pltpu.VMEM((TM, D), F32)],
        compiler_params=_cparams(),
        name="gla_bwd",
    )(q, v, fb, lb_logits, s0, o_f, gg, head_norm_g)


def _conv_mixer_kernel(x_ref, mod_ref, g_ref, w_ref, cw_ref, a_ref):
    i = pl.program_id(0)
    h = _modulate(x_ref[...], g_ref[...], mod_ref[0, 0], mod_ref[0, 1]).astype(BF16)
    bg = _dot(h, w_ref[:, 0 * D:1 * D])
    u = _dot(h, w_ref[:, 1 * D:2 * D]) * _dot(h, w_ref[:, 2 * D:3 * D])
    seg = jnp.where(i < PROMPT_TILES, PROMPT_LEN, GRID_W)
    pos = lax.broadcasted_iota(jnp.int32, (TM, 1), 0) & (seg - 1)
    prev = jnp.where(pos == 0, 0.0, pltpu.roll(u, 1, axis=0))
    nxt = jnp.where(pos == seg - 1, 0.0, pltpu.roll(u, TM - 1, axis=0))
    conv = cw_ref[0:1] * prev + cw_ref[1:2] * u + cw_ref[2:3] * nxt
    a_ref[...] = (bg * conv).astype(BF16)


def _conv_mixer(x, mod, norm_g, w_in, conv_w):
    return pl.pallas_call(
        _conv_mixer_kernel,
        grid=(N_TILES,),
        in_specs=[_TILE_SPEC, _MOD_SPEC, _ROW_SPEC, pl.BlockSpec((D, 3 * D), lambda i: (0, 0)),
                  pl.BlockSpec((3, D), lambda i: (0, 0))],
        out_specs=_TILE_SPEC,
        out_shape=jax.ShapeDtypeStruct((T, D), BF16),
        compiler_params=_cparams(),
        name="conv_mixer",
    )(x, mod, norm_g, w_in, conv_w)


def _first_index_of(vals, target, lane):
    return jnp.min(jnp.where(vals == target, lane, float(N_EXPERTS)), axis=-1, keepdims=True)


def _route(h2, rw_ref, rb_ref):
    h_hi, h_lo = _split_bf16(h2)
    w_hi, w_lo = _split_bf16(rw_ref[...])
    logits = _dot(h_hi, w_hi) + (_dot(h_hi, w_lo) + _dot(h_lo, w_hi))
    scores = jax.nn.sigmoid(logits)
    sel = scores + rb_ref[...]
    lane = lax.broadcasted_iota(jnp.int32, (TM, N_EXPERTS), 1).astype(F32)
    neg = -jnp.inf
    best = grp = first = second = None
    for g in range(N_GROUPS):
        in_g = jnp.logical_and(lane >= g * EPG, lane < (g + 1) * EPG)
        vals = jnp.where(in_g, sel, neg)
        m1 = jnp.max(vals, axis=-1, keepdims=True)
        i1 = _first_index_of(vals, m1, lane)
        rest = jnp.where(lane == i1, neg, vals)
        m2 = jnp.max(rest, axis=-1, keepdims=True)
        i2 = _first_index_of(rest, m2, lane)
        gs = m1 + m2
        if g == 0:
            best, grp, first, second = gs, jnp.zeros_like(i1), i1, i2
        else:
            upd = gs > best
            best = jnp.where(upd, gs, best)
            grp = jnp.where(upd, float(g), grp)
            first = jnp.where(upd, i1, first)
            second = jnp.where(upd, i2, second)
    e_lo = jnp.minimum(first, second)
    e_hi = jnp.maximum(first, second)
    s_lo = jnp.sum(jnp.where(lane == e_lo, scores, 0.0), axis=-1, keepdims=True)
    s_hi = jnp.sum(jnp.where(lane == e_hi, scores, 0.0), axis=-1, keepdims=True)
    tot = s_lo + s_hi
    a = e_lo - grp * EPG
    b = e_hi - grp * EPG
    pair = jnp.where(a == 0.0, b - 1.0, jnp.where(a == 1.0, b + 1.0, 5.0))
    return grp * N_PAIRS + pair, s_lo / tot, s_hi / tot


def _post_mixer_kernel(a_ref, x_ref, mod_ref, g2_ref, wo_ref, rw_ref, rb_ref,
                       x1_ref, info_ref, cnt_ref, carry_ref):
    i = pl.program_id(0)

    @pl.when(i == 0)
    def _():
        carry_ref[...] = jnp.zeros_like(carry_ref)

    x1 = x_ref[...] + mod_ref[0, 2] * _dot(a_ref[...], wo_ref[...])
    x1_ref[...] = x1
    h2 = _modulate(x1, g2_ref[...], mod_ref[0, 3], mod_ref[0, 4])
    bucket, g_lo, g_hi = _route(h2, rw_ref, rb_ref)

    lane = lax.broadcasted_iota(jnp.int32, (TM, LANES), 1)
    onehot = lane.astype(F32) == bucket
    r = lax.broadcasted_iota(jnp.int32, (TM, TM), 0)
    c = lax.broadcasted_iota(jnp.int32, (TM, TM), 1)
    before = _dot((r > c).astype(BF16), onehot.astype(BF16))
    oh = onehot.astype(F32)
    rank = jnp.sum(oh * (before + carry_ref[...]), axis=-1, keepdims=True)
    carry_ref[...] = carry_ref[...] + jnp.sum(oh, axis=0, keepdims=True)
    cnt_ref[...] = carry_ref[...]
    info = jnp.where(lane == 0, bucket,
                     jnp.where(lane == 1, rank,
                               jnp.where(lane == 2, g_lo, jnp.where(lane == 3, g_hi, 0.0))))
    info_ref[...] = info


def _post_mixer(a, x, mod, norm_g2, w_out, router_w, router_b):
    return pl.pallas_call(
        _post_mixer_kernel,
        grid=(N_TILES,),
        in_specs=[_TILE_SPEC, _TILE_SPEC, _MOD_SPEC, _ROW_SPEC,
                  pl.BlockSpec((D, D), lambda i: (0, 0)),
                  pl.BlockSpec((D, N_EXPERTS), lambda i: (0, 0)),
                  pl.BlockSpec((1, N_EXPERTS), lambda i: (0, 0))],
        out_specs=[_TILE_SPEC, pl.BlockSpec((TM, LANES), lambda i: (i, 0)),
                   pl.BlockSpec((1, LANES), lambda i: (0, 0))],
        out_shape=[jax.ShapeDtypeStruct((T, D), F32), jax.ShapeDtypeStruct((T, LANES), F32),
                   jax.ShapeDtypeStruct((1, LANES), F32)],
        scratch_shapes=[pltpu.VMEM((1, LANES), F32)],
        compiler_params=_cparams(),
        name="post_mixer",
    )(a, x, mod, norm_g2, w_out, router_w, router_b)


_SMEM_TILE_SPEC = pl.BlockSpec((1, 1, TM), lambda i, *_: (i, 0, 0), memory_space=pltpu.SMEM)


def _row_copies(n, make_copy):
    def start(r, carry):
        make_copy(r).start()
        return carry

    def wait(r, carry):
        make_copy(r).wait()
        return carry

    lax.fori_loop(0, n, start, 0)
    lax.fori_loop(0, n, wait, 0)


def _dispatch_kernel(off_ref, partial_ref, bucket_ref, rank_ref, x1_ref, mod_ref, g2_ref, info_ref,
                     xs_ref, buf_ref, zero_ref, sem, zero_sem):
    @pl.when(pl.program_id(0) == 0)
    def _():
        zero_ref[...] = jnp.zeros_like(zero_ref)

        def zero_copy(j):
            return pltpu.make_async_copy(zero_ref, xs_ref.at[pl.ds(j * TM_E, TM_E)], zero_sem)

        def start(j, carry):
            @pl.when(partial_ref[j] != 0)
            def _():
                zero_copy(j).start()
            return carry

        def wait(j, carry):
            @pl.when(partial_ref[j] != 0)
            def _():
                zero_copy(j).wait()
            return carry

        lax.fori_loop(0, N_TILES_E, start, 0)
        lax.fori_loop(0, N_TILES_E, wait, 0)

    h2 = _modulate(x1_ref[...], g2_ref[...], mod_ref[0, 3], mod_ref[0, 4])
    buf_ref[:, :D] = h2
    lane = lax.broadcasted_iota(jnp.int32, (TM, LANES), 1)
    info = info_ref[...]
    buf_ref[:, D:] = jnp.where(lane == 0, info[:, 2:3], jnp.where(lane == 1, info[:, 3:4], 0.0))

    def make_copy(r):
        slot = off_ref[bucket_ref[0, 0, r]] + rank_ref[0, 0, r]
        return pltpu.make_async_copy(buf_ref.at[pl.ds(r, 1)], xs_ref.at[pl.ds(slot, 1)], sem)

    _row_copies(TM, make_copy)


def _dispatch(off, partial, bucket, rank, x1, mod, norm_g2, info):
    tile = lambda w: pl.BlockSpec((TM, w), lambda i, *_: (i, 0))
    return pl.pallas_call(
        _dispatch_kernel,
        grid_spec=pltpu.PrefetchScalarGridSpec(
            num_scalar_prefetch=2,
            grid=(N_TILES,),
            in_specs=[_SMEM_TILE_SPEC, _SMEM_TILE_SPEC, tile(D),
                      pl.BlockSpec((1, 6, 1, D), lambda i, *_: (_cond_of_tile(i), 0, 0, 0)),
                      pl.BlockSpec((1, D), lambda i, *_: (0, 0)), tile(LANES)],
            out_specs=pl.BlockSpec(memory_space=pl.ANY),
            scratch_shapes=[pltpu.VMEM((TM, XW), F32), pltpu.VMEM((TM_E, XW), F32),
                            pltpu.SemaphoreType.DMA(()), pltpu.SemaphoreType.DMA(())],
        ),
        out_shape=jax.ShapeDtypeStruct((S_ROWS, XW), F32),
        compiler_params=_cparams(),
        name="moe_dispatch",
    )(off, partial, bucket, rank, x1, mod, norm_g2, info)


def _expert_kernel(ea_ref, eb_ref, nrows_ref, x_ref, wga_ref, wua_ref, wda_ref,
                   wgb_ref, wub_ref, wdb_ref, y_ref):
    j = pl.program_id(0)
    n = nrows_ref[j]

    @pl.when(n > 0)
    def _():
        valid = lax.broadcasted_iota(jnp.int32, (TM_E, 1), 0) < n
        xe = jnp.where(valid, x_ref[...], 0.0)
        x = xe[:, :D].astype(BF16)

        def ffn(wg_ref, wu_ref, wd_ref, gate):
            act = _silu(_dot(x, wg_ref[0])) * _dot(x, wu_ref[0]) * gate
            return _dot(act.astype(BF16), wd_ref[0])

        y_ref[...] = (ffn(wga_ref, wua_ref, wda_ref, xe[:, D:D + 1])
                      + ffn(wgb_ref, wub_ref, wdb_ref, xe[:, D + 1:D + 2]))

    @pl.when(n == 0)
    def _():
        y_ref[...] = jnp.zeros_like(y_ref)


def _experts(tile_ea, tile_eb, tile_rows, xs, w_gate, w_up, w_down):
    wspec = lambda shape, sel: pl.BlockSpec((1,) + shape, lambda j, ea, eb, nr: (sel(ea, eb)[j], 0, 0))
    first = lambda ea, eb: ea
    second = lambda ea, eb: eb
    return pl.pallas_call(
        _expert_kernel,
        grid_spec=pltpu.PrefetchScalarGridSpec(
            num_scalar_prefetch=3,
            grid=(N_TILES_E,),
            in_specs=[pl.BlockSpec((TM_E, XW), lambda j, *_: (j, 0)),
                      wspec((D, D_FF), first), wspec((D, D_FF), first), wspec((D_FF, D), first),
                      wspec((D, D_FF), second), wspec((D, D_FF), second), wspec((D_FF, D), second)],
            out_specs=pl.BlockSpec((TM_E, D), lambda j, *_: (j, 0)),
        ),
        out_shape=jax.ShapeDtypeStruct((S_ROWS, D), F32),
        compiler_params=_cparams(),
        name="moe_experts",
    )(tile_ea, tile_eb, tile_rows, xs, w_gate, w_up, w_down, w_gate, w_up, w_down)


def _combine_kernel(off_ref, bucket_ref, rank_ref, x1_ref, mod_ref, gf_ref, y_ref,
                    *rest, final):
    out_refs, (buf_ref, sem) = rest[:-2], rest[-2:]

    def make_copy(r):
        slot = off_ref[bucket_ref[0, 0, r]] + rank_ref[0, 0, r]
        return pltpu.make_async_copy(y_ref.at[pl.ds(slot, 1)], buf_ref.at[pl.ds(r, 1)], sem)

    _row_copies(TM, make_copy)
    x2 = x1_ref[...] + mod_ref[0, 5] * buf_ref[...]
    if not final:
        out_refs[0][...] = x2
        return
    y = _rmsnorm(x2, gf_ref[...])
    i = pl.program_id(0)
    yp_ref, ys_ref = out_refs

    @pl.when(i < PROMPT_TILES)
    def _():
        yp_ref[...] = y

    @pl.when(i >= PROMPT_TILES)
    def _():
        ys_ref[...] = y


def _combine(off, bucket, rank, x1, mod, final_g, y_sorted, *, final):
    tile = pl.BlockSpec((TM, D), lambda i, *_: (i, 0))
    if final:
        out_specs = [pl.BlockSpec((TM, D), lambda i, *_: (jnp.minimum(i, PROMPT_TILES - 1), 0)),
                     pl.BlockSpec((TM, D), lambda i, *_: (jnp.maximum(i - PROMPT_TILES, 0), 0))]
        out_shape = [jax.ShapeDtypeStruct((T_PROMPT, D), F32), jax.ShapeDtypeStruct((T_SAMPLE, D), F32)]
    else:
        out_specs = [tile]
        out_shape = [jax.ShapeDtypeStruct((T, D), F32)]
    return pl.pallas_call(
        functools.partial(_combine_kernel, final=final),
        grid_spec=pltpu.PrefetchScalarGridSpec(
            num_scalar_prefetch=1,
            grid=(N_TILES,),
            in_specs=[_SMEM_TILE_SPEC, _SMEM_TILE_SPEC, tile,
                      pl.BlockSpec((1, 6, 1, D), lambda i, *_: (_cond_of_tile(i), 0, 0, 0)),
                      pl.BlockSpec((1, D), lambda i, *_: (0, 0)),
                      pl.BlockSpec(memory_space=pl.ANY)],
            out_specs=out_specs,
            scratch_shapes=[pltpu.VMEM((TM, D), F32), pltpu.SemaphoreType.DMA(())],
        ),
        out_shape=out_shape,
        compiler_params=_cparams(),
        name="moe_combine_final" if final else "moe_combine",
    )(off, bucket, rank, x1, mod, final_g, y_sorted)


def _bucket_plan(info, counts):
    bucket = info[:, 0].astype(jnp.int32).reshape(N_TILES, 1, TM)
    rank = info[:, 1].astype(jnp.int32).reshape(N_TILES, 1, TM)
    cnt = counts[0, :N_BUCKETS].astype(jnp.int32)
    tiles = (cnt + TM_E - 1) // TM_E
    tile_end = jnp.cumsum(tiles)
    tile_start = tile_end - tiles
    off = tile_start * TM_E
    j = jnp.arange(N_TILES_E, dtype=jnp.int32)
    n_live = tile_end[-1]
    j_live = jnp.minimum(j, n_live - 1)
    b = jnp.sum((tile_end[None, :] <= j_live[:, None]).astype(jnp.int32), axis=1)
    b = jnp.minimum(b, N_BUCKETS - 1)
    rows = jnp.clip(cnt[b] - (j - tile_start[b]) * TM_E, 0, TM_E)
    rows = jnp.where(j < n_live, rows, 0).astype(jnp.int32)
    grp = b // N_PAIRS
    pair = b % N_PAIRS
    ea = grp * EPG + jnp.asarray(_PAIR_A, jnp.int32)[pair]
    eb = grp * EPG + jnp.asarray(_PAIR_B, jnp.int32)[pair]
    return off.astype(jnp.int32), bucket, rank, ea, eb, rows


def _moe(a, x, mod, norm_g2, w_out, router_w, router_b, w_gate, w_up, w_down, final_g, *, final):
    x1, info, counts = _post_mixer(a, x, mod, norm_g2, w_out, router_w, router_b)
    off, bucket, rank, ea, eb, rows = _bucket_plan(info, counts)
    partial = (rows < TM_E).astype(jnp.int32)
    xs = _dispatch(off, partial, bucket, rank, x1, mod, norm_g2, info)
    ys = _experts(ea, eb, rows, xs, w_gate, w_up, w_down)
    return _combine(off, bucket, rank, x1, mod, final_g, ys, final=final)


def kernel(x_prompt, x_sample, state_hgrn, c, c_ctx, w_mod, b_mod, norm_mix_g, norm_ffn_g, norm_final_g,
           hgrn_w_in, hgrn_lb_logits, hgrn_norm_g, hgrn_w_out, conv_w_in, conv_w, conv_w_out,
           router_w, router_b, moe_w_gate, moe_w_up, moe_w_down):
    x = jnp.concatenate([x_prompt.reshape(T_PROMPT, D), x_sample.reshape(T_SAMPLE, D)], axis=0)
    cond = jnp.concatenate([c_ctx[None], c, jnp.zeros((COND_ROWS - N_COND, D), F32)], axis=0)
    mods = _adaln(cond, w_mod, b_mod)[:, :N_COND].reshape(2, N_COND, 6, 1, D)
    rb = router_b.reshape(1, N_EXPERTS)
    final_g = norm_final_g.reshape(1, D)
    bf = lambda w: w.astype(BF16)

    q, v, ff, fb, gg = _hgrn_proj(x, mods[0], norm_mix_g[0:1], bf(hgrn_w_in[0]))
    s0 = state_hgrn[:, 0]
    o_f, s_f = _gla_fwd(q, v, ff, hgrn_lb_logits, s0)
    a, s_b = _gla_bwd(q, v, fb, hgrn_lb_logits, s0, o_f, gg, hgrn_norm_g[0:1])
    (x,) = _moe(a, x, mods[0], norm_ffn_g[0:1], bf(hgrn_w_out[0]), router_w, rb,
                bf(moe_w_gate[0]), bf(moe_w_up[0]), bf(moe_w_down[0]), final_g, final=False)

    a = _conv_mixer(x, mods[1], norm_mix_g[1:2], bf(conv_w_in[0]), conv_w[0])
    y_p, y_s = _moe(a, x, mods[1], norm_ffn_g[1:2], bf(conv_w_out[0]), router_w, rb,
                    bf(moe_w_gate[1]), bf(moe_w_up[1]), bf(moe_w_down[1]), final_g, final=True)

    new_state = jnp.stack([s_f, s_b], axis=1)[:, None]
    return (y_p.reshape(N_PROMPT_SEQ, PROMPT_LEN, D), y_s.reshape(N_SAMPLE_SEQ, SAMPLE_LEN, D), new_state)
```

```python
import functools

import jax
import jax.numpy as jnp
from jax import lax
from jax.experimental import pallas as pl
from jax.experimental.pallas import tpu as pltpu

F32 = jnp.float32
BF16 = jnp.bfloat16

D = 1024
N_PROMPT_SEQ = 16
PROMPT_LEN = 256
N_SAMPLE_SEQ = 2
SAMPLE_LEN = 4096
GRID_W = 64
T_PROMPT = N_PROMPT_SEQ * PROMPT_LEN
T_SAMPLE = N_SAMPLE_SEQ * SAMPLE_LEN
T = T_PROMPT + T_SAMPLE
N_COND = 1 + N_SAMPLE_SEQ
COND_ROWS = 8

HEADS = 8
DK = 128
DV = 128
CHUNK = 64
N_EXPERTS = 16
N_GROUPS = 4
EPG = 4
N_PAIRS = 6
N_BUCKETS = N_GROUPS * N_PAIRS
D_FF = 512
EPS = 1e-6

TM = 256
N_TILES = T // TM
PROMPT_TILES = T_PROMPT // TM
TILES_PER_SAMPLE = SAMPLE_LEN // TM
CHUNKS_PER_TILE = TM // CHUNK
TM_E = 256
N_TILES_E = T // TM_E + N_BUCKETS
S_ROWS = N_TILES_E * TM_E
XW = D + 128
LANES = 128
MOD_TN = 1536

VMEM_LIMIT = 56 * 1024 * 1024

_PAIR_A = (0, 0, 0, 1, 1, 2)
_PAIR_B = (1, 2, 3, 2, 3, 3)


def _cparams():
    return pltpu.CompilerParams(dimension_semantics=("arbitrary",), vmem_limit_bytes=VMEM_LIMIT)


def _cond_of_tile(t):
    return jnp.where(t < PROMPT_TILES, 0, 1 + (t - PROMPT_TILES) // TILES_PER_SAMPLE)


def _sample_of_tile(t):
    return jnp.clip((t - PROMPT_TILES) // TILES_PER_SAMPLE, 0, N_SAMPLE_SEQ - 1)


def _rmsnorm(x, g):
    ms = jnp.mean(x * x, axis=-1, keepdims=True)
    return (x * lax.rsqrt(ms + EPS)) * g


def _modulate(x, g, shift, scale):
    return _rmsnorm(x, g) * (1.0 + scale) + shift


def _silu(x):
    return x * jax.nn.sigmoid(x)


def _dot(a, b):
    return jnp.dot(a, b, preferred_element_type=F32)


def _dot_nt(a, b):
    return lax.dot_general(a, b, (((1,), (1,)), ((), ())), preferred_element_type=F32)


def _dot_tn(a, b):
    return lax.dot_general(a, b, (((0,), (0,)), ((), ())), preferred_element_type=F32)


def _split_bf16(x):
    hi = x.astype(BF16)
    lo = (x - hi.astype(F32)).astype(BF16)
    return hi, lo


def _mod_kernel(cond_ref, w_ref, b_ref, o_ref):
    s = _silu(cond_ref[...])
    o_ref[0] = _dot(s.astype(BF16), w_ref[0].astype(BF16)) + b_ref[0]


def _adaln(cond, w_mod, b_mod):
    depth = w_mod.shape[0]
    return pl.pallas_call(
        _mod_kernel,
        grid=(depth, 6 * D // MOD_TN),
        in_specs=[
            pl.BlockSpec((COND_ROWS, D), lambda i, j: (0, 0)),
            pl.BlockSpec((1, D, MOD_TN), lambda i, j: (i, 0, j)),
            pl.BlockSpec((1, 1, MOD_TN), lambda i, j: (i, 0, j)),
        ],
        out_specs=pl.BlockSpec((1, COND_ROWS, MOD_TN), lambda i, j: (i, 0, j)),
        out_shape=jax.ShapeDtypeStruct((depth, COND_ROWS, 6 * D), F32),
        compiler_params=pltpu.CompilerParams(
            dimension_semantics=("arbitrary", "arbitrary"), vmem_limit_bytes=VMEM_LIMIT),
        name="adaln",
    )(cond, w_mod, b_mod.reshape(depth, 1, 6 * D))


_MOD_SPEC = pl.BlockSpec((1, 6, 1, D), lambda i: (_cond_of_tile(i), 0, 0, 0))
_ROW_SPEC = pl.BlockSpec((1, D), lambda i: (0, 0))
_TILE_SPEC = pl.BlockSpec((TM, D), lambda i: (i, 0))


def _hgrn_proj_kernel(x_ref, mod_ref, g_ref, w_ref, q_ref, v_ref, ff_ref, fb_ref, gg_ref):
    h = _modulate(x_ref[...], g_ref[...], mod_ref[0, 0], mod_ref[0, 1]).astype(BF16)
    q_ref[...] = (_dot(h, w_ref[:, 0 * D:1 * D]) * (DK ** -0.5)).astype(BF16)
    v_ref[...] = _dot(h, w_ref[:, 1 * D:2 * D]).astype(BF16)
    ff_ref[...] = _dot(h, w_ref[:, 2 * D:3 * D])
    fb_ref[...] = _dot(h, w_ref[:, 3 * D:4 * D])
    gg_ref[...] = _dot(h, w_ref[:, 4 * D:5 * D]).astype(BF16)


def _hgrn_proj(x, mod, norm_g, w_in):
    bf = jax.ShapeDtypeStruct((T, D), BF16)
    f32 = jax.ShapeDtypeStruct((T, D), F32)
    return pl.pallas_call(
        _hgrn_proj_kernel,
        grid=(N_TILES,),
        in_specs=[_TILE_SPEC, _MOD_SPEC, _ROW_SPEC, pl.BlockSpec((D, 5 * D), lambda i: (0, 0))],
        out_specs=[_TILE_SPEC] * 5,
        out_shape=[bf, bf, f32, f32, bf],
        compiler_params=_cparams(),
        name="hgrn_proj",
    )(x, mod, norm_g, w_in)


def _gla_tile(t, q_ref, v_ref, f_ref, lbl_ref, s0_ref, st_ref, o_ref, *, reverse):
    is_prompt = t < PROMPT_TILES
    rel = t - PROMPT_TILES
    edge = (TILES_PER_SAMPLE - 1) if reverse else 0
    sample_start = jnp.logical_and(rel >= 0, rel % TILES_PER_SAMPLE == edge)

    @pl.when(is_prompt)
    def _():
        st_ref[...] = jnp.zeros_like(st_ref)

    @pl.when(sample_start)
    def _():
        for h in range(HEADS):
            st_ref[h] = s0_ref[0, 0, h].T

    logits = lbl_ref[...]
    e = jnp.exp(logits - jnp.max(logits, axis=0, keepdims=True))
    lb = e[0:1] / jnp.sum(e, axis=0, keepdims=True)

    row = lax.broadcasted_iota(jnp.int32, (CHUNK, CHUNK), 0)
    col = lax.broadcasted_iota(jnp.int32, (CHUNK, CHUNK), 1)
    keep = (row <= col) if reverse else (row >= col)
    tri = keep.astype(BF16)
    ref_idx = (CHUNK - 1 - CHUNK // 2) if reverse else CHUNK // 2
    last_idx = 0 if reverse else CHUNK - 1

    for cc in range(CHUNKS_PER_TILE):
        c = (CHUNKS_PER_TILE - 1 - cc) if reverse else cc
        rows = slice(c * CHUNK, (c + 1) * CHUNK)
        f = lb + (1.0 - lb) * jax.nn.sigmoid(f_ref[rows, :])
        k = 1.0 - f
        lf_hi, lf_lo = _split_bf16(jnp.log(f))
        cum = _dot(tri, lf_hi) + _dot(tri, lf_lo)
        ref = cum[ref_idx:ref_idx + 1]
        last = cum[last_idx:last_idx + 1]
        a = jnp.exp(cum - ref)
        b = jnp.exp(ref - cum)
        qa32 = q_ref[rows, :].astype(F32) * a
        kb32 = k * b
        qa = qa32.astype(BF16)
        kb = kb32.astype(BF16)
        qe = (qa32 * jnp.exp(ref)).astype(BF16)
        kd = (kb32 * jnp.exp(last - ref)).astype(BF16)
        decay = jnp.exp(last)
        v = v_ref[rows, :]
        for h in range(HEADS):
            hs = slice(h * DK, (h + 1) * DK)
            scores = jnp.where(keep, _dot_nt(qa[:, hs], kb[:, hs]), 0.0).astype(BF16)
            st = st_ref[h]
            o = _dot(scores, v[:, hs]) + _dot_nt(qe[:, hs], st.astype(BF16))
            o_ref[rows, hs] = o
            st_ref[h] = st * decay[:, hs] + _dot_tn(v[:, hs], kd[:, hs])


def _store_prompt_state(t, st_ref, snew_ref):
    @pl.when(t < PROMPT_TILES)
    def _():
        for h in range(HEADS):
            snew_ref[0, h] = st_ref[h].T


def _gla_fwd_kernel(q_ref, v_ref, f_ref, lbl_ref, s0_ref, o_ref, snew_ref, st_ref):
    t = pl.program_id(0)
    _gla_tile(t, q_ref, v_ref, f_ref, lbl_ref, s0_ref, st_ref, o_ref, reverse=False)
    _store_prompt_state(t, st_ref, snew_ref)


def _gla_bwd_kernel(q_ref, v_ref, f_ref, lbl_ref, s0_ref, of_ref, gg_ref, ng_ref,
                    a_ref, snew_ref, st_ref, ob_ref):
    t = N_TILES - 1 - pl.program_id(0)
    _gla_tile(t, q_ref, v_ref, f_ref, lbl_ref, s0_ref, st_ref, ob_ref, reverse=True)
    _store_prompt_state(t, st_ref, snew_ref)
    ng = ng_ref[...]
    for h in range(HEADS):
        hs = slice(h * DV, (h + 1) * DV)
        o = of_ref[:, hs] + ob_ref[:, hs]
        a_ref[:, hs] = (_rmsnorm(o, ng) * _silu(gg_ref[:, hs].astype(F32))).astype(BF16)


def _state_specs(direction, tile_of_step):
    s0 = pl.BlockSpec((1, 1, HEADS, DK, DV),
                      lambda i: (_sample_of_tile(tile_of_step(i)), direction, 0, 0, 0))
    snew = pl.BlockSpec((1, HEADS, DK, DV),
                        lambda i: (jnp.minimum(tile_of_step(i), PROMPT_TILES - 1), 0, 0, 0))
    return s0, snew


def _gla_fwd(q, v, ff, lb_logits, s0):
    s0_spec, snew_spec = _state_specs(0, lambda i: i)
    return pl.pallas_call(
        _gla_fwd_kernel,
        grid=(N_TILES,),
        in_specs=[_TILE_SPEC, _TILE_SPEC, _TILE_SPEC,
                  pl.BlockSpec(lb_logits.shape, lambda i: (0, 0)), s0_spec],
        out_specs=[_TILE_SPEC, snew_spec],
        out_shape=[jax.ShapeDtypeStruct((T, D), F32),
                   jax.ShapeDtypeStruct((N_PROMPT_SEQ, HEADS, DK, DV), F32)],
        scratch_shapes=[pltpu.VMEM((HEADS, DV, DK), F32)],
        compiler_params=_cparams(),
        name="gla_fwd",
    )(q, v, ff, lb_logits, s0)


def _gla_bwd(q, v, fb, lb_logits, s0, o_f, gg, head_norm_g):
    rev = lambda i: N_TILES - 1 - i
    tile = pl.BlockSpec((TM, D), lambda i: (rev(i), 0))
    s0_spec, snew_spec = _state_specs(1, rev)
    return pl.pallas_call(
        _gla_bwd_kernel,
        grid=(N_TILES,),
        in_specs=[tile, tile, tile, pl.BlockSpec(lb_logits.shape, lambda i: (0, 0)), s0_spec,
                  tile, tile, pl.BlockSpec((1, DV), lambda i: (0, 0))],
        out_specs=[tile, snew_spec],
        out_shape=[jax.ShapeDtypeStruct((T, D), BF16),
                   jax.ShapeDtypeStruct((N_PROMPT_SEQ, HEADS, DK, DV), F32)],
        scratch_shapes=[pltpu.VMEM((HEADS, DV, DK), F32), pltpu.VMEM((TM, D), F32)],
        compiler_params=_cparams(),
        name="gla_bwd",
    )(q, v, fb, lb_logits, s0, o_f, gg, head_norm_g)


def _conv_mixer_kernel(x_ref, mod_ref, g_ref, w_ref, cw_ref, a_ref):
    i = pl.program_id(0)
    h = _modulate(x_ref[...], g_ref[...], mod_ref[0, 0], mod_ref[0, 1]).astype(BF16)
    bg = _dot(h, w_ref[:, 0 * D:1 * D])
    u = _dot(h, w_ref[:, 1 * D:2 * D]) * _dot(h, w_ref[:, 2 * D:3 * D])
    seg = jnp.where(i < PROMPT_TILES, PROMPT_LEN, GRID_W)
    pos = lax.broadcasted_iota(jnp.int32, (TM, 1), 0) & (seg - 1)
    prev = jnp.where(pos == 0, 0.0, pltpu.roll(u, 1, axis=0))
    nxt = jnp.where(pos == seg - 1, 0.0, pltpu.roll(u, TM - 1, axis=0))
    conv = cw_ref[0:1] * prev + cw_ref[1:2] * u + cw_ref[2:3] * nxt
    a_ref[...] = (bg * conv).astype(BF16)


def _conv_mixer(x, mod, norm_g, w_in, conv_w):
    return pl.pallas_call(
        _conv_mixer_kernel,
        grid=(N_TILES,),
        in_specs=[_TILE_SPEC, _MOD_SPEC, _ROW_SPEC, pl.BlockSpec((D, 3 * D), lambda i: (0, 0)),
                  pl.BlockSpec((3, D), lambda i: (0, 0))],
        out_specs=_TILE_SPEC,
        out_shape=jax.ShapeDtypeStruct((T, D), BF16),
        compiler_params=_cparams(),
        name="conv_mixer",
    )(x, mod, norm_g, w_in, conv_w)


def _first_index_of(vals, target, lane):
    return jnp.min(jnp.where(vals == target, lane, float(N_EXPERTS)), axis=-1, keepdims=True)


def _route(h2, rw_ref, rb_ref):
    h_hi, h_lo = _split_bf16(h2)
    w_hi, w_lo = _split_bf16(rw_ref[...])
    logits = _dot(h_hi, w_hi) + (_dot(h_hi, w_lo) + _dot(h_lo, w_hi))
    scores = jax.nn.sigmoid(logits)
    sel = scores + rb_ref[...]
    lane = lax.broadcasted_iota(jnp.int32, (TM, N_EXPERTS), 1).astype(F32)
    neg = -jnp.inf
    best = grp = first = second = None
    for g in range(N_GROUPS):
        in_g = jnp.logical_and(lane >= g * EPG, lane < (g + 1) * EPG)
        vals = jnp.where(in_g, sel, neg)
        m1 = jnp.max(vals, axis=-1, keepdims=True)
        i1 = _first_index_of(vals, m1, lane)
        rest = jnp.where(lane == i1, neg, vals)
        m2 = jnp.max(rest, axis=-1, keepdims=True)
        i2 = _first_index_of(rest, m2, lane)
        gs = m1 + m2
        if g == 0:
            best, grp, first, second = gs, jnp.zeros_like(i1), i1, i2
        else:
            upd = gs > best
            best = jnp.where(upd, gs, best)
            grp = jnp.where(upd, float(g), grp)
            first = jnp.where(upd, i1, first)
            second = jnp.where(upd, i2, second)
    e_lo = jnp.minimum(first, second)
    e_hi = jnp.maximum(first, second)
    s_lo = jnp.sum(jnp.where(lane == e_lo, scores, 0.0), axis=-1, keepdims=True)
    s_hi = jnp.sum(jnp.where(lane == e_hi, scores, 0.0), axis=-1, keepdims=True)
    tot = s_lo + s_hi
    a = e_lo - grp * EPG
    b = e_hi - grp * EPG
    pair = jnp.where(a == 0.0, b - 1.0, jnp.where(a == 1.0, b + 1.0, 5.0))
    return grp * N_PAIRS + pair, s_lo / tot, s_hi / tot


def _post_mixer_kernel(a_ref, x_ref, mod_ref, g2_ref, wo_ref, rw_ref, rb_ref,
                       x1_ref, info_ref, cnt_ref, carry_ref):
    i = pl.program_id(0)

    @pl.when(i == 0)
    def _():
        carry_ref[...] = jnp.zeros_like(carry_ref)

    x1 = x_ref[...] + mod_ref[0, 2] * _dot(a_ref[...], wo_ref[...])
    x1_ref[...] = x1
    h2 = _modulate(x1, g2_ref[...], mod_ref[0, 3], mod_ref[0, 4])
    bucket, g_lo, g_hi = _route(h2, rw_ref, rb_ref)

    lane = lax.broadcasted_iota(jnp.int32, (TM, LANES), 1)
    onehot = lane.astype(F32) == bucket
    r = lax.broadcasted_iota(jnp.int32, (TM, TM), 0)
    c = lax.broadcasted_iota(jnp.int32, (TM, TM), 1)
    before = _dot((r > c).astype(BF16), onehot.astype(BF16))
    oh = onehot.astype(F32)
    rank = jnp.sum(oh * (before + carry_ref[...]), axis=-1, keepdims=True)
    carry_ref[...] = carry_ref[...] + jnp.sum(oh, axis=0, keepdims=True)
    cnt_ref[...] = carry_ref[...]
    info = jnp.where(lane == 0, bucket,
                     jnp.where(lane == 1, rank,
                               jnp.where(lane == 2, g_lo, jnp.where(lane == 3, g_hi, 0.0))))
    info_ref[...] = info


def _post_mixer(a, x, mod, norm_g2, w_out, router_w, router_b):
    return pl.pallas_call(
        _post_mixer_kernel,
        grid=(N_TILES,),
        in_specs=[_TILE_SPEC, _TILE_SPEC, _MOD_SPEC, _ROW_SPEC,
                  pl.BlockSpec((D, D), lambda i: (0, 0)),
                  pl.BlockSpec((D, N_EXPERTS), lambda i: (0, 0)),
                  pl.BlockSpec((1, N_EXPERTS), lambda i: (0, 0))],
        out_specs=[_TILE_SPEC, pl.BlockSpec((TM, LANES), lambda i: (i, 0)),
                   pl.BlockSpec((1, LANES), lambda i: (0, 0))],
        out_shape=[jax.ShapeDtypeStruct((T, D), F32), jax.ShapeDtypeStruct((T, LANES), F32),
                   jax.ShapeDtypeStruct((1, LANES), F32)],
        scratch_shapes=[pltpu.VMEM((1, LANES), F32)],
        compiler_params=_cparams(),
        name="post_mixer",
    )(a, x, mod, norm_g2, w_out, router_w, router_b)


_SMEM_TILE_SPEC = pl.BlockSpec((1, 1, TM), lambda i, *_: (i, 0, 0), memory_space=pltpu.SMEM)


def _row_copies(n, make_copy, wait_all):
    for r in range(n):
        make_copy(r).start()
    wait_all.wait()


def _dispatch_kernel(partial_ref, slot_ref, x1_ref, mod_ref, g2_ref, info_ref,
                     xs_ref, buf_ref, zero_ref, sem, zero_sem):
    @pl.when(pl.program_id(0) == 0)
    def _():
        zero_ref[...] = jnp.zeros_like(zero_ref)

        def zero_copy(j):
            return pltpu.make_async_copy(zero_ref, xs_ref.at[pl.ds(j * TM_E, TM_E)], zero_sem)

        def start(j, carry):
            @pl.when(partial_ref[j] != 0)
            def _():
                zero_copy(j).start()
            return carry

        def wait(j, carry):
            @pl.when(partial_ref[j] != 0)
            def _():
                zero_copy(j).wait()
            return carry

        lax.fori_loop(0, N_TILES_E, start, 0)
        lax.fori_loop(0, N_TILES_E, wait, 0)

    h2 = _modulate(x1_ref[...], g2_ref[...], mod_ref[0, 3], mod_ref[0, 4])
    buf_ref[:, :D] = h2
    lane = lax.broadcasted_iota(jnp.int32, (TM, LANES), 1)
    info = info_ref[...]
    buf_ref[:, D:] = jnp.where(lane == 0, info[:, 2:3], jnp.where(lane == 1, info[:, 3:4], 0.0))

    def make_copy(r):
        return pltpu.make_async_copy(buf_ref.at[pl.ds(r, 1)], xs_ref.at[pl.ds(slot_ref[0, 0, r], 1)], sem)

    _row_copies(TM, make_copy, pltpu.make_async_copy(buf_ref, xs_ref.at[pl.ds(0, TM)], sem))


def _dispatch(partial, slot, x1, mod, norm_g2, info):
    tile = lambda w: pl.BlockSpec((TM, w), lambda i, *_: (i, 0))
    return pl.pallas_call(
        _dispatch_kernel,
        grid_spec=pltpu.PrefetchScalarGridSpec(
            num_scalar_prefetch=1,
            grid=(N_TILES,),
            in_specs=[_SMEM_TILE_SPEC, tile(D),
                      pl.BlockSpec((1, 6, 1, D), lambda i, *_: (_cond_of_tile(i), 0, 0, 0)),
                      pl.BlockSpec((1, D), lambda i, *_: (0, 0)), tile(LANES)],
            out_specs=pl.BlockSpec(memory_space=pl.ANY),
            scratch_shapes=[pltpu.VMEM((TM, XW), F32), pltpu.VMEM((TM_E, XW), F32),
                            pltpu.SemaphoreType.DMA(()), pltpu.SemaphoreType.DMA(())],
        ),
        out_shape=jax.ShapeDtypeStruct((S_ROWS, XW), F32),
        compiler_params=_cparams(),
        name="moe_dispatch",
    )(partial, slot, x1, mod, norm_g2, info)


def _expert_kernel(ea_ref, eb_ref, nrows_ref, x_ref, wga_ref, wua_ref, wda_ref,
                   wgb_ref, wub_ref, wdb_ref, y_ref):
    j = pl.program_id(0)
    n = nrows_ref[j]

    @pl.when(n > 0)
    def _():
        valid = lax.broadcasted_iota(jnp.int32, (TM_E, 1), 0) < n
        xe = jnp.where(valid, x_ref[...], 0.0)
        x = xe[:, :D].astype(BF16)

        def ffn(wg_ref, wu_ref, wd_ref, gate):
            act = _silu(_dot(x, wg_ref[0])) * _dot(x, wu_ref[0]) * gate
            return _dot(act.astype(BF16), wd_ref[0])

        y_ref[...] = (ffn(wga_ref, wua_ref, wda_ref, xe[:, D:D + 1])
                      + ffn(wgb_ref, wub_ref, wdb_ref, xe[:, D + 1:D + 2]))

    @pl.when(n == 0)
    def _():
        y_ref[...] = jnp.zeros_like(y_ref)


def _experts(tile_ea, tile_eb, tile_rows, xs, w_gate, w_up, w_down):
    wspec = lambda shape, sel: pl.BlockSpec((1,) + shape, lambda j, ea, eb, nr: (sel(ea, eb)[j], 0, 0))
    first = lambda ea, eb: ea
    second = lambda ea, eb: eb
    return pl.pallas_call(
        _expert_kernel,
        grid_spec=pltpu.PrefetchScalarGridSpec(
            num_scalar_prefetch=3,
            grid=(N_TILES_E,),
            in_specs=[pl.BlockSpec((TM_E, XW), lambda j, *_: (j, 0)),
                      wspec((D, D_FF), first), wspec((D, D_FF), first), wspec((D_FF, D), first),
                      wspec((D, D_FF), second), wspec((D, D_FF), second), wspec((D_FF, D), second)],
            out_specs=pl.BlockSpec((TM_E, D), lambda j, *_: (j, 0)),
        ),
        out_shape=jax.ShapeDtypeStruct((S_ROWS, D), F32),
        compiler_params=_cparams(),
        name="moe_experts",
    )(tile_ea, tile_eb, tile_rows, xs, w_gate, w_up, w_down, w_gate, w_up, w_down)


def _combine_kernel(slot_ref, x1_ref, mod_ref, gf_ref, y_ref, *rest, final):
    out_refs, (buf_ref, sem) = rest[:-2], rest[-2:]

    def make_copy(r):
        return pltpu.make_async_copy(y_ref.at[pl.ds(slot_ref[0, 0, r], 1)], buf_ref.at[pl.ds(r, 1)], sem)

    _row_copies(TM, make_copy, pltpu.make_async_copy(y_ref.at[pl.ds(0, TM)], buf_ref, sem))
    x2 = x1_ref[...] + mod_ref[0, 5] * buf_ref[...]
    if not final:
        out_refs[0][...] = x2
        return
    y = _rmsnorm(x2, gf_ref[...])
    i = pl.program_id(0)
    yp_ref, ys_ref = out_refs

    @pl.when(i < PROMPT_TILES)
    def _():
        yp_ref[...] = y

    @pl.when(i >= PROMPT_TILES)
    def _():
        ys_ref[...] = y


def _combine(slot, x1, mod, final_g, y_sorted, *, final):
    tile = _TILE_SPEC
    if final:
        out_specs = [pl.BlockSpec((TM, D), lambda i: (jnp.minimum(i, PROMPT_TILES - 1), 0)),
                     pl.BlockSpec((TM, D), lambda i: (jnp.maximum(i - PROMPT_TILES, 0), 0))]
        out_shape = [jax.ShapeDtypeStruct((T_PROMPT, D), F32), jax.ShapeDtypeStruct((T_SAMPLE, D), F32)]
    else:
        out_specs = [tile]
        out_shape = [jax.ShapeDtypeStruct((T, D), F32)]
    return pl.pallas_call(
        functools.partial(_combine_kernel, final=final),
        grid=(N_TILES,),
        in_specs=[_SMEM_TILE_SPEC, tile, _MOD_SPEC, _ROW_SPEC, pl.BlockSpec(memory_space=pl.ANY)],
        out_specs=out_specs,
        scratch_shapes=[pltpu.VMEM((TM, D), F32), pltpu.SemaphoreType.DMA(())],
        out_shape=out_shape,
        compiler_params=_cparams(),
        name="moe_combine_final" if final else "moe_combine",
    )(slot, x1, mod, final_g, y_sorted)


def _bucket_plan(info, counts):
    bucket = info[:, 0].astype(jnp.int32)
    rank = info[:, 1].astype(jnp.int32)
    cnt = counts[0, :N_BUCKETS].astype(jnp.int32)
    tiles = (cnt + TM_E - 1) // TM_E
    tile_end = jnp.cumsum(tiles)
    tile_start = tile_end - tiles
    off = tile_start * TM_E
    j = jnp.arange(N_TILES_E, dtype=jnp.int32)
    n_live = tile_end[-1]
    j_live = jnp.minimum(j, n_live - 1)
    b = jnp.sum((tile_end[None, :] <= j_live[:, None]).astype(jnp.int32), axis=1)
    b = jnp.minimum(b, N_BUCKETS - 1)
    rows = jnp.clip(cnt[b] - (j - tile_start[b]) * TM_E, 0, TM_E)
    rows = jnp.where(j < n_live, rows, 0).astype(jnp.int32)
    grp = b // N_PAIRS
    pair = b % N_PAIRS
    ea = grp * EPG + jnp.asarray(_PAIR_A, jnp.int32)[pair]
    eb = grp * EPG + jnp.asarray(_PAIR_B, jnp.int32)[pair]
    slot = (off[bucket] + rank).astype(jnp.int32).reshape(N_TILES, 1, TM)
    return slot, ea, eb, rows


def _moe(a, x, mod, norm_g2, w_out, router_w, router_b, w_gate, w_up, w_down, final_g, *, final):
    x1, info, counts = _post_mixer(a, x, mod, norm_g2, w_out, router_w, router_b)
    slot, ea, eb, rows = _bucket_plan(info, counts)
    partial = (rows < TM_E).astype(jnp.int32)
    xs = _dispatch(partial, slot, x1, mod, norm_g2, info)
    ys = _experts(ea, eb, rows, xs, w_gate, w_up, w_down)
    return _combine(slot, x1, mod, final_g, ys, final=final)


def kernel(x_prompt, x_sample, state_hgrn, c, c_ctx, w_mod, b_mod, norm_mix_g, norm_ffn_g, norm_final_g,
           hgrn_w_in, hgrn_lb_logits, hgrn_norm_g, hgrn_w_out, conv_w_in, conv_w, conv_w_out,
           router_w, router_b, moe_w_gate, moe_w_up, moe_w_down):
    x = jnp.concatenate([x_prompt.reshape(T_PROMPT, D), x_sample.reshape(T_SAMPLE, D)], axis=0)
    cond = jnp.concatenate([c_ctx[None], c, jnp.zeros((COND_ROWS - N_COND, D), F32)], axis=0)
    mods = _adaln(cond, w_mod, b_mod)[:, :N_COND].reshape(2, N_COND, 6, 1, D)
    rb = router_b.reshape(1, N_EXPERTS)
    final_g = norm_final_g.reshape(1, D)
    bf = lambda w: w.astype(BF16)

    q, v, ff, fb, gg = _hgrn_proj(x, mods[0], norm_mix_g[0:1], bf(hgrn_w_in[0]))
    s0 = state_hgrn[:, 0]
    o_f, s_f = _gla_fwd(q, v, ff, hgrn_lb_logits, s0)
    a, s_b = _gla_bwd(q, v, fb, hgrn_lb_logits, s0, o_f, gg, hgrn_norm_g[0:1])
    (x,) = _moe(a, x, mods[0], norm_ffn_g[0:1], bf(hgrn_w_out[0]), router_w, rb,
                bf(moe_w_gate[0]), bf(moe_w_up[0]), bf(moe_w_down[0]), final_g, final=False)

    a = _conv_mixer(x, mods[1], norm_mix_g[1:2], bf(conv_w_in[0]), conv_w[0])
    y_p, y_s = _moe(a, x, mods[1], norm_ffn_g[1:2], bf(conv_w_out[0]), router_w, rb,
                    bf(moe_w_gate[1]), bf(moe_w_up[1]), bf(moe_w_down[1]), final_g, final=True)

    new_state = jnp.stack([s_f, s_b], axis=1)[:, None]
    return (y_p.reshape(N_PROMPT_SEQ, PROMPT_LEN, D), y_s.reshape(N_SAMPLE_SEQ, SAMPLE_LEN, D), new_state)
```

```python
import functools

import jax
import jax.numpy as jnp
from jax import lax
from jax.experimental import pallas as pl
from jax.experimental.pallas import tpu as pltpu

F32 = jnp.float32
BF16 = jnp.bfloat16

D = 1024
N_PROMPT_SEQ = 16
PROMPT_LEN = 256
N_SAMPLE_SEQ = 2
SAMPLE_LEN = 4096
GRID_W = 64
T_PROMPT = N_PROMPT_SEQ * PROMPT_LEN
T_SAMPLE = N_SAMPLE_SEQ * SAMPLE_LEN
T = T_PROMPT + T_SAMPLE
N_COND = 1 + N_SAMPLE_SEQ
COND_ROWS = 8

HEADS = 8
DK = 128
DV = 128
CHUNK = 64
N_EXPERTS = 16
N_GROUPS = 4
EPG = 4
N_PAIRS = 6
N_BUCKETS = N_GROUPS * N_PAIRS
D_FF = 512
EPS = 1e-6

TM = 256
N_TILES = T // TM
PROMPT_TILES = T_PROMPT // TM
TILES_PER_SAMPLE = SAMPLE_LEN // TM
CHUNKS_PER_TILE = TM // CHUNK
TM_E = 256
N_TILES_E = T // TM_E + N_BUCKETS
S_ROWS = N_TILES_E * TM_E
XW = D + 128
LANES = 128
MOD_TN = 1536

VMEM_LIMIT = 56 * 1024 * 1024

_PAIR_A = (0, 0, 0, 1, 1, 2)
_PAIR_B = (1, 2, 3, 2, 3, 3)


def _cparams():
    return pltpu.CompilerParams(dimension_semantics=("arbitrary",), vmem_limit_bytes=VMEM_LIMIT)


def _cond_of_tile(t):
    return jnp.where(t < PROMPT_TILES, 0, 1 + (t - PROMPT_TILES) // TILES_PER_SAMPLE)


def _sample_of_tile(t):
    return jnp.clip((t - PROMPT_TILES) // TILES_PER_SAMPLE, 0, N_SAMPLE_SEQ - 1)


def _rmsnorm(x, g):
    ms = jnp.mean(x * x, axis=-1, keepdims=True)
    return (x * lax.rsqrt(ms + EPS)) * g


def _modulate(x, g, shift, scale):
    return _rmsnorm(x, g) * (1.0 + scale) + shift


def _silu(x):
    return x * jax.nn.sigmoid(x)


def _dot(a, b):
    return jnp.dot(a, b, preferred_element_type=F32)


def _dot_nt(a, b):
    return lax.dot_general(a, b, (((1,), (1,)), ((), ())), preferred_element_type=F32)


def _dot_tn(a, b):
    return lax.dot_general(a, b, (((0,), (0,)), ((), ())), preferred_element_type=F32)


def _split_bf16(x):
    hi = x.astype(BF16)
    lo = (x - hi.astype(F32)).astype(BF16)
    return hi, lo


def _mod_kernel(cond_ref, w_ref, b_ref, o_ref):
    s = _silu(cond_ref[...])
    o_ref[0] = _dot(s.astype(BF16), w_ref[0].astype(BF16)) + b_ref[0]


def _adaln(cond, w_mod, b_mod):
    depth = w_mod.shape[0]
    return pl.pallas_call(
        _mod_kernel,
        grid=(depth, 6 * D // MOD_TN),
        in_specs=[
            pl.BlockSpec((COND_ROWS, D), lambda i, j: (0, 0)),
            pl.BlockSpec((1, D, MOD_TN), lambda i, j: (i, 0, j)),
            pl.BlockSpec((1, 1, MOD_TN), lambda i, j: (i, 0, j)),
        ],
        out_specs=pl.BlockSpec((1, COND_ROWS, MOD_TN), lambda i, j: (i, 0, j)),
        out_shape=jax.ShapeDtypeStruct((depth, COND_ROWS, 6 * D), F32),
        compiler_params=pltpu.CompilerParams(
            dimension_semantics=("arbitrary", "arbitrary"), vmem_limit_bytes=VMEM_LIMIT),
        name="adaln",
    )(cond, w_mod, b_mod.reshape(depth, 1, 6 * D))


_MOD_SPEC = pl.BlockSpec((1, 6, 1, D), lambda i: (_cond_of_tile(i), 0, 0, 0))
_ROW_SPEC = pl.BlockSpec((1, D), lambda i: (0, 0))
_TILE_SPEC = pl.BlockSpec((TM, D), lambda i: (i, 0))


def _hgrn_proj_kernel(x_ref, mod_ref, g_ref, w_ref, q_ref, v_ref, ff_ref, fb_ref, gg_ref):
    h = _modulate(x_ref[...], g_ref[...], mod_ref[0, 0], mod_ref[0, 1]).astype(BF16)
    q_ref[...] = (_dot(h, w_ref[:, 0 * D:1 * D]) * (DK ** -0.5)).astype(BF16)
    v_ref[...] = _dot(h, w_ref[:, 1 * D:2 * D]).astype(BF16)
    ff_ref[...] = _dot(h, w_ref[:, 2 * D:3 * D])
    fb_ref[...] = _dot(h, w_ref[:, 3 * D:4 * D])
    gg_ref[...] = _dot(h, w_ref[:, 4 * D:5 * D]).astype(BF16)


def _hgrn_proj(x, mod, norm_g, w_in):
    bf = jax.ShapeDtypeStruct((T, D), BF16)
    f32 = jax.ShapeDtypeStruct((T, D), F32)
    return pl.pallas_call(
        _hgrn_proj_kernel,
        grid=(N_TILES,),
        in_specs=[_TILE_SPEC, _MOD_SPEC, _ROW_SPEC, pl.BlockSpec((D, 5 * D), lambda i: (0, 0))],
        out_specs=[_TILE_SPEC] * 5,
        out_shape=[bf, bf, f32, f32, bf],
        compiler_params=_cparams(),
        name="hgrn_proj",
    )(x, mod, norm_g, w_in)


def _gla_tile(t, q_ref, v_ref, f_ref, lbl_ref, s0_ref, st_ref, ma_ref, o_ref, *, reverse):
    is_prompt = t < PROMPT_TILES
    rel = t - PROMPT_TILES
    edge = (TILES_PER_SAMPLE - 1) if reverse else 0
    sample_start = jnp.logical_and(rel >= 0, rel % TILES_PER_SAMPLE == edge)

    @pl.when(is_prompt)
    def _():
        st_ref[...] = jnp.zeros_like(st_ref)

    @pl.when(sample_start)
    def _():
        for h in range(HEADS):
            st_ref[h] = s0_ref[0, 0, h].T

    logits = lbl_ref[...]
    e = jnp.exp(logits - jnp.max(logits, axis=0, keepdims=True))
    lb = e[0:1] / jnp.sum(e, axis=0, keepdims=True)

    ref_idx = (CHUNK - 1 - CHUNK // 2) if reverse else CHUNK // 2
    last_idx = 0 if reverse else CHUNK - 1
    shift = CHUNK.bit_length() - 1

    row = lax.broadcasted_iota(jnp.int32, (TM, TM), 0)
    col = lax.broadcasted_iota(jnp.int32, (TM, TM), 1)
    same = (row >> shift) == (col >> shift)
    ref_row = ((row >> shift) << shift) + ref_idx
    keep = jnp.logical_and(same, (row <= col) if reverse else (row >= col))

    @pl.when(pl.program_id(0) == 0)
    def _():
        at_ref = jnp.logical_and(same, (ref_row <= col) if reverse else (ref_row >= col))
        ma_ref[...] = (keep.astype(F32) - at_ref.astype(F32)).astype(BF16)

    ma = ma_ref[...]

    rr = lax.broadcasted_iota(jnp.int32, (2 * CHUNKS_PER_TILE, TM), 0)
    cc = lax.broadcasted_iota(jnp.int32, (2 * CHUNKS_PER_TILE, TM), 1)
    chunk = rr & (CHUNKS_PER_TILE - 1)
    target = chunk * CHUNK + jnp.where(rr < CHUNKS_PER_TILE, ref_idx, last_idx)
    covered = (cc >= target) if reverse else (cc <= target)
    rm = jnp.logical_and((cc >> shift) == chunk, covered).astype(BF16)

    chunk_rows = [slice(c * CHUNK, (c + 1) * CHUNK) for c in range(CHUNKS_PER_TILE)]
    order = range(CHUNKS_PER_TILE - 1, -1, -1) if reverse else range(CHUNKS_PER_TILE)
    pair_w = 2 * DK

    def per_chunk_scale(x, scale):
        return jnp.concatenate([x[rs] * scale[c:c + 1] for c, rs in enumerate(chunk_rows)], axis=0)

    def pair_cols(p):
        return slice(p * pair_w, (p + 1) * pair_w)

    def stage_decay(p):
        lbp = lb[:, pair_cols(p)]
        f = lbp + (1.0 - lbp) * jax.nn.sigmoid(f_ref[:, pair_cols(p)])
        lf_hi, lf_lo = _split_bf16(jnp.log(f))
        z = _dot(ma, lf_hi) + _dot(ma, lf_lo)
        marks = _dot(rm, lf_hi) + _dot(rm, lf_lo)
        return 1.0 - f, z, marks

    def stage_scores(p, k, z, marks):
        ref, last = marks[:CHUNKS_PER_TILE], marks[CHUNKS_PER_TILE:]
        qa32 = q_ref[:, pair_cols(p)].astype(F32) * jnp.exp(z)
        kb32 = k * jnp.exp(-z)
        qa = qa32.astype(BF16)
        kb = kb32.astype(BF16)
        qe = per_chunk_scale(qa32, jnp.exp(ref)).astype(BF16)
        kd = per_chunk_scale(kb32, jnp.exp(last - ref)).astype(BF16)
        v = v_ref[:, pair_cols(p)]
        scores, incr = [], []
        for hh in range(2):
            ls = slice(hh * DK, (hh + 1) * DK)
            scores.append(_dot_nt(qa[:, ls], kb[:, ls]))
            kd_h = kd[:, ls]
            kd_blocks = jnp.concatenate(
                [jnp.concatenate([part for part in (jnp.zeros((c * CHUNK, DK), BF16), kd_h[rs],
                                                    jnp.zeros((TM - (c + 1) * CHUNK, DK), BF16))
                                  if part.shape[0]], axis=0)
                 for c, rs in enumerate(chunk_rows)], axis=1)
            incr.append(_dot_tn(v[:, ls], kd_blocks))
        return scores, incr, qe, jnp.exp(last), v

    def stage_output(p, scores, incr, qe, decay, v):
        for hh in range(2):
            h = 2 * p + hh
            ls = slice(hh * DK, (hh + 1) * DK)
            hs = slice(h * DK, (h + 1) * DK)
            o_intra = _dot(jnp.where(keep, scores[hh], 0.0).astype(BF16), v[:, ls])
            st = st_ref[h]
            for c in order:
                rs = chunk_rows[c]
                o_ref[rs, hs] = o_intra[rs] + _dot_nt(qe[rs, ls], st.astype(BF16))
                st = st * decay[c:c + 1, ls] + incr[hh][:, c * DK:(c + 1) * DK]
            st_ref[h] = st

    n_pairs = HEADS // 2
    decayed, scored = {}, {}
    for step in range(n_pairs + 2):
        if step < n_pairs:
            decayed[step] = stage_decay(step)
        if 0 <= step - 1 < n_pairs:
            scored[step - 1] = stage_scores(step - 1, *decayed.pop(step - 1))
        if 0 <= step - 2 < n_pairs:
            stage_output(step - 2, *scored.pop(step - 2))


def _store_prompt_state(t, st_ref, snew_ref):
    @pl.when(t < PROMPT_TILES)
    def _():
        for h in range(HEADS):
            snew_ref[0, h] = st_ref[h].T


def _gla_fwd_kernel(q_ref, v_ref, f_ref, lbl_ref, s0_ref, o_ref, snew_ref, st_ref, ma_ref):
    t = pl.program_id(0)
    _gla_tile(t, q_ref, v_ref, f_ref, lbl_ref, s0_ref, st_ref, ma_ref, o_ref, reverse=False)
    _store_prompt_state(t, st_ref, snew_ref)


def _gla_bwd_kernel(q_ref, v_ref, f_ref, lbl_ref, s0_ref, of_ref, gg_ref, ng_ref,
                    a_ref, snew_ref, st_ref, ma_ref, ob_ref):
    t = N_TILES - 1 - pl.program_id(0)
    _gla_tile(t, q_ref, v_ref, f_ref, lbl_ref, s0_ref, st_ref, ma_ref, ob_ref, reverse=True)
    _store_prompt_state(t, st_ref, snew_ref)
    ng = ng_ref[...]
    for h in range(HEADS):
        hs = slice(h * DV, (h + 1) * DV)
        o = of_ref[:, hs] + ob_ref[:, hs]
        a_ref[:, hs] = (_rmsnorm(o, ng) * _silu(gg_ref[:, hs].astype(F32))).astype(BF16)


def _state_specs(direction, tile_of_step):
    s0 = pl.BlockSpec((1, 1, HEADS, DK, DV),
                      lambda i: (_sample_of_tile(tile_of_step(i)), direction, 0, 0, 0))
    snew = pl.BlockSpec((1, HEADS, DK, DV),
                        lambda i: (jnp.minimum(tile_of_step(i), PROMPT_TILES - 1), 0, 0, 0))
    return s0, snew


def _gla_fwd(q, v, ff, lb_logits, s0):
    s0_spec, snew_spec = _state_specs(0, lambda i: i)
    return pl.pallas_call(
        _gla_fwd_kernel,
        grid=(N_TILES,),
        in_specs=[_TILE_SPEC, _TILE_SPEC, _TILE_SPEC,
                  pl.BlockSpec(lb_logits.shape, lambda i: (0, 0)), s0_spec],
        out_specs=[_TILE_SPEC, snew_spec],
        out_shape=[jax.ShapeDtypeStruct((T, D), F32),
                   jax.ShapeDtypeStruct((N_PROMPT_SEQ, HEADS, DK, DV), F32)],
        scratch_shapes=[pltpu.VMEM((HEADS, DV, DK), F32), pltpu.VMEM((TM, TM), BF16)],
        compiler_params=_cparams(),
        name="gla_fwd",
    )(q, v, ff, lb_logits, s0)


def _gla_bwd(q, v, fb, lb_logits, s0, o_f, gg, head_norm_g):
    rev = lambda i: N_TILES - 1 - i
    tile = pl.BlockSpec((TM, D), lambda i: (rev(i), 0))
    s0_spec, snew_spec = _state_specs(1, rev)
    return pl.pallas_call(
        _gla_bwd_kernel,
        grid=(N_TILES,),
        in_specs=[tile, tile, tile, pl.BlockSpec(lb_logits.shape, lambda i: (0, 0)), s0_spec,
                  tile, tile, pl.BlockSpec((1, DV), lambda i: (0, 0))],
        out_specs=[tile, snew_spec],
        out_shape=[jax.ShapeDtypeStruct((T, D), BF16),
                   jax.ShapeDtypeStruct((N_PROMPT_SEQ, HEADS, DK, DV), F32)],
        scratch_shapes=[pltpu.VMEM((HEADS, DV, DK), F32), pltpu.VMEM((TM, TM), BF16),
                        pltpu.VMEM((TM, D), F32)],
        compiler_params=_cparams(),
        name="gla_bwd",
    )(q, v, fb, lb_logits, s0, o_f, gg, head_norm_g)


def _conv_mixer_kernel(x_ref, mod_ref, g_ref, w_ref, cw_ref, a_ref):
    i = pl.program_id(0)
    h = _modulate(x_ref[...], g_ref[...], mod_ref[0, 0], mod_ref[0, 1]).astype(BF16)
    bg = _dot(h, w_ref[:, 0 * D:1 * D])
    u = _dot(h, w_ref[:, 1 * D:2 * D]) * _dot(h, w_ref[:, 2 * D:3 * D])
    seg = jnp.where(i < PROMPT_TILES, PROMPT_LEN, GRID_W)
    pos = lax.broadcasted_iota(jnp.int32, (TM, 1), 0) & (seg - 1)
    prev = jnp.where(pos == 0, 0.0, pltpu.roll(u, 1, axis=0))
    nxt = jnp.where(pos == seg - 1, 0.0, pltpu.roll(u, TM - 1, axis=0))
    conv = cw_ref[0:1] * prev + cw_ref[1:2] * u + cw_ref[2:3] * nxt
    a_ref[...] = (bg * conv).astype(BF16)


def _conv_mixer(x, mod, norm_g, w_in, conv_w):
    return pl.pallas_call(
        _conv_mixer_kernel,
        grid=(N_TILES,),
        in_specs=[_TILE_SPEC, _MOD_SPEC, _ROW_SPEC, pl.BlockSpec((D, 3 * D), lambda i: (0, 0)),
                  pl.BlockSpec((3, D), lambda i: (0, 0))],
        out_specs=_TILE_SPEC,
        out_shape=jax.ShapeDtypeStruct((T, D), BF16),
        compiler_params=_cparams(),
        name="conv_mixer",
    )(x, mod, norm_g, w_in, conv_w)


def _first_index_of(vals, target, lane):
    return jnp.min(jnp.where(vals == target, lane, float(N_EXPERTS)), axis=-1, keepdims=True)


def _route(h2, rw_ref, rb_ref):
    h_hi, h_lo = _split_bf16(h2)
    w_hi, w_lo = _split_bf16(rw_ref[...])
    logits = _dot(h_hi, w_hi) + (_dot(h_hi, w_lo) + _dot(h_lo, w_hi))
    scores = jax.nn.sigmoid(logits)
    sel = scores + rb_ref[...]
    lane = lax.broadcasted_iota(jnp.int32, (TM, N_EXPERTS), 1).astype(F32)
    neg = -jnp.inf
    best = grp = first = second = None
    for g in range(N_GROUPS):
        in_g = jnp.logical_and(lane >= g * EPG, lane < (g + 1) * EPG)
        vals = jnp.where(in_g, sel, neg)
        m1 = jnp.max(vals, axis=-1, keepdims=True)
        i1 = _first_index_of(vals, m1, lane)
        rest = jnp.where(lane == i1, neg, vals)
        m2 = jnp.max(rest, axis=-1, keepdims=True)
        i2 = _first_index_of(rest, m2, lane)
        gs = m1 + m2
        if g == 0:
            best, grp, first, second = gs, jnp.zeros_like(i1), i1, i2
        else:
            upd = gs > best
            best = jnp.where(upd, gs, best)
            grp = jnp.where(upd, float(g), grp)
            first = jnp.where(upd, i1, first)
            second = jnp.where(upd, i2, second)
    e_lo = jnp.minimum(first, second)
    e_hi = jnp.maximum(first, second)
    s_lo = jnp.sum(jnp.where(lane == e_lo, scores, 0.0), axis=-1, keepdims=True)
    s_hi = jnp.sum(jnp.where(lane == e_hi, scores, 0.0), axis=-1, keepdims=True)
    tot = s_lo + s_hi
    a = e_lo - grp * EPG
    b = e_hi - grp * EPG
    pair = jnp.where(a == 0.0, b - 1.0, jnp.where(a == 1.0, b + 1.0, 5.0))
    return grp * N_PAIRS + pair, s_lo / tot, s_hi / tot


def _post_mixer_kernel(a_ref, x_ref, mod_ref, g2_ref, wo_ref, rw_ref, rb_ref,
                       x1_ref, info_ref, cnt_ref, carry_ref):
    i = pl.program_id(0)

    @pl.when(i == 0)
    def _():
        carry_ref[...] = jnp.zeros_like(carry_ref)

    x1 = x_ref[...] + mod_ref[0, 2] * _dot(a_ref[...], wo_ref[...])
    x1_ref[...] = x1
    h2 = _modulate(x1, g2_ref[...], mod_ref[0, 3], mod_ref[0, 4])
    bucket, g_lo, g_hi = _route(h2, rw_ref, rb_ref)

    lane = lax.broadcasted_iota(jnp.int32, (TM, LANES), 1)
    onehot = lane.astype(F32) == bucket
    r = lax.broadcasted_iota(jnp.int32, (TM, TM), 0)
    c = lax.broadcasted_iota(jnp.int32, (TM, TM), 1)
    before = _dot((r > c).astype(BF16), onehot.astype(BF16))
    oh = onehot.astype(F32)
    rank = jnp.sum(oh * (before + carry_ref[...]), axis=-1, keepdims=True)
    carry_ref[...] = carry_ref[...] + jnp.sum(oh, axis=0, keepdims=True)
    cnt_ref[...] = carry_ref[...]
    info = jnp.where(lane == 0, bucket,
                     jnp.where(lane == 1, rank,
                               jnp.where(lane == 2, g_lo, jnp.where(lane == 3, g_hi, 0.0))))
    info_ref[...] = info


def _post_mixer(a, x, mod, norm_g2, w_out, router_w, router_b):
    return pl.pallas_call(
        _post_mixer_kernel,
        grid=(N_TILES,),
        in_specs=[_TILE_SPEC, _TILE_SPEC, _MOD_SPEC, _ROW_SPEC,
                  pl.BlockSpec((D, D), lambda i: (0, 0)),
                  pl.BlockSpec((D, N_EXPERTS), lambda i: (0, 0)),
                  pl.BlockSpec((1, N_EXPERTS), lambda i: (0, 0))],
        out_specs=[_TILE_SPEC, pl.BlockSpec((TM, LANES), lambda i: (i, 0)),
                   pl.BlockSpec((1, LANES), lambda i: (0, 0))],
        out_shape=[jax.ShapeDtypeStruct((T, D), F32), jax.ShapeDtypeStruct((T, LANES), F32),
                   jax.ShapeDtypeStruct((1, LANES), F32)],
        scratch_shapes=[pltpu.VMEM((1, LANES), F32)],
        compiler_params=_cparams(),
        name="post_mixer",
    )(a, x, mod, norm_g2, w_out, router_w, router_b)


_SMEM_TILE_SPEC = pl.BlockSpec((1, 1, TM), lambda i, *_: (i, 0, 0), memory_space=pltpu.SMEM)


def _row_copies(n, make_copy, wait_all):
    for r in range(n):
        make_copy(r).start()
    wait_all.wait()


def _dispatch_kernel(partial_ref, slot_ref, x1_ref, mod_ref, g2_ref, info_ref,
                     xs_ref, buf_ref, zero_ref, sem, zero_sem):
    @pl.when(pl.program_id(0) == 0)
    def _():
        zero_ref[...] = jnp.zeros_like(zero_ref)

        def zero_copy(j):
            return pltpu.make_async_copy(zero_ref, xs_ref.at[pl.ds(j * TM_E, TM_E)], zero_sem)

        def start(j, carry):
            @pl.when(partial_ref[j] != 0)
            def _():
                zero_copy(j).start()
            return carry

        def wait(j, carry):
            @pl.when(partial_ref[j] != 0)
            def _():
                zero_copy(j).wait()
            return carry

        lax.fori_loop(0, N_TILES_E, start, 0)
        lax.fori_loop(0, N_TILES_E, wait, 0)

    h2 = _modulate(x1_ref[...], g2_ref[...], mod_ref[0, 3], mod_ref[0, 4])
    buf_ref[:, :D] = h2
    lane = lax.broadcasted_iota(jnp.int32, (TM, LANES), 1)
    info = info_ref[...]
    buf_ref[:, D:] = jnp.where(lane == 0, info[:, 2:3], jnp.where(lane == 1, info[:, 3:4], 0.0))

    def make_copy(r):
        return pltpu.make_async_copy(buf_ref.at[pl.ds(r, 1)], xs_ref.at[pl.ds(slot_ref[0, 0, r], 1)], sem)

    _row_copies(TM, make_copy, pltpu.make_async_copy(buf_ref, xs_ref.at[pl.ds(0, TM)], sem))


def _dispatch(partial, slot, x1, mod, norm_g2, info):
    tile = lambda w: pl.BlockSpec((TM, w), lambda i, *_: (i, 0))
    return pl.pallas_call(
        _dispatch_kernel,
        grid_spec=pltpu.PrefetchScalarGridSpec(
            num_scalar_prefetch=1,
            grid=(N_TILES,),
            in_specs=[_SMEM_TILE_SPEC, tile(D),
                      pl.BlockSpec((1, 6, 1, D), lambda i, *_: (_cond_of_tile(i), 0, 0, 0)),
                      pl.BlockSpec((1, D), lambda i, *_: (0, 0)), tile(LANES)],
            out_specs=pl.BlockSpec(memory_space=pl.ANY),
            scratch_shapes=[pltpu.VMEM((TM, XW), F32), pltpu.VMEM((TM_E, XW), F32),
                            pltpu.SemaphoreType.DMA(()), pltpu.SemaphoreType.DMA(())],
        ),
        out_shape=jax.ShapeDtypeStruct((S_ROWS, XW), F32),
        compiler_params=_cparams(),
        name="moe_dispatch",
    )(partial, slot, x1, mod, norm_g2, info)


def _expert_kernel(ea_ref, eb_ref, nrows_ref, x_ref, wga_ref, wua_ref, wda_ref,
                   wgb_ref, wub_ref, wdb_ref, y_ref):
    j = pl.program_id(0)
    n = nrows_ref[j]

    @pl.when(n > 0)
    def _():
        valid = lax.broadcasted_iota(jnp.int32, (TM_E, 1), 0) < n
        xe = jnp.where(valid, x_ref[...], 0.0)
        x = xe[:, :D].astype(BF16)

        def ffn(wg_ref, wu_ref, wd_ref, gate):
            act = _silu(_dot(x, wg_ref[0])) * _dot(x, wu_ref[0]) * gate
            return _dot(act.astype(BF16), wd_ref[0])

        y_ref[...] = (ffn(wga_ref, wua_ref, wda_ref, xe[:, D:D + 1])
                      + ffn(wgb_ref, wub_ref, wdb_ref, xe[:, D + 1:D + 2]))

    @pl.when(n == 0)
    def _():
        y_ref[...] = jnp.zeros_like(y_ref)


def _experts(tile_ea, tile_eb, tile_rows, xs, w_gate, w_up, w_down):
    wspec = lambda shape, sel: pl.BlockSpec((1,) + shape, lambda j, ea, eb, nr: (sel(ea, eb)[j], 0, 0))
    first = lambda ea, eb: ea
    second = lambda ea, eb: eb
    return pl.pallas_call(
        _expert_kernel,
        grid_spec=pltpu.PrefetchScalarGridSpec(
            num_scalar_prefetch=3,
            grid=(N_TILES_E,),
            in_specs=[pl.BlockSpec((TM_E, XW), lambda j, *_: (j, 0)),
                      wspec((D, D_FF), first), wspec((D, D_FF), first), wspec((D_FF, D), first),
                      wspec((D, D_FF), second), wspec((D, D_FF), second), wspec((D_FF, D), second)],
            out_specs=pl.BlockSpec((TM_E, D), lambda j, *_: (j, 0)),
        ),
        out_shape=jax.ShapeDtypeStruct((S_ROWS, D), F32),
        compiler_params=_cparams(),
        name="moe_experts",
    )(tile_ea, tile_eb, tile_rows, xs, w_gate, w_up, w_down, w_gate, w_up, w_down)


def _combine_kernel(slot_ref, x1_ref, mod_ref, gf_ref, y_ref, *rest, final):
    out_refs, (buf_ref, sem) = rest[:-2], rest[-2:]

    def make_copy(r):
        return pltpu.make_async_copy(y_ref.at[pl.ds(slot_ref[0, 0, r], 1)], buf_ref.at[pl.ds(r, 1)], sem)

    _row_copies(TM, make_copy, pltpu.make_async_copy(y_ref.at[pl.ds(0, TM)], buf_ref, sem))
    x2 = x1_ref[...] + mod_ref[0, 5] * buf_ref[...]
    if not final:
        out_refs[0][...] = x2
        return
    y = _rmsnorm(x2, gf_ref[...])
    i = pl.program_id(0)
    yp_ref, ys_ref = out_refs

    @pl.when(i < PROMPT_TILES)
    def _():
        yp_ref[...] = y

    @pl.when(i >= PROMPT_TILES)
    def _():
        ys_ref[...] = y


def _combine(slot, x1, mod, final_g, y_sorted, *, final):
    tile = _TILE_SPEC
    if final:
        out_specs = [pl.BlockSpec((TM, D), lambda i: (jnp.minimum(i, PROMPT_TILES - 1), 0)),
                     pl.BlockSpec((TM, D), lambda i: (jnp.maximum(i - PROMPT_TILES, 0), 0))]
        out_shape = [jax.ShapeDtypeStruct((T_PROMPT, D), F32), jax.ShapeDtypeStruct((T_SAMPLE, D), F32)]
    else:
        out_specs = [tile]
        out_shape = [jax.ShapeDtypeStruct((T, D), F32)]
    return pl.pallas_call(
        functools.partial(_combine_kernel, final=final),
        grid=(N_TILES,),
        in_specs=[_SMEM_TILE_SPEC, tile, _MOD_SPEC, _ROW_SPEC, pl.BlockSpec(memory_space=pl.ANY)],
        out_specs=out_specs,
        scratch_shapes=[pltpu.VMEM((TM, D), F32), pltpu.SemaphoreType.DMA(())],
        out_shape=out_shape,
        compiler_params=_cparams(),
        name="moe_combine_final" if final else "moe_combine",
    )(slot, x1, mod, final_g, y_sorted)


def _bucket_plan(info, counts):
    bucket = info[:, 0].astype(jnp.int32)
    rank = info[:, 1].astype(jnp.int32)
    cnt = counts[0, :N_BUCKETS].astype(jnp.int32)
    tiles = (cnt + TM_E - 1) // TM_E
    tile_end = jnp.cumsum(tiles)
    tile_start = tile_end - tiles
    off = tile_start * TM_E
    j = jnp.arange(N_TILES_E, dtype=jnp.int32)
    n_live = tile_end[-1]
    j_live = jnp.minimum(j, n_live - 1)
    b = jnp.sum((tile_end[None, :] <= j_live[:, None]).astype(jnp.int32), axis=1)
    b = jnp.minimum(b, N_BUCKETS - 1)
    rows = jnp.clip(cnt[b] - (j - tile_start[b]) * TM_E, 0, TM_E)
    rows = jnp.where(j < n_live, rows, 0).astype(jnp.int32)
    grp = b // N_PAIRS
    pair = b % N_PAIRS
    ea = grp * EPG + jnp.asarray(_PAIR_A, jnp.int32)[pair]
    eb = grp * EPG + jnp.asarray(_PAIR_B, jnp.int32)[pair]
    slot = (off[bucket] + rank).astype(jnp.int32).reshape(N_TILES, 1, TM)
    return slot, ea, eb, rows


def _moe(a, x, mod, norm_g2, w_out, router_w, router_b, w_gate, w_up, w_down, final_g, *, final):
    x1, info, counts = _post_mixer(a, x, mod, norm_g2, w_out, router_w, router_b)
    slot, ea, eb, rows = _bucket_plan(info, counts)
    partial = (rows < TM_E).astype(jnp.int32)
    xs = _dispatch(partial, slot, x1, mod, norm_g2, info)
    ys = _experts(ea, eb, rows, xs, w_gate, w_up, w_down)
    return _combine(slot, x1, mod, final_g, ys, final=final)


def kernel(x_prompt, x_sample, state_hgrn, c, c_ctx, w_mod, b_mod, norm_mix_g, norm_ffn_g, norm_final_g,
           hgrn_w_in, hgrn_lb_logits, hgrn_norm_g, hgrn_w_out, conv_w_in, conv_w, conv_w_out,
           router_w, router_b, moe_w_gate, moe_w_up, moe_w_down):
    x = jnp.concatenate([x_prompt.reshape(T_PROMPT, D), x_sample.reshape(T_SAMPLE, D)], axis=0)
    cond = jnp.concatenate([c_ctx[None], c, jnp.zeros((COND_ROWS - N_COND, D), F32)], axis=0)
    mods = _adaln(cond, w_mod, b_mod)[:, :N_COND].reshape(2, N_COND, 6, 1, D)
    rb = router_b.reshape(1, N_EXPERTS)
    final_g = norm_final_g.reshape(1, D)
    bf = lambda w: w.astype(BF16)

    q, v, ff, fb, gg = _hgrn_proj(x, mods[0], norm_mix_g[0:1], bf(hgrn_w_in[0]))
    s0 = state_hgrn[:, 0]
    o_f, s_f = _gla_fwd(q, v, ff, hgrn_lb_logits, s0)
    a, s_b = _gla_bwd(q, v, fb, hgrn_lb_logits, s0, o_f, gg, hgrn_norm_g[0:1])
    (x,) = _moe(a, x, mods[0], norm_ffn_g[0:1], bf(hgrn_w_out[0]), router_w, rb,
                bf(moe_w_gate[0]), bf(moe_w_up[0]), bf(moe_w_down[0]), final_g, final=False)

    a = _conv_mixer(x, mods[1], norm_mix_g[1:2], bf(conv_w_in[0]), conv_w[0])
    y_p, y_s = _moe(a, x, mods[1], norm_ffn_g[1:2], bf(conv_w_out[0]), router_w, rb,
                    bf(moe_w_gate[1]), bf(moe_w_up[1]), bf(moe_w_down[1]), final_g, final=True)

    new_state = jnp.stack([s_f, s_b], axis=1)[:, None]
    return (y_p.reshape(N_PROMPT_SEQ, PROMPT_LEN, D), y_s.reshape(N_SAMPLE_SEQ, SAMPLE_LEN, D), new_state)
```

```python
import functools

import jax
import jax.numpy as jnp
from jax import lax
from jax.experimental import pallas as pl
from jax.experimental.pallas import tpu as pltpu

F32 = jnp.float32
BF16 = jnp.bfloat16

D = 1024
N_PROMPT_SEQ = 16
PROMPT_LEN = 256
N_SAMPLE_SEQ = 2
SAMPLE_LEN = 4096
GRID_W = 64
T_PROMPT = N_PROMPT_SEQ * PROMPT_LEN
T_SAMPLE = N_SAMPLE_SEQ * SAMPLE_LEN
T = T_PROMPT + T_SAMPLE
N_COND = 1 + N_SAMPLE_SEQ
COND_ROWS = 8

HEADS = 8
DK = 128
DV = 128
CHUNK = 64
N_EXPERTS = 16
N_GROUPS = 4
EPG = 4
N_PAIRS = 6
N_BUCKETS = N_GROUPS * N_PAIRS
D_FF = 512
EPS = 1e-6

TM = 256
N_TILES = T // TM
PROMPT_TILES = T_PROMPT // TM
TILES_PER_SAMPLE = SAMPLE_LEN // TM
CHUNKS_PER_TILE = TM // CHUNK
TM_E = 256
N_TILES_E = T // TM_E + N_BUCKETS
S_ROWS = N_TILES_E * TM_E
XW = D + 128
LANES = 128
MOD_TN = 1536

VMEM_LIMIT = 56 * 1024 * 1024

_PAIR_A = (0, 0, 0, 1, 1, 2)
_PAIR_B = (1, 2, 3, 2, 3, 3)


def _cparams():
    return pltpu.CompilerParams(dimension_semantics=("arbitrary",), vmem_limit_bytes=VMEM_LIMIT)


def _cond_of_tile(t):
    return jnp.where(t < PROMPT_TILES, 0, 1 + (t - PROMPT_TILES) // TILES_PER_SAMPLE)


def _sample_of_tile(t):
    return jnp.clip((t - PROMPT_TILES) // TILES_PER_SAMPLE, 0, N_SAMPLE_SEQ - 1)


def _rmsnorm(x, g):
    ms = jnp.mean(x * x, axis=-1, keepdims=True)
    return (x * lax.rsqrt(ms + EPS)) * g


def _modulate(x, g, shift, scale):
    return _rmsnorm(x, g) * (1.0 + scale) + shift


def _silu(x):
    return x * jax.nn.sigmoid(x)


def _dot(a, b):
    return jnp.dot(a, b, preferred_element_type=F32)


def _dot_nt(a, b):
    return lax.dot_general(a, b, (((1,), (1,)), ((), ())), preferred_element_type=F32)


def _dot_tn(a, b):
    return lax.dot_general(a, b, (((0,), (0,)), ((), ())), preferred_element_type=F32)


def _split_bf16(x):
    hi = x.astype(BF16)
    lo = (x - hi.astype(F32)).astype(BF16)
    return hi, lo


def _mod_kernel(cond_ref, w_ref, b_ref, o_ref):
    s = _silu(cond_ref[...])
    o_ref[0] = _dot(s.astype(BF16), w_ref[0].astype(BF16)) + b_ref[0]


def _adaln(cond, w_mod, b_mod):
    depth = w_mod.shape[0]
    return pl.pallas_call(
        _mod_kernel,
        grid=(depth, 6 * D // MOD_TN),
        in_specs=[
            pl.BlockSpec((COND_ROWS, D), lambda i, j: (0, 0)),
            pl.BlockSpec((1, D, MOD_TN), lambda i, j: (i, 0, j)),
            pl.BlockSpec((1, 1, MOD_TN), lambda i, j: (i, 0, j)),
        ],
        out_specs=pl.BlockSpec((1, COND_ROWS, MOD_TN), lambda i, j: (i, 0, j)),
        out_shape=jax.ShapeDtypeStruct((depth, COND_ROWS, 6 * D), F32),
        compiler_params=pltpu.CompilerParams(
            dimension_semantics=("arbitrary", "arbitrary"), vmem_limit_bytes=VMEM_LIMIT),
        name="adaln",
    )(cond, w_mod, b_mod.reshape(depth, 1, 6 * D))


_MOD_SPEC = pl.BlockSpec((1, 6, 1, D), lambda i: (_cond_of_tile(i), 0, 0, 0))
_ROW_SPEC = pl.BlockSpec((1, D), lambda i: (0, 0))
_TILE_SPEC = pl.BlockSpec((TM, D), lambda i: (i, 0))


_XP_SPEC = pl.BlockSpec((TM, D), lambda i: (jnp.minimum(i, PROMPT_TILES - 1), 0))
_XS_SPEC = pl.BlockSpec((TM, D), lambda i: (jnp.maximum(i - PROMPT_TILES, 0), 0))


def _token_tile(x_refs):
    if len(x_refs) == 1:
        return x_refs[0][...]
    return jnp.where(pl.program_id(0) < PROMPT_TILES, x_refs[0][...], x_refs[1][...])


def _token_specs(x_parts):
    return [_TILE_SPEC] if len(x_parts) == 1 else [_XP_SPEC, _XS_SPEC]


def _hgrn_proj_kernel(xp_ref, xs_ref, mod_ref, g_ref, w_ref, q_ref, v_ref, ff_ref, fb_ref, gg_ref):
    x = _token_tile((xp_ref, xs_ref))
    h = _modulate(x, g_ref[...], mod_ref[0, 0], mod_ref[0, 1]).astype(BF16)
    q_ref[...] = (_dot(h, w_ref[:, 0 * D:1 * D]) * (DK ** -0.5)).astype(BF16)
    v_ref[...] = _dot(h, w_ref[:, 1 * D:2 * D]).astype(BF16)
    ff_ref[...] = _dot(h, w_ref[:, 2 * D:3 * D])
    fb_ref[...] = _dot(h, w_ref[:, 3 * D:4 * D])
    gg_ref[...] = _dot(h, w_ref[:, 4 * D:5 * D]).astype(BF16)


def _hgrn_proj(x_parts, mod, norm_g, w_in):
    bf = jax.ShapeDtypeStruct((T, D), BF16)
    f32 = jax.ShapeDtypeStruct((T, D), F32)
    return pl.pallas_call(
        _hgrn_proj_kernel,
        grid=(N_TILES,),
        in_specs=_token_specs(x_parts) + [_MOD_SPEC, _ROW_SPEC, pl.BlockSpec((D, 5 * D), lambda i: (0, 0))],
        out_specs=[_TILE_SPEC] * 5,
        out_shape=[bf, bf, f32, f32, bf],
        compiler_params=_cparams(),
        name="hgrn_proj",
    )(*x_parts, mod, norm_g, w_in)


def _gla_tile(t, q_ref, v_ref, f_ref, lbl_ref, s0_ref, st_ref, ma_ref, o_ref, *, reverse):
    is_prompt = t < PROMPT_TILES
    rel = t - PROMPT_TILES
    edge = (TILES_PER_SAMPLE - 1) if reverse else 0
    sample_start = jnp.logical_and(rel >= 0, rel % TILES_PER_SAMPLE == edge)

    @pl.when(is_prompt)
    def _():
        st_ref[...] = jnp.zeros_like(st_ref)

    @pl.when(sample_start)
    def _():
        for h in range(HEADS):
            st_ref[h] = s0_ref[0, 0, h].T

    logits = lbl_ref[...]
    e = jnp.exp(logits - jnp.max(logits, axis=0, keepdims=True))
    lb = e[0:1] / jnp.sum(e, axis=0, keepdims=True)

    ref_idx = (CHUNK - 1 - CHUNK // 2) if reverse else CHUNK // 2
    last_idx = 0 if reverse else CHUNK - 1
    shift = CHUNK.bit_length() - 1

    row = lax.broadcasted_iota(jnp.int32, (TM, TM), 0)
    col = lax.broadcasted_iota(jnp.int32, (TM, TM), 1)
    same = (row >> shift) == (col >> shift)
    ref_row = ((row >> shift) << shift) + ref_idx
    keep = jnp.logical_and(same, (row <= col) if reverse else (row >= col))

    @pl.when(pl.program_id(0) == 0)
    def _():
        at_ref = jnp.logical_and(same, (ref_row <= col) if reverse else (ref_row >= col))
        ma_ref[...] = (keep.astype(F32) - at_ref.astype(F32)).astype(BF16)

    ma = ma_ref[...]

    rr = lax.broadcasted_iota(jnp.int32, (2 * CHUNKS_PER_TILE, TM), 0)
    cc = lax.broadcasted_iota(jnp.int32, (2 * CHUNKS_PER_TILE, TM), 1)
    chunk = rr & (CHUNKS_PER_TILE - 1)
    target = chunk * CHUNK + jnp.where(rr < CHUNKS_PER_TILE, ref_idx, last_idx)
    covered = (cc >= target) if reverse else (cc <= target)
    rm = jnp.logical_and((cc >> shift) == chunk, covered).astype(BF16)

    chunk_rows = [slice(c * CHUNK, (c + 1) * CHUNK) for c in range(CHUNKS_PER_TILE)]
    order = range(CHUNKS_PER_TILE - 1, -1, -1) if reverse else range(CHUNKS_PER_TILE)
    pair_w = 2 * DK

    def per_chunk_scale(x, scale):
        return jnp.concatenate([x[rs] * scale[c:c + 1] for c, rs in enumerate(chunk_rows)], axis=0)

    def pair_cols(p):
        return slice(p * pair_w, (p + 1) * pair_w)

    def stage_decay(p):
        lbp = lb[:, pair_cols(p)]
        f = lbp + (1.0 - lbp) * jax.nn.sigmoid(f_ref[:, pair_cols(p)])
        lf_hi, lf_lo = _split_bf16(jnp.log(f))
        z = _dot(ma, lf_hi) + _dot(ma, lf_lo)
        marks = _dot(rm, lf_hi) + _dot(rm, lf_lo)
        return 1.0 - f, z, marks

    def stage_scores(p, k, z, marks):
        ref, last = marks[:CHUNKS_PER_TILE], marks[CHUNKS_PER_TILE:]
        qa32 = q_ref[:, pair_cols(p)].astype(F32) * jnp.exp(z)
        kb32 = k * jnp.exp(-z)
        qa = qa32.astype(BF16)
        kb = kb32.astype(BF16)
        qe = per_chunk_scale(qa32, jnp.exp(ref)).astype(BF16)
        kd = per_chunk_scale(kb32, jnp.exp(last - ref)).astype(BF16)
        v = v_ref[:, pair_cols(p)]
        scores, incr = [], []
        for hh in range(2):
            ls = slice(hh * DK, (hh + 1) * DK)
            scores.append(_dot_nt(qa[:, ls], kb[:, ls]))
            kd_h = kd[:, ls]
            kd_blocks = jnp.concatenate(
                [jnp.concatenate([part for part in (jnp.zeros((c * CHUNK, DK), BF16), kd_h[rs],
                                                    jnp.zeros((TM - (c + 1) * CHUNK, DK), BF16))
                                  if part.shape[0]], axis=0)
                 for c, rs in enumerate(chunk_rows)], axis=1)
            incr.append(_dot_tn(v[:, ls], kd_blocks))
        return scores, incr, qe, jnp.exp(last), v

    def stage_output(p, scores, incr, qe, decay, v):
        for hh in range(2):
            h = 2 * p + hh
            ls = slice(hh * DK, (hh + 1) * DK)
            hs = slice(h * DK, (h + 1) * DK)
            o_intra = _dot(jnp.where(keep, scores[hh], 0.0).astype(BF16), v[:, ls])
            st = st_ref[h]
            for c in order:
                rs = chunk_rows[c]
                o_ref[rs, hs] = o_intra[rs] + _dot_nt(qe[rs, ls], st.astype(BF16))
                st = st * decay[c:c + 1, ls] + incr[hh][:, c * DK:(c + 1) * DK]
            st_ref[h] = st

    n_pairs = HEADS // 2
    decayed, scored = {}, {}
    for step in range(n_pairs + 2):
        if step < n_pairs:
            decayed[step] = stage_decay(step)
        if 0 <= step - 1 < n_pairs:
            scored[step - 1] = stage_scores(step - 1, *decayed.pop(step - 1))
        if 0 <= step - 2 < n_pairs:
            stage_output(step - 2, *scored.pop(step - 2))


def _store_prompt_state(t, st_ref, snew_ref):
    @pl.when(t < PROMPT_TILES)
    def _():
        for h in range(HEADS):
            snew_ref[0, h] = st_ref[h].T


def _gla_fwd_kernel(q_ref, v_ref, f_ref, lbl_ref, s0_ref, o_ref, snew_ref, st_ref, ma_ref):
    t = pl.program_id(0)
    _gla_tile(t, q_ref, v_ref, f_ref, lbl_ref, s0_ref, st_ref, ma_ref, o_ref, reverse=False)
    _store_prompt_state(t, st_ref, snew_ref)


def _gla_bwd_kernel(q_ref, v_ref, f_ref, lbl_ref, s0_ref, of_ref, gg_ref, ng_ref,
                    a_ref, snew_ref, st_ref, ma_ref, ob_ref):
    t = N_TILES - 1 - pl.program_id(0)
    _gla_tile(t, q_ref, v_ref, f_ref, lbl_ref, s0_ref, st_ref, ma_ref, ob_ref, reverse=True)
    _store_prompt_state(t, st_ref, snew_ref)
    ng = ng_ref[...]
    for h in range(HEADS):
        hs = slice(h * DV, (h + 1) * DV)
        o = of_ref[:, hs] + ob_ref[:, hs]
        a_ref[:, hs] = (_rmsnorm(o, ng) * _silu(gg_ref[:, hs].astype(F32))).astype(BF16)


def _state_specs(direction, tile_of_step):
    s0 = pl.BlockSpec((1, 1, HEADS, DK, DV),
                      lambda i: (_sample_of_tile(tile_of_step(i)), direction, 0, 0, 0))
    snew = pl.BlockSpec((1, HEADS, DK, DV),
                        lambda i: (jnp.minimum(tile_of_step(i), PROMPT_TILES - 1), 0, 0, 0))
    return s0, snew


def _gla_fwd(q, v, ff, lb_logits, s0):
    s0_spec, snew_spec = _state_specs(0, lambda i: i)
    return pl.pallas_call(
        _gla_fwd_kernel,
        grid=(N_TILES,),
        in_specs=[_TILE_SPEC, _TILE_SPEC, _TILE_SPEC,
                  pl.BlockSpec(lb_logits.shape, lambda i: (0, 0)), s0_spec],
        out_specs=[_TILE_SPEC, snew_spec],
        out_shape=[jax.ShapeDtypeStruct((T, D), F32),
                   jax.ShapeDtypeStruct((N_PROMPT_SEQ, HEADS, DK, DV), F32)],
        scratch_shapes=[pltpu.VMEM((HEADS, DV, DK), F32), pltpu.VMEM((TM, TM), BF16)],
        compiler_params=_cparams(),
        name="gla_fwd",
    )(q, v, ff, lb_logits, s0)


def _gla_bwd(q, v, fb, lb_logits, s0, o_f, gg, head_norm_g):
    rev = lambda i: N_TILES - 1 - i
    tile = pl.BlockSpec((TM, D), lambda i: (rev(i), 0))
    s0_spec, snew_spec = _state_specs(1, rev)
    return pl.pallas_call(
        _gla_bwd_kernel,
        grid=(N_TILES,),
        in_specs=[tile, tile, tile, pl.BlockSpec(lb_logits.shape, lambda i: (0, 0)), s0_spec,
                  tile, tile, pl.BlockSpec((1, DV), lambda i: (0, 0))],
        out_specs=[tile, snew_spec],
        out_shape=[jax.ShapeDtypeStruct((T, D), BF16),
                   jax.ShapeDtypeStruct((N_PROMPT_SEQ, HEADS, DK, DV), F32)],
        scratch_shapes=[pltpu.VMEM((HEADS, DV, DK), F32), pltpu.VMEM((TM, TM), BF16),
                        pltpu.VMEM((TM, D), F32)],
        compiler_params=_cparams(),
        name="gla_bwd",
    )(q, v, fb, lb_logits, s0, o_f, gg, head_norm_g)


def _conv_mixer_kernel(x_ref, mod_ref, g_ref, w_ref, cw_ref, a_ref):
    i = pl.program_id(0)
    h = _modulate(x_ref[...], g_ref[...], mod_ref[0, 0], mod_ref[0, 1]).astype(BF16)
    bg = _dot(h, w_ref[:, 0 * D:1 * D])
    u = _dot(h, w_ref[:, 1 * D:2 * D]) * _dot(h, w_ref[:, 2 * D:3 * D])
    seg = jnp.where(i < PROMPT_TILES, PROMPT_LEN, GRID_W)
    pos = lax.broadcasted_iota(jnp.int32, (TM, 1), 0) & (seg - 1)
    prev = jnp.where(pos == 0, 0.0, pltpu.roll(u, 1, axis=0))
    nxt = jnp.where(pos == seg - 1, 0.0, pltpu.roll(u, TM - 1, axis=0))
    conv = cw_ref[0:1] * prev + cw_ref[1:2] * u + cw_ref[2:3] * nxt
    a_ref[...] = (bg * conv).astype(BF16)


def _conv_mixer(x, mod, norm_g, w_in, conv_w):
    return pl.pallas_call(
        _conv_mixer_kernel,
        grid=(N_TILES,),
        in_specs=[_TILE_SPEC, _MOD_SPEC, _ROW_SPEC, pl.BlockSpec((D, 3 * D), lambda i: (0, 0)),
                  pl.BlockSpec((3, D), lambda i: (0, 0))],
        out_specs=_TILE_SPEC,
        out_shape=jax.ShapeDtypeStruct((T, D), BF16),
        compiler_params=_cparams(),
        name="conv_mixer",
    )(x, mod, norm_g, w_in, conv_w)


ROUTE_ROWS = 32


def _first_member(vals, target):
    idx = jnp.full_like(target, float(EPG - 1))
    for j in range(EPG - 2, -1, -1):
        idx = jnp.where(vals[j] == target, float(j), idx)
    return idx


def _route(h2, rwt_ref, rb_ref):
    h_hi, h_lo = _split_bf16(h2)
    w_hi, w_lo = _split_bf16(rwt_ref[...])
    logits = _dot_nt(w_hi, h_hi) + (_dot_nt(w_hi, h_lo) + _dot_nt(w_lo, h_hi))
    scores = jax.nn.sigmoid(logits)
    sel = scores + rb_ref[...]
    member = [sel[j * N_GROUPS:(j + 1) * N_GROUPS] for j in range(EPG)]
    m1 = functools.reduce(jnp.maximum, member)
    i1 = _first_member(member, m1)
    rest = [jnp.where(i1 == float(j), -jnp.inf, member[j]) for j in range(EPG)]
    m2 = functools.reduce(jnp.maximum, rest)
    i2 = _first_member(rest, m2)
    group_score = m1 + m2

    best, grp, first, second = group_score[0:1], jnp.zeros((1, TM), F32), i1[0:1], i2[0:1]
    for g in range(1, N_GROUPS):
        upd = group_score[g:g + 1] > best
        best = jnp.where(upd, group_score[g:g + 1], best)
        grp = jnp.where(upd, float(g), grp)
        first = jnp.where(upd, i1[g:g + 1], first)
        second = jnp.where(upd, i2[g:g + 1], second)
    a = jnp.minimum(first, second)
    b = jnp.maximum(first, second)
    row = lax.broadcasted_iota(jnp.int32, (N_EXPERTS, TM), 0).astype(F32)
    s_lo = jnp.sum(jnp.where(row == a * N_GROUPS + grp, scores, 0.0), axis=0, keepdims=True)
    s_hi = jnp.sum(jnp.where(row == b * N_GROUPS + grp, scores, 0.0), axis=0, keepdims=True)
    tot = s_lo + s_hi
    pair = jnp.where(a == 0.0, b - 1.0, jnp.where(a == 1.0, b + 1.0, 5.0))
    return grp * N_PAIRS + pair, s_lo / tot, s_hi / tot


def _post_mixer_kernel(*refs, n_x):
    a_ref, x_refs = refs[0], refs[1:1 + n_x]
    (mod_ref, g2_ref, wo_ref, rwt_ref, rb_ref,
     x1_ref, gates_ref, bucket_ref, rank_ref, cnt_ref, carry_ref) = refs[1 + n_x:]
    i = pl.program_id(0)

    @pl.when(i == 0)
    def _():
        carry_ref[...] = jnp.zeros_like(carry_ref)

    x1 = _token_tile(x_refs) + mod_ref[0, 2] * _dot(a_ref[...], wo_ref[...])
    x1_ref[...] = x1
    h2 = _modulate(x1, g2_ref[...], mod_ref[0, 3], mod_ref[0, 4])
    bucket, g_lo, g_hi = _route(h2, rwt_ref, rb_ref)

    onehot = lax.broadcasted_iota(jnp.int32, (ROUTE_ROWS, TM), 0).astype(F32) == bucket
    s_idx = lax.broadcasted_iota(jnp.int32, (TM, TM), 0)
    t_idx = lax.broadcasted_iota(jnp.int32, (TM, TM), 1)
    before = _dot(onehot.astype(BF16), (s_idx < t_idx).astype(BF16))
    oh = onehot.astype(F32)
    carry = carry_ref[...]
    rank = jnp.sum(oh * (before + carry[:, 0:1]), axis=0, keepdims=True)
    carry = carry + jnp.sum(oh, axis=1, keepdims=True)
    carry_ref[...] = carry
    cnt_ref[...] = carry
    bucket_ref[0] = bucket.astype(jnp.int32)
    rank_ref[0] = rank.astype(jnp.int32)
    grow = lax.broadcasted_iota(jnp.int32, (LANES, TM), 0)
    gates_ref[...] = jnp.where(grow == 0, g_lo, jnp.where(grow == 1, g_hi, 0.0)).T


def _post_mixer(a, x_parts, mod, norm_g2, w_out, router_wt, router_bt):
    idx_spec = pl.BlockSpec((1, 1, TM), lambda i: (i, 0, 0))
    return pl.pallas_call(
        functools.partial(_post_mixer_kernel, n_x=len(x_parts)),
        grid=(N_TILES,),
        in_specs=[_TILE_SPEC] + _token_specs(x_parts) + [
            _MOD_SPEC, _ROW_SPEC,
            pl.BlockSpec((D, D), lambda i: (0, 0)),
            pl.BlockSpec((N_EXPERTS, D), lambda i: (0, 0)),
            pl.BlockSpec((N_EXPERTS, 1), lambda i: (0, 0))],
        out_specs=[_TILE_SPEC, pl.BlockSpec((TM, LANES), lambda i: (i, 0)), idx_spec, idx_spec,
                   pl.BlockSpec((ROUTE_ROWS, LANES), lambda i: (0, 0))],
        out_shape=[jax.ShapeDtypeStruct((T, D), F32), jax.ShapeDtypeStruct((T, LANES), F32),
                   jax.ShapeDtypeStruct((N_TILES, 1, TM), jnp.int32),
                   jax.ShapeDtypeStruct((N_TILES, 1, TM), jnp.int32),
                   jax.ShapeDtypeStruct((ROUTE_ROWS, LANES), F32)],
        scratch_shapes=[pltpu.VMEM((ROUTE_ROWS, LANES), F32)],
        compiler_params=_cparams(),
        name="post_mixer",
    )(a, *x_parts, mod, norm_g2, w_out, router_wt, router_bt)


def _smem_tile_spec(tile_of_step):
    return pl.BlockSpec((1, 1, TM), lambda i, *_: (tile_of_step(i), 0, 0), memory_space=pltpu.SMEM)


def _start_rows(n, make_copy):
    for r in range(n):
        make_copy(r).start()


def _dispatch_kernel(partial_ref, slot_ref, x1_ref, mod_ref, g2_ref, gates_ref,
                     xs_ref, buf_ref, zero_ref, sems, zero_sem):
    i = pl.program_id(0)
    cur = i % 2

    @pl.when(i == 0)
    def _():
        zero_ref[...] = jnp.zeros_like(zero_ref)

        def zero_copy(j):
            return pltpu.make_async_copy(zero_ref, xs_ref.at[pl.ds(j * TM_E, TM_E)], zero_sem)

        def start(j, carry):
            @pl.when(partial_ref[j] != 0)
            def _():
                zero_copy(j).start()
            return carry

        def wait(j, carry):
            @pl.when(partial_ref[j] != 0)
            def _():
                zero_copy(j).wait()
            return carry

        lax.fori_loop(0, N_TILES_E, start, 0)
        lax.fori_loop(0, N_TILES_E, wait, 0)

    def wait_tile(b):
        pltpu.make_async_copy(buf_ref.at[b], xs_ref.at[pl.ds(0, TM)], sems.at[b]).wait()

    buf_ref[cur, :, :D] = _modulate(x1_ref[...], g2_ref[...], mod_ref[0, 3], mod_ref[0, 4])
    buf_ref[cur, :, D:] = gates_ref[...]
    _start_rows(TM, lambda r: pltpu.make_async_copy(
        buf_ref.at[cur, pl.ds(r, 1)], xs_ref.at[pl.ds(slot_ref[0, 0, r], 1)], sems.at[cur]))

    @pl.when(i > 0)
    def _():
        wait_tile(1 - cur)

    @pl.when(i == N_TILES - 1)
    def _():
        wait_tile(cur)


def _dispatch(partial, slot, x1, mod, norm_g2, gates):
    tile = lambda w: pl.BlockSpec((TM, w), lambda i, *_: (i, 0))
    return pl.pallas_call(
        _dispatch_kernel,
        grid_spec=pltpu.PrefetchScalarGridSpec(
            num_scalar_prefetch=1,
            grid=(N_TILES,),
            in_specs=[_smem_tile_spec(lambda i: i), tile(D),
                      pl.BlockSpec((1, 6, 1, D), lambda i, *_: (_cond_of_tile(i), 0, 0, 0)),
                      pl.BlockSpec((1, D), lambda i, *_: (0, 0)), tile(LANES)],
            out_specs=pl.BlockSpec(memory_space=pl.ANY),
            scratch_shapes=[pltpu.VMEM((2, TM, XW), F32), pltpu.VMEM((TM_E, XW), F32),
                            pltpu.SemaphoreType.DMA((2,)), pltpu.SemaphoreType.DMA(())],
        ),
        out_shape=jax.ShapeDtypeStruct((S_ROWS, XW), F32),
        compiler_params=_cparams(),
        name="moe_dispatch",
    )(partial, slot, x1, mod, norm_g2, gates)


def _expert_kernel(ea_ref, eb_ref, nrows_ref, x_ref, wga_ref, wua_ref, wda_ref,
                   wgb_ref, wub_ref, wdb_ref, y_ref):
    j = pl.program_id(0)
    n = nrows_ref[j]

    @pl.when(n > 0)
    def _():
        valid = lax.broadcasted_iota(jnp.int32, (TM_E, 1), 0) < n
        xe = jnp.where(valid, x_ref[...], 0.0)
        x = xe[:, :D].astype(BF16)

        def ffn(wg_ref, wu_ref, wd_ref, gate):
            act = _silu(_dot(x, wg_ref[0])) * _dot(x, wu_ref[0]) * gate
            return _dot(act.astype(BF16), wd_ref[0])

        y_ref[...] = (ffn(wga_ref, wua_ref, wda_ref, xe[:, D:D + 1])
                      + ffn(wgb_ref, wub_ref, wdb_ref, xe[:, D + 1:D + 2]))

    @pl.when(n == 0)
    def _():
        y_ref[...] = jnp.zeros_like(y_ref)


def _experts(tile_ea, tile_eb, tile_rows, xs, w_gate, w_up, w_down):
    wspec = lambda shape, sel: pl.BlockSpec((1,) + shape, lambda j, ea, eb, nr: (sel(ea, eb)[j], 0, 0))
    first = lambda ea, eb: ea
    second = lambda ea, eb: eb
    return pl.pallas_call(
        _expert_kernel,
        grid_spec=pltpu.PrefetchScalarGridSpec(
            num_scalar_prefetch=3,
            grid=(N_TILES_E,),
            in_specs=[pl.BlockSpec((TM_E, XW), lambda j, *_: (j, 0)),
                      wspec((D, D_FF), first), wspec((D, D_FF), first), wspec((D_FF, D), first),
                      wspec((D, D_FF), second), wspec((D, D_FF), second), wspec((D_FF, D), second)],
            out_specs=pl.BlockSpec((TM_E, D), lambda j, *_: (j, 0)),
        ),
        out_shape=jax.ShapeDtypeStruct((S_ROWS, D), F32),
        compiler_params=_cparams(),
        name="moe_experts",
    )(tile_ea, tile_eb, tile_rows, xs, w_gate, w_up, w_down, w_gate, w_up, w_down)


def _combine_kernel(slot_ref, next_slot_ref, x1_ref, mod_ref, gf_ref, y_ref, *rest, final):
    out_refs, (buf_ref, sems) = rest[:-2], rest[-2:]
    i = pl.program_id(0)
    cur = i % 2

    def gather(slots, b):
        _start_rows(TM, lambda r: pltpu.make_async_copy(
            y_ref.at[pl.ds(slots[0, 0, r], 1)], buf_ref.at[b, pl.ds(r, 1)], sems.at[b]))

    @pl.when(i == 0)
    def _():
        gather(slot_ref, 0)

    @pl.when(i + 1 < N_TILES)
    def _():
        gather(next_slot_ref, 1 - cur)

    pltpu.make_async_copy(y_ref.at[pl.ds(0, TM)], buf_ref.at[cur], sems.at[cur]).wait()
    x2 = x1_ref[...] + mod_ref[0, 5] * buf_ref[cur]
    if not final:
        out_refs[0][...] = x2
        return
    y = _rmsnorm(x2, gf_ref[...])
    yp_ref, ys_ref = out_refs

    @pl.when(i < PROMPT_TILES)
    def _():
        yp_ref[...] = y

    @pl.when(i >= PROMPT_TILES)
    def _():
        ys_ref[...] = y


def _combine(slot, x1, mod, final_g, y_sorted, *, final):
    if final:
        out_specs = [_XP_SPEC, _XS_SPEC]
        out_shape = [jax.ShapeDtypeStruct((T_PROMPT, D), F32), jax.ShapeDtypeStruct((T_SAMPLE, D), F32)]
    else:
        out_specs = [_TILE_SPEC]
        out_shape = [jax.ShapeDtypeStruct((T, D), F32)]
    return pl.pallas_call(
        functools.partial(_combine_kernel, final=final),
        grid=(N_TILES,),
        in_specs=[_smem_tile_spec(lambda i: i), _smem_tile_spec(lambda i: jnp.minimum(i + 1, N_TILES - 1)),
                  _TILE_SPEC, _MOD_SPEC, _ROW_SPEC, pl.BlockSpec(memory_space=pl.ANY)],
        out_specs=out_specs,
        scratch_shapes=[pltpu.VMEM((2, TM, D), F32), pltpu.SemaphoreType.DMA((2,))],
        out_shape=out_shape,
        compiler_params=_cparams(),
        name="moe_combine_final" if final else "moe_combine",
    )(slot, slot, x1, mod, final_g, y_sorted)


def _bucket_plan(bucket, rank, counts):
    cnt = counts[:N_BUCKETS, 0].astype(jnp.int32)
    tiles = (cnt + TM_E - 1) // TM_E
    tile_end = jnp.cumsum(tiles)
    tile_start = tile_end - tiles
    off = tile_start * TM_E
    slot = rank + sum(jnp.where(bucket == b, off[b], 0) for b in range(N_BUCKETS))
    j = jnp.arange(N_TILES_E, dtype=jnp.int32)
    n_live = tile_end[-1]
    j_live = jnp.minimum(j, n_live - 1)
    b = jnp.sum((tile_end[None, :] <= j_live[:, None]).astype(jnp.int32), axis=1)
    b = jnp.minimum(b, N_BUCKETS - 1)
    rows = jnp.clip(cnt[b] - (j - tile_start[b]) * TM_E, 0, TM_E)
    rows = jnp.where(j < n_live, rows, 0).astype(jnp.int32)
    grp = b // N_PAIRS
    pair = b % N_PAIRS
    ea = grp * EPG + jnp.asarray(_PAIR_A, jnp.int32)[pair]
    eb = grp * EPG + jnp.asarray(_PAIR_B, jnp.int32)[pair]
    return slot.astype(jnp.int32), ea, eb, rows


def _moe(a, x_parts, mod, norm_g2, w_out, router_wt, router_bt, w_gate, w_up, w_down, final_g, *, final):
    x1, gates, bucket, rank, counts = _post_mixer(a, x_parts, mod, norm_g2, w_out, router_wt, router_bt)
    slot, ea, eb, rows = _bucket_plan(bucket, rank, counts)
    partial = (rows < TM_E).astype(jnp.int32)
    xs = _dispatch(partial, slot, x1, mod, norm_g2, gates)
    ys = _experts(ea, eb, rows, xs, w_gate, w_up, w_down)
    return _combine(slot, x1, mod, final_g, ys, final=final)


def kernel(x_prompt, x_sample, state_hgrn, c, c_ctx, w_mod, b_mod, norm_mix_g, norm_ffn_g, norm_final_g,
           hgrn_w_in, hgrn_lb_logits, hgrn_norm_g, hgrn_w_out, conv_w_in, conv_w, conv_w_out,
           router_w, router_b, moe_w_gate, moe_w_up, moe_w_down):
    x_parts = (x_prompt.reshape(T_PROMPT, D), x_sample.reshape(T_SAMPLE, D))
    cond = jnp.concatenate([c_ctx[None], c, jnp.zeros((COND_ROWS - N_COND, D), F32)], axis=0)
    mods = _adaln(cond, w_mod, b_mod)[:, :N_COND].reshape(2, N_COND, 6, 1, D)
    member_major = lambda w: w.reshape(N_GROUPS, EPG, -1).transpose(1, 0, 2).reshape(N_EXPERTS, -1)
    rwt = member_major(router_w.T)
    rbt = member_major(router_b.reshape(N_EXPERTS, 1))
    final_g = norm_final_g.reshape(1, D)
    bf = lambda w: w.astype(BF16)

    q, v, ff, fb, gg = _hgrn_proj(x_parts, mods[0], norm_mix_g[0:1], bf(hgrn_w_in[0]))
    s0 = state_hgrn[:, 0]
    o_f, s_f = _gla_fwd(q, v, ff, hgrn_lb_logits, s0)
    a, s_b = _gla_bwd(q, v, fb, hgrn_lb_logits, s0, o_f, gg, hgrn_norm_g[0:1])
    (x,) = _moe(a, x_parts, mods[0], norm_ffn_g[0:1], bf(hgrn_w_out[0]), rwt, rbt,
                bf(moe_w_gate[0]), bf(moe_w_up[0]), bf(moe_w_down[0]), final_g, final=False)

    a = _conv_mixer(x, mods[1], norm_mix_g[1:2], bf(conv_w_in[0]), conv_w[0])
    y_p, y_s = _moe(a, (x,), mods[1], norm_ffn_g[1:2], bf(conv_w_out[0]), rwt, rbt,
                    bf(moe_w_gate[1]), bf(moe_w_up[1]), bf(moe_w_down[1]), final_g, final=True)

    new_state = jnp.stack([s_f, s_b], axis=1)[:, None]
    return (y_p.reshape(N_PROMPT_SEQ, PROMPT_LEN, D), y_s.reshape(N_SAMPLE_SEQ, SAMPLE_LEN, D), new_state)
```

```python
import functools

import jax
import jax.numpy as jnp
from jax import lax
from jax.experimental import pallas as pl
from jax.experimental.pallas import tpu as pltpu

F32 = jnp.float32
BF16 = jnp.bfloat16

D = 1024
N_PROMPT_SEQ = 16
PROMPT_LEN = 256
N_SAMPLE_SEQ = 2
SAMPLE_LEN = 4096
GRID_W = 64
T_PROMPT = N_PROMPT_SEQ * PROMPT_LEN
T_SAMPLE = N_SAMPLE_SEQ * SAMPLE_LEN
T = T_PROMPT + T_SAMPLE
N_COND = 1 + N_SAMPLE_SEQ
COND_ROWS = 8

HEADS = 8
DK = 128
DV = 128
CHUNK = 64
N_EXPERTS = 16
N_GROUPS = 4
EPG = 4
N_PAIRS = 6
N_BUCKETS = N_GROUPS * N_PAIRS
D_FF = 512
EPS = 1e-6

TM = 256
N_TILES = T // TM
PROMPT_TILES = T_PROMPT // TM
TILES_PER_SAMPLE = SAMPLE_LEN // TM
CHUNKS_PER_TILE = TM // CHUNK
TM_E = 256
N_TILES_E = T // TM_E + N_BUCKETS
S_ROWS = N_TILES_E * TM_E
XW = D + 128
LANES = 128
MOD_TN = 1536

VMEM_LIMIT = 56 * 1024 * 1024

_SLOT_A = (0, 3, 3, 0, 0, 1)
_SLOT_B = (1, 1, 2, 2, 3, 2)
_SLOT_A_IS_HIGHER = tuple(int(a > b) for a, b in zip(_SLOT_A, _SLOT_B))


def _cparams():
    return pltpu.CompilerParams(dimension_semantics=("arbitrary",), vmem_limit_bytes=VMEM_LIMIT)


def _cond_of_tile(t):
    return jnp.where(t < PROMPT_TILES, 0, 1 + (t - PROMPT_TILES) // TILES_PER_SAMPLE)


def _sample_of_tile(t):
    return jnp.clip((t - PROMPT_TILES) // TILES_PER_SAMPLE, 0, N_SAMPLE_SEQ - 1)


def _rmsnorm(x, g):
    ms = jnp.mean(x * x, axis=-1, keepdims=True)
    return (x * lax.rsqrt(ms + EPS)) * g


def _modulate(x, g, shift, scale):
    return _rmsnorm(x, g) * (1.0 + scale) + shift


def _silu(x):
    return x * jax.nn.sigmoid(x)


def _dot(a, b):
    return jnp.dot(a, b, preferred_element_type=F32)


def _dot_nt(a, b):
    return lax.dot_general(a, b, (((1,), (1,)), ((), ())), preferred_element_type=F32)


def _dot_tn(a, b):
    return lax.dot_general(a, b, (((0,), (0,)), ((), ())), preferred_element_type=F32)


def _split_bf16(x):
    hi = x.astype(BF16)
    lo = (x - hi.astype(F32)).astype(BF16)
    return hi, lo


def _mod_kernel(cond_ref, w_ref, b_ref, o_ref):
    s = _silu(cond_ref[...])
    o_ref[0] = _dot(s.astype(BF16), w_ref[0].astype(BF16)) + b_ref[0]


def _adaln(cond, w_mod, b_mod):
    depth = w_mod.shape[0]
    return pl.pallas_call(
        _mod_kernel,
        grid=(depth, 6 * D // MOD_TN),
        in_specs=[
            pl.BlockSpec((COND_ROWS, D), lambda i, j: (0, 0)),
            pl.BlockSpec((1, D, MOD_TN), lambda i, j: (i, 0, j)),
            pl.BlockSpec((1, 1, MOD_TN), lambda i, j: (i, 0, j)),
        ],
        out_specs=pl.BlockSpec((1, COND_ROWS, MOD_TN), lambda i, j: (i, 0, j)),
        out_shape=jax.ShapeDtypeStruct((depth, COND_ROWS, 6 * D), F32),
        compiler_params=pltpu.CompilerParams(
            dimension_semantics=("arbitrary", "arbitrary"), vmem_limit_bytes=VMEM_LIMIT),
        name="adaln",
    )(cond, w_mod, b_mod.reshape(depth, 1, 6 * D))


_MOD_SPEC = pl.BlockSpec((1, 6, 1, D), lambda i: (_cond_of_tile(i), 0, 0, 0))
_ROW_SPEC = pl.BlockSpec((1, D), lambda i: (0, 0))
_TILE_SPEC = pl.BlockSpec((TM, D), lambda i: (i, 0))


_XP_SPEC = pl.BlockSpec((TM, D), lambda i: (jnp.minimum(i, PROMPT_TILES - 1), 0))
_XS_SPEC = pl.BlockSpec((TM, D), lambda i: (jnp.maximum(i - PROMPT_TILES, 0), 0))


def _token_tile(x_refs):
    if len(x_refs) == 1:
        return x_refs[0][...]
    return jnp.where(pl.program_id(0) < PROMPT_TILES, x_refs[0][...], x_refs[1][...])


def _token_specs(x_parts):
    return [_TILE_SPEC] if len(x_parts) == 1 else [_XP_SPEC, _XS_SPEC]


def _hgrn_proj_kernel(xp_ref, xs_ref, mod_ref, g_ref, w_ref, q_ref, v_ref, ff_ref, fb_ref, gg_ref):
    x = _token_tile((xp_ref, xs_ref))
    h = _modulate(x, g_ref[...], mod_ref[0, 0], mod_ref[0, 1]).astype(BF16)
    q_ref[...] = (_dot(h, w_ref[:, 0 * D:1 * D]) * (DK ** -0.5)).astype(BF16)
    v_ref[...] = _dot(h, w_ref[:, 1 * D:2 * D]).astype(BF16)
    ff_ref[...] = _dot(h, w_ref[:, 2 * D:3 * D])
    fb_ref[...] = _dot(h, w_ref[:, 3 * D:4 * D])
    gg_ref[...] = _dot(h, w_ref[:, 4 * D:5 * D]).astype(BF16)


def _hgrn_proj(x_parts, mod, norm_g, w_in):
    bf = jax.ShapeDtypeStruct((T, D), BF16)
    f32 = jax.ShapeDtypeStruct((T, D), F32)
    return pl.pallas_call(
        _hgrn_proj_kernel,
        grid=(N_TILES,),
        in_specs=_token_specs(x_parts) + [_MOD_SPEC, _ROW_SPEC, pl.BlockSpec((D, 5 * D), lambda i: (0, 0))],
        out_specs=[_TILE_SPEC] * 5,
        out_shape=[bf, bf, f32, f32, bf],
        compiler_params=_cparams(),
        name="hgrn_proj",
    )(*x_parts, mod, norm_g, w_in)


def _gla_tile(t, q_ref, v_ref, f_ref, lbl_ref, s0_ref, st_ref, ma_ref, o_ref, *, reverse):
    is_prompt = t < PROMPT_TILES
    rel = t - PROMPT_TILES
    edge = (TILES_PER_SAMPLE - 1) if reverse else 0
    sample_start = jnp.logical_and(rel >= 0, rel % TILES_PER_SAMPLE == edge)

    @pl.when(is_prompt)
    def _():
        st_ref[...] = jnp.zeros_like(st_ref)

    @pl.when(sample_start)
    def _():
        for h in range(HEADS):
            st_ref[h] = s0_ref[0, 0, h].T

    logits = lbl_ref[...]
    e = jnp.exp(logits - jnp.max(logits, axis=0, keepdims=True))
    lb = e[0:1] / jnp.sum(e, axis=0, keepdims=True)

    ref_idx = (CHUNK - 1 - CHUNK // 2) if reverse else CHUNK // 2
    last_idx = 0 if reverse else CHUNK - 1
    shift = CHUNK.bit_length() - 1

    row = lax.broadcasted_iota(jnp.int32, (TM, TM), 0)
    col = lax.broadcasted_iota(jnp.int32, (TM, TM), 1)
    same = (row >> shift) == (col >> shift)
    ref_row = ((row >> shift) << shift) + ref_idx
    keep = jnp.logical_and(same, (row <= col) if reverse else (row >= col))

    @pl.when(pl.program_id(0) == 0)
    def _():
        at_ref = jnp.logical_and(same, (ref_row <= col) if reverse else (ref_row >= col))
        ma_ref[...] = (keep.astype(F32) - at_ref.astype(F32)).astype(BF16)

    ma = ma_ref[...]

    rr = lax.broadcasted_iota(jnp.int32, (2 * CHUNKS_PER_TILE, TM), 0)
    cc = lax.broadcasted_iota(jnp.int32, (2 * CHUNKS_PER_TILE, TM), 1)
    chunk = rr & (CHUNKS_PER_TILE - 1)
    target = chunk * CHUNK + jnp.where(rr < CHUNKS_PER_TILE, ref_idx, last_idx)
    covered = (cc >= target) if reverse else (cc <= target)
    rm = jnp.logical_and((cc >> shift) == chunk, covered).astype(BF16)

    chunk_rows = [slice(c * CHUNK, (c + 1) * CHUNK) for c in range(CHUNKS_PER_TILE)]
    order = range(CHUNKS_PER_TILE - 1, -1, -1) if reverse else range(CHUNKS_PER_TILE)
    pair_w = 2 * DK

    def per_chunk_scale(x, scale):
        return jnp.concatenate([x[rs] * scale[c:c + 1] for c, rs in enumerate(chunk_rows)], axis=0)

    def pair_cols(p):
        return slice(p * pair_w, (p + 1) * pair_w)

    def stage_decay(p):
        lbp = lb[:, pair_cols(p)]
        f = lbp + (1.0 - lbp) * jax.nn.sigmoid(f_ref[:, pair_cols(p)])
        lf_hi, lf_lo = _split_bf16(jnp.log(f))
        z = _dot(ma, lf_hi) + _dot(ma, lf_lo)
        marks = _dot(rm, lf_hi) + _dot(rm, lf_lo)
        return 1.0 - f, z, marks

    def stage_scores(p, k, z, marks):
        ref, last = marks[:CHUNKS_PER_TILE], marks[CHUNKS_PER_TILE:]
        qa32 = q_ref[:, pair_cols(p)].astype(F32) * jnp.exp(z)
        kb32 = k * jnp.exp(-z)
        qa = qa32.astype(BF16)
        kb = kb32.astype(BF16)
        qe = per_chunk_scale(qa32, jnp.exp(ref)).astype(BF16)
        kd = per_chunk_scale(kb32, jnp.exp(last - ref)).astype(BF16)
        v = v_ref[:, pair_cols(p)]
        scores, incr = [], []
        for hh in range(2):
            ls = slice(hh * DK, (hh + 1) * DK)
            scores.append(_dot_nt(qa[:, ls], kb[:, ls]))
            kd_h = kd[:, ls]
            kd_blocks = jnp.concatenate(
                [jnp.concatenate([part for part in (jnp.zeros((c * CHUNK, DK), BF16), kd_h[rs],
                                                    jnp.zeros((TM - (c + 1) * CHUNK, DK), BF16))
                                  if part.shape[0]], axis=0)
                 for c, rs in enumerate(chunk_rows)], axis=1)
            incr.append(_dot_tn(v[:, ls], kd_blocks))
        return scores, incr, qe, jnp.exp(last), v

    def stage_output(p, scores, incr, qe, decay, v):
        for hh in range(2):
            h = 2 * p + hh
            ls = slice(hh * DK, (hh + 1) * DK)
            hs = slice(h * DK, (h + 1) * DK)
            o_intra = _dot(jnp.where(keep, scores[hh], 0.0).astype(BF16), v[:, ls])
            st = st_ref[h]
            for c in order:
                rs = chunk_rows[c]
                o_ref[rs, hs] = o_intra[rs] + _dot_nt(qe[rs, ls], st.astype(BF16))
                st = st * decay[c:c + 1, ls] + incr[hh][:, c * DK:(c + 1) * DK]
            st_ref[h] = st

    n_pairs = HEADS // 2
    decayed, scored = {}, {}
    for step in range(n_pairs + 2):
        if step < n_pairs:
            decayed[step] = stage_decay(step)
        if 0 <= step - 1 < n_pairs:
            scored[step - 1] = stage_scores(step - 1, *decayed.pop(step - 1))
        if 0 <= step - 2 < n_pairs:
            stage_output(step - 2, *scored.pop(step - 2))


def _store_prompt_state(t, st_ref, snew_ref):
    @pl.when(t < PROMPT_TILES)
    def _():
        for h in range(HEADS):
            snew_ref[0, h] = st_ref[h].T


def _gla_fwd_kernel(q_ref, v_ref, f_ref, lbl_ref, s0_ref, o_ref, snew_ref, st_ref, ma_ref):
    t = pl.program_id(0)
    _gla_tile(t, q_ref, v_ref, f_ref, lbl_ref, s0_ref, st_ref, ma_ref, o_ref, reverse=False)
    _store_prompt_state(t, st_ref, snew_ref)


def _gla_bwd_kernel(q_ref, v_ref, f_ref, lbl_ref, s0_ref, of_ref, gg_ref, ng_ref,
                    a_ref, snew_ref, st_ref, ma_ref, ob_ref):
    t = N_TILES - 1 - pl.program_id(0)
    _gla_tile(t, q_ref, v_ref, f_ref, lbl_ref, s0_ref, st_ref, ma_ref, ob_ref, reverse=True)
    _store_prompt_state(t, st_ref, snew_ref)
    ng = ng_ref[...]
    for h in range(HEADS):
        hs = slice(h * DV, (h + 1) * DV)
        o = of_ref[:, hs] + ob_ref[:, hs]
        a_ref[:, hs] = (_rmsnorm(o, ng) * _silu(gg_ref[:, hs].astype(F32))).astype(BF16)


def _state_specs(direction, tile_of_step):
    s0 = pl.BlockSpec((1, 1, HEADS, DK, DV),
                      lambda i: (_sample_of_tile(tile_of_step(i)), direction, 0, 0, 0))
    snew = pl.BlockSpec((1, HEADS, DK, DV),
                        lambda i: (jnp.minimum(tile_of_step(i), PROMPT_TILES - 1), 0, 0, 0))
    return s0, snew


def _gla_fwd(q, v, ff, lb_logits, s0):
    s0_spec, snew_spec = _state_specs(0, lambda i: i)
    return pl.pallas_call(
        _gla_fwd_kernel,
        grid=(N_TILES,),
        in_specs=[_TILE_SPEC, _TILE_SPEC, _TILE_SPEC,
                  pl.BlockSpec(lb_logits.shape, lambda i: (0, 0)), s0_spec],
        out_specs=[_TILE_SPEC, snew_spec],
        out_shape=[jax.ShapeDtypeStruct((T, D), F32),
                   jax.ShapeDtypeStruct((N_PROMPT_SEQ, HEADS, DK, DV), F32)],
        scratch_shapes=[pltpu.VMEM((HEADS, DV, DK), F32), pltpu.VMEM((TM, TM), BF16)],
        compiler_params=_cparams(),
        name="gla_fwd",
    )(q, v, ff, lb_logits, s0)


def _gla_bwd(q, v, fb, lb_logits, s0, o_f, gg, head_norm_g):
    rev = lambda i: N_TILES - 1 - i
    tile = pl.BlockSpec((TM, D), lambda i: (rev(i), 0))
    s0_spec, snew_spec = _state_specs(1, rev)
    return pl.pallas_call(
        _gla_bwd_kernel,
        grid=(N_TILES,),
        in_specs=[tile, tile, tile, pl.BlockSpec(lb_logits.shape, lambda i: (0, 0)), s0_spec,
                  tile, tile, pl.BlockSpec((1, DV), lambda i: (0, 0))],
        out_specs=[tile, snew_spec],
        out_shape=[jax.ShapeDtypeStruct((T, D), BF16),
                   jax.ShapeDtypeStruct((N_PROMPT_SEQ, HEADS, DK, DV), F32)],
        scratch_shapes=[pltpu.VMEM((HEADS, DV, DK), F32), pltpu.VMEM((TM, TM), BF16),
                        pltpu.VMEM((TM, D), F32)],
        compiler_params=_cparams(),
        name="gla_bwd",
    )(q, v, fb, lb_logits, s0, o_f, gg, head_norm_g)


def _conv_mixer_kernel(x_ref, mod_ref, g_ref, w_ref, cw_ref, a_ref):
    i = pl.program_id(0)
    h = _modulate(x_ref[...], g_ref[...], mod_ref[0, 0], mod_ref[0, 1]).astype(BF16)
    bg = _dot(h, w_ref[:, 0 * D:1 * D])
    u = _dot(h, w_ref[:, 1 * D:2 * D]) * _dot(h, w_ref[:, 2 * D:3 * D])
    seg = jnp.where(i < PROMPT_TILES, PROMPT_LEN, GRID_W)
    pos = lax.broadcasted_iota(jnp.int32, (TM, 1), 0) & (seg - 1)
    prev = jnp.where(pos == 0, 0.0, pltpu.roll(u, 1, axis=0))
    nxt = jnp.where(pos == seg - 1, 0.0, pltpu.roll(u, TM - 1, axis=0))
    conv = cw_ref[0:1] * prev + cw_ref[1:2] * u + cw_ref[2:3] * nxt
    a_ref[...] = (bg * conv).astype(BF16)


def _conv_mixer(x, mod, norm_g, w_in, conv_w):
    return pl.pallas_call(
        _conv_mixer_kernel,
        grid=(N_TILES,),
        in_specs=[_TILE_SPEC, _MOD_SPEC, _ROW_SPEC, pl.BlockSpec((D, 3 * D), lambda i: (0, 0)),
                  pl.BlockSpec((3, D), lambda i: (0, 0))],
        out_specs=_TILE_SPEC,
        out_shape=jax.ShapeDtypeStruct((T, D), BF16),
        compiler_params=_cparams(),
        name="conv_mixer",
    )(x, mod, norm_g, w_in, conv_w)


ROUTE_ROWS = 32


def _first_member(vals, target):
    idx = jnp.full_like(target, float(EPG - 1))
    for j in range(EPG - 2, -1, -1):
        idx = jnp.where(vals[j] == target, float(j), idx)
    return idx


def _route(h2, rwt_ref, rb_ref):
    h_hi, h_lo = _split_bf16(h2)
    w_hi, w_lo = _split_bf16(rwt_ref[...])
    logits = _dot_nt(w_hi, h_hi) + (_dot_nt(w_hi, h_lo) + _dot_nt(w_lo, h_hi))
    scores = jax.nn.sigmoid(logits)
    sel = scores + rb_ref[...]
    member = [sel[j * N_GROUPS:(j + 1) * N_GROUPS] for j in range(EPG)]
    m1 = functools.reduce(jnp.maximum, member)
    i1 = _first_member(member, m1)
    rest = [jnp.where(i1 == float(j), -jnp.inf, member[j]) for j in range(EPG)]
    m2 = functools.reduce(jnp.maximum, rest)
    i2 = _first_member(rest, m2)
    group_score = m1 + m2

    best, grp, first, second = group_score[0:1], jnp.zeros((1, TM), F32), i1[0:1], i2[0:1]
    for g in range(1, N_GROUPS):
        upd = group_score[g:g + 1] > best
        best = jnp.where(upd, group_score[g:g + 1], best)
        grp = jnp.where(upd, float(g), grp)
        first = jnp.where(upd, i1[g:g + 1], first)
        second = jnp.where(upd, i2[g:g + 1], second)
    a = jnp.minimum(first, second)
    b = jnp.maximum(first, second)
    row = lax.broadcasted_iota(jnp.int32, (N_EXPERTS, TM), 0).astype(F32)
    s_lo = jnp.sum(jnp.where(row == a * N_GROUPS + grp, scores, 0.0), axis=0, keepdims=True)
    s_hi = jnp.sum(jnp.where(row == b * N_GROUPS + grp, scores, 0.0), axis=0, keepdims=True)
    tot = s_lo + s_hi
    pair = jnp.where(a == 0.0, jnp.where(b == 1.0, 0.0, jnp.where(b == 2.0, 3.0, 4.0)),
                     jnp.where(a == 1.0, jnp.where(b == 2.0, 5.0, 1.0), 2.0))
    return grp * N_PAIRS + pair, s_lo / tot, s_hi / tot


def _post_mixer_kernel(*refs, n_x):
    a_ref, x_refs = refs[0], refs[1:1 + n_x]
    (mod_ref, g2_ref, wo_ref, rwt_ref, rb_ref,
     x1_ref, gates_ref, bucket_ref, rank_ref, cnt_ref, carry_ref) = refs[1 + n_x:]
    i = pl.program_id(0)

    @pl.when(i == 0)
    def _():
        carry_ref[...] = jnp.zeros_like(carry_ref)

    x1 = _token_tile(x_refs) + mod_ref[0, 2] * _dot(a_ref[...], wo_ref[...])
    x1_ref[...] = x1
    h2 = _modulate(x1, g2_ref[...], mod_ref[0, 3], mod_ref[0, 4])
    bucket, g_lo, g_hi = _route(h2, rwt_ref, rb_ref)

    onehot = lax.broadcasted_iota(jnp.int32, (ROUTE_ROWS, TM), 0).astype(F32) == bucket
    s_idx = lax.broadcasted_iota(jnp.int32, (TM, TM), 0)
    t_idx = lax.broadcasted_iota(jnp.int32, (TM, TM), 1)
    before = _dot(onehot.astype(BF16), (s_idx < t_idx).astype(BF16))
    oh = onehot.astype(F32)
    carry = carry_ref[...]
    rank = jnp.sum(oh * (before + carry[:, 0:1]), axis=0, keepdims=True)
    carry = carry + jnp.sum(oh, axis=1, keepdims=True)
    carry_ref[...] = carry
    cnt_ref[...] = carry
    bucket_ref[0] = bucket.astype(jnp.int32)
    rank_ref[0] = rank.astype(jnp.int32)
    grow = lax.broadcasted_iota(jnp.int32, (LANES, TM), 0)
    gates_ref[...] = jnp.where(grow == 0, g_lo, jnp.where(grow == 1, g_hi, 0.0)).T


def _post_mixer(a, x_parts, mod, norm_g2, w_out, router_wt, router_bt):
    idx_spec = pl.BlockSpec((1, 1, TM), lambda i: (i, 0, 0))
    return pl.pallas_call(
        functools.partial(_post_mixer_kernel, n_x=len(x_parts)),
        grid=(N_TILES,),
        in_specs=[_TILE_SPEC] + _token_specs(x_parts) + [
            _MOD_SPEC, _ROW_SPEC,
            pl.BlockSpec((D, D), lambda i: (0, 0)),
            pl.BlockSpec((N_EXPERTS, D), lambda i: (0, 0)),
            pl.BlockSpec((N_EXPERTS, 1), lambda i: (0, 0))],
        out_specs=[_TILE_SPEC, pl.BlockSpec((TM, LANES), lambda i: (i, 0)), idx_spec, idx_spec,
                   pl.BlockSpec((ROUTE_ROWS, LANES), lambda i: (0, 0))],
        out_shape=[jax.ShapeDtypeStruct((T, D), F32), jax.ShapeDtypeStruct((T, LANES), F32),
                   jax.ShapeDtypeStruct((N_TILES, 1, TM), jnp.int32),
                   jax.ShapeDtypeStruct((N_TILES, 1, TM), jnp.int32),
                   jax.ShapeDtypeStruct((ROUTE_ROWS, LANES), F32)],
        scratch_shapes=[pltpu.VMEM((ROUTE_ROWS, LANES), F32)],
        compiler_params=_cparams(),
        name="post_mixer",
    )(a, *x_parts, mod, norm_g2, w_out, router_wt, router_bt)


def _smem_tile_spec(tile_of_step):
    return pl.BlockSpec((1, 1, TM), lambda i, *_: (tile_of_step(i), 0, 0), memory_space=pltpu.SMEM)


def _start_rows(n, make_copy):
    for r in range(n):
        make_copy(r).start()


def _dispatch_kernel(partial_ref, slot_ref, x1_ref, mod_ref, g2_ref, gates_ref,
                     xs_ref, buf_ref, zero_ref, sems, zero_sem):
    i = pl.program_id(0)
    cur = i % 2

    @pl.when(i == 0)
    def _():
        zero_ref[...] = jnp.zeros_like(zero_ref)

        def zero_copy(j):
            return pltpu.make_async_copy(zero_ref, xs_ref.at[pl.ds(j * TM_E, TM_E)], zero_sem)

        def start(j, carry):
            @pl.when(partial_ref[j] != 0)
            def _():
                zero_copy(j).start()
            return carry

        def wait(j, carry):
            @pl.when(partial_ref[j] != 0)
            def _():
                zero_copy(j).wait()
            return carry

        lax.fori_loop(0, N_TILES_E, start, 0)
        lax.fori_loop(0, N_TILES_E, wait, 0)

    def wait_tile(b):
        pltpu.make_async_copy(buf_ref.at[b], xs_ref.at[pl.ds(0, TM)], sems.at[b]).wait()

    h2 = _modulate(x1_ref[...], g2_ref[...], mod_ref[0, 3], mod_ref[0, 4])
    gates = gates_ref[...]

    def send(b):
        buf_ref[b, :, :D] = h2
        buf_ref[b, :, D:] = gates
        _start_rows(TM, lambda r: pltpu.make_async_copy(
            buf_ref.at[b, pl.ds(r, 1)], xs_ref.at[pl.ds(slot_ref[0, 0, r], 1)], sems.at[b]))

        @pl.when(i > 0)
        def _():
            wait_tile(1 - b)

        @pl.when(i == N_TILES - 1)
        def _():
            wait_tile(b)

    for b in range(2):
        pl.when(cur == b)(functools.partial(send, b))


def _dispatch(partial, slot, x1, mod, norm_g2, gates):
    tile = lambda w: pl.BlockSpec((TM, w), lambda i, *_: (i, 0))
    return pl.pallas_call(
        _dispatch_kernel,
        grid_spec=pltpu.PrefetchScalarGridSpec(
            num_scalar_prefetch=1,
            grid=(N_TILES,),
            in_specs=[_smem_tile_spec(lambda i: i), tile(D),
                      pl.BlockSpec((1, 6, 1, D), lambda i, *_: (_cond_of_tile(i), 0, 0, 0)),
                      pl.BlockSpec((1, D), lambda i, *_: (0, 0)), tile(LANES)],
            out_specs=pl.BlockSpec(memory_space=pl.ANY),
            scratch_shapes=[pltpu.VMEM((2, TM, XW), F32), pltpu.VMEM((TM_E, XW), F32),
                            pltpu.SemaphoreType.DMA((2,)), pltpu.SemaphoreType.DMA(())],
        ),
        out_shape=jax.ShapeDtypeStruct((S_ROWS, XW), F32),
        compiler_params=_cparams(),
        name="moe_dispatch",
    )(partial, slot, x1, mod, norm_g2, gates)


def _expert_kernel(ea_ref, eb_ref, a_higher_ref, nrows_ref, x_ref, *refs):
    w32 = (refs[0:3], refs[3:6])
    y_ref = refs[6]
    w16 = (refs[7:10], refs[10:13])
    j = pl.program_id(0)
    n = nrows_ref[j]
    prev = jnp.maximum(j - 1, 0)

    for e_ref, src, dst in ((ea_ref, w32[0], w16[0]), (eb_ref, w32[1], w16[1])):
        @pl.when(jnp.logical_or(j == 0, e_ref[j] != e_ref[prev]))
        def _(src=src, dst=dst):
            for s, d in zip(src, dst):
                d[...] = s[0, 0].astype(BF16)

    @pl.when(n > 0)
    def _():
        valid = lax.broadcasted_iota(jnp.int32, (TM_E, 1), 0) < n
        xe = jnp.where(valid, x_ref[...], 0.0)
        x = xe[:, :D].astype(BF16)
        g_lo, g_hi = xe[:, D:D + 1], xe[:, D + 1:D + 2]
        a_higher = a_higher_ref[j] != 0

        def ffn(w, gate):
            wg_ref, wu_ref, wd_ref = w
            act = _silu(_dot(x, wg_ref[...])) * _dot(x, wu_ref[...]) * gate
            return _dot(act.astype(BF16), wd_ref[...])

        y_ref[...] = (ffn(w16[0], jnp.where(a_higher, g_hi, g_lo))
                      + ffn(w16[1], jnp.where(a_higher, g_lo, g_hi)))

    @pl.when(n == 0)
    def _():
        y_ref[...] = jnp.zeros_like(y_ref)


def _experts(tile_ea, tile_eb, tile_a_higher, tile_rows, xs, w_gate, w_up, w_down, layer):
    wspec = lambda shape, sel: pl.BlockSpec((1, 1) + shape,
                                            lambda j, ea, eb, *_: (layer, sel(ea, eb)[j], 0, 0))
    first = lambda ea, eb: ea
    second = lambda ea, eb: eb
    w_shapes = ((D, D_FF), (D, D_FF), (D_FF, D))
    return pl.pallas_call(
        _expert_kernel,
        grid_spec=pltpu.PrefetchScalarGridSpec(
            num_scalar_prefetch=4,
            grid=(N_TILES_E,),
            in_specs=[pl.BlockSpec((TM_E, XW), lambda j, *_: (j, 0))]
                     + [wspec(s, first) for s in w_shapes] + [wspec(s, second) for s in w_shapes],
            out_specs=pl.BlockSpec((TM_E, D), lambda j, *_: (j, 0)),
            scratch_shapes=[pltpu.VMEM(s, BF16) for s in w_shapes * 2],
        ),
        out_shape=jax.ShapeDtypeStruct((S_ROWS, D), F32),
        compiler_params=_cparams(),
        name="moe_experts",
    )(tile_ea, tile_eb, tile_a_higher, tile_rows, xs, w_gate, w_up, w_down, w_gate, w_up, w_down)


def _combine_kernel(slot_ref, next_slot_ref, x1_ref, mod_ref, gf_ref, y_ref, *rest, final):
    out_refs, (buf_ref, sems) = rest[:-2], rest[-2:]
    i = pl.program_id(0)
    cur = i % 2

    def gather(slots, b):
        _start_rows(TM, lambda r: pltpu.make_async_copy(
            y_ref.at[pl.ds(slots[0, 0, r], 1)], buf_ref.at[b, pl.ds(r, 1)], sems.at[b]))

    @pl.when(i == 0)
    def _():
        gather(slot_ref, 0)

    def combine(b):
        @pl.when(i + 1 < N_TILES)
        def _():
            gather(next_slot_ref, 1 - b)

        pltpu.make_async_copy(y_ref.at[pl.ds(0, TM)], buf_ref.at[b], sems.at[b]).wait()
        x2 = x1_ref[...] + mod_ref[0, 5] * buf_ref[b]
        if not final:
            out_refs[0][...] = x2
            return
        y = _rmsnorm(x2, gf_ref[...])
        yp_ref, ys_ref = out_refs

        @pl.when(i < PROMPT_TILES)
        def _():
            yp_ref[...] = y

        @pl.when(i >= PROMPT_TILES)
        def _():
            ys_ref[...] = y

    for b in range(2):
        pl.when(cur == b)(functools.partial(combine, b))


def _combine(slot, x1, mod, final_g, y_sorted, *, final):
    if final:
        out_specs = [_XP_SPEC, _XS_SPEC]
        out_shape = [jax.ShapeDtypeStruct((T_PROMPT, D), F32), jax.ShapeDtypeStruct((T_SAMPLE, D), F32)]
    else:
        out_specs = [_TILE_SPEC]
        out_shape = [jax.ShapeDtypeStruct((T, D), F32)]
    return pl.pallas_call(
        functools.partial(_combine_kernel, final=final),
        grid=(N_TILES,),
        in_specs=[_smem_tile_spec(lambda i: i), _smem_tile_spec(lambda i: jnp.minimum(i + 1, N_TILES - 1)),
                  _TILE_SPEC, _MOD_SPEC, _ROW_SPEC, pl.BlockSpec(memory_space=pl.ANY)],
        out_specs=out_specs,
        scratch_shapes=[pltpu.VMEM((2, TM, D), F32), pltpu.SemaphoreType.DMA((2,))],
        out_shape=out_shape,
        compiler_params=_cparams(),
        name="moe_combine_final" if final else "moe_combine",
    )(slot, slot, x1, mod, final_g, y_sorted)


def _slot_kernel(off_ref, bucket_ref, rank_ref, slot_ref):
    bucket = bucket_ref[...]
    slot = rank_ref[...]
    for b in range(N_BUCKETS):
        slot = slot + jnp.where(bucket == b, off_ref[b], 0)
    slot_ref[...] = slot


def _slots(off, bucket, rank):
    whole = pl.BlockSpec(bucket.shape, lambda i, *_: (0, 0, 0))
    return pl.pallas_call(
        _slot_kernel,
        grid_spec=pltpu.PrefetchScalarGridSpec(
            num_scalar_prefetch=1, grid=(1,), in_specs=[whole, whole], out_specs=whole),
        out_shape=jax.ShapeDtypeStruct(bucket.shape, jnp.int32),
        compiler_params=_cparams(),
        name="moe_slots",
    )(off, bucket, rank)


def _bucket_plan(bucket, rank, counts):
    cnt = counts[:N_BUCKETS, 0].astype(jnp.int32)
    tiles = (cnt + TM_E - 1) // TM_E
    tile_end = jnp.cumsum(tiles)
    tile_start = tile_end - tiles
    slot = _slots((tile_start * TM_E).astype(jnp.int32), bucket, rank)
    j = jnp.arange(N_TILES_E, dtype=jnp.int32)
    n_live = tile_end[-1]
    j_live = jnp.minimum(j, n_live - 1)
    b = jnp.sum((tile_end[None, :] <= j_live[:, None]).astype(jnp.int32), axis=1)
    b = jnp.minimum(b, N_BUCKETS - 1)
    rows = jnp.clip(cnt[b] - (j - tile_start[b]) * TM_E, 0, TM_E)
    rows = jnp.where(j < n_live, rows, 0).astype(jnp.int32)
    grp = b // N_PAIRS
    pair = b % N_PAIRS
    ea = grp * EPG + jnp.asarray(_SLOT_A, jnp.int32)[pair]
    eb = grp * EPG + jnp.asarray(_SLOT_B, jnp.int32)[pair]
    a_higher = jnp.asarray(_SLOT_A_IS_HIGHER, jnp.int32)[pair]
    return slot, ea, eb, a_higher, rows


def _moe(a, x_parts, mod, norm_g2, w_out, router_wt, router_bt, w_gate, w_up, w_down, layer, final_g,
         *, final):
    x1, gates, bucket, rank, counts = _post_mixer(a, x_parts, mod, norm_g2, w_out, router_wt, router_bt)
    slot, ea, eb, a_higher, rows = _bucket_plan(bucket, rank, counts)
    partial = (rows < TM_E).astype(jnp.int32)
    xs = _dispatch(partial, slot, x1, mod, norm_g2, gates)
    ys = _experts(ea, eb, a_higher, rows, xs, w_gate, w_up, w_down, layer)
    return _combine(slot, x1, mod, final_g, ys, final=final)


def kernel(x_prompt, x_sample, state_hgrn, c, c_ctx, w_mod, b_mod, norm_mix_g, norm_ffn_g, norm_final_g,
           hgrn_w_in, hgrn_lb_logits, hgrn_norm_g, hgrn_w_out, conv_w_in, conv_w, conv_w_out,
           router_w, router_b, moe_w_gate, moe_w_up, moe_w_down):
    x_parts = (x_prompt.reshape(T_PROMPT, D), x_sample.reshape(T_SAMPLE, D))
    cond = jnp.concatenate([c_ctx[None], c, jnp.zeros((COND_ROWS - N_COND, D), F32)], axis=0)
    mods = _adaln(cond, w_mod, b_mod)[:, :N_COND].reshape(2, N_COND, 6, 1, D)
    member_major = lambda w: w.reshape(N_GROUPS, EPG, -1).transpose(1, 0, 2).reshape(N_EXPERTS, -1)
    rwt = member_major(router_w.T)
    rbt = member_major(router_b.reshape(N_EXPERTS, 1))
    final_g = norm_final_g.reshape(1, D)
    bf = lambda w: w.astype(BF16)

    q, v, ff, fb, gg = _hgrn_proj(x_parts, mods[0], norm_mix_g[0:1], bf(hgrn_w_in[0]))
    s0 = state_hgrn[:, 0]
    o_f, s_f = _gla_fwd(q, v, ff, hgrn_lb_logits, s0)
    a, s_b = _gla_bwd(q, v, fb, hgrn_lb_logits, s0, o_f, gg, hgrn_norm_g[0:1])
    (x,) = _moe(a, x_parts, mods[0], norm_ffn_g[0:1], bf(hgrn_w_out[0]), rwt, rbt,
                moe_w_gate, moe_w_up, moe_w_down, 0, final_g, final=False)

    a = _conv_mixer(x, mods[1], norm_mix_g[1:2], bf(conv_w_in[0]), conv_w[0])
    y_p, y_s = _moe(a, (x,), mods[1], norm_ffn_g[1:2], bf(conv_w_out[0]), rwt, rbt,
                    moe_w_gate, moe_w_up, moe_w_down, 1, final_g, final=True)

    new_state = jnp.stack([s_f, s_b], axis=1)[:, None]
    return (y_p.reshape(N_PROMPT_SEQ, PROMPT_LEN, D), y_s.reshape(N_SAMPLE_SEQ, SAMPLE_LEN, D), new_state)
```

```python
import functools

import jax
import jax.numpy as jnp
from jax import lax
from jax.experimental import pallas as pl
from jax.experimental.pallas import tpu as pltpu

F32 = jnp.float32
BF16 = jnp.bfloat16

D = 1024
N_PROMPT_SEQ = 16
PROMPT_LEN = 256
N_SAMPLE_SEQ = 2
SAMPLE_LEN = 4096
GRID_W = 64
T_PROMPT = N_PROMPT_SEQ * PROMPT_LEN
T_SAMPLE = N_SAMPLE_SEQ * SAMPLE_LEN
T = T_PROMPT + T_SAMPLE
N_COND = 1 + N_SAMPLE_SEQ
COND_ROWS = 8

HEADS = 8
DK = 128
DV = 128
CHUNK = 64
N_EXPERTS = 16
N_GROUPS = 4
EPG = 4
N_PAIRS = 6
N_BUCKETS = N_GROUPS * N_PAIRS
D_FF = 512
EPS = 1e-6

TM = 256
N_TILES = T // TM
PROMPT_TILES = T_PROMPT // TM
TILES_PER_SAMPLE = SAMPLE_LEN // TM
CHUNKS_PER_TILE = TM // CHUNK
TM_E = 256
N_TILES_E = T // TM_E + N_BUCKETS
S_ROWS = N_TILES_E * TM_E
XW = D + 128
LANES = 128
MOD_TN = 1536

VMEM_LIMIT = 56 * 1024 * 1024

_SLOT_A = (0, 3, 3, 0, 0, 1)
_SLOT_B = (1, 1, 2, 2, 3, 2)
_SLOT_A_IS_HIGHER = tuple(int(a > b) for a, b in zip(_SLOT_A, _SLOT_B))


def _cparams():
    return pltpu.CompilerParams(dimension_semantics=("arbitrary",), vmem_limit_bytes=VMEM_LIMIT)


def _cond_of_tile(t):
    return jnp.where(t < PROMPT_TILES, 0, 1 + (t - PROMPT_TILES) // TILES_PER_SAMPLE)


def _sample_of_tile(t):
    return jnp.clip((t - PROMPT_TILES) // TILES_PER_SAMPLE, 0, N_SAMPLE_SEQ - 1)


def _rmsnorm(x, g):
    ms = jnp.mean(x * x, axis=-1, keepdims=True)
    return (x * lax.rsqrt(ms + EPS)) * g


def _modulate(x, g, shift, scale):
    return _rmsnorm(x, g) * (1.0 + scale) + shift


def _silu(x):
    return x * jax.nn.sigmoid(x)


def _dot(a, b):
    return jnp.dot(a, b, preferred_element_type=F32)


def _dot_nt(a, b):
    return lax.dot_general(a, b, (((1,), (1,)), ((), ())), preferred_element_type=F32)


def _dot_tn(a, b):
    return lax.dot_general(a, b, (((0,), (0,)), ((), ())), preferred_element_type=F32)


def _split_bf16(x):
    hi = x.astype(BF16)
    lo = (x - hi.astype(F32)).astype(BF16)
    return hi, lo


def _mod_kernel(cond_ref, w_ref, b_ref, o_ref):
    s = _silu(cond_ref[...])
    o_ref[0] = _dot(s.astype(BF16), w_ref[0].astype(BF16)) + b_ref[0]


def _adaln(cond, w_mod, b_mod):
    depth = w_mod.shape[0]
    return pl.pallas_call(
        _mod_kernel,
        grid=(depth, 6 * D // MOD_TN),
        in_specs=[
            pl.BlockSpec((COND_ROWS, D), lambda i, j: (0, 0)),
            pl.BlockSpec((1, D, MOD_TN), lambda i, j: (i, 0, j)),
            pl.BlockSpec((1, 1, MOD_TN), lambda i, j: (i, 0, j)),
        ],
        out_specs=pl.BlockSpec((1, COND_ROWS, MOD_TN), lambda i, j: (i, 0, j)),
        out_shape=jax.ShapeDtypeStruct((depth, COND_ROWS, 6 * D), F32),
        compiler_params=pltpu.CompilerParams(
            dimension_semantics=("arbitrary", "arbitrary"), vmem_limit_bytes=VMEM_LIMIT),
        name="adaln",
    )(cond, w_mod, b_mod.reshape(depth, 1, 6 * D))


_MOD_SPEC = pl.BlockSpec((1, 6, 1, D), lambda i: (_cond_of_tile(i), 0, 0, 0))
_ROW_SPEC = pl.BlockSpec((1, D), lambda i: (0, 0))
_TILE_SPEC = pl.BlockSpec((TM, D), lambda i: (i, 0))


def _token_tile(x_refs, rows=slice(None), prompt_steps=PROMPT_TILES):
    if len(x_refs) == 1:
        return x_refs[0][rows]
    return jnp.where(pl.program_id(0) < prompt_steps, x_refs[0][rows], x_refs[1][rows])


def _token_specs(x_parts, tiles_per_step=1):
    rows = tiles_per_step * TM
    if len(x_parts) == 1:
        return [pl.BlockSpec((rows, D), lambda i: (i, 0))]
    prompt_steps = T_PROMPT // rows
    return [pl.BlockSpec((rows, D), lambda i: (jnp.minimum(i, prompt_steps - 1), 0)),
            pl.BlockSpec((rows, D), lambda i: (jnp.maximum(i - prompt_steps, 0), 0))]


def _hgrn_proj_kernel(xp_ref, xs_ref, mod_ref, g_ref, w_ref, q_ref, v_ref, ff_ref, fb_ref, gg_ref):
    x = _token_tile((xp_ref, xs_ref))
    h = _modulate(x, g_ref[...], mod_ref[0, 0], mod_ref[0, 1]).astype(BF16)
    q_ref[...] = (_dot(h, w_ref[:, 0 * D:1 * D]) * (DK ** -0.5)).astype(BF16)
    v_ref[...] = _dot(h, w_ref[:, 1 * D:2 * D]).astype(BF16)
    ff_ref[...] = _dot(h, w_ref[:, 2 * D:3 * D])
    fb_ref[...] = _dot(h, w_ref[:, 3 * D:4 * D])
    gg_ref[...] = _dot(h, w_ref[:, 4 * D:5 * D]).astype(BF16)


def _hgrn_proj(x_parts, mod, norm_g, w_in):
    bf = jax.ShapeDtypeStruct((T, D), BF16)
    f32 = jax.ShapeDtypeStruct((T, D), F32)
    return pl.pallas_call(
        _hgrn_proj_kernel,
        grid=(N_TILES,),
        in_specs=_token_specs(x_parts) + [_MOD_SPEC, _ROW_SPEC, pl.BlockSpec((D, 5 * D), lambda i: (0, 0))],
        out_specs=[_TILE_SPEC] * 5,
        out_shape=[bf, bf, f32, f32, bf],
        compiler_params=_cparams(),
        name="hgrn_proj",
    )(*x_parts, mod, norm_g, w_in)


def _gla_tile(t, q_ref, v_ref, f_ref, lbl_ref, s0_ref, st_ref, ma_ref, o_ref, *, reverse):
    is_prompt = t < PROMPT_TILES
    rel = t - PROMPT_TILES
    edge = (TILES_PER_SAMPLE - 1) if reverse else 0
    sample_start = jnp.logical_and(rel >= 0, rel % TILES_PER_SAMPLE == edge)

    @pl.when(is_prompt)
    def _():
        st_ref[...] = jnp.zeros_like(st_ref)

    @pl.when(sample_start)
    def _():
        for h in range(HEADS):
            st_ref[h] = s0_ref[0, 0, h].T

    logits = lbl_ref[...]
    e = jnp.exp(logits - jnp.max(logits, axis=0, keepdims=True))
    lb = e[0:1] / jnp.sum(e, axis=0, keepdims=True)

    ref_idx = (CHUNK - 1 - CHUNK // 2) if reverse else CHUNK // 2
    last_idx = 0 if reverse else CHUNK - 1
    shift = CHUNK.bit_length() - 1

    row = lax.broadcasted_iota(jnp.int32, (TM, TM), 0)
    col = lax.broadcasted_iota(jnp.int32, (TM, TM), 1)
    same = (row >> shift) == (col >> shift)
    ref_row = ((row >> shift) << shift) + ref_idx
    keep = jnp.logical_and(same, (row <= col) if reverse else (row >= col))

    @pl.when(pl.program_id(0) == 0)
    def _():
        at_ref = jnp.logical_and(same, (ref_row <= col) if reverse else (ref_row >= col))
        ma_ref[...] = (keep.astype(F32) - at_ref.astype(F32)).astype(BF16)

    ma = ma_ref[...]

    rr = lax.broadcasted_iota(jnp.int32, (2 * CHUNKS_PER_TILE, TM), 0)
    cc = lax.broadcasted_iota(jnp.int32, (2 * CHUNKS_PER_TILE, TM), 1)
    chunk = rr & (CHUNKS_PER_TILE - 1)
    target = chunk * CHUNK + jnp.where(rr < CHUNKS_PER_TILE, ref_idx, last_idx)
    covered = (cc >= target) if reverse else (cc <= target)
    rm = jnp.logical_and((cc >> shift) == chunk, covered).astype(BF16)

    chunk_rows = [slice(c * CHUNK, (c + 1) * CHUNK) for c in range(CHUNKS_PER_TILE)]
    order = range(CHUNKS_PER_TILE - 1, -1, -1) if reverse else range(CHUNKS_PER_TILE)
    pair_w = 2 * DK

    def per_chunk_scale(x, scale):
        return jnp.concatenate([x[rs] * scale[c:c + 1] for c, rs in enumerate(chunk_rows)], axis=0)

    def pair_cols(p):
        return slice(p * pair_w, (p + 1) * pair_w)

    def stage_decay(p):
        lbp = lb[:, pair_cols(p)]
        f = lbp + (1.0 - lbp) * jax.nn.sigmoid(f_ref[:, pair_cols(p)])
        lf_hi, lf_lo = _split_bf16(jnp.log(f))
        z = _dot(ma, lf_hi) + _dot(ma, lf_lo)
        marks = _dot(rm, lf_hi) + _dot(rm, lf_lo)
        return 1.0 - f, z, marks

    def stage_scores(p, k, z, marks):
        ref, last = marks[:CHUNKS_PER_TILE], marks[CHUNKS_PER_TILE:]
        qa32 = q_ref[:, pair_cols(p)].astype(F32) * jnp.exp(z)
        kb32 = k * jnp.exp(-z)
        qa = qa32.astype(BF16)
        kb = kb32.astype(BF16)
        qe = per_chunk_scale(qa32, jnp.exp(ref)).astype(BF16)
        kd = per_chunk_scale(kb32, jnp.exp(last - ref)).astype(BF16)
        v = v_ref[:, pair_cols(p)]
        scores, incr = [], []
        for hh in range(2):
            ls = slice(hh * DK, (hh + 1) * DK)
            scores.append(_dot_nt(qa[:, ls], kb[:, ls]))
            kd_h = kd[:, ls]
            kd_blocks = jnp.concatenate(
                [jnp.concatenate([part for part in (jnp.zeros((c * CHUNK, DK), BF16), kd_h[rs],
                                                    jnp.zeros((TM - (c + 1) * CHUNK, DK), BF16))
                                  if part.shape[0]], axis=0)
                 for c, rs in enumerate(chunk_rows)], axis=1)
            incr.append(_dot_tn(v[:, ls], kd_blocks))
        return scores, incr, qe, jnp.exp(last), v

    def stage_output(p, scores, incr, qe, decay, v):
        for hh in range(2):
            h = 2 * p + hh
            ls = slice(hh * DK, (hh + 1) * DK)
            hs = slice(h * DK, (h + 1) * DK)
            o_intra = _dot(jnp.where(keep, scores[hh], 0.0).astype(BF16), v[:, ls])
            st = st_ref[h]
            for c in order:
                rs = chunk_rows[c]
                o_ref[rs, hs] = o_intra[rs] + _dot_nt(qe[rs, ls], st.astype(BF16))
                st = st * decay[c:c + 1, ls] + incr[hh][:, c * DK:(c + 1) * DK]
            st_ref[h] = st

    n_pairs = HEADS // 2
    decayed, scored = {}, {}
    for step in range(n_pairs + 2):
        if step < n_pairs:
            decayed[step] = stage_decay(step)
        if 0 <= step - 1 < n_pairs:
            scored[step - 1] = stage_scores(step - 1, *decayed.pop(step - 1))
        if 0 <= step - 2 < n_pairs:
            stage_output(step - 2, *scored.pop(step - 2))


def _store_prompt_state(t, st_ref, snew_ref):
    @pl.when(t < PROMPT_TILES)
    def _():
        for h in range(HEADS):
            snew_ref[0, h] = st_ref[h].T


def _gla_fwd_kernel(q_ref, v_ref, f_ref, lbl_ref, s0_ref, o_ref, snew_ref, st_ref, ma_ref):
    t = pl.program_id(0)
    _gla_tile(t, q_ref, v_ref, f_ref, lbl_ref, s0_ref, st_ref, ma_ref, o_ref, reverse=False)
    _store_prompt_state(t, st_ref, snew_ref)


def _gla_bwd_kernel(q_ref, v_ref, f_ref, lbl_ref, s0_ref, of_ref, gg_ref, ng_ref,
                    a_ref, snew_ref, st_ref, ma_ref, ob_ref):
    t = N_TILES - 1 - pl.program_id(0)
    _gla_tile(t, q_ref, v_ref, f_ref, lbl_ref, s0_ref, st_ref, ma_ref, ob_ref, reverse=True)
    _store_prompt_state(t, st_ref, snew_ref)
    ng = ng_ref[...]
    for h in range(HEADS):
        hs = slice(h * DV, (h + 1) * DV)
        o = of_ref[:, hs] + ob_ref[:, hs]
        a_ref[:, hs] = (_rmsnorm(o, ng) * _silu(gg_ref[:, hs].astype(F32))).astype(BF16)


def _state_specs(direction, tile_of_step):
    s0 = pl.BlockSpec((1, 1, HEADS, DK, DV),
                      lambda i: (_sample_of_tile(tile_of_step(i)), direction, 0, 0, 0))
    snew = pl.BlockSpec((1, HEADS, DK, DV),
                        lambda i: (jnp.minimum(tile_of_step(i), PROMPT_TILES - 1), 0, 0, 0))
    return s0, snew


def _gla_fwd(q, v, ff, lb_logits, s0):
    s0_spec, snew_spec = _state_specs(0, lambda i: i)
    return pl.pallas_call(
        _gla_fwd_kernel,
        grid=(N_TILES,),
        in_specs=[_TILE_SPEC, _TILE_SPEC, _TILE_SPEC,
                  pl.BlockSpec(lb_logits.shape, lambda i: (0, 0)), s0_spec],
        out_specs=[_TILE_SPEC, snew_spec],
        out_shape=[jax.ShapeDtypeStruct((T, D), F32),
                   jax.ShapeDtypeStruct((N_PROMPT_SEQ, HEADS, DK, DV), F32)],
        scratch_shapes=[pltpu.VMEM((HEADS, DV, DK), F32), pltpu.VMEM((TM, TM), BF16)],
        compiler_params=_cparams(),
        name="gla_fwd",
    )(q, v, ff, lb_logits, s0)


def _gla_bwd(q, v, fb, lb_logits, s0, o_f, gg, head_norm_g):
    rev = lambda i: N_TILES - 1 - i
    tile = pl.BlockSpec((TM, D), lambda i: (rev(i), 0))
    s0_spec, snew_spec = _state_specs(1, rev)
    return pl.pallas_call(
        _gla_bwd_kernel,
        grid=(N_TILES,),
        in_specs=[tile, tile, tile, pl.BlockSpec(lb_logits.shape, lambda i: (0, 0)), s0_spec,
                  tile, tile, pl.BlockSpec((1, DV), lambda i: (0, 0))],
        out_specs=[tile, snew_spec],
        out_shape=[jax.ShapeDtypeStruct((T, D), BF16),
                   jax.ShapeDtypeStruct((N_PROMPT_SEQ, HEADS, DK, DV), F32)],
        scratch_shapes=[pltpu.VMEM((HEADS, DV, DK), F32), pltpu.VMEM((TM, TM), BF16),
                        pltpu.VMEM((TM, D), F32)],
        compiler_params=_cparams(),
        name="gla_bwd",
    )(q, v, fb, lb_logits, s0, o_f, gg, head_norm_g)


def _conv_mixer_kernel(x_ref, mod_ref, g_ref, w_ref, cw_ref, a_ref):
    i = pl.program_id(0)
    h = _modulate(x_ref[...], g_ref[...], mod_ref[0, 0], mod_ref[0, 1]).astype(BF16)
    bg = _dot(h, w_ref[:, 0 * D:1 * D])
    u = _dot(h, w_ref[:, 1 * D:2 * D]) * _dot(h, w_ref[:, 2 * D:3 * D])
    seg = jnp.where(i < PROMPT_TILES, PROMPT_LEN, GRID_W)
    pos = lax.broadcasted_iota(jnp.int32, (TM, 1), 0) & (seg - 1)
    prev = jnp.where(pos == 0, 0.0, pltpu.roll(u, 1, axis=0))
    nxt = jnp.where(pos == seg - 1, 0.0, pltpu.roll(u, TM - 1, axis=0))
    conv = cw_ref[0:1] * prev + cw_ref[1:2] * u + cw_ref[2:3] * nxt
    a_ref[...] = (bg * conv).astype(BF16)


def _conv_mixer(x, mod, norm_g, w_in, conv_w):
    return pl.pallas_call(
        _conv_mixer_kernel,
        grid=(N_TILES,),
        in_specs=[_TILE_SPEC, _MOD_SPEC, _ROW_SPEC, pl.BlockSpec((D, 3 * D), lambda i: (0, 0)),
                  pl.BlockSpec((3, D), lambda i: (0, 0))],
        out_specs=_TILE_SPEC,
        out_shape=jax.ShapeDtypeStruct((T, D), BF16),
        compiler_params=_cparams(),
        name="conv_mixer",
    )(x, mod, norm_g, w_in, conv_w)


ROUTE_ROWS = 32
ROUTE_TILES = 2


def _first_member(vals, target):
    idx = jnp.full_like(target, float(EPG - 1))
    for j in range(EPG - 2, -1, -1):
        idx = jnp.where(vals[j] == target, float(j), idx)
    return idx


def _router_logits(h2, rwt_ref):
    h_hi, h_lo = _split_bf16(h2)
    w_hi, w_lo = _split_bf16(rwt_ref[...])
    return _dot_nt(w_hi, h_hi) + (_dot_nt(w_hi, h_lo) + _dot_nt(w_lo, h_hi))


def _select_experts(logits, rb_ref):
    scores = jax.nn.sigmoid(logits)
    sel = scores + rb_ref[...]
    member = [sel[j * N_GROUPS:(j + 1) * N_GROUPS] for j in range(EPG)]
    m1 = functools.reduce(jnp.maximum, member)
    i1 = _first_member(member, m1)
    rest = [jnp.where(i1 == float(j), -jnp.inf, member[j]) for j in range(EPG)]
    m2 = functools.reduce(jnp.maximum, rest)
    i2 = _first_member(rest, m2)
    group_score = m1 + m2

    best, grp, first, second = group_score[0:1], jnp.zeros((1, TM), F32), i1[0:1], i2[0:1]
    for g in range(1, N_GROUPS):
        upd = group_score[g:g + 1] > best
        best = jnp.where(upd, group_score[g:g + 1], best)
        grp = jnp.where(upd, float(g), grp)
        first = jnp.where(upd, i1[g:g + 1], first)
        second = jnp.where(upd, i2[g:g + 1], second)
    a = jnp.minimum(first, second)
    b = jnp.maximum(first, second)
    row = lax.broadcasted_iota(jnp.int32, (N_EXPERTS, TM), 0).astype(F32)
    s_lo = jnp.sum(jnp.where(row == a * N_GROUPS + grp, scores, 0.0), axis=0, keepdims=True)
    s_hi = jnp.sum(jnp.where(row == b * N_GROUPS + grp, scores, 0.0), axis=0, keepdims=True)
    tot = s_lo + s_hi
    pair = jnp.where(a == 0.0, jnp.where(b == 1.0, 0.0, jnp.where(b == 2.0, 3.0, 4.0)),
                     jnp.where(a == 1.0, jnp.where(b == 2.0, 5.0, 1.0), 2.0))
    return grp * N_PAIRS + pair, s_lo / tot, s_hi / tot


def _post_mixer_kernel(*refs, n_x):
    a_ref, x_refs = refs[0], refs[1:1 + n_x]
    (mod_ref, g2_ref, wo_ref, rwt_ref, rb_ref,
     x1_ref, gates_ref, bucket_ref, rank_ref, cnt_ref, carry_ref, earlier_ref) = refs[1 + n_x:]

    @pl.when(pl.program_id(0) == 0)
    def _():
        carry_ref[...] = jnp.zeros_like(carry_ref)
        s_idx = lax.broadcasted_iota(jnp.int32, (TM, TM), 0)
        t_idx = lax.broadcasted_iota(jnp.int32, (TM, TM), 1)
        earlier_ref[...] = (s_idx < t_idx).astype(BF16)

    def stage_mix(s):
        rows = slice(s * TM, (s + 1) * TM)
        x1 = (_token_tile(x_refs, rows, PROMPT_TILES // ROUTE_TILES)
              + mod_ref[0, 2] * _dot(a_ref[rows], wo_ref[...]))
        x1_ref[rows] = x1
        return _modulate(x1, g2_ref[...], mod_ref[0, 3], mod_ref[0, 4])

    def stage_rank(s, bucket, g_lo, g_hi):
        onehot = lax.broadcasted_iota(jnp.int32, (ROUTE_ROWS, TM), 0).astype(F32) == bucket
        before = _dot(onehot.astype(BF16), earlier_ref[...])
        oh = onehot.astype(F32)
        carry = carry_ref[...]
        rank = jnp.sum(oh * (before + carry[:, 0:1]), axis=0, keepdims=True)
        carry_ref[...] = carry + jnp.sum(oh, axis=1, keepdims=True)
        bucket_ref[s] = bucket.astype(jnp.int32)
        rank_ref[s] = rank.astype(jnp.int32)
        grow = lax.broadcasted_iota(jnp.int32, (LANES, TM), 0)
        gates_ref[s * TM:(s + 1) * TM] = jnp.where(grow == 0, g_lo, jnp.where(grow == 1, g_hi, 0.0)).T

    tiles = range(ROUTE_TILES)
    h2 = [stage_mix(s) for s in tiles]
    logits = [_router_logits(h2[s], rwt_ref) for s in tiles]
    picks = [_select_experts(logits[s], rb_ref) for s in tiles]
    for s in tiles:
        stage_rank(s, *picks[s])
    cnt_ref[...] = carry_ref[...]


def _post_mixer(a, x_parts, mod, norm_g2, w_out, router_wt, router_bt):
    rows = ROUTE_TILES * TM
    idx_spec = pl.BlockSpec((ROUTE_TILES, 1, TM), lambda i: (i, 0, 0))
    return pl.pallas_call(
        functools.partial(_post_mixer_kernel, n_x=len(x_parts)),
        grid=(N_TILES // ROUTE_TILES,),
        in_specs=[pl.BlockSpec((rows, D), lambda i: (i, 0))] + _token_specs(x_parts, ROUTE_TILES) + [
            pl.BlockSpec((1, 6, 1, D), lambda i: (_cond_of_tile(i * ROUTE_TILES), 0, 0, 0)), _ROW_SPEC,
            pl.BlockSpec((D, D), lambda i: (0, 0)),
            pl.BlockSpec((N_EXPERTS, D), lambda i: (0, 0)),
            pl.BlockSpec((N_EXPERTS, 1), lambda i: (0, 0))],
        out_specs=[pl.BlockSpec((rows, D), lambda i: (i, 0)), pl.BlockSpec((rows, LANES), lambda i: (i, 0)),
                   idx_spec, idx_spec, pl.BlockSpec((ROUTE_ROWS, LANES), lambda i: (0, 0))],
        out_shape=[jax.ShapeDtypeStruct((T, D), F32), jax.ShapeDtypeStruct((T, LANES), F32),
                   jax.ShapeDtypeStruct((N_TILES, 1, TM), jnp.int32),
                   jax.ShapeDtypeStruct((N_TILES, 1, TM), jnp.int32),
                   jax.ShapeDtypeStruct((ROUTE_ROWS, LANES), F32)],
        scratch_shapes=[pltpu.VMEM((ROUTE_ROWS, LANES), F32), pltpu.VMEM((TM, TM), BF16)],
        compiler_params=_cparams(),
        name="post_mixer",
    )(a, *x_parts, mod, norm_g2, w_out, router_wt, router_bt)


def _smem_tile_spec(tile_of_step):
    return pl.BlockSpec((1, 1, TM), lambda i, *_: (tile_of_step(i), 0, 0), memory_space=pltpu.SMEM)


def _start_rows(n, make_copy):
    for r in range(n):
        make_copy(r).start()


def _dispatch_kernel(partial_ref, slot_ref, x1_ref, mod_ref, g2_ref, gates_ref,
                     xs_ref, buf_ref, zero_ref, sems, zero_sem):
    i = pl.program_id(0)
    cur = i % 2

    @pl.when(i == 0)
    def _():
        zero_ref[...] = jnp.zeros_like(zero_ref)

        def zero_copy(j):
            return pltpu.make_async_copy(zero_ref, xs_ref.at[pl.ds(j * TM_E, TM_E)], zero_sem)

        def start(j, carry):
            @pl.when(partial_ref[j] != 0)
            def _():
                zero_copy(j).start()
            return carry

        def wait(j, carry):
            @pl.when(partial_ref[j] != 0)
            def _():
                zero_copy(j).wait()
            return carry

        lax.fori_loop(0, N_TILES_E, start, 0)
        lax.fori_loop(0, N_TILES_E, wait, 0)

    def wait_tile(b):
        pltpu.make_async_copy(buf_ref.at[b], xs_ref.at[pl.ds(0, TM)], sems.at[b]).wait()

    h2 = _modulate(x1_ref[...], g2_ref[...], mod_ref[0, 3], mod_ref[0, 4])
    gates = gates_ref[...]

    def send(b):
        buf_ref[b, :, :D] = h2
        buf_ref[b, :, D:] = gates
        _start_rows(TM, lambda r: pltpu.make_async_copy(
            buf_ref.at[b, pl.ds(r, 1)], xs_ref.at[pl.ds(slot_ref[0, 0, r], 1)], sems.at[b]))

        @pl.when(i > 0)
        def _():
            wait_tile(1 - b)

        @pl.when(i == N_TILES - 1)
        def _():
            wait_tile(b)

    for b in range(2):
        pl.when(cur == b)(functools.partial(send, b))


def _dispatch(partial, slot, x1, mod, norm_g2, gates):
    tile = lambda w: pl.BlockSpec((TM, w), lambda i, *_: (i, 0))
    return pl.pallas_call(
        _dispatch_kernel,
        grid_spec=pltpu.PrefetchScalarGridSpec(
            num_scalar_prefetch=1,
            grid=(N_TILES,),
            in_specs=[_smem_tile_spec(lambda i: i), tile(D),
                      pl.BlockSpec((1, 6, 1, D), lambda i, *_: (_cond_of_tile(i), 0, 0, 0)),
                      pl.BlockSpec((1, D), lambda i, *_: (0, 0)), tile(LANES)],
            out_specs=pl.BlockSpec(memory_space=pl.ANY),
            scratch_shapes=[pltpu.VMEM((2, TM, XW), F32), pltpu.VMEM((TM_E, XW), F32),
                            pltpu.SemaphoreType.DMA((2,)), pltpu.SemaphoreType.DMA(())],
        ),
        out_shape=jax.ShapeDtypeStruct((S_ROWS, XW), F32),
        compiler_params=_cparams(),
        name="moe_dispatch",
    )(partial, slot, x1, mod, norm_g2, gates)


def _expert_kernel(ea_ref, eb_ref, a_higher_ref, nrows_ref, x_ref, *refs):
    w32 = (refs[0:3], refs[3:6])
    y_ref = refs[6]
    w16 = (refs[7:10], refs[10:13])
    j = pl.program_id(0)
    n = nrows_ref[j]
    prev = jnp.maximum(j - 1, 0)

    for e_ref, src, dst in ((ea_ref, w32[0], w16[0]), (eb_ref, w32[1], w16[1])):
        @pl.when(jnp.logical_or(j == 0, e_ref[j] != e_ref[prev]))
        def _(src=src, dst=dst):
            for s, d in zip(src, dst):
                d[...] = s[0, 0].astype(BF16)

    @pl.when(n > 0)
    def _():
        valid = lax.broadcasted_iota(jnp.int32, (TM_E, 1), 0) < n
        xe = jnp.where(valid, x_ref[...], 0.0)
        x = xe[:, :D].astype(BF16)
        g_lo, g_hi = xe[:, D:D + 1], xe[:, D + 1:D + 2]
        a_higher = a_higher_ref[j] != 0

        def ffn(w, gate):
            wg_ref, wu_ref, wd_ref = w
            act = _silu(_dot(x, wg_ref[...])) * _dot(x, wu_ref[...]) * gate
            return _dot(act.astype(BF16), wd_ref[...])

        y_ref[...] = (ffn(w16[0], jnp.where(a_higher, g_hi, g_lo))
                      + ffn(w16[1], jnp.where(a_higher, g_lo, g_hi)))

    @pl.when(n == 0)
    def _():
        y_ref[...] = jnp.zeros_like(y_ref)


def _experts(tile_ea, tile_eb, tile_a_higher, tile_rows, xs, w_gate, w_up, w_down, layer):
    wspec = lambda shape, sel: pl.BlockSpec((1, 1) + shape,
                                            lambda j, ea, eb, *_: (layer, sel(ea, eb)[j], 0, 0))
    first = lambda ea, eb: ea
    second = lambda ea, eb: eb
    w_shapes = ((D, D_FF), (D, D_FF), (D_FF, D))
    return pl.pallas_call(
        _expert_kernel,
        grid_spec=pltpu.PrefetchScalarGridSpec(
            num_scalar_prefetch=4,
            grid=(N_TILES_E,),
            in_specs=[pl.BlockSpec((TM_E, XW), lambda j, *_: (j, 0))]
                     + [wspec(s, first) for s in w_shapes] + [wspec(s, second) for s in w_shapes],
            out_specs=pl.BlockSpec((TM_E, D), lambda j, *_: (j, 0)),
            scratch_shapes=[pltpu.VMEM(s, BF16) for s in w_shapes * 2],
        ),
        out_shape=jax.ShapeDtypeStruct((S_ROWS, D), F32),
        compiler_params=_cparams(),
        name="moe_experts",
    )(tile_ea, tile_eb, tile_a_higher, tile_rows, xs, w_gate, w_up, w_down, w_gate, w_up, w_down)


def _combine_kernel(slot_ref, next_slot_ref, x1_ref, mod_ref, gf_ref, y_ref, *rest, final):
    out_refs, (buf_ref, sems) = rest[:-2], rest[-2:]
    i = pl.program_id(0)
    cur = i % 2

    def gather(slots, b):
        _start_rows(TM, lambda r: pltpu.make_async_copy(
            y_ref.at[pl.ds(slots[0, 0, r], 1)], buf_ref.at[b, pl.ds(r, 1)], sems.at[b]))

    @pl.when(i == 0)
    def _():
        gather(slot_ref, 0)

    def combine(b):
        @pl.when(i + 1 < N_TILES)
        def _():
            gather(next_slot_ref, 1 - b)

        pltpu.make_async_copy(y_ref.at[pl.ds(0, TM)], buf_ref.at[b], sems.at[b]).wait()
        x2 = x1_ref[...] + mod_ref[0, 5] * buf_ref[b]
        if not final:
            out_refs[0][...] = x2
            return
        y = _rmsnorm(x2, gf_ref[...])
        yp_ref, ys_ref = out_refs

        @pl.when(i < PROMPT_TILES)
        def _():
            yp_ref[...] = y

        @pl.when(i >= PROMPT_TILES)
        def _():
            ys_ref[...] = y

    for b in range(2):
        pl.when(cur == b)(functools.partial(combine, b))


def _combine(slot, x1, mod, final_g, y_sorted, *, final):
    if final:
        out_specs = _token_specs((None, None))
        out_shape = [jax.ShapeDtypeStruct((T_PROMPT, D), F32), jax.ShapeDtypeStruct((T_SAMPLE, D), F32)]
    else:
        out_specs = [_TILE_SPEC]
        out_shape = [jax.ShapeDtypeStruct((T, D), F32)]
    return pl.pallas_call(
        functools.partial(_combine_kernel, final=final),
        grid=(N_TILES,),
        in_specs=[_smem_tile_spec(lambda i: i), _smem_tile_spec(lambda i: jnp.minimum(i + 1, N_TILES - 1)),
                  _TILE_SPEC, _MOD_SPEC, _ROW_SPEC, pl.BlockSpec(memory_space=pl.ANY)],
        out_specs=out_specs,
        scratch_shapes=[pltpu.VMEM((2, TM, D), F32), pltpu.SemaphoreType.DMA((2,))],
        out_shape=out_shape,
        compiler_params=_cparams(),
        name="moe_combine_final" if final else "moe_combine",
    )(slot, slot, x1, mod, final_g, y_sorted)


def _slot_kernel(off_ref, bucket_ref, rank_ref, slot_ref):
    bucket = bucket_ref[...]
    slot = rank_ref[...]
    for b in range(N_BUCKETS):
        slot = slot + jnp.where(bucket == b, off_ref[b], 0)
    slot_ref[...] = slot


def _slots(off, bucket, rank):
    whole = pl.BlockSpec(bucket.shape, lambda i, *_: (0, 0, 0))
    return pl.pallas_call(
        _slot_kernel,
        grid_spec=pltpu.PrefetchScalarGridSpec(
            num_scalar_prefetch=1, grid=(1,), in_specs=[whole, whole], out_specs=whole),
        out_shape=jax.ShapeDtypeStruct(bucket.shape, jnp.int32),
        compiler_params=_cparams(),
        name="moe_slots",
    )(off, bucket, rank)


def _bucket_plan(bucket, rank, counts):
    cnt = counts[:N_BUCKETS, 0].astype(jnp.int32)
    tiles = (cnt + TM_E - 1) // TM_E
    tile_end = jnp.cumsum(tiles)
    tile_start = tile_end - tiles
    slot = _slots((tile_start * TM_E).astype(jnp.int32), bucket, rank)
    j = jnp.arange(N_TILES_E, dtype=jnp.int32)
    n_live = tile_end[-1]
    j_live = jnp.minimum(j, n_live - 1)
    b = jnp.sum((tile_end[None, :] <= j_live[:, None]).astype(jnp.int32), axis=1)
    b = jnp.minimum(b, N_BUCKETS - 1)
    rows = jnp.clip(cnt[b] - (j - tile_start[b]) * TM_E, 0, TM_E)
    rows = jnp.where(j < n_live, rows, 0).astype(jnp.int32)
    grp = b // N_PAIRS
    pair = b % N_PAIRS
    ea = grp * EPG + jnp.asarray(_SLOT_A, jnp.int32)[pair]
    eb = grp * EPG + jnp.asarray(_SLOT_B, jnp.int32)[pair]
    a_higher = jnp.asarray(_SLOT_A_IS_HIGHER, jnp.int32)[pair]
    return slot, ea, eb, a_higher, rows


def _moe(a, x_parts, mod, norm_g2, w_out, router_wt, router_bt, w_gate, w_up, w_down, layer, final_g,
         *, final):
    x1, gates, bucket, rank, counts = _post_mixer(a, x_parts, mod, norm_g2, w_out, router_wt, router_bt)
    slot, ea, eb, a_higher, rows = _bucket_plan(bucket, rank, counts)
    partial = (rows < TM_E).astype(jnp.int32)
    xs = _dispatch(partial, slot, x1, mod, norm_g2, gates)
    ys = _experts(ea, eb, a_higher, rows, xs, w_gate, w_up, w_down, layer)
    return _combine(slot, x1, mod, final_g, ys, final=final)


def kernel(x_prompt, x_sample, state_hgrn, c, c_ctx, w_mod, b_mod, norm_mix_g, norm_ffn_g, norm_final_g,
           hgrn_w_in, hgrn_lb_logits, hgrn_norm_g, hgrn_w_out, conv_w_in, conv_w, conv_w_out,
           router_w, router_b, moe_w_gate, moe_w_up, moe_w_down):
    x_parts = (x_prompt.reshape(T_PROMPT, D), x_sample.reshape(T_SAMPLE, D))
    cond = jnp.concatenate([c_ctx[None], c, jnp.zeros((COND_ROWS - N_COND, D), F32)], axis=0)
    mods = _adaln(cond, w_mod, b_mod)[:, :N_COND].reshape(2, N_COND, 6, 1, D)
    member_major = lambda w: w.reshape(N_GROUPS, EPG, -1).transpose(1, 0, 2).reshape(N_EXPERTS, -1)
    rwt = member_major(router_w.T)
    rbt = member_major(router_b.reshape(N_EXPERTS, 1))
    final_g = norm_final_g.reshape(1, D)
    bf = lambda w: w.astype(BF16)

    q, v, ff, fb, gg = _hgrn_proj(x_parts, mods[0], norm_mix_g[0:1], bf(hgrn_w_in[0]))
    s0 = state_hgrn[:, 0]
    o_f, s_f = _gla_fwd(q, v, ff, hgrn_lb_logits, s0)
    a, s_b = _gla_bwd(q, v, fb, hgrn_lb_logits, s0, o_f, gg, hgrn_norm_g[0:1])
    (x,) = _moe(a, x_parts, mods[0], norm_ffn_g[0:1], bf(hgrn_w_out[0]), rwt, rbt,
                moe_w_gate, moe_w_up, moe_w_down, 0, final_g, final=False)

    a = _conv_mixer(x, mods[1], norm_mix_g[1:2], bf(conv_w_in[0]), conv_w[0])
    y_p, y_s = _moe(a, (x,), mods[1], norm_ffn_g[1:2], bf(conv_w_out[0]), rwt, rbt,
                    moe_w_gate, moe_w_up, moe_w_down, 1, final_g, final=True)

    new_state = jnp.stack([s_f, s_b], axis=1)[:, None]
    return (y_p.reshape(N_PROMPT_SEQ, PROMPT_LEN, D), y_s.reshape(N_SAMPLE_SEQ, SAMPLE_LEN, D), new_state)
```

```python
import functools

import jax
import jax.numpy as jnp
from jax import lax
from jax.experimental import pallas as pl
from jax.experimental.pallas import tpu as pltpu

F32 = jnp.float32
BF16 = jnp.bfloat16

D = 1024
N_PROMPT_SEQ = 16
PROMPT_LEN = 256
N_SAMPLE_SEQ = 2
SAMPLE_LEN = 4096
GRID_W = 64
T_PROMPT = N_PROMPT_SEQ * PROMPT_LEN
T_SAMPLE = N_SAMPLE_SEQ * SAMPLE_LEN
T = T_PROMPT + T_SAMPLE
N_COND = 1 + N_SAMPLE_SEQ
COND_ROWS = 8

HEADS = 8
DK = 128
DV = 128
CHUNK = 64
N_EXPERTS = 16
N_GROUPS = 4
EPG = 4
N_PAIRS = 6
N_BUCKETS = N_GROUPS * N_PAIRS
D_FF = 512
EPS = 1e-6

TM = 256
N_TILES = T // TM
PROMPT_TILES = T_PROMPT // TM
TILES_PER_SAMPLE = SAMPLE_LEN // TM
CHUNKS_PER_TILE = TM // CHUNK
TM_E = 256
N_TILES_E = T // TM_E + N_BUCKETS
S_ROWS = N_TILES_E * TM_E
XW = D + 128
LANES = 128
MOD_TN = 1536

VMEM_LIMIT = 56 * 1024 * 1024

_SLOT_A = (0, 3, 3, 0, 0, 1)
_SLOT_B = (1, 1, 2, 2, 3, 2)
_SLOT_A_IS_HIGHER = tuple(int(a > b) for a, b in zip(_SLOT_A, _SLOT_B))


def _cparams():
    return pltpu.CompilerParams(dimension_semantics=("arbitrary",), vmem_limit_bytes=VMEM_LIMIT)


def _cond_of_tile(t):
    return jnp.where(t < PROMPT_TILES, 0, 1 + (t - PROMPT_TILES) // TILES_PER_SAMPLE)


def _sample_of_tile(t):
    return jnp.clip((t - PROMPT_TILES) // TILES_PER_SAMPLE, 0, N_SAMPLE_SEQ - 1)


def _rmsnorm(x, g):
    ms = jnp.mean(x * x, axis=-1, keepdims=True)
    return (x * lax.rsqrt(ms + EPS)) * g


def _modulate(x, g, shift, scale):
    return _rmsnorm(x, g) * (1.0 + scale) + shift


def _silu(x):
    return x * jax.nn.sigmoid(x)


def _dot(a, b):
    return jnp.dot(a, b, preferred_element_type=F32)


def _dot_nt(a, b):
    return lax.dot_general(a, b, (((1,), (1,)), ((), ())), preferred_element_type=F32)


def _dot_tn(a, b):
    return lax.dot_general(a, b, (((0,), (0,)), ((), ())), preferred_element_type=F32)


def _split_bf16(x):
    hi = x.astype(BF16)
    lo = (x - hi.astype(F32)).astype(BF16)
    return hi, lo


def _mod_kernel(cond_ref, w_ref, b_ref, o_ref):
    s = _silu(cond_ref[...])
    o_ref[0] = _dot(s.astype(BF16), w_ref[0].astype(BF16)) + b_ref[0]


def _adaln(cond, w_mod, b_mod):
    depth = w_mod.shape[0]
    return pl.pallas_call(
        _mod_kernel,
        grid=(depth, 6 * D // MOD_TN),
        in_specs=[
            pl.BlockSpec((COND_ROWS, D), lambda i, j: (0, 0)),
            pl.BlockSpec((1, D, MOD_TN), lambda i, j: (i, 0, j)),
            pl.BlockSpec((1, 1, MOD_TN), lambda i, j: (i, 0, j)),
        ],
        out_specs=pl.BlockSpec((1, COND_ROWS, MOD_TN), lambda i, j: (i, 0, j)),
        out_shape=jax.ShapeDtypeStruct((depth, COND_ROWS, 6 * D), F32),
        compiler_params=pltpu.CompilerParams(
            dimension_semantics=("arbitrary", "arbitrary"), vmem_limit_bytes=VMEM_LIMIT),
        name="adaln",
    )(cond, w_mod, b_mod.reshape(depth, 1, 6 * D))


_MOD_SPEC = pl.BlockSpec((1, 6, 1, D), lambda i: (_cond_of_tile(i), 0, 0, 0))
_ROW_SPEC = pl.BlockSpec((1, D), lambda i: (0, 0))
_TILE_SPEC = pl.BlockSpec((TM, D), lambda i: (i, 0))


def _token_tile(x_refs, rows=slice(None), prompt_steps=PROMPT_TILES):
    if len(x_refs) == 1:
        return x_refs[0][rows]
    return jnp.where(pl.program_id(0) < prompt_steps, x_refs[0][rows], x_refs[1][rows])


def _token_specs(x_parts, tiles_per_step=1):
    rows = tiles_per_step * TM
    if len(x_parts) == 1:
        return [pl.BlockSpec((rows, D), lambda i: (i, 0))]
    prompt_steps = T_PROMPT // rows
    return [pl.BlockSpec((rows, D), lambda i: (jnp.minimum(i, prompt_steps - 1), 0)),
            pl.BlockSpec((rows, D), lambda i: (jnp.maximum(i - prompt_steps, 0), 0))]


def _hgrn_proj_kernel(xp_ref, xs_ref, mod_ref, g_ref, w_ref, q_ref, v_ref, ff_ref, fb_ref, gg_ref):
    x = _token_tile((xp_ref, xs_ref))
    h = _modulate(x, g_ref[...], mod_ref[0, 0], mod_ref[0, 1]).astype(BF16)
    q_ref[...] = (_dot(h, w_ref[:, 0 * D:1 * D]) * (DK ** -0.5)).astype(BF16)
    v_ref[...] = _dot(h, w_ref[:, 1 * D:2 * D]).astype(BF16)
    ff_ref[...] = _dot(h, w_ref[:, 2 * D:3 * D])
    fb_ref[...] = _dot(h, w_ref[:, 3 * D:4 * D])
    gg_ref[...] = _dot(h, w_ref[:, 4 * D:5 * D]).astype(BF16)


def _hgrn_proj(x_parts, mod, norm_g, w_in):
    bf = jax.ShapeDtypeStruct((T, D), BF16)
    f32 = jax.ShapeDtypeStruct((T, D), F32)
    return pl.pallas_call(
        _hgrn_proj_kernel,
        grid=(N_TILES,),
        in_specs=_token_specs(x_parts) + [_MOD_SPEC, _ROW_SPEC, pl.BlockSpec((D, 5 * D), lambda i: (0, 0))],
        out_specs=[_TILE_SPEC] * 5,
        out_shape=[bf, bf, f32, f32, bf],
        compiler_params=_cparams(),
        name="hgrn_proj",
    )(*x_parts, mod, norm_g, w_in)


def _gla_tile(t, q_ref, v_ref, f_ref, lbl_ref, s0_ref, st_ref, ma_ref, o_ref, *, reverse):
    is_prompt = t < PROMPT_TILES
    rel = t - PROMPT_TILES
    edge = (TILES_PER_SAMPLE - 1) if reverse else 0
    sample_start = jnp.logical_and(rel >= 0, rel % TILES_PER_SAMPLE == edge)

    @pl.when(is_prompt)
    def _():
        st_ref[...] = jnp.zeros_like(st_ref)

    @pl.when(sample_start)
    def _():
        for h in range(HEADS):
            st_ref[h] = s0_ref[0, 0, h].T

    logits = lbl_ref[...]
    e = jnp.exp(logits - jnp.max(logits, axis=0, keepdims=True))
    lb = e[0:1] / jnp.sum(e, axis=0, keepdims=True)

    ref_idx = (CHUNK - 1 - CHUNK // 2) if reverse else CHUNK // 2
    last_idx = 0 if reverse else CHUNK - 1
    shift = CHUNK.bit_length() - 1

    row = lax.broadcasted_iota(jnp.int32, (TM, TM), 0)
    col = lax.broadcasted_iota(jnp.int32, (TM, TM), 1)
    same = (row >> shift) == (col >> shift)
    ref_row = ((row >> shift) << shift) + ref_idx
    keep = jnp.logical_and(same, (row <= col) if reverse else (row >= col))

    @pl.when(pl.program_id(0) == 0)
    def _():
        at_ref = jnp.logical_and(same, (ref_row <= col) if reverse else (ref_row >= col))
        ma_ref[...] = (keep.astype(F32) - at_ref.astype(F32)).astype(BF16)

    ma = ma_ref[...]

    rr = lax.broadcasted_iota(jnp.int32, (2 * CHUNKS_PER_TILE, TM), 0)
    cc = lax.broadcasted_iota(jnp.int32, (2 * CHUNKS_PER_TILE, TM), 1)
    chunk = rr & (CHUNKS_PER_TILE - 1)
    target = chunk * CHUNK + jnp.where(rr < CHUNKS_PER_TILE, ref_idx, last_idx)
    covered = (cc >= target) if reverse else (cc <= target)
    rm = jnp.logical_and((cc >> shift) == chunk, covered).astype(BF16)

    chunk_rows = [slice(c * CHUNK, (c + 1) * CHUNK) for c in range(CHUNKS_PER_TILE)]
    order = range(CHUNKS_PER_TILE - 1, -1, -1) if reverse else range(CHUNKS_PER_TILE)
    pair_w = 2 * DK

    def per_chunk_scale(x, scale):
        return jnp.concatenate([x[rs] * scale[c:c + 1] for c, rs in enumerate(chunk_rows)], axis=0)

    def pair_cols(p):
        return slice(p * pair_w, (p + 1) * pair_w)

    def stage_decay(p):
        lbp = lb[:, pair_cols(p)]
        f = lbp + (1.0 - lbp) * jax.nn.sigmoid(f_ref[:, pair_cols(p)])
        lf_hi, lf_lo = _split_bf16(jnp.log(f))
        z = _dot(ma, lf_hi) + _dot(ma, lf_lo)
        marks = _dot(rm, lf_hi) + _dot(rm, lf_lo)
        return 1.0 - f, z, marks

    def stage_scores(p, k, z, marks):
        ref, last = marks[:CHUNKS_PER_TILE], marks[CHUNKS_PER_TILE:]
        qa32 = q_ref[:, pair_cols(p)].astype(F32) * jnp.exp(z)
        kb32 = k * jnp.exp(-z)
        qa = qa32.astype(BF16)
        kb = kb32.astype(BF16)
        qe = per_chunk_scale(qa32, jnp.exp(ref)).astype(BF16)
        kd = per_chunk_scale(kb32, jnp.exp(last - ref)).astype(BF16)
        v = v_ref[:, pair_cols(p)]
        scores, incr = [], []
        for hh in range(2):
            ls = slice(hh * DK, (hh + 1) * DK)
            scores.append(_dot_nt(qa[:, ls], kb[:, ls]))
            kd_h = kd[:, ls]
            kd_blocks = jnp.concatenate(
                [jnp.concatenate([part for part in (jnp.zeros((c * CHUNK, DK), BF16), kd_h[rs],
                                                    jnp.zeros((TM - (c + 1) * CHUNK, DK), BF16))
                                  if part.shape[0]], axis=0)
                 for c, rs in enumerate(chunk_rows)], axis=1)
            incr.append(_dot_tn(v[:, ls], kd_blocks))
        return scores, incr, qe, jnp.exp(last), v

    def stage_output(p, scores, incr, qe, decay, v):
        for hh in range(2):
            h = 2 * p + hh
            ls = slice(hh * DK, (hh + 1) * DK)
            hs = slice(h * DK, (h + 1) * DK)
            o_intra = _dot(jnp.where(keep, scores[hh], 0.0).astype(BF16), v[:, ls])
            st = st_ref[h]
            for c in order:
                rs = chunk_rows[c]
                o_ref[rs, hs] = o_intra[rs] + _dot_nt(qe[rs, ls], st.astype(BF16))
                st = st * decay[c:c + 1, ls] + incr[hh][:, c * DK:(c + 1) * DK]
            st_ref[h] = st

    n_pairs = HEADS // 2
    decayed, scored = {}, {}
    for step in range(n_pairs + 2):
        if step < n_pairs:
            decayed[step] = stage_decay(step)
        if 0 <= step - 1 < n_pairs:
            scored[step - 1] = stage_scores(step - 1, *decayed.pop(step - 1))
        if 0 <= step - 2 < n_pairs:
            stage_output(step - 2, *scored.pop(step - 2))


def _store_prompt_state(t, st_ref, snew_ref):
    @pl.when(t < PROMPT_TILES)
    def _():
        for h in range(HEADS):
            snew_ref[0, h] = st_ref[h].T


def _gla_fwd_kernel(q_ref, v_ref, f_ref, lbl_ref, s0_ref, o_ref, snew_ref, st_ref, ma_ref):
    t = pl.program_id(0)
    _gla_tile(t, q_ref, v_ref, f_ref, lbl_ref, s0_ref, st_ref, ma_ref, o_ref, reverse=False)
    _store_prompt_state(t, st_ref, snew_ref)


def _gla_bwd_kernel(q_ref, v_ref, f_ref, lbl_ref, s0_ref, of_ref, gg_ref, ng_ref,
                    a_ref, snew_ref, st_ref, ma_ref, ob_ref):
    t = N_TILES - 1 - pl.program_id(0)
    _gla_tile(t, q_ref, v_ref, f_ref, lbl_ref, s0_ref, st_ref, ma_ref, ob_ref, reverse=True)
    _store_prompt_state(t, st_ref, snew_ref)
    ng = ng_ref[...]
    for h in range(HEADS):
        hs = slice(h * DV, (h + 1) * DV)
        o = of_ref[:, hs] + ob_ref[:, hs]
        a_ref[:, hs] = (_rmsnorm(o, ng) * _silu(gg_ref[:, hs].astype(F32))).astype(BF16)


def _state_specs(direction, tile_of_step):
    s0 = pl.BlockSpec((1, 1, HEADS, DK, DV),
                      lambda i: (_sample_of_tile(tile_of_step(i)), direction, 0, 0, 0))
    snew = pl.BlockSpec((1, HEADS, DK, DV),
                        lambda i: (jnp.minimum(tile_of_step(i), PROMPT_TILES - 1), 0, 0, 0))
    return s0, snew


def _gla_fwd(q, v, ff, lb_logits, s0):
    s0_spec, snew_spec = _state_specs(0, lambda i: i)
    return pl.pallas_call(
        _gla_fwd_kernel,
        grid=(N_TILES,),
        in_specs=[_TILE_SPEC, _TILE_SPEC, _TILE_SPEC,
                  pl.BlockSpec(lb_logits.shape, lambda i: (0, 0)), s0_spec],
        out_specs=[_TILE_SPEC, snew_spec],
        out_shape=[jax.ShapeDtypeStruct((T, D), F32),
                   jax.ShapeDtypeStruct((N_PROMPT_SEQ, HEADS, DK, DV), F32)],
        scratch_shapes=[pltpu.VMEM((HEADS, DV, DK), F32), pltpu.VMEM((TM, TM), BF16)],
        compiler_params=_cparams(),
        name="gla_fwd",
    )(q, v, ff, lb_logits, s0)


def _gla_bwd(q, v, fb, lb_logits, s0, o_f, gg, head_norm_g):
    rev = lambda i: N_TILES - 1 - i
    tile = pl.BlockSpec((TM, D), lambda i: (rev(i), 0))
    s0_spec, snew_spec = _state_specs(1, rev)
    return pl.pallas_call(
        _gla_bwd_kernel,
        grid=(N_TILES,),
        in_specs=[tile, tile, tile, pl.BlockSpec(lb_logits.shape, lambda i: (0, 0)), s0_spec,
                  tile, tile, pl.BlockSpec((1, DV), lambda i: (0, 0))],
        out_specs=[tile, snew_spec],
        out_shape=[jax.ShapeDtypeStruct((T, D), BF16),
                   jax.ShapeDtypeStruct((N_PROMPT_SEQ, HEADS, DK, DV), F32)],
        scratch_shapes=[pltpu.VMEM((HEADS, DV, DK), F32), pltpu.VMEM((TM, TM), BF16),
                        pltpu.VMEM((TM, D), F32)],
        compiler_params=_cparams(),
        name="gla_bwd",
    )(q, v, fb, lb_logits, s0, o_f, gg, head_norm_g)


def _conv_mixer_kernel(x_ref, mod_ref, g_ref, w_ref, cw_ref, a_ref):
    i = pl.program_id(0)
    h = _modulate(x_ref[...], g_ref[...], mod_ref[0, 0], mod_ref[0, 1]).astype(BF16)
    bg = _dot(h, w_ref[:, 0 * D:1 * D])
    u = _dot(h, w_ref[:, 1 * D:2 * D]) * _dot(h, w_ref[:, 2 * D:3 * D])
    seg = jnp.where(i < PROMPT_TILES, PROMPT_LEN, GRID_W)
    pos = lax.broadcasted_iota(jnp.int32, (TM, 1), 0) & (seg - 1)
    prev = jnp.where(pos == 0, 0.0, pltpu.roll(u, 1, axis=0))
    nxt = jnp.where(pos == seg - 1, 0.0, pltpu.roll(u, TM - 1, axis=0))
    conv = cw_ref[0:1] * prev + cw_ref[1:2] * u + cw_ref[2:3] * nxt
    a_ref[...] = (bg * conv).astype(BF16)


def _conv_mixer(x, mod, norm_g, w_in, conv_w):
    return pl.pallas_call(
        _conv_mixer_kernel,
        grid=(N_TILES,),
        in_specs=[_TILE_SPEC, _MOD_SPEC, _ROW_SPEC, pl.BlockSpec((D, 3 * D), lambda i: (0, 0)),
                  pl.BlockSpec((3, D), lambda i: (0, 0))],
        out_specs=_TILE_SPEC,
        out_shape=jax.ShapeDtypeStruct((T, D), BF16),
        compiler_params=_cparams(),
        name="conv_mixer",
    )(x, mod, norm_g, w_in, conv_w)


ROUTE_ROWS = 32
ROUTE_TILES = 2


def _first_member(vals, target):
    idx = jnp.full_like(target, float(EPG - 1))
    for j in range(EPG - 2, -1, -1):
        idx = jnp.where(vals[j] == target, float(j), idx)
    return idx


def _router_logits(h2, rwt_ref):
    h_hi, h_lo = _split_bf16(h2)
    w_hi, w_lo = _split_bf16(rwt_ref[...])
    return _dot_nt(w_hi, h_hi) + (_dot_nt(w_hi, h_lo) + _dot_nt(w_lo, h_hi))


def _select_experts(logits, rb_ref):
    scores = jax.nn.sigmoid(logits)
    sel = scores + rb_ref[...]
    member = [sel[j * N_GROUPS:(j + 1) * N_GROUPS] for j in range(EPG)]
    m1 = functools.reduce(jnp.maximum, member)
    i1 = _first_member(member, m1)
    rest = [jnp.where(i1 == float(j), -jnp.inf, member[j]) for j in range(EPG)]
    m2 = functools.reduce(jnp.maximum, rest)
    i2 = _first_member(rest, m2)
    group_score = m1 + m2

    best, grp, first, second = group_score[0:1], jnp.zeros((1, TM), F32), i1[0:1], i2[0:1]
    for g in range(1, N_GROUPS):
        upd = group_score[g:g + 1] > best
        best = jnp.where(upd, group_score[g:g + 1], best)
        grp = jnp.where(upd, float(g), grp)
        first = jnp.where(upd, i1[g:g + 1], first)
        second = jnp.where(upd, i2[g:g + 1], second)
    a = jnp.minimum(first, second)
    b = jnp.maximum(first, second)
    row = lax.broadcasted_iota(jnp.int32, (N_EXPERTS, TM), 0).astype(F32)
    s_lo = jnp.sum(jnp.where(row == a * N_GROUPS + grp, scores, 0.0), axis=0, keepdims=True)
    s_hi = jnp.sum(jnp.where(row == b * N_GROUPS + grp, scores, 0.0), axis=0, keepdims=True)
    tot = s_lo + s_hi
    pair = jnp.where(a == 0.0, jnp.where(b == 1.0, 0.0, jnp.where(b == 2.0, 3.0, 4.0)),
                     jnp.where(a == 1.0, jnp.where(b == 2.0, 5.0, 1.0), 2.0))
    return grp * N_PAIRS + pair, s_lo / tot, s_hi / tot


def _post_mixer_kernel(*refs, n_x):
    a_ref, x_refs = refs[0], refs[1:1 + n_x]
    (mod_ref, g2_ref, wo_ref, rwt_ref, rb_ref,
     x1_ref, gates_ref, bucket_ref, rank_ref, cnt_ref, carry_ref, earlier_ref) = refs[1 + n_x:]

    @pl.when(pl.program_id(0) == 0)
    def _():
        carry_ref[...] = jnp.zeros_like(carry_ref)
        s_idx = lax.broadcasted_iota(jnp.int32, (TM, TM), 0)
        t_idx = lax.broadcasted_iota(jnp.int32, (TM, TM), 1)
        earlier_ref[...] = (s_idx < t_idx).astype(BF16)

    def stage_mix(s):
        rows = slice(s * TM, (s + 1) * TM)
        x1 = (_token_tile(x_refs, rows, PROMPT_TILES // ROUTE_TILES)
              + mod_ref[0, 2] * _dot(a_ref[rows], wo_ref[...]))
        x1_ref[rows] = x1
        return _modulate(x1, g2_ref[...], mod_ref[0, 3], mod_ref[0, 4])

    def stage_rank(s, bucket, g_lo, g_hi):
        onehot = lax.broadcasted_iota(jnp.int32, (ROUTE_ROWS, TM), 0).astype(F32) == bucket
        before = _dot(onehot.astype(BF16), earlier_ref[...])
        oh = onehot.astype(F32)
        carry = carry_ref[...]
        rank = jnp.sum(oh * (before + carry[:, 0:1]), axis=0, keepdims=True)
        carry_ref[...] = carry + jnp.sum(oh, axis=1, keepdims=True)
        bucket_ref[s] = bucket.astype(jnp.int32)
        rank_ref[s] = rank.astype(jnp.int32)
        grow = lax.broadcasted_iota(jnp.int32, (LANES, TM), 0)
        gates_ref[s * TM:(s + 1) * TM] = jnp.where(grow == 0, g_lo, jnp.where(grow == 1, g_hi, 0.0)).T

    tiles = range(ROUTE_TILES)
    h2 = [stage_mix(s) for s in tiles]
    logits = [_router_logits(h2[s], rwt_ref) for s in tiles]
    picks = [_select_experts(logits[s], rb_ref) for s in tiles]
    for s in tiles:
        stage_rank(s, *picks[s])
    cnt_ref[...] = carry_ref[...]


def _post_mixer(a, x_parts, mod, norm_g2, w_out, router_wt, router_bt):
    rows = ROUTE_TILES * TM
    idx_spec = pl.BlockSpec((ROUTE_TILES, 1, TM), lambda i: (i, 0, 0))
    return pl.pallas_call(
        functools.partial(_post_mixer_kernel, n_x=len(x_parts)),
        grid=(N_TILES // ROUTE_TILES,),
        in_specs=[pl.BlockSpec((rows, D), lambda i: (i, 0))] + _token_specs(x_parts, ROUTE_TILES) + [
            pl.BlockSpec((1, 6, 1, D), lambda i: (_cond_of_tile(i * ROUTE_TILES), 0, 0, 0)), _ROW_SPEC,
            pl.BlockSpec((D, D), lambda i: (0, 0)),
            pl.BlockSpec((N_EXPERTS, D), lambda i: (0, 0)),
            pl.BlockSpec((N_EXPERTS, 1), lambda i: (0, 0))],
        out_specs=[pl.BlockSpec((rows, D), lambda i: (i, 0)), pl.BlockSpec((rows, LANES), lambda i: (i, 0)),
                   idx_spec, idx_spec, pl.BlockSpec((ROUTE_ROWS, LANES), lambda i: (0, 0))],
        out_shape=[jax.ShapeDtypeStruct((T, D), F32), jax.ShapeDtypeStruct((T, LANES), F32),
                   jax.ShapeDtypeStruct((N_TILES, 1, TM), jnp.int32),
                   jax.ShapeDtypeStruct((N_TILES, 1, TM), jnp.int32),
                   jax.ShapeDtypeStruct((ROUTE_ROWS, LANES), F32)],
        scratch_shapes=[pltpu.VMEM((ROUTE_ROWS, LANES), F32), pltpu.VMEM((TM, TM), BF16)],
        compiler_params=_cparams(),
        name="post_mixer",
    )(a, *x_parts, mod, norm_g2, w_out, router_wt, router_bt)


def _smem_tile_spec(tile_of_step):
    return pl.BlockSpec((1, 1, TM), lambda i, *_: (tile_of_step(i), 0, 0), memory_space=pltpu.SMEM)


def _start_rows(n, make_copy):
    for r in range(n):
        make_copy(r).start()


def _dispatch_kernel(partial_ref, slot_ref, x1_ref, mod_ref, g2_ref, gates_ref,
                     xs_ref, buf_ref, zero_ref, sems, zero_sem):
    i = pl.program_id(0)
    cur = i % 2

    @pl.when(i == 0)
    def _():
        zero_ref[...] = jnp.zeros_like(zero_ref)

        def zero_copy(j):
            return pltpu.make_async_copy(zero_ref, xs_ref.at[pl.ds(j * TM_E, TM_E)], zero_sem)

        def start(j, carry):
            @pl.when(partial_ref[j] != 0)
            def _():
                zero_copy(j).start()
            return carry

        def wait(j, carry):
            @pl.when(partial_ref[j] != 0)
            def _():
                zero_copy(j).wait()
            return carry

        lax.fori_loop(0, N_TILES_E, start, 0)
        lax.fori_loop(0, N_TILES_E, wait, 0)

    def wait_tile(b):
        pltpu.make_async_copy(buf_ref.at[b], xs_ref.at[pl.ds(0, TM)], sems.at[b]).wait()

    h2 = _modulate(x1_ref[...], g2_ref[...], mod_ref[0, 3], mod_ref[0, 4])
    gates = gates_ref[...]

    def send(b):
        buf_ref[b, :, :D] = h2
        buf_ref[b, :, D:] = gates
        _start_rows(TM, lambda r: pltpu.make_async_copy(
            buf_ref.at[b, pl.ds(r, 1)], xs_ref.at[pl.ds(slot_ref[0, 0, r], 1)], sems.at[b]))

        @pl.when(i > 0)
        def _():
            wait_tile(1 - b)

        @pl.when(i == N_TILES - 1)
        def _():
            wait_tile(b)

    for b in range(2):
        pl.when(cur == b)(functools.partial(send, b))


def _dispatch(partial, slot, x1, mod, norm_g2, gates):
    tile = lambda w: pl.BlockSpec((TM, w), lambda i, *_: (i, 0))
    return pl.pallas_call(
        _dispatch_kernel,
        grid_spec=pltpu.PrefetchScalarGridSpec(
            num_scalar_prefetch=1,
            grid=(N_TILES,),
            in_specs=[_smem_tile_spec(lambda i: i), tile(D),
                      pl.BlockSpec((1, 6, 1, D), lambda i, *_: (_cond_of_tile(i), 0, 0, 0)),
                      pl.BlockSpec((1, D), lambda i, *_: (0, 0)), tile(LANES)],
            out_specs=pl.BlockSpec(memory_space=pl.ANY),
            scratch_shapes=[pltpu.VMEM((2, TM, XW), F32), pltpu.VMEM((TM_E, XW), F32),
                            pltpu.SemaphoreType.DMA((2,)), pltpu.SemaphoreType.DMA(())],
        ),
        out_shape=jax.ShapeDtypeStruct((S_ROWS, XW), F32),
        compiler_params=_cparams(),
        name="moe_dispatch",
    )(partial, slot, x1, mod, norm_g2, gates)


_W_SHAPES = ((D, D_FF), (D, D_FF), (D_FF, D))


def _expert_kernel(ea_ref, eb_ref, next_a_ref, next_b_ref, a_higher_ref, nrows_ref,
                   x_ref, wg_hbm, wu_hbm, wd_hbm, y_ref, *scratch, layer):
    stage, w16, sems = scratch[0:3], scratch[3:6], scratch[6]
    j = pl.program_id(0)
    n = nrows_ref[j]
    prev = jnp.maximum(j - 1, 0)

    def fetch(slot, expert):
        return [pltpu.make_async_copy(w.at[layer, expert], s.at[slot], sems.at[slot])
                for w, s in zip((wg_hbm, wu_hbm, wd_hbm), stage)]

    for slot, e_ref, next_ref in ((0, ea_ref, next_a_ref), (1, eb_ref, next_b_ref)):
        @pl.when(j == 0)
        def _(slot=slot, e_ref=e_ref):
            for copy in fetch(slot, e_ref[0]):
                copy.start()

        @pl.when(jnp.logical_or(j == 0, e_ref[j] != e_ref[prev]))
        def _(slot=slot, e_ref=e_ref, next_ref=next_ref):
            for copy in fetch(slot, e_ref[j]):
                copy.wait()
            for s, d in zip(stage, w16):
                d[slot] = s[slot].astype(BF16)

            @pl.when(next_ref[j] >= 0)
            def _():
                for copy in fetch(slot, next_ref[j]):
                    copy.start()

    @pl.when(n > 0)
    def _():
        valid = lax.broadcasted_iota(jnp.int32, (TM_E, 1), 0) < n
        xe = jnp.where(valid, x_ref[...], 0.0)
        x = xe[:, :D].astype(BF16)
        g_lo, g_hi = xe[:, D:D + 1], xe[:, D + 1:D + 2]
        a_higher = a_higher_ref[j] != 0

        def ffn(slot, gate):
            wg_ref, wu_ref, wd_ref = w16
            act = _silu(_dot(x, wg_ref[slot])) * _dot(x, wu_ref[slot]) * gate
            return _dot(act.astype(BF16), wd_ref[slot])

        y_ref[...] = (ffn(0, jnp.where(a_higher, g_hi, g_lo))
                      + ffn(1, jnp.where(a_higher, g_lo, g_hi)))

    @pl.when(n == 0)
    def _():
        y_ref[...] = jnp.zeros_like(y_ref)


def _experts(tile_ea, tile_eb, tile_next_a, tile_next_b, tile_a_higher, tile_rows, xs,
             w_gate, w_up, w_down, layer):
    any_spec = pl.BlockSpec(memory_space=pl.ANY)
    return pl.pallas_call(
        functools.partial(_expert_kernel, layer=layer),
        grid_spec=pltpu.PrefetchScalarGridSpec(
            num_scalar_prefetch=6,
            grid=(N_TILES_E,),
            in_specs=[pl.BlockSpec((TM_E, XW), lambda j, *_: (j, 0)), any_spec, any_spec, any_spec],
            out_specs=pl.BlockSpec((TM_E, D), lambda j, *_: (j, 0)),
            scratch_shapes=[pltpu.VMEM((2,) + s, F32) for s in _W_SHAPES]
                           + [pltpu.VMEM((2,) + s, BF16) for s in _W_SHAPES]
                           + [pltpu.SemaphoreType.DMA((2,))],
        ),
        out_shape=jax.ShapeDtypeStruct((S_ROWS, D), F32),
        compiler_params=_cparams(),
        name="moe_experts",
    )(tile_ea, tile_eb, tile_next_a, tile_next_b, tile_a_higher, tile_rows, xs, w_gate, w_up, w_down)


def _combine_kernel(slot_ref, next_slot_ref, x1_ref, mod_ref, gf_ref, y_ref, *rest, final):
    out_refs, (buf_ref, sems) = rest[:-2], rest[-2:]
    i = pl.program_id(0)
    cur = i % 2

    def gather(slots, b):
        _start_rows(TM, lambda r: pltpu.make_async_copy(
            y_ref.at[pl.ds(slots[0, 0, r], 1)], buf_ref.at[b, pl.ds(r, 1)], sems.at[b]))

    @pl.when(i == 0)
    def _():
        gather(slot_ref, 0)

    def combine(b):
        @pl.when(i + 1 < N_TILES)
        def _():
            gather(next_slot_ref, 1 - b)

        pltpu.make_async_copy(y_ref.at[pl.ds(0, TM)], buf_ref.at[b], sems.at[b]).wait()
        x2 = x1_ref[...] + mod_ref[0, 5] * buf_ref[b]
        if not final:
            out_refs[0][...] = x2
            return
        y = _rmsnorm(x2, gf_ref[...])
        yp_ref, ys_ref = out_refs

        @pl.when(i < PROMPT_TILES)
        def _():
            yp_ref[...] = y

        @pl.when(i >= PROMPT_TILES)
        def _():
            ys_ref[...] = y

    for b in range(2):
        pl.when(cur == b)(functools.partial(combine, b))


def _combine(slot, x1, mod, final_g, y_sorted, *, final):
    if final:
        out_specs = _token_specs((None, None))
        out_shape = [jax.ShapeDtypeStruct((T_PROMPT, D), F32), jax.ShapeDtypeStruct((T_SAMPLE, D), F32)]
    else:
        out_specs = [_TILE_SPEC]
        out_shape = [jax.ShapeDtypeStruct((T, D), F32)]
    return pl.pallas_call(
        functools.partial(_combine_kernel, final=final),
        grid=(N_TILES,),
        in_specs=[_smem_tile_spec(lambda i: i), _smem_tile_spec(lambda i: jnp.minimum(i + 1, N_TILES - 1)),
                  _TILE_SPEC, _MOD_SPEC, _ROW_SPEC, pl.BlockSpec(memory_space=pl.ANY)],
        out_specs=out_specs,
        scratch_shapes=[pltpu.VMEM((2, TM, D), F32), pltpu.SemaphoreType.DMA((2,))],
        out_shape=out_shape,
        compiler_params=_cparams(),
        name="moe_combine_final" if final else "moe_combine",
    )(slot, slot, x1, mod, final_g, y_sorted)


def _slot_kernel(off_ref, bucket_ref, rank_ref, slot_ref):
    bucket = bucket_ref[...]
    slot = rank_ref[...]
    for b in range(N_BUCKETS):
        slot = slot + jnp.where(bucket == b, off_ref[b], 0)
    slot_ref[...] = slot


def _slots(off, bucket, rank):
    whole = pl.BlockSpec(bucket.shape, lambda i, *_: (0, 0, 0))
    return pl.pallas_call(
        _slot_kernel,
        grid_spec=pltpu.PrefetchScalarGridSpec(
            num_scalar_prefetch=1, grid=(1,), in_specs=[whole, whole], out_specs=whole),
        out_shape=jax.ShapeDtypeStruct(bucket.shape, jnp.int32),
        compiler_params=_cparams(),
        name="moe_slots",
    )(off, bucket, rank)


def _next_different(e):
    j = jnp.arange(e.shape[0], dtype=jnp.int32)
    later_diff = jnp.logical_and(e[None, :] != e[:, None], j[None, :] > j[:, None])
    first = jnp.min(jnp.where(later_diff, j[None, :], e.shape[0]), axis=1)
    found = first < e.shape[0]
    return jnp.where(found, e[jnp.minimum(first, e.shape[0] - 1)], -1).astype(jnp.int32)


def _bucket_plan(bucket, rank, counts):
    cnt = counts[:N_BUCKETS, 0].astype(jnp.int32)
    tiles = (cnt + TM_E - 1) // TM_E
    tile_end = jnp.cumsum(tiles)
    tile_start = tile_end - tiles
    slot = _slots((tile_start * TM_E).astype(jnp.int32), bucket, rank)
    j = jnp.arange(N_TILES_E, dtype=jnp.int32)
    n_live = tile_end[-1]
    j_live = jnp.minimum(j, n_live - 1)
    b = jnp.sum((tile_end[None, :] <= j_live[:, None]).astype(jnp.int32), axis=1)
    b = jnp.minimum(b, N_BUCKETS - 1)
    rows = jnp.clip(cnt[b] - (j - tile_start[b]) * TM_E, 0, TM_E)
    rows = jnp.where(j < n_live, rows, 0).astype(jnp.int32)
    grp = b // N_PAIRS
    pair = b % N_PAIRS
    ea = grp * EPG + jnp.asarray(_SLOT_A, jnp.int32)[pair]
    eb = grp * EPG + jnp.asarray(_SLOT_B, jnp.int32)[pair]
    a_higher = jnp.asarray(_SLOT_A_IS_HIGHER, jnp.int32)[pair]
    return slot, (ea, eb, _next_different(ea), _next_different(eb), a_higher, rows)


def _moe(a, x_parts, mod, norm_g2, w_out, router_wt, router_bt, w_gate, w_up, w_down, layer, final_g,
         *, final):
    x1, gates, bucket, rank, counts = _post_mixer(a, x_parts, mod, norm_g2, w_out, router_wt, router_bt)
    slot, tile_plan = _bucket_plan(bucket, rank, counts)
    partial = (tile_plan[-1] < TM_E).astype(jnp.int32)
    xs = _dispatch(partial, slot, x1, mod, norm_g2, gates)
    ys = _experts(*tile_plan, xs, w_gate, w_up, w_down, layer)
    return _combine(slot, x1, mod, final_g, ys, final=final)


def kernel(x_prompt, x_sample, state_hgrn, c, c_ctx, w_mod, b_mod, norm_mix_g, norm_ffn_g, norm_final_g,
           hgrn_w_in, hgrn_lb_logits, hgrn_norm_g, hgrn_w_out, conv_w_in, conv_w, conv_w_out,
           router_w, router_b, moe_w_gate, moe_w_up, moe_w_down):
    x_parts = (x_prompt.reshape(T_PROMPT, D), x_sample.reshape(T_SAMPLE, D))
    cond = jnp.concatenate([c_ctx[None], c, jnp.zeros((COND_ROWS - N_COND, D), F32)], axis=0)
    mods = _adaln(cond, w_mod, b_mod)[:, :N_COND].reshape(2, N_COND, 6, 1, D)
    member_major = lambda w: w.reshape(N_GROUPS, EPG, -1).transpose(1, 0, 2).reshape(N_EXPERTS, -1)
    rwt = member_major(router_w.T)
    rbt = member_major(router_b.reshape(N_EXPERTS, 1))
    final_g = norm_final_g.reshape(1, D)
    bf = lambda w: w.astype(BF16)

    q, v, ff, fb, gg = _hgrn_proj(x_parts, mods[0], norm_mix_g[0:1], bf(hgrn_w_in[0]))
    s0 = state_hgrn[:, 0]
    o_f, s_f = _gla_fwd(q, v, ff, hgrn_lb_logits, s0)
    a, s_b = _gla_bwd(q, v, fb, hgrn_lb_logits, s0, o_f, gg, hgrn_norm_g[0:1])
    (x,) = _moe(a, x_parts, mods[0], norm_ffn_g[0:1], bf(hgrn_w_out[0]), rwt, rbt,
                moe_w_gate, moe_w_up, moe_w_down, 0, final_g, final=False)

    a = _conv_mixer(x, mods[1], norm_mix_g[1:2], bf(conv_w_in[0]), conv_w[0])
    y_p, y_s = _moe(a, (x,), mods[1], norm_ffn_g[1:2], bf(conv_w_out[0]), rwt, rbt,
                    moe_w_gate, moe_w_up, moe_w_down, 1, final_g, final=True)

    new_state = jnp.stack([s_f, s_b], axis=1)[:, None]
    return (y_p.reshape(N_PROMPT_SEQ, PROMPT_LEN, D), y_s.reshape(N_SAMPLE_SEQ, SAMPLE_LEN, D), new_state)
```

```python
import functools

import jax
import jax.numpy as jnp
from jax import lax
from jax.experimental import pallas as pl
from jax.experimental.pallas import tpu as pltpu

F32 = jnp.float32
BF16 = jnp.bfloat16

D = 1024
N_PROMPT_SEQ = 16
PROMPT_LEN = 256
N_SAMPLE_SEQ = 2
SAMPLE_LEN = 4096
GRID_W = 64
T_PROMPT = N_PROMPT_SEQ * PROMPT_LEN
T_SAMPLE = N_SAMPLE_SEQ * SAMPLE_LEN
T = T_PROMPT + T_SAMPLE
N_COND = 1 + N_SAMPLE_SEQ
COND_ROWS = 8

HEADS = 8
DK = 128
DV = 128
CHUNK = 64
N_EXPERTS = 16
N_GROUPS = 4
EPG = 4
N_PAIRS = 6
N_BUCKETS = N_GROUPS * N_PAIRS
D_FF = 512
EPS = 1e-6

TM = 256
N_TILES = T // TM
PROMPT_TILES = T_PROMPT // TM
TILES_PER_SAMPLE = SAMPLE_LEN // TM
CHUNKS_PER_TILE = TM // CHUNK
TM_E = 256
N_TILES_E = T // TM_E + N_BUCKETS
S_ROWS = N_TILES_E * TM_E
XW = D + 128
LANES = 128
MOD_TN = 1536

VMEM_LIMIT = 56 * 1024 * 1024

_SLOT_A = (0, 3, 3, 0, 0, 1)
_SLOT_B = (1, 1, 2, 2, 3, 2)
_SLOT_A_IS_HIGHER = tuple(int(a > b) for a, b in zip(_SLOT_A, _SLOT_B))


def _cparams():
    return pltpu.CompilerParams(dimension_semantics=("arbitrary",), vmem_limit_bytes=VMEM_LIMIT)


def _cond_of_tile(t):
    return jnp.where(t < PROMPT_TILES, 0, 1 + (t - PROMPT_TILES) // TILES_PER_SAMPLE)


def _sample_of_tile(t):
    return jnp.clip((t - PROMPT_TILES) // TILES_PER_SAMPLE, 0, N_SAMPLE_SEQ - 1)


def _rmsnorm(x, g):
    ms = jnp.mean(x * x, axis=-1, keepdims=True)
    return (x * lax.rsqrt(ms + EPS)) * g


def _modulate(x, g, shift, scale):
    return _rmsnorm(x, g) * (1.0 + scale) + shift


def _silu(x):
    return x * jax.nn.sigmoid(x)


def _dot(a, b):
    return jnp.dot(a, b, preferred_element_type=F32)


def _dot_nt(a, b):
    return lax.dot_general(a, b, (((1,), (1,)), ((), ())), preferred_element_type=F32)


def _dot_tn(a, b):
    return lax.dot_general(a, b, (((0,), (0,)), ((), ())), preferred_element_type=F32)


def _split_bf16(x):
    hi = x.astype(BF16)
    lo = (x - hi.astype(F32)).astype(BF16)
    return hi, lo


def _mod_kernel(cond_ref, w_ref, b_ref, o_ref):
    s = _silu(cond_ref[...])
    o_ref[0] = _dot(s.astype(BF16), w_ref[0].astype(BF16)) + b_ref[0]


def _adaln(cond, w_mod, b_mod):
    depth = w_mod.shape[0]
    return pl.pallas_call(
        _mod_kernel,
        grid=(depth, 6 * D // MOD_TN),
        in_specs=[
            pl.BlockSpec((COND_ROWS, D), lambda i, j: (0, 0)),
            pl.BlockSpec((1, D, MOD_TN), lambda i, j: (i, 0, j)),
            pl.BlockSpec((1, 1, MOD_TN), lambda i, j: (i, 0, j)),
        ],
        out_specs=pl.BlockSpec((1, COND_ROWS, MOD_TN), lambda i, j: (i, 0, j)),
        out_shape=jax.ShapeDtypeStruct((depth, COND_ROWS, 6 * D), F32),
        compiler_params=pltpu.CompilerParams(
            dimension_semantics=("arbitrary", "arbitrary"), vmem_limit_bytes=VMEM_LIMIT),
        name="adaln",
    )(cond, w_mod, b_mod.reshape(depth, 1, 6 * D))


_MOD_SPEC = pl.BlockSpec((1, 6, 1, D), lambda i: (_cond_of_tile(i), 0, 0, 0))
_ROW_SPEC = pl.BlockSpec((1, D), lambda i: (0, 0))
_TILE_SPEC = pl.BlockSpec((TM, D), lambda i: (i, 0))


def _token_tile(x_refs, rows=slice(None), prompt_steps=PROMPT_TILES):
    if len(x_refs) == 1:
        return x_refs[0][rows]
    return jnp.where(pl.program_id(0) < prompt_steps, x_refs[0][rows], x_refs[1][rows])


def _token_specs(x_parts, tiles_per_step=1):
    rows = tiles_per_step * TM
    if len(x_parts) == 1:
        return [pl.BlockSpec((rows, D), lambda i: (i, 0))]
    prompt_steps = T_PROMPT // rows
    return [pl.BlockSpec((rows, D), lambda i: (jnp.minimum(i, prompt_steps - 1), 0)),
            pl.BlockSpec((rows, D), lambda i: (jnp.maximum(i - prompt_steps, 0), 0))]


def _hgrn_proj_kernel(xp_ref, xs_ref, mod_ref, g_ref, w_ref, q_ref, v_ref, ff_ref, fb_ref, gg_ref):
    x = _token_tile((xp_ref, xs_ref))
    h = _modulate(x, g_ref[...], mod_ref[0, 0], mod_ref[0, 1]).astype(BF16)
    q_ref[...] = (_dot(h, w_ref[:, 0 * D:1 * D]) * (DK ** -0.5)).astype(BF16)
    v_ref[...] = _dot(h, w_ref[:, 1 * D:2 * D]).astype(BF16)
    ff_ref[...] = _dot(h, w_ref[:, 2 * D:3 * D])
    fb_ref[...] = _dot(h, w_ref[:, 3 * D:4 * D])
    gg_ref[...] = _dot(h, w_ref[:, 4 * D:5 * D]).astype(BF16)


def _hgrn_proj(x_parts, mod, norm_g, w_in):
    bf = jax.ShapeDtypeStruct((T, D), BF16)
    f32 = jax.ShapeDtypeStruct((T, D), F32)
    return pl.pallas_call(
        _hgrn_proj_kernel,
        grid=(N_TILES,),
        in_specs=_token_specs(x_parts) + [_MOD_SPEC, _ROW_SPEC, pl.BlockSpec((D, 5 * D), lambda i: (0, 0))],
        out_specs=[_TILE_SPEC] * 5,
        out_shape=[bf, bf, f32, f32, bf],
        compiler_params=_cparams(),
        name="hgrn_proj",
    )(*x_parts, mod, norm_g, w_in)


def _gla_tile(t, q_ref, v_ref, f_ref, lbl_ref, s0_ref, st_ref, ma_ref, o_ref, *, reverse):
    is_prompt = t < PROMPT_TILES
    rel = t - PROMPT_TILES
    edge = (TILES_PER_SAMPLE - 1) if reverse else 0
    sample_start = jnp.logical_and(rel >= 0, rel % TILES_PER_SAMPLE == edge)

    @pl.when(is_prompt)
    def _():
        st_ref[...] = jnp.zeros_like(st_ref)

    @pl.when(sample_start)
    def _():
        for h in range(HEADS):
            st_ref[h] = s0_ref[0, 0, h].T

    logits = lbl_ref[...]
    e = jnp.exp(logits - jnp.max(logits, axis=0, keepdims=True))
    lb = e[0:1] / jnp.sum(e, axis=0, keepdims=True)

    ref_idx = (CHUNK - 1 - CHUNK // 2) if reverse else CHUNK // 2
    last_idx = 0 if reverse else CHUNK - 1
    shift = CHUNK.bit_length() - 1

    row = lax.broadcasted_iota(jnp.int32, (TM, TM), 0)
    col = lax.broadcasted_iota(jnp.int32, (TM, TM), 1)
    same = (row >> shift) == (col >> shift)
    ref_row = ((row >> shift) << shift) + ref_idx
    keep = jnp.logical_and(same, (row <= col) if reverse else (row >= col))

    @pl.when(pl.program_id(0) == 0)
    def _():
        at_ref = jnp.logical_and(same, (ref_row <= col) if reverse else (ref_row >= col))
        ma_ref[...] = (keep.astype(F32) - at_ref.astype(F32)).astype(BF16)

    ma = ma_ref[...]

    rr = lax.broadcasted_iota(jnp.int32, (2 * CHUNKS_PER_TILE, TM), 0)
    cc = lax.broadcasted_iota(jnp.int32, (2 * CHUNKS_PER_TILE, TM), 1)
    chunk = rr & (CHUNKS_PER_TILE - 1)
    target = chunk * CHUNK + jnp.where(rr < CHUNKS_PER_TILE, ref_idx, last_idx)
    covered = (cc >= target) if reverse else (cc <= target)
    rm = jnp.logical_and((cc >> shift) == chunk, covered).astype(BF16)

    chunk_rows = [slice(c * CHUNK, (c + 1) * CHUNK) for c in range(CHUNKS_PER_TILE)]
    order = range(CHUNKS_PER_TILE - 1, -1, -1) if reverse else range(CHUNKS_PER_TILE)
    pair_w = 2 * DK

    def per_chunk_scale(x, scale):
        return jnp.concatenate([x[rs] * scale[c:c + 1] for c, rs in enumerate(chunk_rows)], axis=0)

    def pair_cols(p):
        return slice(p * pair_w, (p + 1) * pair_w)

    def stage_decay(p):
        lbp = lb[:, pair_cols(p)]
        f = lbp + (1.0 - lbp) * jax.nn.sigmoid(f_ref[:, pair_cols(p)])
        lf_hi, lf_lo = _split_bf16(jnp.log(f))
        z = _dot(ma, lf_hi) + _dot(ma, lf_lo)
        marks = _dot(rm, lf_hi) + _dot(rm, lf_lo)
        return 1.0 - f, z, marks

    def stage_scores(p, k, z, marks):
        ref, last = marks[:CHUNKS_PER_TILE], marks[CHUNKS_PER_TILE:]
        qa32 = q_ref[:, pair_cols(p)].astype(F32) * jnp.exp(z)
        kb32 = k * jnp.exp(-z)
        qa = qa32.astype(BF16)
        kb = kb32.astype(BF16)
        qe = per_chunk_scale(qa32, jnp.exp(ref)).astype(BF16)
        kd = per_chunk_scale(kb32, jnp.exp(last - ref)).astype(BF16)
        v = v_ref[:, pair_cols(p)]
        scores, incr = [], []
        for hh in range(2):
            ls = slice(hh * DK, (hh + 1) * DK)
            scores.append(_dot_nt(qa[:, ls], kb[:, ls]))
            kd_h = kd[:, ls]
            kd_blocks = jnp.concatenate(
                [jnp.concatenate([part for part in (jnp.zeros((c * CHUNK, DK), BF16), kd_h[rs],
                                                    jnp.zeros((TM - (c + 1) * CHUNK, DK), BF16))
                                  if part.shape[0]], axis=0)
                 for c, rs in enumerate(chunk_rows)], axis=1)
            incr.append(_dot_tn(v[:, ls], kd_blocks))
        return scores, incr, qe, jnp.exp(last), v

    def stage_output(p, scores, incr, qe, decay, v):
        for hh in range(2):
            h = 2 * p + hh
            ls = slice(hh * DK, (hh + 1) * DK)
            hs = slice(h * DK, (h + 1) * DK)
            o_intra = _dot(jnp.where(keep, scores[hh], 0.0).astype(BF16), v[:, ls])
            st = st_ref[h]
            for c in order:
                rs = chunk_rows[c]
                o_ref[rs, hs] = o_intra[rs] + _dot_nt(qe[rs, ls], st.astype(BF16))
                st = st * decay[c:c + 1, ls] + incr[hh][:, c * DK:(c + 1) * DK]
            st_ref[h] = st

    n_pairs = HEADS // 2
    decayed, scored = {}, {}
    for step in range(n_pairs + 2):
        if step < n_pairs:
            decayed[step] = stage_decay(step)
        if 0 <= step - 1 < n_pairs:
            scored[step - 1] = stage_scores(step - 1, *decayed.pop(step - 1))
        if 0 <= step - 2 < n_pairs:
            stage_output(step - 2, *scored.pop(step - 2))


def _store_prompt_state(t, st_ref, snew_ref):
    @pl.when(t < PROMPT_TILES)
    def _():
        for h in range(HEADS):
            snew_ref[0, h] = st_ref[h].T


def _gla_fwd_kernel(q_ref, v_ref, f_ref, lbl_ref, s0_ref, o_ref, snew_ref, st_ref, ma_ref):
    t = pl.program_id(0)
    _gla_tile(t, q_ref, v_ref, f_ref, lbl_ref, s0_ref, st_ref, ma_ref, o_ref, reverse=False)
    _store_prompt_state(t, st_ref, snew_ref)


def _gla_bwd_kernel(q_ref, v_ref, f_ref, lbl_ref, s0_ref, of_ref, gg_ref, ng_ref,
                    a_ref, snew_ref, st_ref, ma_ref, ob_ref):
    t = N_TILES - 1 - pl.program_id(0)
    _gla_tile(t, q_ref, v_ref, f_ref, lbl_ref, s0_ref, st_ref, ma_ref, ob_ref, reverse=True)
    _store_prompt_state(t, st_ref, snew_ref)
    ng = ng_ref[...]
    for h in range(HEADS):
        hs = slice(h * DV, (h + 1) * DV)
        o = of_ref[:, hs] + ob_ref[:, hs]
        a_ref[:, hs] = (_rmsnorm(o, ng) * _silu(gg_ref[:, hs].astype(F32))).astype(BF16)


def _state_specs(direction, tile_of_step):
    s0 = pl.BlockSpec((1, 1, HEADS, DK, DV),
                      lambda i: (_sample_of_tile(tile_of_step(i)), direction, 0, 0, 0))
    snew = pl.BlockSpec((1, HEADS, DK, DV),
                        lambda i: (jnp.minimum(tile_of_step(i), PROMPT_TILES - 1), 0, 0, 0))
    return s0, snew


def _gla_fwd(q, v, ff, lb_logits, s0):
    s0_spec, snew_spec = _state_specs(0, lambda i: i)
    return pl.pallas_call(
        _gla_fwd_kernel,
        grid=(N_TILES,),
        in_specs=[_TILE_SPEC, _TILE_SPEC, _TILE_SPEC,
                  pl.BlockSpec(lb_logits.shape, lambda i: (0, 0)), s0_spec],
        out_specs=[_TILE_SPEC, snew_spec],
        out_shape=[jax.ShapeDtypeStruct((T, D), F32),
                   jax.ShapeDtypeStruct((N_PROMPT_SEQ, HEADS, DK, DV), F32)],
        scratch_shapes=[pltpu.VMEM((HEADS, DV, DK), F32), pltpu.VMEM((TM, TM), BF16)],
        compiler_params=_cparams(),
        name="gla_fwd",
    )(q, v, ff, lb_logits, s0)


def _gla_bwd(q, v, fb, lb_logits, s0, o_f, gg, head_norm_g):
    rev = lambda i: N_TILES - 1 - i
    tile = pl.BlockSpec((TM, D), lambda i: (rev(i), 0))
    s0_spec, snew_spec = _state_specs(1, rev)
    return pl.pallas_call(
        _gla_bwd_kernel,
        grid=(N_TILES,),
        in_specs=[tile, tile, tile, pl.BlockSpec(lb_logits.shape, lambda i: (0, 0)), s0_spec,
                  tile, tile, pl.BlockSpec((1, DV), lambda i: (0, 0))],
        out_specs=[tile, snew_spec],
        out_shape=[jax.ShapeDtypeStruct((T, D), BF16),
                   jax.ShapeDtypeStruct((N_PROMPT_SEQ, HEADS, DK, DV), F32)],
        scratch_shapes=[pltpu.VMEM((HEADS, DV, DK), F32), pltpu.VMEM((TM, TM), BF16),
                        pltpu.VMEM((TM, D), F32)],
        compiler_params=_cparams(),
        name="gla_bwd",
    )(q, v, fb, lb_logits, s0, o_f, gg, head_norm_g)


def _conv_mixer_kernel(x_ref, mod_ref, g_ref, w_ref, cw_ref, a_ref):
    i = pl.program_id(0)
    h = _modulate(x_ref[...], g_ref[...], mod_ref[0, 0], mod_ref[0, 1]).astype(BF16)
    bg = _dot(h, w_ref[:, 0 * D:1 * D])
    u = _dot(h, w_ref[:, 1 * D:2 * D]) * _dot(h, w_ref[:, 2 * D:3 * D])
    seg = jnp.where(i < PROMPT_TILES, PROMPT_LEN, GRID_W)
    pos = lax.broadcasted_iota(jnp.int32, (TM, 1), 0) & (seg - 1)
    prev = jnp.where(pos == 0, 0.0, pltpu.roll(u, 1, axis=0))
    nxt = jnp.where(pos == seg - 1, 0.0, pltpu.roll(u, TM - 1, axis=0))
    conv = cw_ref[0:1] * prev + cw_ref[1:2] * u + cw_ref[2:3] * nxt
    a_ref[...] = (bg * conv).astype(BF16)


def _conv_mixer(x, mod, norm_g, w_in, conv_w):
    return pl.pallas_call(
        _conv_mixer_kernel,
        grid=(N_TILES,),
        in_specs=[_TILE_SPEC, _MOD_SPEC, _ROW_SPEC, pl.BlockSpec((D, 3 * D), lambda i: (0, 0)),
                  pl.BlockSpec((3, D), lambda i: (0, 0))],
        out_specs=_TILE_SPEC,
        out_shape=jax.ShapeDtypeStruct((T, D), BF16),
        compiler_params=_cparams(),
        name="conv_mixer",
    )(x, mod, norm_g, w_in, conv_w)


ROUTE_ROWS = 32
ROUTE_TILES = 2


def _first_member(vals, target):
    idx = jnp.full_like(target, float(EPG - 1))
    for j in range(EPG - 2, -1, -1):
        idx = jnp.where(vals[j] == target, float(j), idx)
    return idx


def _router_logits(h2, rwt_ref):
    h_hi, h_lo = _split_bf16(h2)
    w_hi, w_lo = _split_bf16(rwt_ref[...])
    return _dot_nt(w_hi, h_hi) + (_dot_nt(w_hi, h_lo) + _dot_nt(w_lo, h_hi))


def _select_experts(logits, rb_ref):
    scores = jax.nn.sigmoid(logits)
    sel = scores + rb_ref[...]
    member = [sel[j * N_GROUPS:(j + 1) * N_GROUPS] for j in range(EPG)]
    m1 = functools.reduce(jnp.maximum, member)
    i1 = _first_member(member, m1)
    rest = [jnp.where(i1 == float(j), -jnp.inf, member[j]) for j in range(EPG)]
    m2 = functools.reduce(jnp.maximum, rest)
    i2 = _first_member(rest, m2)
    group_score = m1 + m2

    best, grp, first, second = group_score[0:1], jnp.zeros((1, TM), F32), i1[0:1], i2[0:1]
    for g in range(1, N_GROUPS):
        upd = group_score[g:g + 1] > best
        best = jnp.where(upd, group_score[g:g + 1], best)
        grp = jnp.where(upd, float(g), grp)
        first = jnp.where(upd, i1[g:g + 1], first)
        second = jnp.where(upd, i2[g:g + 1], second)
    a = jnp.minimum(first, second)
    b = jnp.maximum(first, second)
    row = lax.broadcasted_iota(jnp.int32, (N_EXPERTS, TM), 0).astype(F32)
    s_lo = jnp.sum(jnp.where(row == a * N_GROUPS + grp, scores, 0.0), axis=0, keepdims=True)
    s_hi = jnp.sum(jnp.where(row == b * N_GROUPS + grp, scores, 0.0), axis=0, keepdims=True)
    tot = s_lo + s_hi
    pair = jnp.where(a == 0.0, jnp.where(b == 1.0, 0.0, jnp.where(b == 2.0, 3.0, 4.0)),
                     jnp.where(a == 1.0, jnp.where(b == 2.0, 5.0, 1.0), 2.0))
    return grp * N_PAIRS + pair, s_lo / tot, s_hi / tot


def _post_mixer_kernel(*refs, n_x):
    a_ref, x_refs = refs[0], refs[1:1 + n_x]
    (mod_ref, g2_ref, wo_ref, rwt_ref, rb_ref,
     x1_ref, gates_ref, bucket_ref, rank_ref, cnt_ref, carry_ref, earlier_ref) = refs[1 + n_x:]

    @pl.when(pl.program_id(0) == 0)
    def _():
        carry_ref[...] = jnp.zeros_like(carry_ref)
        s_idx = lax.broadcasted_iota(jnp.int32, (TM, TM), 0)
        t_idx = lax.broadcasted_iota(jnp.int32, (TM, TM), 1)
        earlier_ref[...] = (s_idx < t_idx).astype(BF16)

    def stage_mix(s):
        rows = slice(s * TM, (s + 1) * TM)
        x1 = (_token_tile(x_refs, rows, PROMPT_TILES // ROUTE_TILES)
              + mod_ref[0, 2] * _dot(a_ref[rows], wo_ref[...]))
        x1_ref[rows] = x1
        return _modulate(x1, g2_ref[...], mod_ref[0, 3], mod_ref[0, 4])

    def stage_rank(s, bucket, g_lo, g_hi):
        onehot = lax.broadcasted_iota(jnp.int32, (ROUTE_ROWS, TM), 0).astype(F32) == bucket
        before = _dot(onehot.astype(BF16), earlier_ref[...])
        oh = onehot.astype(F32)
        carry = carry_ref[...]
        rank = jnp.sum(oh * (before + carry[:, 0:1]), axis=0, keepdims=True)
        carry_ref[...] = carry + jnp.sum(oh, axis=1, keepdims=True)
        bucket_ref[s] = bucket.astype(jnp.int32)
        rank_ref[s] = rank.astype(jnp.int32)
        grow = lax.broadcasted_iota(jnp.int32, (LANES, TM), 0)
        gates_ref[s * TM:(s + 1) * TM] = jnp.where(grow == 0, g_lo, jnp.where(grow == 1, g_hi, 0.0)).T

    tiles = range(ROUTE_TILES)
    h2 = [stage_mix(s) for s in tiles]
    logits = [_router_logits(h2[s], rwt_ref) for s in tiles]
    picks = [_select_experts(logits[s], rb_ref) for s in tiles]
    for s in tiles:
        stage_rank(s, *picks[s])
    cnt_ref[...] = carry_ref[...]


def _post_mixer(a, x_parts, mod, norm_g2, w_out, router_wt, router_bt):
    rows = ROUTE_TILES * TM
    idx_spec = pl.BlockSpec((ROUTE_TILES, 1, TM), lambda i: (i, 0, 0))
    return pl.pallas_call(
        functools.partial(_post_mixer_kernel, n_x=len(x_parts)),
        grid=(N_TILES // ROUTE_TILES,),
        in_specs=[pl.BlockSpec((rows, D), lambda i: (i, 0))] + _token_specs(x_parts, ROUTE_TILES) + [
            pl.BlockSpec((1, 6, 1, D), lambda i: (_cond_of_tile(i * ROUTE_TILES), 0, 0, 0)), _ROW_SPEC,
            pl.BlockSpec((D, D), lambda i: (0, 0)),
            pl.BlockSpec((N_EXPERTS, D), lambda i: (0, 0)),
            pl.BlockSpec((N_EXPERTS, 1), lambda i: (0, 0))],
        out_specs=[pl.BlockSpec((rows, D), lambda i: (i, 0)), pl.BlockSpec((rows, LANES), lambda i: (i, 0)),
                   idx_spec, idx_spec, pl.BlockSpec((ROUTE_ROWS, LANES), lambda i: (0, 0))],
        out_shape=[jax.ShapeDtypeStruct((T, D), F32), jax.ShapeDtypeStruct((T, LANES), F32),
                   jax.ShapeDtypeStruct((N_TILES, 1, TM), jnp.int32),
                   jax.ShapeDtypeStruct((N_TILES, 1, TM), jnp.int32),
                   jax.ShapeDtypeStruct((ROUTE_ROWS, LANES), F32)],
        scratch_shapes=[pltpu.VMEM((ROUTE_ROWS, LANES), F32), pltpu.VMEM((TM, TM), BF16)],
        compiler_params=_cparams(),
        name="post_mixer",
    )(a, *x_parts, mod, norm_g2, w_out, router_wt, router_bt)


def _smem_tile_spec(tile_of_step):
    return pl.BlockSpec((1, 1, TM), lambda i, *_: (tile_of_step(i), 0, 0), memory_space=pltpu.SMEM)


def _start_rows(n, make_copy):
    for r in range(n):
        make_copy(r).start(priority=r % 2)


def _dispatch_kernel(partial_ref, slot_ref, x1_ref, mod_ref, g2_ref, gates_ref,
                     xs_ref, buf_ref, zero_ref, sems, zero_sem):
    i = pl.program_id(0)
    cur = i % 2

    @pl.when(i == 0)
    def _():
        zero_ref[...] = jnp.zeros_like(zero_ref)

        def zero_copy(j):
            return pltpu.make_async_copy(zero_ref, xs_ref.at[pl.ds(j * TM_E, TM_E)], zero_sem)

        def start(j, carry):
            @pl.when(partial_ref[j] != 0)
            def _():
                zero_copy(j).start()
            return carry

        def wait(j, carry):
            @pl.when(partial_ref[j] != 0)
            def _():
                zero_copy(j).wait()
            return carry

        lax.fori_loop(0, N_TILES_E, start, 0)
        lax.fori_loop(0, N_TILES_E, wait, 0)

    def wait_tile(b):
        pltpu.make_async_copy(buf_ref.at[b], xs_ref.at[pl.ds(0, TM)], sems.at[b]).wait()

    h2 = _modulate(x1_ref[...], g2_ref[...], mod_ref[0, 3], mod_ref[0, 4])
    gates = gates_ref[...]

    def send(b):
        buf_ref[b, :, :D] = h2
        buf_ref[b, :, D:] = gates
        _start_rows(TM, lambda r: pltpu.make_async_copy(
            buf_ref.at[b, pl.ds(r, 1)], xs_ref.at[pl.ds(slot_ref[0, 0, r], 1)], sems.at[b]))

        @pl.when(i > 0)
        def _():
            wait_tile(1 - b)

        @pl.when(i == N_TILES - 1)
        def _():
            wait_tile(b)

    for b in range(2):
        pl.when(cur == b)(functools.partial(send, b))


def _dispatch(partial, slot, x1, mod, norm_g2, gates):
    tile = lambda w: pl.BlockSpec((TM, w), lambda i, *_: (i, 0))
    return pl.pallas_call(
        _dispatch_kernel,
        grid_spec=pltpu.PrefetchScalarGridSpec(
            num_scalar_prefetch=1,
            grid=(N_TILES,),
            in_specs=[_smem_tile_spec(lambda i: i), tile(D),
                      pl.BlockSpec((1, 6, 1, D), lambda i, *_: (_cond_of_tile(i), 0, 0, 0)),
                      pl.BlockSpec((1, D), lambda i, *_: (0, 0)), tile(LANES)],
            out_specs=pl.BlockSpec(memory_space=pl.ANY),
            scratch_shapes=[pltpu.VMEM((2, TM, XW), F32), pltpu.VMEM((TM_E, XW), F32),
                            pltpu.SemaphoreType.DMA((2,)), pltpu.SemaphoreType.DMA(())],
        ),
        out_shape=jax.ShapeDtypeStruct((S_ROWS, XW), F32),
        compiler_params=_cparams(),
        name="moe_dispatch",
    )(partial, slot, x1, mod, norm_g2, gates)


_W_SHAPES = ((D, D_FF), (D, D_FF), (D_FF, D))


def _expert_kernel(ea_ref, eb_ref, next_a_ref, next_b_ref, a_higher_ref, nrows_ref,
                   x_ref, wg_hbm, wu_hbm, wd_hbm, y_ref, *scratch, layer):
    stage, w16, sems = scratch[0:3], scratch[3:6], scratch[6]
    j = pl.program_id(0)
    n = nrows_ref[j]
    prev = jnp.maximum(j - 1, 0)

    def fetch(slot, expert):
        return [pltpu.make_async_copy(w.at[layer, expert], s.at[slot], sems.at[slot])
                for w, s in zip((wg_hbm, wu_hbm, wd_hbm), stage)]

    for slot, e_ref, next_ref in ((0, ea_ref, next_a_ref), (1, eb_ref, next_b_ref)):
        @pl.when(j == 0)
        def _(slot=slot, e_ref=e_ref):
            for copy in fetch(slot, e_ref[0]):
                copy.start()

        @pl.when(jnp.logical_or(j == 0, e_ref[j] != e_ref[prev]))
        def _(slot=slot, e_ref=e_ref, next_ref=next_ref):
            for copy in fetch(slot, e_ref[j]):
                copy.wait()
            for s, d in zip(stage, w16):
                d[slot] = s[slot].astype(BF16)

            @pl.when(next_ref[j] >= 0)
            def _():
                for copy in fetch(slot, next_ref[j]):
                    copy.start()

    def run(rows):
        valid = lax.broadcasted_iota(jnp.int32, (rows, 1), 0) < n
        xe = jnp.where(valid, x_ref[:rows], 0.0)
        x = xe[:, :D].astype(BF16)
        g_lo, g_hi = xe[:, D:D + 1], xe[:, D + 1:D + 2]
        a_higher = a_higher_ref[j] != 0

        def ffn(slot, gate):
            wg_ref, wu_ref, wd_ref = w16
            act = _silu(_dot(x, wg_ref[slot])) * _dot(x, wu_ref[slot]) * gate
            return _dot(act.astype(BF16), wd_ref[slot])

        y_ref[:rows] = (ffn(0, jnp.where(a_higher, g_hi, g_lo))
                        + ffn(1, jnp.where(a_higher, g_lo, g_hi)))
        if rows < TM_E:
            y_ref[rows:] = jnp.zeros((TM_E - rows, D), F32)

    half = TM_E // 2
    pl.when(n > half)(functools.partial(run, TM_E))
    pl.when(jnp.logical_and(n > 0, n <= half))(functools.partial(run, half))

    @pl.when(n == 0)
    def _():
        y_ref[...] = jnp.zeros_like(y_ref)


def _experts(tile_ea, tile_eb, tile_next_a, tile_next_b, tile_a_higher, tile_rows, xs,
             w_gate, w_up, w_down, layer):
    any_spec = pl.BlockSpec(memory_space=pl.ANY)
    return pl.pallas_call(
        functools.partial(_expert_kernel, layer=layer),
        grid_spec=pltpu.PrefetchScalarGridSpec(
            num_scalar_prefetch=6,
            grid=(N_TILES_E,),
            in_specs=[pl.BlockSpec((TM_E, XW), lambda j, *_: (j, 0)), any_spec, any_spec, any_spec],
            out_specs=pl.BlockSpec((TM_E, D), lambda j, *_: (j, 0)),
            scratch_shapes=[pltpu.VMEM((2,) + s, F32) for s in _W_SHAPES]
                           + [pltpu.VMEM((2,) + s, BF16) for s in _W_SHAPES]
                           + [pltpu.SemaphoreType.DMA((2,))],
        ),
        out_shape=jax.ShapeDtypeStruct((S_ROWS, D), F32),
        compiler_params=_cparams(),
        name="moe_experts",
    )(tile_ea, tile_eb, tile_next_a, tile_next_b, tile_a_higher, tile_rows, xs, w_gate, w_up, w_down)


def _combine_kernel(slot_ref, next_slot_ref, x1_ref, mod_ref, gf_ref, y_ref, *rest, final):
    out_refs, (buf_ref, sems) = rest[:-2], rest[-2:]
    i = pl.program_id(0)
    cur = i % 2

    def gather(slots, b):
        _start_rows(TM, lambda r: pltpu.make_async_copy(
            y_ref.at[pl.ds(slots[0, 0, r], 1)], buf_ref.at[b, pl.ds(r, 1)], sems.at[b]))

    @pl.when(i == 0)
    def _():
        gather(slot_ref, 0)

    def combine(b):
        @pl.when(i + 1 < N_TILES)
        def _():
            gather(next_slot_ref, 1 - b)

        pltpu.make_async_copy(y_ref.at[pl.ds(0, TM)], buf_ref.at[b], sems.at[b]).wait()
        x2 = x1_ref[...] + mod_ref[0, 5] * buf_ref[b]
        if not final:
            out_refs[0][...] = x2
            return
        y = _rmsnorm(x2, gf_ref[...])
        yp_ref, ys_ref = out_refs

        @pl.when(i < PROMPT_TILES)
        def _():
            yp_ref[...] = y

        @pl.when(i >= PROMPT_TILES)
        def _():
            ys_ref[...] = y

    for b in range(2):
        pl.when(cur == b)(functools.partial(combine, b))


def _combine(slot, x1, mod, final_g, y_sorted, *, final):
    if final:
        out_specs = _token_specs((None, None))
        out_shape = [jax.ShapeDtypeStruct((T_PROMPT, D), F32), jax.ShapeDtypeStruct((T_SAMPLE, D), F32)]
    else:
        out_specs = [_TILE_SPEC]
        out_shape = [jax.ShapeDtypeStruct((T, D), F32)]
    return pl.pallas_call(
        functools.partial(_combine_kernel, final=final),
        grid=(N_TILES,),
        in_specs=[_smem_tile_spec(lambda i: i), _smem_tile_spec(lambda i: jnp.minimum(i + 1, N_TILES - 1)),
                  _TILE_SPEC, _MOD_SPEC, _ROW_SPEC, pl.BlockSpec(memory_space=pl.ANY)],
        out_specs=out_specs,
        scratch_shapes=[pltpu.VMEM((2, TM, D), F32), pltpu.SemaphoreType.DMA((2,))],
        out_shape=out_shape,
        compiler_params=_cparams(),
        name="moe_combine_final" if final else "moe_combine",
    )(slot, slot, x1, mod, final_g, y_sorted)


def _slot_kernel(off_ref, bucket_ref, rank_ref, slot_ref):
    bucket = bucket_ref[...]
    slot = rank_ref[...]
    for b in range(N_BUCKETS):
        slot = slot + jnp.where(bucket == b, off_ref[b], 0)
    slot_ref[...] = slot


def _slots(off, bucket, rank):
    whole = pl.BlockSpec(bucket.shape, lambda i, *_: (0, 0, 0))
    return pl.pallas_call(
        _slot_kernel,
        grid_spec=pltpu.PrefetchScalarGridSpec(
            num_scalar_prefetch=1, grid=(1,), in_specs=[whole, whole], out_specs=whole),
        out_shape=jax.ShapeDtypeStruct(bucket.shape, jnp.int32),
        compiler_params=_cparams(),
        name="moe_slots",
    )(off, bucket, rank)


def _next_different(e):
    j = jnp.arange(e.shape[0], dtype=jnp.int32)
    later_diff = jnp.logical_and(e[None, :] != e[:, None], j[None, :] > j[:, None])
    first = jnp.min(jnp.where(later_diff, j[None, :], e.shape[0]), axis=1)
    found = first < e.shape[0]
    return jnp.where(found, e[jnp.minimum(first, e.shape[0] - 1)], -1).astype(jnp.int32)


def _bucket_plan(bucket, rank, counts):
    cnt = counts[:N_BUCKETS, 0].astype(jnp.int32)
    tiles = (cnt + TM_E - 1) // TM_E
    tile_end = jnp.cumsum(tiles)
    tile_start = tile_end - tiles
    slot = _slots((tile_start * TM_E).astype(jnp.int32), bucket, rank)
    j = jnp.arange(N_TILES_E, dtype=jnp.int32)
    n_live = tile_end[-1]
    j_live = jnp.minimum(j, n_live - 1)
    b = jnp.sum((tile_end[None, :] <= j_live[:, None]).astype(jnp.int32), axis=1)
    b = jnp.minimum(b, N_BUCKETS - 1)
    rows = jnp.clip(cnt[b] - (j - tile_start[b]) * TM_E, 0, TM_E)
    rows = jnp.where(j < n_live, rows, 0).astype(jnp.int32)
    grp = b // N_PAIRS
    pair = b % N_PAIRS
    ea = grp * EPG + jnp.asarray(_SLOT_A, jnp.int32)[pair]
    eb = grp * EPG + jnp.asarray(_SLOT_B, jnp.int32)[pair]
    a_higher = jnp.asarray(_SLOT_A_IS_HIGHER, jnp.int32)[pair]
    return slot, (ea, eb, _next_different(ea), _next_different(eb), a_higher, rows)


def _moe(a, x_parts, mod, norm_g2, w_out, router_wt, router_bt, w_gate, w_up, w_down, layer, final_g,
         *, final):
    x1, gates, bucket, rank, counts = _post_mixer(a, x_parts, mod, norm_g2, w_out, router_wt, router_bt)
    slot, tile_plan = _bucket_plan(bucket, rank, counts)
    partial = (tile_plan[-1] < TM_E).astype(jnp.int32)
    xs = _dispatch(partial, slot, x1, mod, norm_g2, gates)
    ys = _experts(*tile_plan, xs, w_gate, w_up, w_down, layer)
    return _combine(slot, x1, mod, final_g, ys, final=final)


def kernel(x_prompt, x_sample, state_hgrn, c, c_ctx, w_mod, b_mod, norm_mix_g, norm_ffn_g, norm_final_g,
           hgrn_w_in, hgrn_lb_logits, hgrn_norm_g, hgrn_w_out, conv_w_in, conv_w, conv_w_out,
           router_w, router_b, moe_w_gate, moe_w_up, moe_w_down):
    x_parts = (x_prompt.reshape(T_PROMPT, D), x_sample.reshape(T_SAMPLE, D))
    cond = jnp.concatenate([c_ctx[None], c, jnp.zeros((COND_ROWS - N_COND, D), F32)], axis=0)
    mods = _adaln(cond, w_mod, b_mod)[:, :N_COND].reshape(2, N_COND, 6, 1, D)
    member_major = lambda w: w.reshape(N_GROUPS, EPG, -1).transpose(1, 0, 2).reshape(N_EXPERTS, -1)
    rwt = member_major(router_w.T)
    rbt = member_major(router_b.reshape(N_EXPERTS, 1))
    final_g = norm_final_g.reshape(1, D)
    bf = lambda w: w.astype(BF16)

    q, v, ff, fb, gg = _hgrn_proj(x_parts, mods[0], norm_mix_g[0:1], bf(hgrn_w_in[0]))
    s0 = state_hgrn[:, 0]
    o_f, s_f = _gla_fwd(q, v, ff, hgrn_lb_logits, s0)
    a, s_b = _gla_bwd(q, v, fb, hgrn_lb_logits, s0, o_f, gg, hgrn_norm_g[0:1])
    (x,) = _moe(a, x_parts, mods[0], norm_ffn_g[0:1], bf(hgrn_w_out[0]), rwt, rbt,
                moe_w_gate, moe_w_up, moe_w_down, 0, final_g, final=False)

    a = _conv_mixer(x, mods[1], norm_mix_g[1:2], bf(conv_w_in[0]), conv_w[0])
    y_p, y_s = _moe(a, (x,), mods[1], norm_ffn_g[1:2], bf(conv_w_out[0]), rwt, rbt,
                    moe_w_gate, moe_w_up, moe_w_down, 1, final_g, final=True)

    new_state = jnp.stack([s_f, s_b], axis=1)[:, None]
    return (y_p.reshape(N_PROMPT_SEQ, PROMPT_LEN, D), y_s.reshape(N_SAMPLE_SEQ, SAMPLE_LEN, D), new_state)
```

```python
import functools

import jax
import jax.numpy as jnp
from jax import lax
from jax.experimental import pallas as pl
from jax.experimental.pallas import tpu as pltpu

F32 = jnp.float32
BF16 = jnp.bfloat16

D = 1024
N_PROMPT_SEQ = 16
PROMPT_LEN = 256
N_SAMPLE_SEQ = 2
SAMPLE_LEN = 4096
GRID_W = 64
T_PROMPT = N_PROMPT_SEQ * PROMPT_LEN
T_SAMPLE = N_SAMPLE_SEQ * SAMPLE_LEN
T = T_PROMPT + T_SAMPLE
N_COND = 1 + N_SAMPLE_SEQ
COND_ROWS = 8

HEADS = 8
DK = 128
DV = 128
CHUNK = 64
N_EXPERTS = 16
N_GROUPS = 4
EPG = 4
N_PAIRS = 6
N_BUCKETS = N_GROUPS * N_PAIRS
D_FF = 512
EPS = 1e-6

TM = 256
N_TILES = T // TM
PROMPT_TILES = T_PROMPT // TM
TILES_PER_SAMPLE = SAMPLE_LEN // TM
CHUNKS_PER_TILE = TM // CHUNK
TM_E = 256
N_TILES_E = T // TM_E + N_BUCKETS
S_ROWS = N_TILES_E * TM_E
LANES = 128
D_TILES = D // LANES
X_TILES = D_TILES + 1
MOD_TN = 1536

VMEM_LIMIT = 56 * 1024 * 1024

_SLOT_A = (0, 3, 3, 0, 0, 1)
_SLOT_B = (1, 1, 2, 2, 3, 2)
_SLOT_A_IS_HIGHER = tuple(int(a > b) for a, b in zip(_SLOT_A, _SLOT_B))


def _cparams():
    return pltpu.CompilerParams(dimension_semantics=("arbitrary",), vmem_limit_bytes=VMEM_LIMIT)


def _cond_of_tile(t):
    return jnp.where(t < PROMPT_TILES, 0, 1 + (t - PROMPT_TILES) // TILES_PER_SAMPLE)


def _sample_of_tile(t):
    return jnp.clip((t - PROMPT_TILES) // TILES_PER_SAMPLE, 0, N_SAMPLE_SEQ - 1)


def _rmsnorm(x, g):
    ms = jnp.mean(x * x, axis=-1, keepdims=True)
    return (x * lax.rsqrt(ms + EPS)) * g


def _modulate(x, g, shift, scale):
    return _rmsnorm(x, g) * (1.0 + scale) + shift


def _silu(x):
    return x * jax.nn.sigmoid(x)


def _dot(a, b):
    return jnp.dot(a, b, preferred_element_type=F32)


def _dot_nt(a, b):
    return lax.dot_general(a, b, (((1,), (1,)), ((), ())), preferred_element_type=F32)


def _dot_tn(a, b):
    return lax.dot_general(a, b, (((0,), (0,)), ((), ())), preferred_element_type=F32)


def _split_bf16(x):
    hi = x.astype(BF16)
    lo = (x - hi.astype(F32)).astype(BF16)
    return hi, lo


def _mod_kernel(cond_ref, w_ref, b_ref, o_ref):
    s = _silu(cond_ref[...])
    o_ref[0] = _dot(s.astype(BF16), w_ref[0].astype(BF16)) + b_ref[0]


def _adaln(cond, w_mod, b_mod):
    depth = w_mod.shape[0]
    return pl.pallas_call(
        _mod_kernel,
        grid=(depth, 6 * D // MOD_TN),
        in_specs=[
            pl.BlockSpec((COND_ROWS, D), lambda i, j: (0, 0)),
            pl.BlockSpec((1, D, MOD_TN), lambda i, j: (i, 0, j)),
            pl.BlockSpec((1, 1, MOD_TN), lambda i, j: (i, 0, j)),
        ],
        out_specs=pl.BlockSpec((1, COND_ROWS, MOD_TN), lambda i, j: (i, 0, j)),
        out_shape=jax.ShapeDtypeStruct((depth, COND_ROWS, 6 * D), F32),
        compiler_params=pltpu.CompilerParams(
            dimension_semantics=("arbitrary", "arbitrary"), vmem_limit_bytes=VMEM_LIMIT),
        name="adaln",
    )(cond, w_mod, b_mod.reshape(depth, 1, 6 * D))


_MOD_SPEC = pl.BlockSpec((1, 6, 1, D), lambda i: (_cond_of_tile(i), 0, 0, 0))
_ROW_SPEC = pl.BlockSpec((1, D), lambda i: (0, 0))
_TILE_SPEC = pl.BlockSpec((TM, D), lambda i: (i, 0))


def _token_tile(x_refs, rows=slice(None), prompt_steps=PROMPT_TILES):
    if len(x_refs) == 1:
        return x_refs[0][rows]
    return jnp.where(pl.program_id(0) < prompt_steps, x_refs[0][rows], x_refs[1][rows])


def _token_specs(x_parts, tiles_per_step=1):
    rows = tiles_per_step * TM
    if len(x_parts) == 1:
        return [pl.BlockSpec((rows, D), lambda i: (i, 0))]
    prompt_steps = T_PROMPT // rows
    return [pl.BlockSpec((rows, D), lambda i: (jnp.minimum(i, prompt_steps - 1), 0)),
            pl.BlockSpec((rows, D), lambda i: (jnp.maximum(i - prompt_steps, 0), 0))]


def _hgrn_proj_kernel(xp_ref, xs_ref, mod_ref, g_ref, w_ref, q_ref, v_ref, ff_ref, fb_ref, gg_ref):
    x = _token_tile((xp_ref, xs_ref))
    h = _modulate(x, g_ref[...], mod_ref[0, 0], mod_ref[0, 1]).astype(BF16)
    q_ref[...] = (_dot(h, w_ref[:, 0 * D:1 * D]) * (DK ** -0.5)).astype(BF16)
    v_ref[...] = _dot(h, w_ref[:, 1 * D:2 * D]).astype(BF16)
    ff_ref[...] = _dot(h, w_ref[:, 2 * D:3 * D])
    fb_ref[...] = _dot(h, w_ref[:, 3 * D:4 * D])
    gg_ref[...] = _dot(h, w_ref[:, 4 * D:5 * D]).astype(BF16)


def _hgrn_proj(x_parts, mod, norm_g, w_in):
    bf = jax.ShapeDtypeStruct((T, D), BF16)
    f32 = jax.ShapeDtypeStruct((T, D), F32)
    return pl.pallas_call(
        _hgrn_proj_kernel,
        grid=(N_TILES,),
        in_specs=_token_specs(x_parts) + [_MOD_SPEC, _ROW_SPEC, pl.BlockSpec((D, 5 * D), lambda i: (0, 0))],
        out_specs=[_TILE_SPEC] * 5,
        out_shape=[bf, bf, f32, f32, bf],
        compiler_params=_cparams(),
        name="hgrn_proj",
    )(*x_parts, mod, norm_g, w_in)


def _gla_tile(t, q_ref, v_ref, f_ref, lbl_ref, s0_ref, st_ref, ma_ref, o_ref, *, reverse):
    is_prompt = t < PROMPT_TILES
    rel = t - PROMPT_TILES
    edge = (TILES_PER_SAMPLE - 1) if reverse else 0
    sample_start = jnp.logical_and(rel >= 0, rel % TILES_PER_SAMPLE == edge)

    @pl.when(is_prompt)
    def _():
        st_ref[...] = jnp.zeros_like(st_ref)

    @pl.when(sample_start)
    def _():
        for h in range(HEADS):
            st_ref[h] = s0_ref[0, 0, h].T

    logits = lbl_ref[...]
    e = jnp.exp(logits - jnp.max(logits, axis=0, keepdims=True))
    lb = e[0:1] / jnp.sum(e, axis=0, keepdims=True)

    ref_idx = (CHUNK - 1 - CHUNK // 2) if reverse else CHUNK // 2
    last_idx = 0 if reverse else CHUNK - 1
    shift = CHUNK.bit_length() - 1

    row = lax.broadcasted_iota(jnp.int32, (TM, TM), 0)
    col = lax.broadcasted_iota(jnp.int32, (TM, TM), 1)
    same = (row >> shift) == (col >> shift)
    ref_row = ((row >> shift) << shift) + ref_idx
    keep = jnp.logical_and(same, (row <= col) if reverse else (row >= col))

    @pl.when(pl.program_id(0) == 0)
    def _():
        at_ref = jnp.logical_and(same, (ref_row <= col) if reverse else (ref_row >= col))
        ma_ref[...] = (keep.astype(F32) - at_ref.astype(F32)).astype(BF16)

    ma = ma_ref[...]

    rr = lax.broadcasted_iota(jnp.int32, (2 * CHUNKS_PER_TILE, TM), 0)
    cc = lax.broadcasted_iota(jnp.int32, (2 * CHUNKS_PER_TILE, TM), 1)
    chunk = rr & (CHUNKS_PER_TILE - 1)
    target = chunk * CHUNK + jnp.where(rr < CHUNKS_PER_TILE, ref_idx, last_idx)
    covered = (cc >= target) if reverse else (cc <= target)
    rm = jnp.logical_and((cc >> shift) == chunk, covered).astype(BF16)

    chunk_rows = [slice(c * CHUNK, (c + 1) * CHUNK) for c in range(CHUNKS_PER_TILE)]
    order = range(CHUNKS_PER_TILE - 1, -1, -1) if reverse else range(CHUNKS_PER_TILE)
    pair_w = 2 * DK

    def per_chunk_scale(x, scale):
        return jnp.concatenate([x[rs] * scale[c:c + 1] for c, rs in enumerate(chunk_rows)], axis=0)

    def pair_cols(p):
        return slice(p * pair_w, (p + 1) * pair_w)

    def stage_decay(p):
        lbp = lb[:, pair_cols(p)]
        f = lbp + (1.0 - lbp) * jax.nn.sigmoid(f_ref[:, pair_cols(p)])
        lf_hi, lf_lo = _split_bf16(jnp.log(f))
        z = _dot(ma, lf_hi) + _dot(ma, lf_lo)
        marks = _dot(rm, lf_hi) + _dot(rm, lf_lo)
        return 1.0 - f, z, marks

    def stage_scores(p, k, z, marks):
        ref, last = marks[:CHUNKS_PER_TILE], marks[CHUNKS_PER_TILE:]
        qa32 = q_ref[:, pair_cols(p)].astype(F32) * jnp.exp(z)
        kb32 = k * jnp.exp(-z)
        qa = qa32.astype(BF16)
        kb = kb32.astype(BF16)
        qe = per_chunk_scale(qa32, jnp.exp(ref)).astype(BF16)
        kd = per_chunk_scale(kb32, jnp.exp(last - ref)).astype(BF16)
        v = v_ref[:, pair_cols(p)]
        scores, incr = [], []
        for hh in range(2):
            ls = slice(hh * DK, (hh + 1) * DK)
            scores.append(_dot_nt(qa[:, ls], kb[:, ls]))
            kd_h = kd[:, ls]
            kd_blocks = jnp.concatenate(
                [jnp.concatenate([part for part in (jnp.zeros((c * CHUNK, DK), BF16), kd_h[rs],
                                                    jnp.zeros((TM - (c + 1) * CHUNK, DK), BF16))
                                  if part.shape[0]], axis=0)
                 for c, rs in enumerate(chunk_rows)], axis=1)
            incr.append(_dot_tn(v[:, ls], kd_blocks))
        return scores, incr, qe, jnp.exp(last), v

    def stage_output(p, scores, incr, qe, decay, v):
        for hh in range(2):
            h = 2 * p + hh
            ls = slice(hh * DK, (hh + 1) * DK)
            hs = slice(h * DK, (h + 1) * DK)
            o_intra = _dot(jnp.where(keep, scores[hh], 0.0).astype(BF16), v[:, ls])
            st = st_ref[h]
            for c in order:
                rs = chunk_rows[c]
                o_ref[rs, hs] = o_intra[rs] + _dot_nt(qe[rs, ls], st.astype(BF16))
                st = st * decay[c:c + 1, ls] + incr[hh][:, c * DK:(c + 1) * DK]
            st_ref[h] = st

    n_pairs = HEADS // 2
    decayed, scored = {}, {}
    for step in range(n_pairs + 2):
        if step < n_pairs:
            decayed[step] = stage_decay(step)
        if 0 <= step - 1 < n_pairs:
            scored[step - 1] = stage_scores(step - 1, *decayed.pop(step - 1))
        if 0 <= step - 2 < n_pairs:
            stage_output(step - 2, *scored.pop(step - 2))


def _store_prompt_state(t, st_ref, snew_ref):
    @pl.when(t < PROMPT_TILES)
    def _():
        for h in range(HEADS):
            snew_ref[0, h] = st_ref[h].T


def _gla_fwd_kernel(q_ref, v_ref, f_ref, lbl_ref, s0_ref, o_ref, snew_ref, st_ref, ma_ref):
    t = pl.program_id(0)
    _gla_tile(t, q_ref, v_ref, f_ref, lbl_ref, s0_ref, st_ref, ma_ref, o_ref, reverse=False)
    _store_prompt_state(t, st_ref, snew_ref)


def _gla_bwd_kernel(q_ref, v_ref, f_ref, lbl_ref, s0_ref, of_ref, gg_ref, ng_ref,
                    a_ref, snew_ref, st_ref, ma_ref, ob_ref):
    t = N_TILES - 1 - pl.program_id(0)
    _gla_tile(t, q_ref, v_ref, f_ref, lbl_ref, s0_ref, st_ref, ma_ref, ob_ref, reverse=True)
    _store_prompt_state(t, st_ref, snew_ref)
    ng = ng_ref[...]
    for h in range(HEADS):
        hs = slice(h * DV, (h + 1) * DV)
        o = of_ref[:, hs] + ob_ref[:, hs]
        a_ref[:, hs] = (_rmsnorm(o, ng) * _silu(gg_ref[:, hs].astype(F32))).astype(BF16)


def _state_specs(direction, tile_of_step):
    s0 = pl.BlockSpec((1, 1, HEADS, DK, DV),
                      lambda i: (_sample_of_tile(tile_of_step(i)), direction, 0, 0, 0))
    snew = pl.BlockSpec((1, HEADS, DK, DV),
                        lambda i: (jnp.minimum(tile_of_step(i), PROMPT_TILES - 1), 0, 0, 0))
    return s0, snew


def _gla_fwd(q, v, ff, lb_logits, s0):
    s0_spec, snew_spec = _state_specs(0, lambda i: i)
    return pl.pallas_call(
        _gla_fwd_kernel,
        grid=(N_TILES,),
        in_specs=[_TILE_SPEC, _TILE_SPEC, _TILE_SPEC,
                  pl.BlockSpec(lb_logits.shape, lambda i: (0, 0)), s0_spec],
        out_specs=[_TILE_SPEC, snew_spec],
        out_shape=[jax.ShapeDtypeStruct((T, D), F32),
                   jax.ShapeDtypeStruct((N_PROMPT_SEQ, HEADS, DK, DV), F32)],
        scratch_shapes=[pltpu.VMEM((HEADS, DV, DK), F32), pltpu.VMEM((TM, TM), BF16)],
        compiler_params=_cparams(),
        name="gla_fwd",
    )(q, v, ff, lb_logits, s0)


def _gla_bwd(q, v, fb, lb_logits, s0, o_f, gg, head_norm_g):
    rev = lambda i: N_TILES - 1 - i
    tile = pl.BlockSpec((TM, D), lambda i: (rev(i), 0))
    s0_spec, snew_spec = _state_specs(1, rev)
    return pl.pallas_call(
        _gla_bwd_kernel,
        grid=(N_TILES,),
        in_specs=[tile, tile, tile, pl.BlockSpec(lb_logits.shape, lambda i: (0, 0)), s0_spec,
                  tile, tile, pl.BlockSpec((1, DV), lambda i: (0, 0))],
        out_specs=[tile, snew_spec],
        out_shape=[jax.ShapeDtypeStruct((T, D), BF16),
                   jax.ShapeDtypeStruct((N_PROMPT_SEQ, HEADS, DK, DV), F32)],
        scratch_shapes=[pltpu.VMEM((HEADS, DV, DK), F32), pltpu.VMEM((TM, TM), BF16),
                        pltpu.VMEM((TM, D), F32)],
        compiler_params=_cparams(),
        name="gla_bwd",
    )(q, v, fb, lb_logits, s0, o_f, gg, head_norm_g)


def _conv_mixer_kernel(x_ref, mod_ref, g_ref, w_ref, cw_ref, a_ref):
    i = pl.program_id(0)
    h = _modulate(x_ref[...], g_ref[...], mod_ref[0, 0], mod_ref[0, 1]).astype(BF16)
    bg = _dot(h, w_ref[:, 0 * D:1 * D])
    u = _dot(h, w_ref[:, 1 * D:2 * D]) * _dot(h, w_ref[:, 2 * D:3 * D])
    seg = jnp.where(i < PROMPT_TILES, PROMPT_LEN, GRID_W)
    pos = lax.broadcasted_iota(jnp.int32, (TM, 1), 0) & (seg - 1)
    prev = jnp.where(pos == 0, 0.0, pltpu.roll(u, 1, axis=0))
    nxt = jnp.where(pos == seg - 1, 0.0, pltpu.roll(u, TM - 1, axis=0))
    conv = cw_ref[0:1] * prev + cw_ref[1:2] * u + cw_ref[2:3] * nxt
    a_ref[...] = (bg * conv).astype(BF16)


def _conv_mixer(x, mod, norm_g, w_in, conv_w):
    return pl.pallas_call(
        _conv_mixer_kernel,
        grid=(N_TILES,),
        in_specs=[_TILE_SPEC, _MOD_SPEC, _ROW_SPEC, pl.BlockSpec((D, 3 * D), lambda i: (0, 0)),
                  pl.BlockSpec((3, D), lambda i: (0, 0))],
        out_specs=_TILE_SPEC,
        out_shape=jax.ShapeDtypeStruct((T, D), BF16),
        compiler_params=_cparams(),
        name="conv_mixer",
    )(x, mod, norm_g, w_in, conv_w)


ROUTE_ROWS = 32
ROUTE_TILES = 2


def _first_member(vals, target):
    idx = jnp.full_like(target, float(EPG - 1))
    for j in range(EPG - 2, -1, -1):
        idx = jnp.where(vals[j] == target, float(j), idx)
    return idx


def _router_logits(h2, rwt_ref):
    h_hi, h_lo = _split_bf16(h2)
    w_hi, w_lo = _split_bf16(rwt_ref[...])
    return _dot_nt(w_hi, h_hi) + (_dot_nt(w_hi, h_lo) + _dot_nt(w_lo, h_hi))


def _select_experts(logits, rb_ref):
    scores = jax.nn.sigmoid(logits)
    sel = scores + rb_ref[...]
    member = [sel[j * N_GROUPS:(j + 1) * N_GROUPS] for j in range(EPG)]
    m1 = functools.reduce(jnp.maximum, member)
    i1 = _first_member(member, m1)
    rest = [jnp.where(i1 == float(j), -jnp.inf, member[j]) for j in range(EPG)]
    m2 = functools.reduce(jnp.maximum, rest)
    i2 = _first_member(rest, m2)
    group_score = m1 + m2

    best, grp, first, second = group_score[0:1], jnp.zeros((1, TM), F32), i1[0:1], i2[0:1]
    for g in range(1, N_GROUPS):
        upd = group_score[g:g + 1] > best
        best = jnp.where(upd, group_score[g:g + 1], best)
        grp = jnp.where(upd, float(g), grp)
        first = jnp.where(upd, i1[g:g + 1], first)
        second = jnp.where(upd, i2[g:g + 1], second)
    a = jnp.minimum(first, second)
    b = jnp.maximum(first, second)
    row = lax.broadcasted_iota(jnp.int32, (N_EXPERTS, TM), 0).astype(F32)
    s_lo = jnp.sum(jnp.where(row == a * N_GROUPS + grp, scores, 0.0), axis=0, keepdims=True)
    s_hi = jnp.sum(jnp.where(row == b * N_GROUPS + grp, scores, 0.0), axis=0, keepdims=True)
    tot = s_lo + s_hi
    pair = jnp.where(a == 0.0, jnp.where(b == 1.0, 0.0, jnp.where(b == 2.0, 3.0, 4.0)),
                     jnp.where(a == 1.0, jnp.where(b == 2.0, 5.0, 1.0), 2.0))
    return grp * N_PAIRS + pair, s_lo / tot, s_hi / tot


def _post_mixer_kernel(*refs, n_x):
    a_ref, x_refs = refs[0], refs[1:1 + n_x]
    (mod_ref, g2_ref, wo_ref, rwt_ref, rb_ref,
     x1_ref, gates_ref, bucket_ref, rank_ref, cnt_ref, carry_ref, earlier_ref) = refs[1 + n_x:]

    @pl.when(pl.program_id(0) == 0)
    def _():
        carry_ref[...] = jnp.zeros_like(carry_ref)
        s_idx = lax.broadcasted_iota(jnp.int32, (TM, TM), 0)
        t_idx = lax.broadcasted_iota(jnp.int32, (TM, TM), 1)
        earlier_ref[...] = (s_idx < t_idx).astype(BF16)

    def stage_mix(s):
        rows = slice(s * TM, (s + 1) * TM)
        x1 = (_token_tile(x_refs, rows, PROMPT_TILES // ROUTE_TILES)
              + mod_ref[0, 2] * _dot(a_ref[rows], wo_ref[...]))
        x1_ref[rows] = x1
        return _modulate(x1, g2_ref[...], mod_ref[0, 3], mod_ref[0, 4])

    def stage_rank(s, bucket, g_lo, g_hi):
        onehot = lax.broadcasted_iota(jnp.int32, (ROUTE_ROWS, TM), 0).astype(F32) == bucket
        before = _dot(onehot.astype(BF16), earlier_ref[...])
        oh = onehot.astype(F32)
        carry = carry_ref[...]
        rank = jnp.sum(oh * (before + carry[:, 0:1]), axis=0, keepdims=True)
        carry_ref[...] = carry + jnp.sum(oh, axis=1, keepdims=True)
        bucket_ref[s] = bucket.astype(jnp.int32)
        rank_ref[s] = rank.astype(jnp.int32)
        grow = lax.broadcasted_iota(jnp.int32, (LANES, TM), 0)
        gates_ref[s * TM:(s + 1) * TM] = jnp.where(grow == 0, g_lo, jnp.where(grow == 1, g_hi, 0.0)).T

    tiles = range(ROUTE_TILES)
    h2 = [stage_mix(s) for s in tiles]
    logits = [_router_logits(h2[s], rwt_ref) for s in tiles]
    picks = [_select_experts(logits[s], rb_ref) for s in tiles]
    for s in tiles:
        stage_rank(s, *picks[s])
    cnt_ref[...] = carry_ref[...]


def _post_mixer(a, x_parts, mod, norm_g2, w_out, router_wt, router_bt):
    rows = ROUTE_TILES * TM
    idx_spec = pl.BlockSpec((ROUTE_TILES, 1, TM), lambda i: (i, 0, 0))
    return pl.pallas_call(
        functools.partial(_post_mixer_kernel, n_x=len(x_parts)),
        grid=(N_TILES // ROUTE_TILES,),
        in_specs=[pl.BlockSpec((rows, D), lambda i: (i, 0))] + _token_specs(x_parts, ROUTE_TILES) + [
            pl.BlockSpec((1, 6, 1, D), lambda i: (_cond_of_tile(i * ROUTE_TILES), 0, 0, 0)), _ROW_SPEC,
            pl.BlockSpec((D, D), lambda i: (0, 0)),
            pl.BlockSpec((N_EXPERTS, D), lambda i: (0, 0)),
            pl.BlockSpec((N_EXPERTS, 1), lambda i: (0, 0))],
        out_specs=[pl.BlockSpec((rows, D), lambda i: (i, 0)), pl.BlockSpec((rows, LANES), lambda i: (i, 0)),
                   idx_spec, idx_spec, pl.BlockSpec((ROUTE_ROWS, LANES), lambda i: (0, 0))],
        out_shape=[jax.ShapeDtypeStruct((T, D), F32), jax.ShapeDtypeStruct((T, LANES), F32),
                   jax.ShapeDtypeStruct((N_TILES, 1, TM), jnp.int32),
                   jax.ShapeDtypeStruct((N_TILES, 1, TM), jnp.int32),
                   jax.ShapeDtypeStruct((ROUTE_ROWS, LANES), F32)],
        scratch_shapes=[pltpu.VMEM((ROUTE_ROWS, LANES), F32), pltpu.VMEM((TM, TM), BF16)],
        compiler_params=_cparams(),
        name="post_mixer",
    )(a, *x_parts, mod, norm_g2, w_out, router_wt, router_bt)


def _smem_tile_spec(tile_of_step):
    return pl.BlockSpec((1, 1, TM), lambda i, *_: (tile_of_step(i), 0, 0), memory_space=pltpu.SMEM)


def _start_rows(n, make_copy):
    for r in range(n):
        make_copy(r).start(priority=r % 2)


def _dispatch_kernel(partial_ref, slot_ref, x1_ref, mod_ref, g2_ref, gates_ref,
                     xs_ref, buf_ref, zero_ref, sems, zero_sem):
    i = pl.program_id(0)
    cur = i % 2

    @pl.when(i == 0)
    def _():
        zero_ref[...] = jnp.zeros_like(zero_ref)

        def zero_copy(j):
            return pltpu.make_async_copy(zero_ref, xs_ref.at[pl.ds(j * TM_E, TM_E)], zero_sem)

        def start(j, carry):
            @pl.when(partial_ref[j] != 0)
            def _():
                zero_copy(j).start()
            return carry

        def wait(j, carry):
            @pl.when(partial_ref[j] != 0)
            def _():
                zero_copy(j).wait()
            return carry

        lax.fori_loop(0, N_TILES_E, start, 0)
        lax.fori_loop(0, N_TILES_E, wait, 0)

    def wait_tile(b):
        pltpu.make_async_copy(buf_ref.at[b], xs_ref.at[pl.ds(0, TM)], sems.at[b]).wait()

    h2 = _modulate(x1_ref[...], g2_ref[...], mod_ref[0, 3], mod_ref[0, 4])
    gates = gates_ref[...]

    def send(b):
        buf_ref[b, :, :D_TILES, :] = h2.reshape(TM, D_TILES, LANES)
        buf_ref[b, :, D_TILES:, :] = gates.reshape(TM, 1, LANES)
        _start_rows(TM, lambda r: pltpu.make_async_copy(
            buf_ref.at[b, pl.ds(r, 1)], xs_ref.at[pl.ds(slot_ref[0, 0, r], 1)], sems.at[b]))

        @pl.when(i > 0)
        def _():
            wait_tile(1 - b)

        @pl.when(i == N_TILES - 1)
        def _():
            wait_tile(b)

    for b in range(2):
        pl.when(cur == b)(functools.partial(send, b))


def _dispatch(partial, slot, x1, mod, norm_g2, gates):
    tile = lambda w: pl.BlockSpec((TM, w), lambda i, *_: (i, 0))
    return pl.pallas_call(
        _dispatch_kernel,
        grid_spec=pltpu.PrefetchScalarGridSpec(
            num_scalar_prefetch=1,
            grid=(N_TILES,),
            in_specs=[_smem_tile_spec(lambda i: i), tile(D),
                      pl.BlockSpec((1, 6, 1, D), lambda i, *_: (_cond_of_tile(i), 0, 0, 0)),
                      pl.BlockSpec((1, D), lambda i, *_: (0, 0)), tile(LANES)],
            out_specs=pl.BlockSpec(memory_space=pl.ANY),
            scratch_shapes=[pltpu.VMEM((2, TM, X_TILES, LANES), F32), pltpu.VMEM((TM_E, X_TILES, LANES), F32),
                            pltpu.SemaphoreType.DMA((2,)), pltpu.SemaphoreType.DMA(())],
        ),
        out_shape=jax.ShapeDtypeStruct((S_ROWS, X_TILES, LANES), F32),
        compiler_params=_cparams(),
        name="moe_dispatch",
    )(partial, slot, x1, mod, norm_g2, gates)


_W_SHAPES = ((D, D_FF), (D, D_FF), (D_FF, D))


def _expert_kernel(ea_ref, eb_ref, next_a_ref, next_b_ref, a_higher_ref, nrows_ref,
                   x_ref, wg_hbm, wu_hbm, wd_hbm, y_ref, *scratch, layer):
    stage, w16, sems = scratch[0:3], scratch[3:6], scratch[6]
    j = pl.program_id(0)
    n = nrows_ref[j]
    prev = jnp.maximum(j - 1, 0)

    def fetch(slot, expert):
        return [pltpu.make_async_copy(w.at[layer, expert], s.at[slot], sems.at[slot])
                for w, s in zip((wg_hbm, wu_hbm, wd_hbm), stage)]

    for slot, e_ref, next_ref in ((0, ea_ref, next_a_ref), (1, eb_ref, next_b_ref)):
        @pl.when(j == 0)
        def _(slot=slot, e_ref=e_ref):
            for copy in fetch(slot, e_ref[0]):
                copy.start()

        @pl.when(jnp.logical_or(j == 0, e_ref[j] != e_ref[prev]))
        def _(slot=slot, e_ref=e_ref, next_ref=next_ref):
            for copy in fetch(slot, e_ref[j]):
                copy.wait()
            for s, d in zip(stage, w16):
                d[slot] = s[slot].astype(BF16)

            @pl.when(next_ref[j] >= 0)
            def _():
                for copy in fetch(slot, next_ref[j]):
                    copy.start()

    def run(rows):
        valid = lax.broadcasted_iota(jnp.int32, (rows, 1, 1), 0) < n
        xe = jnp.where(valid, x_ref[:rows], 0.0)
        x = xe[:, :D_TILES, :].reshape(rows, D).astype(BF16)
        gates = xe[:, D_TILES:, :].reshape(rows, LANES)
        g_lo, g_hi = gates[:, 0:1], gates[:, 1:2]
        a_higher = a_higher_ref[j] != 0

        def ffn(slot, gate):
            wg_ref, wu_ref, wd_ref = w16
            act = _silu(_dot(x, wg_ref[slot])) * _dot(x, wu_ref[slot]) * gate
            return _dot(act.astype(BF16), wd_ref[slot])

        y = ffn(0, jnp.where(a_higher, g_hi, g_lo)) + ffn(1, jnp.where(a_higher, g_lo, g_hi))
        y_ref[:rows] = y.reshape(rows, D_TILES, LANES)
        if rows < TM_E:
            y_ref[rows:] = jnp.zeros((TM_E - rows, D_TILES, LANES), F32)

    half = TM_E // 2
    pl.when(n > half)(functools.partial(run, TM_E))
    pl.when(jnp.logical_and(n > 0, n <= half))(functools.partial(run, half))

    @pl.when(n == 0)
    def _():
        y_ref[...] = jnp.zeros_like(y_ref)


def _experts(tile_ea, tile_eb, tile_next_a, tile_next_b, tile_a_higher, tile_rows, xs,
             w_gate, w_up, w_down, layer):
    any_spec = pl.BlockSpec(memory_space=pl.ANY)
    return pl.pallas_call(
        functools.partial(_expert_kernel, layer=layer),
        grid_spec=pltpu.PrefetchScalarGridSpec(
            num_scalar_prefetch=6,
            grid=(N_TILES_E,),
            in_specs=[pl.BlockSpec((TM_E, X_TILES, LANES), lambda j, *_: (j, 0, 0)),
                      any_spec, any_spec, any_spec],
            out_specs=pl.BlockSpec((TM_E, D_TILES, LANES), lambda j, *_: (j, 0, 0)),
            scratch_shapes=[pltpu.VMEM((2,) + s, F32) for s in _W_SHAPES]
                           + [pltpu.VMEM((2,) + s, BF16) for s in _W_SHAPES]
                           + [pltpu.SemaphoreType.DMA((2,))],
        ),
        out_shape=jax.ShapeDtypeStruct((S_ROWS, D_TILES, LANES), F32),
        compiler_params=_cparams(),
        name="moe_experts",
    )(tile_ea, tile_eb, tile_next_a, tile_next_b, tile_a_higher, tile_rows, xs, w_gate, w_up, w_down)


def _combine_kernel(slot_ref, next_slot_ref, x1_ref, mod_ref, gf_ref, y_ref, *rest, final):
    out_refs, (buf_ref, sems) = rest[:-2], rest[-2:]
    i = pl.program_id(0)
    cur = i % 2

    def gather(slots, b):
        _start_rows(TM, lambda r: pltpu.make_async_copy(
            y_ref.at[pl.ds(slots[0, 0, r], 1)], buf_ref.at[b, pl.ds(r, 1)], sems.at[b]))

    @pl.when(i == 0)
    def _():
        gather(slot_ref, 0)

    def combine(b):
        @pl.when(i + 1 < N_TILES)
        def _():
            gather(next_slot_ref, 1 - b)

        pltpu.make_async_copy(y_ref.at[pl.ds(0, TM)], buf_ref.at[b], sems.at[b]).wait()
        x2 = x1_ref[...] + mod_ref[0, 5] * buf_ref[b].reshape(TM, D)
        if not final:
            out_refs[0][...] = x2
            return
        y = _rmsnorm(x2, gf_ref[...])
        yp_ref, ys_ref = out_refs

        @pl.when(i < PROMPT_TILES)
        def _():
            yp_ref[...] = y

        @pl.when(i >= PROMPT_TILES)
        def _():
            ys_ref[...] = y

    for b in range(2):
        pl.when(cur == b)(functools.partial(combine, b))


def _combine(slot, x1, mod, final_g, y_sorted, *, final):
    if final:
        out_specs = _token_specs((None, None))
        out_shape = [jax.ShapeDtypeStruct((T_PROMPT, D), F32), jax.ShapeDtypeStruct((T_SAMPLE, D), F32)]
    else:
        out_specs = [_TILE_SPEC]
        out_shape = [jax.ShapeDtypeStruct((T, D), F32)]
    return pl.pallas_call(
        functools.partial(_combine_kernel, final=final),
        grid=(N_TILES,),
        in_specs=[_smem_tile_spec(lambda i: i), _smem_tile_spec(lambda i: jnp.minimum(i + 1, N_TILES - 1)),
                  _TILE_SPEC, _MOD_SPEC, _ROW_SPEC, pl.BlockSpec(memory_space=pl.ANY)],
        out_specs=out_specs,
        scratch_shapes=[pltpu.VMEM((2, TM, D_TILES, LANES), F32), pltpu.SemaphoreType.DMA((2,))],
        out_shape=out_shape,
        compiler_params=_cparams(),
        name="moe_combine_final" if final else "moe_combine",
    )(slot, slot, x1, mod, final_g, y_sorted)


def _slot_kernel(off_ref, bucket_ref, rank_ref, slot_ref):
    bucket = bucket_ref[...]
    slot = rank_ref[...]
    for b in range(N_BUCKETS):
        slot = slot + jnp.where(bucket == b, off_ref[b], 0)
    slot_ref[...] = slot


def _slots(off, bucket, rank):
    whole = pl.BlockSpec(bucket.shape, lambda i, *_: (0, 0, 0))
    return pl.pallas_call(
        _slot_kernel,
        grid_spec=pltpu.PrefetchScalarGridSpec(
            num_scalar_prefetch=1, grid=(1,), in_specs=[whole, whole], out_specs=whole),
        out_shape=jax.ShapeDtypeStruct(bucket.shape, jnp.int32),
        compiler_params=_cparams(),
        name="moe_slots",
    )(off, bucket, rank)


def _next_different(e):
    j = jnp.arange(e.shape[0], dtype=jnp.int32)
    later_diff = jnp.logical_and(e[None, :] != e[:, None], j[None, :] > j[:, None])
    first = jnp.min(jnp.where(later_diff, j[None, :], e.shape[0]), axis=1)
    found = first < e.shape[0]
    return jnp.where(found, e[jnp.minimum(first, e.shape[0] - 1)], -1).astype(jnp.int32)


def _bucket_plan(bucket, rank, counts):
    cnt = counts[:N_BUCKETS, 0].astype(jnp.int32)
    tiles = (cnt + TM_E - 1) // TM_E
    tile_end = jnp.cumsum(tiles)
    tile_start = tile_end - tiles
    slot = _slots((tile_start * TM_E).astype(jnp.int32), bucket, rank)
    j = jnp.arange(N_TILES_E, dtype=jnp.int32)
    n_live = tile_end[-1]
    j_live = jnp.minimum(j, n_live - 1)
    b = jnp.sum((tile_end[None, :] <= j_live[:, None]).astype(jnp.int32), axis=1)
    b = jnp.minimum(b, N_BUCKETS - 1)
    rows = jnp.clip(cnt[b] - (j - tile_start[b]) * TM_E, 0, TM_E)
    rows = jnp.where(j < n_live, rows, 0).astype(jnp.int32)
    grp = b // N_PAIRS
    pair = b % N_PAIRS
    ea = grp * EPG + jnp.asarray(_SLOT_A, jnp.int32)[pair]
    eb = grp * EPG + jnp.asarray(_SLOT_B, jnp.int32)[pair]
    a_higher = jnp.asarray(_SLOT_A_IS_HIGHER, jnp.int32)[pair]
    return slot, (ea, eb, _next_different(ea), _next_different(eb), a_higher, rows)


def _moe(a, x_parts, mod, norm_g2, w_out, router_wt, router_bt, w_gate, w_up, w_down, layer, final_g,
         *, final):
    x1, gates, bucket, rank, counts = _post_mixer(a, x_parts, mod, norm_g2, w_out, router_wt, router_bt)
    slot, tile_plan = _bucket_plan(bucket, rank, counts)
    partial = (tile_plan[-1] < TM_E).astype(jnp.int32)
    xs = _dispatch(partial, slot, x1, mod, norm_g2, gates)
    ys = _experts(*tile_plan, xs, w_gate, w_up, w_down, layer)
    return _combine(slot, x1, mod, final_g, ys, final=final)


def kernel(x_prompt, x_sample, state_hgrn, c, c_ctx, w_mod, b_mod, norm_mix_g, norm_ffn_g, norm_final_g,
           hgrn_w_in, hgrn_lb_logits, hgrn_norm_g, hgrn_w_out, conv_w_in, conv_w, conv_w_out,
           router_w, router_b, moe_w_gate, moe_w_up, moe_w_down):
    x_parts = (x_prompt.reshape(T_PROMPT, D), x_sample.reshape(T_SAMPLE, D))
    cond = jnp.concatenate([c_ctx[None], c, jnp.zeros((COND_ROWS - N_COND, D), F32)], axis=0)
    mods = _adaln(cond, w_mod, b_mod)[:, :N_COND].reshape(2, N_COND, 6, 1, D)
    member_major = lambda w: w.reshape(N_GROUPS, EPG, -1).transpose(1, 0, 2).reshape(N_EXPERTS, -1)
    rwt = member_major(router_w.T)
    rbt = member_major(router_b.reshape(N_EXPERTS, 1))
    final_g = norm_final_g.reshape(1, D)
    bf = lambda w: w.astype(BF16)

    q, v, ff, fb, gg = _hgrn_proj(x_parts, mods[0], norm_mix_g[0:1], bf(hgrn_w_in[0]))
    s0 = state_hgrn[:, 0]
    o_f, s_f = _gla_fwd(q, v, ff, hgrn_lb_logits, s0)
    a, s_b = _gla_bwd(q, v, fb, hgrn_lb_logits, s0, o_f, gg, hgrn_norm_g[0:1])
    (x,) = _moe(a, x_parts, mods[0], norm_ffn_g[0:1], bf(hgrn_w_out[0]), rwt, rbt,
                moe_w_gate, moe_w_up, moe_w_down, 0, final_g, final=False)

    a = _conv_mixer(x, mods[1], norm_mix_g[1:2], bf(conv_w_in[0]), conv_w[0])
    y_p, y_s = _moe(a, (x,), mods[1], norm_ffn_g[1:2], bf(conv_w_out[0]), rwt, rbt,
                    moe_w_gate, moe_w_up, moe_w_down, 1, final_g, final=True)

    new_state = jnp.stack([s_f, s_b], axis=1)[:, None]
    return (y_p.reshape(N_PROMPT_SEQ, PROMPT_LEN, D), y_s.reshape(N_SAMPLE_SEQ, SAMPLE_LEN, D), new_state)
```

```python
import functools

import jax
import jax.numpy as jnp
from jax import lax
from jax.experimental import pallas as pl
from jax.experimental.pallas import tpu as pltpu

F32 = jnp.float32
BF16 = jnp.bfloat16

D = 1024
N_PROMPT_SEQ = 16
PROMPT_LEN = 256
N_SAMPLE_SEQ = 2
SAMPLE_LEN = 4096
GRID_W = 64
T_PROMPT = N_PROMPT_SEQ * PROMPT_LEN
T_SAMPLE = N_SAMPLE_SEQ * SAMPLE_LEN
T = T_PROMPT + T_SAMPLE
N_COND = 1 + N_SAMPLE_SEQ
COND_ROWS = 8

HEADS = 8
DK = 128
DV = 128
CHUNK = 64
N_EXPERTS = 16
N_GROUPS = 4
EPG = 4
N_PAIRS = 6
N_BUCKETS = N_GROUPS * N_PAIRS
D_FF = 512
EPS = 1e-6

TM = 256
N_TILES = T // TM
PROMPT_TILES = T_PROMPT // TM
TILES_PER_SAMPLE = SAMPLE_LEN // TM
CHUNKS_PER_TILE = TM // CHUNK
STEP_TILES = 2
STEP_ROWS = STEP_TILES * TM
N_STEPS = N_TILES // STEP_TILES
PROMPT_STEPS = PROMPT_TILES // STEP_TILES
TM_E = 256
N_TILES_E = T // TM_E + N_BUCKETS
S_ROWS = N_TILES_E * TM_E
XW = D + 128
LANES = 128
MOD_TN = 1536

VMEM_LIMIT = 56 * 1024 * 1024

_SLOT_A = (0, 3, 3, 0, 0, 1)
_SLOT_B = (1, 1, 2, 2, 3, 2)
_SLOT_A_IS_HIGHER = tuple(int(a > b) for a, b in zip(_SLOT_A, _SLOT_B))


def _cparams():
    return pltpu.CompilerParams(dimension_semantics=("arbitrary",), vmem_limit_bytes=VMEM_LIMIT)


def _cond_of_tile(t):
    return jnp.where(t < PROMPT_TILES, 0, 1 + (t - PROMPT_TILES) // TILES_PER_SAMPLE)


def _sample_of_tile(t):
    return jnp.clip((t - PROMPT_TILES) // TILES_PER_SAMPLE, 0, N_SAMPLE_SEQ - 1)


def _rmsnorm(x, g):
    ms = jnp.mean(x * x, axis=-1, keepdims=True)
    return (x * lax.rsqrt(ms + EPS)) * g


def _modulate(x, g, shift, scale):
    return _rmsnorm(x, g) * (1.0 + scale) + shift


def _silu(x):
    return x * jax.nn.sigmoid(x)


def _dot(a, b):
    return jnp.dot(a, b, preferred_element_type=F32)


def _dot_nt(a, b):
    return lax.dot_general(a, b, (((1,), (1,)), ((), ())), preferred_element_type=F32)


def _dot_tn(a, b):
    return lax.dot_general(a, b, (((0,), (0,)), ((), ())), preferred_element_type=F32)


def _split_bf16(x):
    hi = x.astype(BF16)
    lo = (x - hi.astype(F32)).astype(BF16)
    return hi, lo


def _mod_kernel(cond_ref, w_ref, b_ref, o_ref):
    s = _silu(cond_ref[...])
    o_ref[0] = _dot(s.astype(BF16), w_ref[0].astype(BF16)) + b_ref[0]


def _adaln(cond, w_mod, b_mod):
    depth = w_mod.shape[0]
    return pl.pallas_call(
        _mod_kernel,
        grid=(depth, 6 * D // MOD_TN),
        in_specs=[
            pl.BlockSpec((COND_ROWS, D), lambda i, j: (0, 0)),
            pl.BlockSpec((1, D, MOD_TN), lambda i, j: (i, 0, j)),
            pl.BlockSpec((1, 1, MOD_TN), lambda i, j: (i, 0, j)),
        ],
        out_specs=pl.BlockSpec((1, COND_ROWS, MOD_TN), lambda i, j: (i, 0, j)),
        out_shape=jax.ShapeDtypeStruct((depth, COND_ROWS, 6 * D), F32),
        compiler_params=pltpu.CompilerParams(
            dimension_semantics=("arbitrary", "arbitrary"), vmem_limit_bytes=VMEM_LIMIT),
        name="adaln",
    )(cond, w_mod, b_mod.reshape(depth, 1, 6 * D))


_MOD_SPEC = pl.BlockSpec((1, 6, 1, D), lambda i: (_cond_of_tile(i), 0, 0, 0))
_ROW_SPEC = pl.BlockSpec((1, D), lambda i: (0, 0))
_TILE_SPEC = pl.BlockSpec((TM, D), lambda i: (i, 0))
_STEP_MOD_SPEC = pl.BlockSpec((1, 6, 1, D), lambda i, *_: (_cond_of_tile(i * STEP_TILES), 0, 0, 0))
_STEP_SPEC = pl.BlockSpec((STEP_ROWS, D), lambda i, *_: (i, 0))
_SUB_ROWS = tuple(slice(s * TM, (s + 1) * TM) for s in range(STEP_TILES))


def _token_tile(x_refs, rows=slice(None), prompt_steps=PROMPT_TILES):
    if len(x_refs) == 1:
        return x_refs[0][rows]
    return jnp.where(pl.program_id(0) < prompt_steps, x_refs[0][rows], x_refs[1][rows])


def _token_specs(x_parts, tiles_per_step=1):
    rows = tiles_per_step * TM
    if len(x_parts) == 1:
        return [pl.BlockSpec((rows, D), lambda i: (i, 0))]
    prompt_steps = T_PROMPT // rows
    return [pl.BlockSpec((rows, D), lambda i: (jnp.minimum(i, prompt_steps - 1), 0)),
            pl.BlockSpec((rows, D), lambda i: (jnp.maximum(i - prompt_steps, 0), 0))]


def _hgrn_proj_kernel(xp_ref, xs_ref, mod_ref, g_ref, w_ref, q_ref, v_ref, ff_ref, fb_ref, gg_ref):
    h = [_modulate(_token_tile((xp_ref, xs_ref), rows, PROMPT_STEPS),
                   g_ref[...], mod_ref[0, 0], mod_ref[0, 1]).astype(BF16) for rows in _SUB_ROWS]
    for rows, h_tile in zip(_SUB_ROWS, h):
        q_ref[rows] = (_dot(h_tile, w_ref[:, 0 * D:1 * D]) * (DK ** -0.5)).astype(BF16)
        v_ref[rows] = _dot(h_tile, w_ref[:, 1 * D:2 * D]).astype(BF16)
        ff_ref[rows] = _dot(h_tile, w_ref[:, 2 * D:3 * D])
        fb_ref[rows] = _dot(h_tile, w_ref[:, 3 * D:4 * D])
        gg_ref[rows] = _dot(h_tile, w_ref[:, 4 * D:5 * D]).astype(BF16)


def _hgrn_proj(x_parts, mod, norm_g, w_in):
    bf = jax.ShapeDtypeStruct((T, D), BF16)
    f32 = jax.ShapeDtypeStruct((T, D), F32)
    return pl.pallas_call(
        _hgrn_proj_kernel,
        grid=(N_STEPS,),
        in_specs=_token_specs(x_parts, STEP_TILES) + [_STEP_MOD_SPEC, _ROW_SPEC,
                                                      pl.BlockSpec((D, 5 * D), lambda i: (0, 0))],
        out_specs=[_STEP_SPEC] * 5,
        out_shape=[bf, bf, f32, f32, bf],
        compiler_params=_cparams(),
        name="hgrn_proj",
    )(*x_parts, mod, norm_g, w_in)


def _gla_tile(t, q_ref, v_ref, f_ref, lbl_ref, s0_ref, st_ref, ma_ref, o_ref, *, reverse):
    is_prompt = t < PROMPT_TILES
    rel = t - PROMPT_TILES
    edge = (TILES_PER_SAMPLE - 1) if reverse else 0
    sample_start = jnp.logical_and(rel >= 0, rel % TILES_PER_SAMPLE == edge)

    @pl.when(is_prompt)
    def _():
        st_ref[...] = jnp.zeros_like(st_ref)

    @pl.when(sample_start)
    def _():
        for h in range(HEADS):
            st_ref[h] = s0_ref[0, 0, h].T

    logits = lbl_ref[...]
    e = jnp.exp(logits - jnp.max(logits, axis=0, keepdims=True))
    lb = e[0:1] / jnp.sum(e, axis=0, keepdims=True)

    ref_idx = (CHUNK - 1 - CHUNK // 2) if reverse else CHUNK // 2
    last_idx = 0 if reverse else CHUNK - 1
    shift = CHUNK.bit_length() - 1

    row = lax.broadcasted_iota(jnp.int32, (TM, TM), 0)
    col = lax.broadcasted_iota(jnp.int32, (TM, TM), 1)
    same = (row >> shift) == (col >> shift)
    ref_row = ((row >> shift) << shift) + ref_idx
    keep = jnp.logical_and(same, (row <= col) if reverse else (row >= col))

    @pl.when(pl.program_id(0) == 0)
    def _():
        at_ref = jnp.logical_and(same, (ref_row <= col) if reverse else (ref_row >= col))
        ma_ref[...] = (keep.astype(F32) - at_ref.astype(F32)).astype(BF16)

    ma = ma_ref[...]

    rr = lax.broadcasted_iota(jnp.int32, (2 * CHUNKS_PER_TILE, TM), 0)
    cc = lax.broadcasted_iota(jnp.int32, (2 * CHUNKS_PER_TILE, TM), 1)
    chunk = rr & (CHUNKS_PER_TILE - 1)
    target = chunk * CHUNK + jnp.where(rr < CHUNKS_PER_TILE, ref_idx, last_idx)
    covered = (cc >= target) if reverse else (cc <= target)
    rm = jnp.logical_and((cc >> shift) == chunk, covered).astype(BF16)

    chunk_rows = [slice(c * CHUNK, (c + 1) * CHUNK) for c in range(CHUNKS_PER_TILE)]
    order = range(CHUNKS_PER_TILE - 1, -1, -1) if reverse else range(CHUNKS_PER_TILE)
    pair_w = 2 * DK

    def per_chunk_scale(x, scale):
        return jnp.concatenate([x[rs] * scale[c:c + 1] for c, rs in enumerate(chunk_rows)], axis=0)

    def pair_cols(p):
        return slice(p * pair_w, (p + 1) * pair_w)

    def stage_decay(p):
        lbp = lb[:, pair_cols(p)]
        f = lbp + (1.0 - lbp) * jax.nn.sigmoid(f_ref[:, pair_cols(p)])
        lf_hi, lf_lo = _split_bf16(jnp.log(f))
        z = _dot(ma, lf_hi) + _dot(ma, lf_lo)
        marks = _dot(rm, lf_hi) + _dot(rm, lf_lo)
        return 1.0 - f, z, marks

    def stage_scores(p, k, z, marks):
        ref, last = marks[:CHUNKS_PER_TILE], marks[CHUNKS_PER_TILE:]
        qa32 = q_ref[:, pair_cols(p)].astype(F32) * jnp.exp(z)
        kb32 = k * jnp.exp(-z)
        qa = qa32.astype(BF16)
        kb = kb32.astype(BF16)
        qe = per_chunk_scale(qa32, jnp.exp(ref)).astype(BF16)
        kd = per_chunk_scale(kb32, jnp.exp(last - ref)).astype(BF16)
        v = v_ref[:, pair_cols(p)]
        scores, incr = [], []
        for hh in range(2):
            ls = slice(hh * DK, (hh + 1) * DK)
            scores.append(_dot_nt(qa[:, ls], kb[:, ls]))
            kd_h = kd[:, ls]
            kd_blocks = jnp.concatenate(
                [jnp.concatenate([part for part in (jnp.zeros((c * CHUNK, DK), BF16), kd_h[rs],
                                                    jnp.zeros((TM - (c + 1) * CHUNK, DK), BF16))
                                  if part.shape[0]], axis=0)
                 for c, rs in enumerate(chunk_rows)], axis=1)
            incr.append(_dot_tn(v[:, ls], kd_blocks))
        return scores, incr, qe, jnp.exp(last), v

    def stage_output(p, scores, incr, qe, decay, v):
        for hh in range(2):
            h = 2 * p + hh
            ls = slice(hh * DK, (hh + 1) * DK)
            hs = slice(h * DK, (h + 1) * DK)
            o_intra = _dot(jnp.where(keep, scores[hh], 0.0).astype(BF16), v[:, ls])
            st = st_ref[h]
            for c in order:
                rs = chunk_rows[c]
                o_ref[rs, hs] = o_intra[rs] + _dot_nt(qe[rs, ls], st.astype(BF16))
                st = st * decay[c:c + 1, ls] + incr[hh][:, c * DK:(c + 1) * DK]
            st_ref[h] = st

    n_pairs = HEADS // 2
    decayed, scored = {}, {}
    for step in range(n_pairs + 2):
        if step < n_pairs:
            decayed[step] = stage_decay(step)
        if 0 <= step - 1 < n_pairs:
            scored[step - 1] = stage_scores(step - 1, *decayed.pop(step - 1))
        if 0 <= step - 2 < n_pairs:
            stage_output(step - 2, *scored.pop(step - 2))


def _store_prompt_state(t, st_ref, snew_ref):
    @pl.when(t < PROMPT_TILES)
    def _():
        for h in range(HEADS):
            snew_ref[0, h] = st_ref[h].T


def _gla_fwd_kernel(q_ref, v_ref, f_ref, lbl_ref, s0_ref, o_ref, snew_ref, st_ref, ma_ref):
    t = pl.program_id(0)
    _gla_tile(t, q_ref, v_ref, f_ref, lbl_ref, s0_ref, st_ref, ma_ref, o_ref, reverse=False)
    _store_prompt_state(t, st_ref, snew_ref)


def _gla_bwd_kernel(q_ref, v_ref, f_ref, lbl_ref, s0_ref, of_ref, gg_ref, ng_ref,
                    a_ref, snew_ref, st_ref, ma_ref, ob_ref):
    t = N_TILES - 1 - pl.program_id(0)
    _gla_tile(t, q_ref, v_ref, f_ref, lbl_ref, s0_ref, st_ref, ma_ref, ob_ref, reverse=True)
    _store_prompt_state(t, st_ref, snew_ref)
    ng = ng_ref[...]
    for h in range(HEADS):
        hs = slice(h * DV, (h + 1) * DV)
        o = of_ref[:, hs] + ob_ref[:, hs]
        a_ref[:, hs] = (_rmsnorm(o, ng) * _silu(gg_ref[:, hs].astype(F32))).astype(BF16)


def _state_specs(direction, tile_of_step):
    s0 = pl.BlockSpec((1, 1, HEADS, DK, DV),
                      lambda i: (_sample_of_tile(tile_of_step(i)), direction, 0, 0, 0))
    snew = pl.BlockSpec((1, HEADS, DK, DV),
                        lambda i: (jnp.minimum(tile_of_step(i), PROMPT_TILES - 1), 0, 0, 0))
    return s0, snew


def _gla_fwd(q, v, ff, lb_logits, s0):
    s0_spec, snew_spec = _state_specs(0, lambda i: i)
    return pl.pallas_call(
        _gla_fwd_kernel,
        grid=(N_TILES,),
        in_specs=[_TILE_SPEC, _TILE_SPEC, _TILE_SPEC,
                  pl.BlockSpec(lb_logits.shape, lambda i: (0, 0)), s0_spec],
        out_specs=[_TILE_SPEC, snew_spec],
        out_shape=[jax.ShapeDtypeStruct((T, D), F32),
                   jax.ShapeDtypeStruct((N_PROMPT_SEQ, HEADS, DK, DV), F32)],
        scratch_shapes=[pltpu.VMEM((HEADS, DV, DK), F32), pltpu.VMEM((TM, TM), BF16)],
        compiler_params=_cparams(),
        name="gla_fwd",
    )(q, v, ff, lb_logits, s0)


def _gla_bwd(q, v, fb, lb_logits, s0, o_f, gg, head_norm_g):
    rev = lambda i: N_TILES - 1 - i
    tile = pl.BlockSpec((TM, D), lambda i: (rev(i), 0))
    s0_spec, snew_spec = _state_specs(1, rev)
    return pl.pallas_call(
        _gla_bwd_kernel,
        grid=(N_TILES,),
        in_specs=[tile, tile, tile, pl.BlockSpec(lb_logits.shape, lambda i: (0, 0)), s0_spec,
                  tile, tile, pl.BlockSpec((1, DV), lambda i: (0, 0))],
        out_specs=[tile, snew_spec],
        out_shape=[jax.ShapeDtypeStruct((T, D), BF16),
                   jax.ShapeDtypeStruct((N_PROMPT_SEQ, HEADS, DK, DV), F32)],
        scratch_shapes=[pltpu.VMEM((HEADS, DV, DK), F32), pltpu.VMEM((TM, TM), BF16),
                        pltpu.VMEM((TM, D), F32)],
        compiler_params=_cparams(),
        name="gla_bwd",
    )(q, v, fb, lb_logits, s0, o_f, gg, head_norm_g)


def _conv_mixer_kernel(x_ref, mod_ref, g_ref, w_ref, cw_ref, a_ref):
    seg = jnp.where(pl.program_id(0) < PROMPT_STEPS, PROMPT_LEN, GRID_W)
    pos = lax.broadcasted_iota(jnp.int32, (TM, 1), 0) & (seg - 1)

    def project(rows):
        h = _modulate(x_ref[rows], g_ref[...], mod_ref[0, 0], mod_ref[0, 1]).astype(BF16)
        return [_dot(h, w_ref[:, k * D:(k + 1) * D]) for k in range(3)]

    def convolve(rows, bg, cg, x_in):
        u = cg * x_in
        prev = jnp.where(pos == 0, 0.0, pltpu.roll(u, 1, axis=0))
        nxt = jnp.where(pos == seg - 1, 0.0, pltpu.roll(u, TM - 1, axis=0))
        conv = cw_ref[0:1] * prev + cw_ref[1:2] * u + cw_ref[2:3] * nxt
        a_ref[rows] = (bg * conv).astype(BF16)

    projected = [project(rows) for rows in _SUB_ROWS]
    for rows, parts in zip(_SUB_ROWS, projected):
        convolve(rows, *parts)


def _conv_mixer(x, mod, norm_g, w_in, conv_w):
    return pl.pallas_call(
        _conv_mixer_kernel,
        grid=(N_STEPS,),
        in_specs=[_STEP_SPEC, _STEP_MOD_SPEC, _ROW_SPEC, pl.BlockSpec((D, 3 * D), lambda i: (0, 0)),
                  pl.BlockSpec((3, D), lambda i: (0, 0))],
        out_specs=_STEP_SPEC,
        out_shape=jax.ShapeDtypeStruct((T, D), BF16),
        compiler_params=_cparams(),
        name="conv_mixer",
    )(x, mod, norm_g, w_in, conv_w)


ROUTE_ROWS = 32


def _first_member(vals, target):
    idx = jnp.full_like(target, float(EPG - 1))
    for j in range(EPG - 2, -1, -1):
        idx = jnp.where(vals[j] == target, float(j), idx)
    return idx


def _router_logits(h2, rwt_ref):
    h_hi, h_lo = _split_bf16(h2)
    w_hi, w_lo = _split_bf16(rwt_ref[...])
    return _dot_nt(w_hi, h_hi) + (_dot_nt(w_hi, h_lo) + _dot_nt(w_lo, h_hi))


def _select_experts(logits, rb_ref):
    scores = jax.nn.sigmoid(logits)
    sel = scores + rb_ref[...]
    member = [sel[j * N_GROUPS:(j + 1) * N_GROUPS] for j in range(EPG)]
    m1 = functools.reduce(jnp.maximum, member)
    i1 = _first_member(member, m1)
    rest = [jnp.where(i1 == float(j), -jnp.inf, member[j]) for j in range(EPG)]
    m2 = functools.reduce(jnp.maximum, rest)
    i2 = _first_member(rest, m2)
    group_score = m1 + m2

    best, grp, first, second = group_score[0:1], jnp.zeros((1, TM), F32), i1[0:1], i2[0:1]
    for g in range(1, N_GROUPS):
        upd = group_score[g:g + 1] > best
        best = jnp.where(upd, group_score[g:g + 1], best)
        grp = jnp.where(upd, float(g), grp)
        first = jnp.where(upd, i1[g:g + 1], first)
        second = jnp.where(upd, i2[g:g + 1], second)
    a = jnp.minimum(first, second)
    b = jnp.maximum(first, second)
    row = lax.broadcasted_iota(jnp.int32, (N_EXPERTS, TM), 0).astype(F32)
    s_lo = jnp.sum(jnp.where(row == a * N_GROUPS + grp, scores, 0.0), axis=0, keepdims=True)
    s_hi = jnp.sum(jnp.where(row == b * N_GROUPS + grp, scores, 0.0), axis=0, keepdims=True)
    tot = s_lo + s_hi
    pair = jnp.where(a == 0.0, jnp.where(b == 1.0, 0.0, jnp.where(b == 2.0, 3.0, 4.0)),
                     jnp.where(a == 1.0, jnp.where(b == 2.0, 5.0, 1.0), 2.0))
    return grp * N_PAIRS + pair, s_lo / tot, s_hi / tot


def _post_mixer_kernel(*refs, n_x):
    a_ref, x_refs = refs[0], refs[1:1 + n_x]
    (mod_ref, g2_ref, wo_ref, rwt_ref, rb_ref,
     x1_ref, gates_ref, bucket_ref, rank_ref, cnt_ref, carry_ref, earlier_ref) = refs[1 + n_x:]

    @pl.when(pl.program_id(0) == 0)
    def _():
        carry_ref[...] = jnp.zeros_like(carry_ref)
        s_idx = lax.broadcasted_iota(jnp.int32, (TM, TM), 0)
        t_idx = lax.broadcasted_iota(jnp.int32, (TM, TM), 1)
        earlier_ref[...] = (s_idx < t_idx).astype(BF16)

    def stage_mix(s):
        rows = slice(s * TM, (s + 1) * TM)
        x1 = (_token_tile(x_refs, rows, PROMPT_STEPS)
              + mod_ref[0, 2] * _dot(a_ref[rows], wo_ref[...]))
        x1_ref[rows] = x1
        return _modulate(x1, g2_ref[...], mod_ref[0, 3], mod_ref[0, 4])

    def stage_rank(s, bucket, g_lo, g_hi):
        onehot = lax.broadcasted_iota(jnp.int32, (ROUTE_ROWS, TM), 0).astype(F32) == bucket
        before = _dot(onehot.astype(BF16), earlier_ref[...])
        oh = onehot.astype(F32)
        carry = carry_ref[...]
        rank = jnp.sum(oh * (before + carry[:, 0:1]), axis=0, keepdims=True)
        carry_ref[...] = carry + jnp.sum(oh, axis=1, keepdims=True)
        bucket_ref[s] = bucket.astype(jnp.int32)
        rank_ref[s] = rank.astype(jnp.int32)
        grow = lax.broadcasted_iota(jnp.int32, (LANES, TM), 0)
        gates_ref[s * TM:(s + 1) * TM] = jnp.where(grow == 0, g_lo, jnp.where(grow == 1, g_hi, 0.0)).T

    tiles = range(STEP_TILES)
    h2 = [stage_mix(s) for s in tiles]
    logits = [_router_logits(h2[s], rwt_ref) for s in tiles]
    picks = [_select_experts(logits[s], rb_ref) for s in tiles]
    for s in tiles:
        stage_rank(s, *picks[s])
    cnt_ref[...] = carry_ref[...]


def _post_mixer(a, x_parts, mod, norm_g2, w_out, router_wt, router_bt):
    rows = STEP_ROWS
    idx_spec = pl.BlockSpec((STEP_TILES, 1, TM), lambda i: (i, 0, 0))
    return pl.pallas_call(
        functools.partial(_post_mixer_kernel, n_x=len(x_parts)),
        grid=(N_STEPS,),
        in_specs=[pl.BlockSpec((rows, D), lambda i: (i, 0))] + _token_specs(x_parts, STEP_TILES) + [
            pl.BlockSpec((1, 6, 1, D), lambda i: (_cond_of_tile(i * STEP_TILES), 0, 0, 0)), _ROW_SPEC,
            pl.BlockSpec((D, D), lambda i: (0, 0)),
            pl.BlockSpec((N_EXPERTS, D), lambda i: (0, 0)),
            pl.BlockSpec((N_EXPERTS, 1), lambda i: (0, 0))],
        out_specs=[pl.BlockSpec((rows, D), lambda i: (i, 0)), pl.BlockSpec((rows, LANES), lambda i: (i, 0)),
                   idx_spec, idx_spec, pl.BlockSpec((ROUTE_ROWS, LANES), lambda i: (0, 0))],
        out_shape=[jax.ShapeDtypeStruct((T, D), F32), jax.ShapeDtypeStruct((T, LANES), F32),
                   jax.ShapeDtypeStruct((N_TILES, 1, TM), jnp.int32),
                   jax.ShapeDtypeStruct((N_TILES, 1, TM), jnp.int32),
                   jax.ShapeDtypeStruct((ROUTE_ROWS, LANES), F32)],
        scratch_shapes=[pltpu.VMEM((ROUTE_ROWS, LANES), F32), pltpu.VMEM((TM, TM), BF16)],
        compiler_params=_cparams(),
        name="post_mixer",
    )(a, *x_parts, mod, norm_g2, w_out, router_wt, router_bt)


def _smem_step_spec(step_of_step):
    return pl.BlockSpec((STEP_TILES, 1, TM), lambda i, *_: (step_of_step(i), 0, 0), memory_space=pltpu.SMEM)


def _start_step_rows(make_copy):
    for tile in range(STEP_TILES):
        for r in range(TM):
            make_copy(tile, r).start(priority=r % 2)


def _dispatch_kernel(partial_ref, slot_ref, x1_ref, mod_ref, g2_ref, gates_ref,
                     xs_ref, buf_ref, zero_ref, sems, zero_sem):
    i = pl.program_id(0)
    cur = i % 2

    @pl.when(i == 0)
    def _():
        zero_ref[...] = jnp.zeros_like(zero_ref)

        def zero_copy(j):
            return pltpu.make_async_copy(zero_ref, xs_ref.at[pl.ds(j * TM_E, TM_E)], zero_sem)

        def start(j, carry):
            @pl.when(partial_ref[j] != 0)
            def _():
                zero_copy(j).start()
            return carry

        def wait(j, carry):
            @pl.when(partial_ref[j] != 0)
            def _():
                zero_copy(j).wait()
            return carry

        lax.fori_loop(0, N_TILES_E, start, 0)
        lax.fori_loop(0, N_TILES_E, wait, 0)

    def wait_step(b):
        pltpu.make_async_copy(buf_ref.at[b], xs_ref.at[pl.ds(0, STEP_ROWS)], sems.at[b]).wait()

    h2 = _modulate(x1_ref[...], g2_ref[...], mod_ref[0, 3], mod_ref[0, 4])
    gates = gates_ref[...]

    def send(b):
        buf_ref[b, :, :D] = h2
        buf_ref[b, :, D:] = gates
        _start_step_rows(lambda tile, r: pltpu.make_async_copy(
            buf_ref.at[b, pl.ds(tile * TM + r, 1)], xs_ref.at[pl.ds(slot_ref[tile, 0, r], 1)], sems.at[b]))

        @pl.when(i > 0)
        def _():
            wait_step(1 - b)

        @pl.when(i == N_STEPS - 1)
        def _():
            wait_step(b)

    for b in range(2):
        pl.when(cur == b)(functools.partial(send, b))


def _dispatch(partial, slot, x1, mod, norm_g2, gates):
    return pl.pallas_call(
        _dispatch_kernel,
        grid_spec=pltpu.PrefetchScalarGridSpec(
            num_scalar_prefetch=1,
            grid=(N_STEPS,),
            in_specs=[_smem_step_spec(lambda i: i), _STEP_SPEC, _STEP_MOD_SPEC,
                      pl.BlockSpec((1, D), lambda i, *_: (0, 0)),
                      pl.BlockSpec((STEP_ROWS, LANES), lambda i, *_: (i, 0))],
            out_specs=pl.BlockSpec(memory_space=pl.ANY),
            scratch_shapes=[pltpu.VMEM((2, STEP_ROWS, XW), F32), pltpu.VMEM((TM_E, XW), F32),
                            pltpu.SemaphoreType.DMA((2,)), pltpu.SemaphoreType.DMA(())],
        ),
        out_shape=jax.ShapeDtypeStruct((S_ROWS, XW), F32),
        compiler_params=_cparams(),
        name="moe_dispatch",
    )(partial, slot, x1, mod, norm_g2, gates)


_W_SHAPES = ((D, D_FF), (D, D_FF), (D_FF, D))


def _expert_kernel(ea_ref, eb_ref, next_a_ref, next_b_ref, a_higher_ref, nrows_ref,
                   x_ref, wg_hbm, wu_hbm, wd_hbm, y_ref, *scratch, layer):
    stage, w16, sems = scratch[0:3], scratch[3:6], scratch[6]
    j = pl.program_id(0)
    n = nrows_ref[j]
    prev = jnp.maximum(j - 1, 0)

    def fetch(slot, expert):
        return [pltpu.make_async_copy(w.at[layer, expert], s.at[slot], sems.at[slot])
                for w, s in zip((wg_hbm, wu_hbm, wd_hbm), stage)]

    for slot, e_ref, next_ref in ((0, ea_ref, next_a_ref), (1, eb_ref, next_b_ref)):
        @pl.when(j == 0)
        def _(slot=slot, e_ref=e_ref):
            for copy in fetch(slot, e_ref[0]):
                copy.start()

        @pl.when(jnp.logical_or(j == 0, e_ref[j] != e_ref[prev]))
        def _(slot=slot, e_ref=e_ref, next_ref=next_ref):
            for copy in fetch(slot, e_ref[j]):
                copy.wait()
            for s, d in zip(stage, w16):
                d[slot] = s[slot].astype(BF16)

            @pl.when(next_ref[j] >= 0)
            def _():
                for copy in fetch(slot, next_ref[j]):
                    copy.start()

    def run(rows):
        valid = lax.broadcasted_iota(jnp.int32, (rows, 1), 0) < n
        xe = jnp.where(valid, x_ref[:rows], 0.0)
        x = xe[:, :D].astype(BF16)
        g_lo, g_hi = xe[:, D:D + 1], xe[:, D + 1:D + 2]
        a_higher = a_higher_ref[j] != 0

        def ffn(slot, gate):
            wg_ref, wu_ref, wd_ref = w16
            act = _silu(_dot(x, wg_ref[slot])) * _dot(x, wu_ref[slot]) * gate
            return _dot(act.astype(BF16), wd_ref[slot])

        y_ref[:rows] = (ffn(0, jnp.where(a_higher, g_hi, g_lo))
                        + ffn(1, jnp.where(a_higher, g_lo, g_hi)))
        if rows < TM_E:
            y_ref[rows:] = jnp.zeros((TM_E - rows, D), F32)

    half = TM_E // 2
    pl.when(n > half)(functools.partial(run, TM_E))
    pl.when(jnp.logical_and(n > 0, n <= half))(functools.partial(run, half))

    @pl.when(n == 0)
    def _():
        y_ref[...] = jnp.zeros_like(y_ref)


def _experts(tile_ea, tile_eb, tile_next_a, tile_next_b, tile_a_higher, tile_rows, xs,
             w_gate, w_up, w_down, layer):
    any_spec = pl.BlockSpec(memory_space=pl.ANY)
    return pl.pallas_call(
        functools.partial(_expert_kernel, layer=layer),
        grid_spec=pltpu.PrefetchScalarGridSpec(
            num_scalar_prefetch=6,
            grid=(N_TILES_E,),
            in_specs=[pl.BlockSpec((TM_E, XW), lambda j, *_: (j, 0)), any_spec, any_spec, any_spec],
            out_specs=pl.BlockSpec((TM_E, D), lambda j, *_: (j, 0)),
            scratch_shapes=[pltpu.VMEM((2,) + s, F32) for s in _W_SHAPES]
                           + [pltpu.VMEM((2,) + s, BF16) for s in _W_SHAPES]
                           + [pltpu.SemaphoreType.DMA((2,))],
        ),
        out_shape=jax.ShapeDtypeStruct((S_ROWS, D), F32),
        compiler_params=_cparams(),
        name="moe_experts",
    )(tile_ea, tile_eb, tile_next_a, tile_next_b, tile_a_higher, tile_rows, xs, w_gate, w_up, w_down)


def _combine_kernel(slot_ref, next_slot_ref, x1_ref, mod_ref, gf_ref, y_ref, *rest, final):
    out_refs, (buf_ref, sems) = rest[:-2], rest[-2:]
    i = pl.program_id(0)
    cur = i % 2

    def gather(slots, b):
        _start_step_rows(lambda tile, r: pltpu.make_async_copy(
            y_ref.at[pl.ds(slots[tile, 0, r], 1)], buf_ref.at[b, pl.ds(tile * TM + r, 1)], sems.at[b]))

    @pl.when(i == 0)
    def _():
        gather(slot_ref, 0)

    def combine(b):
        @pl.when(i + 1 < N_STEPS)
        def _():
            gather(next_slot_ref, 1 - b)

        pltpu.make_async_copy(y_ref.at[pl.ds(0, STEP_ROWS)], buf_ref.at[b], sems.at[b]).wait()
        x2 = x1_ref[...] + mod_ref[0, 5] * buf_ref[b]
        if not final:
            out_refs[0][...] = x2
            return
        y = _rmsnorm(x2, gf_ref[...])
        yp_ref, ys_ref = out_refs

        @pl.when(i < PROMPT_STEPS)
        def _():
            yp_ref[...] = y

        @pl.when(i >= PROMPT_STEPS)
        def _():
            ys_ref[...] = y

    for b in range(2):
        pl.when(cur == b)(functools.partial(combine, b))


def _combine(slot, x1, mod, final_g, y_sorted, *, final):
    if final:
        out_specs = _token_specs((None, None), STEP_TILES)
        out_shape = [jax.ShapeDtypeStruct((T_PROMPT, D), F32), jax.ShapeDtypeStruct((T_SAMPLE, D), F32)]
    else:
        out_specs = [_STEP_SPEC]
        out_shape = [jax.ShapeDtypeStruct((T, D), F32)]
    return pl.pallas_call(
        functools.partial(_combine_kernel, final=final),
        grid=(N_STEPS,),
        in_specs=[_smem_step_spec(lambda i: i), _smem_step_spec(lambda i: jnp.minimum(i + 1, N_STEPS - 1)),
                  _STEP_SPEC, _STEP_MOD_SPEC, _ROW_SPEC, pl.BlockSpec(memory_space=pl.ANY)],
        out_specs=out_specs,
        scratch_shapes=[pltpu.VMEM((2, STEP_ROWS, D), F32), pltpu.SemaphoreType.DMA((2,))],
        out_shape=out_shape,
        compiler_params=_cparams(),
        name="moe_combine_final" if final else "moe_combine",
    )(slot, slot, x1, mod, final_g, y_sorted)


def _slot_kernel(off_ref, bucket_ref, rank_ref, slot_ref):
    bucket = bucket_ref[...]
    slot = rank_ref[...]
    for b in range(N_BUCKETS):
        slot = slot + jnp.where(bucket == b, off_ref[b], 0)
    slot_ref[...] = slot


def _slots(off, bucket, rank):
    whole = pl.BlockSpec(bucket.shape, lambda i, *_: (0, 0, 0))
    return pl.pallas_call(
        _slot_kernel,
        grid_spec=pltpu.PrefetchScalarGridSpec(
            num_scalar_prefetch=1, grid=(1,), in_specs=[whole, whole], out_specs=whole),
        out_shape=jax.ShapeDtypeStruct(bucket.shape, jnp.int32),
        compiler_params=_cparams(),
        name="moe_slots",
    )(off, bucket, rank)


def _next_different(e):
    j = jnp.arange(e.shape[0], dtype=jnp.int32)
    later_diff = jnp.logical_and(e[None, :] != e[:, None], j[None, :] > j[:, None])
    first = jnp.min(jnp.where(later_diff, j[None, :], e.shape[0]), axis=1)
    found = first < e.shape[0]
    return jnp.where(found, e[jnp.minimum(first, e.shape[0] - 1)], -1).astype(jnp.int32)


def _bucket_plan(bucket, rank, counts):
    cnt = counts[:N_BUCKETS, 0].astype(jnp.int32)
    tiles = (cnt + TM_E - 1) // TM_E
    tile_end = jnp.cumsum(tiles)
    tile_start = tile_end - tiles
    slot = _slots((tile_start * TM_E).astype(jnp.int32), bucket, rank)
    j = jnp.arange(N_TILES_E, dtype=jnp.int32)
    n_live = tile_end[-1]
    j_live = jnp.minimum(j, n_live - 1)
    b = jnp.sum((tile_end[None, :] <= j_live[:, None]).astype(jnp.int32), axis=1)
    b = jnp.minimum(b, N_BUCKETS - 1)
    rows = jnp.clip(cnt[b] - (j - tile_start[b]) * TM_E, 0, TM_E)
    rows = jnp.where(j < n_live, rows, 0).astype(jnp.int32)
    grp = b // N_PAIRS
    pair = b % N_PAIRS
    ea = grp * EPG + jnp.asarray(_SLOT_A, jnp.int32)[pair]
    eb = grp * EPG + jnp.asarray(_SLOT_B, jnp.int32)[pair]
    a_higher = jnp.asarray(_SLOT_A_IS_HIGHER, jnp.int32)[pair]
    return slot, (ea, eb, _next_different(ea), _next_different(eb), a_higher, rows)


def _moe(a, x_parts, mod, norm_g2, w_out, router_wt, router_bt, w_gate, w_up, w_down, layer, final_g,
         *, final):
    x1, gates, bucket, rank, counts = _post_mixer(a, x_parts, mod, norm_g2, w_out, router_wt, router_bt)
    slot, tile_plan = _bucket_plan(bucket, rank, counts)
    partial = (tile_plan[-1] < TM_E).astype(jnp.int32)
    xs = _dispatch(partial, slot, x1, mod, norm_g2, gates)
    ys = _experts(*tile_plan, xs, w_gate, w_up, w_down, layer)
    return _combine(slot, x1, mod, final_g, ys, final=final)


def kernel(x_prompt, x_sample, state_hgrn, c, c_ctx, w_mod, b_mod, norm_mix_g, norm_ffn_g, norm_final_g,
           hgrn_w_in, hgrn_lb_logits, hgrn_norm_g, hgrn_w_out, conv_w_in, conv_w, conv_w_out,
           router_w, router_b, moe_w_gate, moe_w_up, moe_w_down):
    x_parts = (x_prompt.reshape(T_PROMPT, D), x_sample.reshape(T_SAMPLE, D))
    cond = jnp.concatenate([c_ctx[None], c, jnp.zeros((COND_ROWS - N_COND, D), F32)], axis=0)
    mods = _adaln(cond, w_mod, b_mod)[:, :N_COND].reshape(2, N_COND, 6, 1, D)
    member_major = lambda w: w.reshape(N_GROUPS, EPG, -1).transpose(1, 0, 2).reshape(N_EXPERTS, -1)
    rwt = member_major(router_w.T)
    rbt = member_major(router_b.reshape(N_EXPERTS, 1))
    final_g = norm_final_g.reshape(1, D)
    bf = lambda w: w.astype(BF16)

    q, v, ff, fb, gg = _hgrn_proj(x_parts, mods[0], norm_mix_g[0:1], bf(hgrn_w_in[0]))
    s0 = state_hgrn[:, 0]
    o_f, s_f = _gla_fwd(q, v, ff, hgrn_lb_logits, s0)
    a, s_b = _gla_bwd(q, v, fb, hgrn_lb_logits, s0, o_f, gg, hgrn_norm_g[0:1])
    (x,) = _moe(a, x_parts, mods[0], norm_ffn_g[0:1], bf(hgrn_w_out[0]), rwt, rbt,
                moe_w_gate, moe_w_up, moe_w_down, 0, final_g, final=False)

    a = _conv_mixer(x, mods[1], norm_mix_g[1:2], bf(conv_w_in[0]), conv_w[0])
    y_p, y_s = _moe(a, (x,), mods[1], norm_ffn_g[1:2], bf(conv_w_out[0]), rwt, rbt,
                    moe_w_gate, moe_w_up, moe_w_down, 1, final_g, final=True)

    new_state = jnp.stack([s_f, s_b], axis=1)[:, None]
    return (y_p.reshape(N_PROMPT_SEQ, PROMPT_LEN, D), y_s.reshape(N_SAMPLE_SEQ, SAMPLE_LEN, D), new_state)
```

```python
import functools

import jax
import jax.numpy as jnp
from jax import lax
from jax.experimental import pallas as pl
from jax.experimental.pallas import tpu as pltpu

F32 = jnp.float32
BF16 = jnp.bfloat16

D = 1024
N_PROMPT_SEQ = 16
PROMPT_LEN = 256
N_SAMPLE_SEQ = 2
SAMPLE_LEN = 4096
GRID_W = 64
T_PROMPT = N_PROMPT_SEQ * PROMPT_LEN
T_SAMPLE = N_SAMPLE_SEQ * SAMPLE_LEN
T = T_PROMPT + T_SAMPLE
N_COND = 1 + N_SAMPLE_SEQ
COND_ROWS = 8

HEADS = 8
DK = 128
DV = 128
CHUNK = 64
N_EXPERTS = 16
N_GROUPS = 4
EPG = 4
N_PAIRS = 6
N_BUCKETS = N_GROUPS * N_PAIRS
D_FF = 512
EPS = 1e-6

TM = 256
N_TILES = T // TM
PROMPT_TILES = T_PROMPT // TM
TILES_PER_SAMPLE = SAMPLE_LEN // TM
CHUNKS_PER_TILE = TM // CHUNK
STEP_TILES = 2
STEP_ROWS = STEP_TILES * TM
N_STEPS = N_TILES // STEP_TILES
PROMPT_STEPS = PROMPT_TILES // STEP_TILES
TM_E = 256
N_TILES_E = T // TM_E + N_BUCKETS
S_ROWS = N_TILES_E * TM_E
XW = D + 128
LANES = 128
MOD_TN = 1536

VMEM_LIMIT = 56 * 1024 * 1024

_SLOT_A = (0, 3, 3, 0, 0, 1)
_SLOT_B = (1, 1, 2, 2, 3, 2)
_SLOT_A_IS_HIGHER = tuple(int(a > b) for a, b in zip(_SLOT_A, _SLOT_B))


def _cparams():
    return pltpu.CompilerParams(dimension_semantics=("arbitrary",), vmem_limit_bytes=VMEM_LIMIT)


def _cond_of_tile(t):
    return jnp.where(t < PROMPT_TILES, 0, 1 + (t - PROMPT_TILES) // TILES_PER_SAMPLE)


def _rmsnorm(x, g):
    ms = jnp.mean(x * x, axis=-1, keepdims=True)
    return (x * lax.rsqrt(ms + EPS)) * g


def _modulate(x, g, shift, scale):
    return _rmsnorm(x, g) * (1.0 + scale) + shift


def _silu(x):
    return x * jax.nn.sigmoid(x)


def _dot(a, b):
    return jnp.dot(a, b, preferred_element_type=F32)


def _dot_nt(a, b):
    return lax.dot_general(a, b, (((1,), (1,)), ((), ())), preferred_element_type=F32)


def _dot_tn(a, b):
    return lax.dot_general(a, b, (((0,), (0,)), ((), ())), preferred_element_type=F32)


def _split_bf16(x):
    hi = x.astype(BF16)
    lo = (x - hi.astype(F32)).astype(BF16)
    return hi, lo


def _mod_kernel(cond_ref, w_ref, b_ref, o_ref):
    s = _silu(cond_ref[...])
    o_ref[0] = _dot(s.astype(BF16), w_ref[0].astype(BF16)) + b_ref[0]


def _adaln(cond, w_mod, b_mod):
    depth = w_mod.shape[0]
    return pl.pallas_call(
        _mod_kernel,
        grid=(depth, 6 * D // MOD_TN),
        in_specs=[
            pl.BlockSpec((COND_ROWS, D), lambda i, j: (0, 0)),
            pl.BlockSpec((1, D, MOD_TN), lambda i, j: (i, 0, j)),
            pl.BlockSpec((1, 1, MOD_TN), lambda i, j: (i, 0, j)),
        ],
        out_specs=pl.BlockSpec((1, COND_ROWS, MOD_TN), lambda i, j: (i, 0, j)),
        out_shape=jax.ShapeDtypeStruct((depth, COND_ROWS, 6 * D), F32),
        compiler_params=pltpu.CompilerParams(
            dimension_semantics=("arbitrary", "arbitrary"), vmem_limit_bytes=VMEM_LIMIT),
        name="adaln",
    )(cond, w_mod, b_mod.reshape(depth, 1, 6 * D))


_MOD_SPEC = pl.BlockSpec((1, 6, 1, D), lambda i: (_cond_of_tile(i), 0, 0, 0))
_ROW_SPEC = pl.BlockSpec((1, D), lambda i: (0, 0))
_TILE_SPEC = pl.BlockSpec((TM, D), lambda i: (i, 0))
_STEP_MOD_SPEC = pl.BlockSpec((1, 6, 1, D), lambda i, *_: (_cond_of_tile(i * STEP_TILES), 0, 0, 0))
_STEP_SPEC = pl.BlockSpec((STEP_ROWS, D), lambda i, *_: (i, 0))
_SUB_ROWS = tuple(slice(s * TM, (s + 1) * TM) for s in range(STEP_TILES))


def _token_tile(x_refs, rows=slice(None), prompt_steps=PROMPT_TILES):
    if len(x_refs) == 1:
        return x_refs[0][rows]
    return jnp.where(pl.program_id(0) < prompt_steps, x_refs[0][rows], x_refs[1][rows])


def _token_specs(x_parts, tiles_per_step=1):
    rows = tiles_per_step * TM
    if len(x_parts) == 1:
        return [pl.BlockSpec((rows, D), lambda i: (i, 0))]
    prompt_steps = T_PROMPT // rows
    return [pl.BlockSpec((rows, D), lambda i: (jnp.minimum(i, prompt_steps - 1), 0)),
            pl.BlockSpec((rows, D), lambda i: (jnp.maximum(i - prompt_steps, 0), 0))]


def _hgrn_proj_kernel(xp_ref, xs_ref, mod_ref, g_ref, w_ref, q_ref, v_ref, ff_ref, fb_ref, gg_ref):
    h = [_modulate(_token_tile((xp_ref, xs_ref), rows, PROMPT_STEPS),
                   g_ref[...], mod_ref[0, 0], mod_ref[0, 1]).astype(BF16) for rows in _SUB_ROWS]
    for rows, h_tile in zip(_SUB_ROWS, h):
        q_ref[rows] = (_dot(h_tile, w_ref[:, 0 * D:1 * D]) * (DK ** -0.5)).astype(BF16)
        v_ref[rows] = _dot(h_tile, w_ref[:, 1 * D:2 * D]).astype(BF16)
        ff_ref[rows] = _dot(h_tile, w_ref[:, 2 * D:3 * D])
        fb_ref[rows] = _dot(h_tile, w_ref[:, 3 * D:4 * D])
        gg_ref[rows] = _dot(h_tile, w_ref[:, 4 * D:5 * D]).astype(BF16)


def _hgrn_proj(x_parts, mod, norm_g, w_in):
    bf = jax.ShapeDtypeStruct((T, D), BF16)
    f32 = jax.ShapeDtypeStruct((T, D), F32)
    return pl.pallas_call(
        _hgrn_proj_kernel,
        grid=(N_STEPS,),
        in_specs=_token_specs(x_parts, STEP_TILES) + [_STEP_MOD_SPEC, _ROW_SPEC,
                                                      pl.BlockSpec((D, 5 * D), lambda i: (0, 0))],
        out_specs=[_STEP_SPEC] * 5,
        out_shape=[bf, bf, f32, f32, bf],
        compiler_params=_cparams(),
        name="hgrn_proj",
    )(*x_parts, mod, norm_g, w_in)


STEPS_PER_SAMPLE = SAMPLE_LEN // STEP_ROWS


def _gla_step(block, q_ref, v_ref, f_ref, lbl_ref, s0_ref, st_ref, st_first_ref, ma_ref, o_ref,
              *, reverse, finish=None):
    is_prompt = block < PROMPT_STEPS
    rel = block - PROMPT_STEPS
    edge = (STEPS_PER_SAMPLE - 1) if reverse else 0
    sample_start = jnp.logical_and(rel >= 0, rel % STEPS_PER_SAMPLE == edge)

    @pl.when(is_prompt)
    def _():
        st_ref[...] = jnp.zeros_like(st_ref)

    @pl.when(sample_start)
    def _():
        for h in range(HEADS):
            st_ref[h] = s0_ref[0, 0, h].T

    logits = lbl_ref[...]
    e = jnp.exp(logits - jnp.max(logits, axis=0, keepdims=True))
    lb = e[0:1] / jnp.sum(e, axis=0, keepdims=True)

    ref_idx = (CHUNK - 1 - CHUNK // 2) if reverse else CHUNK // 2
    last_idx = 0 if reverse else CHUNK - 1
    shift = CHUNK.bit_length() - 1

    row = lax.broadcasted_iota(jnp.int32, (TM, TM), 0)
    col = lax.broadcasted_iota(jnp.int32, (TM, TM), 1)
    same = (row >> shift) == (col >> shift)
    ref_row = ((row >> shift) << shift) + ref_idx
    keep = jnp.logical_and(same, (row <= col) if reverse else (row >= col))

    @pl.when(pl.program_id(0) == 0)
    def _():
        at_ref = jnp.logical_and(same, (ref_row <= col) if reverse else (ref_row >= col))
        ma_ref[...] = (keep.astype(F32) - at_ref.astype(F32)).astype(BF16)

    ma = ma_ref[...]

    rr = lax.broadcasted_iota(jnp.int32, (2 * CHUNKS_PER_TILE, TM), 0)
    cc = lax.broadcasted_iota(jnp.int32, (2 * CHUNKS_PER_TILE, TM), 1)
    chunk = rr & (CHUNKS_PER_TILE - 1)
    target = chunk * CHUNK + jnp.where(rr < CHUNKS_PER_TILE, ref_idx, last_idx)
    covered = (cc >= target) if reverse else (cc <= target)
    rm = jnp.logical_and((cc >> shift) == chunk, covered).astype(BF16)

    chunk_rows = [slice(c * CHUNK, (c + 1) * CHUNK) for c in range(CHUNKS_PER_TILE)]
    order = range(CHUNKS_PER_TILE - 1, -1, -1) if reverse else range(CHUNKS_PER_TILE)
    pair_w = 2 * DK
    n_pairs = HEADS // 2
    tile_order = _SUB_ROWS[::-1] if reverse else _SUB_ROWS

    def per_chunk_scale(x, scale):
        return jnp.concatenate([x[rs] * scale[c:c + 1] for c, rs in enumerate(chunk_rows)], axis=0)

    def pair_cols(p):
        return slice(p * pair_w, (p + 1) * pair_w)

    def stage_decay(k, p):
        lbp = lb[:, pair_cols(p)]
        f = lbp + (1.0 - lbp) * jax.nn.sigmoid(f_ref[tile_order[k], pair_cols(p)])
        lf_hi, lf_lo = _split_bf16(jnp.log(f))
        z = _dot(ma, lf_hi) + _dot(ma, lf_lo)
        marks = _dot(rm, lf_hi) + _dot(rm, lf_lo)
        return 1.0 - f, z, marks

    def stage_scores(k, p, kk, z, marks):
        ref, last = marks[:CHUNKS_PER_TILE], marks[CHUNKS_PER_TILE:]
        qa32 = q_ref[tile_order[k], pair_cols(p)].astype(F32) * jnp.exp(z)
        kb32 = kk * jnp.exp(-z)
        qa = qa32.astype(BF16)
        kb = kb32.astype(BF16)
        qe = per_chunk_scale(qa32, jnp.exp(ref)).astype(BF16)
        kd = per_chunk_scale(kb32, jnp.exp(last - ref)).astype(BF16)
        v = v_ref[tile_order[k], pair_cols(p)]
        scores, incr = [], []
        for hh in range(2):
            ls = slice(hh * DK, (hh + 1) * DK)
            scores.append(_dot_nt(qa[:, ls], kb[:, ls]))
            kd_h = kd[:, ls]
            kd_blocks = jnp.concatenate(
                [jnp.concatenate([part for part in (jnp.zeros((c * CHUNK, DK), BF16), kd_h[rs],
                                                    jnp.zeros((TM - (c + 1) * CHUNK, DK), BF16))
                                  if part.shape[0]], axis=0)
                 for c, rs in enumerate(chunk_rows)], axis=1)
            incr.append(_dot_tn(v[:, ls], kd_blocks))
        return scores, incr, qe, jnp.exp(last), v

    def stage_output(k, p, scores, incr, qe, decay, v):
        base = tile_order[k].start
        for hh in range(2):
            h = 2 * p + hh
            ls = slice(hh * DK, (hh + 1) * DK)
            hs = slice(h * DK, (h + 1) * DK)
            o_intra = _dot(jnp.where(keep, scores[hh], 0.0).astype(BF16), v[:, ls])
            st = st_ref[h]
            if k > 0:
                st = jnp.where(is_prompt, 0.0, st)
            for c in order:
                rs = chunk_rows[c]
                o_ref[base + rs.start:base + rs.stop, hs] = (
                    o_intra[rs] + _dot_nt(qe[rs, ls], st.astype(BF16)))
                st = st * decay[c:c + 1, ls] + incr[hh][:, c * DK:(c + 1) * DK]
            st_ref[h] = st
            if k == 0:
                st_first_ref[h] = st
        if finish is not None:
            return (tile_order[k], p)

    def stage_finish(k, p, rows, pair):
        finish(rows, pair)

    units = [(k, p) for k in range(STEP_TILES) for p in range(n_pairs)]
    stages = [stage_decay, stage_scores, stage_output] + ([stage_finish] if finish is not None else [])
    carried = [dict() for _ in stages]
    for step in range(len(units) + len(stages) - 1):
        for depth, stage in enumerate(stages):
            u = step - depth
            if 0 <= u < len(units):
                out = stage(*units[u], *(carried[depth].pop(u) if depth else ()))
                if depth + 1 < len(stages):
                    carried[depth + 1][u] = out


def _store_prompt_states(block, st_ref, st_first_ref, snew_ref, *, reverse):
    first, second = (1, 0) if reverse else (0, 1)

    @pl.when(block < PROMPT_STEPS)
    def _():
        for h in range(HEADS):
            snew_ref[first, h] = st_first_ref[h].T
            snew_ref[second, h] = st_ref[h].T


def _gla_fwd_kernel(q_ref, v_ref, f_ref, lbl_ref, s0_ref, o_ref, snew_ref, st_ref, st_first_ref, ma_ref):
    block = pl.program_id(0)
    _gla_step(block, q_ref, v_ref, f_ref, lbl_ref, s0_ref, st_ref, st_first_ref, ma_ref, o_ref,
              reverse=False)
    _store_prompt_states(block, st_ref, st_first_ref, snew_ref, reverse=False)


def _gla_bwd_kernel(q_ref, v_ref, f_ref, lbl_ref, s0_ref, of_ref, gg_ref, ng_ref,
                    a_ref, snew_ref, st_ref, st_first_ref, ma_ref, ob_ref):
    block = N_STEPS - 1 - pl.program_id(0)

    def finish(rows, p):
        for h in (2 * p, 2 * p + 1):
            hs = slice(h * DV, (h + 1) * DV)
            o = of_ref[rows, hs] + ob_ref[rows, hs]
            a_ref[rows, hs] = (_rmsnorm(o, ng_ref[...]) * _silu(gg_ref[rows, hs].astype(F32))).astype(BF16)

    _gla_step(block, q_ref, v_ref, f_ref, lbl_ref, s0_ref, st_ref, st_first_ref, ma_ref, ob_ref,
              reverse=True, finish=finish)
    _store_prompt_states(block, st_ref, st_first_ref, snew_ref, reverse=True)


def _state_specs(direction, block_of_step):
    sample = lambda i: jnp.clip((block_of_step(i) - PROMPT_STEPS) // STEPS_PER_SAMPLE, 0, N_SAMPLE_SEQ - 1)
    s0 = pl.BlockSpec((1, 1, HEADS, DK, DV), lambda i: (sample(i), direction, 0, 0, 0))
    snew = pl.BlockSpec((STEP_TILES, HEADS, DK, DV),
                        lambda i: (jnp.minimum(block_of_step(i), PROMPT_STEPS - 1), 0, 0, 0))
    return s0, snew


_GLA_STATE_SCRATCH = [pltpu.VMEM((HEADS, DV, DK), F32), pltpu.VMEM((HEADS, DV, DK), F32),
                      pltpu.VMEM((TM, TM), BF16)]


def _gla_fwd(q, v, ff, lb_logits, s0):
    s0_spec, snew_spec = _state_specs(0, lambda i: i)
    return pl.pallas_call(
        _gla_fwd_kernel,
        grid=(N_STEPS,),
        in_specs=[_STEP_SPEC, _STEP_SPEC, _STEP_SPEC,
                  pl.BlockSpec(lb_logits.shape, lambda i: (0, 0)), s0_spec],
        out_specs=[_STEP_SPEC, snew_spec],
        out_shape=[jax.ShapeDtypeStruct((T, D), F32),
                   jax.ShapeDtypeStruct((N_PROMPT_SEQ, HEADS, DK, DV), F32)],
        scratch_shapes=_GLA_STATE_SCRATCH,
        compiler_params=_cparams(),
        name="gla_fwd",
    )(q, v, ff, lb_logits, s0)


def _gla_bwd(q, v, fb, lb_logits, s0, o_f, gg, head_norm_g):
    rev = lambda i: N_STEPS - 1 - i
    block = pl.BlockSpec((STEP_ROWS, D), lambda i: (rev(i), 0))
    s0_spec, snew_spec = _state_specs(1, rev)
    return pl.pallas_call(
        _gla_bwd_kernel,
        grid=(N_STEPS,),
        in_specs=[block, block, block, pl.BlockSpec(lb_logits.shape, lambda i: (0, 0)), s0_spec,
                  block, block, pl.BlockSpec((1, DV), lambda i: (0, 0))],
        out_specs=[block, snew_spec],
        out_shape=[jax.ShapeDtypeStruct((T, D), BF16),
                   jax.ShapeDtypeStruct((N_PROMPT_SEQ, HEADS, DK, DV), F32)],
        scratch_shapes=_GLA_STATE_SCRATCH + [pltpu.VMEM((STEP_ROWS, D), F32)],
        compiler_params=_cparams(),
        name="gla_bwd",
    )(q, v, fb, lb_logits, s0, o_f, gg, head_norm_g)


def _conv_mixer_kernel(x_ref, mod_ref, g_ref, w_ref, cw_ref, a_ref):
    seg = jnp.where(pl.program_id(0) < PROMPT_STEPS, PROMPT_LEN, GRID_W)
    pos = lax.broadcasted_iota(jnp.int32, (TM, 1), 0) & (seg - 1)

    def project(rows):
        h = _modulate(x_ref[rows], g_ref[...], mod_ref[0, 0], mod_ref[0, 1]).astype(BF16)
        return [_dot(h, w_ref[:, k * D:(k + 1) * D]) for k in range(3)]

    def convolve(rows, bg, cg, x_in):
        u = cg * x_in
        prev = jnp.where(pos == 0, 0.0, pltpu.roll(u, 1, axis=0))
        nxt = jnp.where(pos == seg - 1, 0.0, pltpu.roll(u, TM - 1, axis=0))
        conv = cw_ref[0:1] * prev + cw_ref[1:2] * u + cw_ref[2:3] * nxt
        a_ref[rows] = (bg * conv).astype(BF16)

    projected = [project(rows) for rows in _SUB_ROWS]
    for rows, parts in zip(_SUB_ROWS, projected):
        convolve(rows, *parts)


def _conv_mixer(x, mod, norm_g, w_in, conv_w):
    return pl.pallas_call(
        _conv_mixer_kernel,
        grid=(N_STEPS,),
        in_specs=[_STEP_SPEC, _STEP_MOD_SPEC, _ROW_SPEC, pl.BlockSpec((D, 3 * D), lambda i: (0, 0)),
                  pl.BlockSpec((3, D), lambda i: (0, 0))],
        out_specs=_STEP_SPEC,
        out_shape=jax.ShapeDtypeStruct((T, D), BF16),
        compiler_params=_cparams(),
        name="conv_mixer",
    )(x, mod, norm_g, w_in, conv_w)


ROUTE_ROWS = 32


def _first_member(vals, target):
    idx = jnp.full_like(target, float(EPG - 1))
    for j in range(EPG - 2, -1, -1):
        idx = jnp.where(vals[j] == target, float(j), idx)
    return idx


def _router_logits(h2, rwt_ref):
    h_hi, h_lo = _split_bf16(h2)
    w_hi, w_lo = _split_bf16(rwt_ref[...])
    return _dot_nt(w_hi, h_hi) + (_dot_nt(w_hi, h_lo) + _dot_nt(w_lo, h_hi))


def _select_experts(logits, rb_ref):
    scores = jax.nn.sigmoid(logits)
    sel = scores + rb_ref[...]
    member = [sel[j * N_GROUPS:(j + 1) * N_GROUPS] for j in range(EPG)]
    m1 = functools.reduce(jnp.maximum, member)
    i1 = _first_member(member, m1)
    rest = [jnp.where(i1 == float(j), -jnp.inf, member[j]) for j in range(EPG)]
    m2 = functools.reduce(jnp.maximum, rest)
    i2 = _first_member(rest, m2)
    group_score = m1 + m2

    best, grp, first, second = group_score[0:1], jnp.zeros((1, TM), F32), i1[0:1], i2[0:1]
    for g in range(1, N_GROUPS):
        upd = group_score[g:g + 1] > best
        best = jnp.where(upd, group_score[g:g + 1], best)
        grp = jnp.where(upd, float(g), grp)
        first = jnp.where(upd, i1[g:g + 1], first)
        second = jnp.where(upd, i2[g:g + 1], second)
    a = jnp.minimum(first, second)
    b = jnp.maximum(first, second)
    row = lax.broadcasted_iota(jnp.int32, (N_EXPERTS, TM), 0).astype(F32)
    s_lo = jnp.sum(jnp.where(row == a * N_GROUPS + grp, scores, 0.0), axis=0, keepdims=True)
    s_hi = jnp.sum(jnp.where(row == b * N_GROUPS + grp, scores, 0.0), axis=0, keepdims=True)
    tot = s_lo + s_hi
    pair = jnp.where(a == 0.0, jnp.where(b == 1.0, 0.0, jnp.where(b == 2.0, 3.0, 4.0)),
                     jnp.where(a == 1.0, jnp.where(b == 2.0, 5.0, 1.0), 2.0))
    return grp * N_PAIRS + pair, s_lo / tot, s_hi / tot


def _post_mixer_kernel(*refs, n_x):
    a_ref, x_refs = refs[0], refs[1:1 + n_x]
    (mod_ref, g2_ref, wo_ref, rwt_ref, rb_ref,
     x1_ref, gates_ref, bucket_ref, rank_ref, cnt_ref, carry_ref, earlier_ref) = refs[1 + n_x:]

    @pl.when(pl.program_id(0) == 0)
    def _():
        carry_ref[...] = jnp.zeros_like(carry_ref)
        s_idx = lax.broadcasted_iota(jnp.int32, (TM, TM), 0)
        t_idx = lax.broadcasted_iota(jnp.int32, (TM, TM), 1)
        earlier_ref[...] = (s_idx < t_idx).astype(BF16)

    def stage_mix(s):
        rows = slice(s * TM, (s + 1) * TM)
        x1 = (_token_tile(x_refs, rows, PROMPT_STEPS)
              + mod_ref[0, 2] * _dot(a_ref[rows], wo_ref[...]))
        x1_ref[rows] = x1
        return _modulate(x1, g2_ref[...], mod_ref[0, 3], mod_ref[0, 4])

    def stage_rank(s, bucket, g_lo, g_hi):
        onehot = lax.broadcasted_iota(jnp.int32, (ROUTE_ROWS, TM), 0).astype(F32) == bucket
        before = _dot(onehot.astype(BF16), earlier_ref[...])
        oh = onehot.astype(F32)
        carry = carry_ref[...]
        rank = jnp.sum(oh * (before + carry[:, 0:1]), axis=0, keepdims=True)
        carry_ref[...] = carry + jnp.sum(oh, axis=1, keepdims=True)
        bucket_ref[s] = bucket.astype(jnp.int32)
        rank_ref[s] = rank.astype(jnp.int32)
        grow = lax.broadcasted_iota(jnp.int32, (LANES, TM), 0)
        gates_ref[s * TM:(s + 1) * TM] = jnp.where(grow == 0, g_lo, jnp.where(grow == 1, g_hi, 0.0)).T

    tiles = range(STEP_TILES)
    h2 = [stage_mix(s) for s in tiles]
    logits = [_router_logits(h2[s], rwt_ref) for s in tiles]
    picks = [_select_experts(logits[s], rb_ref) for s in tiles]
    for s in tiles:
        stage_rank(s, *picks[s])
    cnt_ref[...] = carry_ref[...]


def _post_mixer(a, x_parts, mod, norm_g2, w_out, router_wt, router_bt):
    rows = STEP_ROWS
    idx_spec = pl.BlockSpec((STEP_TILES, 1, TM), lambda i: (i, 0, 0))
    return pl.pallas_call(
        functools.partial(_post_mixer_kernel, n_x=len(x_parts)),
        grid=(N_STEPS,),
        in_specs=[pl.BlockSpec((rows, D), lambda i: (i, 0))] + _token_specs(x_parts, STEP_TILES) + [
            pl.BlockSpec((1, 6, 1, D), lambda i: (_cond_of_tile(i * STEP_TILES), 0, 0, 0)), _ROW_SPEC,
            pl.BlockSpec((D, D), lambda i: (0, 0)),
            pl.BlockSpec((N_EXPERTS, D), lambda i: (0, 0)),
            pl.BlockSpec((N_EXPERTS, 1), lambda i: (0, 0))],
        out_specs=[pl.BlockSpec((rows, D), lambda i: (i, 0)), pl.BlockSpec((rows, LANES), lambda i: (i, 0)),
                   idx_spec, idx_spec, pl.BlockSpec((ROUTE_ROWS, LANES), lambda i: (0, 0))],
        out_shape=[jax.ShapeDtypeStruct((T, D), F32), jax.ShapeDtypeStruct((T, LANES), F32),
                   jax.ShapeDtypeStruct((N_TILES, 1, TM), jnp.int32),
                   jax.ShapeDtypeStruct((N_TILES, 1, TM), jnp.int32),
                   jax.ShapeDtypeStruct((ROUTE_ROWS, LANES), F32)],
        scratch_shapes=[pltpu.VMEM((ROUTE_ROWS, LANES), F32), pltpu.VMEM((TM, TM), BF16)],
        compiler_params=_cparams(),
        name="post_mixer",
    )(a, *x_parts, mod, norm_g2, w_out, router_wt, router_bt)


def _smem_step_spec(step_of_step):
    return pl.BlockSpec((STEP_TILES, 1, TM), lambda i, *_: (step_of_step(i), 0, 0), memory_space=pltpu.SMEM)


def _start_step_rows(make_copy):
    for tile in range(STEP_TILES):
        for r in range(TM):
            make_copy(tile, r).start(priority=r % 2)


def _dispatch_kernel(partial_ref, slot_ref, x1_ref, mod_ref, g2_ref, gates_ref,
                     xs_ref, buf_ref, zero_ref, sems, zero_sem):
    i = pl.program_id(0)
    cur = i % 2

    @pl.when(i == 0)
    def _():
        zero_ref[...] = jnp.zeros_like(zero_ref)

        def zero_copy(j):
            return pltpu.make_async_copy(zero_ref, xs_ref.at[pl.ds(j * TM_E, TM_E)], zero_sem)

        def start(j, carry):
            @pl.when(partial_ref[j] != 0)
            def _():
                zero_copy(j).start()
            return carry

        def wait(j, carry):
            @pl.when(partial_ref[j] != 0)
            def _():
                zero_copy(j).wait()
            return carry

        lax.fori_loop(0, N_TILES_E, start, 0)
        lax.fori_loop(0, N_TILES_E, wait, 0)

    def wait_step(b):
        pltpu.make_async_copy(buf_ref.at[b], xs_ref.at[pl.ds(0, STEP_ROWS)], sems.at[b]).wait()

    h2 = _modulate(x1_ref[...], g2_ref[...], mod_ref[0, 3], mod_ref[0, 4])
    gates = gates_ref[...]

    def send(b):
        buf_ref[b, :, :D] = h2
        buf_ref[b, :, D:] = gates
        _start_step_rows(lambda tile, r: pltpu.make_async_copy(
            buf_ref.at[b, pl.ds(tile * TM + r, 1)], xs_ref.at[pl.ds(slot_ref[tile, 0, r], 1)], sems.at[b]))

        @pl.when(i > 0)
        def _():
            wait_step(1 - b)

        @pl.when(i == N_STEPS - 1)
        def _():
            wait_step(b)

    for b in range(2):
        pl.when(cur == b)(functools.partial(send, b))


def _dispatch(partial, slot, x1, mod, norm_g2, gates):
    return pl.pallas_call(
        _dispatch_kernel,
        grid_spec=pltpu.PrefetchScalarGridSpec(
            num_scalar_prefetch=1,
            grid=(N_STEPS,),
            in_specs=[_smem_step_spec(lambda i: i), _STEP_SPEC, _STEP_MOD_SPEC,
                      pl.BlockSpec((1, D), lambda i, *_: (0, 0)),
                      pl.BlockSpec((STEP_ROWS, LANES), lambda i, *_: (i, 0))],
            out_specs=pl.BlockSpec(memory_space=pl.ANY),
            scratch_shapes=[pltpu.VMEM((2, STEP_ROWS, XW), F32), pltpu.VMEM((TM_E, XW), F32),
                            pltpu.SemaphoreType.DMA((2,)), pltpu.SemaphoreType.DMA(())],
        ),
        out_shape=jax.ShapeDtypeStruct((S_ROWS, XW), F32),
        compiler_params=_cparams(),
        name="moe_dispatch",
    )(partial, slot, x1, mod, norm_g2, gates)


_W_SHAPES = ((D, D_FF), (D, D_FF), (D_FF, D))


def _expert_kernel(ea_ref, eb_ref, next_a_ref, next_b_ref, a_higher_ref, nrows_ref,
                   x_ref, wg_hbm, wu_hbm, wd_hbm, y_ref, *scratch, layer):
    stage, w16, sems = scratch[0:3], scratch[3:6], scratch[6]
    j = pl.program_id(0)
    n = nrows_ref[j]
    prev = jnp.maximum(j - 1, 0)

    def fetch(slot, expert):
        return [pltpu.make_async_copy(w.at[layer, expert], s.at[slot], sems.at[slot])
                for w, s in zip((wg_hbm, wu_hbm, wd_hbm), stage)]

    for slot, e_ref, next_ref in ((0, ea_ref, next_a_ref), (1, eb_ref, next_b_ref)):
        @pl.when(j == 0)
        def _(slot=slot, e_ref=e_ref):
            for copy in fetch(slot, e_ref[0]):
                copy.start()

        @pl.when(jnp.logical_or(j == 0, e_ref[j] != e_ref[prev]))
        def _(slot=slot, e_ref=e_ref, next_ref=next_ref):
            for copy in fetch(slot, e_ref[j]):
                copy.wait()
            for s, d in zip(stage, w16):
                d[slot] = s[slot].astype(BF16)

            @pl.when(next_ref[j] >= 0)
            def _():
                for copy in fetch(slot, next_ref[j]):
                    copy.start()

    def run(rows):
        valid = lax.broadcasted_iota(jnp.int32, (rows, 1), 0) < n
        xe = jnp.where(valid, x_ref[:rows], 0.0)
        x = xe[:, :D].astype(BF16)
        g_lo, g_hi = xe[:, D:D + 1], xe[:, D + 1:D + 2]
        a_higher = a_higher_ref[j] != 0

        def ffn(slot, gate):
            wg_ref, wu_ref, wd_ref = w16
            act = _silu(_dot(x, wg_ref[slot])) * _dot(x, wu_ref[slot]) * gate
            return _dot(act.astype(BF16), wd_ref[slot])

        y_ref[:rows] = (ffn(0, jnp.where(a_higher, g_hi, g_lo))
                        + ffn(1, jnp.where(a_higher, g_lo, g_hi)))
        if rows < TM_E:
            y_ref[rows:] = jnp.zeros((TM_E - rows, D), F32)

    half = TM_E // 2
    pl.when(n > half)(functools.partial(run, TM_E))
    pl.when(jnp.logical_and(n > 0, n <= half))(functools.partial(run, half))

    @pl.when(n == 0)
    def _():
        y_ref[...] = jnp.zeros_like(y_ref)


def _experts(tile_ea, tile_eb, tile_next_a, tile_next_b, tile_a_higher, tile_rows, xs,
             w_gate, w_up, w_down, layer):
    any_spec = pl.BlockSpec(memory_space=pl.ANY)
    return pl.pallas_call(
        functools.partial(_expert_kernel, layer=layer),
        grid_spec=pltpu.PrefetchScalarGridSpec(
            num_scalar_prefetch=6,
            grid=(N_TILES_E,),
            in_specs=[pl.BlockSpec((TM_E, XW), lambda j, *_: (j, 0)), any_spec, any_spec, any_spec],
            out_specs=pl.BlockSpec((TM_E, D), lambda j, *_: (j, 0)),
            scratch_shapes=[pltpu.VMEM((2,) + s, F32) for s in _W_SHAPES]
                           + [pltpu.VMEM((2,) + s, BF16) for s in _W_SHAPES]
                           + [pltpu.SemaphoreType.DMA((2,))],
        ),
        out_shape=jax.ShapeDtypeStruct((S_ROWS, D), F32),
        compiler_params=_cparams(),
        name="moe_experts",
    )(tile_ea, tile_eb, tile_next_a, tile_next_b, tile_a_higher, tile_rows, xs, w_gate, w_up, w_down)


def _combine_kernel(slot_ref, next_slot_ref, x1_ref, mod_ref, gf_ref, y_ref, *rest, final):
    out_refs, (buf_ref, sems) = rest[:-2], rest[-2:]
    i = pl.program_id(0)
    cur = i % 2

    def gather(slots, b):
        _start_step_rows(lambda tile, r: pltpu.make_async_copy(
            y_ref.at[pl.ds(slots[tile, 0, r], 1)], buf_ref.at[b, pl.ds(tile * TM + r, 1)], sems.at[b]))

    @pl.when(i == 0)
    def _():
        gather(slot_ref, 0)

    def combine(b):
        @pl.when(i + 1 < N_STEPS)
        def _():
            gather(next_slot_ref, 1 - b)

        pltpu.make_async_copy(y_ref.at[pl.ds(0, STEP_ROWS)], buf_ref.at[b], sems.at[b]).wait()
        x2 = x1_ref[...] + mod_ref[0, 5] * buf_ref[b]
        if not final:
            out_refs[0][...] = x2
            return
        y = _rmsnorm(x2, gf_ref[...])
        yp_ref, ys_ref = out_refs

        @pl.when(i < PROMPT_STEPS)
        def _():
            yp_ref[...] = y

        @pl.when(i >= PROMPT_STEPS)
        def _():
            ys_ref[...] = y

    for b in range(2):
        pl.when(cur == b)(functools.partial(combine, b))


def _combine(slot, x1, mod, final_g, y_sorted, *, final):
    if final:
        out_specs = _token_specs((None, None), STEP_TILES)
        out_shape = [jax.ShapeDtypeStruct((T_PROMPT, D), F32), jax.ShapeDtypeStruct((T_SAMPLE, D), F32)]
    else:
        out_specs = [_STEP_SPEC]
        out_shape = [jax.ShapeDtypeStruct((T, D), F32)]
    return pl.pallas_call(
        functools.partial(_combine_kernel, final=final),
        grid=(N_STEPS,),
        in_specs=[_smem_step_spec(lambda i: i), _smem_step_spec(lambda i: jnp.minimum(i + 1, N_STEPS - 1)),
                  _STEP_SPEC, _STEP_MOD_SPEC, _ROW_SPEC, pl.BlockSpec(memory_space=pl.ANY)],
        out_specs=out_specs,
        scratch_shapes=[pltpu.VMEM((2, STEP_ROWS, D), F32), pltpu.SemaphoreType.DMA((2,))],
        out_shape=out_shape,
        compiler_params=_cparams(),
        name="moe_combine_final" if final else "moe_combine",
    )(slot, slot, x1, mod, final_g, y_sorted)


def _slot_kernel(off_ref, bucket_ref, rank_ref, slot_ref):
    bucket = bucket_ref[...]
    slot = rank_ref[...]
    for b in range(N_BUCKETS):
        slot = slot + jnp.where(bucket == b, off_ref[b], 0)
    slot_ref[...] = slot


def _slots(off, bucket, rank):
    whole = pl.BlockSpec(bucket.shape, lambda i, *_: (0, 0, 0))
    return pl.pallas_call(
        _slot_kernel,
        grid_spec=pltpu.PrefetchScalarGridSpec(
            num_scalar_prefetch=1, grid=(1,), in_specs=[whole, whole], out_specs=whole),
        out_shape=jax.ShapeDtypeStruct(bucket.shape, jnp.int32),
        compiler_params=_cparams(),
        name="moe_slots",
    )(off, bucket, rank)


def _next_different(e):
    j = jnp.arange(e.shape[0], dtype=jnp.int32)
    later_diff = jnp.logical_and(e[None, :] != e[:, None], j[None, :] > j[:, None])
    first = jnp.min(jnp.where(later_diff, j[None, :], e.shape[0]), axis=1)
    found = first < e.shape[0]
    return jnp.where(found, e[jnp.minimum(first, e.shape[0] - 1)], -1).astype(jnp.int32)


def _bucket_plan(bucket, rank, counts):
    cnt = counts[:N_BUCKETS, 0].astype(jnp.int32)
    tiles = (cnt + TM_E - 1) // TM_E
    tile_end = jnp.cumsum(tiles)
    tile_start = tile_end - tiles
    slot = _slots((tile_start * TM_E).astype(jnp.int32), bucket, rank)
    j = jnp.arange(N_TILES_E, dtype=jnp.int32)
    n_live = tile_end[-1]
    j_live = jnp.minimum(j, n_live - 1)
    b = jnp.sum((tile_end[None, :] <= j_live[:, None]).astype(jnp.int32), axis=1)
    b = jnp.minimum(b, N_BUCKETS - 1)
    rows = jnp.clip(cnt[b] - (j - tile_start[b]) * TM_E, 0, TM_E)
    rows = jnp.where(j < n_live, rows, 0).astype(jnp.int32)
    grp = b // N_PAIRS
    pair = b % N_PAIRS
    ea = grp * EPG + jnp.asarray(_SLOT_A, jnp.int32)[pair]
    eb = grp * EPG + jnp.asarray(_SLOT_B, jnp.int32)[pair]
    a_higher = jnp.asarray(_SLOT_A_IS_HIGHER, jnp.int32)[pair]
    return slot, (ea, eb, _next_different(ea), _next_different(eb), a_higher, rows)


def _moe(a, x_parts, mod, norm_g2, w_out, router_wt, router_bt, w_gate, w_up, w_down, layer, final_g,
         *, final):
    x1, gates, bucket, rank, counts = _post_mixer(a, x_parts, mod, norm_g2, w_out, router_wt, router_bt)
    slot, tile_plan = _bucket_plan(bucket, rank, counts)
    partial = (tile_plan[-1] < TM_E).astype(jnp.int32)
    xs = _dispatch(partial, slot, x1, mod, norm_g2, gates)
    ys = _experts(*tile_plan, xs, w_gate, w_up, w_down, layer)
    return _combine(slot, x1, mod, final_g, ys, final=final)


def kernel(x_prompt, x_sample, state_hgrn, c, c_ctx, w_mod, b_mod, norm_mix_g, norm_ffn_g, norm_final_g,
           hgrn_w_in, hgrn_lb_logits, hgrn_norm_g, hgrn_w_out, conv_w_in, conv_w, conv_w_out,
           router_w, router_b, moe_w_gate, moe_w_up, moe_w_down):
    x_parts = (x_prompt.reshape(T_PROMPT, D), x_sample.reshape(T_SAMPLE, D))
    cond = jnp.concatenate([c_ctx[None], c, jnp.zeros((COND_ROWS - N_COND, D), F32)], axis=0)
    mods = _adaln(cond, w_mod, b_mod)[:, :N_COND].reshape(2, N_COND, 6, 1, D)
    member_major = lambda w: w.reshape(N_GROUPS, EPG, -1).transpose(1, 0, 2).reshape(N_EXPERTS, -1)
    rwt = member_major(router_w.T)
    rbt = member_major(router_b.reshape(N_EXPERTS, 1))
    final_g = norm_final_g.reshape(1, D)
    bf = lambda w: w.astype(BF16)

    q, v, ff, fb, gg = _hgrn_proj(x_parts, mods[0], norm_mix_g[0:1], bf(hgrn_w_in[0]))
    s0 = state_hgrn[:, 0]
    o_f, s_f = _gla_fwd(q, v, ff, hgrn_lb_logits, s0)
    a, s_b = _gla_bwd(q, v, fb, hgrn_lb_logits, s0, o_f, gg, hgrn_norm_g[0:1])
    (x,) = _moe(a, x_parts, mods[0], norm_ffn_g[0:1], bf(hgrn_w_out[0]), rwt, rbt,
                moe_w_gate, moe_w_up, moe_w_down, 0, final_g, final=False)

    a = _conv_mixer(x, mods[1], norm_mix_g[1:2], bf(conv_w_in[0]), conv_w[0])
    y_p, y_s = _moe(a, (x,), mods[1], norm_ffn_g[1:2], bf(conv_w_out[0]), rwt, rbt,
                    moe_w_gate, moe_w_up, moe_w_down, 1, final_g, final=True)

    new_state = jnp.stack([s_f, s_b], axis=1)[:, None]
    return (y_p.reshape(N_PROMPT_SEQ, PROMPT_LEN, D), y_s.reshape(N_SAMPLE_SEQ, SAMPLE_LEN, D), new_state)
```

```python
import functools

import jax
import jax.numpy as jnp
from jax import lax
from jax.experimental import pallas as pl
from jax.experimental.pallas import tpu as pltpu

F32 = jnp.float32
BF16 = jnp.bfloat16

D = 1024
N_PROMPT_SEQ = 16
PROMPT_LEN = 256
N_SAMPLE_SEQ = 2
SAMPLE_LEN = 4096
GRID_W = 64
T_PROMPT = N_PROMPT_SEQ * PROMPT_LEN
T_SAMPLE = N_SAMPLE_SEQ * SAMPLE_LEN
T = T_PROMPT + T_SAMPLE
N_COND = 1 + N_SAMPLE_SEQ
COND_ROWS = 8

HEADS = 8
DK = 128
DV = 128
CHUNK = 64
N_EXPERTS = 16
N_GROUPS = 4
EPG = 4
N_PAIRS = 6
N_BUCKETS = N_GROUPS * N_PAIRS
D_FF = 512
EPS = 1e-6

TM = 256
N_TILES = T // TM
PROMPT_TILES = T_PROMPT // TM
TILES_PER_SAMPLE = SAMPLE_LEN // TM
CHUNKS_PER_TILE = TM // CHUNK
STEP_TILES = 2
STEP_ROWS = STEP_TILES * TM
N_STEPS = N_TILES // STEP_TILES
PROMPT_STEPS = PROMPT_TILES // STEP_TILES
MOVE_TILES = 4
MOVE_ROWS = MOVE_TILES * TM
N_MOVE_STEPS = N_TILES // MOVE_TILES
TM_E = 256
N_TILES_E = T // TM_E + N_BUCKETS
S_ROWS = N_TILES_E * TM_E
XW = D + 128
LANES = 128
MOD_TN = 1536

VMEM_LIMIT = 56 * 1024 * 1024

_SLOT_A = (0, 3, 3, 0, 0, 1)
_SLOT_B = (1, 1, 2, 2, 3, 2)
_SLOT_A_IS_HIGHER = tuple(int(a > b) for a, b in zip(_SLOT_A, _SLOT_B))


def _cparams():
    return pltpu.CompilerParams(dimension_semantics=("arbitrary",), vmem_limit_bytes=VMEM_LIMIT)


def _cond_of_tile(t):
    return jnp.where(t < PROMPT_TILES, 0, 1 + (t - PROMPT_TILES) // TILES_PER_SAMPLE)


def _rmsnorm(x, g):
    ms = jnp.mean(x * x, axis=-1, keepdims=True)
    return (x * lax.rsqrt(ms + EPS)) * g


def _modulate(x, g, shift, scale):
    return _rmsnorm(x, g) * (1.0 + scale) + shift


def _silu(x):
    return x * jax.nn.sigmoid(x)


def _dot(a, b):
    return jnp.dot(a, b, preferred_element_type=F32)


def _dot_nt(a, b):
    return lax.dot_general(a, b, (((1,), (1,)), ((), ())), preferred_element_type=F32)


def _dot_tn(a, b):
    return lax.dot_general(a, b, (((0,), (0,)), ((), ())), preferred_element_type=F32)


def _split_bf16(x):
    hi = x.astype(BF16)
    lo = (x - hi.astype(F32)).astype(BF16)
    return hi, lo


def _mod_kernel(cond_ref, w_ref, b_ref, o_ref):
    s = _silu(cond_ref[...])
    o_ref[0] = _dot(s.astype(BF16), w_ref[0].astype(BF16)) + b_ref[0]


def _adaln(cond, w_mod, b_mod):
    depth = w_mod.shape[0]
    return pl.pallas_call(
        _mod_kernel,
        grid=(depth, 6 * D // MOD_TN),
        in_specs=[
            pl.BlockSpec((COND_ROWS, D), lambda i, j: (0, 0)),
            pl.BlockSpec((1, D, MOD_TN), lambda i, j: (i, 0, j)),
            pl.BlockSpec((1, 1, MOD_TN), lambda i, j: (i, 0, j)),
        ],
        out_specs=pl.BlockSpec((1, COND_ROWS, MOD_TN), lambda i, j: (i, 0, j)),
        out_shape=jax.ShapeDtypeStruct((depth, COND_ROWS, 6 * D), F32),
        compiler_params=pltpu.CompilerParams(
            dimension_semantics=("arbitrary", "arbitrary"), vmem_limit_bytes=VMEM_LIMIT),
        name="adaln",
    )(cond, w_mod, b_mod.reshape(depth, 1, 6 * D))


_MOD_SPEC = pl.BlockSpec((1, 6, 1, D), lambda i: (_cond_of_tile(i), 0, 0, 0))
_ROW_SPEC = pl.BlockSpec((1, D), lambda i: (0, 0))
_TILE_SPEC = pl.BlockSpec((TM, D), lambda i: (i, 0))
_STEP_MOD_SPEC = pl.BlockSpec((1, 6, 1, D), lambda i, *_: (_cond_of_tile(i * STEP_TILES), 0, 0, 0))
_STEP_SPEC = pl.BlockSpec((STEP_ROWS, D), lambda i, *_: (i, 0))
_SUB_ROWS = tuple(slice(s * TM, (s + 1) * TM) for s in range(STEP_TILES))


def _token_tile(x_refs, rows=slice(None), prompt_steps=PROMPT_TILES):
    if len(x_refs) == 1:
        return x_refs[0][rows]
    return jnp.where(pl.program_id(0) < prompt_steps, x_refs[0][rows], x_refs[1][rows])


def _token_specs(x_parts, tiles_per_step=1):
    rows = tiles_per_step * TM
    if len(x_parts) == 1:
        return [pl.BlockSpec((rows, D), lambda i: (i, 0))]
    prompt_steps = T_PROMPT // rows
    return [pl.BlockSpec((rows, D), lambda i: (jnp.minimum(i, prompt_steps - 1), 0)),
            pl.BlockSpec((rows, D), lambda i: (jnp.maximum(i - prompt_steps, 0), 0))]


def _hgrn_proj_kernel(xp_ref, xs_ref, mod_ref, g_ref, w_ref, q_ref, v_ref, ff_ref, fb_ref, gg_ref):
    h = [_modulate(_token_tile((xp_ref, xs_ref), rows, PROMPT_STEPS),
                   g_ref[...], mod_ref[0, 0], mod_ref[0, 1]).astype(BF16) for rows in _SUB_ROWS]
    for rows, h_tile in zip(_SUB_ROWS, h):
        q_ref[rows] = (_dot(h_tile, w_ref[:, 0 * D:1 * D]) * (DK ** -0.5)).astype(BF16)
        v_ref[rows] = _dot(h_tile, w_ref[:, 1 * D:2 * D]).astype(BF16)
        ff_ref[rows] = _dot(h_tile, w_ref[:, 2 * D:3 * D])
        fb_ref[rows] = _dot(h_tile, w_ref[:, 3 * D:4 * D])
        gg_ref[rows] = _dot(h_tile, w_ref[:, 4 * D:5 * D]).astype(BF16)


def _hgrn_proj(x_parts, mod, norm_g, w_in):
    bf = jax.ShapeDtypeStruct((T, D), BF16)
    f32 = jax.ShapeDtypeStruct((T, D), F32)
    return pl.pallas_call(
        _hgrn_proj_kernel,
        grid=(N_STEPS,),
        in_specs=_token_specs(x_parts, STEP_TILES) + [_STEP_MOD_SPEC, _ROW_SPEC,
                                                      pl.BlockSpec((D, 5 * D), lambda i: (0, 0))],
        out_specs=[_STEP_SPEC] * 5,
        out_shape=[bf, bf, f32, f32, bf],
        compiler_params=_cparams(),
        name="hgrn_proj",
    )(*x_parts, mod, norm_g, w_in)


STEPS_PER_SAMPLE = SAMPLE_LEN // STEP_ROWS


def _gla_step(block, q_ref, v_ref, f_ref, lbl_ref, s0_ref, st_ref, st_first_ref, ma_ref, o_ref,
              *, reverse, finish=None):
    is_prompt = block < PROMPT_STEPS
    rel = block - PROMPT_STEPS
    edge = (STEPS_PER_SAMPLE - 1) if reverse else 0
    sample_start = jnp.logical_and(rel >= 0, rel % STEPS_PER_SAMPLE == edge)

    @pl.when(is_prompt)
    def _():
        st_ref[...] = jnp.zeros_like(st_ref)

    @pl.when(sample_start)
    def _():
        for h in range(HEADS):
            st_ref[h] = s0_ref[0, 0, h].T

    logits = lbl_ref[...]
    e = jnp.exp(logits - jnp.max(logits, axis=0, keepdims=True))
    lb = e[0:1] / jnp.sum(e, axis=0, keepdims=True)

    ref_idx = (CHUNK - 1 - CHUNK // 2) if reverse else CHUNK // 2
    last_idx = 0 if reverse else CHUNK - 1
    shift = CHUNK.bit_length() - 1

    row = lax.broadcasted_iota(jnp.int32, (TM, TM), 0)
    col = lax.broadcasted_iota(jnp.int32, (TM, TM), 1)
    same = (row >> shift) == (col >> shift)
    ref_row = ((row >> shift) << shift) + ref_idx
    keep = jnp.logical_and(same, (row <= col) if reverse else (row >= col))

    @pl.when(pl.program_id(0) == 0)
    def _():
        at_ref = jnp.logical_and(same, (ref_row <= col) if reverse else (ref_row >= col))
        ma_ref[...] = (keep.astype(F32) - at_ref.astype(F32)).astype(BF16)

    ma = ma_ref[...]

    rr = lax.broadcasted_iota(jnp.int32, (2 * CHUNKS_PER_TILE, TM), 0)
    cc = lax.broadcasted_iota(jnp.int32, (2 * CHUNKS_PER_TILE, TM), 1)
    chunk = rr & (CHUNKS_PER_TILE - 1)
    target = chunk * CHUNK + jnp.where(rr < CHUNKS_PER_TILE, ref_idx, last_idx)
    covered = (cc >= target) if reverse else (cc <= target)
    rm = jnp.logical_and((cc >> shift) == chunk, covered).astype(BF16)

    chunk_rows = [slice(c * CHUNK, (c + 1) * CHUNK) for c in range(CHUNKS_PER_TILE)]
    order = range(CHUNKS_PER_TILE - 1, -1, -1) if reverse else range(CHUNKS_PER_TILE)
    pair_w = 2 * DK
    n_pairs = HEADS // 2
    tile_order = _SUB_ROWS[::-1] if reverse else _SUB_ROWS

    def per_chunk_scale(x, scale):
        return jnp.concatenate([x[rs] * scale[c:c + 1] for c, rs in enumerate(chunk_rows)], axis=0)

    def pair_cols(p):
        return slice(p * pair_w, (p + 1) * pair_w)

    def stage_decay(k, p):
        lbp = lb[:, pair_cols(p)]
        f = lbp + (1.0 - lbp) * jax.nn.sigmoid(f_ref[tile_order[k], pair_cols(p)])
        lf_hi, lf_lo = _split_bf16(jnp.log(f))
        z = _dot(ma, lf_hi) + _dot(ma, lf_lo)
        marks = _dot(rm, lf_hi) + _dot(rm, lf_lo)
        return 1.0 - f, z, marks

    def stage_scores(k, p, kk, z, marks):
        ref, last = marks[:CHUNKS_PER_TILE], marks[CHUNKS_PER_TILE:]
        qa32 = q_ref[tile_order[k], pair_cols(p)].astype(F32) * jnp.exp(z)
        kb32 = kk * jnp.exp(-z)
        qa = qa32.astype(BF16)
        kb = kb32.astype(BF16)
        qe = per_chunk_scale(qa32, jnp.exp(ref)).astype(BF16)
        kd = per_chunk_scale(kb32, jnp.exp(last - ref)).astype(BF16)
        v = v_ref[tile_order[k], pair_cols(p)]
        scores, incr = [], []
        for hh in range(2):
            ls = slice(hh * DK, (hh + 1) * DK)
            scores.append(_dot_nt(qa[:, ls], kb[:, ls]))
            kd_h = kd[:, ls]
            kd_blocks = jnp.concatenate(
                [jnp.concatenate([part for part in (jnp.zeros((c * CHUNK, DK), BF16), kd_h[rs],
                                                    jnp.zeros((TM - (c + 1) * CHUNK, DK), BF16))
                                  if part.shape[0]], axis=0)
                 for c, rs in enumerate(chunk_rows)], axis=1)
            incr.append(_dot_tn(v[:, ls], kd_blocks))
        return scores, incr, qe, jnp.exp(last), v

    def stage_output(k, p, scores, incr, qe, decay, v):
        base = tile_order[k].start
        for hh in range(2):
            h = 2 * p + hh
            ls = slice(hh * DK, (hh + 1) * DK)
            hs = slice(h * DK, (h + 1) * DK)
            o_intra = _dot(jnp.where(keep, scores[hh], 0.0).astype(BF16), v[:, ls])
            st = st_ref[h]
            if k > 0:
                st = jnp.where(is_prompt, 0.0, st)
            for c in order:
                rs = chunk_rows[c]
                o_ref[base + rs.start:base + rs.stop, hs] = (
                    o_intra[rs] + _dot_nt(qe[rs, ls], st.astype(BF16)))
                st = st * decay[c:c + 1, ls] + incr[hh][:, c * DK:(c + 1) * DK]
            st_ref[h] = st
            if k == 0:
                st_first_ref[h] = st
        if finish is not None:
            return (tile_order[k], p)

    def stage_finish(k, p, rows, pair):
        finish(rows, pair)

    units = [(k, p) for k in range(STEP_TILES) for p in range(n_pairs)]
    stages = [stage_decay, stage_scores, stage_output] + ([stage_finish] if finish is not None else [])
    carried = [dict() for _ in stages]
    for step in range(len(units) + len(stages) - 1):
        for depth, stage in enumerate(stages):
            u = step - depth
            if 0 <= u < len(units):
                out = stage(*units[u], *(carried[depth].pop(u) if depth else ()))
                if depth + 1 < len(stages):
                    carried[depth + 1][u] = out


def _store_prompt_states(block, st_ref, st_first_ref, snew_ref, *, reverse):
    first, second = (1, 0) if reverse else (0, 1)

    @pl.when(block < PROMPT_STEPS)
    def _():
        for h in range(HEADS):
            snew_ref[first, h] = st_first_ref[h].T
            snew_ref[second, h] = st_ref[h].T


def _gla_fwd_kernel(q_ref, v_ref, f_ref, lbl_ref, s0_ref, o_ref, snew_ref, st_ref, st_first_ref, ma_ref):
    block = pl.program_id(0)
    _gla_step(block, q_ref, v_ref, f_ref, lbl_ref, s0_ref, st_ref, st_first_ref, ma_ref, o_ref,
              reverse=False)
    _store_prompt_states(block, st_ref, st_first_ref, snew_ref, reverse=False)


def _gla_bwd_kernel(q_ref, v_ref, f_ref, lbl_ref, s0_ref, of_ref, gg_ref, ng_ref,
                    a_ref, snew_ref, st_ref, st_first_ref, ma_ref, ob_ref):
    block = N_STEPS - 1 - pl.program_id(0)

    def finish(rows, p):
        for h in (2 * p, 2 * p + 1):
            hs = slice(h * DV, (h + 1) * DV)
            o = of_ref[rows, hs] + ob_ref[rows, hs]
            a_ref[rows, hs] = (_rmsnorm(o, ng_ref[...]) * _silu(gg_ref[rows, hs].astype(F32))).astype(BF16)

    _gla_step(block, q_ref, v_ref, f_ref, lbl_ref, s0_ref, st_ref, st_first_ref, ma_ref, ob_ref,
              reverse=True, finish=finish)
    _store_prompt_states(block, st_ref, st_first_ref, snew_ref, reverse=True)


def _state_specs(direction, block_of_step):
    sample = lambda i: jnp.clip((block_of_step(i) - PROMPT_STEPS) // STEPS_PER_SAMPLE, 0, N_SAMPLE_SEQ - 1)
    s0 = pl.BlockSpec((1, 1, HEADS, DK, DV), lambda i: (sample(i), direction, 0, 0, 0))
    snew = pl.BlockSpec((STEP_TILES, HEADS, DK, DV),
                        lambda i: (jnp.minimum(block_of_step(i), PROMPT_STEPS - 1), 0, 0, 0))
    return s0, snew


_GLA_STATE_SCRATCH = [pltpu.VMEM((HEADS, DV, DK), F32), pltpu.VMEM((HEADS, DV, DK), F32),
                      pltpu.VMEM((TM, TM), BF16)]


def _gla_fwd(q, v, ff, lb_logits, s0):
    s0_spec, snew_spec = _state_specs(0, lambda i: i)
    return pl.pallas_call(
        _gla_fwd_kernel,
        grid=(N_STEPS,),
        in_specs=[_STEP_SPEC, _STEP_SPEC, _STEP_SPEC,
                  pl.BlockSpec(lb_logits.shape, lambda i: (0, 0)), s0_spec],
        out_specs=[_STEP_SPEC, snew_spec],
        out_shape=[jax.ShapeDtypeStruct((T, D), F32),
                   jax.ShapeDtypeStruct((N_PROMPT_SEQ, HEADS, DK, DV), F32)],
        scratch_shapes=_GLA_STATE_SCRATCH,
        compiler_params=_cparams(),
        name="gla_fwd",
    )(q, v, ff, lb_logits, s0)


def _gla_bwd(q, v, fb, lb_logits, s0, o_f, gg, head_norm_g):
    rev = lambda i: N_STEPS - 1 - i
    block = pl.BlockSpec((STEP_ROWS, D), lambda i: (rev(i), 0))
    s0_spec, snew_spec = _state_specs(1, rev)
    return pl.pallas_call(
        _gla_bwd_kernel,
        grid=(N_STEPS,),
        in_specs=[block, block, block, pl.BlockSpec(lb_logits.shape, lambda i: (0, 0)), s0_spec,
                  block, block, pl.BlockSpec((1, DV), lambda i: (0, 0))],
        out_specs=[block, snew_spec],
        out_shape=[jax.ShapeDtypeStruct((T, D), BF16),
                   jax.ShapeDtypeStruct((N_PROMPT_SEQ, HEADS, DK, DV), F32)],
        scratch_shapes=_GLA_STATE_SCRATCH + [pltpu.VMEM((STEP_ROWS, D), F32)],
        compiler_params=_cparams(),
        name="gla_bwd",
    )(q, v, fb, lb_logits, s0, o_f, gg, head_norm_g)


def _conv_mixer_kernel(x_ref, mod_ref, g_ref, w_ref, cw_ref, a_ref):
    seg = jnp.where(pl.program_id(0) < PROMPT_STEPS, PROMPT_LEN, GRID_W)
    pos = lax.broadcasted_iota(jnp.int32, (TM, 1), 0) & (seg - 1)

    def project(rows):
        h = _modulate(x_ref[rows], g_ref[...], mod_ref[0, 0], mod_ref[0, 1]).astype(BF16)
        return [_dot(h, w_ref[:, k * D:(k + 1) * D]) for k in range(3)]

    def convolve(rows, bg, cg, x_in):
        u = cg * x_in
        prev = jnp.where(pos == 0, 0.0, pltpu.roll(u, 1, axis=0))
        nxt = jnp.where(pos == seg - 1, 0.0, pltpu.roll(u, TM - 1, axis=0))
        conv = cw_ref[0:1] * prev + cw_ref[1:2] * u + cw_ref[2:3] * nxt
        a_ref[rows] = (bg * conv).astype(BF16)

    projected = [project(rows) for rows in _SUB_ROWS]
    for rows, parts in zip(_SUB_ROWS, projected):
        convolve(rows, *parts)


def _conv_mixer(x, mod, norm_g, w_in, conv_w):
    return pl.pallas_call(
        _conv_mixer_kernel,
        grid=(N_STEPS,),
        in_specs=[_STEP_SPEC, _STEP_MOD_SPEC, _ROW_SPEC, pl.BlockSpec((D, 3 * D), lambda i: (0, 0)),
                  pl.BlockSpec((3, D), lambda i: (0, 0))],
        out_specs=_STEP_SPEC,
        out_shape=jax.ShapeDtypeStruct((T, D), BF16),
        compiler_params=_cparams(),
        name="conv_mixer",
    )(x, mod, norm_g, w_in, conv_w)


ROUTE_ROWS = 32


def _first_member(vals, target):
    idx = jnp.full_like(target, float(EPG - 1))
    for j in range(EPG - 2, -1, -1):
        idx = jnp.where(vals[j] == target, float(j), idx)
    return idx


def _router_logits(h2, rwt_ref):
    h_hi, h_lo = _split_bf16(h2)
    w_hi, w_lo = _split_bf16(rwt_ref[...])
    return _dot_nt(w_hi, h_hi) + (_dot_nt(w_hi, h_lo) + _dot_nt(w_lo, h_hi))


def _select_experts(logits, rb_ref):
    scores = jax.nn.sigmoid(logits)
    sel = scores + rb_ref[...]
    member = [sel[j * N_GROUPS:(j + 1) * N_GROUPS] for j in range(EPG)]
    m1 = functools.reduce(jnp.maximum, member)
    i1 = _first_member(member, m1)
    rest = [jnp.where(i1 == float(j), -jnp.inf, member[j]) for j in range(EPG)]
    m2 = functools.reduce(jnp.maximum, rest)
    i2 = _first_member(rest, m2)
    group_score = m1 + m2

    best, grp, first, second = group_score[0:1], jnp.zeros((1, TM), F32), i1[0:1], i2[0:1]
    for g in range(1, N_GROUPS):
        upd = group_score[g:g + 1] > best
        best = jnp.where(upd, group_score[g:g + 1], best)
        grp = jnp.where(upd, float(g), grp)
        first = jnp.where(upd, i1[g:g + 1], first)
        second = jnp.where(upd, i2[g:g + 1], second)
    a = jnp.minimum(first, second)
    b = jnp.maximum(first, second)
    row = lax.broadcasted_iota(jnp.int32, (N_EXPERTS, TM), 0).astype(F32)
    s_lo = jnp.sum(jnp.where(row == a * N_GROUPS + grp, scores, 0.0), axis=0, keepdims=True)
    s_hi = jnp.sum(jnp.where(row == b * N_GROUPS + grp, scores, 0.0), axis=0, keepdims=True)
    tot = s_lo + s_hi
    pair = jnp.where(a == 0.0, jnp.where(b == 1.0, 0.0, jnp.where(b == 2.0, 3.0, 4.0)),
                     jnp.where(a == 1.0, jnp.where(b == 2.0, 5.0, 1.0), 2.0))
    return grp * N_PAIRS + pair, s_lo / tot, s_hi / tot


def _post_mixer_kernel(*refs, n_x):
    a_ref, x_refs = refs[0], refs[1:1 + n_x]
    (mod_ref, g2_ref, wo_ref, rwt_ref, rb_ref,
     x1_ref, gates_ref, bucket_ref, rank_ref, cnt_ref, carry_ref, earlier_ref) = refs[1 + n_x:]

    @pl.when(pl.program_id(0) == 0)
    def _():
        carry_ref[...] = jnp.zeros_like(carry_ref)
        s_idx = lax.broadcasted_iota(jnp.int32, (TM, TM), 0)
        t_idx = lax.broadcasted_iota(jnp.int32, (TM, TM), 1)
        earlier_ref[...] = (s_idx < t_idx).astype(BF16)

    def stage_mix(s):
        rows = slice(s * TM, (s + 1) * TM)
        x1 = (_token_tile(x_refs, rows, PROMPT_STEPS)
              + mod_ref[0, 2] * _dot(a_ref[rows], wo_ref[...]))
        x1_ref[rows] = x1
        return _modulate(x1, g2_ref[...], mod_ref[0, 3], mod_ref[0, 4])

    def stage_rank(s, bucket, g_lo, g_hi):
        onehot = lax.broadcasted_iota(jnp.int32, (ROUTE_ROWS, TM), 0).astype(F32) == bucket
        before = _dot(onehot.astype(BF16), earlier_ref[...])
        oh = onehot.astype(F32)
        carry = carry_ref[...]
        rank = jnp.sum(oh * (before + carry[:, 0:1]), axis=0, keepdims=True)
        carry_ref[...] = carry + jnp.sum(oh, axis=1, keepdims=True)
        bucket_ref[s] = bucket.astype(jnp.int32)
        rank_ref[s] = rank.astype(jnp.int32)
        grow = lax.broadcasted_iota(jnp.int32, (LANES, TM), 0)
        gates_ref[s * TM:(s + 1) * TM] = jnp.where(grow == 0, g_lo, jnp.where(grow == 1, g_hi, 0.0)).T

    tiles = range(STEP_TILES)
    h2 = [stage_mix(s) for s in tiles]
    logits = [_router_logits(h2[s], rwt_ref) for s in tiles]
    picks = [_select_experts(logits[s], rb_ref) for s in tiles]
    for s in tiles:
        stage_rank(s, *picks[s])
    cnt_ref[...] = carry_ref[...]


def _post_mixer(a, x_parts, mod, norm_g2, w_out, router_wt, router_bt):
    rows = STEP_ROWS
    idx_spec = pl.BlockSpec((STEP_TILES, 1, TM), lambda i: (i, 0, 0))
    return pl.pallas_call(
        functools.partial(_post_mixer_kernel, n_x=len(x_parts)),
        grid=(N_STEPS,),
        in_specs=[pl.BlockSpec((rows, D), lambda i: (i, 0))] + _token_specs(x_parts, STEP_TILES) + [
            pl.BlockSpec((1, 6, 1, D), lambda i: (_cond_of_tile(i * STEP_TILES), 0, 0, 0)), _ROW_SPEC,
            pl.BlockSpec((D, D), lambda i: (0, 0)),
            pl.BlockSpec((N_EXPERTS, D), lambda i: (0, 0)),
            pl.BlockSpec((N_EXPERTS, 1), lambda i: (0, 0))],
        out_specs=[pl.BlockSpec((rows, D), lambda i: (i, 0)), pl.BlockSpec((rows, LANES), lambda i: (i, 0)),
                   idx_spec, idx_spec, pl.BlockSpec((ROUTE_ROWS, LANES), lambda i: (0, 0))],
        out_shape=[jax.ShapeDtypeStruct((T, D), F32), jax.ShapeDtypeStruct((T, LANES), F32),
                   jax.ShapeDtypeStruct((N_TILES, 1, TM), jnp.int32),
                   jax.ShapeDtypeStruct((N_TILES, 1, TM), jnp.int32),
                   jax.ShapeDtypeStruct((ROUTE_ROWS, LANES), F32)],
        scratch_shapes=[pltpu.VMEM((ROUTE_ROWS, LANES), F32), pltpu.VMEM((TM, TM), BF16)],
        compiler_params=_cparams(),
        name="post_mixer",
    )(a, *x_parts, mod, norm_g2, w_out, router_wt, router_bt)


_MOVE_SPEC = pl.BlockSpec((MOVE_ROWS, D), lambda i, *_: (i, 0))
_MOVE_MOD_SPEC = pl.BlockSpec((1, 6, 1, D), lambda i, *_: (_cond_of_tile(i * MOVE_TILES), 0, 0, 0))


def _smem_step_spec(step_of_step):
    return pl.BlockSpec((MOVE_TILES, 1, TM), lambda i, *_: (step_of_step(i), 0, 0), memory_space=pltpu.SMEM)


def _start_step_rows(make_copy):
    for tile in range(MOVE_TILES):
        for r in range(TM):
            make_copy(tile, r).start(priority=r % 2)


def _dispatch_kernel(partial_ref, slot_ref, x1_ref, mod_ref, g2_ref, gates_ref,
                     xs_ref, buf_ref, zero_ref, sems, zero_sem):
    i = pl.program_id(0)
    cur = i % 2

    @pl.when(i == 0)
    def _():
        zero_ref[...] = jnp.zeros_like(zero_ref)

        def zero_copy(j):
            return pltpu.make_async_copy(zero_ref, xs_ref.at[pl.ds(j * TM_E, TM_E)], zero_sem)

        def start(j, carry):
            @pl.when(partial_ref[j] != 0)
            def _():
                zero_copy(j).start()
            return carry

        def wait(j, carry):
            @pl.when(partial_ref[j] != 0)
            def _():
                zero_copy(j).wait()
            return carry

        lax.fori_loop(0, N_TILES_E, start, 0)
        lax.fori_loop(0, N_TILES_E, wait, 0)

    def wait_step(b):
        pltpu.make_async_copy(buf_ref.at[b], xs_ref.at[pl.ds(0, MOVE_ROWS)], sems.at[b]).wait()

    h2 = _modulate(x1_ref[...], g2_ref[...], mod_ref[0, 3], mod_ref[0, 4])
    gates = gates_ref[...]

    def send(b):
        buf_ref[b, :, :D] = h2
        buf_ref[b, :, D:] = gates
        _start_step_rows(lambda tile, r: pltpu.make_async_copy(
            buf_ref.at[b, pl.ds(tile * TM + r, 1)], xs_ref.at[pl.ds(slot_ref[tile, 0, r], 1)], sems.at[b]))

        @pl.when(i > 0)
        def _():
            wait_step(1 - b)

        @pl.when(i == N_MOVE_STEPS - 1)
        def _():
            wait_step(b)

    for b in range(2):
        pl.when(cur == b)(functools.partial(send, b))


def _dispatch(partial, slot, x1, mod, norm_g2, gates):
    return pl.pallas_call(
        _dispatch_kernel,
        grid_spec=pltpu.PrefetchScalarGridSpec(
            num_scalar_prefetch=1,
            grid=(N_MOVE_STEPS,),
            in_specs=[_smem_step_spec(lambda i: i), _MOVE_SPEC, _MOVE_MOD_SPEC,
                      pl.BlockSpec((1, D), lambda i, *_: (0, 0)),
                      pl.BlockSpec((MOVE_ROWS, LANES), lambda i, *_: (i, 0))],
            out_specs=pl.BlockSpec(memory_space=pl.ANY),
            scratch_shapes=[pltpu.VMEM((2, MOVE_ROWS, XW), F32), pltpu.VMEM((TM_E, XW), F32),
                            pltpu.SemaphoreType.DMA((2,)), pltpu.SemaphoreType.DMA(())],
        ),
        out_shape=jax.ShapeDtypeStruct((S_ROWS, XW), F32),
        compiler_params=_cparams(),
        name="moe_dispatch",
    )(partial, slot, x1, mod, norm_g2, gates)


_W_SHAPES = ((D, D_FF), (D, D_FF), (D_FF, D))


def _expert_kernel(ea_ref, eb_ref, next_a_ref, next_b_ref, a_higher_ref, nrows_ref,
                   x_ref, wg_hbm, wu_hbm, wd_hbm, y_ref, *scratch, layer):
    stage, w16, sems = scratch[0:3], scratch[3:6], scratch[6]
    j = pl.program_id(0)
    n = nrows_ref[j]
    prev = jnp.maximum(j - 1, 0)

    def fetch(slot, expert):
        return [pltpu.make_async_copy(w.at[layer, expert], s.at[slot], sems.at[slot])
                for w, s in zip((wg_hbm, wu_hbm, wd_hbm), stage)]

    for slot, e_ref, next_ref in ((0, ea_ref, next_a_ref), (1, eb_ref, next_b_ref)):
        @pl.when(j == 0)
        def _(slot=slot, e_ref=e_ref):
            for copy in fetch(slot, e_ref[0]):
                copy.start()

        @pl.when(jnp.logical_or(j == 0, e_ref[j] != e_ref[prev]))
        def _(slot=slot, e_ref=e_ref, next_ref=next_ref):
            for copy in fetch(slot, e_ref[j]):
                copy.wait()
            for s, d in zip(stage, w16):
                d[slot] = s[slot].astype(BF16)

            @pl.when(next_ref[j] >= 0)
            def _():
                for copy in fetch(slot, next_ref[j]):
                    copy.start()

    def run(rows):
        valid = lax.broadcasted_iota(jnp.int32, (rows, 1), 0) < n
        xe = jnp.where(valid, x_ref[:rows], 0.0)
        x = xe[:, :D].astype(BF16)
        g_lo, g_hi = xe[:, D:D + 1], xe[:, D + 1:D + 2]
        a_higher = a_higher_ref[j] != 0

        def ffn(slot, gate):
            wg_ref, wu_ref, wd_ref = w16
            act = _silu(_dot(x, wg_ref[slot])) * _dot(x, wu_ref[slot]) * gate
            return _dot(act.astype(BF16), wd_ref[slot])

        y_ref[:rows] = (ffn(0, jnp.where(a_higher, g_hi, g_lo))
                        + ffn(1, jnp.where(a_higher, g_lo, g_hi)))
        if rows < TM_E:
            y_ref[rows:] = jnp.zeros((TM_E - rows, D), F32)

    half = TM_E // 2
    pl.when(n > half)(functools.partial(run, TM_E))
    pl.when(jnp.logical_and(n > 0, n <= half))(functools.partial(run, half))

    @pl.when(n == 0)
    def _():
        y_ref[...] = jnp.zeros_like(y_ref)


def _experts(tile_ea, tile_eb, tile_next_a, tile_next_b, tile_a_higher, tile_rows, xs,
             w_gate, w_up, w_down, layer):
    any_spec = pl.BlockSpec(memory_space=pl.ANY)
    return pl.pallas_call(
        functools.partial(_expert_kernel, layer=layer),
        grid_spec=pltpu.PrefetchScalarGridSpec(
            num_scalar_prefetch=6,
            grid=(N_TILES_E,),
            in_specs=[pl.BlockSpec((TM_E, XW), lambda j, *_: (j, 0)), any_spec, any_spec, any_spec],
            out_specs=pl.BlockSpec((TM_E, D), lambda j, *_: (j, 0)),
            scratch_shapes=[pltpu.VMEM((2,) + s, F32) for s in _W_SHAPES]
                           + [pltpu.VMEM((2,) + s, BF16) for s in _W_SHAPES]
                           + [pltpu.SemaphoreType.DMA((2,))],
        ),
        out_shape=jax.ShapeDtypeStruct((S_ROWS, D), F32),
        compiler_params=_cparams(),
        name="moe_experts",
    )(tile_ea, tile_eb, tile_next_a, tile_next_b, tile_a_higher, tile_rows, xs, w_gate, w_up, w_down)


def _combine_kernel(slot_ref, next_slot_ref, x1_ref, mod_ref, gf_ref, y_ref, *rest, final):
    out_refs, (buf_ref, sems) = rest[:-2], rest[-2:]
    i = pl.program_id(0)
    cur = i % 2

    def gather(slots, b):
        _start_step_rows(lambda tile, r: pltpu.make_async_copy(
            y_ref.at[pl.ds(slots[tile, 0, r], 1)], buf_ref.at[b, pl.ds(tile * TM + r, 1)], sems.at[b]))

    @pl.when(i == 0)
    def _():
        gather(slot_ref, 0)

    def combine(b):
        @pl.when(i + 1 < N_MOVE_STEPS)
        def _():
            gather(next_slot_ref, 1 - b)

        pltpu.make_async_copy(y_ref.at[pl.ds(0, MOVE_ROWS)], buf_ref.at[b], sems.at[b]).wait()
        x2 = x1_ref[...] + mod_ref[0, 5] * buf_ref[b]
        if not final:
            out_refs[0][...] = x2
            return
        y = _rmsnorm(x2, gf_ref[...])
        yp_ref, ys_ref = out_refs

        @pl.when(i < T_PROMPT // MOVE_ROWS)
        def _():
            yp_ref[...] = y

        @pl.when(i >= T_PROMPT // MOVE_ROWS)
        def _():
            ys_ref[...] = y

    for b in range(2):
        pl.when(cur == b)(functools.partial(combine, b))


def _combine(slot, x1, mod, final_g, y_sorted, *, final):
    if final:
        out_specs = _token_specs((None, None), MOVE_TILES)
        out_shape = [jax.ShapeDtypeStruct((T_PROMPT, D), F32), jax.ShapeDtypeStruct((T_SAMPLE, D), F32)]
    else:
        out_specs = [_MOVE_SPEC]
        out_shape = [jax.ShapeDtypeStruct((T, D), F32)]
    return pl.pallas_call(
        functools.partial(_combine_kernel, final=final),
        grid=(N_MOVE_STEPS,),
        in_specs=[_smem_step_spec(lambda i: i), _smem_step_spec(lambda i: jnp.minimum(i + 1, N_MOVE_STEPS - 1)),
                  _MOVE_SPEC, _MOVE_MOD_SPEC, _ROW_SPEC, pl.BlockSpec(memory_space=pl.ANY)],
        out_specs=out_specs,
        scratch_shapes=[pltpu.VMEM((2, MOVE_ROWS, D), F32), pltpu.SemaphoreType.DMA((2,))],
        out_shape=out_shape,
        compiler_params=_cparams(),
        name="moe_combine_final" if final else "moe_combine",
    )(slot, slot, x1, mod, final_g, y_sorted)


def _slot_kernel(off_ref, bucket_ref, rank_ref, slot_ref):
    bucket = bucket_ref[...]
    slot = rank_ref[...]
    for b in range(N_BUCKETS):
        slot = slot + jnp.where(bucket == b, off_ref[b], 0)
    slot_ref[...] = slot


def _slots(off, bucket, rank):
    whole = pl.BlockSpec(bucket.shape, lambda i, *_: (0, 0, 0))
    return pl.pallas_call(
        _slot_kernel,
        grid_spec=pltpu.PrefetchScalarGridSpec(
            num_scalar_prefetch=1, grid=(1,), in_specs=[whole, whole], out_specs=whole),
        out_shape=jax.ShapeDtypeStruct(bucket.shape, jnp.int32),
        compiler_params=_cparams(),
        name="moe_slots",
    )(off, bucket, rank)


def _next_different(e):
    j = jnp.arange(e.shape[0], dtype=jnp.int32)
    later_diff = jnp.logical_and(e[None, :] != e[:, None], j[None, :] > j[:, None])
    first = jnp.min(jnp.where(later_diff, j[None, :], e.shape[0]), axis=1)
    found = first < e.shape[0]
    return jnp.where(found, e[jnp.minimum(first, e.shape[0] - 1)], -1).astype(jnp.int32)


def _bucket_plan(bucket, rank, counts):
    cnt = counts[:N_BUCKETS, 0].astype(jnp.int32)
    tiles = (cnt + TM_E - 1) // TM_E
    tile_end = jnp.cumsum(tiles)
    tile_start = tile_end - tiles
    slot = _slots((tile_start * TM_E).astype(jnp.int32), bucket, rank)
    j = jnp.arange(N_TILES_E, dtype=jnp.int32)
    n_live = tile_end[-1]
    j_live = jnp.minimum(j, n_live - 1)
    b = jnp.sum((tile_end[None, :] <= j_live[:, None]).astype(jnp.int32), axis=1)
    b = jnp.minimum(b, N_BUCKETS - 1)
    rows = jnp.clip(cnt[b] - (j - tile_start[b]) * TM_E, 0, TM_E)
    rows = jnp.where(j < n_live, rows, 0).astype(jnp.int32)
    grp = b // N_PAIRS
    pair = b % N_PAIRS
    ea = grp * EPG + jnp.asarray(_SLOT_A, jnp.int32)[pair]
    eb = grp * EPG + jnp.asarray(_SLOT_B, jnp.int32)[pair]
    a_higher = jnp.asarray(_SLOT_A_IS_HIGHER, jnp.int32)[pair]
    return slot, (ea, eb, _next_different(ea), _next_different(eb), a_higher, rows)


def _moe(a, x_parts, mod, norm_g2, w_out, router_wt, router_bt, w_gate, w_up, w_down, layer, final_g,
         *, final):
    x1, gates, bucket, rank, counts = _post_mixer(a, x_parts, mod, norm_g2, w_out, router_wt, router_bt)
    slot, tile_plan = _bucket_plan(bucket, rank, counts)
    partial = (tile_plan[-1] < TM_E).astype(jnp.int32)
    xs = _dispatch(partial, slot, x1, mod, norm_g2, gates)
    ys = _experts(*tile_plan, xs, w_gate, w_up, w_down, layer)
    return _combine(slot, x1, mod, final_g, ys, final=final)


def kernel(x_prompt, x_sample, state_hgrn, c, c_ctx, w_mod, b_mod, norm_mix_g, norm_ffn_g, norm_final_g,
           hgrn_w_in, hgrn_lb_logits, hgrn_norm_g, hgrn_w_out, conv_w_in, conv_w, conv_w_out,
           router_w, router_b, moe_w_gate, moe_w_up, moe_w_down):
    x_parts = (x_prompt.reshape(T_PROMPT, D), x_sample.reshape(T_SAMPLE, D))
    cond = jnp.concatenate([c_ctx[None], c, jnp.zeros((COND_ROWS - N_COND, D), F32)], axis=0)
    mods = _adaln(cond, w_mod, b_mod)[:, :N_COND].reshape(2, N_COND, 6, 1, D)
    member_major = lambda w: w.reshape(N_GROUPS, EPG, -1).transpose(1, 0, 2).reshape(N_EXPERTS, -1)
    rwt = member_major(router_w.T)
    rbt = member_major(router_b.reshape(N_EXPERTS, 1))
    final_g = norm_final_g.reshape(1, D)
    bf = lambda w: w.astype(BF16)

    q, v, ff, fb, gg = _hgrn_proj(x_parts, mods[0], norm_mix_g[0:1], bf(hgrn_w_in[0]))
    s0 = state_hgrn[:, 0]
    o_f, s_f = _gla_fwd(q, v, ff, hgrn_lb_logits, s0)
    a, s_b = _gla_bwd(q, v, fb, hgrn_lb_logits, s0, o_f, gg, hgrn_norm_g[0:1])
    (x,) = _moe(a, x_parts, mods[0], norm_ffn_g[0:1], bf(hgrn_w_out[0]), rwt, rbt,
                moe_w_gate, moe_w_up, moe_w_down, 0, final_g, final=False)

    a = _conv_mixer(x, mods[1], norm_mix_g[1:2], bf(conv_w_in[0]), conv_w[0])
    y_p, y_s = _moe(a, (x,), mods[1], norm_ffn_g[1:2], bf(conv_w_out[0]), rwt, rbt,
                    moe_w_gate, moe_w_up, moe_w_down, 1, final_g, final=True)

    new_state = jnp.stack([s_f, s_b], axis=1)[:, None]
    return (y_p.reshape(N_PROMPT_SEQ, PROMPT_LEN, D), y_s.reshape(N_SAMPLE_SEQ, SAMPLE_LEN, D), new_state)
```

```python
import functools

import jax
import jax.numpy as jnp
from jax import lax
from jax.experimental import pallas as pl
from jax.experimental.pallas import tpu as pltpu

F32 = jnp.float32
BF16 = jnp.bfloat16

D = 1024
N_PROMPT_SEQ = 16
PROMPT_LEN = 256
N_SAMPLE_SEQ = 2
SAMPLE_LEN = 4096
GRID_W = 64
T_PROMPT = N_PROMPT_SEQ * PROMPT_LEN
T_SAMPLE = N_SAMPLE_SEQ * SAMPLE_LEN
T = T_PROMPT + T_SAMPLE
N_COND = 1 + N_SAMPLE_SEQ
COND_ROWS = 8

HEADS = 8
DK = 128
DV = 128
CHUNK = 64
N_EXPERTS = 16
N_GROUPS = 4
EPG = 4
N_PAIRS = 6
N_BUCKETS = N_GROUPS * N_PAIRS
D_FF = 512
EPS = 1e-6

TM = 256
N_TILES = T // TM
PROMPT_TILES = T_PROMPT // TM
TILES_PER_SAMPLE = SAMPLE_LEN // TM
CHUNKS_PER_TILE = TM // CHUNK
STEP_TILES = 2
STEP_ROWS = STEP_TILES * TM
N_STEPS = N_TILES // STEP_TILES
PROMPT_STEPS = PROMPT_TILES // STEP_TILES
MOVE_TILES = 2
MOVE_ROWS = MOVE_TILES * TM
N_MOVE_STEPS = N_TILES // MOVE_TILES
TM_E = 256
N_TILES_E = T // TM_E + N_BUCKETS
S_ROWS = N_TILES_E * TM_E
XW = D + 128
LANES = 128
MOD_TN = 1536

VMEM_LIMIT = 56 * 1024 * 1024

_SLOT_A = (0, 3, 3, 0, 0, 1)
_SLOT_B = (1, 1, 2, 2, 3, 2)
_SLOT_A_IS_HIGHER = tuple(int(a > b) for a, b in zip(_SLOT_A, _SLOT_B))


def _cparams():
    return pltpu.CompilerParams(dimension_semantics=("arbitrary",), vmem_limit_bytes=VMEM_LIMIT)


def _cond_of_tile(t):
    return jnp.where(t < PROMPT_TILES, 0, 1 + (t - PROMPT_TILES) // TILES_PER_SAMPLE)


def _rmsnorm(x, g):
    ms = jnp.mean(x * x, axis=-1, keepdims=True)
    return (x * lax.rsqrt(ms + EPS)) * g


def _modulate(x, g, shift, scale):
    return _rmsnorm(x, g) * (1.0 + scale) + shift


def _silu(x):
    return x * jax.nn.sigmoid(x)


def _dot(a, b):
    return jnp.dot(a, b, preferred_element_type=F32)


def _dot_nt(a, b):
    return lax.dot_general(a, b, (((1,), (1,)), ((), ())), preferred_element_type=F32)


def _dot_tn(a, b):
    return lax.dot_general(a, b, (((0,), (0,)), ((), ())), preferred_element_type=F32)


def _split_bf16(x):
    hi = x.astype(BF16)
    lo = (x - hi.astype(F32)).astype(BF16)
    return hi, lo


def _mod_kernel(cond_ref, w_ref, b_ref, o_ref):
    s = _silu(cond_ref[...])
    o_ref[0] = _dot(s.astype(BF16), w_ref[0].astype(BF16)) + b_ref[0]


def _adaln(cond, w_mod, b_mod):
    depth = w_mod.shape[0]
    return pl.pallas_call(
        _mod_kernel,
        grid=(depth, 6 * D // MOD_TN),
        in_specs=[
            pl.BlockSpec((COND_ROWS, D), lambda i, j: (0, 0)),
            pl.BlockSpec((1, D, MOD_TN), lambda i, j: (i, 0, j)),
            pl.BlockSpec((1, 1, MOD_TN), lambda i, j: (i, 0, j)),
        ],
        out_specs=pl.BlockSpec((1, COND_ROWS, MOD_TN), lambda i, j: (i, 0, j)),
        out_shape=jax.ShapeDtypeStruct((depth, COND_ROWS, 6 * D), F32),
        compiler_params=pltpu.CompilerParams(
            dimension_semantics=("arbitrary", "arbitrary"), vmem_limit_bytes=VMEM_LIMIT),
        name="adaln",
    )(cond, w_mod, b_mod.reshape(depth, 1, 6 * D))


_MOD_SPEC = pl.BlockSpec((1, 6, 1, D), lambda i: (_cond_of_tile(i), 0, 0, 0))
_ROW_SPEC = pl.BlockSpec((1, D), lambda i: (0, 0))
_TILE_SPEC = pl.BlockSpec((TM, D), lambda i: (i, 0))
_STEP_MOD_SPEC = pl.BlockSpec((1, 6, 1, D), lambda i, *_: (_cond_of_tile(i * STEP_TILES), 0, 0, 0))
_STEP_SPEC = pl.BlockSpec((STEP_ROWS, D), lambda i, *_: (i, 0))
_SUB_ROWS = tuple(slice(s * TM, (s + 1) * TM) for s in range(STEP_TILES))


def _token_tile(x_refs, rows=slice(None), prompt_steps=PROMPT_TILES):
    if len(x_refs) == 1:
        return x_refs[0][rows]
    return jnp.where(pl.program_id(0) < prompt_steps, x_refs[0][rows], x_refs[1][rows])


def _token_specs(x_parts, tiles_per_step=1):
    rows = tiles_per_step * TM
    if len(x_parts) == 1:
        return [pl.BlockSpec((rows, D), lambda i: (i, 0))]
    prompt_steps = T_PROMPT // rows
    return [pl.BlockSpec((rows, D), lambda i: (jnp.minimum(i, prompt_steps - 1), 0)),
            pl.BlockSpec((rows, D), lambda i: (jnp.maximum(i - prompt_steps, 0), 0))]


def _hgrn_proj_kernel(xp_ref, xs_ref, mod_ref, g_ref, w_ref, q_ref, v_ref, ff_ref, fb_ref, gg_ref):
    h = [_modulate(_token_tile((xp_ref, xs_ref), rows, PROMPT_STEPS),
                   g_ref[...], mod_ref[0, 0], mod_ref[0, 1]).astype(BF16) for rows in _SUB_ROWS]
    for rows, h_tile in zip(_SUB_ROWS, h):
        q_ref[rows] = (_dot(h_tile, w_ref[:, 0 * D:1 * D]) * (DK ** -0.5)).astype(BF16)
        v_ref[rows] = _dot(h_tile, w_ref[:, 1 * D:2 * D]).astype(BF16)
        ff_ref[rows] = _dot(h_tile, w_ref[:, 2 * D:3 * D])
        fb_ref[rows] = _dot(h_tile, w_ref[:, 3 * D:4 * D])
        gg_ref[rows] = _dot(h_tile, w_ref[:, 4 * D:5 * D]).astype(BF16)


def _hgrn_proj(x_parts, mod, norm_g, w_in):
    bf = jax.ShapeDtypeStruct((T, D), BF16)
    f32 = jax.ShapeDtypeStruct((T, D), F32)
    return pl.pallas_call(
        _hgrn_proj_kernel,
        grid=(N_STEPS,),
        in_specs=_token_specs(x_parts, STEP_TILES) + [_STEP_MOD_SPEC, _ROW_SPEC,
                                                      pl.BlockSpec((D, 5 * D), lambda i: (0, 0))],
        out_specs=[_STEP_SPEC] * 5,
        out_shape=[bf, bf, f32, f32, bf],
        compiler_params=_cparams(),
        name="hgrn_proj",
    )(*x_parts, mod, norm_g, w_in)


STEPS_PER_SAMPLE = SAMPLE_LEN // STEP_ROWS


def _gla_step(block, q_ref, v_ref, f_ref, lbl_ref, s0_ref, st_ref, st_first_ref, ma_ref, o_ref,
              *, reverse, finish=None):
    is_prompt = block < PROMPT_STEPS
    rel = block - PROMPT_STEPS
    edge = (STEPS_PER_SAMPLE - 1) if reverse else 0
    sample_start = jnp.logical_and(rel >= 0, rel % STEPS_PER_SAMPLE == edge)

    @pl.when(is_prompt)
    def _():
        st_ref[...] = jnp.zeros_like(st_ref)

    @pl.when(sample_start)
    def _():
        for h in range(HEADS):
            st_ref[h] = s0_ref[0, 0, h].T

    logits = lbl_ref[...]
    e = jnp.exp(logits - jnp.max(logits, axis=0, keepdims=True))
    lb = e[0:1] / jnp.sum(e, axis=0, keepdims=True)

    ref_idx = (CHUNK - 1 - CHUNK // 2) if reverse else CHUNK // 2
    last_idx = 0 if reverse else CHUNK - 1
    shift = CHUNK.bit_length() - 1

    row = lax.broadcasted_iota(jnp.int32, (TM, TM), 0)
    col = lax.broadcasted_iota(jnp.int32, (TM, TM), 1)
    same = (row >> shift) == (col >> shift)
    ref_row = ((row >> shift) << shift) + ref_idx
    keep = jnp.logical_and(same, (row <= col) if reverse else (row >= col))

    @pl.when(pl.program_id(0) == 0)
    def _():
        at_ref = jnp.logical_and(same, (ref_row <= col) if reverse else (ref_row >= col))
        ma_ref[...] = (keep.astype(F32) - at_ref.astype(F32)).astype(BF16)

    ma = ma_ref[...]

    rr = lax.broadcasted_iota(jnp.int32, (2 * CHUNKS_PER_TILE, TM), 0)
    cc = lax.broadcasted_iota(jnp.int32, (2 * CHUNKS_PER_TILE, TM), 1)
    chunk = rr & (CHUNKS_PER_TILE - 1)
    target = chunk * CHUNK + jnp.where(rr < CHUNKS_PER_TILE, ref_idx, last_idx)
    covered = (cc >= target) if reverse else (cc <= target)
    rm = jnp.logical_and((cc >> shift) == chunk, covered).astype(BF16)

    chunk_rows = [slice(c * CHUNK, (c + 1) * CHUNK) for c in range(CHUNKS_PER_TILE)]
    order = range(CHUNKS_PER_TILE - 1, -1, -1) if reverse else range(CHUNKS_PER_TILE)
    pair_w = 2 * DK
    n_pairs = HEADS // 2
    tile_order = _SUB_ROWS[::-1] if reverse else _SUB_ROWS

    def per_chunk_scale(x, scale):
        return jnp.concatenate([x[rs] * scale[c:c + 1] for c, rs in enumerate(chunk_rows)], axis=0)

    def pair_cols(p):
        return slice(p * pair_w, (p + 1) * pair_w)

    def stage_decay(k, p):
        lbp = lb[:, pair_cols(p)]
        f = lbp + (1.0 - lbp) * jax.nn.sigmoid(f_ref[tile_order[k], pair_cols(p)])
        lf_hi, lf_lo = _split_bf16(jnp.log(f))
        z = _dot(ma, lf_hi) + _dot(ma, lf_lo)
        marks = _dot(rm, lf_hi) + _dot(rm, lf_lo)
        return 1.0 - f, z, marks

    def stage_scores(k, p, kk, z, marks):
        ref, last = marks[:CHUNKS_PER_TILE], marks[CHUNKS_PER_TILE:]
        qa32 = q_ref[tile_order[k], pair_cols(p)].astype(F32) * jnp.exp(z)
        kb32 = kk * jnp.exp(-z)
        qa = qa32.astype(BF16)
        kb = kb32.astype(BF16)
        qe = per_chunk_scale(qa32, jnp.exp(ref)).astype(BF16)
        kd = per_chunk_scale(kb32, jnp.exp(last - ref)).astype(BF16)
        v = v_ref[tile_order[k], pair_cols(p)]
        scores, incr = [], []
        for hh in range(2):
            ls = slice(hh * DK, (hh + 1) * DK)
            scores.append(_dot_nt(qa[:, ls], kb[:, ls]))
            kd_h = kd[:, ls]
            kd_blocks = jnp.concatenate(
                [jnp.concatenate([part for part in (jnp.zeros((c * CHUNK, DK), BF16), kd_h[rs],
                                                    jnp.zeros((TM - (c + 1) * CHUNK, DK), BF16))
                                  if part.shape[0]], axis=0)
                 for c, rs in enumerate(chunk_rows)], axis=1)
            incr.append(_dot_tn(v[:, ls], kd_blocks))
        return scores, incr, qe, jnp.exp(last), v

    def stage_output(k, p, scores, incr, qe, decay, v):
        base = tile_order[k].start
        for hh in range(2):
            h = 2 * p + hh
            ls = slice(hh * DK, (hh + 1) * DK)
            hs = slice(h * DK, (h + 1) * DK)
            o_intra = _dot(jnp.where(keep, scores[hh], 0.0).astype(BF16), v[:, ls])
            st = st_ref[h]
            if k > 0:
                st = jnp.where(is_prompt, 0.0, st)
            for c in order:
                rs = chunk_rows[c]
                o_ref[base + rs.start:base + rs.stop, hs] = (
                    o_intra[rs] + _dot_nt(qe[rs, ls], st.astype(BF16)))
                st = st * decay[c:c + 1, ls] + incr[hh][:, c * DK:(c + 1) * DK]
            st_ref[h] = st
            if k == 0:
                st_first_ref[h] = st
        if finish is not None:
            return (tile_order[k], p)

    def stage_finish(k, p, rows, pair):
        finish(rows, pair)

    units = [(k, p) for k in range(STEP_TILES) for p in range(n_pairs)]
    stages = [stage_decay, stage_scores, stage_output] + ([stage_finish] if finish is not None else [])
    carried = [dict() for _ in stages]
    for step in range(len(units) + len(stages) - 1):
        for depth, stage in enumerate(stages):
            u = step - depth
            if 0 <= u < len(units):
                out = stage(*units[u], *(carried[depth].pop(u) if depth else ()))
                if depth + 1 < len(stages):
                    carried[depth + 1][u] = out


def _store_prompt_states(block, st_ref, st_first_ref, snew_ref, *, reverse, sfwd_ref=None):
    first, second = (1, 0) if reverse else (0, 1)

    @pl.when(block < PROMPT_STEPS)
    def _():
        for h in range(HEADS):
            if sfwd_ref is None:
                snew_ref[first, h] = st_first_ref[h].T
                snew_ref[second, h] = st_ref[h].T
            else:
                snew_ref[first, 1, h] = st_first_ref[h].T
                snew_ref[second, 1, h] = st_ref[h].T
        if sfwd_ref is not None:
            snew_ref[:, 0] = sfwd_ref[...]


def _gla_fwd_kernel(q_ref, v_ref, f_ref, lbl_ref, s0_ref, o_ref, snew_ref, st_ref, st_first_ref, ma_ref):
    block = pl.program_id(0)
    _gla_step(block, q_ref, v_ref, f_ref, lbl_ref, s0_ref, st_ref, st_first_ref, ma_ref, o_ref,
              reverse=False)
    _store_prompt_states(block, st_ref, st_first_ref, snew_ref, reverse=False)


def _gla_bwd_kernel(q_ref, v_ref, f_ref, lbl_ref, s0_ref, of_ref, gg_ref, ng_ref, sfwd_ref,
                    a_ref, snew_ref, st_ref, st_first_ref, ma_ref, ob_ref):
    block = N_STEPS - 1 - pl.program_id(0)

    def finish(rows, p):
        for h in (2 * p, 2 * p + 1):
            hs = slice(h * DV, (h + 1) * DV)
            o = of_ref[rows, hs] + ob_ref[rows, hs]
            a_ref[rows, hs] = (_rmsnorm(o, ng_ref[...]) * _silu(gg_ref[rows, hs].astype(F32))).astype(BF16)

    _gla_step(block, q_ref, v_ref, f_ref, lbl_ref, s0_ref, st_ref, st_first_ref, ma_ref, ob_ref,
              reverse=True, finish=finish)
    _store_prompt_states(block, st_ref, st_first_ref, snew_ref, reverse=True, sfwd_ref=sfwd_ref)


def _state_specs(direction, block_of_step):
    sample = lambda i: jnp.clip((block_of_step(i) - PROMPT_STEPS) // STEPS_PER_SAMPLE, 0, N_SAMPLE_SEQ - 1)
    s0 = pl.BlockSpec((1, 1, HEADS, DK, DV), lambda i: (sample(i), direction, 0, 0, 0))
    snew = pl.BlockSpec((STEP_TILES, HEADS, DK, DV),
                        lambda i: (jnp.minimum(block_of_step(i), PROMPT_STEPS - 1), 0, 0, 0))
    return s0, snew


_GLA_STATE_SCRATCH = [pltpu.VMEM((HEADS, DV, DK), F32), pltpu.VMEM((HEADS, DV, DK), F32),
                      pltpu.VMEM((TM, TM), BF16)]


def _gla_fwd(q, v, ff, lb_logits, s0):
    s0_spec, snew_spec = _state_specs(0, lambda i: i)
    return pl.pallas_call(
        _gla_fwd_kernel,
        grid=(N_STEPS,),
        in_specs=[_STEP_SPEC, _STEP_SPEC, _STEP_SPEC,
                  pl.BlockSpec(lb_logits.shape, lambda i: (0, 0)), s0_spec],
        out_specs=[_STEP_SPEC, snew_spec],
        out_shape=[jax.ShapeDtypeStruct((T, D), F32),
                   jax.ShapeDtypeStruct((N_PROMPT_SEQ, HEADS, DK, DV), F32)],
        scratch_shapes=_GLA_STATE_SCRATCH,
        compiler_params=_cparams(),
        name="gla_fwd",
    )(q, v, ff, lb_logits, s0)


def _gla_bwd(q, v, fb, lb_logits, s0, o_f, gg, head_norm_g, s_fwd):
    rev = lambda i: N_STEPS - 1 - i
    block = pl.BlockSpec((STEP_ROWS, D), lambda i: (rev(i), 0))
    s0_spec, sfwd_spec = _state_specs(1, rev)
    prompt_block = lambda i: jnp.minimum(rev(i), PROMPT_STEPS - 1)
    return pl.pallas_call(
        _gla_bwd_kernel,
        grid=(N_STEPS,),
        in_specs=[block, block, block, pl.BlockSpec(lb_logits.shape, lambda i: (0, 0)), s0_spec,
                  block, block, pl.BlockSpec((1, DV), lambda i: (0, 0)), sfwd_spec],
        out_specs=[block, pl.BlockSpec((STEP_TILES, 2, HEADS, DK, DV), lambda i: (prompt_block(i), 0, 0, 0, 0))],
        out_shape=[jax.ShapeDtypeStruct((T, D), BF16),
                   jax.ShapeDtypeStruct((N_PROMPT_SEQ, 2, HEADS, DK, DV), F32)],
        scratch_shapes=_GLA_STATE_SCRATCH + [pltpu.VMEM((STEP_ROWS, D), F32)],
        compiler_params=_cparams(),
        name="gla_bwd",
    )(q, v, fb, lb_logits, s0, o_f, gg, head_norm_g, s_fwd)


def _conv_mixer_kernel(x_ref, mod_ref, g_ref, w_ref, cw_ref, a_ref):
    seg = jnp.where(pl.program_id(0) < PROMPT_STEPS, PROMPT_LEN, GRID_W)
    pos = lax.broadcasted_iota(jnp.int32, (TM, 1), 0) & (seg - 1)

    def project(rows):
        h = _modulate(x_ref[rows], g_ref[...], mod_ref[0, 0], mod_ref[0, 1]).astype(BF16)
        return [_dot(h, w_ref[:, k * D:(k + 1) * D]) for k in range(3)]

    def convolve(rows, bg, cg, x_in):
        u = cg * x_in
        prev = jnp.where(pos == 0, 0.0, pltpu.roll(u, 1, axis=0))
        nxt = jnp.where(pos == seg - 1, 0.0, pltpu.roll(u, TM - 1, axis=0))
        conv = cw_ref[0:1] * prev + cw_ref[1:2] * u + cw_ref[2:3] * nxt
        a_ref[rows] = (bg * conv).astype(BF16)

    projected = [project(rows) for rows in _SUB_ROWS]
    for rows, parts in zip(_SUB_ROWS, projected):
        convolve(rows, *parts)


def _conv_mixer(x, mod, norm_g, w_in, conv_w):
    return pl.pallas_call(
        _conv_mixer_kernel,
        grid=(N_STEPS,),
        in_specs=[_STEP_SPEC, _STEP_MOD_SPEC, _ROW_SPEC, pl.BlockSpec((D, 3 * D), lambda i: (0, 0)),
                  pl.BlockSpec((3, D), lambda i: (0, 0))],
        out_specs=_STEP_SPEC,
        out_shape=jax.ShapeDtypeStruct((T, D), BF16),
        compiler_params=_cparams(),
        name="conv_mixer",
    )(x, mod, norm_g, w_in, conv_w)


ROUTE_ROWS = 32


def _first_member(vals, target):
    idx = jnp.full_like(target, float(EPG - 1))
    for j in range(EPG - 2, -1, -1):
        idx = jnp.where(vals[j] == target, float(j), idx)
    return idx


def _router_logits(h2, rwt_ref):
    h_hi, h_lo = _split_bf16(h2)
    w_hi, w_lo = _split_bf16(rwt_ref[...])
    return _dot_nt(w_hi, h_hi) + (_dot_nt(w_hi, h_lo) + _dot_nt(w_lo, h_hi))


def _select_experts(logits, rb_ref):
    scores = jax.nn.sigmoid(logits)
    sel = scores + rb_ref[...]
    member = [sel[j * N_GROUPS:(j + 1) * N_GROUPS] for j in range(EPG)]
    m1 = functools.reduce(jnp.maximum, member)
    i1 = _first_member(member, m1)
    rest = [jnp.where(i1 == float(j), -jnp.inf, member[j]) for j in range(EPG)]
    m2 = functools.reduce(jnp.maximum, rest)
    i2 = _first_member(rest, m2)
    group_score = m1 + m2

    best, grp, first, second = group_score[0:1], jnp.zeros((1, TM), F32), i1[0:1], i2[0:1]
    for g in range(1, N_GROUPS):
        upd = group_score[g:g + 1] > best
        best = jnp.where(upd, group_score[g:g + 1], best)
        grp = jnp.where(upd, float(g), grp)
        first = jnp.where(upd, i1[g:g + 1], first)
        second = jnp.where(upd, i2[g:g + 1], second)
    a = jnp.minimum(first, second)
    b = jnp.maximum(first, second)
    row = lax.broadcasted_iota(jnp.int32, (N_EXPERTS, TM), 0).astype(F32)
    s_lo = jnp.sum(jnp.where(row == a * N_GROUPS + grp, scores, 0.0), axis=0, keepdims=True)
    s_hi = jnp.sum(jnp.where(row == b * N_GROUPS + grp, scores, 0.0), axis=0, keepdims=True)
    tot = s_lo + s_hi
    pair = jnp.where(a == 0.0, jnp.where(b == 1.0, 0.0, jnp.where(b == 2.0, 3.0, 4.0)),
                     jnp.where(a == 1.0, jnp.where(b == 2.0, 5.0, 1.0), 2.0))
    return grp * N_PAIRS + pair, s_lo / tot, s_hi / tot


def _post_mixer_kernel(*refs, n_x):
    a_ref, x_refs = refs[0], refs[1:1 + n_x]
    (mod_ref, g2_ref, wo_ref, rwt_ref, rb_ref,
     x1_ref, gates_ref, bucket_ref, rank_ref, cnt_ref, carry_ref, earlier_ref) = refs[1 + n_x:]

    @pl.when(pl.program_id(0) == 0)
    def _():
        carry_ref[...] = jnp.zeros_like(carry_ref)
        s_idx = lax.broadcasted_iota(jnp.int32, (TM, TM), 0)
        t_idx = lax.broadcasted_iota(jnp.int32, (TM, TM), 1)
        earlier_ref[...] = (s_idx < t_idx).astype(BF16)

    def stage_mix(s):
        rows = slice(s * TM, (s + 1) * TM)
        x1 = (_token_tile(x_refs, rows, PROMPT_STEPS)
              + mod_ref[0, 2] * _dot(a_ref[rows], wo_ref[...]))
        x1_ref[rows] = x1
        return _modulate(x1, g2_ref[...], mod_ref[0, 3], mod_ref[0, 4])

    def stage_rank(s, bucket, g_lo, g_hi):
        onehot = lax.broadcasted_iota(jnp.int32, (ROUTE_ROWS, TM), 0).astype(F32) == bucket
        before = _dot(onehot.astype(BF16), earlier_ref[...])
        oh = onehot.astype(F32)
        carry = carry_ref[...]
        rank = jnp.sum(oh * (before + carry[:, 0:1]), axis=0, keepdims=True)
        carry_ref[...] = carry + jnp.sum(oh, axis=1, keepdims=True)
        bucket_ref[s] = bucket.astype(jnp.int32)
        rank_ref[s] = rank.astype(jnp.int32)
        grow = lax.broadcasted_iota(jnp.int32, (LANES, TM), 0)
        gates_ref[s * TM:(s + 1) * TM] = jnp.where(grow == 0, g_lo, jnp.where(grow == 1, g_hi, 0.0)).T

    tiles = range(STEP_TILES)
    h2 = [stage_mix(s) for s in tiles]
    logits = [_router_logits(h2[s], rwt_ref) for s in tiles]
    picks = [_select_experts(logits[s], rb_ref) for s in tiles]
    for s in tiles:
        stage_rank(s, *picks[s])
    cnt_ref[...] = carry_ref[...]


def _post_mixer(a, x_parts, mod, norm_g2, w_out, router_wt, router_bt):
    rows = STEP_ROWS
    idx_spec = pl.BlockSpec((STEP_TILES, 1, TM), lambda i: (i, 0, 0))
    return pl.pallas_call(
        functools.partial(_post_mixer_kernel, n_x=len(x_parts)),
        grid=(N_STEPS,),
        in_specs=[pl.BlockSpec((rows, D), lambda i: (i, 0))] + _token_specs(x_parts, STEP_TILES) + [
            pl.BlockSpec((1, 6, 1, D), lambda i: (_cond_of_tile(i * STEP_TILES), 0, 0, 0)), _ROW_SPEC,
            pl.BlockSpec((D, D), lambda i: (0, 0)),
            pl.BlockSpec((N_EXPERTS, D), lambda i: (0, 0)),
            pl.BlockSpec((N_EXPERTS, 1), lambda i: (0, 0))],
        out_specs=[pl.BlockSpec((rows, D), lambda i: (i, 0)), pl.BlockSpec((rows, LANES), lambda i: (i, 0)),
                   idx_spec, idx_spec, pl.BlockSpec((ROUTE_ROWS, LANES), lambda i: (0, 0))],
        out_shape=[jax.ShapeDtypeStruct((T, D), F32), jax.ShapeDtypeStruct((T, LANES), F32),
                   jax.ShapeDtypeStruct((N_TILES, 1, TM), jnp.int32),
                   jax.ShapeDtypeStruct((N_TILES, 1, TM), jnp.int32),
                   jax.ShapeDtypeStruct((ROUTE_ROWS, LANES), F32)],
        scratch_shapes=[pltpu.VMEM((ROUTE_ROWS, LANES), F32), pltpu.VMEM((TM, TM), BF16)],
        compiler_params=_cparams(),
        name="post_mixer",
    )(a, *x_parts, mod, norm_g2, w_out, router_wt, router_bt)


_MOVE_SPEC = pl.BlockSpec((MOVE_ROWS, D), lambda i, *_: (i, 0))
_MOVE_MOD_SPEC = pl.BlockSpec((1, 6, 1, D), lambda i, *_: (_cond_of_tile(i * MOVE_TILES), 0, 0, 0))


def _smem_step_spec(step_of_step):
    return pl.BlockSpec((MOVE_TILES, 1, TM), lambda i, *_: (step_of_step(i), 0, 0), memory_space=pltpu.SMEM)


def _start_step_rows(make_copy):
    for tile in range(MOVE_TILES):
        for r in range(TM):
            make_copy(tile, r).start(priority=r % 2)


def _dispatch_kernel(partial_ref, slot_ref, x1_ref, mod_ref, g2_ref, gates_ref,
                     xs_ref, buf_ref, zero_ref, sems, zero_sem):
    i = pl.program_id(0)
    cur = i % 2

    @pl.when(i == 0)
    def _():
        zero_ref[...] = jnp.zeros_like(zero_ref)

        def zero_copy(j):
            return pltpu.make_async_copy(zero_ref, xs_ref.at[pl.ds(j * TM_E, TM_E)], zero_sem)

        def start(j, carry):
            @pl.when(partial_ref[j] != 0)
            def _():
                zero_copy(j).start()
            return carry

        def wait(j, carry):
            @pl.when(partial_ref[j] != 0)
            def _():
                zero_copy(j).wait()
            return carry

        lax.fori_loop(0, N_TILES_E, start, 0)
        lax.fori_loop(0, N_TILES_E, wait, 0)

    def wait_step(b):
        pltpu.make_async_copy(buf_ref.at[b], xs_ref.at[pl.ds(0, MOVE_ROWS)], sems.at[b]).wait()

    h2 = _modulate(x1_ref[...], g2_ref[...], mod_ref[0, 3], mod_ref[0, 4])
    gates = gates_ref[...]

    def send(b):
        buf_ref[b, :, :D] = h2
        buf_ref[b, :, D:] = gates
        _start_step_rows(lambda tile, r: pltpu.make_async_copy(
            buf_ref.at[b, pl.ds(tile * TM + r, 1)], xs_ref.at[pl.ds(slot_ref[tile, 0, r], 1)], sems.at[b]))

        @pl.when(i > 0)
        def _():
            wait_step(1 - b)

        @pl.when(i == N_MOVE_STEPS - 1)
        def _():
            wait_step(b)

    for b in range(2):
        pl.when(cur == b)(functools.partial(send, b))


def _dispatch(partial, slot, x1, mod, norm_g2, gates):
    return pl.pallas_call(
        _dispatch_kernel,
        grid_spec=pltpu.PrefetchScalarGridSpec(
            num_scalar_prefetch=1,
            grid=(N_MOVE_STEPS,),
            in_specs=[_smem_step_spec(lambda i: i), _MOVE_SPEC, _MOVE_MOD_SPEC,
                      pl.BlockSpec((1, D), lambda i, *_: (0, 0)),
                      pl.BlockSpec((MOVE_ROWS, LANES), lambda i, *_: (i, 0))],
            out_specs=pl.BlockSpec(memory_space=pl.ANY),
            scratch_shapes=[pltpu.VMEM((2, MOVE_ROWS, XW), F32), pltpu.VMEM((TM_E, XW), F32),
                            pltpu.SemaphoreType.DMA((2,)), pltpu.SemaphoreType.DMA(())],
        ),
        out_shape=jax.ShapeDtypeStruct((S_ROWS, XW), F32),
        compiler_params=_cparams(),
        name="moe_dispatch",
    )(partial, slot, x1, mod, norm_g2, gates)


_W_SHAPES = ((D, D_FF), (D, D_FF), (D_FF, D))


def _expert_kernel(ea_ref, eb_ref, next_a_ref, next_b_ref, a_higher_ref, nrows_ref,
                   x_ref, wg_hbm, wu_hbm, wd_hbm, y_ref, *scratch, layer):
    stage, w16, sems = scratch[0:3], scratch[3:6], scratch[6]
    j = pl.program_id(0)
    n = nrows_ref[j]
    prev = jnp.maximum(j - 1, 0)

    def fetch(slot, expert):
        return [pltpu.make_async_copy(w.at[layer, expert], s.at[slot], sems.at[slot])
                for w, s in zip((wg_hbm, wu_hbm, wd_hbm), stage)]

    for slot, e_ref, next_ref in ((0, ea_ref, next_a_ref), (1, eb_ref, next_b_ref)):
        @pl.when(j == 0)
        def _(slot=slot, e_ref=e_ref):
            for copy in fetch(slot, e_ref[0]):
                copy.start()

        @pl.when(jnp.logical_or(j == 0, e_ref[j] != e_ref[prev]))
        def _(slot=slot, e_ref=e_ref, next_ref=next_ref):
            for copy in fetch(slot, e_ref[j]):
                copy.wait()
            for s, d in zip(stage, w16):
                d[slot] = s[slot].astype(BF16)

            @pl.when(next_ref[j] >= 0)
            def _():
                for copy in fetch(slot, next_ref[j]):
                    copy.start()

    def run(rows):
        valid = lax.broadcasted_iota(jnp.int32, (rows, 1), 0) < n
        xe = jnp.where(valid, x_ref[:rows], 0.0)
        x = xe[:, :D].astype(BF16)
        g_lo, g_hi = xe[:, D:D + 1], xe[:, D + 1:D + 2]
        a_higher = a_higher_ref[j] != 0

        def ffn(slot, gate):
            wg_ref, wu_ref, wd_ref = w16
            act = _silu(_dot(x, wg_ref[slot])) * _dot(x, wu_ref[slot]) * gate
            return _dot(act.astype(BF16), wd_ref[slot])

        y_ref[:rows] = (ffn(0, jnp.where(a_higher, g_hi, g_lo))
                        + ffn(1, jnp.where(a_higher, g_lo, g_hi)))
        if rows < TM_E:
            y_ref[rows:] = jnp.zeros((TM_E - rows, D), F32)

    half = TM_E // 2
    pl.when(n > half)(functools.partial(run, TM_E))
    pl.when(jnp.logical_and(n > 0, n <= half))(functools.partial(run, half))

    @pl.when(n == 0)
    def _():
        y_ref[...] = jnp.zeros_like(y_ref)


def _experts(tile_ea, tile_eb, tile_next_a, tile_next_b, tile_a_higher, tile_rows, xs,
             w_gate, w_up, w_down, layer):
    any_spec = pl.BlockSpec(memory_space=pl.ANY)
    return pl.pallas_call(
        functools.partial(_expert_kernel, layer=layer),
        grid_spec=pltpu.PrefetchScalarGridSpec(
            num_scalar_prefetch=6,
            grid=(N_TILES_E,),
            in_specs=[pl.BlockSpec((TM_E, XW), lambda j, *_: (j, 0)), any_spec, any_spec, any_spec],
            out_specs=pl.BlockSpec((TM_E, D), lambda j, *_: (j, 0)),
            scratch_shapes=[pltpu.VMEM((2,) + s, F32) for s in _W_SHAPES]
                           + [pltpu.VMEM((2,) + s, BF16) for s in _W_SHAPES]
                           + [pltpu.SemaphoreType.DMA((2,))],
        ),
        out_shape=jax.ShapeDtypeStruct((S_ROWS, D), F32),
        compiler_params=_cparams(),
        name="moe_experts",
    )(tile_ea, tile_eb, tile_next_a, tile_next_b, tile_a_higher, tile_rows, xs, w_gate, w_up, w_down)


def _combine_kernel(slot_ref, next_slot_ref, x1_ref, mod_ref, gf_ref, y_ref, *rest, final):
    out_refs, (buf_ref, sems) = rest[:-2], rest[-2:]
    i = pl.program_id(0)
    cur = i % 2

    def gather(slots, b):
        _start_step_rows(lambda tile, r: pltpu.make_async_copy(
            y_ref.at[pl.ds(slots[tile, 0, r], 1)], buf_ref.at[b, pl.ds(tile * TM + r, 1)], sems.at[b]))

    @pl.when(i == 0)
    def _():
        gather(slot_ref, 0)

    def combine(b):
        @pl.when(i + 1 < N_MOVE_STEPS)
        def _():
            gather(next_slot_ref, 1 - b)

        pltpu.make_async_copy(y_ref.at[pl.ds(0, MOVE_ROWS)], buf_ref.at[b], sems.at[b]).wait()
        x2 = x1_ref[...] + mod_ref[0, 5] * buf_ref[b]
        if not final:
            out_refs[0][...] = x2
            return
        y = _rmsnorm(x2, gf_ref[...])
        yp_ref, ys_ref = out_refs

        @pl.when(i < T_PROMPT // MOVE_ROWS)
        def _():
            yp_ref[...] = y

        @pl.when(i >= T_PROMPT // MOVE_ROWS)
        def _():
            ys_ref[...] = y

    for b in range(2):
        pl.when(cur == b)(functools.partial(combine, b))


def _combine(slot, x1, mod, final_g, y_sorted, *, final):
    if final:
        out_specs = _token_specs((None, None), MOVE_TILES)
        out_shape = [jax.ShapeDtypeStruct((T_PROMPT, D), F32), jax.ShapeDtypeStruct((T_SAMPLE, D), F32)]
    else:
        out_specs = [_MOVE_SPEC]
        out_shape = [jax.ShapeDtypeStruct((T, D), F32)]
    return pl.pallas_call(
        functools.partial(_combine_kernel, final=final),
        grid=(N_MOVE_STEPS,),
        in_specs=[_smem_step_spec(lambda i: i), _smem_step_spec(lambda i: jnp.minimum(i + 1, N_MOVE_STEPS - 1)),
                  _MOVE_SPEC, _MOVE_MOD_SPEC, _ROW_SPEC, pl.BlockSpec(memory_space=pl.ANY)],
        out_specs=out_specs,
        scratch_shapes=[pltpu.VMEM((2, MOVE_ROWS, D), F32), pltpu.SemaphoreType.DMA((2,))],
        out_shape=out_shape,
        compiler_params=_cparams(),
        name="moe_combine_final" if final else "moe_combine",
    )(slot, slot, x1, mod, final_g, y_sorted)


def _plan_kernel(cnt_ref, bucket_ref, rank_ref, slot_ref,
                 ea_ref, eb_ref, next_a_ref, next_b_ref, a_higher_ref, rows_ref, partial_ref):
    slot = rank_ref[...]
    bucket = bucket_ref[...]
    tile0 = jnp.int32(0)
    last = [jnp.int32(0)] * 3
    for b in range(N_BUCKETS):
        n = cnt_ref[b]
        tiles = (n + (TM_E - 1)) // TM_E
        pair = b % N_PAIRS
        per_tile = ((b // N_PAIRS) * EPG + _SLOT_A[pair], (b // N_PAIRS) * EPG + _SLOT_B[pair],
                    _SLOT_A_IS_HIGHER[pair])

        def fill(k, carry, n=n, tile0=tile0, per_tile=per_tile):
            j = tile0 + k
            live = jnp.minimum(n - k * TM_E, TM_E)
            rows_ref[j] = live
            partial_ref[j] = (live < TM_E).astype(jnp.int32)
            for ref, value in zip((ea_ref, eb_ref, a_higher_ref), per_tile):
                ref[j] = jnp.int32(value)
            return carry

        lax.fori_loop(0, tiles, fill, 0)
        slot = slot + jnp.where(bucket == b, tile0 * TM_E, 0)
        last = [jnp.where(tiles > 0, value, old) for value, old in zip(per_tile, last)]
        tile0 = tile0 + tiles

    def unused(j, carry):
        rows_ref[j] = jnp.int32(0)
        partial_ref[j] = jnp.int32(1)
        for ref, value in zip((ea_ref, eb_ref, a_higher_ref), last):
            ref[j] = value
        return carry

    lax.fori_loop(tile0, N_TILES_E, unused, 0)

    for e_ref, next_ref in ((ea_ref, next_a_ref), (eb_ref, next_b_ref)):
        next_ref[N_TILES_E - 1] = jnp.int32(-1)

        def backward(k, carry, e_ref=e_ref, next_ref=next_ref):
            j = N_TILES_E - 2 - k
            next_ref[j] = jnp.where(e_ref[j] == e_ref[j + 1], next_ref[j + 1], e_ref[j + 1])
            return carry

        lax.fori_loop(0, N_TILES_E - 1, backward, 0)
    slot_ref[...] = slot


def _plan(counts, bucket, rank):
    whole = pl.BlockSpec(bucket.shape, lambda i, *_: (0, 0, 0))
    smem = pl.BlockSpec(memory_space=pltpu.SMEM)
    per_tile = jax.ShapeDtypeStruct((N_TILES_E,), jnp.int32)
    slot, *tile_plan = pl.pallas_call(
        _plan_kernel,
        grid_spec=pltpu.PrefetchScalarGridSpec(
            num_scalar_prefetch=1, grid=(1,), in_specs=[whole, whole], out_specs=[whole] + [smem] * 7),
        out_shape=[jax.ShapeDtypeStruct(bucket.shape, jnp.int32)] + [per_tile] * 7,
        compiler_params=_cparams(),
        name="moe_plan",
    )(counts[:, 0].astype(jnp.int32), bucket, rank)
    return slot, tile_plan


def _moe(a, x_parts, mod, norm_g2, w_out, router_wt, router_bt, w_gate, w_up, w_down, layer, final_g,
         *, final):
    x1, gates, bucket, rank, counts = _post_mixer(a, x_parts, mod, norm_g2, w_out, router_wt, router_bt)
    slot, (*expert_plan, partial) = _plan(counts, bucket, rank)
    xs = _dispatch(partial, slot, x1, mod, norm_g2, gates)
    ys = _experts(*expert_plan, xs, w_gate, w_up, w_down, layer)
    return _combine(slot, x1, mod, final_g, ys, final=final)


def kernel(x_prompt, x_sample, state_hgrn, c, c_ctx, w_mod, b_mod, norm_mix_g, norm_ffn_g, norm_final_g,
           hgrn_w_in, hgrn_lb_logits, hgrn_norm_g, hgrn_w_out, conv_w_in, conv_w, conv_w_out,
           router_w, router_b, moe_w_gate, moe_w_up, moe_w_down):
    x_parts = (x_prompt.reshape(T_PROMPT, D), x_sample.reshape(T_SAMPLE, D))
    cond = jnp.concatenate([c_ctx[None], c, jnp.zeros((COND_ROWS - N_COND, D), F32)], axis=0)
    mods = _adaln(cond, w_mod, b_mod)[:, :N_COND].reshape(2, N_COND, 6, 1, D)
    member_major = lambda w: w.reshape(N_GROUPS, EPG, -1).transpose(1, 0, 2).reshape(N_EXPERTS, -1)
    rwt = member_major(router_w.T)
    rbt = member_major(router_b.reshape(N_EXPERTS, 1))
    final_g = norm_final_g.reshape(1, D)
    bf = lambda w: w.astype(BF16)

    q, v, ff, fb, gg = _hgrn_proj(x_parts, mods[0], norm_mix_g[0:1], bf(hgrn_w_in[0]))
    s0 = state_hgrn[:, 0]
    o_f, s_f = _gla_fwd(q, v, ff, hgrn_lb_logits, s0)
    a, new_state = _gla_bwd(q, v, fb, hgrn_lb_logits, s0, o_f, gg, hgrn_norm_g[0:1], s_f)
    (x,) = _moe(a, x_parts, mods[0], norm_ffn_g[0:1], bf(hgrn_w_out[0]), rwt, rbt,
                moe_w_gate, moe_w_up, moe_w_down, 0, final_g, final=False)

    a = _conv_mixer(x, mods[1], norm_mix_g[1:2], bf(conv_w_in[0]), conv_w[0])
    y_p, y_s = _moe(a, (x,), mods[1], norm_ffn_g[1:2], bf(conv_w_out[0]), rwt, rbt,
                    moe_w_gate, moe_w_up, moe_w_down, 1, final_g, final=True)

    return (y_p.reshape(N_PROMPT_SEQ, PROMPT_LEN, D), y_s.reshape(N_SAMPLE_SEQ, SAMPLE_LEN, D),
            new_state[:, None])
```

```python
import functools

import jax
import jax.numpy as jnp
from jax import lax
from jax.experimental import pallas as pl
from jax.experimental.pallas import tpu as pltpu

F32 = jnp.float32
BF16 = jnp.bfloat16

D = 1024
N_PROMPT_SEQ = 16
PROMPT_LEN = 256
N_SAMPLE_SEQ = 2
SAMPLE_LEN = 4096
GRID_W = 64
T_PROMPT = N_PROMPT_SEQ * PROMPT_LEN
T_SAMPLE = N_SAMPLE_SEQ * SAMPLE_LEN
T = T_PROMPT + T_SAMPLE
N_COND = 1 + N_SAMPLE_SEQ
COND_ROWS = 8

HEADS = 8
DK = 128
DV = 128
CHUNK = 64
N_EXPERTS = 16
N_GROUPS = 4
EPG = 4
N_PAIRS = 6
N_BUCKETS = N_GROUPS * N_PAIRS
D_FF = 512
EPS = 1e-6

TM = 256
N_TILES = T // TM
PROMPT_TILES = T_PROMPT // TM
TILES_PER_SAMPLE = SAMPLE_LEN // TM
CHUNKS_PER_TILE = TM // CHUNK
STEP_TILES = 2
STEP_ROWS = STEP_TILES * TM
N_STEPS = N_TILES // STEP_TILES
PROMPT_STEPS = PROMPT_TILES // STEP_TILES
MOVE_TILES = 2
MOVE_ROWS = MOVE_TILES * TM
N_MOVE_STEPS = N_TILES // MOVE_TILES
ROUTE_TILES = 4
TM_E = 288
N_TILES_E = T // TM_E + N_BUCKETS
S_ROWS = N_TILES_E * TM_E
XW = D + 128
LANES = 128
MOD_TN = 1536

VMEM_LIMIT = 56 * 1024 * 1024

_SLOT_A = (0, 3, 3, 0, 0, 1)
_SLOT_B = (1, 1, 2, 2, 3, 2)
_SLOT_A_IS_HIGHER = tuple(int(a > b) for a, b in zip(_SLOT_A, _SLOT_B))


def _cparams():
    return pltpu.CompilerParams(dimension_semantics=("arbitrary",), vmem_limit_bytes=VMEM_LIMIT)


def _cond_of_tile(t):
    return jnp.where(t < PROMPT_TILES, 0, 1 + (t - PROMPT_TILES) // TILES_PER_SAMPLE)


def _rmsnorm(x, g):
    ms = jnp.mean(x * x, axis=-1, keepdims=True)
    return (x * lax.rsqrt(ms + EPS)) * g


def _modulate(x, g, shift, scale):
    return _rmsnorm(x, g) * (1.0 + scale) + shift


def _silu(x):
    return x * jax.nn.sigmoid(x)


def _dot(a, b):
    return jnp.dot(a, b, preferred_element_type=F32)


def _dot_nt(a, b):
    return lax.dot_general(a, b, (((1,), (1,)), ((), ())), preferred_element_type=F32)


def _dot_tn(a, b):
    return lax.dot_general(a, b, (((0,), (0,)), ((), ())), preferred_element_type=F32)


def _split_bf16(x):
    hi = x.astype(BF16)
    lo = (x - hi.astype(F32)).astype(BF16)
    return hi, lo


def _mod_kernel(cond_ref, w_ref, b_ref, o_ref):
    s = _silu(cond_ref[...])
    o_ref[0] = _dot(s.astype(BF16), w_ref[0].astype(BF16)) + b_ref[0]


def _adaln(cond, w_mod, b_mod):
    depth = w_mod.shape[0]
    return pl.pallas_call(
        _mod_kernel,
        grid=(depth, 6 * D // MOD_TN),
        in_specs=[
            pl.BlockSpec((COND_ROWS, D), lambda i, j: (0, 0)),
            pl.BlockSpec((1, D, MOD_TN), lambda i, j: (i, 0, j)),
            pl.BlockSpec((1, 1, MOD_TN), lambda i, j: (i, 0, j)),
        ],
        out_specs=pl.BlockSpec((1, COND_ROWS, MOD_TN), lambda i, j: (i, 0, j)),
        out_shape=jax.ShapeDtypeStruct((depth, COND_ROWS, 6 * D), F32),
        compiler_params=pltpu.CompilerParams(
            dimension_semantics=("arbitrary", "arbitrary"), vmem_limit_bytes=VMEM_LIMIT),
        name="adaln",
    )(cond, w_mod, b_mod.reshape(depth, 1, 6 * D))


_MOD_SPEC = pl.BlockSpec((1, 6, 1, D), lambda i: (_cond_of_tile(i), 0, 0, 0))
_ROW_SPEC = pl.BlockSpec((1, D), lambda i: (0, 0))
_TILE_SPEC = pl.BlockSpec((TM, D), lambda i: (i, 0))
_STEP_MOD_SPEC = pl.BlockSpec((1, 6, 1, D), lambda i, *_: (_cond_of_tile(i * STEP_TILES), 0, 0, 0))
_STEP_SPEC = pl.BlockSpec((STEP_ROWS, D), lambda i, *_: (i, 0))
_SUB_ROWS = tuple(slice(s * TM, (s + 1) * TM) for s in range(STEP_TILES))


def _token_tile(x_refs, rows=slice(None), prompt_steps=PROMPT_TILES):
    if len(x_refs) == 1:
        return x_refs[0][rows]
    return jnp.where(pl.program_id(0) < prompt_steps, x_refs[0][rows], x_refs[1][rows])


def _token_specs(x_parts, tiles_per_step=1):
    rows = tiles_per_step * TM
    if len(x_parts) == 1:
        return [pl.BlockSpec((rows, D), lambda i: (i, 0))]
    prompt_steps = T_PROMPT // rows
    return [pl.BlockSpec((rows, D), lambda i: (jnp.minimum(i, prompt_steps - 1), 0)),
            pl.BlockSpec((rows, D), lambda i: (jnp.maximum(i - prompt_steps, 0), 0))]


def _hgrn_proj_kernel(xp_ref, xs_ref, mod_ref, g_ref, w_ref, q_ref, v_ref, ff_ref, fb_ref, gg_ref):
    h = [_modulate(_token_tile((xp_ref, xs_ref), rows, PROMPT_STEPS),
                   g_ref[...], mod_ref[0, 0], mod_ref[0, 1]).astype(BF16) for rows in _SUB_ROWS]
    for rows, h_tile in zip(_SUB_ROWS, h):
        q_ref[rows] = (_dot(h_tile, w_ref[:, 0 * D:1 * D]) * (DK ** -0.5)).astype(BF16)
        v_ref[rows] = _dot(h_tile, w_ref[:, 1 * D:2 * D]).astype(BF16)
        ff_ref[rows] = _dot(h_tile, w_ref[:, 2 * D:3 * D])
        fb_ref[rows] = _dot(h_tile, w_ref[:, 3 * D:4 * D])
        gg_ref[rows] = _dot(h_tile, w_ref[:, 4 * D:5 * D]).astype(BF16)


def _hgrn_proj(x_parts, mod, norm_g, w_in):
    bf = jax.ShapeDtypeStruct((T, D), BF16)
    f32 = jax.ShapeDtypeStruct((T, D), F32)
    return pl.pallas_call(
        _hgrn_proj_kernel,
        grid=(N_STEPS,),
        in_specs=_token_specs(x_parts, STEP_TILES) + [_STEP_MOD_SPEC, _ROW_SPEC,
                                                      pl.BlockSpec((D, 5 * D), lambda i: (0, 0))],
        out_specs=[_STEP_SPEC] * 5,
        out_shape=[bf, bf, f32, f32, bf],
        compiler_params=_cparams(),
        name="hgrn_proj",
    )(*x_parts, mod, norm_g, w_in)


STEPS_PER_SAMPLE = SAMPLE_LEN // STEP_ROWS


def _gla_step(block, q_ref, v_ref, f_ref, lbl_ref, s0_ref, st_ref, st_first_ref, ma_ref, o_ref,
              *, reverse, finish=None):
    is_prompt = block < PROMPT_STEPS
    rel = block - PROMPT_STEPS
    edge = (STEPS_PER_SAMPLE - 1) if reverse else 0
    sample_start = jnp.logical_and(rel >= 0, rel % STEPS_PER_SAMPLE == edge)

    @pl.when(is_prompt)
    def _():
        st_ref[...] = jnp.zeros_like(st_ref)

    @pl.when(sample_start)
    def _():
        for h in range(HEADS):
            st_ref[h] = s0_ref[0, 0, h].T

    logits = lbl_ref[...]
    e = jnp.exp(logits - jnp.max(logits, axis=0, keepdims=True))
    lb = e[0:1] / jnp.sum(e, axis=0, keepdims=True)

    ref_idx = (CHUNK - 1 - CHUNK // 2) if reverse else CHUNK // 2
    last_idx = 0 if reverse else CHUNK - 1
    shift = CHUNK.bit_length() - 1

    row = lax.broadcasted_iota(jnp.int32, (TM, TM), 0)
    col = lax.broadcasted_iota(jnp.int32, (TM, TM), 1)
    same = (row >> shift) == (col >> shift)
    ref_row = ((row >> shift) << shift) + ref_idx
    keep = jnp.logical_and(same, (row <= col) if reverse else (row >= col))

    @pl.when(pl.program_id(0) == 0)
    def _():
        at_ref = jnp.logical_and(same, (ref_row <= col) if reverse else (ref_row >= col))
        ma_ref[...] = (keep.astype(F32) - at_ref.astype(F32)).astype(BF16)

    ma = ma_ref[...]

    rr = lax.broadcasted_iota(jnp.int32, (2 * CHUNKS_PER_TILE, TM), 0)
    cc = lax.broadcasted_iota(jnp.int32, (2 * CHUNKS_PER_TILE, TM), 1)
    chunk = rr & (CHUNKS_PER_TILE - 1)
    target = chunk * CHUNK + jnp.where(rr < CHUNKS_PER_TILE, ref_idx, last_idx)
    covered = (cc >= target) if reverse else (cc <= target)
    rm = jnp.logical_and((cc >> shift) == chunk, covered).astype(BF16)

    chunk_rows = [slice(c * CHUNK, (c + 1) * CHUNK) for c in range(CHUNKS_PER_TILE)]
    order = range(CHUNKS_PER_TILE - 1, -1, -1) if reverse else range(CHUNKS_PER_TILE)
    pair_w = 2 * DK
    n_pairs = HEADS // 2
    tile_order = _SUB_ROWS[::-1] if reverse else _SUB_ROWS

    def per_chunk_scale(x, scale):
        return jnp.concatenate([x[rs] * scale[c:c + 1] for c, rs in enumerate(chunk_rows)], axis=0)

    def pair_cols(p):
        return slice(p * pair_w, (p + 1) * pair_w)

    def stage_decay(k, p):
        lbp = lb[:, pair_cols(p)]
        f = lbp + (1.0 - lbp) * jax.nn.sigmoid(f_ref[tile_order[k], pair_cols(p)])
        lf_hi, lf_lo = _split_bf16(jnp.log(f))
        z = _dot(ma, lf_hi) + _dot(ma, lf_lo)
        marks = _dot(rm, lf_hi) + _dot(rm, lf_lo)
        return 1.0 - f, z, marks

    def stage_scores(k, p, kk, z, marks):
        ref, last = marks[:CHUNKS_PER_TILE], marks[CHUNKS_PER_TILE:]
        qa32 = q_ref[tile_order[k], pair_cols(p)].astype(F32) * jnp.exp(z)
        kb32 = kk * jnp.exp(-z)
        qa = qa32.astype(BF16)
        kb = kb32.astype(BF16)
        qe = per_chunk_scale(qa32, jnp.exp(ref)).astype(BF16)
        kd = per_chunk_scale(kb32, jnp.exp(last - ref)).astype(BF16)
        v = v_ref[tile_order[k], pair_cols(p)]
        scores, incr = [], []
        for hh in range(2):
            ls = slice(hh * DK, (hh + 1) * DK)
            scores.append(_dot_nt(qa[:, ls], kb[:, ls]))
            kd_h = kd[:, ls]
            kd_blocks = jnp.concatenate(
                [jnp.concatenate([part for part in (jnp.zeros((c * CHUNK, DK), BF16), kd_h[rs],
                                                    jnp.zeros((TM - (c + 1) * CHUNK, DK), BF16))
                                  if part.shape[0]], axis=0)
                 for c, rs in enumerate(chunk_rows)], axis=1)
            incr.append(_dot_tn(v[:, ls], kd_blocks))
        return scores, incr, qe, jnp.exp(last), v

    def stage_output(k, p, scores, incr, qe, decay, v):
        base = tile_order[k].start
        for hh in range(2):
            h = 2 * p + hh
            ls = slice(hh * DK, (hh + 1) * DK)
            hs = slice(h * DK, (h + 1) * DK)
            o_intra = _dot(jnp.where(keep, scores[hh], 0.0).astype(BF16), v[:, ls])
            st = st_ref[h]
            if k > 0:
                st = jnp.where(is_prompt, 0.0, st)
            for c in order:
                rs = chunk_rows[c]
                o_ref[base + rs.start:base + rs.stop, hs] = (
                    o_intra[rs] + _dot_nt(qe[rs, ls], st.astype(BF16)))
                st = st * decay[c:c + 1, ls] + incr[hh][:, c * DK:(c + 1) * DK]
            st_ref[h] = st
            if k == 0:
                st_first_ref[h] = st
        if finish is not None:
            return (tile_order[k], p)

    def stage_finish(k, p, rows, pair):
        finish(rows, pair)

    units = [(k, p) for k in range(STEP_TILES) for p in range(n_pairs)]
    stages = [stage_decay, stage_scores, stage_output] + ([stage_finish] if finish is not None else [])
    carried = [dict() for _ in stages]
    for step in range(len(units) + len(stages) - 1):
        for depth, stage in enumerate(stages):
            u = step - depth
            if 0 <= u < len(units):
                out = stage(*units[u], *(carried[depth].pop(u) if depth else ()))
                if depth + 1 < len(stages):
                    carried[depth + 1][u] = out


def _store_prompt_states(block, st_ref, st_first_ref, snew_ref, *, reverse, sfwd_ref=None):
    first, second = (1, 0) if reverse else (0, 1)

    @pl.when(block < PROMPT_STEPS)
    def _():
        for h in range(HEADS):
            if sfwd_ref is None:
                snew_ref[first, h] = st_first_ref[h].T
                snew_ref[second, h] = st_ref[h].T
            else:
                snew_ref[first, 1, h] = st_first_ref[h].T
                snew_ref[second, 1, h] = st_ref[h].T
        if sfwd_ref is not None:
            snew_ref[:, 0] = sfwd_ref[...]


def _gla_fwd_kernel(q_ref, v_ref, f_ref, lbl_ref, s0_ref, o_ref, snew_ref, st_ref, st_first_ref, ma_ref):
    block = pl.program_id(0)
    _gla_step(block, q_ref, v_ref, f_ref, lbl_ref, s0_ref, st_ref, st_first_ref, ma_ref, o_ref,
              reverse=False)
    _store_prompt_states(block, st_ref, st_first_ref, snew_ref, reverse=False)


def _gla_bwd_kernel(q_ref, v_ref, f_ref, lbl_ref, s0_ref, of_ref, gg_ref, ng_ref, sfwd_ref,
                    a_ref, snew_ref, st_ref, st_first_ref, ma_ref, ob_ref):
    block = N_STEPS - 1 - pl.program_id(0)

    def finish(rows, p):
        for h in (2 * p, 2 * p + 1):
            hs = slice(h * DV, (h + 1) * DV)
            o = of_ref[rows, hs] + ob_ref[rows, hs]
            a_ref[rows, hs] = (_rmsnorm(o, ng_ref[...]) * _silu(gg_ref[rows, hs].astype(F32))).astype(BF16)

    _gla_step(block, q_ref, v_ref, f_ref, lbl_ref, s0_ref, st_ref, st_first_ref, ma_ref, ob_ref,
              reverse=True, finish=finish)
    _store_prompt_states(block, st_ref, st_first_ref, snew_ref, reverse=True, sfwd_ref=sfwd_ref)


def _state_specs(direction, block_of_step):
    sample = lambda i: jnp.clip((block_of_step(i) - PROMPT_STEPS) // STEPS_PER_SAMPLE, 0, N_SAMPLE_SEQ - 1)
    s0 = pl.BlockSpec((1, 1, HEADS, DK, DV), lambda i: (sample(i), direction, 0, 0, 0))
    snew = pl.BlockSpec((STEP_TILES, HEADS, DK, DV),
                        lambda i: (jnp.minimum(block_of_step(i), PROMPT_STEPS - 1), 0, 0, 0))
    return s0, snew


_GLA_STATE_SCRATCH = [pltpu.VMEM((HEADS, DV, DK), F32), pltpu.VMEM((HEADS, DV, DK), F32),
                      pltpu.VMEM((TM, TM), BF16)]


def _gla_fwd(q, v, ff, lb_logits, s0):
    s0_spec, snew_spec = _state_specs(0, lambda i: i)
    return pl.pallas_call(
        _gla_fwd_kernel,
        grid=(N_STEPS,),
        in_specs=[_STEP_SPEC, _STEP_SPEC, _STEP_SPEC,
                  pl.BlockSpec(lb_logits.shape, lambda i: (0, 0)), s0_spec],
        out_specs=[_STEP_SPEC, snew_spec],
        out_shape=[jax.ShapeDtypeStruct((T, D), F32),
                   jax.ShapeDtypeStruct((N_PROMPT_SEQ, HEADS, DK, DV), F32)],
        scratch_shapes=_GLA_STATE_SCRATCH,
        compiler_params=_cparams(),
        name="gla_fwd",
    )(q, v, ff, lb_logits, s0)


def _gla_bwd(q, v, fb, lb_logits, s0, o_f, gg, head_norm_g, s_fwd):
    rev = lambda i: N_STEPS - 1 - i
    block = pl.BlockSpec((STEP_ROWS, D), lambda i: (rev(i), 0))
    s0_spec, sfwd_spec = _state_specs(1, rev)
    prompt_block = lambda i: jnp.minimum(rev(i), PROMPT_STEPS - 1)
    return pl.pallas_call(
        _gla_bwd_kernel,
        grid=(N_STEPS,),
        in_specs=[block, block, block, pl.BlockSpec(lb_logits.shape, lambda i: (0, 0)), s0_spec,
                  block, block, pl.BlockSpec((1, DV), lambda i: (0, 0)), sfwd_spec],
        out_specs=[block, pl.BlockSpec((STEP_TILES, 2, HEADS, DK, DV), lambda i: (prompt_block(i), 0, 0, 0, 0))],
        out_shape=[jax.ShapeDtypeStruct((T, D), BF16),
                   jax.ShapeDtypeStruct((N_PROMPT_SEQ, 2, HEADS, DK, DV), F32)],
        scratch_shapes=_GLA_STATE_SCRATCH + [pltpu.VMEM((STEP_ROWS, D), F32)],
        compiler_params=_cparams(),
        name="gla_bwd",
    )(q, v, fb, lb_logits, s0, o_f, gg, head_norm_g, s_fwd)


def _conv_mixer_kernel(x_ref, mod_ref, g_ref, w_ref, cw_ref, a_ref):
    seg = jnp.where(pl.program_id(0) < PROMPT_STEPS, PROMPT_LEN, GRID_W)
    pos = lax.broadcasted_iota(jnp.int32, (TM, 1), 0) & (seg - 1)

    def project(rows):
        h = _modulate(x_ref[rows], g_ref[...], mod_ref[0, 0], mod_ref[0, 1]).astype(BF16)
        return [_dot(h, w_ref[:, k * D:(k + 1) * D]) for k in range(3)]

    def convolve(rows, bg, cg, x_in):
        u = cg * x_in
        prev = jnp.where(pos == 0, 0.0, pltpu.roll(u, 1, axis=0))
        nxt = jnp.where(pos == seg - 1, 0.0, pltpu.roll(u, TM - 1, axis=0))
        conv = cw_ref[0:1] * prev + cw_ref[1:2] * u + cw_ref[2:3] * nxt
        a_ref[rows] = (bg * conv).astype(BF16)

    projected = [project(rows) for rows in _SUB_ROWS]
    for rows, parts in zip(_SUB_ROWS, projected):
        convolve(rows, *parts)


def _conv_mixer(x, mod, norm_g, w_in, conv_w):
    return pl.pallas_call(
        _conv_mixer_kernel,
        grid=(N_STEPS,),
        in_specs=[_STEP_SPEC, _STEP_MOD_SPEC, _ROW_SPEC, pl.BlockSpec((D, 3 * D), lambda i: (0, 0)),
                  pl.BlockSpec((3, D), lambda i: (0, 0))],
        out_specs=_STEP_SPEC,
        out_shape=jax.ShapeDtypeStruct((T, D), BF16),
        compiler_params=_cparams(),
        name="conv_mixer",
    )(x, mod, norm_g, w_in, conv_w)


ROUTE_ROWS = 32


def _first_member(vals, target):
    idx = jnp.full_like(target, float(EPG - 1))
    for j in range(EPG - 2, -1, -1):
        idx = jnp.where(vals[j] == target, float(j), idx)
    return idx


def _router_logits(h2, rwt_ref):
    h_hi, h_lo = _split_bf16(h2)
    w_hi, w_lo = _split_bf16(rwt_ref[...])
    return _dot_nt(w_hi, h_hi) + (_dot_nt(w_hi, h_lo) + _dot_nt(w_lo, h_hi))


def _select_experts(logits, rb_ref):
    scores = jax.nn.sigmoid(logits)
    sel = scores + rb_ref[...]
    member = [sel[j * N_GROUPS:(j + 1) * N_GROUPS] for j in range(EPG)]
    m1 = functools.reduce(jnp.maximum, member)
    i1 = _first_member(member, m1)
    rest = [jnp.where(i1 == float(j), -jnp.inf, member[j]) for j in range(EPG)]
    m2 = functools.reduce(jnp.maximum, rest)
    i2 = _first_member(rest, m2)
    group_score = m1 + m2

    best, grp, first, second = group_score[0:1], jnp.zeros((1, TM), F32), i1[0:1], i2[0:1]
    for g in range(1, N_GROUPS):
        upd = group_score[g:g + 1] > best
        best = jnp.where(upd, group_score[g:g + 1], best)
        grp = jnp.where(upd, float(g), grp)
        first = jnp.where(upd, i1[g:g + 1], first)
        second = jnp.where(upd, i2[g:g + 1], second)
    a = jnp.minimum(first, second)
    b = jnp.maximum(first, second)
    row = lax.broadcasted_iota(jnp.int32, (N_EXPERTS, TM), 0).astype(F32)
    s_lo = jnp.sum(jnp.where(row == a * N_GROUPS + grp, scores, 0.0), axis=0, keepdims=True)
    s_hi = jnp.sum(jnp.where(row == b * N_GROUPS + grp, scores, 0.0), axis=0, keepdims=True)
    tot = s_lo + s_hi
    pair = jnp.where(a == 0.0, jnp.where(b == 1.0, 0.0, jnp.where(b == 2.0, 3.0, 4.0)),
                     jnp.where(a == 1.0, jnp.where(b == 2.0, 5.0, 1.0), 2.0))
    return grp * N_PAIRS + pair, s_lo / tot, s_hi / tot


def _post_mixer_kernel(*refs, n_x):
    a_ref, x_refs = refs[0], refs[1:1 + n_x]
    (mod_ref, g2_ref, wo_ref, rwt_ref, rb_ref,
     x1_ref, gates_ref, bucket_ref, rank_ref, cnt_ref, carry_ref, earlier_ref) = refs[1 + n_x:]

    @pl.when(pl.program_id(0) == 0)
    def _():
        carry_ref[...] = jnp.zeros_like(carry_ref)
        s_idx = lax.broadcasted_iota(jnp.int32, (TM, TM), 0)
        t_idx = lax.broadcasted_iota(jnp.int32, (TM, TM), 1)
        earlier_ref[...] = (s_idx < t_idx).astype(BF16)

    def stage_mix(s):
        rows = slice(s * TM, (s + 1) * TM)
        x1 = (_token_tile(x_refs, rows, PROMPT_TILES // ROUTE_TILES)
              + mod_ref[0, 2] * _dot(a_ref[rows], wo_ref[...]))
        x1_ref[rows] = x1
        return _modulate(x1, g2_ref[...], mod_ref[0, 3], mod_ref[0, 4])

    def stage_rank(s, bucket, g_lo, g_hi):
        onehot = lax.broadcasted_iota(jnp.int32, (ROUTE_ROWS, TM), 0).astype(F32) == bucket
        before = _dot(onehot.astype(BF16), earlier_ref[...])
        oh = onehot.astype(F32)
        carry = carry_ref[...]
        rank = jnp.sum(oh * (before + carry[:, 0:1]), axis=0, keepdims=True)
        carry_ref[...] = carry + jnp.sum(oh, axis=1, keepdims=True)
        bucket_ref[s] = bucket.astype(jnp.int32)
        rank_ref[s] = rank.astype(jnp.int32)
        grow = lax.broadcasted_iota(jnp.int32, (LANES, TM), 0)
        gates_ref[s * TM:(s + 1) * TM] = jnp.where(grow == 0, g_lo, jnp.where(grow == 1, g_hi, 0.0)).T

    tiles = range(ROUTE_TILES)
    h2 = [stage_mix(s) for s in tiles]
    logits = [_router_logits(h2[s], rwt_ref) for s in tiles]
    picks = [_select_experts(logits[s], rb_ref) for s in tiles]
    for s in tiles:
        stage_rank(s, *picks[s])
    cnt_ref[...] = carry_ref[...]


def _post_mixer(a, x_parts, mod, norm_g2, w_out, router_wt, router_bt):
    rows = ROUTE_TILES * TM
    idx_spec = pl.BlockSpec((ROUTE_TILES, 1, TM), lambda i: (i, 0, 0))
    return pl.pallas_call(
        functools.partial(_post_mixer_kernel, n_x=len(x_parts)),
        grid=(N_TILES // ROUTE_TILES,),
        in_specs=[pl.BlockSpec((rows, D), lambda i: (i, 0))] + _token_specs(x_parts, ROUTE_TILES) + [
            pl.BlockSpec((1, 6, 1, D), lambda i: (_cond_of_tile(i * ROUTE_TILES), 0, 0, 0)), _ROW_SPEC,
            pl.BlockSpec((D, D), lambda i: (0, 0)),
            pl.BlockSpec((N_EXPERTS, D), lambda i: (0, 0)),
            pl.BlockSpec((N_EXPERTS, 1), lambda i: (0, 0))],
        out_specs=[pl.BlockSpec((rows, D), lambda i: (i, 0)), pl.BlockSpec((rows, LANES), lambda i: (i, 0)),
                   idx_spec, idx_spec, pl.BlockSpec((ROUTE_ROWS, LANES), lambda i: (0, 0))],
        out_shape=[jax.ShapeDtypeStruct((T, D), F32), jax.ShapeDtypeStruct((T, LANES), F32),
                   jax.ShapeDtypeStruct((N_TILES, 1, TM), jnp.int32),
                   jax.ShapeDtypeStruct((N_TILES, 1, TM), jnp.int32),
                   jax.ShapeDtypeStruct((ROUTE_ROWS, LANES), F32)],
        scratch_shapes=[pltpu.VMEM((ROUTE_ROWS, LANES), F32), pltpu.VMEM((TM, TM), BF16)],
        compiler_params=_cparams(),
        name="post_mixer",
    )(a, *x_parts, mod, norm_g2, w_out, router_wt, router_bt)


_MOVE_SPEC = pl.BlockSpec((MOVE_ROWS, D), lambda i, *_: (i, 0))
_MOVE_MOD_SPEC = pl.BlockSpec((1, 6, 1, D), lambda i, *_: (_cond_of_tile(i * MOVE_TILES), 0, 0, 0))


def _smem_step_spec(step_of_step):
    return pl.BlockSpec((MOVE_TILES, 1, TM), lambda i, *_: (step_of_step(i), 0, 0), memory_space=pltpu.SMEM)


def _start_step_rows(make_copy):
    for tile in range(MOVE_TILES):
        for r in range(TM):
            make_copy(tile, r).start(priority=r % 2)


def _dispatch_kernel(partial_ref, slot_ref, x1_ref, mod_ref, g2_ref, gates_ref,
                     xs_ref, buf_ref, zero_ref, sems, zero_sem):
    i = pl.program_id(0)
    cur = i % 2

    @pl.when(i == 0)
    def _():
        zero_ref[...] = jnp.zeros_like(zero_ref)

        def zero_copy(j):
            return pltpu.make_async_copy(zero_ref, xs_ref.at[pl.ds(j * TM_E, TM_E)], zero_sem)

        def start(j, carry):
            @pl.when(partial_ref[j] != 0)
            def _():
                zero_copy(j).start()
            return carry

        def wait(j, carry):
            @pl.when(partial_ref[j] != 0)
            def _():
                zero_copy(j).wait()
            return carry

        lax.fori_loop(0, N_TILES_E, start, 0)
        lax.fori_loop(0, N_TILES_E, wait, 0)

    def wait_step(b):
        pltpu.make_async_copy(buf_ref.at[b], xs_ref.at[pl.ds(0, MOVE_ROWS)], sems.at[b]).wait()

    h2 = _modulate(x1_ref[...], g2_ref[...], mod_ref[0, 3], mod_ref[0, 4])
    gates = gates_ref[...]

    def send(b):
        buf_ref[b, :, :D] = h2
        buf_ref[b, :, D:] = gates
        _start_step_rows(lambda tile, r: pltpu.make_async_copy(
            buf_ref.at[b, pl.ds(tile * TM + r, 1)], xs_ref.at[pl.ds(slot_ref[tile, 0, r], 1)], sems.at[b]))

        @pl.when(i > 0)
        def _():
            wait_step(1 - b)

        @pl.when(i == N_MOVE_STEPS - 1)
        def _():
            wait_step(b)

    for b in range(2):
        pl.when(cur == b)(functools.partial(send, b))


def _dispatch(partial, slot, x1, mod, norm_g2, gates):
    return pl.pallas_call(
        _dispatch_kernel,
        grid_spec=pltpu.PrefetchScalarGridSpec(
            num_scalar_prefetch=1,
            grid=(N_MOVE_STEPS,),
            in_specs=[_smem_step_spec(lambda i: i), _MOVE_SPEC, _MOVE_MOD_SPEC,
                      pl.BlockSpec((1, D), lambda i, *_: (0, 0)),
                      pl.BlockSpec((MOVE_ROWS, LANES), lambda i, *_: (i, 0))],
            out_specs=pl.BlockSpec(memory_space=pl.ANY),
            scratch_shapes=[pltpu.VMEM((2, MOVE_ROWS, XW), F32), pltpu.VMEM((TM_E, XW), F32),
                            pltpu.SemaphoreType.DMA((2,)), pltpu.SemaphoreType.DMA(())],
        ),
        out_shape=jax.ShapeDtypeStruct((S_ROWS, XW), F32),
        compiler_params=_cparams(),
        name="moe_dispatch",
    )(partial, slot, x1, mod, norm_g2, gates)


_W_SHAPES = ((D, D_FF), (D, D_FF), (D_FF, D))


def _expert_kernel(ea_ref, eb_ref, next_a_ref, next_b_ref, a_higher_ref, nrows_ref,
                   x_ref, wg_hbm, wu_hbm, wd_hbm, y_ref, *scratch, layer):
    stage, w16, sems = scratch[0:3], scratch[3:6], scratch[6]
    j = pl.program_id(0)
    n = nrows_ref[j]
    prev = jnp.maximum(j - 1, 0)

    def fetch(slot, expert):
        return [pltpu.make_async_copy(w.at[layer, expert], s.at[slot], sems.at[slot])
                for w, s in zip((wg_hbm, wu_hbm, wd_hbm), stage)]

    for slot, e_ref, next_ref in ((0, ea_ref, next_a_ref), (1, eb_ref, next_b_ref)):
        @pl.when(j == 0)
        def _(slot=slot, e_ref=e_ref):
            for copy in fetch(slot, e_ref[0]):
                copy.start()

        @pl.when(jnp.logical_or(j == 0, e_ref[j] != e_ref[prev]))
        def _(slot=slot, e_ref=e_ref, next_ref=next_ref):
            for copy in fetch(slot, e_ref[j]):
                copy.wait()
            for s, d in zip(stage, w16):
                d[slot] = s[slot].astype(BF16)

            @pl.when(next_ref[j] >= 0)
            def _():
                for copy in fetch(slot, next_ref[j]):
                    copy.start()

    def run(rows):
        valid = lax.broadcasted_iota(jnp.int32, (rows, 1), 0) < n
        xe = jnp.where(valid, x_ref[:rows], 0.0)
        x = xe[:, :D].astype(BF16)
        g_lo, g_hi = xe[:, D:D + 1], xe[:, D + 1:D + 2]
        a_higher = a_higher_ref[j] != 0

        def ffn(slot, gate):
            wg_ref, wu_ref, wd_ref = w16
            act = _silu(_dot(x, wg_ref[slot])) * _dot(x, wu_ref[slot]) * gate
            return _dot(act.astype(BF16), wd_ref[slot])

        y_ref[:rows] = (ffn(0, jnp.where(a_higher, g_hi, g_lo))
                        + ffn(1, jnp.where(a_higher, g_lo, g_hi)))
        if rows < TM_E:
            y_ref[rows:] = jnp.zeros((TM_E - rows, D), F32)

    half = TM_E // 2
    pl.when(n > half)(functools.partial(run, TM_E))
    pl.when(jnp.logical_and(n > 0, n <= half))(functools.partial(run, half))

    @pl.when(n == 0)
    def _():
        y_ref[...] = jnp.zeros_like(y_ref)


def _experts(tile_ea, tile_eb, tile_next_a, tile_next_b, tile_a_higher, tile_rows, xs,
             w_gate, w_up, w_down, layer):
    any_spec = pl.BlockSpec(memory_space=pl.ANY)
    return pl.pallas_call(
        functools.partial(_expert_kernel, layer=layer),
        grid_spec=pltpu.PrefetchScalarGridSpec(
            num_scalar_prefetch=6,
            grid=(N_TILES_E,),
            in_specs=[pl.BlockSpec((TM_E, XW), lambda j, *_: (j, 0)), any_spec, any_spec, any_spec],
            out_specs=pl.BlockSpec((TM_E, D), lambda j, *_: (j, 0)),
            scratch_shapes=[pltpu.VMEM((2,) + s, F32) for s in _W_SHAPES]
                           + [pltpu.VMEM((2,) + s, BF16) for s in _W_SHAPES]
                           + [pltpu.SemaphoreType.DMA((2,))],
        ),
        out_shape=jax.ShapeDtypeStruct((S_ROWS, D), F32),
        compiler_params=_cparams(),
        name="moe_experts",
    )(tile_ea, tile_eb, tile_next_a, tile_next_b, tile_a_higher, tile_rows, xs, w_gate, w_up, w_down)


def _combine_kernel(slot_ref, next_slot_ref, x1_ref, mod_ref, gf_ref, y_ref, *rest, final):
    out_refs, (buf_ref, sems) = rest[:-2], rest[-2:]
    i = pl.program_id(0)
    cur = i % 2

    def gather(slots, b):
        _start_step_rows(lambda tile, r: pltpu.make_async_copy(
            y_ref.at[pl.ds(slots[tile, 0, r], 1)], buf_ref.at[b, pl.ds(tile * TM + r, 1)], sems.at[b]))

    @pl.when(i == 0)
    def _():
        gather(slot_ref, 0)

    def combine(b):
        @pl.when(i + 1 < N_MOVE_STEPS)
        def _():
            gather(next_slot_ref, 1 - b)

        pltpu.make_async_copy(y_ref.at[pl.ds(0, MOVE_ROWS)], buf_ref.at[b], sems.at[b]).wait()
        x2 = x1_ref[...] + mod_ref[0, 5] * buf_ref[b]
        if not final:
            out_refs[0][...] = x2
            return
        y = _rmsnorm(x2, gf_ref[...])
        yp_ref, ys_ref = out_refs

        @pl.when(i < T_PROMPT // MOVE_ROWS)
        def _():
            yp_ref[...] = y

        @pl.when(i >= T_PROMPT // MOVE_ROWS)
        def _():
            ys_ref[...] = y

    for b in range(2):
        pl.when(cur == b)(functools.partial(combine, b))


def _combine(slot, x1, mod, final_g, y_sorted, *, final):
    if final:
        out_specs = _token_specs((None, None), MOVE_TILES)
        out_shape = [jax.ShapeDtypeStruct((T_PROMPT, D), F32), jax.ShapeDtypeStruct((T_SAMPLE, D), F32)]
    else:
        out_specs = [_MOVE_SPEC]
        out_shape = [jax.ShapeDtypeStruct((T, D), F32)]
    return pl.pallas_call(
        functools.partial(_combine_kernel, final=final),
        grid=(N_MOVE_STEPS,),
        in_specs=[_smem_step_spec(lambda i: i), _smem_step_spec(lambda i: jnp.minimum(i + 1, N_MOVE_STEPS - 1)),
                  _MOVE_SPEC, _MOVE_MOD_SPEC, _ROW_SPEC, pl.BlockSpec(memory_space=pl.ANY)],
        out_specs=out_specs,
        scratch_shapes=[pltpu.VMEM((2, MOVE_ROWS, D), F32), pltpu.SemaphoreType.DMA((2,))],
        out_shape=out_shape,
        compiler_params=_cparams(),
        name="moe_combine_final" if final else "moe_combine",
    )(slot, slot, x1, mod, final_g, y_sorted)


def _plan_kernel(cnt_ref, bucket_ref, rank_ref, slot_ref,
                 ea_ref, eb_ref, next_a_ref, next_b_ref, a_higher_ref, rows_ref, partial_ref):
    slot = rank_ref[...]
    bucket = bucket_ref[...]
    tile0 = jnp.int32(0)
    last = [jnp.int32(0)] * 3
    for b in range(N_BUCKETS):
        n = cnt_ref[b]
        tiles = (n + (TM_E - 1)) // TM_E
        pair = b % N_PAIRS
        per_tile = ((b // N_PAIRS) * EPG + _SLOT_A[pair], (b // N_PAIRS) * EPG + _SLOT_B[pair],
                    _SLOT_A_IS_HIGHER[pair])

        def fill(k, carry, n=n, tile0=tile0, per_tile=per_tile):
            j = tile0 + k
            live = jnp.minimum(n - k * TM_E, TM_E)
            rows_ref[j] = live
            partial_ref[j] = (live < TM_E).astype(jnp.int32)
            for ref, value in zip((ea_ref, eb_ref, a_higher_ref), per_tile):
                ref[j] = jnp.int32(value)
            return carry

        lax.fori_loop(0, tiles, fill, 0)
        slot = slot + jnp.where(bucket == b, tile0 * TM_E, 0)
        last = [jnp.where(tiles > 0, value, old) for value, old in zip(per_tile, last)]
        tile0 = tile0 + tiles

    def unused(j, carry):
        rows_ref[j] = jnp.int32(0)
        partial_ref[j] = jnp.int32(1)
        for ref, value in zip((ea_ref, eb_ref, a_higher_ref), last):
            ref[j] = value
        return carry

    lax.fori_loop(tile0, N_TILES_E, unused, 0)

    for e_ref, next_ref in ((ea_ref, next_a_ref), (eb_ref, next_b_ref)):
        next_ref[N_TILES_E - 1] = jnp.int32(-1)

        def backward(k, carry, e_ref=e_ref, next_ref=next_ref):
            j = N_TILES_E - 2 - k
            next_ref[j] = jnp.where(e_ref[j] == e_ref[j + 1], next_ref[j + 1], e_ref[j + 1])
            return carry

        lax.fori_loop(0, N_TILES_E - 1, backward, 0)
    slot_ref[...] = slot


def _plan(counts, bucket, rank):
    whole = pl.BlockSpec(bucket.shape, lambda i, *_: (0, 0, 0))
    smem = pl.BlockSpec(memory_space=pltpu.SMEM)
    per_tile = jax.ShapeDtypeStruct((N_TILES_E,), jnp.int32)
    slot, *tile_plan = pl.pallas_call(
        _plan_kernel,
        grid_spec=pltpu.PrefetchScalarGridSpec(
            num_scalar_prefetch=1, grid=(1,), in_specs=[whole, whole], out_specs=[whole] + [smem] * 7),
        out_shape=[jax.ShapeDtypeStruct(bucket.shape, jnp.int32)] + [per_tile] * 7,
        compiler_params=_cparams(),
        name="moe_plan",
    )(counts[:, 0].astype(jnp.int32), bucket, rank)
    return slot, tile_plan


def _moe(a, x_parts, mod, norm_g2, w_out, router_wt, router_bt, w_gate, w_up, w_down, layer, final_g,
         *, final):
    x1, gates, bucket, rank, counts = _post_mixer(a, x_parts, mod, norm_g2, w_out, router_wt, router_bt)
    slot, (*expert_plan, partial) = _plan(counts, bucket, rank)
    xs = _dispatch(partial, slot, x1, mod, norm_g2, gates)
    ys = _experts(*expert_plan, xs, w_gate, w_up, w_down, layer)
    return _combine(slot, x1, mod, final_g, ys, final=final)


def kernel(x_prompt, x_sample, state_hgrn, c, c_ctx, w_mod, b_mod, norm_mix_g, norm_ffn_g, norm_final_g,
           hgrn_w_in, hgrn_lb_logits, hgrn_norm_g, hgrn_w_out, conv_w_in, conv_w, conv_w_out,
           router_w, router_b, moe_w_gate, moe_w_up, moe_w_down):
    x_parts = (x_prompt.reshape(T_PROMPT, D), x_sample.reshape(T_SAMPLE, D))
    cond = jnp.concatenate([c_ctx[None], c, jnp.zeros((COND_ROWS - N_COND, D), F32)], axis=0)
    mods = _adaln(cond, w_mod, b_mod)[:, :N_COND].reshape(2, N_COND, 6, 1, D)
    member_major = lambda w: w.reshape(N_GROUPS, EPG, -1).transpose(1, 0, 2).reshape(N_EXPERTS, -1)
    rwt = member_major(router_w.T)
    rbt = member_major(router_b.reshape(N_EXPERTS, 1))
    final_g = norm_final_g.reshape(1, D)
    bf = lambda w: w.astype(BF16)

    q, v, ff, fb, gg = _hgrn_proj(x_parts, mods[0], norm_mix_g[0:1], bf(hgrn_w_in[0]))
    s0 = state_hgrn[:, 0]
    o_f, s_f = _gla_fwd(q, v, ff, hgrn_lb_logits, s0)
    a, new_state = _gla_bwd(q, v, fb, hgrn_lb_logits, s0, o_f, gg, hgrn_norm_g[0:1], s_f)
    (x,) = _moe(a, x_parts, mods[0], norm_ffn_g[0:1], bf(hgrn_w_out[0]), rwt, rbt,
                moe_w_gate, moe_w_up, moe_w_down, 0, final_g, final=False)

    a = _conv_mixer(x, mods[1], norm_mix_g[1:2], bf(conv_w_in[0]), conv_w[0])
    y_p, y_s = _moe(a, (x,), mods[1], norm_ffn_g[1:2], bf(conv_w_out[0]), rwt, rbt,
                    moe_w_gate, moe_w_up, moe_w_down, 1, final_g, final=True)

    return (y_p.reshape(N_PROMPT_SEQ, PROMPT_LEN, D), y_s.reshape(N_SAMPLE_SEQ, SAMPLE_LEN, D),
            new_state[:, None])
```

```python
import functools

import jax
import jax.numpy as jnp
from jax import lax
from jax.experimental import pallas as pl
from jax.experimental.pallas import tpu as pltpu

F32 = jnp.float32
BF16 = jnp.bfloat16

D = 1024
N_PROMPT_SEQ = 16
PROMPT_LEN = 256
N_SAMPLE_SEQ = 2
SAMPLE_LEN = 4096
GRID_W = 64
T_PROMPT = N_PROMPT_SEQ * PROMPT_LEN
T_SAMPLE = N_SAMPLE_SEQ * SAMPLE_LEN
T = T_PROMPT + T_SAMPLE
N_COND = 1 + N_SAMPLE_SEQ
COND_ROWS = 8

HEADS = 8
DK = 128
DV = 128
CHUNK = 64
N_EXPERTS = 16
N_GROUPS = 4
EPG = 4
N_PAIRS = 6
N_BUCKETS = N_GROUPS * N_PAIRS
D_FF = 512
EPS = 1e-6

TM = 256
N_TILES = T // TM
PROMPT_TILES = T_PROMPT // TM
TILES_PER_SAMPLE = SAMPLE_LEN // TM
CHUNKS_PER_TILE = TM // CHUNK
STEP_TILES = 2
STEP_ROWS = STEP_TILES * TM
N_STEPS = N_TILES // STEP_TILES
PROMPT_STEPS = PROMPT_TILES // STEP_TILES
MOVE_TILES = 2
MOVE_ROWS = MOVE_TILES * TM
N_MOVE_STEPS = N_TILES // MOVE_TILES
ROUTE_TILES = 4
TM_E = 288
N_TILES_E = T // TM_E + N_BUCKETS
S_ROWS = N_TILES_E * TM_E
XW = D + 128
LANES = 128
MOD_TN = 1536

V7X_VMEM_BYTES = 64 * 1024 * 1024
VMEM_LIMIT = V7X_VMEM_BYTES * 7 // 8

_SLOT_A = (0, 3, 3, 0, 0, 1)
_SLOT_B = (1, 1, 2, 2, 3, 2)
_SLOT_A_IS_HIGHER = tuple(int(a > b) for a, b in zip(_SLOT_A, _SLOT_B))


def _cparams():
    return pltpu.CompilerParams(dimension_semantics=("arbitrary",), vmem_limit_bytes=VMEM_LIMIT)


def _cond_of_tile(t):
    return jnp.where(t < PROMPT_TILES, 0, 1 + (t - PROMPT_TILES) // TILES_PER_SAMPLE)


def _rmsnorm(x, g):
    ms = jnp.mean(x * x, axis=-1, keepdims=True)
    return (x * lax.rsqrt(ms + EPS)) * g


def _modulate(x, g, shift, scale):
    return _rmsnorm(x, g) * (1.0 + scale) + shift


def _silu(x):
    return x * jax.nn.sigmoid(x)


def _dot(a, b):
    return jnp.dot(a, b, preferred_element_type=F32)


def _dot_nt(a, b):
    return lax.dot_general(a, b, (((1,), (1,)), ((), ())), preferred_element_type=F32)


def _dot_tn(a, b):
    return lax.dot_general(a, b, (((0,), (0,)), ((), ())), preferred_element_type=F32)


def _split_bf16(x):
    hi = x.astype(BF16)
    lo = (x - hi.astype(F32)).astype(BF16)
    return hi, lo


def _mod_kernel(cond_ref, w_ref, b_ref, o_ref):
    s = _silu(cond_ref[...])
    o_ref[0] = _dot(s.astype(BF16), w_ref[0].astype(BF16)) + b_ref[0]


def _adaln(cond, w_mod, b_mod):
    depth = w_mod.shape[0]
    return pl.pallas_call(
        _mod_kernel,
        grid=(depth, 6 * D // MOD_TN),
        in_specs=[
            pl.BlockSpec((COND_ROWS, D), lambda i, j: (0, 0)),
            pl.BlockSpec((1, D, MOD_TN), lambda i, j: (i, 0, j)),
            pl.BlockSpec((1, 1, MOD_TN), lambda i, j: (i, 0, j)),
        ],
        out_specs=pl.BlockSpec((1, COND_ROWS, MOD_TN), lambda i, j: (i, 0, j)),
        out_shape=jax.ShapeDtypeStruct((depth, COND_ROWS, 6 * D), F32),
        compiler_params=pltpu.CompilerParams(
            dimension_semantics=("arbitrary", "arbitrary"), vmem_limit_bytes=VMEM_LIMIT),
        name="adaln",
    )(cond, w_mod, b_mod.reshape(depth, 1, 6 * D))


_ROW_SPEC = pl.BlockSpec((1, D), lambda i: (0, 0))
_STEP_MOD_SPEC = pl.BlockSpec((1, 6, 1, D), lambda i, *_: (_cond_of_tile(i * STEP_TILES), 0, 0, 0))
_STEP_SPEC = pl.BlockSpec((STEP_ROWS, D), lambda i, *_: (i, 0))
_SUB_ROWS = tuple(slice(s * TM, (s + 1) * TM) for s in range(STEP_TILES))


def _load_weight_bf16(w_hbm, w16_ref, stage_ref, sem):
    n = w16_ref.shape[1] // D

    def slab(c):
        return pltpu.make_async_copy(w_hbm.at[:, pl.ds(c * D, D)], stage_ref.at[c % 2], sem.at[c % 2])

    slab(0).start()
    for c in range(n):
        if c + 1 < n:
            slab(c + 1).start()
        slab(c).wait()
        w16_ref[:, c * D:(c + 1) * D] = stage_ref[c % 2].astype(BF16)


def _weight_scratch(n_slabs):
    n_stage = min(n_slabs, 2)
    return [pltpu.VMEM((D, n_slabs * D), BF16), pltpu.VMEM((n_stage, D, D), F32),
            pltpu.SemaphoreType.DMA((n_stage,))]


_ANY_SPEC = pl.BlockSpec(memory_space=pl.ANY)


def _token_tile(x_refs, rows=slice(None), prompt_steps=PROMPT_TILES):
    if len(x_refs) == 1:
        return x_refs[0][rows]
    return jnp.where(pl.program_id(0) < prompt_steps, x_refs[0][rows], x_refs[1][rows])


def _token_specs(x_parts, tiles_per_step=1):
    rows = tiles_per_step * TM
    if len(x_parts) == 1:
        return [pl.BlockSpec((rows, D), lambda i: (i, 0))]
    prompt_steps = T_PROMPT // rows
    return [pl.BlockSpec((rows, D), lambda i: (jnp.minimum(i, prompt_steps - 1), 0)),
            pl.BlockSpec((rows, D), lambda i: (jnp.maximum(i - prompt_steps, 0), 0))]


def _hgrn_proj_kernel(xp_ref, xs_ref, mod_ref, g_ref, w_hbm, q_ref, v_ref, ff_ref, fb_ref, gg_ref,
                      w_ref, stage_ref, sem):
    @pl.when(pl.program_id(0) == 0)
    def _():
        _load_weight_bf16(w_hbm, w_ref, stage_ref, sem)

    h = [_modulate(_token_tile((xp_ref, xs_ref), rows, PROMPT_STEPS),
                   g_ref[...], mod_ref[0, 0], mod_ref[0, 1]).astype(BF16) for rows in _SUB_ROWS]
    for rows, h_tile in zip(_SUB_ROWS, h):
        q_ref[rows] = (_dot(h_tile, w_ref[:, 0 * D:1 * D]) * (DK ** -0.5)).astype(BF16)
        v_ref[rows] = _dot(h_tile, w_ref[:, 1 * D:2 * D]).astype(BF16)
        ff_ref[rows] = _dot(h_tile, w_ref[:, 2 * D:3 * D])
        fb_ref[rows] = _dot(h_tile, w_ref[:, 3 * D:4 * D])
        gg_ref[rows] = _dot(h_tile, w_ref[:, 4 * D:5 * D]).astype(BF16)


def _hgrn_proj(x_parts, mod, norm_g, w_in):
    bf = jax.ShapeDtypeStruct((T, D), BF16)
    f32 = jax.ShapeDtypeStruct((T, D), F32)
    return pl.pallas_call(
        _hgrn_proj_kernel,
        grid=(N_STEPS,),
        in_specs=_token_specs(x_parts, STEP_TILES) + [_STEP_MOD_SPEC, _ROW_SPEC, _ANY_SPEC],
        out_specs=[_STEP_SPEC] * 5,
        out_shape=[bf, bf, f32, f32, bf],
        scratch_shapes=_weight_scratch(5),
        compiler_params=_cparams(),
        name="hgrn_proj",
    )(*x_parts, mod, norm_g, w_in)


STEPS_PER_SAMPLE = SAMPLE_LEN // STEP_ROWS


def _gla_step(block, q_ref, v_ref, f_ref, lbl_ref, s0_ref, st_ref, st_first_ref, ma_ref, o_ref,
              *, reverse, finish=None):
    is_prompt = block < PROMPT_STEPS
    rel = block - PROMPT_STEPS
    edge = (STEPS_PER_SAMPLE - 1) if reverse else 0
    sample_start = jnp.logical_and(rel >= 0, rel % STEPS_PER_SAMPLE == edge)

    @pl.when(is_prompt)
    def _():
        st_ref[...] = jnp.zeros_like(st_ref)

    @pl.when(sample_start)
    def _():
        for h in range(HEADS):
            st_ref[h] = s0_ref[0, 0, h].T

    logits = lbl_ref[...]
    e = jnp.exp(logits - jnp.max(logits, axis=0, keepdims=True))
    lb = e[0:1] / jnp.sum(e, axis=0, keepdims=True)

    ref_idx = (CHUNK - 1 - CHUNK // 2) if reverse else CHUNK // 2
    last_idx = 0 if reverse else CHUNK - 1
    shift = CHUNK.bit_length() - 1

    row = lax.broadcasted_iota(jnp.int32, (TM, TM), 0)
    col = lax.broadcasted_iota(jnp.int32, (TM, TM), 1)
    same = (row >> shift) == (col >> shift)
    ref_row = ((row >> shift) << shift) + ref_idx
    keep = jnp.logical_and(same, (row <= col) if reverse else (row >= col))

    @pl.when(pl.program_id(0) == 0)
    def _():
        at_ref = jnp.logical_and(same, (ref_row <= col) if reverse else (ref_row >= col))
        ma_ref[...] = (keep.astype(F32) - at_ref.astype(F32)).astype(BF16)

    ma = ma_ref[...]

    rr = lax.broadcasted_iota(jnp.int32, (2 * CHUNKS_PER_TILE, TM), 0)
    cc = lax.broadcasted_iota(jnp.int32, (2 * CHUNKS_PER_TILE, TM), 1)
    chunk = rr & (CHUNKS_PER_TILE - 1)
    target = chunk * CHUNK + jnp.where(rr < CHUNKS_PER_TILE, ref_idx, last_idx)
    covered = (cc >= target) if reverse else (cc <= target)
    rm = jnp.logical_and((cc >> shift) == chunk, covered).astype(BF16)

    chunk_rows = [slice(c * CHUNK, (c + 1) * CHUNK) for c in range(CHUNKS_PER_TILE)]
    order = range(CHUNKS_PER_TILE - 1, -1, -1) if reverse else range(CHUNKS_PER_TILE)
    pair_w = 2 * DK
    n_pairs = HEADS // 2
    tile_order = _SUB_ROWS[::-1] if reverse else _SUB_ROWS

    def per_chunk_scale(x, scale):
        return jnp.concatenate([x[rs] * scale[c:c + 1] for c, rs in enumerate(chunk_rows)], axis=0)

    def pair_cols(p):
        return slice(p * pair_w, (p + 1) * pair_w)

    def stage_decay(k, p):
        lbp = lb[:, pair_cols(p)]
        f = lbp + (1.0 - lbp) * jax.nn.sigmoid(f_ref[tile_order[k], pair_cols(p)])
        lf_hi, lf_lo = _split_bf16(jnp.log(f))
        z = _dot(ma, lf_hi) + _dot(ma, lf_lo)
        marks = _dot(rm, lf_hi) + _dot(rm, lf_lo)
        return 1.0 - f, z, marks

    def stage_scores(k, p, kk, z, marks):
        ref, last = marks[:CHUNKS_PER_TILE], marks[CHUNKS_PER_TILE:]
        qa32 = q_ref[tile_order[k], pair_cols(p)].astype(F32) * jnp.exp(z)
        kb32 = kk * jnp.exp(-z)
        qa = qa32.astype(BF16)
        kb = kb32.astype(BF16)
        qe = per_chunk_scale(qa32, jnp.exp(ref)).astype(BF16)
        kd = per_chunk_scale(kb32, jnp.exp(last - ref)).astype(BF16)
        v = v_ref[tile_order[k], pair_cols(p)]
        scores, incr = [], []
        for hh in range(2):
            ls = slice(hh * DK, (hh + 1) * DK)
            scores.append(_dot_nt(qa[:, ls], kb[:, ls]))
            kd_h = kd[:, ls]
            kd_blocks = jnp.concatenate(
                [jnp.concatenate([part for part in (jnp.zeros((c * CHUNK, DK), BF16), kd_h[rs],
                                                    jnp.zeros((TM - (c + 1) * CHUNK, DK), BF16))
                                  if part.shape[0]], axis=0)
                 for c, rs in enumerate(chunk_rows)], axis=1)
            incr.append(_dot_tn(v[:, ls], kd_blocks))
        return scores, incr, qe, jnp.exp(last), v

    def stage_output(k, p, scores, incr, qe, decay, v):
        base = tile_order[k].start
        for hh in range(2):
            h = 2 * p + hh
            ls = slice(hh * DK, (hh + 1) * DK)
            hs = slice(h * DK, (h + 1) * DK)
            o_intra = _dot(jnp.where(keep, scores[hh], 0.0).astype(BF16), v[:, ls])
            st = st_ref[h]
            if k > 0:
                st = jnp.where(is_prompt, 0.0, st)
            for c in order:
                rs = chunk_rows[c]
                o_ref[base + rs.start:base + rs.stop, hs] = (
                    o_intra[rs] + _dot_nt(qe[rs, ls], st.astype(BF16)))
                st = st * decay[c:c + 1, ls] + incr[hh][:, c * DK:(c + 1) * DK]
            st_ref[h] = st
            if k == 0:
                st_first_ref[h] = st
        if finish is not None:
            return (tile_order[k], p)

    def stage_finish(k, p, rows, pair):
        finish(rows, pair)

    units = [(k, p) for k in range(STEP_TILES) for p in range(n_pairs)]
    stages = [stage_decay, stage_scores, stage_output] + ([stage_finish] if finish is not None else [])
    carried = [dict() for _ in stages]
    for step in range(len(units) + len(stages) - 1):
        for depth, stage in enumerate(stages):
            u = step - depth
            if 0 <= u < len(units):
                out = stage(*units[u], *(carried[depth].pop(u) if depth else ()))
                if depth + 1 < len(stages):
                    carried[depth + 1][u] = out


def _store_prompt_states(block, st_ref, st_first_ref, snew_ref, *, reverse, sfwd_ref=None):
    first, second = (1, 0) if reverse else (0, 1)

    @pl.when(block < PROMPT_STEPS)
    def _():
        for h in range(HEADS):
            if sfwd_ref is None:
                snew_ref[first, h] = st_first_ref[h].T
                snew_ref[second, h] = st_ref[h].T
            else:
                snew_ref[first, 1, h] = st_first_ref[h].T
                snew_ref[second, 1, h] = st_ref[h].T
        if sfwd_ref is not None:
            snew_ref[:, 0] = sfwd_ref[...]


def _gla_fwd_kernel(q_ref, v_ref, f_ref, lbl_ref, s0_ref, o_ref, snew_ref, st_ref, st_first_ref, ma_ref):
    block = pl.program_id(0)
    _gla_step(block, q_ref, v_ref, f_ref, lbl_ref, s0_ref, st_ref, st_first_ref, ma_ref, o_ref,
              reverse=False)
    _store_prompt_states(block, st_ref, st_first_ref, snew_ref, reverse=False)


def _gla_bwd_kernel(q_ref, v_ref, f_ref, lbl_ref, s0_ref, of_ref, gg_ref, ng_ref, sfwd_ref,
                    a_ref, snew_ref, st_ref, st_first_ref, ma_ref, ob_ref):
    block = N_STEPS - 1 - pl.program_id(0)

    def finish(rows, p):
        for h in (2 * p, 2 * p + 1):
            hs = slice(h * DV, (h + 1) * DV)
            o = of_ref[rows, hs] + ob_ref[rows, hs]
            a_ref[rows, hs] = (_rmsnorm(o, ng_ref[...]) * _silu(gg_ref[rows, hs].astype(F32))).astype(BF16)

    _gla_step(block, q_ref, v_ref, f_ref, lbl_ref, s0_ref, st_ref, st_first_ref, ma_ref, ob_ref,
              reverse=True, finish=finish)
    _store_prompt_states(block, st_ref, st_first_ref, snew_ref, reverse=True, sfwd_ref=sfwd_ref)


def _state_specs(direction, block_of_step):
    sample = lambda i: jnp.clip((block_of_step(i) - PROMPT_STEPS) // STEPS_PER_SAMPLE, 0, N_SAMPLE_SEQ - 1)
    s0 = pl.BlockSpec((1, 1, HEADS, DK, DV), lambda i: (sample(i), direction, 0, 0, 0))
    snew = pl.BlockSpec((STEP_TILES, HEADS, DK, DV),
                        lambda i: (jnp.minimum(block_of_step(i), PROMPT_STEPS - 1), 0, 0, 0))
    return s0, snew


_GLA_STATE_SCRATCH = [pltpu.VMEM((HEADS, DV, DK), F32), pltpu.VMEM((HEADS, DV, DK), F32),
                      pltpu.VMEM((TM, TM), BF16)]


def _gla_fwd(q, v, ff, lb_logits, s0):
    s0_spec, snew_spec = _state_specs(0, lambda i: i)
    return pl.pallas_call(
        _gla_fwd_kernel,
        grid=(N_STEPS,),
        in_specs=[_STEP_SPEC, _STEP_SPEC, _STEP_SPEC,
                  pl.BlockSpec(lb_logits.shape, lambda i: (0, 0)), s0_spec],
        out_specs=[_STEP_SPEC, snew_spec],
        out_shape=[jax.ShapeDtypeStruct((T, D), F32),
                   jax.ShapeDtypeStruct((N_PROMPT_SEQ, HEADS, DK, DV), F32)],
        scratch_shapes=_GLA_STATE_SCRATCH,
        compiler_params=_cparams(),
        name="gla_fwd",
    )(q, v, ff, lb_logits, s0)


def _gla_bwd(q, v, fb, lb_logits, s0, o_f, gg, head_norm_g, s_fwd):
    rev = lambda i: N_STEPS - 1 - i
    block = pl.BlockSpec((STEP_ROWS, D), lambda i: (rev(i), 0))
    s0_spec, sfwd_spec = _state_specs(1, rev)
    prompt_block = lambda i: jnp.minimum(rev(i), PROMPT_STEPS - 1)
    return pl.pallas_call(
        _gla_bwd_kernel,
        grid=(N_STEPS,),
        in_specs=[block, block, block, pl.BlockSpec(lb_logits.shape, lambda i: (0, 0)), s0_spec,
                  block, block, pl.BlockSpec((1, DV), lambda i: (0, 0)), sfwd_spec],
        out_specs=[block, pl.BlockSpec((STEP_TILES, 2, HEADS, DK, DV), lambda i: (prompt_block(i), 0, 0, 0, 0))],
        out_shape=[jax.ShapeDtypeStruct((T, D), BF16),
                   jax.ShapeDtypeStruct((N_PROMPT_SEQ, 2, HEADS, DK, DV), F32)],
        scratch_shapes=_GLA_STATE_SCRATCH + [pltpu.VMEM((STEP_ROWS, D), F32)],
        compiler_params=_cparams(),
        name="gla_bwd",
    )(q, v, fb, lb_logits, s0, o_f, gg, head_norm_g, s_fwd)


def _conv_mixer_kernel(x_ref, mod_ref, g_ref, w_hbm, cw_ref, a_ref, w_ref, stage_ref, sem):
    @pl.when(pl.program_id(0) == 0)
    def _():
        _load_weight_bf16(w_hbm, w_ref, stage_ref, sem)

    seg = jnp.where(pl.program_id(0) < PROMPT_STEPS, PROMPT_LEN, GRID_W)
    pos = lax.broadcasted_iota(jnp.int32, (TM, 1), 0) & (seg - 1)

    def project(rows):
        h = _modulate(x_ref[rows], g_ref[...], mod_ref[0, 0], mod_ref[0, 1]).astype(BF16)
        return [_dot(h, w_ref[:, k * D:(k + 1) * D]) for k in range(3)]

    def convolve(rows, bg, cg, x_in):
        u = cg * x_in
        prev = jnp.where(pos == 0, 0.0, pltpu.roll(u, 1, axis=0))
        nxt = jnp.where(pos == seg - 1, 0.0, pltpu.roll(u, TM - 1, axis=0))
        conv = cw_ref[0:1] * prev + cw_ref[1:2] * u + cw_ref[2:3] * nxt
        a_ref[rows] = (bg * conv).astype(BF16)

    projected = [project(rows) for rows in _SUB_ROWS]
    for rows, parts in zip(_SUB_ROWS, projected):
        convolve(rows, *parts)


def _conv_mixer(x, mod, norm_g, w_in, conv_w):
    return pl.pallas_call(
        _conv_mixer_kernel,
        grid=(N_STEPS,),
        in_specs=[_STEP_SPEC, _STEP_MOD_SPEC, _ROW_SPEC, _ANY_SPEC, pl.BlockSpec((3, D), lambda i: (0, 0))],
        out_specs=_STEP_SPEC,
        out_shape=jax.ShapeDtypeStruct((T, D), BF16),
        scratch_shapes=_weight_scratch(3),
        compiler_params=_cparams(),
        name="conv_mixer",
    )(x, mod, norm_g, w_in, conv_w)


ROUTE_ROWS = 32


def _first_member(vals, target):
    idx = jnp.full_like(target, float(EPG - 1))
    for j in range(EPG - 2, -1, -1):
        idx = jnp.where(vals[j] == target, float(j), idx)
    return idx


def _router_logits(h2, rwt_ref):
    h_hi, h_lo = _split_bf16(h2)
    w_hi, w_lo = _split_bf16(rwt_ref[...])
    return _dot_nt(w_hi, h_hi) + (_dot_nt(w_hi, h_lo) + _dot_nt(w_lo, h_hi))


def _select_experts(logits, rb_ref):
    scores = jax.nn.sigmoid(logits)
    sel = scores + rb_ref[...]
    member = [sel[j * N_GROUPS:(j + 1) * N_GROUPS] for j in range(EPG)]
    m1 = functools.reduce(jnp.maximum, member)
    i1 = _first_member(member, m1)
    rest = [jnp.where(i1 == float(j), -jnp.inf, member[j]) for j in range(EPG)]
    m2 = functools.reduce(jnp.maximum, rest)
    i2 = _first_member(rest, m2)
    group_score = m1 + m2

    best, grp, first, second = group_score[0:1], jnp.zeros((1, TM), F32), i1[0:1], i2[0:1]
    for g in range(1, N_GROUPS):
        upd = group_score[g:g + 1] > best
        best = jnp.where(upd, group_score[g:g + 1], best)
        grp = jnp.where(upd, float(g), grp)
        first = jnp.where(upd, i1[g:g + 1], first)
        second = jnp.where(upd, i2[g:g + 1], second)
    a = jnp.minimum(first, second)
    b = jnp.maximum(first, second)
    row = lax.broadcasted_iota(jnp.int32, (N_EXPERTS, TM), 0).astype(F32)
    s_lo = jnp.sum(jnp.where(row == a * N_GROUPS + grp, scores, 0.0), axis=0, keepdims=True)
    s_hi = jnp.sum(jnp.where(row == b * N_GROUPS + grp, scores, 0.0), axis=0, keepdims=True)
    tot = s_lo + s_hi
    pair = jnp.where(a == 0.0, jnp.where(b == 1.0, 0.0, jnp.where(b == 2.0, 3.0, 4.0)),
                     jnp.where(a == 1.0, jnp.where(b == 2.0, 5.0, 1.0), 2.0))
    return grp * N_PAIRS + pair, s_lo / tot, s_hi / tot


def _post_mixer_kernel(*refs, n_x):
    a_ref, x_refs = refs[0], refs[1:1 + n_x]
    (mod_ref, g2_ref, wo_hbm, rwt_ref, rb_ref,
     x1_ref, gates_ref, bucket_ref, rank_ref, cnt_ref, carry_ref, earlier_ref, wo_ref, stage_ref, sem) = refs[1 + n_x:]

    @pl.when(pl.program_id(0) == 0)
    def _():
        _load_weight_bf16(wo_hbm, wo_ref, stage_ref, sem)
        carry_ref[...] = jnp.zeros_like(carry_ref)
        s_idx = lax.broadcasted_iota(jnp.int32, (TM, TM), 0)
        t_idx = lax.broadcasted_iota(jnp.int32, (TM, TM), 1)
        earlier_ref[...] = (s_idx < t_idx).astype(BF16)

    def stage_mix(s):
        rows = slice(s * TM, (s + 1) * TM)
        x1 = (_token_tile(x_refs, rows, PROMPT_TILES // ROUTE_TILES)
              + mod_ref[0, 2] * _dot(a_ref[rows], wo_ref[...]))
        x1_ref[rows] = x1
        return _modulate(x1, g2_ref[...], mod_ref[0, 3], mod_ref[0, 4])

    def stage_rank(s, bucket, g_lo, g_hi):
        onehot = lax.broadcasted_iota(jnp.int32, (ROUTE_ROWS, TM), 0).astype(F32) == bucket
        before = _dot(onehot.astype(BF16), earlier_ref[...])
        oh = onehot.astype(F32)
        carry = carry_ref[...]
        rank = jnp.sum(oh * (before + carry[:, 0:1]), axis=0, keepdims=True)
        carry_ref[...] = carry + jnp.sum(oh, axis=1, keepdims=True)
        bucket_ref[s] = bucket.astype(jnp.int32)
        rank_ref[s] = rank.astype(jnp.int32)
        grow = lax.broadcasted_iota(jnp.int32, (LANES, TM), 0)
        gates_ref[s * TM:(s + 1) * TM] = jnp.where(grow == 0, g_lo, jnp.where(grow == 1, g_hi, 0.0)).T

    tiles = range(ROUTE_TILES)
    h2 = [stage_mix(s) for s in tiles]
    logits = [_router_logits(h2[s], rwt_ref) for s in tiles]
    picks = [_select_experts(logits[s], rb_ref) for s in tiles]
    for s in tiles:
        stage_rank(s, *picks[s])
    cnt_ref[...] = carry_ref[...]


def _post_mixer(a, x_parts, mod, norm_g2, w_out, router_wt, router_bt):
    rows = ROUTE_TILES * TM
    idx_spec = pl.BlockSpec((ROUTE_TILES, 1, TM), lambda i: (i, 0, 0))
    return pl.pallas_call(
        functools.partial(_post_mixer_kernel, n_x=len(x_parts)),
        grid=(N_TILES // ROUTE_TILES,),
        in_specs=[pl.BlockSpec((rows, D), lambda i: (i, 0))] + _token_specs(x_parts, ROUTE_TILES) + [
            pl.BlockSpec((1, 6, 1, D), lambda i: (_cond_of_tile(i * ROUTE_TILES), 0, 0, 0)), _ROW_SPEC,
            _ANY_SPEC,
            pl.BlockSpec((N_EXPERTS, D), lambda i: (0, 0)),
            pl.BlockSpec((N_EXPERTS, 1), lambda i: (0, 0))],
        out_specs=[pl.BlockSpec((rows, D), lambda i: (i, 0)), pl.BlockSpec((rows, LANES), lambda i: (i, 0)),
                   idx_spec, idx_spec, pl.BlockSpec((ROUTE_ROWS, LANES), lambda i: (0, 0))],
        out_shape=[jax.ShapeDtypeStruct((T, D), F32), jax.ShapeDtypeStruct((T, LANES), F32),
                   jax.ShapeDtypeStruct((N_TILES, 1, TM), jnp.int32),
                   jax.ShapeDtypeStruct((N_TILES, 1, TM), jnp.int32),
                   jax.ShapeDtypeStruct((ROUTE_ROWS, LANES), F32)],
        scratch_shapes=[pltpu.VMEM((ROUTE_ROWS, LANES), F32), pltpu.VMEM((TM, TM), BF16)] + _weight_scratch(1),
        compiler_params=_cparams(),
        name="post_mixer",
    )(a, *x_parts, mod, norm_g2, w_out, router_wt, router_bt)


_MOVE_SPEC = pl.BlockSpec((MOVE_ROWS, D), lambda i, *_: (i, 0))
_MOVE_MOD_SPEC = pl.BlockSpec((1, 6, 1, D), lambda i, *_: (_cond_of_tile(i * MOVE_TILES), 0, 0, 0))


def _smem_step_spec(step_of_step):
    return pl.BlockSpec((MOVE_TILES, 1, TM), lambda i, *_: (step_of_step(i), 0, 0), memory_space=pltpu.SMEM)


def _start_step_rows(make_copy):
    for tile in range(MOVE_TILES):
        for r in range(TM):
            make_copy(tile, r).start(priority=r % 2)


def _dispatch_kernel(partial_ref, slot_ref, x1_ref, mod_ref, g2_ref, gates_ref,
                     xs_ref, buf_ref, zero_ref, sems, zero_sem):
    i = pl.program_id(0)
    cur = i % 2

    @pl.when(i == 0)
    def _():
        zero_ref[...] = jnp.zeros_like(zero_ref)

        def zero_copy(j):
            return pltpu.make_async_copy(zero_ref, xs_ref.at[pl.ds(j * TM_E, TM_E)], zero_sem)

        def start(j, carry):
            @pl.when(partial_ref[j] != 0)
            def _():
                zero_copy(j).start()
            return carry

        def wait(j, carry):
            @pl.when(partial_ref[j] != 0)
            def _():
                zero_copy(j).wait()
            return carry

        lax.fori_loop(0, N_TILES_E, start, 0)
        lax.fori_loop(0, N_TILES_E, wait, 0)

    def wait_step(b):
        pltpu.make_async_copy(buf_ref.at[b], xs_ref.at[pl.ds(0, MOVE_ROWS)], sems.at[b]).wait()

    h2 = _modulate(x1_ref[...], g2_ref[...], mod_ref[0, 3], mod_ref[0, 4])
    gates = gates_ref[...]

    def send(b):
        buf_ref[b, :, :D] = h2
        buf_ref[b, :, D:] = gates
        _start_step_rows(lambda tile, r: pltpu.make_async_copy(
            buf_ref.at[b, pl.ds(tile * TM + r, 1)], xs_ref.at[pl.ds(slot_ref[tile, 0, r], 1)], sems.at[b]))

        @pl.when(i > 0)
        def _():
            wait_step(1 - b)

        @pl.when(i == N_MOVE_STEPS - 1)
        def _():
            wait_step(b)

    for b in range(2):
        pl.when(cur == b)(functools.partial(send, b))


def _dispatch(partial, slot, x1, mod, norm_g2, gates):
    return pl.pallas_call(
        _dispatch_kernel,
        grid_spec=pltpu.PrefetchScalarGridSpec(
            num_scalar_prefetch=1,
            grid=(N_MOVE_STEPS,),
            in_specs=[_smem_step_spec(lambda i: i), _MOVE_SPEC, _MOVE_MOD_SPEC,
                      pl.BlockSpec((1, D), lambda i, *_: (0, 0)),
                      pl.BlockSpec((MOVE_ROWS, LANES), lambda i, *_: (i, 0))],
            out_specs=pl.BlockSpec(memory_space=pl.ANY),
            scratch_shapes=[pltpu.VMEM((2, MOVE_ROWS, XW), F32), pltpu.VMEM((TM_E, XW), F32),
                            pltpu.SemaphoreType.DMA((2,)), pltpu.SemaphoreType.DMA(())],
        ),
        out_shape=jax.ShapeDtypeStruct((S_ROWS, XW), F32),
        compiler_params=_cparams(),
        name="moe_dispatch",
    )(partial, slot, x1, mod, norm_g2, gates)


_W_SHAPES = ((D, D_FF), (D, D_FF), (D_FF, D))


def _expert_kernel(ea_ref, eb_ref, next_a_ref, next_b_ref, a_higher_ref, nrows_ref, xblock_ref,
                   x_ref, wg_hbm, wu_hbm, wd_hbm, y_ref, *scratch, layer):
    stage, w16, sems = scratch[0:3], scratch[3:6], scratch[6]
    j = pl.program_id(0)
    n = nrows_ref[j]
    prev = jnp.maximum(j - 1, 0)

    def fetch(slot, expert):
        return [pltpu.make_async_copy(w.at[layer, expert], s.at[slot], sems.at[slot])
                for w, s in zip((wg_hbm, wu_hbm, wd_hbm), stage)]

    for slot, e_ref, next_ref in ((0, ea_ref, next_a_ref), (1, eb_ref, next_b_ref)):
        @pl.when(j == 0)
        def _(slot=slot, e_ref=e_ref):
            for copy in fetch(slot, e_ref[0]):
                copy.start()

        @pl.when(jnp.logical_or(j == 0, e_ref[j] != e_ref[prev]))
        def _(slot=slot, e_ref=e_ref, next_ref=next_ref):
            for copy in fetch(slot, e_ref[j]):
                copy.wait()
            for s, d in zip(stage, w16):
                d[slot] = s[slot].astype(BF16)

            @pl.when(next_ref[j] >= 0)
            def _():
                for copy in fetch(slot, next_ref[j]):
                    copy.start()

    def run(rows):
        valid = lax.broadcasted_iota(jnp.int32, (rows, 1), 0) < n
        xe = jnp.where(valid, x_ref[:rows], 0.0)
        x = xe[:, :D].astype(BF16)
        g_lo, g_hi = xe[:, D:D + 1], xe[:, D + 1:D + 2]
        a_higher = a_higher_ref[j] != 0

        def ffn(slot, gate):
            wg_ref, wu_ref, wd_ref = w16
            act = _silu(_dot(x, wg_ref[slot])) * _dot(x, wu_ref[slot]) * gate
            return _dot(act.astype(BF16), wd_ref[slot])

        y_ref[:rows] = (ffn(0, jnp.where(a_higher, g_hi, g_lo))
                        + ffn(1, jnp.where(a_higher, g_lo, g_hi)))
        if rows < TM_E:
            y_ref[rows:] = jnp.zeros((TM_E - rows, D), F32)

    half = TM_E // 2
    pl.when(n > half)(functools.partial(run, TM_E))
    pl.when(jnp.logical_and(n > 0, n <= half))(functools.partial(run, half))

    @pl.when(n == 0)
    def _():
        y_ref[...] = jnp.zeros_like(y_ref)


def _experts(tile_ea, tile_eb, tile_next_a, tile_next_b, tile_a_higher, tile_rows, tile_xblock, xs,
             w_gate, w_up, w_down, layer):
    any_spec = pl.BlockSpec(memory_space=pl.ANY)
    return pl.pallas_call(
        functools.partial(_expert_kernel, layer=layer),
        grid_spec=pltpu.PrefetchScalarGridSpec(
            num_scalar_prefetch=7,
            grid=(N_TILES_E,),
            in_specs=[pl.BlockSpec((TM_E, XW), lambda j, *plan: (plan[-1][j], 0)), any_spec, any_spec, any_spec],
            out_specs=pl.BlockSpec((TM_E, D), lambda j, *_: (j, 0)),
            scratch_shapes=[pltpu.VMEM((2,) + s, F32) for s in _W_SHAPES]
                           + [pltpu.VMEM((2,) + s, BF16) for s in _W_SHAPES]
                           + [pltpu.SemaphoreType.DMA((2,))],
        ),
        out_shape=jax.ShapeDtypeStruct((S_ROWS, D), F32),
        compiler_params=_cparams(),
        name="moe_experts",
    )(tile_ea, tile_eb, tile_next_a, tile_next_b, tile_a_higher, tile_rows, tile_xblock, xs,
      w_gate, w_up, w_down)


def _combine_kernel(slot_ref, next_slot_ref, x1_ref, mod_ref, gf_ref, y_ref, *rest, final):
    out_refs, (buf_ref, sems) = rest[:-2], rest[-2:]
    i = pl.program_id(0)
    cur = i % 2

    def gather(slots, b):
        _start_step_rows(lambda tile, r: pltpu.make_async_copy(
            y_ref.at[pl.ds(slots[tile, 0, r], 1)], buf_ref.at[b, pl.ds(tile * TM + r, 1)], sems.at[b]))

    @pl.when(i == 0)
    def _():
        gather(slot_ref, 0)

    def combine(b):
        @pl.when(i + 1 < N_MOVE_STEPS)
        def _():
            gather(next_slot_ref, 1 - b)

        pltpu.make_async_copy(y_ref.at[pl.ds(0, MOVE_ROWS)], buf_ref.at[b], sems.at[b]).wait()
        x2 = x1_ref[...] + mod_ref[0, 5] * buf_ref[b]
        if not final:
            out_refs[0][...] = x2
            return
        y = _rmsnorm(x2, gf_ref[...])
        yp_ref, ys_ref = out_refs

        @pl.when(i < T_PROMPT // MOVE_ROWS)
        def _():
            yp_ref[...] = y

        @pl.when(i >= T_PROMPT // MOVE_ROWS)
        def _():
            ys_ref[...] = y

    for b in range(2):
        pl.when(cur == b)(functools.partial(combine, b))


def _combine(slot, x1, mod, final_g, y_sorted, *, final):
    if final:
        out_specs = _token_specs((None, None), MOVE_TILES)
        out_shape = [jax.ShapeDtypeStruct((T_PROMPT, D), F32), jax.ShapeDtypeStruct((T_SAMPLE, D), F32)]
    else:
        out_specs = [_MOVE_SPEC]
        out_shape = [jax.ShapeDtypeStruct((T, D), F32)]
    return pl.pallas_call(
        functools.partial(_combine_kernel, final=final),
        grid=(N_MOVE_STEPS,),
        in_specs=[_smem_step_spec(lambda i: i), _smem_step_spec(lambda i: jnp.minimum(i + 1, N_MOVE_STEPS - 1)),
                  _MOVE_SPEC, _MOVE_MOD_SPEC, _ROW_SPEC, pl.BlockSpec(memory_space=pl.ANY)],
        out_specs=out_specs,
        scratch_shapes=[pltpu.VMEM((2, MOVE_ROWS, D), F32), pltpu.SemaphoreType.DMA((2,))],
        out_shape=out_shape,
        compiler_params=_cparams(),
        name="moe_combine_final" if final else "moe_combine",
    )(slot, slot, x1, mod, final_g, y_sorted)


def _plan_kernel(cnt_ref, bucket_ref, rank_ref, slot_ref,
                 ea_ref, eb_ref, next_a_ref, next_b_ref, a_higher_ref, rows_ref, xblock_ref, partial_ref):
    slot = rank_ref[...]
    bucket = bucket_ref[...]
    tile0 = jnp.int32(0)
    last = [jnp.int32(0)] * 3
    for b in range(N_BUCKETS):
        n = cnt_ref[b]
        tiles = (n + (TM_E - 1)) // TM_E
        pair = b % N_PAIRS
        per_tile = ((b // N_PAIRS) * EPG + _SLOT_A[pair], (b // N_PAIRS) * EPG + _SLOT_B[pair],
                    _SLOT_A_IS_HIGHER[pair])

        def fill(k, carry, n=n, tile0=tile0, per_tile=per_tile):
            j = tile0 + k
            live = jnp.minimum(n - k * TM_E, TM_E)
            rows_ref[j] = live
            xblock_ref[j] = j
            partial_ref[j] = (live < TM_E).astype(jnp.int32)
            for ref, value in zip((ea_ref, eb_ref, a_higher_ref), per_tile):
                ref[j] = jnp.int32(value)
            return carry

        lax.fori_loop(0, tiles, fill, 0)
        slot = slot + jnp.where(bucket == b, tile0 * TM_E, 0)
        last = [jnp.where(tiles > 0, value, old) for value, old in zip(per_tile, last)]
        tile0 = tile0 + tiles

    def unused(j, carry):
        rows_ref[j] = jnp.int32(0)
        xblock_ref[j] = jnp.maximum(tile0 - 1, 0)
        partial_ref[j] = jnp.int32(1)
        for ref, value in zip((ea_ref, eb_ref, a_higher_ref), last):
            ref[j] = value
        return carry

    lax.fori_loop(tile0, N_TILES_E, unused, 0)

    for e_ref, next_ref in ((ea_ref, next_a_ref), (eb_ref, next_b_ref)):
        next_ref[N_TILES_E - 1] = jnp.int32(-1)

        def backward(k, carry, e_ref=e_ref, next_ref=next_ref):
            j = N_TILES_E - 2 - k
            next_ref[j] = jnp.where(e_ref[j] == e_ref[j + 1], next_ref[j + 1], e_ref[j + 1])
            return carry

        lax.fori_loop(0, N_TILES_E - 1, backward, 0)
    slot_ref[...] = slot


def _plan(counts, bucket, rank):
    whole = pl.BlockSpec(bucket.shape, lambda i, *_: (0, 0, 0))
    smem = pl.BlockSpec(memory_space=pltpu.SMEM)
    per_tile = jax.ShapeDtypeStruct((N_TILES_E,), jnp.int32)
    slot, *tile_plan = pl.pallas_call(
        _plan_kernel,
        grid_spec=pltpu.PrefetchScalarGridSpec(
            num_scalar_prefetch=1, grid=(1,), in_specs=[whole, whole], out_specs=[whole] + [smem] * 8),
        out_shape=[jax.ShapeDtypeStruct(bucket.shape, jnp.int32)] + [per_tile] * 8,
        compiler_params=_cparams(),
        name="moe_plan",
    )(counts[:, 0].astype(jnp.int32), bucket, rank)
    return slot, tile_plan


def _moe(a, x_parts, mod, norm_g2, w_out, router_wt, router_bt, w_gate, w_up, w_down, layer, final_g,
         *, final):
    x1, gates, bucket, rank, counts = _post_mixer(a, x_parts, mod, norm_g2, w_out, router_wt, router_bt)
    slot, (*expert_plan, partial) = _plan(counts, bucket, rank)
    xs = _dispatch(partial, slot, x1, mod, norm_g2, gates)
    ys = _experts(*expert_plan, xs, w_gate, w_up, w_down, layer)
    return _combine(slot, x1, mod, final_g, ys, final=final)


def kernel(x_prompt, x_sample, state_hgrn, c, c_ctx, w_mod, b_mod, norm_mix_g, norm_ffn_g, norm_final_g,
           hgrn_w_in, hgrn_lb_logits, hgrn_norm_g, hgrn_w_out, conv_w_in, conv_w, conv_w_out,
           router_w, router_b, moe_w_gate, moe_w_up, moe_w_down):
    x_parts = (x_prompt.reshape(T_PROMPT, D), x_sample.reshape(T_SAMPLE, D))
    cond = jnp.concatenate([c_ctx[None], c, jnp.zeros((COND_ROWS - N_COND, D), F32)], axis=0)
    mods = _adaln(cond, w_mod, b_mod)[:, :N_COND].reshape(2, N_COND, 6, 1, D)
    member_major = lambda w: w.reshape(N_GROUPS, EPG, -1).transpose(1, 0, 2).reshape(N_EXPERTS, -1)
    rwt = member_major(router_w.T)
    rbt = member_major(router_b.reshape(N_EXPERTS, 1))
    final_g = norm_final_g.reshape(1, D)

    q, v, ff, fb, gg = _hgrn_proj(x_parts, mods[0], norm_mix_g[0:1], hgrn_w_in[0])
    s0 = state_hgrn[:, 0]
    o_f, s_f = _gla_fwd(q, v, ff, hgrn_lb_logits, s0)
    a, new_state = _gla_bwd(q, v, fb, hgrn_lb_logits, s0, o_f, gg, hgrn_norm_g[0:1], s_f)
    (x,) = _moe(a, x_parts, mods[0], norm_ffn_g[0:1], hgrn_w_out[0], rwt, rbt,
                moe_w_gate, moe_w_up, moe_w_down, 0, final_g, final=False)

    a = _conv_mixer(x, mods[1], norm_mix_g[1:2], conv_w_in[0], conv_w[0])
    y_p, y_s = _moe(a, (x,), mods[1], norm_ffn_g[1:2], conv_w_out[0], rwt, rbt,
                    moe_w_gate, moe_w_up, moe_w_down, 1, final_g, final=True)

    return (y_p.reshape(N_PROMPT_SEQ, PROMPT_LEN, D), y_s.reshape(N_SAMPLE_SEQ, SAMPLE_LEN, D),
            new_state[:, None])
```

```python
import functools

import jax
import jax.numpy as jnp
from jax import lax
from jax.experimental import pallas as pl
from jax.experimental.pallas import tpu as pltpu

F32 = jnp.float32
BF16 = jnp.bfloat16

D = 1024
N_PROMPT_SEQ = 16
PROMPT_LEN = 256
N_SAMPLE_SEQ = 2
SAMPLE_LEN = 4096
GRID_W = 64
T_PROMPT = N_PROMPT_SEQ * PROMPT_LEN
T_SAMPLE = N_SAMPLE_SEQ * SAMPLE_LEN
T = T_PROMPT + T_SAMPLE
N_COND = 1 + N_SAMPLE_SEQ
COND_ROWS = 8

HEADS = 8
DK = 128
DV = 128
CHUNK = 64
N_EXPERTS = 16
N_GROUPS = 4
EPG = 4
N_PAIRS = 6
N_BUCKETS = N_GROUPS * N_PAIRS
D_FF = 512
EPS = 1e-6

TM = 256
N_TILES = T // TM
PROMPT_TILES = T_PROMPT // TM
TILES_PER_SAMPLE = SAMPLE_LEN // TM
CHUNKS_PER_TILE = TM // CHUNK
STEP_TILES = 2
STEP_ROWS = STEP_TILES * TM
N_STEPS = N_TILES // STEP_TILES
PROMPT_STEPS = PROMPT_TILES // STEP_TILES
MOVE_TILES = 2
MOVE_ROWS = MOVE_TILES * TM
N_MOVE_STEPS = N_TILES // MOVE_TILES
ROUTE_TILES = 4
TM_E = 288
N_TILES_E = T // TM_E + N_BUCKETS
S_ROWS = N_TILES_E * TM_E
ZERO_ROWS = 32
XW = D + 128
LANES = 128
MOD_TN = 1536

V7X_VMEM_BYTES = 64 * 1024 * 1024
VMEM_LIMIT = V7X_VMEM_BYTES * 7 // 8

_SLOT_A = (0, 3, 3, 0, 0, 1)
_SLOT_B = (1, 1, 2, 2, 3, 2)
_SLOT_A_IS_HIGHER = tuple(int(a > b) for a, b in zip(_SLOT_A, _SLOT_B))


def _cparams():
    return pltpu.CompilerParams(dimension_semantics=("arbitrary",), vmem_limit_bytes=VMEM_LIMIT)


def _cond_of_tile(t):
    return jnp.where(t < PROMPT_TILES, 0, 1 + (t - PROMPT_TILES) // TILES_PER_SAMPLE)


def _rmsnorm(x, g):
    ms = jnp.mean(x * x, axis=-1, keepdims=True)
    return (x * lax.rsqrt(ms + EPS)) * g


def _modulate(x, g, shift, scale):
    return _rmsnorm(x, g) * (1.0 + scale) + shift


def _silu(x):
    return x * jax.nn.sigmoid(x)


def _dot(a, b):
    return jnp.dot(a, b, preferred_element_type=F32)


def _dot_nt(a, b):
    return lax.dot_general(a, b, (((1,), (1,)), ((), ())), preferred_element_type=F32)


def _dot_tn(a, b):
    return lax.dot_general(a, b, (((0,), (0,)), ((), ())), preferred_element_type=F32)


def _split_bf16(x):
    hi = x.astype(BF16)
    lo = (x - hi.astype(F32)).astype(BF16)
    return hi, lo


def _mod_kernel(cond_ref, w_ref, b_ref, o_ref):
    s = _silu(cond_ref[...])
    o_ref[0] = _dot(s.astype(BF16), w_ref[0].astype(BF16)) + b_ref[0]


def _adaln(cond, w_mod, b_mod):
    depth = w_mod.shape[0]
    return pl.pallas_call(
        _mod_kernel,
        grid=(depth, 6 * D // MOD_TN),
        in_specs=[
            pl.BlockSpec((COND_ROWS, D), lambda i, j: (0, 0)),
            pl.BlockSpec((1, D, MOD_TN), lambda i, j: (i, 0, j)),
            pl.BlockSpec((1, 1, MOD_TN), lambda i, j: (i, 0, j)),
        ],
        out_specs=pl.BlockSpec((1, COND_ROWS, MOD_TN), lambda i, j: (i, 0, j)),
        out_shape=jax.ShapeDtypeStruct((depth, COND_ROWS, 6 * D), F32),
        compiler_params=pltpu.CompilerParams(
            dimension_semantics=("arbitrary", "arbitrary"), vmem_limit_bytes=VMEM_LIMIT),
        name="adaln",
    )(cond, w_mod, b_mod.reshape(depth, 1, 6 * D))


_ROW_SPEC = pl.BlockSpec((1, D), lambda i: (0, 0))
_STEP_MOD_SPEC = pl.BlockSpec((1, 6, 1, D), lambda i, *_: (_cond_of_tile(i * STEP_TILES), 0, 0, 0))
_STEP_SPEC = pl.BlockSpec((STEP_ROWS, D), lambda i, *_: (i, 0))
_SUB_ROWS = tuple(slice(s * TM, (s + 1) * TM) for s in range(STEP_TILES))


def _load_weight_bf16(w_hbm, w16_ref, stage_ref, sem):
    n = w16_ref.shape[1] // D

    def slab(c):
        return pltpu.make_async_copy(w_hbm.at[:, pl.ds(c * D, D)], stage_ref.at[c % 2], sem.at[c % 2])

    slab(0).start()
    for c in range(n):
        if c + 1 < n:
            slab(c + 1).start()
        slab(c).wait()
        w16_ref[:, c * D:(c + 1) * D] = stage_ref[c % 2].astype(BF16)


def _weight_scratch(n_slabs):
    n_stage = min(n_slabs, 2)
    return [pltpu.VMEM((D, n_slabs * D), BF16), pltpu.VMEM((n_stage, D, D), F32),
            pltpu.SemaphoreType.DMA((n_stage,))]


_ANY_SPEC = pl.BlockSpec(memory_space=pl.ANY)


def _token_tile(x_refs, rows=slice(None), prompt_steps=PROMPT_TILES):
    if len(x_refs) == 1:
        return x_refs[0][rows]
    return jnp.where(pl.program_id(0) < prompt_steps, x_refs[0][rows], x_refs[1][rows])


def _token_specs(x_parts, tiles_per_step=1):
    rows = tiles_per_step * TM
    if len(x_parts) == 1:
        return [pl.BlockSpec((rows, D), lambda i: (i, 0))]
    prompt_steps = T_PROMPT // rows
    return [pl.BlockSpec((rows, D), lambda i: (jnp.minimum(i, prompt_steps - 1), 0)),
            pl.BlockSpec((rows, D), lambda i: (jnp.maximum(i - prompt_steps, 0), 0))]


def _hgrn_proj_kernel(xp_ref, xs_ref, mod_ref, g_ref, w_hbm, q_ref, v_ref, ff_ref, fb_ref, gg_ref,
                      w_ref, stage_ref, sem):
    @pl.when(pl.program_id(0) == 0)
    def _():
        _load_weight_bf16(w_hbm, w_ref, stage_ref, sem)

    h = [_modulate(_token_tile((xp_ref, xs_ref), rows, PROMPT_STEPS),
                   g_ref[...], mod_ref[0, 0], mod_ref[0, 1]).astype(BF16) for rows in _SUB_ROWS]
    for rows, h_tile in zip(_SUB_ROWS, h):
        q_ref[rows] = (_dot(h_tile, w_ref[:, 0 * D:1 * D]) * (DK ** -0.5)).astype(BF16)
        v_ref[rows] = _dot(h_tile, w_ref[:, 1 * D:2 * D]).astype(BF16)
        ff_ref[rows] = _dot(h_tile, w_ref[:, 2 * D:3 * D])
        fb_ref[rows] = _dot(h_tile, w_ref[:, 3 * D:4 * D])
        gg_ref[rows] = _dot(h_tile, w_ref[:, 4 * D:5 * D]).astype(BF16)


def _hgrn_proj(x_parts, mod, norm_g, w_in):
    bf = jax.ShapeDtypeStruct((T, D), BF16)
    f32 = jax.ShapeDtypeStruct((T, D), F32)
    return pl.pallas_call(
        _hgrn_proj_kernel,
        grid=(N_STEPS,),
        in_specs=_token_specs(x_parts, STEP_TILES) + [_STEP_MOD_SPEC, _ROW_SPEC, _ANY_SPEC],
        out_specs=[_STEP_SPEC] * 5,
        out_shape=[bf, bf, f32, f32, bf],
        scratch_shapes=_weight_scratch(5),
        compiler_params=_cparams(),
        name="hgrn_proj",
    )(*x_parts, mod, norm_g, w_in)


STEPS_PER_SAMPLE = SAMPLE_LEN // STEP_ROWS


def _gla_step(block, q_ref, v_ref, f_ref, lbl_ref, s0_ref, st_ref, st_first_ref, ma_ref, o_ref,
              *, reverse, finish=None):
    is_prompt = block < PROMPT_STEPS
    rel = block - PROMPT_STEPS
    edge = (STEPS_PER_SAMPLE - 1) if reverse else 0
    sample_start = jnp.logical_and(rel >= 0, rel % STEPS_PER_SAMPLE == edge)

    @pl.when(is_prompt)
    def _():
        st_ref[...] = jnp.zeros_like(st_ref)

    @pl.when(sample_start)
    def _():
        for h in range(HEADS):
            st_ref[h] = s0_ref[0, 0, h].T

    logits = lbl_ref[...]
    e = jnp.exp(logits - jnp.max(logits, axis=0, keepdims=True))
    lb = e[0:1] / jnp.sum(e, axis=0, keepdims=True)

    ref_idx = (CHUNK - 1 - CHUNK // 2) if reverse else CHUNK // 2
    last_idx = 0 if reverse else CHUNK - 1
    shift = CHUNK.bit_length() - 1

    row = lax.broadcasted_iota(jnp.int32, (TM, TM), 0)
    col = lax.broadcasted_iota(jnp.int32, (TM, TM), 1)
    same = (row >> shift) == (col >> shift)
    ref_row = ((row >> shift) << shift) + ref_idx
    keep = jnp.logical_and(same, (row <= col) if reverse else (row >= col))

    @pl.when(pl.program_id(0) == 0)
    def _():
        at_ref = jnp.logical_and(same, (ref_row <= col) if reverse else (ref_row >= col))
        ma_ref[...] = (keep.astype(F32) - at_ref.astype(F32)).astype(BF16)

    ma = ma_ref[...]

    rr = lax.broadcasted_iota(jnp.int32, (2 * CHUNKS_PER_TILE, TM), 0)
    cc = lax.broadcasted_iota(jnp.int32, (2 * CHUNKS_PER_TILE, TM), 1)
    chunk = rr & (CHUNKS_PER_TILE - 1)
    target = chunk * CHUNK + jnp.where(rr < CHUNKS_PER_TILE, ref_idx, last_idx)
    covered = (cc >= target) if reverse else (cc <= target)
    rm = jnp.logical_and((cc >> shift) == chunk, covered).astype(BF16)

    chunk_rows = [slice(c * CHUNK, (c + 1) * CHUNK) for c in range(CHUNKS_PER_TILE)]
    order = range(CHUNKS_PER_TILE - 1, -1, -1) if reverse else range(CHUNKS_PER_TILE)
    pair_w = 2 * DK
    n_pairs = HEADS // 2
    tile_order = _SUB_ROWS[::-1] if reverse else _SUB_ROWS

    def per_chunk_scale(x, scale):
        return jnp.concatenate([x[rs] * scale[c:c + 1] for c, rs in enumerate(chunk_rows)], axis=0)

    def pair_cols(p):
        return slice(p * pair_w, (p + 1) * pair_w)

    def stage_decay(k, p):
        lbp = lb[:, pair_cols(p)]
        f = lbp + (1.0 - lbp) * jax.nn.sigmoid(f_ref[tile_order[k], pair_cols(p)])
        lf_hi, lf_lo = _split_bf16(jnp.log(f))
        z = _dot(ma, lf_hi) + _dot(ma, lf_lo)
        marks = _dot(rm, lf_hi) + _dot(rm, lf_lo)
        return 1.0 - f, z, marks

    def stage_scores(k, p, kk, z, marks):
        ref, last = marks[:CHUNKS_PER_TILE], marks[CHUNKS_PER_TILE:]
        qa32 = q_ref[tile_order[k], pair_cols(p)].astype(F32) * jnp.exp(z)
        kb32 = kk * jnp.exp(-z)
        qa = qa32.astype(BF16)
        kb = kb32.astype(BF16)
        qe = per_chunk_scale(qa32, jnp.exp(ref)).astype(BF16)
        kd = per_chunk_scale(kb32, jnp.exp(last - ref)).astype(BF16)
        v = v_ref[tile_order[k], pair_cols(p)]
        scores, incr = [], []
        for hh in range(2):
            ls = slice(hh * DK, (hh + 1) * DK)
            scores.append(_dot_nt(qa[:, ls], kb[:, ls]))
            kd_h = kd[:, ls]
            kd_blocks = jnp.concatenate(
                [jnp.concatenate([part for part in (jnp.zeros((c * CHUNK, DK), BF16), kd_h[rs],
                                                    jnp.zeros((TM - (c + 1) * CHUNK, DK), BF16))
                                  if part.shape[0]], axis=0)
                 for c, rs in enumerate(chunk_rows)], axis=1)
            incr.append(_dot_tn(v[:, ls], kd_blocks))
        return scores, incr, qe, jnp.exp(last), v

    def stage_output(k, p, scores, incr, qe, decay, v):
        base = tile_order[k].start
        for hh in range(2):
            h = 2 * p + hh
            ls = slice(hh * DK, (hh + 1) * DK)
            hs = slice(h * DK, (h + 1) * DK)
            o_intra = _dot(jnp.where(keep, scores[hh], 0.0).astype(BF16), v[:, ls])
            st = st_ref[h]
            if k > 0:
                st = jnp.where(is_prompt, 0.0, st)
            for c in order:
                rs = chunk_rows[c]
                o_ref[base + rs.start:base + rs.stop, hs] = (
                    o_intra[rs] + _dot_nt(qe[rs, ls], st.astype(BF16)))
                st = st * decay[c:c + 1, ls] + incr[hh][:, c * DK:(c + 1) * DK]
            st_ref[h] = st
            if k == 0:
                st_first_ref[h] = st
        if finish is not None:
            return (tile_order[k], p)

    def stage_finish(k, p, rows, pair):
        finish(rows, pair)

    units = [(k, p) for k in range(STEP_TILES) for p in range(n_pairs)]
    stages = [stage_decay, stage_scores, stage_output] + ([stage_finish] if finish is not None else [])
    carried = [dict() for _ in stages]
    for step in range(len(units) + len(stages) - 1):
        for depth, stage in enumerate(stages):
            u = step - depth
            if 0 <= u < len(units):
                out = stage(*units[u], *(carried[depth].pop(u) if depth else ()))
                if depth + 1 < len(stages):
                    carried[depth + 1][u] = out


def _store_prompt_states(block, st_ref, st_first_ref, snew_ref, *, reverse, sfwd_ref=None):
    first, second = (1, 0) if reverse else (0, 1)

    @pl.when(block < PROMPT_STEPS)
    def _():
        for h in range(HEADS):
            if sfwd_ref is None:
                snew_ref[first, h] = st_first_ref[h].T
                snew_ref[second, h] = st_ref[h].T
            else:
                snew_ref[first, 1, h] = st_first_ref[h].T
                snew_ref[second, 1, h] = st_ref[h].T
        if sfwd_ref is not None:
            snew_ref[:, 0] = sfwd_ref[...]


def _gla_fwd_kernel(q_ref, v_ref, f_ref, lbl_ref, s0_ref, o_ref, snew_ref, st_ref, st_first_ref, ma_ref):
    block = pl.program_id(0)
    _gla_step(block, q_ref, v_ref, f_ref, lbl_ref, s0_ref, st_ref, st_first_ref, ma_ref, o_ref,
              reverse=False)
    _store_prompt_states(block, st_ref, st_first_ref, snew_ref, reverse=False)


def _gla_bwd_kernel(q_ref, v_ref, f_ref, lbl_ref, s0_ref, of_ref, gg_ref, ng_ref, sfwd_ref,
                    a_ref, snew_ref, st_ref, st_first_ref, ma_ref, ob_ref):
    block = N_STEPS - 1 - pl.program_id(0)

    def finish(rows, p):
        for h in (2 * p, 2 * p + 1):
            hs = slice(h * DV, (h + 1) * DV)
            o = of_ref[rows, hs] + ob_ref[rows, hs]
            a_ref[rows, hs] = (_rmsnorm(o, ng_ref[...]) * _silu(gg_ref[rows, hs].astype(F32))).astype(BF16)

    _gla_step(block, q_ref, v_ref, f_ref, lbl_ref, s0_ref, st_ref, st_first_ref, ma_ref, ob_ref,
              reverse=True, finish=finish)
    _store_prompt_states(block, st_ref, st_first_ref, snew_ref, reverse=True, sfwd_ref=sfwd_ref)


def _state_specs(direction, block_of_step):
    sample = lambda i: jnp.clip((block_of_step(i) - PROMPT_STEPS) // STEPS_PER_SAMPLE, 0, N_SAMPLE_SEQ - 1)
    s0 = pl.BlockSpec((1, 1, HEADS, DK, DV), lambda i: (sample(i), direction, 0, 0, 0))
    snew = pl.BlockSpec((STEP_TILES, HEADS, DK, DV),
                        lambda i: (jnp.minimum(block_of_step(i), PROMPT_STEPS - 1), 0, 0, 0))
    return s0, snew


_GLA_STATE_SCRATCH = [pltpu.VMEM((HEADS, DV, DK), F32), pltpu.VMEM((HEADS, DV, DK), F32),
                      pltpu.VMEM((TM, TM), BF16)]


def _gla_fwd(q, v, ff, lb_logits, s0):
    s0_spec, snew_spec = _state_specs(0, lambda i: i)
    return pl.pallas_call(
        _gla_fwd_kernel,
        grid=(N_STEPS,),
        in_specs=[_STEP_SPEC, _STEP_SPEC, _STEP_SPEC,
                  pl.BlockSpec(lb_logits.shape, lambda i: (0, 0)), s0_spec],
        out_specs=[_STEP_SPEC, snew_spec],
        out_shape=[jax.ShapeDtypeStruct((T, D), F32),
                   jax.ShapeDtypeStruct((N_PROMPT_SEQ, HEADS, DK, DV), F32)],
        scratch_shapes=_GLA_STATE_SCRATCH,
        compiler_params=_cparams(),
        name="gla_fwd",
    )(q, v, ff, lb_logits, s0)


def _gla_bwd(q, v, fb, lb_logits, s0, o_f, gg, head_norm_g, s_fwd):
    rev = lambda i: N_STEPS - 1 - i
    block = pl.BlockSpec((STEP_ROWS, D), lambda i: (rev(i), 0))
    s0_spec, sfwd_spec = _state_specs(1, rev)
    prompt_block = lambda i: jnp.minimum(rev(i), PROMPT_STEPS - 1)
    return pl.pallas_call(
        _gla_bwd_kernel,
        grid=(N_STEPS,),
        in_specs=[block, block, block, pl.BlockSpec(lb_logits.shape, lambda i: (0, 0)), s0_spec,
                  block, block, pl.BlockSpec((1, DV), lambda i: (0, 0)), sfwd_spec],
        out_specs=[block, pl.BlockSpec((STEP_TILES, 2, HEADS, DK, DV), lambda i: (prompt_block(i), 0, 0, 0, 0))],
        out_shape=[jax.ShapeDtypeStruct((T, D), BF16),
                   jax.ShapeDtypeStruct((N_PROMPT_SEQ, 2, HEADS, DK, DV), F32)],
        scratch_shapes=_GLA_STATE_SCRATCH + [pltpu.VMEM((STEP_ROWS, D), F32)],
        compiler_params=_cparams(),
        name="gla_bwd",
    )(q, v, fb, lb_logits, s0, o_f, gg, head_norm_g, s_fwd)


def _conv_mixer_kernel(x_ref, mod_ref, g_ref, w_hbm, cw_ref, a_ref, w_ref, stage_ref, sem):
    @pl.when(pl.program_id(0) == 0)
    def _():
        _load_weight_bf16(w_hbm, w_ref, stage_ref, sem)

    seg = jnp.where(pl.program_id(0) < PROMPT_STEPS, PROMPT_LEN, GRID_W)
    pos = lax.broadcasted_iota(jnp.int32, (TM, 1), 0) & (seg - 1)

    def project(rows):
        h = _modulate(x_ref[rows], g_ref[...], mod_ref[0, 0], mod_ref[0, 1]).astype(BF16)
        return [_dot(h, w_ref[:, k * D:(k + 1) * D]) for k in range(3)]

    def convolve(rows, bg, cg, x_in):
        u = cg * x_in
        prev = jnp.where(pos == 0, 0.0, pltpu.roll(u, 1, axis=0))
        nxt = jnp.where(pos == seg - 1, 0.0, pltpu.roll(u, TM - 1, axis=0))
        conv = cw_ref[0:1] * prev + cw_ref[1:2] * u + cw_ref[2:3] * nxt
        a_ref[rows] = (bg * conv).astype(BF16)

    projected = [project(rows) for rows in _SUB_ROWS]
    for rows, parts in zip(_SUB_ROWS, projected):
        convolve(rows, *parts)


def _conv_mixer(x, mod, norm_g, w_in, conv_w):
    return pl.pallas_call(
        _conv_mixer_kernel,
        grid=(N_STEPS,),
        in_specs=[_STEP_SPEC, _STEP_MOD_SPEC, _ROW_SPEC, _ANY_SPEC, pl.BlockSpec((3, D), lambda i: (0, 0))],
        out_specs=_STEP_SPEC,
        out_shape=jax.ShapeDtypeStruct((T, D), BF16),
        scratch_shapes=_weight_scratch(3),
        compiler_params=_cparams(),
        name="conv_mixer",
    )(x, mod, norm_g, w_in, conv_w)


ROUTE_ROWS = 32


def _first_member(vals, target):
    idx = jnp.full_like(target, float(EPG - 1))
    for j in range(EPG - 2, -1, -1):
        idx = jnp.where(vals[j] == target, float(j), idx)
    return idx


def _router_logits(h2, rwt_ref):
    h_hi, h_lo = _split_bf16(h2)
    w_hi, w_lo = _split_bf16(rwt_ref[...])
    return _dot_nt(w_hi, h_hi) + (_dot_nt(w_hi, h_lo) + _dot_nt(w_lo, h_hi))


def _select_experts(logits, rb_ref):
    scores = jax.nn.sigmoid(logits)
    sel = scores + rb_ref[...]
    member = [sel[j * N_GROUPS:(j + 1) * N_GROUPS] for j in range(EPG)]
    m1 = functools.reduce(jnp.maximum, member)
    i1 = _first_member(member, m1)
    rest = [jnp.where(i1 == float(j), -jnp.inf, member[j]) for j in range(EPG)]
    m2 = functools.reduce(jnp.maximum, rest)
    i2 = _first_member(rest, m2)
    group_score = m1 + m2

    best, grp, first, second = group_score[0:1], jnp.zeros((1, TM), F32), i1[0:1], i2[0:1]
    for g in range(1, N_GROUPS):
        upd = group_score[g:g + 1] > best
        best = jnp.where(upd, group_score[g:g + 1], best)
        grp = jnp.where(upd, float(g), grp)
        first = jnp.where(upd, i1[g:g + 1], first)
        second = jnp.where(upd, i2[g:g + 1], second)
    a = jnp.minimum(first, second)
    b = jnp.maximum(first, second)
    row = lax.broadcasted_iota(jnp.int32, (N_EXPERTS, TM), 0).astype(F32)
    s_lo = jnp.sum(jnp.where(row == a * N_GROUPS + grp, scores, 0.0), axis=0, keepdims=True)
    s_hi = jnp.sum(jnp.where(row == b * N_GROUPS + grp, scores, 0.0), axis=0, keepdims=True)
    tot = s_lo + s_hi
    pair = jnp.where(a == 0.0, jnp.where(b == 1.0, 0.0, jnp.where(b == 2.0, 3.0, 4.0)),
                     jnp.where(a == 1.0, jnp.where(b == 2.0, 5.0, 1.0), 2.0))
    return grp * N_PAIRS + pair, s_lo / tot, s_hi / tot


def _post_mixer_kernel(*refs, n_x):
    a_ref, x_refs = refs[0], refs[1:1 + n_x]
    (mod_ref, g2_ref, wo_hbm, rwt_ref, rb_ref,
     x1_ref, gates_ref, bucket_ref, rank_ref, cnt_ref, carry_ref, earlier_ref, wo_ref, stage_ref, sem) = refs[1 + n_x:]

    @pl.when(pl.program_id(0) == 0)
    def _():
        _load_weight_bf16(wo_hbm, wo_ref, stage_ref, sem)
        carry_ref[...] = jnp.zeros_like(carry_ref)
        s_idx = lax.broadcasted_iota(jnp.int32, (TM, TM), 0)
        t_idx = lax.broadcasted_iota(jnp.int32, (TM, TM), 1)
        earlier_ref[...] = (s_idx < t_idx).astype(BF16)

    def stage_mix(s):
        rows = slice(s * TM, (s + 1) * TM)
        x1 = (_token_tile(x_refs, rows, PROMPT_TILES // ROUTE_TILES)
              + mod_ref[0, 2] * _dot(a_ref[rows], wo_ref[...]))
        x1_ref[rows] = x1
        return _modulate(x1, g2_ref[...], mod_ref[0, 3], mod_ref[0, 4])

    def stage_rank(s, bucket, g_lo, g_hi):
        onehot = lax.broadcasted_iota(jnp.int32, (ROUTE_ROWS, TM), 0).astype(F32) == bucket
        before = _dot(onehot.astype(BF16), earlier_ref[...])
        oh = onehot.astype(F32)
        carry = carry_ref[...]
        rank = jnp.sum(oh * (before + carry[:, 0:1]), axis=0, keepdims=True)
        carry_ref[...] = carry + jnp.sum(oh, axis=1, keepdims=True)
        bucket_ref[s] = bucket.astype(jnp.int32)
        rank_ref[s] = rank.astype(jnp.int32)
        grow = lax.broadcasted_iota(jnp.int32, (LANES, TM), 0)
        gates_ref[s * TM:(s + 1) * TM] = jnp.where(grow == 0, g_lo, jnp.where(grow == 1, g_hi, 0.0)).T

    tiles = range(ROUTE_TILES)
    h2 = [stage_mix(s) for s in tiles]
    logits = [_router_logits(h2[s], rwt_ref) for s in tiles]
    picks = [_select_experts(logits[s], rb_ref) for s in tiles]
    for s in tiles:
        stage_rank(s, *picks[s])
    cnt_ref[...] = carry_ref[...]


def _post_mixer(a, x_parts, mod, norm_g2, w_out, router_wt, router_bt):
    rows = ROUTE_TILES * TM
    idx_spec = pl.BlockSpec((ROUTE_TILES, 1, TM), lambda i: (i, 0, 0))
    return pl.pallas_call(
        functools.partial(_post_mixer_kernel, n_x=len(x_parts)),
        grid=(N_TILES // ROUTE_TILES,),
        in_specs=[pl.BlockSpec((rows, D), lambda i: (i, 0))] + _token_specs(x_parts, ROUTE_TILES) + [
            pl.BlockSpec((1, 6, 1, D), lambda i: (_cond_of_tile(i * ROUTE_TILES), 0, 0, 0)), _ROW_SPEC,
            _ANY_SPEC,
            pl.BlockSpec((N_EXPERTS, D), lambda i: (0, 0)),
            pl.BlockSpec((N_EXPERTS, 1), lambda i: (0, 0))],
        out_specs=[pl.BlockSpec((rows, D), lambda i: (i, 0)), pl.BlockSpec((rows, LANES), lambda i: (i, 0)),
                   idx_spec, idx_spec, pl.BlockSpec((ROUTE_ROWS, LANES), lambda i: (0, 0))],
        out_shape=[jax.ShapeDtypeStruct((T, D), F32), jax.ShapeDtypeStruct((T, LANES), F32),
                   jax.ShapeDtypeStruct((N_TILES, 1, TM), jnp.int32),
                   jax.ShapeDtypeStruct((N_TILES, 1, TM), jnp.int32),
                   jax.ShapeDtypeStruct((ROUTE_ROWS, LANES), F32)],
        scratch_shapes=[pltpu.VMEM((ROUTE_ROWS, LANES), F32), pltpu.VMEM((TM, TM), BF16)] + _weight_scratch(1),
        compiler_params=_cparams(),
        name="post_mixer",
    )(a, *x_parts, mod, norm_g2, w_out, router_wt, router_bt)


_MOVE_SPEC = pl.BlockSpec((MOVE_ROWS, D), lambda i, *_: (i, 0))
_MOVE_MOD_SPEC = pl.BlockSpec((1, 6, 1, D), lambda i, *_: (_cond_of_tile(i * MOVE_TILES), 0, 0, 0))


def _smem_step_spec(step_of_step):
    return pl.BlockSpec((MOVE_TILES, 1, TM), lambda i, *_: (step_of_step(i), 0, 0), memory_space=pltpu.SMEM)


def _start_step_rows(make_copy):
    for tile in range(MOVE_TILES):
        for r in range(TM):
            make_copy(tile, r).start(priority=r % 2)


def _dispatch_kernel(zero_from_ref, slot_ref, x1_ref, mod_ref, g2_ref, gates_ref,
                     xs_ref, buf_ref, zero_ref, sems, zero_sem):
    i = pl.program_id(0)
    cur = i % 2

    @pl.when(i == 0)
    def _():
        zero_ref[...] = jnp.zeros_like(zero_ref)

        def for_each_chunk(act):
            def tile(j, carry):
                def chunk(c, inner):
                    row = pl.multiple_of(j * TM_E + c * ZERO_ROWS, ZERO_ROWS)
                    act(pltpu.make_async_copy(zero_ref, xs_ref.at[pl.ds(row, ZERO_ROWS)], zero_sem))
                    return inner
                return lax.fori_loop(zero_from_ref[j], TM_E // ZERO_ROWS, chunk, carry)
            lax.fori_loop(0, N_TILES_E, tile, 0)

        for_each_chunk(lambda copy: copy.start())
        for_each_chunk(lambda copy: copy.wait())

    def wait_step(b):
        pltpu.make_async_copy(buf_ref.at[b], xs_ref.at[pl.ds(0, MOVE_ROWS)], sems.at[b]).wait()

    h2 = _modulate(x1_ref[...], g2_ref[...], mod_ref[0, 3], mod_ref[0, 4])
    gates = gates_ref[...]

    def send(b):
        buf_ref[b, :, :D] = h2
        buf_ref[b, :, D:] = gates
        _start_step_rows(lambda tile, r: pltpu.make_async_copy(
            buf_ref.at[b, pl.ds(tile * TM + r, 1)], xs_ref.at[pl.ds(slot_ref[tile, 0, r], 1)], sems.at[b]))

        @pl.when(i > 0)
        def _():
            wait_step(1 - b)

        @pl.when(i == N_MOVE_STEPS - 1)
        def _():
            wait_step(b)

    for b in range(2):
        pl.when(cur == b)(functools.partial(send, b))


def _dispatch(zero_from, slot, x1, mod, norm_g2, gates):
    return pl.pallas_call(
        _dispatch_kernel,
        grid_spec=pltpu.PrefetchScalarGridSpec(
            num_scalar_prefetch=1,
            grid=(N_MOVE_STEPS,),
            in_specs=[_smem_step_spec(lambda i: i), _MOVE_SPEC, _MOVE_MOD_SPEC,
                      pl.BlockSpec((1, D), lambda i, *_: (0, 0)),
                      pl.BlockSpec((MOVE_ROWS, LANES), lambda i, *_: (i, 0))],
            out_specs=pl.BlockSpec(memory_space=pl.ANY),
            scratch_shapes=[pltpu.VMEM((2, MOVE_ROWS, XW), F32), pltpu.VMEM((ZERO_ROWS, XW), F32),
                            pltpu.SemaphoreType.DMA((2,)), pltpu.SemaphoreType.DMA(())],
        ),
        out_shape=jax.ShapeDtypeStruct((S_ROWS, XW), F32),
        compiler_params=_cparams(),
        name="moe_dispatch",
    )(zero_from, slot, x1, mod, norm_g2, gates)


_W_SHAPES = ((D, D_FF), (D, D_FF), (D_FF, D))


def _expert_kernel(ea_ref, eb_ref, next_a_ref, next_b_ref, a_higher_ref, nrows_ref, xblock_ref,
                   x_ref, wg_hbm, wu_hbm, wd_hbm, y_ref, *scratch, layer):
    stage, w16, sems = scratch[0:3], scratch[3:6], scratch[6]
    j = pl.program_id(0)
    n = nrows_ref[j]
    prev = jnp.maximum(j - 1, 0)

    def fetch(slot, expert):
        return [pltpu.make_async_copy(w.at[layer, expert], s.at[slot], sems.at[slot])
                for w, s in zip((wg_hbm, wu_hbm, wd_hbm), stage)]

    for slot, e_ref, next_ref in ((0, ea_ref, next_a_ref), (1, eb_ref, next_b_ref)):
        @pl.when(j == 0)
        def _(slot=slot, e_ref=e_ref):
            for copy in fetch(slot, e_ref[0]):
                copy.start()

        @pl.when(jnp.logical_or(j == 0, e_ref[j] != e_ref[prev]))
        def _(slot=slot, e_ref=e_ref, next_ref=next_ref):
            for copy in fetch(slot, e_ref[j]):
                copy.wait()
            for s, d in zip(stage, w16):
                d[slot] = s[slot].astype(BF16)

            @pl.when(next_ref[j] >= 0)
            def _():
                for copy in fetch(slot, next_ref[j]):
                    copy.start()

    def run(rows):
        valid = lax.broadcasted_iota(jnp.int32, (rows, 1), 0) < n
        xe = jnp.where(valid, x_ref[:rows], 0.0)
        x = xe[:, :D].astype(BF16)
        g_lo, g_hi = xe[:, D:D + 1], xe[:, D + 1:D + 2]
        a_higher = a_higher_ref[j] != 0

        def ffn(slot, gate):
            wg_ref, wu_ref, wd_ref = w16
            act = _silu(_dot(x, wg_ref[slot])) * _dot(x, wu_ref[slot]) * gate
            return _dot(act.astype(BF16), wd_ref[slot])

        y_ref[:rows] = (ffn(0, jnp.where(a_higher, g_hi, g_lo))
                        + ffn(1, jnp.where(a_higher, g_lo, g_hi)))
        if rows < TM_E:
            y_ref[rows:] = jnp.zeros((TM_E - rows, D), F32)

    half = TM_E // 2
    pl.when(n > half)(functools.partial(run, TM_E))
    pl.when(jnp.logical_and(n > 0, n <= half))(functools.partial(run, half))

    @pl.when(n == 0)
    def _():
        y_ref[...] = jnp.zeros_like(y_ref)


def _experts(tile_ea, tile_eb, tile_next_a, tile_next_b, tile_a_higher, tile_rows, tile_xblock, xs,
             w_gate, w_up, w_down, layer):
    any_spec = pl.BlockSpec(memory_space=pl.ANY)
    return pl.pallas_call(
        functools.partial(_expert_kernel, layer=layer),
        grid_spec=pltpu.PrefetchScalarGridSpec(
            num_scalar_prefetch=7,
            grid=(N_TILES_E,),
            in_specs=[pl.BlockSpec((TM_E, XW), lambda j, *plan: (plan[-1][j], 0)), any_spec, any_spec, any_spec],
            out_specs=pl.BlockSpec((TM_E, D), lambda j, *_: (j, 0)),
            scratch_shapes=[pltpu.VMEM((2,) + s, F32) for s in _W_SHAPES]
                           + [pltpu.VMEM((2,) + s, BF16) for s in _W_SHAPES]
                           + [pltpu.SemaphoreType.DMA((2,))],
        ),
        out_shape=jax.ShapeDtypeStruct((S_ROWS, D), F32),
        compiler_params=_cparams(),
        name="moe_experts",
    )(tile_ea, tile_eb, tile_next_a, tile_next_b, tile_a_higher, tile_rows, tile_xblock, xs,
      w_gate, w_up, w_down)


def _combine_kernel(slot_ref, next_slot_ref, x1_ref, mod_ref, gf_ref, y_ref, *rest, final):
    out_refs, (buf_ref, sems) = rest[:-2], rest[-2:]
    i = pl.program_id(0)
    cur = i % 2

    def gather(slots, b):
        _start_step_rows(lambda tile, r: pltpu.make_async_copy(
            y_ref.at[pl.ds(slots[tile, 0, r], 1)], buf_ref.at[b, pl.ds(tile * TM + r, 1)], sems.at[b]))

    @pl.when(i == 0)
    def _():
        gather(slot_ref, 0)

    def combine(b):
        @pl.when(i + 1 < N_MOVE_STEPS)
        def _():
            gather(next_slot_ref, 1 - b)

        pltpu.make_async_copy(y_ref.at[pl.ds(0, MOVE_ROWS)], buf_ref.at[b], sems.at[b]).wait()
        x2 = x1_ref[...] + mod_ref[0, 5] * buf_ref[b]
        if not final:
            out_refs[0][...] = x2
            return
        y = _rmsnorm(x2, gf_ref[...])
        yp_ref, ys_ref = out_refs

        @pl.when(i < T_PROMPT // MOVE_ROWS)
        def _():
            yp_ref[...] = y

        @pl.when(i >= T_PROMPT // MOVE_ROWS)
        def _():
            ys_ref[...] = y

    for b in range(2):
        pl.when(cur == b)(functools.partial(combine, b))


def _combine(slot, x1, mod, final_g, y_sorted, *, final):
    if final:
        out_specs = _token_specs((None, None), MOVE_TILES)
        out_shape = [jax.ShapeDtypeStruct((T_PROMPT, D), F32), jax.ShapeDtypeStruct((T_SAMPLE, D), F32)]
    else:
        out_specs = [_MOVE_SPEC]
        out_shape = [jax.ShapeDtypeStruct((T, D), F32)]
    return pl.pallas_call(
        functools.partial(_combine_kernel, final=final),
        grid=(N_MOVE_STEPS,),
        in_specs=[_smem_step_spec(lambda i: i), _smem_step_spec(lambda i: jnp.minimum(i + 1, N_MOVE_STEPS - 1)),
                  _MOVE_SPEC, _MOVE_MOD_SPEC, _ROW_SPEC, pl.BlockSpec(memory_space=pl.ANY)],
        out_specs=out_specs,
        scratch_shapes=[pltpu.VMEM((2, MOVE_ROWS, D), F32), pltpu.SemaphoreType.DMA((2,))],
        out_shape=out_shape,
        compiler_params=_cparams(),
        name="moe_combine_final" if final else "moe_combine",
    )(slot, slot, x1, mod, final_g, y_sorted)


def _plan_kernel(cnt_ref, bucket_ref, rank_ref, slot_ref,
                 ea_ref, eb_ref, next_a_ref, next_b_ref, a_higher_ref, rows_ref, xblock_ref, zero_from_ref):
    slot = rank_ref[...]
    bucket = bucket_ref[...]
    tile0 = jnp.int32(0)
    last = [jnp.int32(0)] * 3
    for b in range(N_BUCKETS):
        n = cnt_ref[b]
        tiles = (n + (TM_E - 1)) // TM_E
        pair = b % N_PAIRS
        per_tile = ((b // N_PAIRS) * EPG + _SLOT_A[pair], (b // N_PAIRS) * EPG + _SLOT_B[pair],
                    _SLOT_A_IS_HIGHER[pair])

        def fill(k, carry, n=n, tile0=tile0, per_tile=per_tile):
            j = tile0 + k
            live = jnp.minimum(n - k * TM_E, TM_E)
            rows_ref[j] = live
            xblock_ref[j] = j
            zero_from_ref[j] = live // ZERO_ROWS
            for ref, value in zip((ea_ref, eb_ref, a_higher_ref), per_tile):
                ref[j] = jnp.int32(value)
            return carry

        lax.fori_loop(0, tiles, fill, 0)
        slot = slot + jnp.where(bucket == b, tile0 * TM_E, 0)
        last = [jnp.where(tiles > 0, value, old) for value, old in zip(per_tile, last)]
        tile0 = tile0 + tiles

    def unused(j, carry):
        rows_ref[j] = jnp.int32(0)
        xblock_ref[j] = jnp.maximum(tile0 - 1, 0)
        zero_from_ref[j] = jnp.int32(0)
        for ref, value in zip((ea_ref, eb_ref, a_higher_ref), last):
            ref[j] = value
        return carry

    lax.fori_loop(tile0, N_TILES_E, unused, 0)

    for e_ref, next_ref in ((ea_ref, next_a_ref), (eb_ref, next_b_ref)):
        next_ref[N_TILES_E - 1] = jnp.int32(-1)

        def backward(k, carry, e_ref=e_ref, next_ref=next_ref):
            j = N_TILES_E - 2 - k
            next_ref[j] = jnp.where(e_ref[j] == e_ref[j + 1], next_ref[j + 1], e_ref[j + 1])
            return carry

        lax.fori_loop(0, N_TILES_E - 1, backward, 0)
    slot_ref[...] = slot


def _plan(counts, bucket, rank):
    whole = pl.BlockSpec(bucket.shape, lambda i, *_: (0, 0, 0))
    smem = pl.BlockSpec(memory_space=pltpu.SMEM)
    per_tile = jax.ShapeDtypeStruct((N_TILES_E,), jnp.int32)
    slot, *tile_plan = pl.pallas_call(
        _plan_kernel,
        grid_spec=pltpu.PrefetchScalarGridSpec(
            num_scalar_prefetch=1, grid=(1,), in_specs=[whole, whole], out_specs=[whole] + [smem] * 8),
        out_shape=[jax.ShapeDtypeStruct(bucket.shape, jnp.int32)] + [per_tile] * 8,
        compiler_params=_cparams(),
        name="moe_plan",
    )(counts[:, 0].astype(jnp.int32), bucket, rank)
    return slot, tile_plan


def _moe(a, x_parts, mod, norm_g2, w_out, router_wt, router_bt, w_gate, w_up, w_down, layer, final_g,
         *, final):
    x1, gates, bucket, rank, counts = _post_mixer(a, x_parts, mod, norm_g2, w_out, router_wt, router_bt)
    slot, (*expert_plan, zero_from) = _plan(counts, bucket, rank)
    xs = _dispatch(zero_from, slot, x1, mod, norm_g2, gates)
    ys = _experts(*expert_plan, xs, w_gate, w_up, w_down, layer)
    return _combine(slot, x1, mod, final_g, ys, final=final)


def kernel(x_prompt, x_sample, state_hgrn, c, c_ctx, w_mod, b_mod, norm_mix_g, norm_ffn_g, norm_final_g,
           hgrn_w_in, hgrn_lb_logits, hgrn_norm_g, hgrn_w_out, conv_w_in, conv_w, conv_w_out,
           router_w, router_b, moe_w_gate, moe_w_up, moe_w_down):
    x_parts = (x_prompt.reshape(T_PROMPT, D), x_sample.reshape(T_SAMPLE, D))
    cond = jnp.concatenate([c_ctx[None], c, jnp.zeros((COND_ROWS - N_COND, D), F32)], axis=0)
    mods = _adaln(cond, w_mod, b_mod)[:, :N_COND].reshape(2, N_COND, 6, 1, D)
    member_major = lambda w: w.reshape(N_GROUPS, EPG, -1).transpose(1, 0, 2).reshape(N_EXPERTS, -1)
    rwt = member_major(router_w.T)
    rbt = member_major(router_b.reshape(N_EXPERTS, 1))
    final_g = norm_final_g.reshape(1, D)

    q, v, ff, fb, gg = _hgrn_proj(x_parts, mods[0], norm_mix_g[0:1], hgrn_w_in[0])
    s0 = state_hgrn[:, 0]
    o_f, s_f = _gla_fwd(q, v, ff, hgrn_lb_logits, s0)
    a, new_state = _gla_bwd(q, v, fb, hgrn_lb_logits, s0, o_f, gg, hgrn_norm_g[0:1], s_f)
    (x,) = _moe(a, x_parts, mods[0], norm_ffn_g[0:1], hgrn_w_out[0], rwt, rbt,
                moe_w_gate, moe_w_up, moe_w_down, 0, final_g, final=False)

    a = _conv_mixer(x, mods[1], norm_mix_g[1:2], conv_w_in[0], conv_w[0])
    y_p, y_s = _moe(a, (x,), mods[1], norm_ffn_g[1:2], conv_w_out[0], rwt, rbt,
                    moe_w_gate, moe_w_up, moe_w_down, 1, final_g, final=True)

    return (y_p.reshape(N_PROMPT_SEQ, PROMPT_LEN, D), y_s.reshape(N_SAMPLE_SEQ, SAMPLE_LEN, D),
            new_state[:, None])
```

```python
import functools

import jax
import jax.numpy as jnp
from jax import lax
from jax.experimental import pallas as pl
from jax.experimental.pallas import tpu as pltpu

F32 = jnp.float32
BF16 = jnp.bfloat16

D = 1024
N_PROMPT_SEQ = 16
PROMPT_LEN = 256
N_SAMPLE_SEQ = 2
SAMPLE_LEN = 4096
GRID_W = 64
T_PROMPT = N_PROMPT_SEQ * PROMPT_LEN
T_SAMPLE = N_SAMPLE_SEQ * SAMPLE_LEN
T = T_PROMPT + T_SAMPLE
N_COND = 1 + N_SAMPLE_SEQ
COND_ROWS = 8

HEADS = 8
DK = 128
DV = 128
CHUNK = 64
N_EXPERTS = 16
N_GROUPS = 4
EPG = 4
N_PAIRS = 6
N_BUCKETS = N_GROUPS * N_PAIRS
D_FF = 512
EPS = 1e-6

TM = 256
N_TILES = T // TM
PROMPT_TILES = T_PROMPT // TM
TILES_PER_SAMPLE = SAMPLE_LEN // TM
CHUNKS_PER_TILE = TM // CHUNK
STEP_TILES = 2
STEP_ROWS = STEP_TILES * TM
N_STEPS = N_TILES // STEP_TILES
PROMPT_STEPS = PROMPT_TILES // STEP_TILES
MOVE_TILES = 2
MOVE_ROWS = MOVE_TILES * TM
N_MOVE_STEPS = N_TILES // MOVE_TILES
ROUTE_TILES = 4
CONV_TILES = 4
TM_E = 288
N_TILES_E = T // TM_E + N_BUCKETS
S_ROWS = N_TILES_E * TM_E
ZERO_ROWS = 32
XW = D + 128
LANES = 128
MOD_TN = 1536

V7X_VMEM_BYTES = 64 * 1024 * 1024
VMEM_LIMIT = V7X_VMEM_BYTES * 7 // 8

_SLOT_A = (0, 3, 3, 0, 0, 1)
_SLOT_B = (1, 1, 2, 2, 3, 2)
_SLOT_A_IS_HIGHER = tuple(int(a > b) for a, b in zip(_SLOT_A, _SLOT_B))


def _cparams():
    return pltpu.CompilerParams(dimension_semantics=("arbitrary",), vmem_limit_bytes=VMEM_LIMIT)


def _cond_of_tile(t):
    return jnp.where(t < PROMPT_TILES, 0, 1 + (t - PROMPT_TILES) // TILES_PER_SAMPLE)


def _rmsnorm(x, g):
    ms = jnp.mean(x * x, axis=-1, keepdims=True)
    return (x * lax.rsqrt(ms + EPS)) * g


def _modulate(x, g, shift, scale):
    return _rmsnorm(x, g) * (1.0 + scale) + shift


def _silu(x):
    return x * jax.nn.sigmoid(x)


def _dot(a, b):
    return jnp.dot(a, b, preferred_element_type=F32)


def _dot_nt(a, b):
    return lax.dot_general(a, b, (((1,), (1,)), ((), ())), preferred_element_type=F32)


def _dot_tn(a, b):
    return lax.dot_general(a, b, (((0,), (0,)), ((), ())), preferred_element_type=F32)


def _split_bf16(x):
    hi = x.astype(BF16)
    lo = (x - hi.astype(F32)).astype(BF16)
    return hi, lo


def _mod_kernel(cond_ref, w_ref, b_ref, o_ref):
    s = _silu(cond_ref[...])
    o_ref[0] = _dot(s.astype(BF16), w_ref[0].astype(BF16)) + b_ref[0]


def _adaln(cond, w_mod, b_mod):
    depth = w_mod.shape[0]
    return pl.pallas_call(
        _mod_kernel,
        grid=(depth, 6 * D // MOD_TN),
        in_specs=[
            pl.BlockSpec((COND_ROWS, D), lambda i, j: (0, 0)),
            pl.BlockSpec((1, D, MOD_TN), lambda i, j: (i, 0, j)),
            pl.BlockSpec((1, 1, MOD_TN), lambda i, j: (i, 0, j)),
        ],
        out_specs=pl.BlockSpec((1, COND_ROWS, MOD_TN), lambda i, j: (i, 0, j)),
        out_shape=jax.ShapeDtypeStruct((depth, COND_ROWS, 6 * D), F32),
        compiler_params=pltpu.CompilerParams(
            dimension_semantics=("arbitrary", "arbitrary"), vmem_limit_bytes=VMEM_LIMIT),
        name="adaln",
    )(cond, w_mod, b_mod.reshape(depth, 1, 6 * D))


_ROW_SPEC = pl.BlockSpec((1, D), lambda i: (0, 0))
_STEP_MOD_SPEC = pl.BlockSpec((1, 6, 1, D), lambda i, *_: (_cond_of_tile(i * STEP_TILES), 0, 0, 0))
_STEP_SPEC = pl.BlockSpec((STEP_ROWS, D), lambda i, *_: (i, 0))
_SUB_ROWS = tuple(slice(s * TM, (s + 1) * TM) for s in range(STEP_TILES))


def _load_weight_bf16(w_hbm, w16_ref, stage_ref, sem):
    n = w16_ref.shape[1] // D

    def slab(c):
        return pltpu.make_async_copy(w_hbm.at[:, pl.ds(c * D, D)], stage_ref.at[c % 2], sem.at[c % 2])

    slab(0).start()
    for c in range(n):
        if c + 1 < n:
            slab(c + 1).start()
        slab(c).wait()
        w16_ref[:, c * D:(c + 1) * D] = stage_ref[c % 2].astype(BF16)


def _weight_scratch(n_slabs):
    n_stage = min(n_slabs, 2)
    return [pltpu.VMEM((D, n_slabs * D), BF16), pltpu.VMEM((n_stage, D, D), F32),
            pltpu.SemaphoreType.DMA((n_stage,))]


_ANY_SPEC = pl.BlockSpec(memory_space=pl.ANY)


def _token_tile(x_refs, rows=slice(None), prompt_steps=PROMPT_TILES):
    if len(x_refs) == 1:
        return x_refs[0][rows]
    return jnp.where(pl.program_id(0) < prompt_steps, x_refs[0][rows], x_refs[1][rows])


def _token_specs(x_parts, tiles_per_step=1):
    rows = tiles_per_step * TM
    if len(x_parts) == 1:
        return [pl.BlockSpec((rows, D), lambda i: (i, 0))]
    prompt_steps = T_PROMPT // rows
    return [pl.BlockSpec((rows, D), lambda i: (jnp.minimum(i, prompt_steps - 1), 0)),
            pl.BlockSpec((rows, D), lambda i: (jnp.maximum(i - prompt_steps, 0), 0))]


def _hgrn_proj_kernel(xp_ref, xs_ref, mod_ref, g_ref, w_hbm, q_ref, v_ref, ff_ref, fb_ref, gg_ref,
                      w_ref, stage_ref, sem):
    @pl.when(pl.program_id(0) == 0)
    def _():
        _load_weight_bf16(w_hbm, w_ref, stage_ref, sem)

    h = [_modulate(_token_tile((xp_ref, xs_ref), rows, PROMPT_STEPS),
                   g_ref[...], mod_ref[0, 0], mod_ref[0, 1]).astype(BF16) for rows in _SUB_ROWS]
    for rows, h_tile in zip(_SUB_ROWS, h):
        q_ref[rows] = (_dot(h_tile, w_ref[:, 0 * D:1 * D]) * (DK ** -0.5)).astype(BF16)
        v_ref[rows] = _dot(h_tile, w_ref[:, 1 * D:2 * D]).astype(BF16)
        ff_ref[rows] = _dot(h_tile, w_ref[:, 2 * D:3 * D])
        fb_ref[rows] = _dot(h_tile, w_ref[:, 3 * D:4 * D])
        gg_ref[rows] = _dot(h_tile, w_ref[:, 4 * D:5 * D]).astype(BF16)


def _hgrn_proj(x_parts, mod, norm_g, w_in):
    bf = jax.ShapeDtypeStruct((T, D), BF16)
    f32 = jax.ShapeDtypeStruct((T, D), F32)
    return pl.pallas_call(
        _hgrn_proj_kernel,
        grid=(N_STEPS,),
        in_specs=_token_specs(x_parts, STEP_TILES) + [_STEP_MOD_SPEC, _ROW_SPEC, _ANY_SPEC],
        out_specs=[_STEP_SPEC] * 5,
        out_shape=[bf, bf, f32, f32, bf],
        scratch_shapes=_weight_scratch(5),
        compiler_params=_cparams(),
        name="hgrn_proj",
    )(*x_parts, mod, norm_g, w_in)


STEPS_PER_SAMPLE = SAMPLE_LEN // STEP_ROWS


def _gla_step(block, q_ref, v_ref, f_ref, lbl_ref, s0_ref, st_ref, st_first_ref, ma_ref, o_ref,
              *, reverse, finish=None):
    is_prompt = block < PROMPT_STEPS
    rel = block - PROMPT_STEPS
    edge = (STEPS_PER_SAMPLE - 1) if reverse else 0
    sample_start = jnp.logical_and(rel >= 0, rel % STEPS_PER_SAMPLE == edge)

    @pl.when(is_prompt)
    def _():
        st_ref[...] = jnp.zeros_like(st_ref)

    @pl.when(sample_start)
    def _():
        for h in range(HEADS):
            st_ref[h] = s0_ref[0, 0, h].T

    logits = lbl_ref[...]
    e = jnp.exp(logits - jnp.max(logits, axis=0, keepdims=True))
    lb = e[0:1] / jnp.sum(e, axis=0, keepdims=True)

    ref_idx = (CHUNK - 1 - CHUNK // 2) if reverse else CHUNK // 2
    last_idx = 0 if reverse else CHUNK - 1
    shift = CHUNK.bit_length() - 1

    row = lax.broadcasted_iota(jnp.int32, (TM, TM), 0)
    col = lax.broadcasted_iota(jnp.int32, (TM, TM), 1)
    same = (row >> shift) == (col >> shift)
    ref_row = ((row >> shift) << shift) + ref_idx
    keep = jnp.logical_and(same, (row <= col) if reverse else (row >= col))

    @pl.when(pl.program_id(0) == 0)
    def _():
        at_ref = jnp.logical_and(same, (ref_row <= col) if reverse else (ref_row >= col))
        ma_ref[...] = (keep.astype(F32) - at_ref.astype(F32)).astype(BF16)

    ma = ma_ref[...]

    rr = lax.broadcasted_iota(jnp.int32, (2 * CHUNKS_PER_TILE, TM), 0)
    cc = lax.broadcasted_iota(jnp.int32, (2 * CHUNKS_PER_TILE, TM), 1)
    chunk = rr & (CHUNKS_PER_TILE - 1)
    target = chunk * CHUNK + jnp.where(rr < CHUNKS_PER_TILE, ref_idx, last_idx)
    covered = (cc >= target) if reverse else (cc <= target)
    rm = jnp.logical_and((cc >> shift) == chunk, covered).astype(BF16)

    chunk_rows = [slice(c * CHUNK, (c + 1) * CHUNK) for c in range(CHUNKS_PER_TILE)]
    order = range(CHUNKS_PER_TILE - 1, -1, -1) if reverse else range(CHUNKS_PER_TILE)
    pair_w = 2 * DK
    n_pairs = HEADS // 2
    tile_order = _SUB_ROWS[::-1] if reverse else _SUB_ROWS

    def per_chunk_scale(x, scale):
        return jnp.concatenate([x[rs] * scale[c:c + 1] for c, rs in enumerate(chunk_rows)], axis=0)

    def pair_cols(p):
        return slice(p * pair_w, (p + 1) * pair_w)

    def stage_decay(k, p):
        lbp = lb[:, pair_cols(p)]
        f = lbp + (1.0 - lbp) * jax.nn.sigmoid(f_ref[tile_order[k], pair_cols(p)])
        lf_hi, lf_lo = _split_bf16(jnp.log(f))
        z = _dot(ma, lf_hi) + _dot(ma, lf_lo)
        marks = _dot(rm, lf_hi) + _dot(rm, lf_lo)
        return 1.0 - f, z, marks

    def stage_scores(k, p, kk, z, marks):
        ref, last = marks[:CHUNKS_PER_TILE], marks[CHUNKS_PER_TILE:]
        qa32 = q_ref[tile_order[k], pair_cols(p)].astype(F32) * jnp.exp(z)
        kb32 = kk * jnp.exp(-z)
        qa = qa32.astype(BF16)
        kb = kb32.astype(BF16)
        qe = per_chunk_scale(qa32, jnp.exp(ref)).astype(BF16)
        kd = per_chunk_scale(kb32, jnp.exp(last - ref)).astype(BF16)
        v = v_ref[tile_order[k], pair_cols(p)]
        scores, incr = [], []
        for hh in range(2):
            ls = slice(hh * DK, (hh + 1) * DK)
            scores.append(_dot_nt(qa[:, ls], kb[:, ls]))
            kd_h = kd[:, ls]
            kd_blocks = jnp.concatenate(
                [jnp.concatenate([part for part in (jnp.zeros((c * CHUNK, DK), BF16), kd_h[rs],
                                                    jnp.zeros((TM - (c + 1) * CHUNK, DK), BF16))
                                  if part.shape[0]], axis=0)
                 for c, rs in enumerate(chunk_rows)], axis=1)
            incr.append(_dot_tn(v[:, ls], kd_blocks))
        return scores, incr, qe, jnp.exp(last), v

    def stage_output(k, p, scores, incr, qe, decay, v):
        base = tile_order[k].start
        for hh in range(2):
            h = 2 * p + hh
            ls = slice(hh * DK, (hh + 1) * DK)
            hs = slice(h * DK, (h + 1) * DK)
            o_intra = _dot(jnp.where(keep, scores[hh], 0.0).astype(BF16), v[:, ls])
            st = st_ref[h]
            if k > 0:
                st = jnp.where(is_prompt, 0.0, st)
            for c in order:
                rs = chunk_rows[c]
                o_ref[base + rs.start:base + rs.stop, hs] = (
                    o_intra[rs] + _dot_nt(qe[rs, ls], st.astype(BF16)))
                st = st * decay[c:c + 1, ls] + incr[hh][:, c * DK:(c + 1) * DK]
            st_ref[h] = st
            if k == 0:
                st_first_ref[h] = st
        if finish is not None:
            return (tile_order[k], p)

    def stage_finish(k, p, rows, pair):
        finish(rows, pair)

    units = [(k, p) for k in range(STEP_TILES) for p in range(n_pairs)]
    stages = [stage_decay, stage_scores, stage_output] + ([stage_finish] if finish is not None else [])
    carried = [dict() for _ in stages]
    for step in range(len(units) + len(stages) - 1):
        for depth, stage in (enumerate(stages) if reverse else reversed(list(enumerate(stages)))):
            u = step - depth
            if 0 <= u < len(units):
                out = stage(*units[u], *(carried[depth].pop(u) if depth else ()))
                if depth + 1 < len(stages):
                    carried[depth + 1][u] = out


def _store_prompt_states(block, st_ref, st_first_ref, snew_ref, *, reverse, sfwd_ref=None):
    first, second = (1, 0) if reverse else (0, 1)

    @pl.when(block < PROMPT_STEPS)
    def _():
        for h in range(HEADS):
            if sfwd_ref is None:
                snew_ref[first, h] = st_first_ref[h].T
                snew_ref[second, h] = st_ref[h].T
            else:
                snew_ref[first, 1, h] = st_first_ref[h].T
                snew_ref[second, 1, h] = st_ref[h].T
        if sfwd_ref is not None:
            snew_ref[:, 0] = sfwd_ref[...]


def _gla_fwd_kernel(q_ref, v_ref, f_ref, lbl_ref, s0_ref, o_ref, snew_ref, st_ref, st_first_ref, ma_ref):
    block = pl.program_id(0)
    _gla_step(block, q_ref, v_ref, f_ref, lbl_ref, s0_ref, st_ref, st_first_ref, ma_ref, o_ref,
              reverse=False)
    _store_prompt_states(block, st_ref, st_first_ref, snew_ref, reverse=False)


def _gla_bwd_kernel(q_ref, v_ref, f_ref, lbl_ref, s0_ref, of_ref, gg_ref, ng_ref, sfwd_ref,
                    a_ref, snew_ref, st_ref, st_first_ref, ma_ref, ob_ref):
    block = N_STEPS - 1 - pl.program_id(0)

    def finish(rows, p):
        for h in (2 * p, 2 * p + 1):
            hs = slice(h * DV, (h + 1) * DV)
            o = of_ref[rows, hs] + ob_ref[rows, hs]
            a_ref[rows, hs] = (_rmsnorm(o, ng_ref[...]) * _silu(gg_ref[rows, hs].astype(F32))).astype(BF16)

    _gla_step(block, q_ref, v_ref, f_ref, lbl_ref, s0_ref, st_ref, st_first_ref, ma_ref, ob_ref,
              reverse=True, finish=finish)
    _store_prompt_states(block, st_ref, st_first_ref, snew_ref, reverse=True, sfwd_ref=sfwd_ref)


def _state_specs(direction, block_of_step):
    sample = lambda i: jnp.clip((block_of_step(i) - PROMPT_STEPS) // STEPS_PER_SAMPLE, 0, N_SAMPLE_SEQ - 1)
    s0 = pl.BlockSpec((1, 1, HEADS, DK, DV), lambda i: (sample(i), direction, 0, 0, 0))
    snew = pl.BlockSpec((STEP_TILES, HEADS, DK, DV),
                        lambda i: (jnp.minimum(block_of_step(i), PROMPT_STEPS - 1), 0, 0, 0))
    return s0, snew


_GLA_STATE_SCRATCH = [pltpu.VMEM((HEADS, DV, DK), F32), pltpu.VMEM((HEADS, DV, DK), F32),
                      pltpu.VMEM((TM, TM), BF16)]


def _gla_fwd(q, v, ff, lb_logits, s0):
    s0_spec, snew_spec = _state_specs(0, lambda i: i)
    return pl.pallas_call(
        _gla_fwd_kernel,
        grid=(N_STEPS,),
        in_specs=[_STEP_SPEC, _STEP_SPEC, _STEP_SPEC,
                  pl.BlockSpec(lb_logits.shape, lambda i: (0, 0)), s0_spec],
        out_specs=[_STEP_SPEC, snew_spec],
        out_shape=[jax.ShapeDtypeStruct((T, D), F32),
                   jax.ShapeDtypeStruct((N_PROMPT_SEQ, HEADS, DK, DV), F32)],
        scratch_shapes=_GLA_STATE_SCRATCH,
        compiler_params=_cparams(),
        name="gla_fwd",
    )(q, v, ff, lb_logits, s0)


def _gla_bwd(q, v, fb, lb_logits, s0, o_f, gg, head_norm_g, s_fwd):
    rev = lambda i: N_STEPS - 1 - i
    block = pl.BlockSpec((STEP_ROWS, D), lambda i: (rev(i), 0))
    s0_spec, sfwd_spec = _state_specs(1, rev)
    prompt_block = lambda i: jnp.minimum(rev(i), PROMPT_STEPS - 1)
    return pl.pallas_call(
        _gla_bwd_kernel,
        grid=(N_STEPS,),
        in_specs=[block, block, block, pl.BlockSpec(lb_logits.shape, lambda i: (0, 0)), s0_spec,
                  block, block, pl.BlockSpec((1, DV), lambda i: (0, 0)), sfwd_spec],
        out_specs=[block, pl.BlockSpec((STEP_TILES, 2, HEADS, DK, DV), lambda i: (prompt_block(i), 0, 0, 0, 0))],
        out_shape=[jax.ShapeDtypeStruct((T, D), BF16),
                   jax.ShapeDtypeStruct((N_PROMPT_SEQ, 2, HEADS, DK, DV), F32)],
        scratch_shapes=_GLA_STATE_SCRATCH + [pltpu.VMEM((STEP_ROWS, D), F32)],
        compiler_params=_cparams(),
        name="gla_bwd",
    )(q, v, fb, lb_logits, s0, o_f, gg, head_norm_g, s_fwd)


def _conv_mixer_kernel(x_ref, mod_ref, g_ref, w_hbm, cw_ref, a_ref, w_ref, stage_ref, sem):
    @pl.when(pl.program_id(0) == 0)
    def _():
        _load_weight_bf16(w_hbm, w_ref, stage_ref, sem)

    seg = jnp.where(pl.program_id(0) < PROMPT_TILES // CONV_TILES, PROMPT_LEN, GRID_W)
    pos = lax.broadcasted_iota(jnp.int32, (TM, 1), 0) & (seg - 1)

    def project(rows):
        h = _modulate(x_ref[rows], g_ref[...], mod_ref[0, 0], mod_ref[0, 1]).astype(BF16)
        return [_dot(h, w_ref[:, k * D:(k + 1) * D]) for k in range(3)]

    def convolve(rows, bg, cg, x_in):
        u = cg * x_in
        prev = jnp.where(pos == 0, 0.0, pltpu.roll(u, 1, axis=0))
        nxt = jnp.where(pos == seg - 1, 0.0, pltpu.roll(u, TM - 1, axis=0))
        conv = cw_ref[0:1] * prev + cw_ref[1:2] * u + cw_ref[2:3] * nxt
        a_ref[rows] = (bg * conv).astype(BF16)

    tiles = [slice(s * TM, (s + 1) * TM) for s in range(CONV_TILES)]
    projected = [project(rows) for rows in tiles]
    for rows, parts in zip(tiles, projected):
        convolve(rows, *parts)


def _conv_mixer(x, mod, norm_g, w_in, conv_w):
    block = pl.BlockSpec((CONV_TILES * TM, D), lambda i: (i, 0))
    return pl.pallas_call(
        _conv_mixer_kernel,
        grid=(N_TILES // CONV_TILES,),
        in_specs=[block, pl.BlockSpec((1, 6, 1, D), lambda i: (_cond_of_tile(i * CONV_TILES), 0, 0, 0)),
                  _ROW_SPEC, _ANY_SPEC, pl.BlockSpec((3, D), lambda i: (0, 0))],
        out_specs=block,
        out_shape=jax.ShapeDtypeStruct((T, D), BF16),
        scratch_shapes=_weight_scratch(3),
        compiler_params=_cparams(),
        name="conv_mixer",
    )(x, mod, norm_g, w_in, conv_w)


ROUTE_ROWS = 32


def _first_member(vals, target):
    idx = jnp.full_like(target, float(EPG - 1))
    for j in range(EPG - 2, -1, -1):
        idx = jnp.where(vals[j] == target, float(j), idx)
    return idx


def _router_logits(h2, rwt_ref):
    h_hi, h_lo = _split_bf16(h2)
    w_hi, w_lo = _split_bf16(rwt_ref[...])
    return _dot_nt(w_hi, h_hi) + (_dot_nt(w_hi, h_lo) + _dot_nt(w_lo, h_hi))


def _select_experts(logits, rb_ref):
    scores = jax.nn.sigmoid(logits)
    sel = scores + rb_ref[...]
    member = [sel[j * N_GROUPS:(j + 1) * N_GROUPS] for j in range(EPG)]
    m1 = functools.reduce(jnp.maximum, member)
    i1 = _first_member(member, m1)
    rest = [jnp.where(i1 == float(j), -jnp.inf, member[j]) for j in range(EPG)]
    m2 = functools.reduce(jnp.maximum, rest)
    i2 = _first_member(rest, m2)
    group_score = m1 + m2

    best, grp, first, second = group_score[0:1], jnp.zeros((1, TM), F32), i1[0:1], i2[0:1]
    for g in range(1, N_GROUPS):
        upd = group_score[g:g + 1] > best
        best = jnp.where(upd, group_score[g:g + 1], best)
        grp = jnp.where(upd, float(g), grp)
        first = jnp.where(upd, i1[g:g + 1], first)
        second = jnp.where(upd, i2[g:g + 1], second)
    a = jnp.minimum(first, second)
    b = jnp.maximum(first, second)
    row = lax.broadcasted_iota(jnp.int32, (N_EXPERTS, TM), 0).astype(F32)
    s_lo = jnp.sum(jnp.where(row == a * N_GROUPS + grp, scores, 0.0), axis=0, keepdims=True)
    s_hi = jnp.sum(jnp.where(row == b * N_GROUPS + grp, scores, 0.0), axis=0, keepdims=True)
    tot = s_lo + s_hi
    pair = jnp.where(a == 0.0, jnp.where(b == 1.0, 0.0, jnp.where(b == 2.0, 3.0, 4.0)),
                     jnp.where(a == 1.0, jnp.where(b == 2.0, 5.0, 1.0), 2.0))
    return grp * N_PAIRS + pair, s_lo / tot, s_hi / tot


def _post_mixer_kernel(*refs, n_x):
    a_ref, x_refs = refs[0], refs[1:1 + n_x]
    (mod_ref, g2_ref, wo_hbm, rwt_ref, rb_ref,
     x1_ref, gates_ref, bucket_ref, rank_ref, cnt_ref, carry_ref, earlier_ref, wo_ref, stage_ref, sem) = refs[1 + n_x:]

    @pl.when(pl.program_id(0) == 0)
    def _():
        _load_weight_bf16(wo_hbm, wo_ref, stage_ref, sem)
        carry_ref[...] = jnp.zeros_like(carry_ref)
        s_idx = lax.broadcasted_iota(jnp.int32, (TM, TM), 0)
        t_idx = lax.broadcasted_iota(jnp.int32, (TM, TM), 1)
        earlier_ref[...] = (s_idx < t_idx).astype(BF16)

    def stage_mix(s):
        rows = slice(s * TM, (s + 1) * TM)
        x1 = (_token_tile(x_refs, rows, PROMPT_TILES // ROUTE_TILES)
              + mod_ref[0, 2] * _dot(a_ref[rows], wo_ref[...]))
        x1_ref[rows] = x1
        return _modulate(x1, g2_ref[...], mod_ref[0, 3], mod_ref[0, 4])

    def stage_rank(s, bucket, g_lo, g_hi):
        onehot = lax.broadcasted_iota(jnp.int32, (ROUTE_ROWS, TM), 0).astype(F32) == bucket
        before = _dot(onehot.astype(BF16), earlier_ref[...])
        oh = onehot.astype(F32)
        carry = carry_ref[...]
        rank = jnp.sum(oh * (before + carry[:, 0:1]), axis=0, keepdims=True)
        carry_ref[...] = carry + jnp.sum(oh, axis=1, keepdims=True)
        bucket_ref[s] = bucket.astype(jnp.int32)
        rank_ref[s] = rank.astype(jnp.int32)
        grow = lax.broadcasted_iota(jnp.int32, (LANES, TM), 0)
        gates_ref[s * TM:(s + 1) * TM] = jnp.where(grow == 0, g_lo, jnp.where(grow == 1, g_hi, 0.0)).T

    tiles = range(ROUTE_TILES)
    h2 = [stage_mix(s) for s in tiles]
    logits = [_router_logits(h2[s], rwt_ref) for s in tiles]
    picks = [_select_experts(logits[s], rb_ref) for s in tiles]
    for s in tiles:
        stage_rank(s, *picks[s])
    cnt_ref[...] = carry_ref[...]


def _post_mixer(a, x_parts, mod, norm_g2, w_out, router_wt, router_bt):
    rows = ROUTE_TILES * TM
    idx_spec = pl.BlockSpec((ROUTE_TILES, 1, TM), lambda i: (i, 0, 0))
    return pl.pallas_call(
        functools.partial(_post_mixer_kernel, n_x=len(x_parts)),
        grid=(N_TILES // ROUTE_TILES,),
        in_specs=[pl.BlockSpec((rows, D), lambda i: (i, 0))] + _token_specs(x_parts, ROUTE_TILES) + [
            pl.BlockSpec((1, 6, 1, D), lambda i: (_cond_of_tile(i * ROUTE_TILES), 0, 0, 0)), _ROW_SPEC,
            _ANY_SPEC,
            pl.BlockSpec((N_EXPERTS, D), lambda i: (0, 0)),
            pl.BlockSpec((N_EXPERTS, 1), lambda i: (0, 0))],
        out_specs=[pl.BlockSpec((rows, D), lambda i: (i, 0)), pl.BlockSpec((rows, LANES), lambda i: (i, 0)),
                   idx_spec, idx_spec, pl.BlockSpec((ROUTE_ROWS, LANES), lambda i: (0, 0))],
        out_shape=[jax.ShapeDtypeStruct((T, D), F32), jax.ShapeDtypeStruct((T, LANES), F32),
                   jax.ShapeDtypeStruct((N_TILES, 1, TM), jnp.int32),
                   jax.ShapeDtypeStruct((N_TILES, 1, TM), jnp.int32),
                   jax.ShapeDtypeStruct((ROUTE_ROWS, LANES), F32)],
        scratch_shapes=[pltpu.VMEM((ROUTE_ROWS, LANES), F32), pltpu.VMEM((TM, TM), BF16)] + _weight_scratch(1),
        compiler_params=_cparams(),
        name="post_mixer",
    )(a, *x_parts, mod, norm_g2, w_out, router_wt, router_bt)


_MOVE_SPEC = pl.BlockSpec((MOVE_ROWS, D), lambda i, *_: (i, 0))
_MOVE_MOD_SPEC = pl.BlockSpec((1, 6, 1, D), lambda i, *_: (_cond_of_tile(i * MOVE_TILES), 0, 0, 0))


def _smem_step_spec(step_of_step):
    return pl.BlockSpec((MOVE_TILES, 1, TM), lambda i, *_: (step_of_step(i), 0, 0), memory_space=pltpu.SMEM)


def _start_step_rows(make_copy):
    for tile in range(MOVE_TILES):
        for r in range(TM):
            make_copy(tile, r).start(priority=r % 2)


def _dispatch_kernel(zero_from_ref, slot_ref, x1_ref, mod_ref, g2_ref, gates_ref,
                     xs_ref, buf_ref, zero_ref, sems, zero_sem):
    i = pl.program_id(0)
    cur = i % 2

    @pl.when(i == 0)
    def _():
        zero_ref[...] = jnp.zeros_like(zero_ref)

        def for_each_chunk(act):
            def tile(j, carry):
                def chunk(c, inner):
                    row = pl.multiple_of(j * TM_E + c * ZERO_ROWS, ZERO_ROWS)
                    act(pltpu.make_async_copy(zero_ref, xs_ref.at[pl.ds(row, ZERO_ROWS)], zero_sem))
                    return inner
                return lax.fori_loop(zero_from_ref[j], TM_E // ZERO_ROWS, chunk, carry)
            lax.fori_loop(0, N_TILES_E, tile, 0)

        for_each_chunk(lambda copy: copy.start())
        for_each_chunk(lambda copy: copy.wait())

    def wait_step(b):
        pltpu.make_async_copy(buf_ref.at[b], xs_ref.at[pl.ds(0, MOVE_ROWS)], sems.at[b]).wait()

    h2 = _modulate(x1_ref[...], g2_ref[...], mod_ref[0, 3], mod_ref[0, 4])
    gates = gates_ref[...]

    def send(b):
        buf_ref[b, :, :D] = h2
        buf_ref[b, :, D:] = gates
        _start_step_rows(lambda tile, r: pltpu.make_async_copy(
            buf_ref.at[b, pl.ds(tile * TM + r, 1)], xs_ref.at[pl.ds(slot_ref[tile, 0, r], 1)], sems.at[b]))

        @pl.when(i > 0)
        def _():
            wait_step(1 - b)

        @pl.when(i == N_MOVE_STEPS - 1)
        def _():
            wait_step(b)

    for b in range(2):
        pl.when(cur == b)(functools.partial(send, b))


def _dispatch(zero_from, slot, x1, mod, norm_g2, gates):
    return pl.pallas_call(
        _dispatch_kernel,
        grid_spec=pltpu.PrefetchScalarGridSpec(
            num_scalar_prefetch=1,
            grid=(N_MOVE_STEPS,),
            in_specs=[_smem_step_spec(lambda i: i), _MOVE_SPEC, _MOVE_MOD_SPEC,
                      pl.BlockSpec((1, D), lambda i, *_: (0, 0)),
                      pl.BlockSpec((MOVE_ROWS, LANES), lambda i, *_: (i, 0))],
            out_specs=pl.BlockSpec(memory_space=pl.ANY),
            scratch_shapes=[pltpu.VMEM((2, MOVE_ROWS, XW), F32), pltpu.VMEM((ZERO_ROWS, XW), F32),
                            pltpu.SemaphoreType.DMA((2,)), pltpu.SemaphoreType.DMA(())],
        ),
        out_shape=jax.ShapeDtypeStruct((S_ROWS, XW), F32),
        compiler_params=_cparams(),
        name="moe_dispatch",
    )(zero_from, slot, x1, mod, norm_g2, gates)


_W_SHAPES = ((D, D_FF), (D, D_FF), (D_FF, D))


def _expert_kernel(ea_ref, eb_ref, next_a_ref, next_b_ref, a_higher_ref, nrows_ref, xblock_ref,
                   x_ref, wg_hbm, wu_hbm, wd_hbm, y_ref, *scratch, layer):
    stage, w16, sems = scratch[0:3], scratch[3:6], scratch[6]
    j = pl.program_id(0)
    n = nrows_ref[j]
    prev = jnp.maximum(j - 1, 0)

    def fetch(slot, expert):
        return [pltpu.make_async_copy(w.at[layer, expert], s.at[slot], sems.at[slot])
                for w, s in zip((wg_hbm, wu_hbm, wd_hbm), stage)]

    for slot, e_ref, next_ref in ((0, ea_ref, next_a_ref), (1, eb_ref, next_b_ref)):
        @pl.when(j == 0)
        def _(slot=slot, e_ref=e_ref):
            for copy in fetch(slot, e_ref[0]):
                copy.start()

        @pl.when(jnp.logical_or(j == 0, e_ref[j] != e_ref[prev]))
        def _(slot=slot, e_ref=e_ref, next_ref=next_ref):
            for copy in fetch(slot, e_ref[j]):
                copy.wait()
            for s, d in zip(stage, w16):
                d[slot] = s[slot].astype(BF16)

            @pl.when(next_ref[j] >= 0)
            def _():
                for copy in fetch(slot, next_ref[j]):
                    copy.start()

    def run(rows):
        valid = lax.broadcasted_iota(jnp.int32, (rows, 1), 0) < n
        xe = jnp.where(valid, x_ref[:rows], 0.0)
        x = xe[:, :D].astype(BF16)
        g_lo, g_hi = xe[:, D:D + 1], xe[:, D + 1:D + 2]
        a_higher = a_higher_ref[j] != 0

        def ffn(slot, gate):
            wg_ref, wu_ref, wd_ref = w16
            act = _silu(_dot(x, wg_ref[slot])) * _dot(x, wu_ref[slot]) * gate
            return _dot(act.astype(BF16), wd_ref[slot])

        y_ref[:rows] = (ffn(0, jnp.where(a_higher, g_hi, g_lo))
                        + ffn(1, jnp.where(a_higher, g_lo, g_hi)))
        if rows < TM_E:
            y_ref[rows:] = jnp.zeros((TM_E - rows, D), F32)

    half = TM_E // 2
    pl.when(n > half)(functools.partial(run, TM_E))
    pl.when(jnp.logical_and(n > 0, n <= half))(functools.partial(run, half))

    @pl.when(n == 0)
    def _():
        y_ref[...] = jnp.zeros_like(y_ref)


def _experts(tile_ea, tile_eb, tile_next_a, tile_next_b, tile_a_higher, tile_rows, tile_xblock, xs,
             w_gate, w_up, w_down, layer):
    any_spec = pl.BlockSpec(memory_space=pl.ANY)
    return pl.pallas_call(
        functools.partial(_expert_kernel, layer=layer),
        grid_spec=pltpu.PrefetchScalarGridSpec(
            num_scalar_prefetch=7,
            grid=(N_TILES_E,),
            in_specs=[pl.BlockSpec((TM_E, XW), lambda j, *plan: (plan[-1][j], 0)), any_spec, any_spec, any_spec],
            out_specs=pl.BlockSpec((TM_E, D), lambda j, *_: (j, 0)),
            scratch_shapes=[pltpu.VMEM((2,) + s, F32) for s in _W_SHAPES]
                           + [pltpu.VMEM((2,) + s, BF16) for s in _W_SHAPES]
                           + [pltpu.SemaphoreType.DMA((2,))],
        ),
        out_shape=jax.ShapeDtypeStruct((S_ROWS, D), F32),
        compiler_params=_cparams(),
        name="moe_experts",
    )(tile_ea, tile_eb, tile_next_a, tile_next_b, tile_a_higher, tile_rows, tile_xblock, xs,
      w_gate, w_up, w_down)


def _combine_kernel(slot_ref, next_slot_ref, x1_ref, mod_ref, gf_ref, y_ref, *rest, final):
    out_refs, (buf_ref, sems) = rest[:-2], rest[-2:]
    i = pl.program_id(0)
    cur = i % 2

    def gather(slots, b):
        _start_step_rows(lambda tile, r: pltpu.make_async_copy(
            y_ref.at[pl.ds(slots[tile, 0, r], 1)], buf_ref.at[b, pl.ds(tile * TM + r, 1)], sems.at[b]))

    @pl.when(i == 0)
    def _():
        gather(slot_ref, 0)

    def combine(b):
        @pl.when(i + 1 < N_MOVE_STEPS)
        def _():
            gather(next_slot_ref, 1 - b)

        pltpu.make_async_copy(y_ref.at[pl.ds(0, MOVE_ROWS)], buf_ref.at[b], sems.at[b]).wait()
        x2 = x1_ref[...] + mod_ref[0, 5] * buf_ref[b]
        if not final:
            out_refs[0][...] = x2
            return
        y = _rmsnorm(x2, gf_ref[...])
        yp_ref, ys_ref = out_refs

        @pl.when(i < T_PROMPT // MOVE_ROWS)
        def _():
            yp_ref[...] = y

        @pl.when(i >= T_PROMPT // MOVE_ROWS)
        def _():
            ys_ref[...] = y

    for b in range(2):
        pl.when(cur == b)(functools.partial(combine, b))


def _combine(slot, x1, mod, final_g, y_sorted, *, final):
    if final:
        out_specs = _token_specs((None, None), MOVE_TILES)
        out_shape = [jax.ShapeDtypeStruct((T_PROMPT, D), F32), jax.ShapeDtypeStruct((T_SAMPLE, D), F32)]
    else:
        out_specs = [_MOVE_SPEC]
        out_shape = [jax.ShapeDtypeStruct((T, D), F32)]
    return pl.pallas_call(
        functools.partial(_combine_kernel, final=final),
        grid=(N_MOVE_STEPS,),
        in_specs=[_smem_step_spec(lambda i: i), _smem_step_spec(lambda i: jnp.minimum(i + 1, N_MOVE_STEPS - 1)),
                  _MOVE_SPEC, _MOVE_MOD_SPEC, _ROW_SPEC, pl.BlockSpec(memory_space=pl.ANY)],
        out_specs=out_specs,
        scratch_shapes=[pltpu.VMEM((2, MOVE_ROWS, D), F32), pltpu.SemaphoreType.DMA((2,))],
        out_shape=out_shape,
        compiler_params=_cparams(),
        name="moe_combine_final" if final else "moe_combine",
    )(slot, slot, x1, mod, final_g, y_sorted)


def _plan_kernel(cnt_ref, bucket_ref, rank_ref, slot_ref,
                 ea_ref, eb_ref, next_a_ref, next_b_ref, a_higher_ref, rows_ref, xblock_ref, zero_from_ref):
    slot = rank_ref[...]
    bucket = bucket_ref[...]
    tile0 = jnp.int32(0)
    last = [jnp.int32(0)] * 3
    for b in range(N_BUCKETS):
        n = cnt_ref[b]
        tiles = (n + (TM_E - 1)) // TM_E
        pair = b % N_PAIRS
        per_tile = ((b // N_PAIRS) * EPG + _SLOT_A[pair], (b // N_PAIRS) * EPG + _SLOT_B[pair],
                    _SLOT_A_IS_HIGHER[pair])

        def fill(k, carry, n=n, tile0=tile0, per_tile=per_tile):
            j = tile0 + k
            live = jnp.minimum(n - k * TM_E, TM_E)
            rows_ref[j] = live
            xblock_ref[j] = j
            zero_from_ref[j] = live // ZERO_ROWS
            for ref, value in zip((ea_ref, eb_ref, a_higher_ref), per_tile):
                ref[j] = jnp.int32(value)
            return carry

        lax.fori_loop(0, tiles, fill, 0)
        slot = slot + jnp.where(bucket == b, tile0 * TM_E, 0)
        last = [jnp.where(tiles > 0, value, old) for value, old in zip(per_tile, last)]
        tile0 = tile0 + tiles

    def unused(j, carry):
        rows_ref[j] = jnp.int32(0)
        xblock_ref[j] = jnp.maximum(tile0 - 1, 0)
        zero_from_ref[j] = jnp.int32(0)
        for ref, value in zip((ea_ref, eb_ref, a_higher_ref), last):
            ref[j] = value
        return carry

    lax.fori_loop(tile0, N_TILES_E, unused, 0)

    for e_ref, next_ref in ((ea_ref, next_a_ref), (eb_ref, next_b_ref)):
        next_ref[N_TILES_E - 1] = jnp.int32(-1)

        def backward(k, carry, e_ref=e_ref, next_ref=next_ref):
            j = N_TILES_E - 2 - k
            next_ref[j] = jnp.where(e_ref[j] == e_ref[j + 1], next_ref[j + 1], e_ref[j + 1])
            return carry

        lax.fori_loop(0, N_TILES_E - 1, backward, 0)
    slot_ref[...] = slot


def _plan(counts, bucket, rank):
    whole = pl.BlockSpec(bucket.shape, lambda i, *_: (0, 0, 0))
    smem = pl.BlockSpec(memory_space=pltpu.SMEM)
    per_tile = jax.ShapeDtypeStruct((N_TILES_E,), jnp.int32)
    slot, *tile_plan = pl.pallas_call(
        _plan_kernel,
        grid_spec=pltpu.PrefetchScalarGridSpec(
            num_scalar_prefetch=1, grid=(1,), in_specs=[whole, whole], out_specs=[whole] + [smem] * 8),
        out_shape=[jax.ShapeDtypeStruct(bucket.shape, jnp.int32)] + [per_tile] * 8,
        compiler_params=_cparams(),
        name="moe_plan",
    )(counts[:, 0].astype(jnp.int32), bucket, rank)
    return slot, tile_plan


def _moe(a, x_parts, mod, norm_g2, w_out, router_wt, router_bt, w_gate, w_up, w_down, layer, final_g,
         *, final):
    x1, gates, bucket, rank, counts = _post_mixer(a, x_parts, mod, norm_g2, w_out, router_wt, router_bt)
    slot, (*expert_plan, zero_from) = _plan(counts, bucket, rank)
    xs = _dispatch(zero_from, slot, x1, mod, norm_g2, gates)
    ys = _experts(*expert_plan, xs, w_gate, w_up, w_down, layer)
    return _combine(slot, x1, mod, final_g, ys, final=final)


def kernel(x_prompt, x_sample, state_hgrn, c, c_ctx, w_mod, b_mod, norm_mix_g, norm_ffn_g, norm_final_g,
           hgrn_w_in, hgrn_lb_logits, hgrn_norm_g, hgrn_w_out, conv_w_in, conv_w, conv_w_out,
           router_w, router_b, moe_w_gate, moe_w_up, moe_w_down):
    x_parts = (x_prompt.reshape(T_PROMPT, D), x_sample.reshape(T_SAMPLE, D))
    cond = jnp.concatenate([c_ctx[None], c, jnp.zeros((COND_ROWS - N_COND, D), F32)], axis=0)
    mods = _adaln(cond, w_mod, b_mod)[:, :N_COND].reshape(2, N_COND, 6, 1, D)
    member_major = lambda w: w.reshape(N_GROUPS, EPG, -1).transpose(1, 0, 2).reshape(N_EXPERTS, -1)
    rwt = member_major(router_w.T)
    rbt = member_major(router_b.reshape(N_EXPERTS, 1))
    final_g = norm_final_g.reshape(1, D)

    q, v, ff, fb, gg = _hgrn_proj(x_parts, mods[0], norm_mix_g[0:1], hgrn_w_in[0])
    s0 = state_hgrn[:, 0]
    o_f, s_f = _gla_fwd(q, v, ff, hgrn_lb_logits, s0)
    a, new_state = _gla_bwd(q, v, fb, hgrn_lb_logits, s0, o_f, gg, hgrn_norm_g[0:1], s_f)
    (x,) = _moe(a, x_parts, mods[0], norm_ffn_g[0:1], hgrn_w_out[0], rwt, rbt,
                moe_w_gate, moe_w_up, moe_w_down, 0, final_g, final=False)

    a = _conv_mixer(x, mods[1], norm_mix_g[1:2], conv_w_in[0], conv_w[0])
    y_p, y_s = _moe(a, (x,), mods[1], norm_ffn_g[1:2], conv_w_out[0], rwt, rbt,
                    moe_w_gate, moe_w_up, moe_w_down, 1, final_g, final=True)

    return (y_p.reshape(N_PROMPT_SEQ, PROMPT_LEN, D), y_s.reshape(N_SAMPLE_SEQ, SAMPLE_LEN, D),
            new_state[:, None])
```

```python
import functools

import jax
import jax.numpy as jnp
from jax import lax
from jax.experimental import pallas as pl
from jax.experimental.pallas import tpu as pltpu

F32 = jnp.float32
BF16 = jnp.bfloat16

D = 1024
N_PROMPT_SEQ = 16
PROMPT_LEN = 256
N_SAMPLE_SEQ = 2
SAMPLE_LEN = 4096
GRID_W = 64
T_PROMPT = N_PROMPT_SEQ * PROMPT_LEN
T_SAMPLE = N_SAMPLE_SEQ * SAMPLE_LEN
T = T_PROMPT + T_SAMPLE
N_COND = 1 + N_SAMPLE_SEQ
COND_ROWS = 8

HEADS = 8
DK = 128
DV = 128
CHUNK = 64
N_EXPERTS = 16
N_GROUPS = 4
EPG = 4
N_PAIRS = 6
N_BUCKETS = N_GROUPS * N_PAIRS
D_FF = 512
EPS = 1e-6

TM = 256
N_TILES = T // TM
PROMPT_TILES = T_PROMPT // TM
TILES_PER_SAMPLE = SAMPLE_LEN // TM
CHUNKS_PER_TILE = TM // CHUNK
STEP_TILES = 2
STEP_ROWS = STEP_TILES * TM
N_STEPS = N_TILES // STEP_TILES
PROMPT_STEPS = PROMPT_TILES // STEP_TILES
MOVE_TILES = 2
MOVE_ROWS = MOVE_TILES * TM
N_MOVE_STEPS = N_TILES // MOVE_TILES
ROUTE_TILES = 4
CONV_TILES = 4
TM_E = 288
N_TILES_E = T // TM_E + N_BUCKETS
S_ROWS = N_TILES_E * TM_E
ZERO_ROWS = 32
XW = D + 128
LANES = 128
MOD_TN = 1536

V7X_VMEM_BYTES = 64 * 1024 * 1024
VMEM_LIMIT = V7X_VMEM_BYTES * 7 // 8

_SLOT_A = (0, 3, 3, 0, 0, 1)
_SLOT_B = (1, 1, 2, 2, 3, 2)
_SLOT_A_IS_HIGHER = tuple(int(a > b) for a, b in zip(_SLOT_A, _SLOT_B))


def _cparams():
    return pltpu.CompilerParams(dimension_semantics=("arbitrary",), vmem_limit_bytes=VMEM_LIMIT)


def _cond_of_tile(t):
    return jnp.where(t < PROMPT_TILES, 0, 1 + (t - PROMPT_TILES) // TILES_PER_SAMPLE)


def _rmsnorm(x, g):
    ms = jnp.mean(x * x, axis=-1, keepdims=True)
    return (x * lax.rsqrt(ms + EPS)) * g


def _modulate(x, g, shift, scale):
    return _rmsnorm(x, g) * (1.0 + scale) + shift


def _silu(x):
    return x * jax.nn.sigmoid(x)


def _dot(a, b):
    return jnp.dot(a, b, preferred_element_type=F32)


def _dot_nt(a, b):
    return lax.dot_general(a, b, (((1,), (1,)), ((), ())), preferred_element_type=F32)


def _dot_tn(a, b):
    return lax.dot_general(a, b, (((0,), (0,)), ((), ())), preferred_element_type=F32)


def _split_bf16(x):
    hi = x.astype(BF16)
    lo = (x - hi.astype(F32)).astype(BF16)
    return hi, lo


def _mod_kernel(cond_ref, w_ref, b_ref, o_ref):
    s = _silu(cond_ref[...])
    o_ref[0] = _dot(s.astype(BF16), w_ref[0].astype(BF16)) + b_ref[0]


def _adaln(cond, w_mod, b_mod):
    depth = w_mod.shape[0]
    return pl.pallas_call(
        _mod_kernel,
        grid=(depth, 6 * D // MOD_TN),
        in_specs=[
            pl.BlockSpec((COND_ROWS, D), lambda i, j: (0, 0)),
            pl.BlockSpec((1, D, MOD_TN), lambda i, j: (i, 0, j)),
            pl.BlockSpec((1, 1, MOD_TN), lambda i, j: (i, 0, j)),
        ],
        out_specs=pl.BlockSpec((1, COND_ROWS, MOD_TN), lambda i, j: (i, 0, j)),
        out_shape=jax.ShapeDtypeStruct((depth, COND_ROWS, 6 * D), F32),
        compiler_params=pltpu.CompilerParams(
            dimension_semantics=("arbitrary", "arbitrary"), vmem_limit_bytes=VMEM_LIMIT),
        name="adaln",
    )(cond, w_mod, b_mod.reshape(depth, 1, 6 * D))


_ROW_SPEC = pl.BlockSpec((1, D), lambda i: (0, 0))
_STEP_MOD_SPEC = pl.BlockSpec((1, 6, 1, D), lambda i, *_: (_cond_of_tile(i * STEP_TILES), 0, 0, 0))
_STEP_SPEC = pl.BlockSpec((STEP_ROWS, D), lambda i, *_: (i, 0))
_SUB_ROWS = tuple(slice(s * TM, (s + 1) * TM) for s in range(STEP_TILES))


def _load_weight_bf16(w_hbm, w16_ref, stage_ref, sem):
    n = w16_ref.shape[1] // D

    def slab(c):
        return pltpu.make_async_copy(w_hbm.at[:, pl.ds(c * D, D)], stage_ref.at[c % 2], sem.at[c % 2])

    slab(0).start()
    for c in range(n):
        if c + 1 < n:
            slab(c + 1).start()
        slab(c).wait()
        w16_ref[:, c * D:(c + 1) * D] = stage_ref[c % 2].astype(BF16)


def _weight_scratch(n_slabs):
    n_stage = min(n_slabs, 2)
    return [pltpu.VMEM((D, n_slabs * D), BF16), pltpu.VMEM((n_stage, D, D), F32),
            pltpu.SemaphoreType.DMA((n_stage,))]


_ANY_SPEC = pl.BlockSpec(memory_space=pl.ANY)


def _token_tile(x_refs, rows=slice(None), prompt_steps=PROMPT_TILES):
    if len(x_refs) == 1:
        return x_refs[0][rows]
    return jnp.where(pl.program_id(0) < prompt_steps, x_refs[0][rows], x_refs[1][rows])


def _token_specs(x_parts, tiles_per_step=1):
    rows = tiles_per_step * TM
    if len(x_parts) == 1:
        return [pl.BlockSpec((rows, D), lambda i: (i, 0))]
    prompt_steps = T_PROMPT // rows
    return [pl.BlockSpec((rows, D), lambda i: (jnp.minimum(i, prompt_steps - 1), 0)),
            pl.BlockSpec((rows, D), lambda i: (jnp.maximum(i - prompt_steps, 0), 0))]


def _hgrn_proj_kernel(xp_ref, xs_ref, mod_ref, g_ref, w_hbm, q_ref, v_ref, ff_ref, fb_ref, gg_ref,
                      w_ref, stage_ref, sem):
    @pl.when(pl.program_id(0) == 0)
    def _():
        _load_weight_bf16(w_hbm, w_ref, stage_ref, sem)

    h = [_modulate(_token_tile((xp_ref, xs_ref), rows, PROMPT_STEPS),
                   g_ref[...], mod_ref[0, 0], mod_ref[0, 1]).astype(BF16) for rows in _SUB_ROWS]
    for rows, h_tile in zip(_SUB_ROWS, h):
        q_ref[rows] = (_dot(h_tile, w_ref[:, 0 * D:1 * D]) * (DK ** -0.5)).astype(BF16)
        v_ref[rows] = _dot(h_tile, w_ref[:, 1 * D:2 * D]).astype(BF16)
        ff_ref[rows] = _dot(h_tile, w_ref[:, 2 * D:3 * D])
        fb_ref[rows] = _dot(h_tile, w_ref[:, 3 * D:4 * D])
        gg_ref[rows] = _dot(h_tile, w_ref[:, 4 * D:5 * D]).astype(BF16)


def _hgrn_proj(x_parts, mod, norm_g, w_in):
    bf = jax.ShapeDtypeStruct((T, D), BF16)
    f32 = jax.ShapeDtypeStruct((T, D), F32)
    return pl.pallas_call(
        _hgrn_proj_kernel,
        grid=(N_STEPS,),
        in_specs=_token_specs(x_parts, STEP_TILES) + [_STEP_MOD_SPEC, _ROW_SPEC, _ANY_SPEC],
        out_specs=[_STEP_SPEC] * 5,
        out_shape=[bf, bf, f32, f32, bf],
        scratch_shapes=_weight_scratch(5),
        compiler_params=_cparams(),
        name="hgrn_proj",
    )(*x_parts, mod, norm_g, w_in)


GLA_TILES = 4
GLA_ROWS = GLA_TILES * TM
N_GLA_STEPS = N_TILES // GLA_TILES
GLA_PROMPT_STEPS = PROMPT_TILES // GLA_TILES
STEPS_PER_SAMPLE = SAMPLE_LEN // GLA_ROWS
_GLA_SPEC = pl.BlockSpec((GLA_ROWS, D), lambda i: (i, 0))
_GLA_TILE_ROWS = tuple(slice(s * TM, (s + 1) * TM) for s in range(GLA_TILES))


def _gla_step(block, q_ref, v_ref, f_ref, lbl_ref, s0_ref, st_ref, st_tiles_ref, ma_ref, o_ref,
              *, reverse, finish=None):
    is_prompt = block < GLA_PROMPT_STEPS
    rel = block - GLA_PROMPT_STEPS
    edge = (STEPS_PER_SAMPLE - 1) if reverse else 0
    sample_start = jnp.logical_and(rel >= 0, rel % STEPS_PER_SAMPLE == edge)

    @pl.when(is_prompt)
    def _():
        st_ref[...] = jnp.zeros_like(st_ref)

    @pl.when(sample_start)
    def _():
        for h in range(HEADS):
            st_ref[h] = s0_ref[0, 0, h].T

    logits = lbl_ref[...]
    e = jnp.exp(logits - jnp.max(logits, axis=0, keepdims=True))
    lb = e[0:1] / jnp.sum(e, axis=0, keepdims=True)

    ref_idx = (CHUNK - 1 - CHUNK // 2) if reverse else CHUNK // 2
    last_idx = 0 if reverse else CHUNK - 1
    shift = CHUNK.bit_length() - 1

    row = lax.broadcasted_iota(jnp.int32, (TM, TM), 0)
    col = lax.broadcasted_iota(jnp.int32, (TM, TM), 1)
    same = (row >> shift) == (col >> shift)
    ref_row = ((row >> shift) << shift) + ref_idx
    keep = jnp.logical_and(same, (row <= col) if reverse else (row >= col))

    @pl.when(pl.program_id(0) == 0)
    def _():
        at_ref = jnp.logical_and(same, (ref_row <= col) if reverse else (ref_row >= col))
        ma_ref[...] = (keep.astype(F32) - at_ref.astype(F32)).astype(BF16)

    ma = ma_ref[...]

    rr = lax.broadcasted_iota(jnp.int32, (2 * CHUNKS_PER_TILE, TM), 0)
    cc = lax.broadcasted_iota(jnp.int32, (2 * CHUNKS_PER_TILE, TM), 1)
    chunk = rr & (CHUNKS_PER_TILE - 1)
    target = chunk * CHUNK + jnp.where(rr < CHUNKS_PER_TILE, ref_idx, last_idx)
    covered = (cc >= target) if reverse else (cc <= target)
    rm = jnp.logical_and((cc >> shift) == chunk, covered).astype(BF16)

    chunk_rows = [slice(c * CHUNK, (c + 1) * CHUNK) for c in range(CHUNKS_PER_TILE)]
    order = range(CHUNKS_PER_TILE - 1, -1, -1) if reverse else range(CHUNKS_PER_TILE)
    pair_w = 2 * DK
    n_pairs = HEADS // 2
    tile_order = _GLA_TILE_ROWS[::-1] if reverse else _GLA_TILE_ROWS

    def per_chunk_scale(x, scale):
        return jnp.concatenate([x[rs] * scale[c:c + 1] for c, rs in enumerate(chunk_rows)], axis=0)

    def pair_cols(p):
        return slice(p * pair_w, (p + 1) * pair_w)

    def stage_decay(k, p):
        lbp = lb[:, pair_cols(p)]
        f = lbp + (1.0 - lbp) * jax.nn.sigmoid(f_ref[tile_order[k], pair_cols(p)])
        lf_hi, lf_lo = _split_bf16(jnp.log(f))
        z = _dot(ma, lf_hi) + _dot(ma, lf_lo)
        marks = _dot(rm, lf_hi) + _dot(rm, lf_lo)
        return 1.0 - f, z, marks

    def stage_scores(k, p, kk, z, marks):
        ref, last = marks[:CHUNKS_PER_TILE], marks[CHUNKS_PER_TILE:]
        qa32 = q_ref[tile_order[k], pair_cols(p)].astype(F32) * jnp.exp(z)
        kb32 = kk * jnp.exp(-z)
        qa = qa32.astype(BF16)
        kb = kb32.astype(BF16)
        qe = per_chunk_scale(qa32, jnp.exp(ref)).astype(BF16)
        kd = per_chunk_scale(kb32, jnp.exp(last - ref)).astype(BF16)
        v = v_ref[tile_order[k], pair_cols(p)]
        scores, incr = [], []
        for hh in range(2):
            ls = slice(hh * DK, (hh + 1) * DK)
            scores.append(_dot_nt(qa[:, ls], kb[:, ls]))
            kd_h = kd[:, ls]
            kd_blocks = jnp.concatenate(
                [jnp.concatenate([part for part in (jnp.zeros((c * CHUNK, DK), BF16), kd_h[rs],
                                                    jnp.zeros((TM - (c + 1) * CHUNK, DK), BF16))
                                  if part.shape[0]], axis=0)
                 for c, rs in enumerate(chunk_rows)], axis=1)
            incr.append(_dot_tn(v[:, ls], kd_blocks))
        return scores, incr, qe, jnp.exp(last), v

    def stage_output(k, p, scores, incr, qe, decay, v):
        base = tile_order[k].start
        for hh in range(2):
            h = 2 * p + hh
            ls = slice(hh * DK, (hh + 1) * DK)
            hs = slice(h * DK, (h + 1) * DK)
            o_intra = _dot(jnp.where(keep, scores[hh], 0.0).astype(BF16), v[:, ls])
            st = st_ref[h]
            if k > 0:
                st = jnp.where(is_prompt, 0.0, st)
            for c in order:
                rs = chunk_rows[c]
                o_ref[base + rs.start:base + rs.stop, hs] = (
                    o_intra[rs] + _dot_nt(qe[rs, ls], st.astype(BF16)))
                st = st * decay[c:c + 1, ls] + incr[hh][:, c * DK:(c + 1) * DK]
            st_ref[h] = st
            if k < GLA_TILES - 1:
                st_tiles_ref[k, h] = st
        if finish is not None:
            return (tile_order[k], p)

    def stage_finish(k, p, rows, pair):
        finish(rows, pair)

    units = [(k, p) for k in range(GLA_TILES) for p in range(n_pairs)]
    stages = [stage_decay, stage_scores, stage_output] + ([stage_finish] if finish is not None else [])
    carried = [dict() for _ in stages]
    for step in range(len(units) + len(stages) - 1):
        for depth, stage in (enumerate(stages) if reverse else reversed(list(enumerate(stages)))):
            u = step - depth
            if 0 <= u < len(units):
                out = stage(*units[u], *(carried[depth].pop(u) if depth else ()))
                if depth + 1 < len(stages):
                    carried[depth + 1][u] = out


def _store_prompt_states(block, st_ref, st_tiles_ref, snew_ref, *, reverse, sfwd_ref=None):
    @pl.when(block < GLA_PROMPT_STEPS)
    def _():
        for k in range(GLA_TILES):
            tile = GLA_TILES - 1 - k if reverse else k
            for h in range(HEADS):
                state = (st_tiles_ref[k, h] if k < GLA_TILES - 1 else st_ref[h]).T
                if sfwd_ref is None:
                    snew_ref[tile, h] = state
                else:
                    snew_ref[tile, 1, h] = state
        if sfwd_ref is not None:
            snew_ref[:, 0] = sfwd_ref[...]


def _gla_fwd_kernel(q_ref, v_ref, f_ref, lbl_ref, s0_ref, o_ref, snew_ref, st_ref, st_tiles_ref, ma_ref):
    block = pl.program_id(0)
    _gla_step(block, q_ref, v_ref, f_ref, lbl_ref, s0_ref, st_ref, st_tiles_ref, ma_ref, o_ref,
              reverse=False)
    _store_prompt_states(block, st_ref, st_tiles_ref, snew_ref, reverse=False)


def _gla_bwd_kernel(q_ref, v_ref, f_ref, lbl_ref, s0_ref, of_ref, gg_ref, ng_ref, sfwd_ref,
                    a_ref, snew_ref, st_ref, st_tiles_ref, ma_ref, ob_ref):
    block = N_GLA_STEPS - 1 - pl.program_id(0)

    def finish(rows, p):
        for h in (2 * p, 2 * p + 1):
            hs = slice(h * DV, (h + 1) * DV)
            o = of_ref[rows, hs] + ob_ref[rows, hs]
            a_ref[rows, hs] = (_rmsnorm(o, ng_ref[...]) * _silu(gg_ref[rows, hs].astype(F32))).astype(BF16)

    _gla_step(block, q_ref, v_ref, f_ref, lbl_ref, s0_ref, st_ref, st_tiles_ref, ma_ref, ob_ref,
              reverse=True, finish=finish)
    _store_prompt_states(block, st_ref, st_tiles_ref, snew_ref, reverse=True, sfwd_ref=sfwd_ref)


def _state_specs(direction, block_of_step):
    sample = lambda i: jnp.clip((block_of_step(i) - GLA_PROMPT_STEPS) // STEPS_PER_SAMPLE, 0, N_SAMPLE_SEQ - 1)
    s0 = pl.BlockSpec((1, 1, HEADS, DK, DV), lambda i: (sample(i), direction, 0, 0, 0))
    snew = pl.BlockSpec((GLA_TILES, HEADS, DK, DV),
                        lambda i: (jnp.minimum(block_of_step(i), GLA_PROMPT_STEPS - 1), 0, 0, 0))
    return s0, snew


_GLA_STATE_SCRATCH = [pltpu.VMEM((HEADS, DV, DK), F32), pltpu.VMEM((GLA_TILES - 1, HEADS, DV, DK), F32),
                      pltpu.VMEM((TM, TM), BF16)]


def _gla_fwd(q, v, ff, lb_logits, s0):
    s0_spec, snew_spec = _state_specs(0, lambda i: i)
    return pl.pallas_call(
        _gla_fwd_kernel,
        grid=(N_GLA_STEPS,),
        in_specs=[_GLA_SPEC, _GLA_SPEC, _GLA_SPEC,
                  pl.BlockSpec(lb_logits.shape, lambda i: (0, 0)), s0_spec],
        out_specs=[_GLA_SPEC, snew_spec],
        out_shape=[jax.ShapeDtypeStruct((T, D), F32),
                   jax.ShapeDtypeStruct((N_PROMPT_SEQ, HEADS, DK, DV), F32)],
        scratch_shapes=_GLA_STATE_SCRATCH,
        compiler_params=_cparams(),
        name="gla_fwd",
    )(q, v, ff, lb_logits, s0)


def _gla_bwd(q, v, fb, lb_logits, s0, o_f, gg, head_norm_g, s_fwd):
    rev = lambda i: N_GLA_STEPS - 1 - i
    block = pl.BlockSpec((GLA_ROWS, D), lambda i: (rev(i), 0))
    s0_spec, sfwd_spec = _state_specs(1, rev)
    prompt_block = lambda i: jnp.minimum(rev(i), GLA_PROMPT_STEPS - 1)
    return pl.pallas_call(
        _gla_bwd_kernel,
        grid=(N_GLA_STEPS,),
        in_specs=[block, block, block, pl.BlockSpec(lb_logits.shape, lambda i: (0, 0)), s0_spec,
                  block, block, pl.BlockSpec((1, DV), lambda i: (0, 0)), sfwd_spec],
        out_specs=[block, pl.BlockSpec((GLA_TILES, 2, HEADS, DK, DV), lambda i: (prompt_block(i), 0, 0, 0, 0))],
        out_shape=[jax.ShapeDtypeStruct((T, D), BF16),
                   jax.ShapeDtypeStruct((N_PROMPT_SEQ, 2, HEADS, DK, DV), F32)],
        scratch_shapes=_GLA_STATE_SCRATCH + [pltpu.VMEM((GLA_ROWS, D), F32)],
        compiler_params=_cparams(),
        name="gla_bwd",
    )(q, v, fb, lb_logits, s0, o_f, gg, head_norm_g, s_fwd)


def _conv_mixer_kernel(x_ref, mod_ref, g_ref, w_hbm, cw_ref, a_ref, w_ref, stage_ref, sem):
    @pl.when(pl.program_id(0) == 0)
    def _():
        _load_weight_bf16(w_hbm, w_ref, stage_ref, sem)

    seg = jnp.where(pl.program_id(0) < PROMPT_TILES // CONV_TILES, PROMPT_LEN, GRID_W)
    pos = lax.broadcasted_iota(jnp.int32, (TM, 1), 0) & (seg - 1)

    def project(rows):
        h = _modulate(x_ref[rows], g_ref[...], mod_ref[0, 0], mod_ref[0, 1]).astype(BF16)
        return [_dot(h, w_ref[:, k * D:(k + 1) * D]) for k in range(3)]

    def convolve(rows, bg, cg, x_in):
        u = cg * x_in
        prev = jnp.where(pos == 0, 0.0, pltpu.roll(u, 1, axis=0))
        nxt = jnp.where(pos == seg - 1, 0.0, pltpu.roll(u, TM - 1, axis=0))
        conv = cw_ref[0:1] * prev + cw_ref[1:2] * u + cw_ref[2:3] * nxt
        a_ref[rows] = (bg * conv).astype(BF16)

    tiles = [slice(s * TM, (s + 1) * TM) for s in range(CONV_TILES)]
    projected = [project(rows) for rows in tiles]
    for rows, parts in zip(tiles, projected):
        convolve(rows, *parts)


def _conv_mixer(x, mod, norm_g, w_in, conv_w):
    block = pl.BlockSpec((CONV_TILES * TM, D), lambda i: (i, 0))
    return pl.pallas_call(
        _conv_mixer_kernel,
        grid=(N_TILES // CONV_TILES,),
        in_specs=[block, pl.BlockSpec((1, 6, 1, D), lambda i: (_cond_of_tile(i * CONV_TILES), 0, 0, 0)),
                  _ROW_SPEC, _ANY_SPEC, pl.BlockSpec((3, D), lambda i: (0, 0))],
        out_specs=block,
        out_shape=jax.ShapeDtypeStruct((T, D), BF16),
        scratch_shapes=_weight_scratch(3),
        compiler_params=_cparams(),
        name="conv_mixer",
    )(x, mod, norm_g, w_in, conv_w)


ROUTE_ROWS = 32


def _first_member(vals, target):
    idx = jnp.full_like(target, float(EPG - 1))
    for j in range(EPG - 2, -1, -1):
        idx = jnp.where(vals[j] == target, float(j), idx)
    return idx


def _router_logits(h2, rwt_ref):
    h_hi, h_lo = _split_bf16(h2)
    w_hi, w_lo = _split_bf16(rwt_ref[...])
    return _dot_nt(w_hi, h_hi) + (_dot_nt(w_hi, h_lo) + _dot_nt(w_lo, h_hi))


def _select_experts(logits, rb_ref):
    scores = jax.nn.sigmoid(logits)
    sel = scores + rb_ref[...]
    member = [sel[j * N_GROUPS:(j + 1) * N_GROUPS] for j in range(EPG)]
    m1 = functools.reduce(jnp.maximum, member)
    i1 = _first_member(member, m1)
    rest = [jnp.where(i1 == float(j), -jnp.inf, member[j]) for j in range(EPG)]
    m2 = functools.reduce(jnp.maximum, rest)
    i2 = _first_member(rest, m2)
    group_score = m1 + m2

    best, grp, first, second = group_score[0:1], jnp.zeros((1, TM), F32), i1[0:1], i2[0:1]
    for g in range(1, N_GROUPS):
        upd = group_score[g:g + 1] > best
        best = jnp.where(upd, group_score[g:g + 1], best)
        grp = jnp.where(upd, float(g), grp)
        first = jnp.where(upd, i1[g:g + 1], first)
        second = jnp.where(upd, i2[g:g + 1], second)
    a = jnp.minimum(first, second)
    b = jnp.maximum(first, second)
    row = lax.broadcasted_iota(jnp.int32, (N_EXPERTS, TM), 0).astype(F32)
    s_lo = jnp.sum(jnp.where(row == a * N_GROUPS + grp, scores, 0.0), axis=0, keepdims=True)
    s_hi = jnp.sum(jnp.where(row == b * N_GROUPS + grp, scores, 0.0), axis=0, keepdims=True)
    tot = s_lo + s_hi
    pair = jnp.where(a == 0.0, jnp.where(b == 1.0, 0.0, jnp.where(b == 2.0, 3.0, 4.0)),
                     jnp.where(a == 1.0, jnp.where(b == 2.0, 5.0, 1.0), 2.0))
    return grp * N_PAIRS + pair, s_lo / tot, s_hi / tot


def _post_mixer_kernel(*refs, n_x):
    a_ref, x_refs = refs[0], refs[1:1 + n_x]
    (mod_ref, g2_ref, wo_hbm, rwt_ref, rb_ref,
     x1_ref, gates_ref, bucket_ref, rank_ref, cnt_ref, carry_ref, earlier_ref, wo_ref, stage_ref, sem) = refs[1 + n_x:]

    @pl.when(pl.program_id(0) == 0)
    def _():
        _load_weight_bf16(wo_hbm, wo_ref, stage_ref, sem)
        carry_ref[...] = jnp.zeros_like(carry_ref)
        s_idx = lax.broadcasted_iota(jnp.int32, (TM, TM), 0)
        t_idx = lax.broadcasted_iota(jnp.int32, (TM, TM), 1)
        earlier_ref[...] = (s_idx < t_idx).astype(BF16)

    def stage_mix(s):
        rows = slice(s * TM, (s + 1) * TM)
        x1 = (_token_tile(x_refs, rows, PROMPT_TILES // ROUTE_TILES)
              + mod_ref[0, 2] * _dot(a_ref[rows], wo_ref[...]))
        x1_ref[rows] = x1
        return _modulate(x1, g2_ref[...], mod_ref[0, 3], mod_ref[0, 4])

    def stage_rank(s, bucket, g_lo, g_hi):
        onehot = lax.broadcasted_iota(jnp.int32, (ROUTE_ROWS, TM), 0).astype(F32) == bucket
        before = _dot(onehot.astype(BF16), earlier_ref[...])
        oh = onehot.astype(F32)
        carry = carry_ref[...]
        rank = jnp.sum(oh * (before + carry[:, 0:1]), axis=0, keepdims=True)
        carry_ref[...] = carry + jnp.sum(oh, axis=1, keepdims=True)
        bucket_ref[s] = bucket.astype(jnp.int32)
        rank_ref[s] = rank.astype(jnp.int32)
        grow = lax.broadcasted_iota(jnp.int32, (LANES, TM), 0)
        gates_ref[s * TM:(s + 1) * TM] = jnp.where(grow == 0, g_lo, jnp.where(grow == 1, g_hi, 0.0)).T

    tiles = range(ROUTE_TILES)
    h2 = [stage_mix(s) for s in tiles]
    logits = [_router_logits(h2[s], rwt_ref) for s in tiles]
    picks = [_select_experts(logits[s], rb_ref) for s in tiles]
    for s in tiles:
        stage_rank(s, *picks[s])
    cnt_ref[...] = carry_ref[...]


def _post_mixer(a, x_parts, mod, norm_g2, w_out, router_wt, router_bt):
    rows = ROUTE_TILES * TM
    idx_spec = pl.BlockSpec((ROUTE_TILES, 1, TM), lambda i: (i, 0, 0))
    return pl.pallas_call(
        functools.partial(_post_mixer_kernel, n_x=len(x_parts)),
        grid=(N_TILES // ROUTE_TILES,),
        in_specs=[pl.BlockSpec((rows, D), lambda i: (i, 0))] + _token_specs(x_parts, ROUTE_TILES) + [
            pl.BlockSpec((1, 6, 1, D), lambda i: (_cond_of_tile(i * ROUTE_TILES), 0, 0, 0)), _ROW_SPEC,
            _ANY_SPEC,
            pl.BlockSpec((N_EXPERTS, D), lambda i: (0, 0)),
            pl.BlockSpec((N_EXPERTS, 1), lambda i: (0, 0))],
        out_specs=[pl.BlockSpec((rows, D), lambda i: (i, 0)), pl.BlockSpec((rows, LANES), lambda i: (i, 0)),
                   idx_spec, idx_spec, pl.BlockSpec((ROUTE_ROWS, LANES), lambda i: (0, 0))],
        out_shape=[jax.ShapeDtypeStruct((T, D), F32), jax.ShapeDtypeStruct((T, LANES), F32),
                   jax.ShapeDtypeStruct((N_TILES, 1, TM), jnp.int32),
                   jax.ShapeDtypeStruct((N_TILES, 1, TM), jnp.int32),
                   jax.ShapeDtypeStruct((ROUTE_ROWS, LANES), F32)],
        scratch_shapes=[pltpu.VMEM((ROUTE_ROWS, LANES), F32), pltpu.VMEM((TM, TM), BF16)] + _weight_scratch(1),
        compiler_params=_cparams(),
        name="post_mixer",
    )(a, *x_parts, mod, norm_g2, w_out, router_wt, router_bt)


_MOVE_SPEC = pl.BlockSpec((MOVE_ROWS, D), lambda i, *_: (i, 0))
_MOVE_MOD_SPEC = pl.BlockSpec((1, 6, 1, D), lambda i, *_: (_cond_of_tile(i * MOVE_TILES), 0, 0, 0))


def _smem_step_spec(step_of_step):
    return pl.BlockSpec((MOVE_TILES, 1, TM), lambda i, *_: (step_of_step(i), 0, 0), memory_space=pltpu.SMEM)


def _start_step_rows(make_copy):
    for tile in range(MOVE_TILES):
        for r in range(TM):
            make_copy(tile, r).start(priority=r % 2)


def _dispatch_kernel(zero_from_ref, slot_ref, x1_ref, mod_ref, g2_ref, gates_ref,
                     xs_ref, buf_ref, zero_ref, sems, zero_sem):
    i = pl.program_id(0)
    cur = i % 2

    @pl.when(i == 0)
    def _():
        zero_ref[...] = jnp.zeros_like(zero_ref)

        def for_each_chunk(act):
            def tile(j, carry):
                def chunk(c, inner):
                    row = pl.multiple_of(j * TM_E + c * ZERO_ROWS, ZERO_ROWS)
                    act(pltpu.make_async_copy(zero_ref, xs_ref.at[pl.ds(row, ZERO_ROWS)], zero_sem))
                    return inner
                return lax.fori_loop(zero_from_ref[j], TM_E // ZERO_ROWS, chunk, carry)
            lax.fori_loop(0, N_TILES_E, tile, 0)

        for_each_chunk(lambda copy: copy.start())
        for_each_chunk(lambda copy: copy.wait())

    def wait_step(b):
        pltpu.make_async_copy(buf_ref.at[b], xs_ref.at[pl.ds(0, MOVE_ROWS)], sems.at[b]).wait()

    h2 = _modulate(x1_ref[...], g2_ref[...], mod_ref[0, 3], mod_ref[0, 4])
    gates = gates_ref[...]

    def send(b):
        buf_ref[b, :, :D] = h2
        buf_ref[b, :, D:] = gates
        _start_step_rows(lambda tile, r: pltpu.make_async_copy(
            buf_ref.at[b, pl.ds(tile * TM + r, 1)], xs_ref.at[pl.ds(slot_ref[tile, 0, r], 1)], sems.at[b]))

        @pl.when(i > 0)
        def _():
            wait_step(1 - b)

        @pl.when(i == N_MOVE_STEPS - 1)
        def _():
            wait_step(b)

    for b in range(2):
        pl.when(cur == b)(functools.partial(send, b))


def _dispatch(zero_from, slot, x1, mod, norm_g2, gates):
    return pl.pallas_call(
        _dispatch_kernel,
        grid_spec=pltpu.PrefetchScalarGridSpec(
            num_scalar_prefetch=1,
            grid=(N_MOVE_STEPS,),
            in_specs=[_smem_step_spec(lambda i: i), _MOVE_SPEC, _MOVE_MOD_SPEC,
                      pl.BlockSpec((1, D), lambda i, *_: (0, 0)),
                      pl.BlockSpec((MOVE_ROWS, LANES), lambda i, *_: (i, 0))],
            out_specs=pl.BlockSpec(memory_space=pl.ANY),
            scratch_shapes=[pltpu.VMEM((2, MOVE_ROWS, XW), F32), pltpu.VMEM((ZERO_ROWS, XW), F32),
                            pltpu.SemaphoreType.DMA((2,)), pltpu.SemaphoreType.DMA(())],
        ),
        out_shape=jax.ShapeDtypeStruct((S_ROWS, XW), F32),
        compiler_params=_cparams(),
        name="moe_dispatch",
    )(zero_from, slot, x1, mod, norm_g2, gates)


_W_SHAPES = ((D, D_FF), (D, D_FF), (D_FF, D))


def _expert_kernel(ea_ref, eb_ref, next_a_ref, next_b_ref, a_higher_ref, nrows_ref, xblock_ref,
                   x_ref, wg_hbm, wu_hbm, wd_hbm, y_ref, *scratch, layer):
    stage, w16, sems = scratch[0:3], scratch[3:6], scratch[6]
    j = pl.program_id(0)
    n = nrows_ref[j]
    prev = jnp.maximum(j - 1, 0)

    def fetch(slot, expert):
        return [pltpu.make_async_copy(w.at[layer, expert], s.at[slot], sems.at[slot])
                for w, s in zip((wg_hbm, wu_hbm, wd_hbm), stage)]

    for slot, e_ref, next_ref in ((0, ea_ref, next_a_ref), (1, eb_ref, next_b_ref)):
        @pl.when(j == 0)
        def _(slot=slot, e_ref=e_ref):
            for copy in fetch(slot, e_ref[0]):
                copy.start()

        @pl.when(jnp.logical_or(j == 0, e_ref[j] != e_ref[prev]))
        def _(slot=slot, e_ref=e_ref, next_ref=next_ref):
            for copy in fetch(slot, e_ref[j]):
                copy.wait()
            for s, d in zip(stage, w16):
                d[slot] = s[slot].astype(BF16)

            @pl.when(next_ref[j] >= 0)
            def _():
                for copy in fetch(slot, next_ref[j]):
                    copy.start()

    def run(rows):
        valid = lax.broadcasted_iota(jnp.int32, (rows, 1), 0) < n
        xe = jnp.where(valid, x_ref[:rows], 0.0)
        x = xe[:, :D].astype(BF16)
        g_lo, g_hi = xe[:, D:D + 1], xe[:, D + 1:D + 2]
        a_higher = a_higher_ref[j] != 0

        def ffn(slot, gate):
            wg_ref, wu_ref, wd_ref = w16
            act = _silu(_dot(x, wg_ref[slot])) * _dot(x, wu_ref[slot]) * gate
            return _dot(act.astype(BF16), wd_ref[slot])

        y_ref[:rows] = (ffn(0, jnp.where(a_higher, g_hi, g_lo))
                        + ffn(1, jnp.where(a_higher, g_lo, g_hi)))
        if rows < TM_E:
            y_ref[rows:] = jnp.zeros((TM_E - rows, D), F32)

    half = TM_E // 2
    pl.when(n > half)(functools.partial(run, TM_E))
    pl.when(jnp.logical_and(n > 0, n <= half))(functools.partial(run, half))

    @pl.when(n == 0)
    def _():
        y_ref[...] = jnp.zeros_like(y_ref)


def _experts(tile_ea, tile_eb, tile_next_a, tile_next_b, tile_a_higher, tile_rows, tile_xblock, xs,
             w_gate, w_up, w_down, layer):
    any_spec = pl.BlockSpec(memory_space=pl.ANY)
    return pl.pallas_call(
        functools.partial(_expert_kernel, layer=layer),
        grid_spec=pltpu.PrefetchScalarGridSpec(
            num_scalar_prefetch=7,
            grid=(N_TILES_E,),
            in_specs=[pl.BlockSpec((TM_E, XW), lambda j, *plan: (plan[-1][j], 0)), any_spec, any_spec, any_spec],
            out_specs=pl.BlockSpec((TM_E, D), lambda j, *_: (j, 0)),
            scratch_shapes=[pltpu.VMEM((2,) + s, F32) for s in _W_SHAPES]
                           + [pltpu.VMEM((2,) + s, BF16) for s in _W_SHAPES]
                           + [pltpu.SemaphoreType.DMA((2,))],
        ),
        out_shape=jax.ShapeDtypeStruct((S_ROWS, D), F32),
        compiler_params=_cparams(),
        name="moe_experts",
    )(tile_ea, tile_eb, tile_next_a, tile_next_b, tile_a_higher, tile_rows, tile_xblock, xs,
      w_gate, w_up, w_down)


def _combine_kernel(slot_ref, next_slot_ref, x1_ref, mod_ref, gf_ref, y_ref, *rest, final):
    out_refs, (buf_ref, sems) = rest[:-2], rest[-2:]
    i = pl.program_id(0)
    cur = i % 2

    def gather(slots, b):
        _start_step_rows(lambda tile, r: pltpu.make_async_copy(
            y_ref.at[pl.ds(slots[tile, 0, r], 1)], buf_ref.at[b, pl.ds(tile * TM + r, 1)], sems.at[b]))

    @pl.when(i == 0)
    def _():
        gather(slot_ref, 0)

    def combine(b):
        @pl.when(i + 1 < N_MOVE_STEPS)
        def _():
            gather(next_slot_ref, 1 - b)

        pltpu.make_async_copy(y_ref.at[pl.ds(0, MOVE_ROWS)], buf_ref.at[b], sems.at[b]).wait()
        x2 = x1_ref[...] + mod_ref[0, 5] * buf_ref[b]
        if not final:
            out_refs[0][...] = x2
            return
        y = _rmsnorm(x2, gf_ref[...])
        yp_ref, ys_ref = out_refs

        @pl.when(i < T_PROMPT // MOVE_ROWS)
        def _():
            yp_ref[...] = y

        @pl.when(i >= T_PROMPT // MOVE_ROWS)
        def _():
            ys_ref[...] = y

    for b in range(2):
        pl.when(cur == b)(functools.partial(combine, b))


def _combine(slot, x1, mod, final_g, y_sorted, *, final):
    if final:
        out_specs = _token_specs((None, None), MOVE_TILES)
        out_shape = [jax.ShapeDtypeStruct((T_PROMPT, D), F32), jax.ShapeDtypeStruct((T_SAMPLE, D), F32)]
    else:
        out_specs = [_MOVE_SPEC]
        out_shape = [jax.ShapeDtypeStruct((T, D), F32)]
    return pl.pallas_call(
        functools.partial(_combine_kernel, final=final),
        grid=(N_MOVE_STEPS,),
        in_specs=[_smem_step_spec(lambda i: i), _smem_step_spec(lambda i: jnp.minimum(i + 1, N_MOVE_STEPS - 1)),
                  _MOVE_SPEC, _MOVE_MOD_SPEC, _ROW_SPEC, pl.BlockSpec(memory_space=pl.ANY)],
        out_specs=out_specs,
        scratch_shapes=[pltpu.VMEM((2, MOVE_ROWS, D), F32), pltpu.SemaphoreType.DMA((2,))],
        out_shape=out_shape,
        compiler_params=_cparams(),
        name="moe_combine_final" if final else "moe_combine",
    )(slot, slot, x1, mod, final_g, y_sorted)


def _plan_kernel(cnt_ref, bucket_ref, rank_ref, slot_ref,
                 ea_ref, eb_ref, next_a_ref, next_b_ref, a_higher_ref, rows_ref, xblock_ref, zero_from_ref):
    slot = rank_ref[...]
    bucket = bucket_ref[...]
    tile0 = jnp.int32(0)
    last = [jnp.int32(0)] * 3
    for b in range(N_BUCKETS):
        n = cnt_ref[b]
        tiles = (n + (TM_E - 1)) // TM_E
        pair = b % N_PAIRS
        per_tile = ((b // N_PAIRS) * EPG + _SLOT_A[pair], (b // N_PAIRS) * EPG + _SLOT_B[pair],
                    _SLOT_A_IS_HIGHER[pair])

        def fill(k, carry, n=n, tile0=tile0, per_tile=per_tile):
            j = tile0 + k
            live = jnp.minimum(n - k * TM_E, TM_E)
            rows_ref[j] = live
            xblock_ref[j] = j
            zero_from_ref[j] = live // ZERO_ROWS
            for ref, value in zip((ea_ref, eb_ref, a_higher_ref), per_tile):
                ref[j] = jnp.int32(value)
            return carry

        lax.fori_loop(0, tiles, fill, 0)
        slot = slot + jnp.where(bucket == b, tile0 * TM_E, 0)
        last = [jnp.where(tiles > 0, value, old) for value, old in zip(per_tile, last)]
        tile0 = tile0 + tiles

    def unused(j, carry):
        rows_ref[j] = jnp.int32(0)
        xblock_ref[j] = jnp.maximum(tile0 - 1, 0)
        zero_from_ref[j] = jnp.int32(0)
        for ref, value in zip((ea_ref, eb_ref, a_higher_ref), last):
            ref[j] = value
        return carry

    lax.fori_loop(tile0, N_TILES_E, unused, 0)

    for e_ref, next_ref in ((ea_ref, next_a_ref), (eb_ref, next_b_ref)):
        next_ref[N_TILES_E - 1] = jnp.int32(-1)

        def backward(k, carry, e_ref=e_ref, next_ref=next_ref):
            j = N_TILES_E - 2 - k
            next_ref[j] = jnp.where(e_ref[j] == e_ref[j + 1], next_ref[j + 1], e_ref[j + 1])
            return carry

        lax.fori_loop(0, N_TILES_E - 1, backward, 0)
    slot_ref[...] = slot


def _plan(counts, bucket, rank):
    whole = pl.BlockSpec(bucket.shape, lambda i, *_: (0, 0, 0))
    smem = pl.BlockSpec(memory_space=pltpu.SMEM)
    per_tile = jax.ShapeDtypeStruct((N_TILES_E,), jnp.int32)
    slot, *tile_plan = pl.pallas_call(
        _plan_kernel,
        grid_spec=pltpu.PrefetchScalarGridSpec(
            num_scalar_prefetch=1, grid=(1,), in_specs=[whole, whole], out_specs=[whole] + [smem] * 8),
        out_shape=[jax.ShapeDtypeStruct(bucket.shape, jnp.int32)] + [per_tile] * 8,
        compiler_params=_cparams(),
        name="moe_plan",
    )(counts[:, 0].astype(jnp.int32), bucket, rank)
    return slot, tile_plan


def _moe(a, x_parts, mod, norm_g2, w_out, router_wt, router_bt, w_gate, w_up, w_down, layer, final_g,
         *, final):
    x1, gates, bucket, rank, counts = _post_mixer(a, x_parts, mod, norm_g2, w_out, router_wt, router_bt)
    slot, (*expert_plan, zero_from) = _plan(counts, bucket, rank)
    xs = _dispatch(zero_from, slot, x1, mod, norm_g2, gates)
    ys = _experts(*expert_plan, xs, w_gate, w_up, w_down, layer)
    return _combine(slot, x1, mod, final_g, ys, final=final)


def kernel(x_prompt, x_sample, state_hgrn, c, c_ctx, w_mod, b_mod, norm_mix_g, norm_ffn_g, norm_final_g,
           hgrn_w_in, hgrn_lb_logits, hgrn_norm_g, hgrn_w_out, conv_w_in, conv_w, conv_w_out,
           router_w, router_b, moe_w_gate, moe_w_up, moe_w_down):
    x_parts = (x_prompt.reshape(T_PROMPT, D), x_sample.reshape(T_SAMPLE, D))
    cond = jnp.concatenate([c_ctx[None], c, jnp.zeros((COND_ROWS - N_COND, D), F32)], axis=0)
    mods = _adaln(cond, w_mod, b_mod)[:, :N_COND].reshape(2, N_COND, 6, 1, D)
    member_major = lambda w: w.reshape(N_GROUPS, EPG, -1).transpose(1, 0, 2).reshape(N_EXPERTS, -1)
    rwt = member_major(router_w.T)
    rbt = member_major(router_b.reshape(N_EXPERTS, 1))
    final_g = norm_final_g.reshape(1, D)

    q, v, ff, fb, gg = _hgrn_proj(x_parts, mods[0], norm_mix_g[0:1], hgrn_w_in[0])
    s0 = state_hgrn[:, 0]
    o_f, s_f = _gla_fwd(q, v, ff, hgrn_lb_logits, s0)
    a, new_state = _gla_bwd(q, v, fb, hgrn_lb_logits, s0, o_f, gg, hgrn_norm_g[0:1], s_f)
    (x,) = _moe(a, x_parts, mods[0], norm_ffn_g[0:1], hgrn_w_out[0], rwt, rbt,
                moe_w_gate, moe_w_up, moe_w_down, 0, final_g, final=False)

    a = _conv_mixer(x, mods[1], norm_mix_g[1:2], conv_w_in[0], conv_w[0])
    y_p, y_s = _moe(a, (x,), mods[1], norm_ffn_g[1:2], conv_w_out[0], rwt, rbt,
                    moe_w_gate, moe_w_up, moe_w_down, 1, final_g, final=True)

    return (y_p.reshape(N_PROMPT_SEQ, PROMPT_LEN, D), y_s.reshape(N_SAMPLE_SEQ, SAMPLE_LEN, D),
            new_state[:, None])
```

```python
import functools

import jax
import jax.numpy as jnp
from jax import lax
from jax.experimental import pallas as pl
from jax.experimental.pallas import tpu as pltpu

F32 = jnp.float32
BF16 = jnp.bfloat16

D = 1024
N_PROMPT_SEQ = 16
PROMPT_LEN = 256
N_SAMPLE_SEQ = 2
SAMPLE_LEN = 4096
GRID_W = 64
T_PROMPT = N_PROMPT_SEQ * PROMPT_LEN
T_SAMPLE = N_SAMPLE_SEQ * SAMPLE_LEN
T = T_PROMPT + T_SAMPLE
N_COND = 1 + N_SAMPLE_SEQ
COND_ROWS = 8

HEADS = 8
DK = 128
DV = 128
CHUNK = 64
N_EXPERTS = 16
N_GROUPS = 4
EPG = 4
N_PAIRS = 6
N_BUCKETS = N_GROUPS * N_PAIRS
D_FF = 512
EPS = 1e-6

TM = 256
N_TILES = T // TM
PROMPT_TILES = T_PROMPT // TM
TILES_PER_SAMPLE = SAMPLE_LEN // TM
CHUNKS_PER_TILE = TM // CHUNK
STEP_TILES = 2
STEP_ROWS = STEP_TILES * TM
N_STEPS = N_TILES // STEP_TILES
PROMPT_STEPS = PROMPT_TILES // STEP_TILES
MOVE_TILES = 2
MOVE_ROWS = MOVE_TILES * TM
N_MOVE_STEPS = N_TILES // MOVE_TILES
ROUTE_TILES = 4
CONV_TILES = 4
TM_E = 288
N_TILES_E = T // TM_E + N_BUCKETS
S_ROWS = N_TILES_E * TM_E
ZERO_ROWS = 32
XW = D + 128
LANES = 128
MOD_TN = 1536

V7X_VMEM_BYTES = 64 * 1024 * 1024
VMEM_LIMIT = V7X_VMEM_BYTES * 7 // 8

_SLOT_A = (0, 3, 3, 0, 0, 1)
_SLOT_B = (1, 1, 2, 2, 3, 2)
_SLOT_A_IS_HIGHER = tuple(int(a > b) for a, b in zip(_SLOT_A, _SLOT_B))


def _cparams():
    return pltpu.CompilerParams(dimension_semantics=("arbitrary",), vmem_limit_bytes=VMEM_LIMIT)


def _cond_of_tile(t):
    return jnp.where(t < PROMPT_TILES, 0, 1 + (t - PROMPT_TILES) // TILES_PER_SAMPLE)


def _rmsnorm(x, g):
    ms = jnp.mean(x * x, axis=-1, keepdims=True)
    return (x * lax.rsqrt(ms + EPS)) * g


def _modulate(x, g, shift, scale):
    return _rmsnorm(x, g) * (1.0 + scale) + shift


def _silu(x):
    return x * jax.nn.sigmoid(x)


def _dot(a, b):
    return jnp.dot(a, b, preferred_element_type=F32)


def _dot_nt(a, b):
    return lax.dot_general(a, b, (((1,), (1,)), ((), ())), preferred_element_type=F32)


def _dot_tn(a, b):
    return lax.dot_general(a, b, (((0,), (0,)), ((), ())), preferred_element_type=F32)


def _split_bf16(x):
    hi = x.astype(BF16)
    lo = (x - hi.astype(F32)).astype(BF16)
    return hi, lo


def _mod_kernel(cond_ref, w_ref, b_ref, o_ref):
    s = _silu(cond_ref[...])
    o_ref[0] = _dot(s.astype(BF16), w_ref[0].astype(BF16)) + b_ref[0]


def _adaln(cond, w_mod, b_mod):
    depth = w_mod.shape[0]
    return pl.pallas_call(
        _mod_kernel,
        grid=(depth, 6 * D // MOD_TN),
        in_specs=[
            pl.BlockSpec((COND_ROWS, D), lambda i, j: (0, 0)),
            pl.BlockSpec((1, D, MOD_TN), lambda i, j: (i, 0, j)),
            pl.BlockSpec((1, 1, MOD_TN), lambda i, j: (i, 0, j)),
        ],
        out_specs=pl.BlockSpec((1, COND_ROWS, MOD_TN), lambda i, j: (i, 0, j)),
        out_shape=jax.ShapeDtypeStruct((depth, COND_ROWS, 6 * D), F32),
        compiler_params=pltpu.CompilerParams(
            dimension_semantics=("arbitrary", "arbitrary"), vmem_limit_bytes=VMEM_LIMIT),
        name="adaln",
    )(cond, w_mod, b_mod.reshape(depth, 1, 6 * D))


_ROW_SPEC = pl.BlockSpec((1, D), lambda i: (0, 0))
_STEP_MOD_SPEC = pl.BlockSpec((1, 6, 1, D), lambda i, *_: (_cond_of_tile(i * STEP_TILES), 0, 0, 0))
_STEP_SPEC = pl.BlockSpec((STEP_ROWS, D), lambda i, *_: (i, 0))
_SUB_ROWS = tuple(slice(s * TM, (s + 1) * TM) for s in range(STEP_TILES))


def _load_weight_bf16(w_hbm, w16_ref, stage_ref, sem):
    n = w16_ref.shape[1] // D

    def slab(c):
        return pltpu.make_async_copy(w_hbm.at[:, pl.ds(c * D, D)], stage_ref.at[c % 2], sem.at[c % 2])

    slab(0).start()
    for c in range(n):
        if c + 1 < n:
            slab(c + 1).start()
        slab(c).wait()
        w16_ref[:, c * D:(c + 1) * D] = stage_ref[c % 2].astype(BF16)


def _weight_scratch(n_slabs):
    n_stage = min(n_slabs, 2)
    return [pltpu.VMEM((D, n_slabs * D), BF16), pltpu.VMEM((n_stage, D, D), F32),
            pltpu.SemaphoreType.DMA((n_stage,))]


_ANY_SPEC = pl.BlockSpec(memory_space=pl.ANY)


def _token_tile(x_refs, rows=slice(None), prompt_steps=PROMPT_TILES):
    if len(x_refs) == 1:
        return x_refs[0][rows]
    return jnp.where(pl.program_id(0) < prompt_steps, x_refs[0][rows], x_refs[1][rows])


def _token_specs(x_parts, tiles_per_step=1):
    rows = tiles_per_step * TM
    if len(x_parts) == 1:
        return [pl.BlockSpec((rows, D), lambda i: (i, 0))]
    prompt_steps = T_PROMPT // rows
    return [pl.BlockSpec((rows, D), lambda i: (jnp.minimum(i, prompt_steps - 1), 0)),
            pl.BlockSpec((rows, D), lambda i: (jnp.maximum(i - prompt_steps, 0), 0))]


def _hgrn_proj_kernel(xp_ref, xs_ref, mod_ref, g_ref, w_hbm, q_ref, v_ref, ff_ref, fb_ref, gg_ref,
                      w_ref, stage_ref, sem):
    @pl.when(pl.program_id(0) == 0)
    def _():
        _load_weight_bf16(w_hbm, w_ref, stage_ref, sem)

    h = [_modulate(_token_tile((xp_ref, xs_ref), rows, PROMPT_STEPS),
                   g_ref[...], mod_ref[0, 0], mod_ref[0, 1]).astype(BF16) for rows in _SUB_ROWS]
    for rows, h_tile in zip(_SUB_ROWS, h):
        q_ref[rows] = (_dot(h_tile, w_ref[:, 0 * D:1 * D]) * (DK ** -0.5)).astype(BF16)
        v_ref[rows] = _dot(h_tile, w_ref[:, 1 * D:2 * D]).astype(BF16)
        ff_ref[rows] = _dot(h_tile, w_ref[:, 2 * D:3 * D])
        fb_ref[rows] = _dot(h_tile, w_ref[:, 3 * D:4 * D])
        gg_ref[rows] = _dot(h_tile, w_ref[:, 4 * D:5 * D]).astype(BF16)


def _hgrn_proj(x_parts, mod, norm_g, w_in):
    bf = jax.ShapeDtypeStruct((T, D), BF16)
    f32 = jax.ShapeDtypeStruct((T, D), F32)
    return pl.pallas_call(
        _hgrn_proj_kernel,
        grid=(N_STEPS,),
        in_specs=_token_specs(x_parts, STEP_TILES) + [_STEP_MOD_SPEC, _ROW_SPEC, _ANY_SPEC],
        out_specs=[_STEP_SPEC] * 5,
        out_shape=[bf, bf, f32, f32, bf],
        scratch_shapes=_weight_scratch(5),
        compiler_params=_cparams(),
        name="hgrn_proj",
    )(*x_parts, mod, norm_g, w_in)


GLA_TILES = 4
GLA_ROWS = GLA_TILES * TM
N_GLA_STEPS = N_TILES // GLA_TILES
GLA_PROMPT_STEPS = PROMPT_TILES // GLA_TILES
STEPS_PER_SAMPLE = SAMPLE_LEN // GLA_ROWS
_GLA_SPEC = pl.BlockSpec((GLA_ROWS, D), lambda i: (i, 0))
_GLA_TILE_ROWS = tuple(slice(s * TM, (s + 1) * TM) for s in range(GLA_TILES))


def _gla_step(block, q_ref, v_ref, f_ref, lbl_ref, s0_ref, st_ref, st_tiles_ref, ma_ref, o_ref,
              *, reverse, finish=None):
    is_prompt = block < GLA_PROMPT_STEPS
    rel = block - GLA_PROMPT_STEPS
    edge = (STEPS_PER_SAMPLE - 1) if reverse else 0
    sample_start = jnp.logical_and(rel >= 0, rel % STEPS_PER_SAMPLE == edge)

    @pl.when(is_prompt)
    def _():
        st_ref[...] = jnp.zeros_like(st_ref)

    @pl.when(sample_start)
    def _():
        for h in range(HEADS):
            st_ref[h] = s0_ref[0, 0, h].T

    logits = lbl_ref[...]
    e = jnp.exp(logits - jnp.max(logits, axis=0, keepdims=True))
    lb = e[0:1] / jnp.sum(e, axis=0, keepdims=True)

    ref_idx = (CHUNK - 1 - CHUNK // 2) if reverse else CHUNK // 2
    last_idx = 0 if reverse else CHUNK - 1
    shift = CHUNK.bit_length() - 1

    row = lax.broadcasted_iota(jnp.int32, (TM, TM), 0)
    col = lax.broadcasted_iota(jnp.int32, (TM, TM), 1)
    same = (row >> shift) == (col >> shift)
    ref_row = ((row >> shift) << shift) + ref_idx
    keep = jnp.logical_and(same, (row <= col) if reverse else (row >= col))

    @pl.when(pl.program_id(0) == 0)
    def _():
        at_ref = jnp.logical_and(same, (ref_row <= col) if reverse else (ref_row >= col))
        ma_ref[...] = (keep.astype(F32) - at_ref.astype(F32)).astype(BF16)

    ma = ma_ref[...]

    rr = lax.broadcasted_iota(jnp.int32, (2 * CHUNKS_PER_TILE, TM), 0)
    cc = lax.broadcasted_iota(jnp.int32, (2 * CHUNKS_PER_TILE, TM), 1)
    chunk = rr & (CHUNKS_PER_TILE - 1)
    target = chunk * CHUNK + jnp.where(rr < CHUNKS_PER_TILE, ref_idx, last_idx)
    covered = (cc >= target) if reverse else (cc <= target)
    rm = jnp.logical_and((cc >> shift) == chunk, covered).astype(BF16)

    chunk_rows = [slice(c * CHUNK, (c + 1) * CHUNK) for c in range(CHUNKS_PER_TILE)]
    order = range(CHUNKS_PER_TILE - 1, -1, -1) if reverse else range(CHUNKS_PER_TILE)
    pair_w = 2 * DK
    n_pairs = HEADS // 2
    tile_order = _GLA_TILE_ROWS[::-1] if reverse else _GLA_TILE_ROWS

    def per_chunk_scale(x, scale):
        return jnp.concatenate([x[rs] * scale[c:c + 1] for c, rs in enumerate(chunk_rows)], axis=0)

    def pair_cols(p):
        return slice(p * pair_w, (p + 1) * pair_w)

    def stage_decay(k, p):
        lbp = lb[:, pair_cols(p)]
        f = lbp + (1.0 - lbp) * jax.nn.sigmoid(f_ref[tile_order[k], pair_cols(p)])
        lf_hi, lf_lo = _split_bf16(jnp.log(f))
        z = _dot(ma, lf_hi) + _dot(ma, lf_lo)
        marks = _dot(rm, lf_hi) + _dot(rm, lf_lo)
        return 1.0 - f, z, marks

    def stage_scores(k, p, kk, z, marks):
        ref, last = marks[:CHUNKS_PER_TILE], marks[CHUNKS_PER_TILE:]
        qa32 = q_ref[tile_order[k], pair_cols(p)].astype(F32) * jnp.exp(z)
        kb32 = kk * jnp.exp(-z)
        qa = qa32.astype(BF16)
        kb = kb32.astype(BF16)
        qe = per_chunk_scale(qa32, jnp.exp(ref)).astype(BF16)
        kd = per_chunk_scale(kb32, jnp.exp(last - ref)).astype(BF16)
        v = v_ref[tile_order[k], pair_cols(p)]
        scores, incr = [], []
        for hh in range(2):
            ls = slice(hh * DK, (hh + 1) * DK)
            scores.append(_dot_nt(qa[:, ls], kb[:, ls]))
            kd_h = kd[:, ls]
            kd_blocks = jnp.concatenate(
                [jnp.concatenate([part for part in (jnp.zeros((c * CHUNK, DK), BF16), kd_h[rs],
                                                    jnp.zeros((TM - (c + 1) * CHUNK, DK), BF16))
                                  if part.shape[0]], axis=0)
                 for c, rs in enumerate(chunk_rows)], axis=1)
            incr.append(_dot_tn(v[:, ls], kd_blocks))
        return scores, incr, qe, jnp.exp(last), v

    def stage_output(k, p, scores, incr, qe, decay, v):
        base = tile_order[k].start
        for hh in range(2):
            h = 2 * p + hh
            ls = slice(hh * DK, (hh + 1) * DK)
            hs = slice(h * DK, (h + 1) * DK)
            o_intra = _dot(jnp.where(keep, scores[hh], 0.0).astype(BF16), v[:, ls])
            st = st_ref[h]
            if k > 0:
                st = jnp.where(is_prompt, 0.0, st)
            for c in order:
                rs = chunk_rows[c]
                o_ref[base + rs.start:base + rs.stop, hs] = (
                    o_intra[rs] + _dot_nt(qe[rs, ls], st.astype(BF16)))
                st = st * decay[c:c + 1, ls] + incr[hh][:, c * DK:(c + 1) * DK]
            st_ref[h] = st
            if k < GLA_TILES - 1:
                st_tiles_ref[k, h] = st
        if finish is not None:
            return (tile_order[k], p)

    def stage_finish(k, p, rows, pair):
        finish(rows, pair)

    units = [(k, p) for k in range(GLA_TILES) for p in range(n_pairs)]
    stages = [stage_decay, stage_scores, stage_output] + ([stage_finish] if finish is not None else [])
    carried = [dict() for _ in stages]
    for step in range(len(units) + len(stages) - 1):
        for depth, stage in (enumerate(stages) if reverse else reversed(list(enumerate(stages)))):
            u = step - depth
            if 0 <= u < len(units):
                out = stage(*units[u], *(carried[depth].pop(u) if depth else ()))
                if depth + 1 < len(stages):
                    carried[depth + 1][u] = out


def _store_prompt_states(block, st_ref, st_tiles_ref, snew_ref, *, reverse, sfwd_ref=None):
    @pl.when(block < GLA_PROMPT_STEPS)
    def _():
        for k in range(GLA_TILES):
            tile = GLA_TILES - 1 - k if reverse else k
            for h in range(HEADS):
                state = (st_tiles_ref[k, h] if k < GLA_TILES - 1 else st_ref[h]).T
                if sfwd_ref is None:
                    snew_ref[tile, h] = state
                else:
                    snew_ref[tile, 1, h] = state
        if sfwd_ref is not None:
            snew_ref[:, 0] = sfwd_ref[...]


def _gla_fwd_kernel(q_ref, v_ref, f_ref, lbl_ref, s0_ref, o_ref, snew_ref, st_ref, st_tiles_ref, ma_ref):
    block = pl.program_id(0)
    _gla_step(block, q_ref, v_ref, f_ref, lbl_ref, s0_ref, st_ref, st_tiles_ref, ma_ref, o_ref,
              reverse=False)
    _store_prompt_states(block, st_ref, st_tiles_ref, snew_ref, reverse=False)


def _gla_bwd_kernel(q_ref, v_ref, f_ref, lbl_ref, s0_ref, of_ref, gg_ref, ng_ref, sfwd_ref,
                    a_ref, snew_ref, st_ref, st_tiles_ref, ma_ref, ob_ref):
    block = N_GLA_STEPS - 1 - pl.program_id(0)

    def finish(rows, p):
        for h in (2 * p, 2 * p + 1):
            hs = slice(h * DV, (h + 1) * DV)
            o = of_ref[rows, hs] + ob_ref[rows, hs]
            a_ref[rows, hs] = (_rmsnorm(o, ng_ref[...]) * _silu(gg_ref[rows, hs].astype(F32))).astype(BF16)

    _gla_step(block, q_ref, v_ref, f_ref, lbl_ref, s0_ref, st_ref, st_tiles_ref, ma_ref, ob_ref,
              reverse=True, finish=finish)
    _store_prompt_states(block, st_ref, st_tiles_ref, snew_ref, reverse=True, sfwd_ref=sfwd_ref)


def _state_specs(direction, block_of_step):
    sample = lambda i: jnp.clip((block_of_step(i) - GLA_PROMPT_STEPS) // STEPS_PER_SAMPLE, 0, N_SAMPLE_SEQ - 1)
    s0 = pl.BlockSpec((1, 1, HEADS, DK, DV), lambda i: (sample(i), direction, 0, 0, 0))
    snew = pl.BlockSpec((GLA_TILES, HEADS, DK, DV),
                        lambda i: (jnp.minimum(block_of_step(i), GLA_PROMPT_STEPS - 1), 0, 0, 0))
    return s0, snew


_GLA_STATE_SCRATCH = [pltpu.VMEM((HEADS, DV, DK), F32), pltpu.VMEM((GLA_TILES - 1, HEADS, DV, DK), F32),
                      pltpu.VMEM((TM, TM), BF16)]


def _gla_fwd(q, v, ff, lb_logits, s0):
    s0_spec, snew_spec = _state_specs(0, lambda i: i)
    return pl.pallas_call(
        _gla_fwd_kernel,
        grid=(N_GLA_STEPS,),
        in_specs=[_GLA_SPEC, _GLA_SPEC, _GLA_SPEC,
                  pl.BlockSpec(lb_logits.shape, lambda i: (0, 0)), s0_spec],
        out_specs=[_GLA_SPEC, snew_spec],
        out_shape=[jax.ShapeDtypeStruct((T, D), F32),
                   jax.ShapeDtypeStruct((N_PROMPT_SEQ, HEADS, DK, DV), F32)],
        scratch_shapes=_GLA_STATE_SCRATCH,
        compiler_params=_cparams(),
        name="gla_fwd",
    )(q, v, ff, lb_logits, s0)


def _gla_bwd(q, v, fb, lb_logits, s0, o_f, gg, head_norm_g, s_fwd):
    rev = lambda i: N_GLA_STEPS - 1 - i
    block = pl.BlockSpec((GLA_ROWS, D), lambda i: (rev(i), 0))
    s0_spec, sfwd_spec = _state_specs(1, rev)
    prompt_block = lambda i: jnp.minimum(rev(i), GLA_PROMPT_STEPS - 1)
    return pl.pallas_call(
        _gla_bwd_kernel,
        grid=(N_GLA_STEPS,),
        in_specs=[block, block, block, pl.BlockSpec(lb_logits.shape, lambda i: (0, 0)), s0_spec,
                  block, block, pl.BlockSpec((1, DV), lambda i: (0, 0)), sfwd_spec],
        out_specs=[block, pl.BlockSpec((GLA_TILES, 2, HEADS, DK, DV), lambda i: (prompt_block(i), 0, 0, 0, 0))],
        out_shape=[jax.ShapeDtypeStruct((T, D), BF16),
                   jax.ShapeDtypeStruct((N_PROMPT_SEQ, 2, HEADS, DK, DV), F32)],
        scratch_shapes=_GLA_STATE_SCRATCH + [pltpu.VMEM((GLA_ROWS, D), F32)],
        compiler_params=_cparams(),
        name="gla_bwd",
    )(q, v, fb, lb_logits, s0, o_f, gg, head_norm_g, s_fwd)


def _conv_mixer_kernel(x_ref, mod_ref, g_ref, w_hbm, cw_ref, a_ref, w_ref, stage_ref, sem):
    @pl.when(pl.program_id(0) == 0)
    def _():
        _load_weight_bf16(w_hbm, w_ref, stage_ref, sem)

    seg = jnp.where(pl.program_id(0) < PROMPT_TILES // CONV_TILES, PROMPT_LEN, GRID_W)
    pos = lax.broadcasted_iota(jnp.int32, (TM, 1), 0) & (seg - 1)

    def project(rows):
        h = _modulate(x_ref[rows], g_ref[...], mod_ref[0, 0], mod_ref[0, 1]).astype(BF16)
        return [_dot(h, w_ref[:, k * D:(k + 1) * D]) for k in range(3)]

    def convolve(rows, bg, cg, x_in):
        u = cg * x_in
        prev = jnp.where(pos == 0, 0.0, pltpu.roll(u, 1, axis=0))
        nxt = jnp.where(pos == seg - 1, 0.0, pltpu.roll(u, TM - 1, axis=0))
        conv = cw_ref[0:1] * prev + cw_ref[1:2] * u + cw_ref[2:3] * nxt
        a_ref[rows] = (bg * conv).astype(BF16)

    tiles = [slice(s * TM, (s + 1) * TM) for s in range(CONV_TILES)]
    projected = [project(rows) for rows in tiles]
    for rows, parts in zip(tiles, projected):
        convolve(rows, *parts)


def _conv_mixer(x, mod, norm_g, w_in, conv_w):
    block = pl.BlockSpec((CONV_TILES * TM, D), lambda i: (i, 0))
    return pl.pallas_call(
        _conv_mixer_kernel,
        grid=(N_TILES // CONV_TILES,),
        in_specs=[block, pl.BlockSpec((1, 6, 1, D), lambda i: (_cond_of_tile(i * CONV_TILES), 0, 0, 0)),
                  _ROW_SPEC, _ANY_SPEC, pl.BlockSpec((3, D), lambda i: (0, 0))],
        out_specs=block,
        out_shape=jax.ShapeDtypeStruct((T, D), BF16),
        scratch_shapes=_weight_scratch(3),
        compiler_params=_cparams(),
        name="conv_mixer",
    )(x, mod, norm_g, w_in, conv_w)


ROUTE_ROWS = 32


def _first_member(vals, target):
    idx = jnp.full_like(target, float(EPG - 1))
    for j in range(EPG - 2, -1, -1):
        idx = jnp.where(vals[j] == target, float(j), idx)
    return idx


def _router_logits(h2, rwt_ref):
    h_hi, h_lo = _split_bf16(h2)
    w_hi, w_lo = _split_bf16(rwt_ref[...])
    return _dot_nt(w_hi, h_hi) + (_dot_nt(w_hi, h_lo) + _dot_nt(w_lo, h_hi))


def _select_experts(logits, rb_ref):
    scores = jax.nn.sigmoid(logits)
    sel = scores + rb_ref[...]
    member = [sel[j * N_GROUPS:(j + 1) * N_GROUPS] for j in range(EPG)]
    m1 = functools.reduce(jnp.maximum, member)
    i1 = _first_member(member, m1)
    rest = [jnp.where(i1 == float(j), -jnp.inf, member[j]) for j in range(EPG)]
    m2 = functools.reduce(jnp.maximum, rest)
    i2 = _first_member(rest, m2)
    group_score = m1 + m2

    best, grp, first, second = group_score[0:1], jnp.zeros((1, TM), F32), i1[0:1], i2[0:1]
    for g in range(1, N_GROUPS):
        upd = group_score[g:g + 1] > best
        best = jnp.where(upd, group_score[g:g + 1], best)
        grp = jnp.where(upd, float(g), grp)
        first = jnp.where(upd, i1[g:g + 1], first)
        second = jnp.where(upd, i2[g:g + 1], second)
    a = jnp.minimum(first, second)
    b = jnp.maximum(first, second)
    row = lax.broadcasted_iota(jnp.int32, (N_EXPERTS, TM), 0).astype(F32)
    s_lo = jnp.sum(jnp.where(row == a * N_GROUPS + grp, scores, 0.0), axis=0, keepdims=True)
    s_hi = jnp.sum(jnp.where(row == b * N_GROUPS + grp, scores, 0.0), axis=0, keepdims=True)
    tot = s_lo + s_hi
    pair = jnp.where(a == 0.0, jnp.where(b == 1.0, 0.0, jnp.where(b == 2.0, 3.0, 4.0)),
                     jnp.where(a == 1.0, jnp.where(b == 2.0, 5.0, 1.0), 2.0))
    return grp * N_PAIRS + pair, s_lo / tot, s_hi / tot


def _post_mixer_kernel(*refs, n_x):
    a_ref, x_refs = refs[0], refs[1:1 + n_x]
    (mod_ref, g2_ref, wo_hbm, rwt_ref, rb_ref,
     x1_ref, gates_ref, bucket_ref, rank_ref, cnt_ref, carry_ref, earlier_ref, wo_ref, stage_ref, sem) = refs[1 + n_x:]

    @pl.when(pl.program_id(0) == 0)
    def _():
        _load_weight_bf16(wo_hbm, wo_ref, stage_ref, sem)
        carry_ref[...] = jnp.zeros_like(carry_ref)
        s_idx = lax.broadcasted_iota(jnp.int32, (TM, TM), 0)
        t_idx = lax.broadcasted_iota(jnp.int32, (TM, TM), 1)
        earlier_ref[...] = (s_idx < t_idx).astype(BF16)

    def stage_mix(s):
        rows = slice(s * TM, (s + 1) * TM)
        x1 = (_token_tile(x_refs, rows, PROMPT_TILES // ROUTE_TILES)
              + mod_ref[0, 2] * _dot(a_ref[rows], wo_ref[...]))
        x1_ref[rows] = x1
        return _modulate(x1, g2_ref[...], mod_ref[0, 3], mod_ref[0, 4])

    def stage_rank(s, bucket, g_lo, g_hi):
        onehot = lax.broadcasted_iota(jnp.int32, (ROUTE_ROWS, TM), 0).astype(F32) == bucket
        before = _dot(onehot.astype(BF16), earlier_ref[...])
        oh = onehot.astype(F32)
        carry = carry_ref[...]
        rank = jnp.sum(oh * (before + carry[:, 0:1]), axis=0, keepdims=True)
        carry_ref[...] = carry + jnp.sum(oh, axis=1, keepdims=True)
        bucket_ref[s] = bucket.astype(jnp.int32)
        rank_ref[s] = rank.astype(jnp.int32)
        grow = lax.broadcasted_iota(jnp.int32, (LANES, TM), 0)
        gates_ref[s * TM:(s + 1) * TM] = jnp.where(grow == 0, g_lo, jnp.where(grow == 1, g_hi, 0.0)).T

    tiles = range(ROUTE_TILES)
    h2 = [stage_mix(s) for s in tiles]
    logits = [_router_logits(h2[s], rwt_ref) for s in tiles]
    picks = [_select_experts(logits[s], rb_ref) for s in tiles]
    for s in tiles:
        stage_rank(s, *picks[s])
    cnt_ref[...] = carry_ref[...]


def _post_mixer(a, x_parts, mod, norm_g2, w_out, router_wt, router_bt):
    rows = ROUTE_TILES * TM
    idx_spec = pl.BlockSpec((ROUTE_TILES, 1, TM), lambda i: (i, 0, 0))
    return pl.pallas_call(
        functools.partial(_post_mixer_kernel, n_x=len(x_parts)),
        grid=(N_TILES // ROUTE_TILES,),
        in_specs=[pl.BlockSpec((rows, D), lambda i: (i, 0))] + _token_specs(x_parts, ROUTE_TILES) + [
            pl.BlockSpec((1, 6, 1, D), lambda i: (_cond_of_tile(i * ROUTE_TILES), 0, 0, 0)), _ROW_SPEC,
            _ANY_SPEC,
            pl.BlockSpec((N_EXPERTS, D), lambda i: (0, 0)),
            pl.BlockSpec((N_EXPERTS, 1), lambda i: (0, 0))],
        out_specs=[pl.BlockSpec((rows, D), lambda i: (i, 0)), pl.BlockSpec((rows, LANES), lambda i: (i, 0)),
                   idx_spec, idx_spec, pl.BlockSpec((ROUTE_ROWS, LANES), lambda i: (0, 0))],
        out_shape=[jax.ShapeDtypeStruct((T, D), F32), jax.ShapeDtypeStruct((T, LANES), F32),
                   jax.ShapeDtypeStruct((N_TILES, 1, TM), jnp.int32),
                   jax.ShapeDtypeStruct((N_TILES, 1, TM), jnp.int32),
                   jax.ShapeDtypeStruct((ROUTE_ROWS, LANES), F32)],
        scratch_shapes=[pltpu.VMEM((ROUTE_ROWS, LANES), F32), pltpu.VMEM((TM, TM), BF16)] + _weight_scratch(1),
        compiler_params=_cparams(),
        name="post_mixer",
    )(a, *x_parts, mod, norm_g2, w_out, router_wt, router_bt)


_MOVE_SPEC = pl.BlockSpec((MOVE_ROWS, D), lambda i, *_: (i, 0))
_MOVE_MOD_SPEC = pl.BlockSpec((1, 6, 1, D), lambda i, *_: (_cond_of_tile(i * MOVE_TILES), 0, 0, 0))


def _smem_step_spec(step_of_step):
    return pl.BlockSpec((MOVE_TILES, 1, TM), lambda i, *_: (step_of_step(i), 0, 0), memory_space=pltpu.SMEM)


def _start_step_rows(make_copy):
    for tile in range(MOVE_TILES):
        for r in range(TM):
            make_copy(tile, r).start(priority=r % 2)


def _dispatch_kernel(zero_from_ref, slot_ref, x1_ref, mod_ref, g2_ref, gates_ref,
                     xs_ref, buf_ref, zero_ref, sems, zero_sem):
    i = pl.program_id(0)
    cur = i % 2

    @pl.when(i == 0)
    def _():
        zero_ref[...] = jnp.zeros_like(zero_ref)

        def for_each_chunk(act):
            def tile(j, carry):
                def chunk(c, inner):
                    row = pl.multiple_of(j * TM_E + c * ZERO_ROWS, ZERO_ROWS)
                    act(pltpu.make_async_copy(zero_ref, xs_ref.at[pl.ds(row, ZERO_ROWS)], zero_sem))
                    return inner
                return lax.fori_loop(zero_from_ref[j], TM_E // ZERO_ROWS, chunk, carry)
            lax.fori_loop(0, N_TILES_E, tile, 0)

        for_each_chunk(lambda copy: copy.start())
        for_each_chunk(lambda copy: copy.wait())

    def wait_step(b):
        pltpu.make_async_copy(buf_ref.at[b], xs_ref.at[pl.ds(0, MOVE_ROWS)], sems.at[b]).wait()

    h2 = _modulate(x1_ref[...], g2_ref[...], mod_ref[0, 3], mod_ref[0, 4])
    gates = gates_ref[...]

    def send(b):
        buf_ref[b, :, :D] = h2
        buf_ref[b, :, D:] = gates
        _start_step_rows(lambda tile, r: pltpu.make_async_copy(
            buf_ref.at[b, pl.ds(tile * TM + r, 1)], xs_ref.at[pl.ds(slot_ref[tile, 0, r], 1)], sems.at[b]))

        @pl.when(i > 0)
        def _():
            wait_step(1 - b)

        @pl.when(i == N_MOVE_STEPS - 1)
        def _():
            wait_step(b)

    for b in range(2):
        pl.when(cur == b)(functools.partial(send, b))


def _dispatch(zero_from, slot, x1, mod, norm_g2, gates):
    return pl.pallas_call(
        _dispatch_kernel,
        grid_spec=pltpu.PrefetchScalarGridSpec(
            num_scalar_prefetch=1,
            grid=(N_MOVE_STEPS,),
            in_specs=[_smem_step_spec(lambda i: i), _MOVE_SPEC, _MOVE_MOD_SPEC,
                      pl.BlockSpec((1, D), lambda i, *_: (0, 0)),
                      pl.BlockSpec((MOVE_ROWS, LANES), lambda i, *_: (i, 0))],
            out_specs=pl.BlockSpec(memory_space=pl.ANY),
            scratch_shapes=[pltpu.VMEM((2, MOVE_ROWS, XW), F32), pltpu.VMEM((ZERO_ROWS, XW), F32),
                            pltpu.SemaphoreType.DMA((2,)), pltpu.SemaphoreType.DMA(())],
        ),
        out_shape=jax.ShapeDtypeStruct((S_ROWS, XW), F32),
        compiler_params=_cparams(),
        name="moe_dispatch",
    )(zero_from, slot, x1, mod, norm_g2, gates)


_W_SHAPES = ((D, D_FF), (D, D_FF), (D_FF, D))


def _expert_kernel(ea_ref, eb_ref, next_a_ref, next_b_ref, a_higher_ref, nrows_ref, xblock_ref,
                   x_ref, wg_hbm, wu_hbm, wd_hbm, y_ref, *scratch, layer):
    stage, w16, sems = scratch[0:3], scratch[3:6], scratch[6]
    j = pl.program_id(0)
    n = nrows_ref[j]
    prev = jnp.maximum(j - 1, 0)

    def fetch(slot, expert):
        return [pltpu.make_async_copy(w.at[layer, expert], s.at[slot], sems.at[slot])
                for w, s in zip((wg_hbm, wu_hbm, wd_hbm), stage)]

    for slot, e_ref, next_ref in ((0, ea_ref, next_a_ref), (1, eb_ref, next_b_ref)):
        @pl.when(j == 0)
        def _(slot=slot, e_ref=e_ref):
            for copy in fetch(slot, e_ref[0]):
                copy.start()

        @pl.when(jnp.logical_or(j == 0, e_ref[j] != e_ref[prev]))
        def _(slot=slot, e_ref=e_ref, next_ref=next_ref):
            for copy in fetch(slot, e_ref[j]):
                copy.wait()
            for s, d in zip(stage, w16):
                d[slot] = s[slot].astype(BF16)

            @pl.when(next_ref[j] >= 0)
            def _():
                for copy in fetch(slot, next_ref[j]):
                    copy.start(priority=1)

    def run(rows):
        valid = lax.broadcasted_iota(jnp.int32, (rows, 1), 0) < n
        xe = jnp.where(valid, x_ref[:rows], 0.0)
        x = xe[:, :D].astype(BF16)
        g_lo, g_hi = xe[:, D:D + 1], xe[:, D + 1:D + 2]
        a_higher = a_higher_ref[j] != 0

        def ffn(slot, gate):
            wg_ref, wu_ref, wd_ref = w16
            act = _silu(_dot(x, wg_ref[slot])) * _dot(x, wu_ref[slot]) * gate
            return _dot(act.astype(BF16), wd_ref[slot])

        y_ref[:rows] = (ffn(0, jnp.where(a_higher, g_hi, g_lo))
                        + ffn(1, jnp.where(a_higher, g_lo, g_hi)))
        if rows < TM_E:
            y_ref[rows:] = jnp.zeros((TM_E - rows, D), F32)

    half = TM_E // 2
    pl.when(n > half)(functools.partial(run, TM_E))
    pl.when(jnp.logical_and(n > 0, n <= half))(functools.partial(run, half))

    @pl.when(n == 0)
    def _():
        y_ref[...] = jnp.zeros_like(y_ref)


def _experts(tile_ea, tile_eb, tile_next_a, tile_next_b, tile_a_higher, tile_rows, tile_xblock, xs,
             w_gate, w_up, w_down, layer):
    any_spec = pl.BlockSpec(memory_space=pl.ANY)
    return pl.pallas_call(
        functools.partial(_expert_kernel, layer=layer),
        grid_spec=pltpu.PrefetchScalarGridSpec(
            num_scalar_prefetch=7,
            grid=(N_TILES_E,),
            in_specs=[pl.BlockSpec((TM_E, XW), lambda j, *plan: (plan[-1][j], 0)), any_spec, any_spec, any_spec],
            out_specs=pl.BlockSpec((TM_E, D), lambda j, *_: (j, 0)),
            scratch_shapes=[pltpu.VMEM((2,) + s, F32) for s in _W_SHAPES]
                           + [pltpu.VMEM((2,) + s, BF16) for s in _W_SHAPES]
                           + [pltpu.SemaphoreType.DMA((2,))],
        ),
        out_shape=jax.ShapeDtypeStruct((S_ROWS, D), F32),
        compiler_params=_cparams(),
        name="moe_experts",
    )(tile_ea, tile_eb, tile_next_a, tile_next_b, tile_a_higher, tile_rows, tile_xblock, xs,
      w_gate, w_up, w_down)


def _combine_kernel(slot_ref, next_slot_ref, x1_ref, mod_ref, gf_ref, y_ref, *rest, final):
    out_refs, (buf_ref, sems) = rest[:-2], rest[-2:]
    i = pl.program_id(0)
    cur = i % 2

    def gather(slots, b):
        _start_step_rows(lambda tile, r: pltpu.make_async_copy(
            y_ref.at[pl.ds(slots[tile, 0, r], 1)], buf_ref.at[b, pl.ds(tile * TM + r, 1)], sems.at[b]))

    @pl.when(i == 0)
    def _():
        gather(slot_ref, 0)

    def combine(b):
        @pl.when(i + 1 < N_MOVE_STEPS)
        def _():
            gather(next_slot_ref, 1 - b)

        pltpu.make_async_copy(y_ref.at[pl.ds(0, MOVE_ROWS)], buf_ref.at[b], sems.at[b]).wait()
        x2 = x1_ref[...] + mod_ref[0, 5] * buf_ref[b]
        if not final:
            out_refs[0][...] = x2
            return
        y = _rmsnorm(x2, gf_ref[...])
        yp_ref, ys_ref = out_refs

        @pl.when(i < T_PROMPT // MOVE_ROWS)
        def _():
            yp_ref[...] = y

        @pl.when(i >= T_PROMPT // MOVE_ROWS)
        def _():
            ys_ref[...] = y

    for b in range(2):
        pl.when(cur == b)(functools.partial(combine, b))


def _combine(slot, x1, mod, final_g, y_sorted, *, final):
    if final:
        out_specs = _token_specs((None, None), MOVE_TILES)
        out_shape = [jax.ShapeDtypeStruct((T_PROMPT, D), F32), jax.ShapeDtypeStruct((T_SAMPLE, D), F32)]
    else:
        out_specs = [_MOVE_SPEC]
        out_shape = [jax.ShapeDtypeStruct((T, D), F32)]
    return pl.pallas_call(
        functools.partial(_combine_kernel, final=final),
        grid=(N_MOVE_STEPS,),
        in_specs=[_smem_step_spec(lambda i: i), _smem_step_spec(lambda i: jnp.minimum(i + 1, N_MOVE_STEPS - 1)),
                  _MOVE_SPEC, _MOVE_MOD_SPEC, _ROW_SPEC, pl.BlockSpec(memory_space=pl.ANY)],
        out_specs=out_specs,
        scratch_shapes=[pltpu.VMEM((2, MOVE_ROWS, D), F32), pltpu.SemaphoreType.DMA((2,))],
        out_shape=out_shape,
        compiler_params=_cparams(),
        name="moe_combine_final" if final else "moe_combine",
    )(slot, slot, x1, mod, final_g, y_sorted)


def _plan_kernel(cnt_ref, bucket_ref, rank_ref, slot_ref,
                 ea_ref, eb_ref, next_a_ref, next_b_ref, a_higher_ref, rows_ref, xblock_ref, zero_from_ref):
    slot = rank_ref[...]
    bucket = bucket_ref[...]
    tile0 = jnp.int32(0)
    last = [jnp.int32(0)] * 3
    for b in range(N_BUCKETS):
        n = cnt_ref[b]
        tiles = (n + (TM_E - 1)) // TM_E
        pair = b % N_PAIRS
        per_tile = ((b // N_PAIRS) * EPG + _SLOT_A[pair], (b // N_PAIRS) * EPG + _SLOT_B[pair],
                    _SLOT_A_IS_HIGHER[pair])

        def fill(k, carry, n=n, tile0=tile0, per_tile=per_tile):
            j = tile0 + k
            live = jnp.minimum(n - k * TM_E, TM_E)
            rows_ref[j] = live
            xblock_ref[j] = j
            zero_from_ref[j] = live // ZERO_ROWS
            for ref, value in zip((ea_ref, eb_ref, a_higher_ref), per_tile):
                ref[j] = jnp.int32(value)
            return carry

        lax.fori_loop(0, tiles, fill, 0)
        slot = slot + jnp.where(bucket == b, tile0 * TM_E, 0)
        last = [jnp.where(tiles > 0, value, old) for value, old in zip(per_tile, last)]
        tile0 = tile0 + tiles

    def unused(j, carry):
        rows_ref[j] = jnp.int32(0)
        xblock_ref[j] = jnp.maximum(tile0 - 1, 0)
        zero_from_ref[j] = jnp.int32(0)
        for ref, value in zip((ea_ref, eb_ref, a_higher_ref), last):
            ref[j] = value
        return carry

    lax.fori_loop(tile0, N_TILES_E, unused, 0)

    for e_ref, next_ref in ((ea_ref, next_a_ref), (eb_ref, next_b_ref)):
        next_ref[N_TILES_E - 1] = jnp.int32(-1)

        def backward(k, carry, e_ref=e_ref, next_ref=next_ref):
            j = N_TILES_E - 2 - k
            next_ref[j] = jnp.where(e_ref[j] == e_ref[j + 1], next_ref[j + 1], e_ref[j + 1])
            return carry

        lax.fori_loop(0, N_TILES_E - 1, backward, 0)
    slot_ref[...] = slot


def _plan(counts, bucket, rank):
    whole = pl.BlockSpec(bucket.shape, lambda i, *_: (0, 0, 0))
    smem = pl.BlockSpec(memory_space=pltpu.SMEM)
    per_tile = jax.ShapeDtypeStruct((N_TILES_E,), jnp.int32)
    slot, *tile_plan = pl.pallas_call(
        _plan_kernel,
        grid_spec=pltpu.PrefetchScalarGridSpec(
            num_scalar_prefetch=1, grid=(1,), in_specs=[whole, whole], out_specs=[whole] + [smem] * 8),
        out_shape=[jax.ShapeDtypeStruct(bucket.shape, jnp.int32)] + [per_tile] * 8,
        compiler_params=_cparams(),
        name="moe_plan",
    )(counts[:, 0].astype(jnp.int32), bucket, rank)
    return slot, tile_plan


def _moe(a, x_parts, mod, norm_g2, w_out, router_wt, router_bt, w_gate, w_up, w_down, layer, final_g,
         *, final):
    x1, gates, bucket, rank, counts = _post_mixer(a, x_parts, mod, norm_g2, w_out, router_wt, router_bt)
    slot, (*expert_plan, zero_from) = _plan(counts, bucket, rank)
    xs = _dispatch(zero_from, slot, x1, mod, norm_g2, gates)
    ys = _experts(*expert_plan, xs, w_gate, w_up, w_down, layer)
    return _combine(slot, x1, mod, final_g, ys, final=final)


def kernel(x_prompt, x_sample, state_hgrn, c, c_ctx, w_mod, b_mod, norm_mix_g, norm_ffn_g, norm_final_g,
           hgrn_w_in, hgrn_lb_logits, hgrn_norm_g, hgrn_w_out, conv_w_in, conv_w, conv_w_out,
           router_w, router_b, moe_w_gate, moe_w_up, moe_w_down):
    x_parts = (x_prompt.reshape(T_PROMPT, D), x_sample.reshape(T_SAMPLE, D))
    cond = jnp.concatenate([c_ctx[None], c, jnp.zeros((COND_ROWS - N_COND, D), F32)], axis=0)
    mods = _adaln(cond, w_mod, b_mod)[:, :N_COND].reshape(2, N_COND, 6, 1, D)
    member_major = lambda w: w.reshape(N_GROUPS, EPG, -1).transpose(1, 0, 2).reshape(N_EXPERTS, -1)
    rwt = member_major(router_w.T)
    rbt = member_major(router_b.reshape(N_EXPERTS, 1))
    final_g = norm_final_g.reshape(1, D)

    q, v, ff, fb, gg = _hgrn_proj(x_parts, mods[0], norm_mix_g[0:1], hgrn_w_in[0])
    s0 = state_hgrn[:, 0]
    o_f, s_f = _gla_fwd(q, v, ff, hgrn_lb_logits, s0)
    a, new_state = _gla_bwd(q, v, fb, hgrn_lb_logits, s0, o_f, gg, hgrn_norm_g[0:1], s_f)
    (x,) = _moe(a, x_parts, mods[0], norm_ffn_g[0:1], hgrn_w_out[0], rwt, rbt,
                moe_w_gate, moe_w_up, moe_w_down, 0, final_g, final=False)

    a = _conv_mixer(x, mods[1], norm_mix_g[1:2], conv_w_in[0], conv_w[0])
    y_p, y_s = _moe(a, (x,), mods[1], norm_ffn_g[1:2], conv_w_out[0], rwt, rbt,
                    moe_w_gate, moe_w_up, moe_w_down, 1, final_g, final=True)

    return (y_p.reshape(N_PROMPT_SEQ, PROMPT_LEN, D), y_s.reshape(N_SAMPLE_SEQ, SAMPLE_LEN, D),
            new_state[:, None])
```

```python
import functools

import jax
import jax.numpy as jnp
from jax import lax
from jax.experimental import pallas as pl
from jax.experimental.pallas import tpu as pltpu

F32 = jnp.float32
BF16 = jnp.bfloat16

D = 1024
N_PROMPT_SEQ = 16
PROMPT_LEN = 256
N_SAMPLE_SEQ = 2
SAMPLE_LEN = 4096
GRID_W = 64
T_PROMPT = N_PROMPT_SEQ * PROMPT_LEN
T_SAMPLE = N_SAMPLE_SEQ * SAMPLE_LEN
T = T_PROMPT + T_SAMPLE
N_COND = 1 + N_SAMPLE_SEQ
COND_ROWS = 8

HEADS = 8
DK = 128
DV = 128
CHUNK = 64
N_EXPERTS = 16
N_GROUPS = 4
EPG = 4
N_PAIRS = 6
N_BUCKETS = N_GROUPS * N_PAIRS
D_FF = 512
EPS = 1e-6

TM = 256
N_TILES = T // TM
PROMPT_TILES = T_PROMPT // TM
TILES_PER_SAMPLE = SAMPLE_LEN // TM
CHUNKS_PER_TILE = TM // CHUNK
STEP_TILES = 2
STEP_ROWS = STEP_TILES * TM
N_STEPS = N_TILES // STEP_TILES
PROMPT_STEPS = PROMPT_TILES // STEP_TILES
MOVE_TILES = 2
MOVE_ROWS = MOVE_TILES * TM
N_MOVE_STEPS = N_TILES // MOVE_TILES
ROUTE_TILES = 4
CONV_TILES = 4
TM_E = 288
N_TILES_E = T // TM_E + N_BUCKETS
S_ROWS = N_TILES_E * TM_E
ZERO_ROWS = 32
XW = D + 128
LANES = 128
MOD_TN = 3072

V7X_VMEM_BYTES = 64 * 1024 * 1024
VMEM_LIMIT = V7X_VMEM_BYTES * 7 // 8

_SLOT_A = (0, 3, 3, 0, 0, 1)
_SLOT_B = (1, 1, 2, 2, 3, 2)
_SLOT_A_IS_HIGHER = tuple(int(a > b) for a, b in zip(_SLOT_A, _SLOT_B))


def _cparams():
    return pltpu.CompilerParams(dimension_semantics=("arbitrary",), vmem_limit_bytes=VMEM_LIMIT)


def _cond_of_tile(t):
    return jnp.where(t < PROMPT_TILES, 0, 1 + (t - PROMPT_TILES) // TILES_PER_SAMPLE)


def _rmsnorm(x, g):
    ms = jnp.mean(x * x, axis=-1, keepdims=True)
    return (x * lax.rsqrt(ms + EPS)) * g


def _modulate(x, g, shift, scale):
    return _rmsnorm(x, g) * (1.0 + scale) + shift


def _silu(x):
    return x * jax.nn.sigmoid(x)


def _dot(a, b):
    return jnp.dot(a, b, preferred_element_type=F32)


def _dot_nt(a, b):
    return lax.dot_general(a, b, (((1,), (1,)), ((), ())), preferred_element_type=F32)


def _dot_tn(a, b):
    return lax.dot_general(a, b, (((0,), (0,)), ((), ())), preferred_element_type=F32)


def _split_bf16(x):
    hi = x.astype(BF16)
    lo = (x - hi.astype(F32)).astype(BF16)
    return hi, lo


def _mod_kernel(cond_ref, w_ref, b_ref, o_ref):
    s = _silu(cond_ref[...])
    o_ref[0] = _dot(s.astype(BF16), w_ref[0].astype(BF16)) + b_ref[0]


def _adaln(cond, w_mod, b_mod):
    depth = w_mod.shape[0]
    return pl.pallas_call(
        _mod_kernel,
        grid=(depth, 6 * D // MOD_TN),
        in_specs=[
            pl.BlockSpec((COND_ROWS, D), lambda i, j: (0, 0)),
            pl.BlockSpec((1, D, MOD_TN), lambda i, j: (i, 0, j)),
            pl.BlockSpec((1, 1, MOD_TN), lambda i, j: (i, 0, j)),
        ],
        out_specs=pl.BlockSpec((1, COND_ROWS, MOD_TN), lambda i, j: (i, 0, j)),
        out_shape=jax.ShapeDtypeStruct((depth, COND_ROWS, 6 * D), F32),
        compiler_params=pltpu.CompilerParams(
            dimension_semantics=("arbitrary", "arbitrary"), vmem_limit_bytes=VMEM_LIMIT),
        name="adaln",
    )(cond, w_mod, b_mod.reshape(depth, 1, 6 * D))


_ROW_SPEC = pl.BlockSpec((1, D), lambda i: (0, 0))
_STEP_MOD_SPEC = pl.BlockSpec((1, 6, 1, D), lambda i, *_: (_cond_of_tile(i * STEP_TILES), 0, 0, 0))
_STEP_SPEC = pl.BlockSpec((STEP_ROWS, D), lambda i, *_: (i, 0))
_SUB_ROWS = tuple(slice(s * TM, (s + 1) * TM) for s in range(STEP_TILES))


def _load_weight_bf16(w_hbm, w16_ref, stage_ref, sem):
    n = w16_ref.shape[1] // D

    def slab(c):
        return pltpu.make_async_copy(w_hbm.at[:, pl.ds(c * D, D)], stage_ref.at[c % 2], sem.at[c % 2])

    slab(0).start()
    for c in range(n):
        if c + 1 < n:
            slab(c + 1).start()
        slab(c).wait()
        w16_ref[:, c * D:(c + 1) * D] = stage_ref[c % 2].astype(BF16)


def _weight_scratch(n_slabs):
    n_stage = min(n_slabs, 2)
    return [pltpu.VMEM((D, n_slabs * D), BF16), pltpu.VMEM((n_stage, D, D), F32),
            pltpu.SemaphoreType.DMA((n_stage,))]


_ANY_SPEC = pl.BlockSpec(memory_space=pl.ANY)


def _token_tile(x_refs, rows=slice(None), prompt_steps=PROMPT_TILES):
    if len(x_refs) == 1:
        return x_refs[0][rows]
    return jnp.where(pl.program_id(0) < prompt_steps, x_refs[0][rows], x_refs[1][rows])


def _token_specs(x_parts, tiles_per_step=1):
    rows = tiles_per_step * TM
    if len(x_parts) == 1:
        return [pl.BlockSpec((rows, D), lambda i: (i, 0))]
    prompt_steps = T_PROMPT // rows
    return [pl.BlockSpec((rows, D), lambda i: (jnp.minimum(i, prompt_steps - 1), 0)),
            pl.BlockSpec((rows, D), lambda i: (jnp.maximum(i - prompt_steps, 0), 0))]


def _hgrn_proj_kernel(xp_ref, xs_ref, mod_ref, g_ref, w_hbm, q_ref, v_ref, ff_ref, fb_ref, gg_ref,
                      w_ref, stage_ref, sem):
    @pl.when(pl.program_id(0) == 0)
    def _():
        _load_weight_bf16(w_hbm, w_ref, stage_ref, sem)

    h = [_modulate(_token_tile((xp_ref, xs_ref), rows, PROMPT_STEPS),
                   g_ref[...], mod_ref[0, 0], mod_ref[0, 1]).astype(BF16) for rows in _SUB_ROWS]
    for rows, h_tile in zip(_SUB_ROWS, h):
        q_ref[rows] = (_dot(h_tile, w_ref[:, 0 * D:1 * D]) * (DK ** -0.5)).astype(BF16)
        v_ref[rows] = _dot(h_tile, w_ref[:, 1 * D:2 * D]).astype(BF16)
        ff_ref[rows] = _dot(h_tile, w_ref[:, 2 * D:3 * D])
        fb_ref[rows] = _dot(h_tile, w_ref[:, 3 * D:4 * D])
        gg_ref[rows] = _dot(h_tile, w_ref[:, 4 * D:5 * D]).astype(BF16)


def _hgrn_proj(x_parts, mod, norm_g, w_in):
    bf = jax.ShapeDtypeStruct((T, D), BF16)
    f32 = jax.ShapeDtypeStruct((T, D), F32)
    return pl.pallas_call(
        _hgrn_proj_kernel,
        grid=(N_STEPS,),
        in_specs=_token_specs(x_parts, STEP_TILES) + [_STEP_MOD_SPEC, _ROW_SPEC, _ANY_SPEC],
        out_specs=[_STEP_SPEC] * 5,
        out_shape=[bf, bf, f32, f32, bf],
        scratch_shapes=_weight_scratch(5),
        compiler_params=_cparams(),
        name="hgrn_proj",
    )(*x_parts, mod, norm_g, w_in)


GLA_TILES = 4
GLA_ROWS = GLA_TILES * TM
N_GLA_STEPS = N_TILES // GLA_TILES
GLA_PROMPT_STEPS = PROMPT_TILES // GLA_TILES
STEPS_PER_SAMPLE = SAMPLE_LEN // GLA_ROWS
_GLA_SPEC = pl.BlockSpec((GLA_ROWS, D), lambda i: (i, 0))
_GLA_TILE_ROWS = tuple(slice(s * TM, (s + 1) * TM) for s in range(GLA_TILES))


def _gla_step(block, q_ref, v_ref, f_ref, lbl_ref, s0_ref, st_ref, st_tiles_ref, ma_ref, o_ref,
              *, reverse, finish=None):
    is_prompt = block < GLA_PROMPT_STEPS
    rel = block - GLA_PROMPT_STEPS
    edge = (STEPS_PER_SAMPLE - 1) if reverse else 0
    sample_start = jnp.logical_and(rel >= 0, rel % STEPS_PER_SAMPLE == edge)

    @pl.when(is_prompt)
    def _():
        st_ref[...] = jnp.zeros_like(st_ref)

    @pl.when(sample_start)
    def _():
        for h in range(HEADS):
            st_ref[h] = s0_ref[0, 0, h].T

    logits = lbl_ref[...]
    e = jnp.exp(logits - jnp.max(logits, axis=0, keepdims=True))
    lb = e[0:1] / jnp.sum(e, axis=0, keepdims=True)

    ref_idx = (CHUNK - 1 - CHUNK // 2) if reverse else CHUNK // 2
    last_idx = 0 if reverse else CHUNK - 1
    shift = CHUNK.bit_length() - 1

    row = lax.broadcasted_iota(jnp.int32, (TM, TM), 0)
    col = lax.broadcasted_iota(jnp.int32, (TM, TM), 1)
    same = (row >> shift) == (col >> shift)
    ref_row = ((row >> shift) << shift) + ref_idx
    keep = jnp.logical_and(same, (row <= col) if reverse else (row >= col))

    @pl.when(pl.program_id(0) == 0)
    def _():
        at_ref = jnp.logical_and(same, (ref_row <= col) if reverse else (ref_row >= col))
        ma_ref[...] = (keep.astype(F32) - at_ref.astype(F32)).astype(BF16)

    ma = ma_ref[...]

    rr = lax.broadcasted_iota(jnp.int32, (2 * CHUNKS_PER_TILE, TM), 0)
    cc = lax.broadcasted_iota(jnp.int32, (2 * CHUNKS_PER_TILE, TM), 1)
    chunk = rr & (CHUNKS_PER_TILE - 1)
    target = chunk * CHUNK + jnp.where(rr < CHUNKS_PER_TILE, ref_idx, last_idx)
    covered = (cc >= target) if reverse else (cc <= target)
    rm = jnp.logical_and((cc >> shift) == chunk, covered).astype(BF16)

    chunk_rows = [slice(c * CHUNK, (c + 1) * CHUNK) for c in range(CHUNKS_PER_TILE)]
    order = range(CHUNKS_PER_TILE - 1, -1, -1) if reverse else range(CHUNKS_PER_TILE)
    pair_w = 2 * DK
    n_pairs = HEADS // 2
    tile_order = _GLA_TILE_ROWS[::-1] if reverse else _GLA_TILE_ROWS

    def per_chunk_scale(x, scale):
        return jnp.concatenate([x[rs] * scale[c:c + 1] for c, rs in enumerate(chunk_rows)], axis=0)

    def pair_cols(p):
        return slice(p * pair_w, (p + 1) * pair_w)

    def stage_decay(k, p):
        lbp = lb[:, pair_cols(p)]
        f = lbp + (1.0 - lbp) * jax.nn.sigmoid(f_ref[tile_order[k], pair_cols(p)])
        lf_hi, lf_lo = _split_bf16(jnp.log(f))
        z = _dot(ma, lf_hi) + _dot(ma, lf_lo)
        marks = _dot(rm, lf_hi) + _dot(rm, lf_lo)
        return 1.0 - f, z, marks

    def stage_scores(k, p, kk, z, marks):
        ref, last = marks[:CHUNKS_PER_TILE], marks[CHUNKS_PER_TILE:]
        qa32 = q_ref[tile_order[k], pair_cols(p)].astype(F32) * jnp.exp(z)
        kb32 = kk * jnp.exp(-z)
        qa = qa32.astype(BF16)
        kb = kb32.astype(BF16)
        qe = per_chunk_scale(qa32, jnp.exp(ref)).astype(BF16)
        kd = per_chunk_scale(kb32, jnp.exp(last - ref)).astype(BF16)
        v = v_ref[tile_order[k], pair_cols(p)]
        scores, incr = [], []
        for hh in range(2):
            ls = slice(hh * DK, (hh + 1) * DK)
            scores.append(_dot_nt(qa[:, ls], kb[:, ls]))
            kd_h = kd[:, ls]
            kd_blocks = jnp.concatenate(
                [jnp.concatenate([part for part in (jnp.zeros((c * CHUNK, DK), BF16), kd_h[rs],
                                                    jnp.zeros((TM - (c + 1) * CHUNK, DK), BF16))
                                  if part.shape[0]], axis=0)
                 for c, rs in enumerate(chunk_rows)], axis=1)
            incr.append(_dot_tn(v[:, ls], kd_blocks))
        return scores, incr, qe, jnp.exp(last), v

    def stage_output(k, p, scores, incr, qe, decay, v):
        base = tile_order[k].start
        for hh in range(2):
            h = 2 * p + hh
            ls = slice(hh * DK, (hh + 1) * DK)
            hs = slice(h * DK, (h + 1) * DK)
            o_intra = _dot(jnp.where(keep, scores[hh], 0.0).astype(BF16), v[:, ls])
            st = st_ref[h]
            if k > 0:
                st = jnp.where(is_prompt, 0.0, st)
            for c in order:
                rs = chunk_rows[c]
                o_ref[base + rs.start:base + rs.stop, hs] = (
                    o_intra[rs] + _dot_nt(qe[rs, ls], st.astype(BF16)))
                st = st * decay[c:c + 1, ls] + incr[hh][:, c * DK:(c + 1) * DK]
            st_ref[h] = st
            if k < GLA_TILES - 1:
                st_tiles_ref[k, h] = st
        if finish is not None:
            return (tile_order[k], p)

    def stage_finish(k, p, rows, pair):
        finish(rows, pair)

    units = [(k, p) for k in range(GLA_TILES) for p in range(n_pairs)]
    stages = [stage_decay, stage_scores, stage_output] + ([stage_finish] if finish is not None else [])
    carried = [dict() for _ in stages]
    for step in range(len(units) + len(stages) - 1):
        for depth, stage in (enumerate(stages) if reverse else reversed(list(enumerate(stages)))):
            u = step - depth
            if 0 <= u < len(units):
                out = stage(*units[u], *(carried[depth].pop(u) if depth else ()))
                if depth + 1 < len(stages):
                    carried[depth + 1][u] = out


def _store_prompt_states(block, st_ref, st_tiles_ref, snew_ref, *, reverse, sfwd_ref=None):
    @pl.when(block < GLA_PROMPT_STEPS)
    def _():
        for k in range(GLA_TILES):
            tile = GLA_TILES - 1 - k if reverse else k
            for h in range(HEADS):
                state = (st_tiles_ref[k, h] if k < GLA_TILES - 1 else st_ref[h]).T
                if sfwd_ref is None:
                    snew_ref[tile, h] = state
                else:
                    snew_ref[tile, 1, h] = state
        if sfwd_ref is not None:
            snew_ref[:, 0] = sfwd_ref[...]


def _gla_fwd_kernel(q_ref, v_ref, f_ref, lbl_ref, s0_ref, o_ref, snew_ref, st_ref, st_tiles_ref, ma_ref):
    block = pl.program_id(0)
    _gla_step(block, q_ref, v_ref, f_ref, lbl_ref, s0_ref, st_ref, st_tiles_ref, ma_ref, o_ref,
              reverse=False)
    _store_prompt_states(block, st_ref, st_tiles_ref, snew_ref, reverse=False)


def _gla_bwd_kernel(q_ref, v_ref, f_ref, lbl_ref, s0_ref, of_ref, gg_ref, ng_ref, sfwd_ref,
                    a_ref, snew_ref, st_ref, st_tiles_ref, ma_ref, ob_ref):
    block = N_GLA_STEPS - 1 - pl.program_id(0)

    def finish(rows, p):
        for h in (2 * p, 2 * p + 1):
            hs = slice(h * DV, (h + 1) * DV)
            o = of_ref[rows, hs] + ob_ref[rows, hs]
            a_ref[rows, hs] = (_rmsnorm(o, ng_ref[...]) * _silu(gg_ref[rows, hs].astype(F32))).astype(BF16)

    _gla_step(block, q_ref, v_ref, f_ref, lbl_ref, s0_ref, st_ref, st_tiles_ref, ma_ref, ob_ref,
              reverse=True, finish=finish)
    _store_prompt_states(block, st_ref, st_tiles_ref, snew_ref, reverse=True, sfwd_ref=sfwd_ref)


def _state_specs(direction, block_of_step):
    sample = lambda i: jnp.clip((block_of_step(i) - GLA_PROMPT_STEPS) // STEPS_PER_SAMPLE, 0, N_SAMPLE_SEQ - 1)
    s0 = pl.BlockSpec((1, 1, HEADS, DK, DV), lambda i: (sample(i), direction, 0, 0, 0))
    snew = pl.BlockSpec((GLA_TILES, HEADS, DK, DV),
                        lambda i: (jnp.minimum(block_of_step(i), GLA_PROMPT_STEPS - 1), 0, 0, 0))
    return s0, snew


_GLA_STATE_SCRATCH = [pltpu.VMEM((HEADS, DV, DK), F32), pltpu.VMEM((GLA_TILES - 1, HEADS, DV, DK), F32),
                      pltpu.VMEM((TM, TM), BF16)]


def _gla_fwd(q, v, ff, lb_logits, s0):
    s0_spec, snew_spec = _state_specs(0, lambda i: i)
    return pl.pallas_call(
        _gla_fwd_kernel,
        grid=(N_GLA_STEPS,),
        in_specs=[_GLA_SPEC, _GLA_SPEC, _GLA_SPEC,
                  pl.BlockSpec(lb_logits.shape, lambda i: (0, 0)), s0_spec],
        out_specs=[_GLA_SPEC, snew_spec],
        out_shape=[jax.ShapeDtypeStruct((T, D), F32),
                   jax.ShapeDtypeStruct((N_PROMPT_SEQ, HEADS, DK, DV), F32)],
        scratch_shapes=_GLA_STATE_SCRATCH,
        compiler_params=_cparams(),
        name="gla_fwd",
    )(q, v, ff, lb_logits, s0)


def _gla_bwd(q, v, fb, lb_logits, s0, o_f, gg, head_norm_g, s_fwd):
    rev = lambda i: N_GLA_STEPS - 1 - i
    block = pl.BlockSpec((GLA_ROWS, D), lambda i: (rev(i), 0))
    s0_spec, sfwd_spec = _state_specs(1, rev)
    prompt_block = lambda i: jnp.minimum(rev(i), GLA_PROMPT_STEPS - 1)
    return pl.pallas_call(
        _gla_bwd_kernel,
        grid=(N_GLA_STEPS,),
        in_specs=[block, block, block, pl.BlockSpec(lb_logits.shape, lambda i: (0, 0)), s0_spec,
                  block, block, pl.BlockSpec((1, DV), lambda i: (0, 0)), sfwd_spec],
        out_specs=[block, pl.BlockSpec((GLA_TILES, 2, HEADS, DK, DV), lambda i: (prompt_block(i), 0, 0, 0, 0))],
        out_shape=[jax.ShapeDtypeStruct((T, D), BF16),
                   jax.ShapeDtypeStruct((N_PROMPT_SEQ, 2, HEADS, DK, DV), F32)],
        scratch_shapes=_GLA_STATE_SCRATCH + [pltpu.VMEM((GLA_ROWS, D), F32)],
        compiler_params=_cparams(),
        name="gla_bwd",
    )(q, v, fb, lb_logits, s0, o_f, gg, head_norm_g, s_fwd)


def _conv_mixer_kernel(x_ref, mod_ref, g_ref, w_hbm, cw_ref, a_ref, w_ref, stage_ref, sem):
    @pl.when(pl.program_id(0) == 0)
    def _():
        _load_weight_bf16(w_hbm, w_ref, stage_ref, sem)

    seg = jnp.where(pl.program_id(0) < PROMPT_TILES // CONV_TILES, PROMPT_LEN, GRID_W)
    pos = lax.broadcasted_iota(jnp.int32, (TM, 1), 0) & (seg - 1)

    def project(rows):
        h = _modulate(x_ref[rows], g_ref[...], mod_ref[0, 0], mod_ref[0, 1]).astype(BF16)
        return [_dot(h, w_ref[:, k * D:(k + 1) * D]) for k in range(3)]

    def convolve(rows, bg, cg, x_in):
        u = cg * x_in
        prev = jnp.where(pos == 0, 0.0, pltpu.roll(u, 1, axis=0))
        nxt = jnp.where(pos == seg - 1, 0.0, pltpu.roll(u, TM - 1, axis=0))
        conv = cw_ref[0:1] * prev + cw_ref[1:2] * u + cw_ref[2:3] * nxt
        a_ref[rows] = (bg * conv).astype(BF16)

    tiles = [slice(s * TM, (s + 1) * TM) for s in range(CONV_TILES)]
    projected = [project(rows) for rows in tiles]
    for rows, parts in zip(tiles, projected):
        convolve(rows, *parts)


def _conv_mixer(x, mod, norm_g, w_in, conv_w):
    block = pl.BlockSpec((CONV_TILES * TM, D), lambda i: (i, 0))
    return pl.pallas_call(
        _conv_mixer_kernel,
        grid=(N_TILES // CONV_TILES,),
        in_specs=[block, pl.BlockSpec((1, 6, 1, D), lambda i: (_cond_of_tile(i * CONV_TILES), 0, 0, 0)),
                  _ROW_SPEC, _ANY_SPEC, pl.BlockSpec((3, D), lambda i: (0, 0))],
        out_specs=block,
        out_shape=jax.ShapeDtypeStruct((T, D), BF16),
        scratch_shapes=_weight_scratch(3),
        compiler_params=_cparams(),
        name="conv_mixer",
    )(x, mod, norm_g, w_in, conv_w)


ROUTE_ROWS = 32


def _first_member(vals, target):
    idx = jnp.full_like(target, float(EPG - 1))
    for j in range(EPG - 2, -1, -1):
        idx = jnp.where(vals[j] == target, float(j), idx)
    return idx


def _router_logits(h2, rwt_ref):
    h_hi, h_lo = _split_bf16(h2)
    w_hi, w_lo = _split_bf16(rwt_ref[...])
    return _dot_nt(w_hi, h_hi) + (_dot_nt(w_hi, h_lo) + _dot_nt(w_lo, h_hi))


def _select_experts(logits, rb_ref):
    scores = jax.nn.sigmoid(logits)
    sel = scores + rb_ref[...]
    member = [sel[j * N_GROUPS:(j + 1) * N_GROUPS] for j in range(EPG)]
    m1 = functools.reduce(jnp.maximum, member)
    i1 = _first_member(member, m1)
    rest = [jnp.where(i1 == float(j), -jnp.inf, member[j]) for j in range(EPG)]
    m2 = functools.reduce(jnp.maximum, rest)
    i2 = _first_member(rest, m2)
    group_score = m1 + m2

    best, grp, first, second = group_score[0:1], jnp.zeros((1, TM), F32), i1[0:1], i2[0:1]
    for g in range(1, N_GROUPS):
        upd = group_score[g:g + 1] > best
        best = jnp.where(upd, group_score[g:g + 1], best)
        grp = jnp.where(upd, float(g), grp)
        first = jnp.where(upd, i1[g:g + 1], first)
        second = jnp.where(upd, i2[g:g + 1], second)
    a = jnp.minimum(first, second)
    b = jnp.maximum(first, second)
    row = lax.broadcasted_iota(jnp.int32, (N_EXPERTS, TM), 0).astype(F32)
    s_lo = jnp.sum(jnp.where(row == a * N_GROUPS + grp, scores, 0.0), axis=0, keepdims=True)
    s_hi = jnp.sum(jnp.where(row == b * N_GROUPS + grp, scores, 0.0), axis=0, keepdims=True)
    tot = s_lo + s_hi
    pair = jnp.where(a == 0.0, jnp.where(b == 1.0, 0.0, jnp.where(b == 2.0, 3.0, 4.0)),
                     jnp.where(a == 1.0, jnp.where(b == 2.0, 5.0, 1.0), 2.0))
    return grp * N_PAIRS + pair, s_lo / tot, s_hi / tot


def _post_mixer_kernel(*refs, n_x):
    a_ref, x_refs = refs[0], refs[1:1 + n_x]
    (mod_ref, g2_ref, wo_hbm, rwt_ref, rb_ref,
     x1_ref, gates_ref, bucket_ref, rank_ref, cnt_ref, carry_ref, earlier_ref, wo_ref, stage_ref, sem) = refs[1 + n_x:]

    @pl.when(pl.program_id(0) == 0)
    def _():
        _load_weight_bf16(wo_hbm, wo_ref, stage_ref, sem)
        carry_ref[...] = jnp.zeros_like(carry_ref)
        s_idx = lax.broadcasted_iota(jnp.int32, (TM, TM), 0)
        t_idx = lax.broadcasted_iota(jnp.int32, (TM, TM), 1)
        earlier_ref[...] = (s_idx < t_idx).astype(BF16)

    def stage_mix(s):
        rows = slice(s * TM, (s + 1) * TM)
        x1 = (_token_tile(x_refs, rows, PROMPT_TILES // ROUTE_TILES)
              + mod_ref[0, 2] * _dot(a_ref[rows], wo_ref[...]))
        x1_ref[rows] = x1
        return _modulate(x1, g2_ref[...], mod_ref[0, 3], mod_ref[0, 4])

    def stage_rank(s, bucket, g_lo, g_hi):
        onehot = lax.broadcasted_iota(jnp.int32, (ROUTE_ROWS, TM), 0).astype(F32) == bucket
        before = _dot(onehot.astype(BF16), earlier_ref[...])
        oh = onehot.astype(F32)
        carry = carry_ref[...]
        rank = jnp.sum(oh * (before + carry[:, 0:1]), axis=0, keepdims=True)
        carry_ref[...] = carry + jnp.sum(oh, axis=1, keepdims=True)
        bucket_ref[s] = bucket.astype(jnp.int32)
        rank_ref[s] = rank.astype(jnp.int32)
        grow = lax.broadcasted_iota(jnp.int32, (LANES, TM), 0)
        gates_ref[s * TM:(s + 1) * TM] = jnp.where(grow == 0, g_lo, jnp.where(grow == 1, g_hi, 0.0)).T

    tiles = range(ROUTE_TILES)
    h2 = [stage_mix(s) for s in tiles]
    logits = [_router_logits(h2[s], rwt_ref) for s in tiles]
    picks = [_select_experts(logits[s], rb_ref) for s in tiles]
    for s in tiles:
        stage_rank(s, *picks[s])
    cnt_ref[...] = carry_ref[...]


def _post_mixer(a, x_parts, mod, norm_g2, w_out, router_wt, router_bt):
    rows = ROUTE_TILES * TM
    idx_spec = pl.BlockSpec((ROUTE_TILES, 1, TM), lambda i: (i, 0, 0))
    return pl.pallas_call(
        functools.partial(_post_mixer_kernel, n_x=len(x_parts)),
        grid=(N_TILES // ROUTE_TILES,),
        in_specs=[pl.BlockSpec((rows, D), lambda i: (i, 0))] + _token_specs(x_parts, ROUTE_TILES) + [
            pl.BlockSpec((1, 6, 1, D), lambda i: (_cond_of_tile(i * ROUTE_TILES), 0, 0, 0)), _ROW_SPEC,
            _ANY_SPEC,
            pl.BlockSpec((N_EXPERTS, D), lambda i: (0, 0)),
            pl.BlockSpec((N_EXPERTS, 1), lambda i: (0, 0))],
        out_specs=[pl.BlockSpec((rows, D), lambda i: (i, 0)), pl.BlockSpec((rows, LANES), lambda i: (i, 0)),
                   idx_spec, idx_spec, pl.BlockSpec((ROUTE_ROWS, LANES), lambda i: (0, 0))],
        out_shape=[jax.ShapeDtypeStruct((T, D), F32), jax.ShapeDtypeStruct((T, LANES), F32),
                   jax.ShapeDtypeStruct((N_TILES, 1, TM), jnp.int32),
                   jax.ShapeDtypeStruct((N_TILES, 1, TM), jnp.int32),
                   jax.ShapeDtypeStruct((ROUTE_ROWS, LANES), F32)],
        scratch_shapes=[pltpu.VMEM((ROUTE_ROWS, LANES), F32), pltpu.VMEM((TM, TM), BF16)] + _weight_scratch(1),
        compiler_params=_cparams(),
        name="post_mixer",
    )(a, *x_parts, mod, norm_g2, w_out, router_wt, router_bt)


_MOVE_SPEC = pl.BlockSpec((MOVE_ROWS, D), lambda i, *_: (i, 0))
_MOVE_MOD_SPEC = pl.BlockSpec((1, 6, 1, D), lambda i, *_: (_cond_of_tile(i * MOVE_TILES), 0, 0, 0))


def _smem_step_spec(step_of_step):
    return pl.BlockSpec((MOVE_TILES, 1, TM), lambda i, *_: (step_of_step(i), 0, 0), memory_space=pltpu.SMEM)


def _start_step_rows(make_copy):
    for tile in range(MOVE_TILES):
        for r in range(TM):
            make_copy(tile, r).start(priority=r % 2)


def _dispatch_kernel(zero_from_ref, slot_ref, x1_ref, mod_ref, g2_ref, gates_ref,
                     xs_ref, buf_ref, zero_ref, sems, zero_sem):
    i = pl.program_id(0)
    cur = i % 2

    @pl.when(i == 0)
    def _():
        zero_ref[...] = jnp.zeros_like(zero_ref)

        def for_each_chunk(act):
            def tile(j, carry):
                def chunk(c, inner):
                    row = pl.multiple_of(j * TM_E + c * ZERO_ROWS, ZERO_ROWS)
                    act(pltpu.make_async_copy(zero_ref, xs_ref.at[pl.ds(row, ZERO_ROWS)], zero_sem))
                    return inner
                return lax.fori_loop(zero_from_ref[j], TM_E // ZERO_ROWS, chunk, carry)
            lax.fori_loop(0, N_TILES_E, tile, 0)

        for_each_chunk(lambda copy: copy.start())
        for_each_chunk(lambda copy: copy.wait())

    def wait_step(b):
        pltpu.make_async_copy(buf_ref.at[b], xs_ref.at[pl.ds(0, MOVE_ROWS)], sems.at[b]).wait()

    h2 = _modulate(x1_ref[...], g2_ref[...], mod_ref[0, 3], mod_ref[0, 4])
    gates = gates_ref[...]

    def send(b):
        buf_ref[b, :, :D] = h2
        buf_ref[b, :, D:] = gates
        _start_step_rows(lambda tile, r: pltpu.make_async_copy(
            buf_ref.at[b, pl.ds(tile * TM + r, 1)], xs_ref.at[pl.ds(slot_ref[tile, 0, r], 1)], sems.at[b]))

        @pl.when(i > 0)
        def _():
            wait_step(1 - b)

        @pl.when(i == N_MOVE_STEPS - 1)
        def _():
            wait_step(b)

    for b in range(2):
        pl.when(cur == b)(functools.partial(send, b))


def _dispatch(zero_from, slot, x1, mod, norm_g2, gates):
    return pl.pallas_call(
        _dispatch_kernel,
        grid_spec=pltpu.PrefetchScalarGridSpec(
            num_scalar_prefetch=1,
            grid=(N_MOVE_STEPS,),
            in_specs=[_smem_step_spec(lambda i: i), _MOVE_SPEC, _MOVE_MOD_SPEC,
                      pl.BlockSpec((1, D), lambda i, *_: (0, 0)),
                      pl.BlockSpec((MOVE_ROWS, LANES), lambda i, *_: (i, 0))],
            out_specs=pl.BlockSpec(memory_space=pl.ANY),
            scratch_shapes=[pltpu.VMEM((2, MOVE_ROWS, XW), F32), pltpu.VMEM((ZERO_ROWS, XW), F32),
                            pltpu.SemaphoreType.DMA((2,)), pltpu.SemaphoreType.DMA(())],
        ),
        out_shape=jax.ShapeDtypeStruct((S_ROWS, XW), F32),
        compiler_params=_cparams(),
        name="moe_dispatch",
    )(zero_from, slot, x1, mod, norm_g2, gates)


_W_SHAPES = ((D, D_FF), (D, D_FF), (D_FF, D))


def _expert_kernel(ea_ref, eb_ref, next_a_ref, next_b_ref, a_higher_ref, nrows_ref, xblock_ref,
                   x_ref, wg_hbm, wu_hbm, wd_hbm, y_ref, *scratch, layer):
    stage, w16, sems = scratch[0:3], scratch[3:6], scratch[6]
    j = pl.program_id(0)
    n = nrows_ref[j]
    prev = jnp.maximum(j - 1, 0)

    def fetch(slot, expert):
        return [pltpu.make_async_copy(w.at[layer, expert], s.at[slot], sems.at[slot])
                for w, s in zip((wg_hbm, wu_hbm, wd_hbm), stage)]

    for slot, e_ref, next_ref in ((0, ea_ref, next_a_ref), (1, eb_ref, next_b_ref)):
        @pl.when(j == 0)
        def _(slot=slot, e_ref=e_ref):
            for copy in fetch(slot, e_ref[0]):
                copy.start()

        @pl.when(jnp.logical_or(j == 0, e_ref[j] != e_ref[prev]))
        def _(slot=slot, e_ref=e_ref, next_ref=next_ref):
            for copy in fetch(slot, e_ref[j]):
                copy.wait()
            for s, d in zip(stage, w16):
                d[slot] = s[slot].astype(BF16)

            @pl.when(next_ref[j] >= 0)
            def _():
                for copy in fetch(slot, next_ref[j]):
                    copy.start(priority=1)

    def run(rows):
        valid = lax.broadcasted_iota(jnp.int32, (rows, 1), 0) < n
        xe = jnp.where(valid, x_ref[:rows], 0.0)
        x = xe[:, :D].astype(BF16)
        g_lo, g_hi = xe[:, D:D + 1], xe[:, D + 1:D + 2]
        a_higher = a_higher_ref[j] != 0

        def ffn(slot, gate):
            wg_ref, wu_ref, wd_ref = w16
            act = _silu(_dot(x, wg_ref[slot])) * _dot(x, wu_ref[slot]) * gate
            return _dot(act.astype(BF16), wd_ref[slot])

        y_ref[:rows] = (ffn(0, jnp.where(a_higher, g_hi, g_lo))
                        + ffn(1, jnp.where(a_higher, g_lo, g_hi)))
        if rows < TM_E:
            y_ref[rows:] = jnp.zeros((TM_E - rows, D), F32)

    half = TM_E // 2
    pl.when(n > half)(functools.partial(run, TM_E))
    pl.when(jnp.logical_and(n > 0, n <= half))(functools.partial(run, half))

    @pl.when(n == 0)
    def _():
        y_ref[...] = jnp.zeros_like(y_ref)


def _experts(tile_ea, tile_eb, tile_next_a, tile_next_b, tile_a_higher, tile_rows, tile_xblock, xs,
             w_gate, w_up, w_down, layer):
    any_spec = pl.BlockSpec(memory_space=pl.ANY)
    return pl.pallas_call(
        functools.partial(_expert_kernel, layer=layer),
        grid_spec=pltpu.PrefetchScalarGridSpec(
            num_scalar_prefetch=7,
            grid=(N_TILES_E,),
            in_specs=[pl.BlockSpec((TM_E, XW), lambda j, *plan: (plan[-1][j], 0)), any_spec, any_spec, any_spec],
            out_specs=pl.BlockSpec((TM_E, D), lambda j, *_: (j, 0)),
            scratch_shapes=[pltpu.VMEM((2,) + s, F32) for s in _W_SHAPES]
                           + [pltpu.VMEM((2,) + s, BF16) for s in _W_SHAPES]
                           + [pltpu.SemaphoreType.DMA((2,))],
        ),
        out_shape=jax.ShapeDtypeStruct((S_ROWS, D), F32),
        compiler_params=_cparams(),
        name="moe_experts",
    )(tile_ea, tile_eb, tile_next_a, tile_next_b, tile_a_higher, tile_rows, tile_xblock, xs,
      w_gate, w_up, w_down)


def _combine_kernel(slot_ref, next_slot_ref, x1_ref, mod_ref, gf_ref, y_ref, *rest, final):
    out_refs, (buf_ref, sems) = rest[:-2], rest[-2:]
    i = pl.program_id(0)
    cur = i % 2

    def gather(slots, b):
        _start_step_rows(lambda tile, r: pltpu.make_async_copy(
            y_ref.at[pl.ds(slots[tile, 0, r], 1)], buf_ref.at[b, pl.ds(tile * TM + r, 1)], sems.at[b]))

    @pl.when(i == 0)
    def _():
        gather(slot_ref, 0)

    def combine(b):
        @pl.when(i + 1 < N_MOVE_STEPS)
        def _():
            gather(next_slot_ref, 1 - b)

        pltpu.make_async_copy(y_ref.at[pl.ds(0, MOVE_ROWS)], buf_ref.at[b], sems.at[b]).wait()
        x2 = x1_ref[...] + mod_ref[0, 5] * buf_ref[b]
        if not final:
            out_refs[0][...] = x2
            return
        y = _rmsnorm(x2, gf_ref[...])
        yp_ref, ys_ref = out_refs

        @pl.when(i < T_PROMPT // MOVE_ROWS)
        def _():
            yp_ref[...] = y

        @pl.when(i >= T_PROMPT // MOVE_ROWS)
        def _():
            ys_ref[...] = y

    for b in range(2):
        pl.when(cur == b)(functools.partial(combine, b))


def _combine(slot, x1, mod, final_g, y_sorted, *, final):
    if final:
        out_specs = _token_specs((None, None), MOVE_TILES)
        out_shape = [jax.ShapeDtypeStruct((T_PROMPT, D), F32), jax.ShapeDtypeStruct((T_SAMPLE, D), F32)]
    else:
        out_specs = [_MOVE_SPEC]
        out_shape = [jax.ShapeDtypeStruct((T, D), F32)]
    return pl.pallas_call(
        functools.partial(_combine_kernel, final=final),
        grid=(N_MOVE_STEPS,),
        in_specs=[_smem_step_spec(lambda i: i), _smem_step_spec(lambda i: jnp.minimum(i + 1, N_MOVE_STEPS - 1)),
                  _MOVE_SPEC, _MOVE_MOD_SPEC, _ROW_SPEC, pl.BlockSpec(memory_space=pl.ANY)],
        out_specs=out_specs,
        scratch_shapes=[pltpu.VMEM((2, MOVE_ROWS, D), F32), pltpu.SemaphoreType.DMA((2,))],
        out_shape=out_shape,
        compiler_params=_cparams(),
        name="moe_combine_final" if final else "moe_combine",
    )(slot, slot, x1, mod, final_g, y_sorted)


def _plan_kernel(cnt_ref, bucket_ref, rank_ref, slot_ref,
                 ea_ref, eb_ref, next_a_ref, next_b_ref, a_higher_ref, rows_ref, xblock_ref, zero_from_ref):
    slot = rank_ref[...]
    bucket = bucket_ref[...]
    tile0 = jnp.int32(0)
    last = [jnp.int32(0)] * 3
    for b in range(N_BUCKETS):
        n = cnt_ref[b]
        tiles = (n + (TM_E - 1)) // TM_E
        pair = b % N_PAIRS
        per_tile = ((b // N_PAIRS) * EPG + _SLOT_A[pair], (b // N_PAIRS) * EPG + _SLOT_B[pair],
                    _SLOT_A_IS_HIGHER[pair])

        def fill(k, carry, n=n, tile0=tile0, per_tile=per_tile):
            j = tile0 + k
            live = jnp.minimum(n - k * TM_E, TM_E)
            rows_ref[j] = live
            xblock_ref[j] = j
            zero_from_ref[j] = live // ZERO_ROWS
            for ref, value in zip((ea_ref, eb_ref, a_higher_ref), per_tile):
                ref[j] = jnp.int32(value)
            return carry

        lax.fori_loop(0, tiles, fill, 0)
        slot = slot + jnp.where(bucket == b, tile0 * TM_E, 0)
        last = [jnp.where(tiles > 0, value, old) for value, old in zip(per_tile, last)]
        tile0 = tile0 + tiles

    def unused(j, carry):
        rows_ref[j] = jnp.int32(0)
        xblock_ref[j] = jnp.maximum(tile0 - 1, 0)
        zero_from_ref[j] = jnp.int32(0)
        for ref, value in zip((ea_ref, eb_ref, a_higher_ref), last):
            ref[j] = value
        return carry

    lax.fori_loop(tile0, N_TILES_E, unused, 0)

    for e_ref, next_ref in ((ea_ref, next_a_ref), (eb_ref, next_b_ref)):
        next_ref[N_TILES_E - 1] = jnp.int32(-1)

        def backward(k, carry, e_ref=e_ref, next_ref=next_ref):
            j = N_TILES_E - 2 - k
            next_ref[j] = jnp.where(e_ref[j] == e_ref[j + 1], next_ref[j + 1], e_ref[j + 1])
            return carry

        lax.fori_loop(0, N_TILES_E - 1, backward, 0)
    slot_ref[...] = slot


def _plan(counts, bucket, rank):
    whole = pl.BlockSpec(bucket.shape, lambda i, *_: (0, 0, 0))
    smem = pl.BlockSpec(memory_space=pltpu.SMEM)
    per_tile = jax.ShapeDtypeStruct((N_TILES_E,), jnp.int32)
    slot, *tile_plan = pl.pallas_call(
        _plan_kernel,
        grid_spec=pltpu.PrefetchScalarGridSpec(
            num_scalar_prefetch=1, grid=(1,), in_specs=[whole, whole], out_specs=[whole] + [smem] * 8),
        out_shape=[jax.ShapeDtypeStruct(bucket.shape, jnp.int32)] + [per_tile] * 8,
        compiler_params=_cparams(),
        name="moe_plan",
    )(counts[:, 0].astype(jnp.int32), bucket, rank)
    return slot, tile_plan


def _moe(a, x_parts, mod, norm_g2, w_out, router_wt, router_bt, w_gate, w_up, w_down, layer, final_g,
         *, final):
    x1, gates, bucket, rank, counts = _post_mixer(a, x_parts, mod, norm_g2, w_out, router_wt, router_bt)
    slot, (*expert_plan, zero_from) = _plan(counts, bucket, rank)
    xs = _dispatch(zero_from, slot, x1, mod, norm_g2, gates)
    ys = _experts(*expert_plan, xs, w_gate, w_up, w_down, layer)
    return _combine(slot, x1, mod, final_g, ys, final=final)


def kernel(x_prompt, x_sample, state_hgrn, c, c_ctx, w_mod, b_mod, norm_mix_g, norm_ffn_g, norm_final_g,
           hgrn_w_in, hgrn_lb_logits, hgrn_norm_g, hgrn_w_out, conv_w_in, conv_w, conv_w_out,
           router_w, router_b, moe_w_gate, moe_w_up, moe_w_down):
    x_parts = (x_prompt.reshape(T_PROMPT, D), x_sample.reshape(T_SAMPLE, D))
    cond = jnp.concatenate([c_ctx[None], c, jnp.zeros((COND_ROWS - N_COND, D), F32)], axis=0)
    mods = _adaln(cond, w_mod, b_mod)[:, :N_COND].reshape(2, N_COND, 6, 1, D)
    member_major = lambda w: w.reshape(N_GROUPS, EPG, -1).transpose(1, 0, 2).reshape(N_EXPERTS, -1)
    rwt = member_major(router_w.T)
    rbt = member_major(router_b.reshape(N_EXPERTS, 1))
    final_g = norm_final_g.reshape(1, D)

    q, v, ff, fb, gg = _hgrn_proj(x_parts, mods[0], norm_mix_g[0:1], hgrn_w_in[0])
    s0 = state_hgrn[:, 0]
    o_f, s_f = _gla_fwd(q, v, ff, hgrn_lb_logits, s0)
    a, new_state = _gla_bwd(q, v, fb, hgrn_lb_logits, s0, o_f, gg, hgrn_norm_g[0:1], s_f)
    (x,) = _moe(a, x_parts, mods[0], norm_ffn_g[0:1], hgrn_w_out[0], rwt, rbt,
                moe_w_gate, moe_w_up, moe_w_down, 0, final_g, final=False)

    a = _conv_mixer(x, mods[1], norm_mix_g[1:2], conv_w_in[0], conv_w[0])
    y_p, y_s = _moe(a, (x,), mods[1], norm_ffn_g[1:2], conv_w_out[0], rwt, rbt,
                    moe_w_gate, moe_w_up, moe_w_down, 1, final_g, final=True)

    return (y_p.reshape(N_PROMPT_SEQ, PROMPT_LEN, D), y_s.reshape(N_SAMPLE_SEQ, SAMPLE_LEN, D),
            new_state[:, None])
```

```python
import functools

import jax
import jax.numpy as jnp
from jax import lax
from jax.experimental import pallas as pl
from jax.experimental.pallas import tpu as pltpu

F32 = jnp.float32
BF16 = jnp.bfloat16

D = 1024
N_PROMPT_SEQ = 16
PROMPT_LEN = 256
N_SAMPLE_SEQ = 2
SAMPLE_LEN = 4096
GRID_W = 64
T_PROMPT = N_PROMPT_SEQ * PROMPT_LEN
T_SAMPLE = N_SAMPLE_SEQ * SAMPLE_LEN
T = T_PROMPT + T_SAMPLE
N_COND = 1 + N_SAMPLE_SEQ
COND_ROWS = 8

HEADS = 8
DK = 128
DV = 128
CHUNK = 64
N_EXPERTS = 16
N_GROUPS = 4
EPG = 4
N_PAIRS = 6
N_BUCKETS = N_GROUPS * N_PAIRS
D_FF = 512
EPS = 1e-6

TM = 256
N_TILES = T // TM
PROMPT_TILES = T_PROMPT // TM
TILES_PER_SAMPLE = SAMPLE_LEN // TM
CHUNKS_PER_TILE = TM // CHUNK
STEP_TILES = 2
STEP_ROWS = STEP_TILES * TM
N_STEPS = N_TILES // STEP_TILES
PROMPT_STEPS = PROMPT_TILES // STEP_TILES
MOVE_TILES = 2
MOVE_ROWS = MOVE_TILES * TM
N_MOVE_STEPS = N_TILES // MOVE_TILES
ROUTE_TILES = 4
CONV_TILES = 4
TM_E = 288
N_TILES_E = T // TM_E + N_BUCKETS
S_ROWS = N_TILES_E * TM_E
ZERO_ROWS = 32
XW = D + 128
LANES = 128
MOD_TN = 3072

V7X_VMEM_BYTES = 64 * 1024 * 1024
VMEM_LIMIT = V7X_VMEM_BYTES * 7 // 8

_SLOT_A = (0, 3, 3, 0, 0, 1)
_SLOT_B = (1, 1, 2, 2, 3, 2)
_SLOT_A_IS_HIGHER = tuple(int(a > b) for a, b in zip(_SLOT_A, _SLOT_B))


def _cparams():
    return pltpu.CompilerParams(dimension_semantics=("arbitrary",), vmem_limit_bytes=VMEM_LIMIT)


def _cond_of_tile(t):
    return jnp.where(t < PROMPT_TILES, 0, 1 + (t - PROMPT_TILES) // TILES_PER_SAMPLE)


def _rmsnorm(x, g):
    ms = jnp.mean(x * x, axis=-1, keepdims=True)
    return (x * lax.rsqrt(ms + EPS)) * g


def _modulate(x, g, shift, scale):
    return _rmsnorm(x, g) * (1.0 + scale) + shift


def _silu(x):
    return x * jax.nn.sigmoid(x)


def _dot(a, b):
    return jnp.dot(a, b, preferred_element_type=F32)


def _dot_nt(a, b):
    return lax.dot_general(a, b, (((1,), (1,)), ((), ())), preferred_element_type=F32)


def _dot_tn(a, b):
    return lax.dot_general(a, b, (((0,), (0,)), ((), ())), preferred_element_type=F32)


def _split_bf16(x):
    hi = x.astype(BF16)
    lo = (x - hi.astype(F32)).astype(BF16)
    return hi, lo


def _mod_kernel(cond_ref, w_ref, b_ref, o_ref):
    s = _silu(cond_ref[...])
    o_ref[0] = _dot(s.astype(BF16), w_ref[0].astype(BF16)) + b_ref[0]


def _adaln(cond, w_mod, b_mod):
    depth = w_mod.shape[0]
    return pl.pallas_call(
        _mod_kernel,
        grid=(depth, 6 * D // MOD_TN),
        in_specs=[
            pl.BlockSpec((COND_ROWS, D), lambda i, j: (0, 0)),
            pl.BlockSpec((1, D, MOD_TN), lambda i, j: (i, 0, j)),
            pl.BlockSpec((1, 1, MOD_TN), lambda i, j: (i, 0, j)),
        ],
        out_specs=pl.BlockSpec((1, COND_ROWS, MOD_TN), lambda i, j: (i, 0, j)),
        out_shape=jax.ShapeDtypeStruct((depth, COND_ROWS, 6 * D), F32),
        compiler_params=pltpu.CompilerParams(
            dimension_semantics=("arbitrary", "arbitrary"), vmem_limit_bytes=VMEM_LIMIT),
        name="adaln",
    )(cond, w_mod, b_mod.reshape(depth, 1, 6 * D))


_ROW_SPEC = pl.BlockSpec((1, D), lambda i: (0, 0))
_STEP_MOD_SPEC = pl.BlockSpec((1, 6, 1, D), lambda i, *_: (_cond_of_tile(i * STEP_TILES), 0, 0, 0))
_STEP_SPEC = pl.BlockSpec((STEP_ROWS, D), lambda i, *_: (i, 0))
_SUB_ROWS = tuple(slice(s * TM, (s + 1) * TM) for s in range(STEP_TILES))


def _load_weight_bf16(w_hbm, w16_ref, stage_ref, sem):
    n = w16_ref.shape[1] // D

    def slab(c):
        return pltpu.make_async_copy(w_hbm.at[:, pl.ds(c * D, D)], stage_ref.at[c % 2], sem.at[c % 2])

    slab(0).start()
    for c in range(n):
        if c + 1 < n:
            slab(c + 1).start()
        slab(c).wait()
        w16_ref[:, c * D:(c + 1) * D] = stage_ref[c % 2].astype(BF16)


def _weight_scratch(n_slabs):
    n_stage = min(n_slabs, 2)
    return [pltpu.VMEM((D, n_slabs * D), BF16), pltpu.VMEM((n_stage, D, D), F32),
            pltpu.SemaphoreType.DMA((n_stage,))]


_ANY_SPEC = pl.BlockSpec(memory_space=pl.ANY)


def _token_tile(x_refs, rows=slice(None), prompt_steps=PROMPT_TILES):
    if len(x_refs) == 1:
        return x_refs[0][rows]
    return jnp.where(pl.program_id(0) < prompt_steps, x_refs[0][rows], x_refs[1][rows])


def _token_specs(x_parts, tiles_per_step=1):
    rows = tiles_per_step * TM
    if len(x_parts) == 1:
        return [pl.BlockSpec((rows, D), lambda i: (i, 0))]
    prompt_steps = T_PROMPT // rows
    return [pl.BlockSpec((rows, D), lambda i: (jnp.minimum(i, prompt_steps - 1), 0)),
            pl.BlockSpec((rows, D), lambda i: (jnp.maximum(i - prompt_steps, 0), 0))]


def _hgrn_proj_kernel(xp_ref, xs_ref, mod_ref, g_ref, w_hbm, q_ref, v_ref, ff_ref, fb_ref, gg_ref,
                      w_ref, stage_ref, sem):
    @pl.when(pl.program_id(0) == 0)
    def _():
        _load_weight_bf16(w_hbm, w_ref, stage_ref, sem)

    h = [_modulate(_token_tile((xp_ref, xs_ref), rows, PROMPT_STEPS),
                   g_ref[...], mod_ref[0, 0], mod_ref[0, 1]).astype(BF16) for rows in _SUB_ROWS]
    for rows, h_tile in zip(_SUB_ROWS, h):
        q_ref[rows] = (_dot(h_tile, w_ref[:, 0 * D:1 * D]) * (DK ** -0.5)).astype(BF16)
        v_ref[rows] = _dot(h_tile, w_ref[:, 1 * D:2 * D]).astype(BF16)
        ff_ref[rows] = _dot(h_tile, w_ref[:, 2 * D:3 * D])
        fb_ref[rows] = _dot(h_tile, w_ref[:, 3 * D:4 * D])
        gg_ref[rows] = _dot(h_tile, w_ref[:, 4 * D:5 * D]).astype(BF16)


def _hgrn_proj(x_parts, mod, norm_g, w_in):
    bf = jax.ShapeDtypeStruct((T, D), BF16)
    f32 = jax.ShapeDtypeStruct((T, D), F32)
    return pl.pallas_call(
        _hgrn_proj_kernel,
        grid=(N_STEPS,),
        in_specs=_token_specs(x_parts, STEP_TILES) + [_STEP_MOD_SPEC, _ROW_SPEC, _ANY_SPEC],
        out_specs=[_STEP_SPEC] * 5,
        out_shape=[bf, bf, f32, f32, bf],
        scratch_shapes=_weight_scratch(5),
        compiler_params=_cparams(),
        name="hgrn_proj",
    )(*x_parts, mod, norm_g, w_in)


GLA_TILES = 4
GLA_ROWS = GLA_TILES * TM
N_GLA_STEPS = N_TILES // GLA_TILES
GLA_PROMPT_STEPS = PROMPT_TILES // GLA_TILES
STEPS_PER_SAMPLE = SAMPLE_LEN // GLA_ROWS
_GLA_SPEC = pl.BlockSpec((GLA_ROWS, D), lambda i: (i, 0))
_GLA_TILE_ROWS = tuple(slice(s * TM, (s + 1) * TM) for s in range(GLA_TILES))


def _gla_step(block, q_ref, v_ref, f_ref, lbl_ref, s0_ref, st_ref, st_tiles_ref, ma_ref, o_ref,
              *, reverse, finish=None):
    is_prompt = block < GLA_PROMPT_STEPS
    rel = block - GLA_PROMPT_STEPS
    edge = (STEPS_PER_SAMPLE - 1) if reverse else 0
    sample_start = jnp.logical_and(rel >= 0, rel % STEPS_PER_SAMPLE == edge)

    @pl.when(is_prompt)
    def _():
        st_ref[...] = jnp.zeros_like(st_ref)

    @pl.when(sample_start)
    def _():
        for h in range(HEADS):
            st_ref[h] = s0_ref[0, 0, h].T

    logits = lbl_ref[...]
    e = jnp.exp(logits - jnp.max(logits, axis=0, keepdims=True))
    lb = e[0:1] / jnp.sum(e, axis=0, keepdims=True)

    ref_idx = (CHUNK - 1 - CHUNK // 2) if reverse else CHUNK // 2
    last_idx = 0 if reverse else CHUNK - 1
    shift = CHUNK.bit_length() - 1

    row = lax.broadcasted_iota(jnp.int32, (TM, TM), 0)
    col = lax.broadcasted_iota(jnp.int32, (TM, TM), 1)
    same = (row >> shift) == (col >> shift)
    ref_row = ((row >> shift) << shift) + ref_idx
    keep = jnp.logical_and(same, (row <= col) if reverse else (row >= col))

    @pl.when(pl.program_id(0) == 0)
    def _():
        at_ref = jnp.logical_and(same, (ref_row <= col) if reverse else (ref_row >= col))
        ma_ref[...] = (keep.astype(F32) - at_ref.astype(F32)).astype(BF16)

    ma = ma_ref[...]

    rr = lax.broadcasted_iota(jnp.int32, (2 * CHUNKS_PER_TILE, TM), 0)
    cc = lax.broadcasted_iota(jnp.int32, (2 * CHUNKS_PER_TILE, TM), 1)
    chunk = rr & (CHUNKS_PER_TILE - 1)
    target = chunk * CHUNK + jnp.where(rr < CHUNKS_PER_TILE, ref_idx, last_idx)
    covered = (cc >= target) if reverse else (cc <= target)
    rm = jnp.logical_and((cc >> shift) == chunk, covered).astype(BF16)

    chunk_rows = [slice(c * CHUNK, (c + 1) * CHUNK) for c in range(CHUNKS_PER_TILE)]
    order = range(CHUNKS_PER_TILE - 1, -1, -1) if reverse else range(CHUNKS_PER_TILE)
    pair_w = 2 * DK
    n_pairs = HEADS // 2
    tile_order = _GLA_TILE_ROWS[::-1] if reverse else _GLA_TILE_ROWS

    def per_chunk_scale(x, scale):
        return jnp.concatenate([x[rs] * scale[c:c + 1] for c, rs in enumerate(chunk_rows)], axis=0)

    def pair_cols(p):
        return slice(p * pair_w, (p + 1) * pair_w)

    def stage_decay(k, p):
        lbp = lb[:, pair_cols(p)]
        f = lbp + (1.0 - lbp) * jax.nn.sigmoid(f_ref[tile_order[k], pair_cols(p)])
        lf_hi, lf_lo = _split_bf16(jnp.log(f))
        z = _dot(ma, lf_hi) + _dot(ma, lf_lo)
        marks = _dot(rm, lf_hi) + _dot(rm, lf_lo)
        return 1.0 - f, z, marks

    def stage_scores(k, p, kk, z, marks):
        ref, last = marks[:CHUNKS_PER_TILE], marks[CHUNKS_PER_TILE:]
        qa32 = q_ref[tile_order[k], pair_cols(p)].astype(F32) * jnp.exp(z)
        kb32 = kk * jnp.exp(-z)
        qa = qa32.astype(BF16)
        kb = kb32.astype(BF16)
        qe = per_chunk_scale(qa32, jnp.exp(ref)).astype(BF16)
        kd = per_chunk_scale(kb32, jnp.exp(last - ref)).astype(BF16)
        v = v_ref[tile_order[k], pair_cols(p)]
        scores, incr = [], []
        for hh in range(2):
            ls = slice(hh * DK, (hh + 1) * DK)
            scores.append(_dot_nt(qa[:, ls], kb[:, ls]))
            kd_h = kd[:, ls]
            kd_blocks = jnp.concatenate(
                [jnp.concatenate([part for part in (jnp.zeros((c * CHUNK, DK), BF16), kd_h[rs],
                                                    jnp.zeros((TM - (c + 1) * CHUNK, DK), BF16))
                                  if part.shape[0]], axis=0)
                 for c, rs in enumerate(chunk_rows)], axis=1)
            incr.append(_dot_tn(v[:, ls], kd_blocks))
        return scores, incr, qe, jnp.exp(last), v

    def stage_output(k, p, scores, incr, qe, decay, v):
        base = tile_order[k].start
        for hh in range(2):
            h = 2 * p + hh
            ls = slice(hh * DK, (hh + 1) * DK)
            hs = slice(h * DK, (h + 1) * DK)
            o_intra = _dot(jnp.where(keep, scores[hh], 0.0).astype(BF16), v[:, ls])
            st = st_ref[h]
            if k > 0:
                st = jnp.where(is_prompt, 0.0, st)
            for c in order:
                rs = chunk_rows[c]
                o_ref[base + rs.start:base + rs.stop, hs] = (
                    o_intra[rs] + _dot_nt(qe[rs, ls], st.astype(BF16)))
                st = st * decay[c:c + 1, ls] + incr[hh][:, c * DK:(c + 1) * DK]
            st_ref[h] = st
            if k < GLA_TILES - 1:
                st_tiles_ref[k, h] = st
        if finish is not None:
            return (tile_order[k], p)

    def stage_finish(k, p, rows, pair):
        finish(rows, pair)

    units = [(k, p) for k in range(GLA_TILES) for p in range(n_pairs)]
    stages = [stage_decay, stage_scores, stage_output] + ([stage_finish] if finish is not None else [])
    carried = [dict() for _ in stages]
    for step in range(len(units) + len(stages) - 1):
        for depth, stage in (enumerate(stages) if reverse else reversed(list(enumerate(stages)))):
            u = step - depth
            if 0 <= u < len(units):
                out = stage(*units[u], *(carried[depth].pop(u) if depth else ()))
                if depth + 1 < len(stages):
                    carried[depth + 1][u] = out


def _store_prompt_states(block, st_ref, st_tiles_ref, snew_ref, *, reverse, sfwd_ref=None):
    @pl.when(block < GLA_PROMPT_STEPS)
    def _():
        for k in range(GLA_TILES):
            tile = GLA_TILES - 1 - k if reverse else k
            for h in range(HEADS):
                state = (st_tiles_ref[k, h] if k < GLA_TILES - 1 else st_ref[h]).T
                if sfwd_ref is None:
                    snew_ref[tile, h] = state
                else:
                    snew_ref[tile, 1, h] = state
        if sfwd_ref is not None:
            snew_ref[:, 0] = sfwd_ref[...]


def _gla_fwd_kernel(q_ref, v_ref, f_ref, lbl_ref, s0_ref, o_ref, snew_ref, st_ref, st_tiles_ref, ma_ref):
    block = pl.program_id(0)
    _gla_step(block, q_ref, v_ref, f_ref, lbl_ref, s0_ref, st_ref, st_tiles_ref, ma_ref, o_ref,
              reverse=False)
    _store_prompt_states(block, st_ref, st_tiles_ref, snew_ref, reverse=False)


def _gla_bwd_kernel(q_ref, v_ref, f_ref, lbl_ref, s0_ref, of_ref, gg_ref, ng_ref, sfwd_ref,
                    a_ref, snew_ref, st_ref, st_tiles_ref, ma_ref, ob_ref):
    block = N_GLA_STEPS - 1 - pl.program_id(0)

    def finish(rows, p):
        for h in (2 * p, 2 * p + 1):
            hs = slice(h * DV, (h + 1) * DV)
            o = of_ref[rows, hs] + ob_ref[rows, hs]
            a_ref[rows, hs] = (_rmsnorm(o, ng_ref[...]) * _silu(gg_ref[rows, hs].astype(F32))).astype(BF16)

    _gla_step(block, q_ref, v_ref, f_ref, lbl_ref, s0_ref, st_ref, st_tiles_ref, ma_ref, ob_ref,
              reverse=True, finish=finish)
    _store_prompt_states(block, st_ref, st_tiles_ref, snew_ref, reverse=True, sfwd_ref=sfwd_ref)


def _state_specs(direction, block_of_step):
    sample = lambda i: jnp.clip((block_of_step(i) - GLA_PROMPT_STEPS) // STEPS_PER_SAMPLE, 0, N_SAMPLE_SEQ - 1)
    s0 = pl.BlockSpec((1, 1, HEADS, DK, DV), lambda i: (sample(i), direction, 0, 0, 0))
    snew = pl.BlockSpec((GLA_TILES, HEADS, DK, DV),
                        lambda i: (jnp.minimum(block_of_step(i), GLA_PROMPT_STEPS - 1), 0, 0, 0))
    return s0, snew


_GLA_STATE_SCRATCH = [pltpu.VMEM((HEADS, DV, DK), F32), pltpu.VMEM((GLA_TILES - 1, HEADS, DV, DK), F32),
                      pltpu.VMEM((TM, TM), BF16)]


def _gla_fwd(q, v, ff, lb_logits, s0):
    s0_spec, snew_spec = _state_specs(0, lambda i: i)
    return pl.pallas_call(
        _gla_fwd_kernel,
        grid=(N_GLA_STEPS,),
        in_specs=[_GLA_SPEC, _GLA_SPEC, _GLA_SPEC,
                  pl.BlockSpec(lb_logits.shape, lambda i: (0, 0)), s0_spec],
        out_specs=[_GLA_SPEC, snew_spec],
        out_shape=[jax.ShapeDtypeStruct((T, D), F32),
                   jax.ShapeDtypeStruct((N_PROMPT_SEQ, HEADS, DK, DV), F32)],
        scratch_shapes=_GLA_STATE_SCRATCH,
        compiler_params=_cparams(),
        name="gla_fwd",
    )(q, v, ff, lb_logits, s0)


def _gla_bwd(q, v, fb, lb_logits, s0, o_f, gg, head_norm_g, s_fwd):
    rev = lambda i: N_GLA_STEPS - 1 - i
    block = pl.BlockSpec((GLA_ROWS, D), lambda i: (rev(i), 0))
    s0_spec, sfwd_spec = _state_specs(1, rev)
    prompt_block = lambda i: jnp.minimum(rev(i), GLA_PROMPT_STEPS - 1)
    return pl.pallas_call(
        _gla_bwd_kernel,
        grid=(N_GLA_STEPS,),
        in_specs=[block, block, block, pl.BlockSpec(lb_logits.shape, lambda i: (0, 0)), s0_spec,
                  block, block, pl.BlockSpec((1, DV), lambda i: (0, 0)), sfwd_spec],
        out_specs=[block, pl.BlockSpec((GLA_TILES, 2, HEADS, DK, DV), lambda i: (prompt_block(i), 0, 0, 0, 0))],
        out_shape=[jax.ShapeDtypeStruct((T, D), BF16),
                   jax.ShapeDtypeStruct((N_PROMPT_SEQ, 2, HEADS, DK, DV), F32)],
        scratch_shapes=_GLA_STATE_SCRATCH + [pltpu.VMEM((GLA_ROWS, D), F32)],
        compiler_params=_cparams(),
        name="gla_bwd",
    )(q, v, fb, lb_logits, s0, o_f, gg, head_norm_g, s_fwd)


def _conv_mixer_kernel(x_ref, mod_ref, g_ref, w_hbm, cw_ref, a_ref, w_ref, stage_ref, sem):
    @pl.when(pl.program_id(0) == 0)
    def _():
        _load_weight_bf16(w_hbm, w_ref, stage_ref, sem)

    seg = jnp.where(pl.program_id(0) < PROMPT_TILES // CONV_TILES, PROMPT_LEN, GRID_W)
    pos = lax.broadcasted_iota(jnp.int32, (TM, 1), 0) & (seg - 1)

    def project(rows):
        h = _modulate(x_ref[rows], g_ref[...], mod_ref[0, 0], mod_ref[0, 1]).astype(BF16)
        return [_dot(h, w_ref[:, k * D:(k + 1) * D]) for k in range(3)]

    def convolve(rows, bg, cg, x_in):
        u = cg * x_in
        prev = jnp.where(pos == 0, 0.0, pltpu.roll(u, 1, axis=0))
        nxt = jnp.where(pos == seg - 1, 0.0, pltpu.roll(u, TM - 1, axis=0))
        conv = cw_ref[0:1] * prev + cw_ref[1:2] * u + cw_ref[2:3] * nxt
        a_ref[rows] = (bg * conv).astype(BF16)

    tiles = [slice(s * TM, (s + 1) * TM) for s in range(CONV_TILES)]
    projected = [project(rows) for rows in tiles]
    for rows, parts in zip(tiles, projected):
        convolve(rows, *parts)


def _conv_mixer(x, mod, norm_g, w_in, conv_w):
    block = pl.BlockSpec((CONV_TILES * TM, D), lambda i: (i, 0))
    return pl.pallas_call(
        _conv_mixer_kernel,
        grid=(N_TILES // CONV_TILES,),
        in_specs=[block, pl.BlockSpec((1, 6, 1, D), lambda i: (_cond_of_tile(i * CONV_TILES), 0, 0, 0)),
                  _ROW_SPEC, _ANY_SPEC, pl.BlockSpec((3, D), lambda i: (0, 0))],
        out_specs=block,
        out_shape=jax.ShapeDtypeStruct((T, D), BF16),
        scratch_shapes=_weight_scratch(3),
        compiler_params=_cparams(),
        name="conv_mixer",
    )(x, mod, norm_g, w_in, conv_w)


ROUTE_ROWS = 32


def _first_member(vals, target):
    idx = jnp.full_like(target, float(EPG - 1))
    for j in range(EPG - 2, -1, -1):
        idx = jnp.where(vals[j] == target, float(j), idx)
    return idx


def _router_logits(h2, rwt_ref):
    h_hi, h_lo = _split_bf16(h2)
    w_hi, w_lo = _split_bf16(rwt_ref[...])
    with_hi = _dot_nt(jnp.concatenate([w_hi, w_lo], axis=0), h_hi)
    return with_hi[:N_EXPERTS] + (_dot_nt(w_hi, h_lo) + with_hi[N_EXPERTS:])


def _select_experts(logits, rb_ref):
    scores = jax.nn.sigmoid(logits)
    sel = scores + rb_ref[...]
    member = [sel[j * N_GROUPS:(j + 1) * N_GROUPS] for j in range(EPG)]
    m1 = functools.reduce(jnp.maximum, member)
    i1 = _first_member(member, m1)
    rest = [jnp.where(i1 == float(j), -jnp.inf, member[j]) for j in range(EPG)]
    m2 = functools.reduce(jnp.maximum, rest)
    i2 = _first_member(rest, m2)
    group_score = m1 + m2

    best, grp, first, second = group_score[0:1], jnp.zeros((1, TM), F32), i1[0:1], i2[0:1]
    for g in range(1, N_GROUPS):
        upd = group_score[g:g + 1] > best
        best = jnp.where(upd, group_score[g:g + 1], best)
        grp = jnp.where(upd, float(g), grp)
        first = jnp.where(upd, i1[g:g + 1], first)
        second = jnp.where(upd, i2[g:g + 1], second)
    a = jnp.minimum(first, second)
    b = jnp.maximum(first, second)
    row = lax.broadcasted_iota(jnp.int32, (N_EXPERTS, TM), 0).astype(F32)
    s_lo = jnp.sum(jnp.where(row == a * N_GROUPS + grp, scores, 0.0), axis=0, keepdims=True)
    s_hi = jnp.sum(jnp.where(row == b * N_GROUPS + grp, scores, 0.0), axis=0, keepdims=True)
    tot = s_lo + s_hi
    pair = jnp.where(a == 0.0, jnp.where(b == 1.0, 0.0, jnp.where(b == 2.0, 3.0, 4.0)),
                     jnp.where(a == 1.0, jnp.where(b == 2.0, 5.0, 1.0), 2.0))
    return grp * N_PAIRS + pair, s_lo / tot, s_hi / tot


def _post_mixer_kernel(*refs, n_x):
    a_ref, x_refs = refs[0], refs[1:1 + n_x]
    (mod_ref, g2_ref, wo_hbm, rwt_ref, rb_ref,
     x1_ref, gates_ref, bucket_ref, rank_ref, cnt_ref, carry_ref, earlier_ref, wo_ref, stage_ref, sem) = refs[1 + n_x:]

    @pl.when(pl.program_id(0) == 0)
    def _():
        _load_weight_bf16(wo_hbm, wo_ref, stage_ref, sem)
        carry_ref[...] = jnp.zeros_like(carry_ref)
        s_idx = lax.broadcasted_iota(jnp.int32, (TM, TM), 0)
        t_idx = lax.broadcasted_iota(jnp.int32, (TM, TM), 1)
        earlier_ref[...] = (s_idx < t_idx).astype(BF16)

    def stage_mix(s):
        rows = slice(s * TM, (s + 1) * TM)
        x1 = (_token_tile(x_refs, rows, PROMPT_TILES // ROUTE_TILES)
              + mod_ref[0, 2] * _dot(a_ref[rows], wo_ref[...]))
        x1_ref[rows] = x1
        return _modulate(x1, g2_ref[...], mod_ref[0, 3], mod_ref[0, 4])

    def stage_rank(s, bucket, g_lo, g_hi):
        onehot = lax.broadcasted_iota(jnp.int32, (ROUTE_ROWS, TM), 0).astype(F32) == bucket
        before = _dot(onehot.astype(BF16), earlier_ref[...])
        oh = onehot.astype(F32)
        carry = carry_ref[...]
        rank = jnp.sum(oh * (before + carry[:, 0:1]), axis=0, keepdims=True)
        carry_ref[...] = carry + jnp.sum(oh, axis=1, keepdims=True)
        bucket_ref[s] = bucket.astype(jnp.int32)
        rank_ref[s] = rank.astype(jnp.int32)
        grow = lax.broadcasted_iota(jnp.int32, (LANES, TM), 0)
        gates_ref[s * TM:(s + 1) * TM] = jnp.where(grow == 0, g_lo, jnp.where(grow == 1, g_hi, 0.0)).T

    tiles = range(ROUTE_TILES)
    h2 = [stage_mix(s) for s in tiles]
    logits = [_router_logits(h2[s], rwt_ref) for s in tiles]
    picks = [_select_experts(logits[s], rb_ref) for s in tiles]
    for s in tiles:
        stage_rank(s, *picks[s])
    cnt_ref[...] = carry_ref[...]


def _post_mixer(a, x_parts, mod, norm_g2, w_out, router_wt, router_bt):
    rows = ROUTE_TILES * TM
    idx_spec = pl.BlockSpec((ROUTE_TILES, 1, TM), lambda i: (i, 0, 0))
    return pl.pallas_call(
        functools.partial(_post_mixer_kernel, n_x=len(x_parts)),
        grid=(N_TILES // ROUTE_TILES,),
        in_specs=[pl.BlockSpec((rows, D), lambda i: (i, 0))] + _token_specs(x_parts, ROUTE_TILES) + [
            pl.BlockSpec((1, 6, 1, D), lambda i: (_cond_of_tile(i * ROUTE_TILES), 0, 0, 0)), _ROW_SPEC,
            _ANY_SPEC,
            pl.BlockSpec((N_EXPERTS, D), lambda i: (0, 0)),
            pl.BlockSpec((N_EXPERTS, 1), lambda i: (0, 0))],
        out_specs=[pl.BlockSpec((rows, D), lambda i: (i, 0)), pl.BlockSpec((rows, LANES), lambda i: (i, 0)),
                   idx_spec, idx_spec, pl.BlockSpec((ROUTE_ROWS, LANES), lambda i: (0, 0))],
        out_shape=[jax.ShapeDtypeStruct((T, D), F32), jax.ShapeDtypeStruct((T, LANES), F32),
                   jax.ShapeDtypeStruct((N_TILES, 1, TM), jnp.int32),
                   jax.ShapeDtypeStruct((N_TILES, 1, TM), jnp.int32),
                   jax.ShapeDtypeStruct((ROUTE_ROWS, LANES), F32)],
        scratch_shapes=[pltpu.VMEM((ROUTE_ROWS, LANES), F32), pltpu.VMEM((TM, TM), BF16)] + _weight_scratch(1),
        compiler_params=_cparams(),
        name="post_mixer",
    )(a, *x_parts, mod, norm_g2, w_out, router_wt, router_bt)


_MOVE_SPEC = pl.BlockSpec((MOVE_ROWS, D), lambda i, *_: (i, 0))
_MOVE_MOD_SPEC = pl.BlockSpec((1, 6, 1, D), lambda i, *_: (_cond_of_tile(i * MOVE_TILES), 0, 0, 0))


def _smem_step_spec(step_of_step):
    return pl.BlockSpec((MOVE_TILES, 1, TM), lambda i, *_: (step_of_step(i), 0, 0), memory_space=pltpu.SMEM)


def _start_step_rows(make_copy):
    for tile in range(MOVE_TILES):
        for r in range(TM):
            make_copy(tile, r).start(priority=r % 2)


def _dispatch_kernel(zero_from_ref, slot_ref, x1_ref, mod_ref, g2_ref, gates_ref,
                     xs_ref, buf_ref, zero_ref, sems, zero_sem):
    i = pl.program_id(0)
    cur = i % 2

    @pl.when(i == 0)
    def _():
        zero_ref[...] = jnp.zeros_like(zero_ref)

        def for_each_chunk(act):
            def tile(j, carry):
                def chunk(c, inner):
                    row = pl.multiple_of(j * TM_E + c * ZERO_ROWS, ZERO_ROWS)
                    act(pltpu.make_async_copy(zero_ref, xs_ref.at[pl.ds(row, ZERO_ROWS)], zero_sem))
                    return inner
                return lax.fori_loop(zero_from_ref[j], TM_E // ZERO_ROWS, chunk, carry)
            lax.fori_loop(0, N_TILES_E, tile, 0)

        for_each_chunk(lambda copy: copy.start())
        for_each_chunk(lambda copy: copy.wait())

    def wait_step(b):
        pltpu.make_async_copy(buf_ref.at[b], xs_ref.at[pl.ds(0, MOVE_ROWS)], sems.at[b]).wait()

    h2 = _modulate(x1_ref[...], g2_ref[...], mod_ref[0, 3], mod_ref[0, 4])
    gates = gates_ref[...]

    def send(b):
        buf_ref[b, :, :D] = h2
        buf_ref[b, :, D:] = gates
        _start_step_rows(lambda tile, r: pltpu.make_async_copy(
            buf_ref.at[b, pl.ds(tile * TM + r, 1)], xs_ref.at[pl.ds(slot_ref[tile, 0, r], 1)], sems.at[b]))

        @pl.when(i > 0)
        def _():
            wait_step(1 - b)

        @pl.when(i == N_MOVE_STEPS - 1)
        def _():
            wait_step(b)

    for b in range(2):
        pl.when(cur == b)(functools.partial(send, b))


def _dispatch(zero_from, slot, x1, mod, norm_g2, gates):
    return pl.pallas_call(
        _dispatch_kernel,
        grid_spec=pltpu.PrefetchScalarGridSpec(
            num_scalar_prefetch=1,
            grid=(N_MOVE_STEPS,),
            in_specs=[_smem_step_spec(lambda i: i), _MOVE_SPEC, _MOVE_MOD_SPEC,
                      pl.BlockSpec((1, D), lambda i, *_: (0, 0)),
                      pl.BlockSpec((MOVE_ROWS, LANES), lambda i, *_: (i, 0))],
            out_specs=pl.BlockSpec(memory_space=pl.ANY),
            scratch_shapes=[pltpu.VMEM((2, MOVE_ROWS, XW), F32), pltpu.VMEM((ZERO_ROWS, XW), F32),
                            pltpu.SemaphoreType.DMA((2,)), pltpu.SemaphoreType.DMA(())],
        ),
        out_shape=jax.ShapeDtypeStruct((S_ROWS, XW), F32),
        compiler_params=_cparams(),
        name="moe_dispatch",
    )(zero_from, slot, x1, mod, norm_g2, gates)


_W_SHAPES = ((D, D_FF), (D, D_FF), (D_FF, D))


def _expert_kernel(ea_ref, eb_ref, next_a_ref, next_b_ref, a_higher_ref, nrows_ref, xblock_ref,
                   x_ref, wg_hbm, wu_hbm, wd_hbm, y_ref, *scratch, layer):
    stage, w16, sems = scratch[0:3], scratch[3:6], scratch[6]
    j = pl.program_id(0)
    n = nrows_ref[j]
    prev = jnp.maximum(j - 1, 0)

    def fetch(slot, expert):
        return [pltpu.make_async_copy(w.at[layer, expert], s.at[slot], sems.at[slot])
                for w, s in zip((wg_hbm, wu_hbm, wd_hbm), stage)]

    for slot, e_ref, next_ref in ((0, ea_ref, next_a_ref), (1, eb_ref, next_b_ref)):
        @pl.when(j == 0)
        def _(slot=slot, e_ref=e_ref):
            for copy in fetch(slot, e_ref[0]):
                copy.start()

        @pl.when(jnp.logical_or(j == 0, e_ref[j] != e_ref[prev]))
        def _(slot=slot, e_ref=e_ref, next_ref=next_ref):
            for copy in fetch(slot, e_ref[j]):
                copy.wait()
            for s, d in zip(stage, w16):
                d[slot] = s[slot].astype(BF16)

            @pl.when(next_ref[j] >= 0)
            def _():
                for copy in fetch(slot, next_ref[j]):
                    copy.start(priority=1)

    def run(rows):
        valid = lax.broadcasted_iota(jnp.int32, (rows, 1), 0) < n
        xe = jnp.where(valid, x_ref[:rows], 0.0)
        x = xe[:, :D].astype(BF16)
        g_lo, g_hi = xe[:, D:D + 1], xe[:, D + 1:D + 2]
        a_higher = a_higher_ref[j] != 0

        def ffn(slot, gate):
            wg_ref, wu_ref, wd_ref = w16
            act = _silu(_dot(x, wg_ref[slot])) * _dot(x, wu_ref[slot]) * gate
            return _dot(act.astype(BF16), wd_ref[slot])

        y_ref[:rows] = (ffn(0, jnp.where(a_higher, g_hi, g_lo))
                        + ffn(1, jnp.where(a_higher, g_lo, g_hi)))
        if rows < TM_E:
            y_ref[rows:] = jnp.zeros((TM_E - rows, D), F32)

    half = TM_E // 2
    pl.when(n > half)(functools.partial(run, TM_E))
    pl.when(jnp.logical_and(n > 0, n <= half))(functools.partial(run, half))

    @pl.when(n == 0)
    def _():
        y_ref[...] = jnp.zeros_like(y_ref)


def _experts(tile_ea, tile_eb, tile_next_a, tile_next_b, tile_a_higher, tile_rows, tile_xblock, xs,
             w_gate, w_up, w_down, layer):
    any_spec = pl.BlockSpec(memory_space=pl.ANY)
    return pl.pallas_call(
        functools.partial(_expert_kernel, layer=layer),
        grid_spec=pltpu.PrefetchScalarGridSpec(
            num_scalar_prefetch=7,
            grid=(N_TILES_E,),
            in_specs=[pl.BlockSpec((TM_E, XW), lambda j, *plan: (plan[-1][j], 0)), any_spec, any_spec, any_spec],
            out_specs=pl.BlockSpec((TM_E, D), lambda j, *_: (j, 0)),
            scratch_shapes=[pltpu.VMEM((2,) + s, F32) for s in _W_SHAPES]
                           + [pltpu.VMEM((2,) + s, BF16) for s in _W_SHAPES]
                           + [pltpu.SemaphoreType.DMA((2,))],
        ),
        out_shape=jax.ShapeDtypeStruct((S_ROWS, D), F32),
        compiler_params=_cparams(),
        name="moe_experts",
    )(tile_ea, tile_eb, tile_next_a, tile_next_b, tile_a_higher, tile_rows, tile_xblock, xs,
      w_gate, w_up, w_down)


def _combine_kernel(slot_ref, next_slot_ref, x1_ref, mod_ref, gf_ref, y_ref, *rest, final):
    out_refs, (buf_ref, sems) = rest[:-2], rest[-2:]
    i = pl.program_id(0)
    cur = i % 2

    def gather(slots, b):
        _start_step_rows(lambda tile, r: pltpu.make_async_copy(
            y_ref.at[pl.ds(slots[tile, 0, r], 1)], buf_ref.at[b, pl.ds(tile * TM + r, 1)], sems.at[b]))

    @pl.when(i == 0)
    def _():
        gather(slot_ref, 0)

    def combine(b):
        @pl.when(i + 1 < N_MOVE_STEPS)
        def _():
            gather(next_slot_ref, 1 - b)

        pltpu.make_async_copy(y_ref.at[pl.ds(0, MOVE_ROWS)], buf_ref.at[b], sems.at[b]).wait()
        x2 = x1_ref[...] + mod_ref[0, 5] * buf_ref[b]
        if not final:
            out_refs[0][...] = x2
            return
        y = _rmsnorm(x2, gf_ref[...])
        yp_ref, ys_ref = out_refs

        @pl.when(i < T_PROMPT // MOVE_ROWS)
        def _():
            yp_ref[...] = y

        @pl.when(i >= T_PROMPT // MOVE_ROWS)
        def _():
            ys_ref[...] = y

    for b in range(2):
        pl.when(cur == b)(functools.partial(combine, b))


def _combine(slot, x1, mod, final_g, y_sorted, *, final):
    if final:
        out_specs = _token_specs((None, None), MOVE_TILES)
        out_shape = [jax.ShapeDtypeStruct((T_PROMPT, D), F32), jax.ShapeDtypeStruct((T_SAMPLE, D), F32)]
    else:
        out_specs = [_MOVE_SPEC]
        out_shape = [jax.ShapeDtypeStruct((T, D), F32)]
    return pl.pallas_call(
        functools.partial(_combine_kernel, final=final),
        grid=(N_MOVE_STEPS,),
        in_specs=[_smem_step_spec(lambda i: i), _smem_step_spec(lambda i: jnp.minimum(i + 1, N_MOVE_STEPS - 1)),
                  _MOVE_SPEC, _MOVE_MOD_SPEC, _ROW_SPEC, pl.BlockSpec(memory_space=pl.ANY)],
        out_specs=out_specs,
        scratch_shapes=[pltpu.VMEM((2, MOVE_ROWS, D), F32), pltpu.SemaphoreType.DMA((2,))],
        out_shape=out_shape,
        compiler_params=_cparams(),
        name="moe_combine_final" if final else "moe_combine",
    )(slot, slot, x1, mod, final_g, y_sorted)


def _plan_kernel(cnt_ref, bucket_ref, rank_ref, slot_ref,
                 ea_ref, eb_ref, next_a_ref, next_b_ref, a_higher_ref, rows_ref, xblock_ref, zero_from_ref):
    slot = rank_ref[...]
    bucket = bucket_ref[...]
    tile0 = jnp.int32(0)
    last = [jnp.int32(0)] * 3
    for b in range(N_BUCKETS):
        n = cnt_ref[b]
        tiles = (n + (TM_E - 1)) // TM_E
        pair = b % N_PAIRS
        per_tile = ((b // N_PAIRS) * EPG + _SLOT_A[pair], (b // N_PAIRS) * EPG + _SLOT_B[pair],
                    _SLOT_A_IS_HIGHER[pair])

        def fill(k, carry, n=n, tile0=tile0, per_tile=per_tile):
            j = tile0 + k
            live = jnp.minimum(n - k * TM_E, TM_E)
            rows_ref[j] = live
            xblock_ref[j] = j
            zero_from_ref[j] = live // ZERO_ROWS
            for ref, value in zip((ea_ref, eb_ref, a_higher_ref), per_tile):
                ref[j] = jnp.int32(value)
            return carry

        lax.fori_loop(0, tiles, fill, 0)
        slot = slot + jnp.where(bucket == b, tile0 * TM_E, 0)
        last = [jnp.where(tiles > 0, value, old) for value, old in zip(per_tile, last)]
        tile0 = tile0 + tiles

    def unused(j, carry):
        rows_ref[j] = jnp.int32(0)
        xblock_ref[j] = jnp.maximum(tile0 - 1, 0)
        zero_from_ref[j] = jnp.int32(0)
        for ref, value in zip((ea_ref, eb_ref, a_higher_ref), last):
            ref[j] = value
        return carry

    lax.fori_loop(tile0, N_TILES_E, unused, 0)

    for e_ref, next_ref in ((ea_ref, next_a_ref), (eb_ref, next_b_ref)):
        next_ref[N_TILES_E - 1] = jnp.int32(-1)

        def backward(k, carry, e_ref=e_ref, next_ref=next_ref):
            j = N_TILES_E - 2 - k
            next_ref[j] = jnp.where(e_ref[j] == e_ref[j + 1], next_ref[j + 1], e_ref[j + 1])
            return carry

        lax.fori_loop(0, N_TILES_E - 1, backward, 0)
    slot_ref[...] = slot


def _plan(counts, bucket, rank):
    whole = pl.BlockSpec(bucket.shape, lambda i, *_: (0, 0, 0))
    smem = pl.BlockSpec(memory_space=pltpu.SMEM)
    per_tile = jax.ShapeDtypeStruct((N_TILES_E,), jnp.int32)
    slot, *tile_plan = pl.pallas_call(
        _plan_kernel,
        grid_spec=pltpu.PrefetchScalarGridSpec(
            num_scalar_prefetch=1, grid=(1,), in_specs=[whole, whole], out_specs=[whole] + [smem] * 8),
        out_shape=[jax.ShapeDtypeStruct(bucket.shape, jnp.int32)] + [per_tile] * 8,
        compiler_params=_cparams(),
        name="moe_plan",
    )(counts[:, 0].astype(jnp.int32), bucket, rank)
    return slot, tile_plan


def _moe(a, x_parts, mod, norm_g2, w_out, router_wt, router_bt, w_gate, w_up, w_down, layer, final_g,
         *, final):
    x1, gates, bucket, rank, counts = _post_mixer(a, x_parts, mod, norm_g2, w_out, router_wt, router_bt)
    slot, (*expert_plan, zero_from) = _plan(counts, bucket, rank)
    xs = _dispatch(zero_from, slot, x1, mod, norm_g2, gates)
    ys = _experts(*expert_plan, xs, w_gate, w_up, w_down, layer)
    return _combine(slot, x1, mod, final_g, ys, final=final)


def kernel(x_prompt, x_sample, state_hgrn, c, c_ctx, w_mod, b_mod, norm_mix_g, norm_ffn_g, norm_final_g,
           hgrn_w_in, hgrn_lb_logits, hgrn_norm_g, hgrn_w_out, conv_w_in, conv_w, conv_w_out,
           router_w, router_b, moe_w_gate, moe_w_up, moe_w_down):
    x_parts = (x_prompt.reshape(T_PROMPT, D), x_sample.reshape(T_SAMPLE, D))
    cond = jnp.concatenate([c_ctx[None], c, jnp.zeros((COND_ROWS - N_COND, D), F32)], axis=0)
    mods = _adaln(cond, w_mod, b_mod)[:, :N_COND].reshape(2, N_COND, 6, 1, D)
    member_major = lambda w: w.reshape(N_GROUPS, EPG, -1).transpose(1, 0, 2).reshape(N_EXPERTS, -1)
    rwt = member_major(router_w.T)
    rbt = member_major(router_b.reshape(N_EXPERTS, 1))
    final_g = norm_final_g.reshape(1, D)

    q, v, ff, fb, gg = _hgrn_proj(x_parts, mods[0], norm_mix_g[0:1], hgrn_w_in[0])
    s0 = state_hgrn[:, 0]
    o_f, s_f = _gla_fwd(q, v, ff, hgrn_lb_logits, s0)
    a, new_state = _gla_bwd(q, v, fb, hgrn_lb_logits, s0, o_f, gg, hgrn_norm_g[0:1], s_f)
    (x,) = _moe(a, x_parts, mods[0], norm_ffn_g[0:1], hgrn_w_out[0], rwt, rbt,
                moe_w_gate, moe_w_up, moe_w_down, 0, final_g, final=False)

    a = _conv_mixer(x, mods[1], norm_mix_g[1:2], conv_w_in[0], conv_w[0])
    y_p, y_s = _moe(a, (x,), mods[1], norm_ffn_g[1:2], conv_w_out[0], rwt, rbt,
                    moe_w_gate, moe_w_up, moe_w_down, 1, final_g, final=True)

    return (y_p.reshape(N_PROMPT_SEQ, PROMPT_LEN, D), y_s.reshape(N_SAMPLE_SEQ, SAMPLE_LEN, D),
            new_state[:, None])
```

```python
import functools

import jax
import jax.numpy as jnp
from jax import lax
from jax.experimental import pallas as pl
from jax.experimental.pallas import tpu as pltpu

F32 = jnp.float32
BF16 = jnp.bfloat16

D = 1024
N_PROMPT_SEQ = 16
PROMPT_LEN = 256
N_SAMPLE_SEQ = 2
SAMPLE_LEN = 4096
GRID_W = 64
T_PROMPT = N_PROMPT_SEQ * PROMPT_LEN
T_SAMPLE = N_SAMPLE_SEQ * SAMPLE_LEN
T = T_PROMPT + T_SAMPLE
N_COND = 1 + N_SAMPLE_SEQ
COND_ROWS = 8

HEADS = 8
DK = 128
DV = 128
CHUNK = 64
N_EXPERTS = 16
N_GROUPS = 4
EPG = 4
N_PAIRS = 6
N_BUCKETS = N_GROUPS * N_PAIRS
D_FF = 512
EPS = 1e-6

TM = 256
N_TILES = T // TM
PROMPT_TILES = T_PROMPT // TM
TILES_PER_SAMPLE = SAMPLE_LEN // TM
CHUNKS_PER_TILE = TM // CHUNK
STEP_TILES = 2
STEP_ROWS = STEP_TILES * TM
N_STEPS = N_TILES // STEP_TILES
PROMPT_STEPS = PROMPT_TILES // STEP_TILES
MOVE_TILES = 2
MOVE_ROWS = MOVE_TILES * TM
N_MOVE_STEPS = N_TILES // MOVE_TILES
ROUTE_TILES = 4
CONV_TILES = 4
TM_E = 288
N_TILES_E = T // TM_E + N_BUCKETS
S_ROWS = N_TILES_E * TM_E
ZERO_ROWS = 32
XW = D + 128
LANES = 128
MOD_TN = 3072

V7X_VMEM_BYTES = 64 * 1024 * 1024
VMEM_LIMIT = V7X_VMEM_BYTES * 7 // 8

_SLOT_A = (0, 3, 3, 0, 0, 1)
_SLOT_B = (1, 1, 2, 2, 3, 2)
_SLOT_A_IS_HIGHER = tuple(int(a > b) for a, b in zip(_SLOT_A, _SLOT_B))


def _cparams():
    return pltpu.CompilerParams(dimension_semantics=("arbitrary",), vmem_limit_bytes=VMEM_LIMIT)


def _cond_of_tile(t):
    return jnp.where(t < PROMPT_TILES, 0, 1 + (t - PROMPT_TILES) // TILES_PER_SAMPLE)


def _rmsnorm(x, g):
    ms = jnp.mean(x * x, axis=-1, keepdims=True)
    return (x * lax.rsqrt(ms + EPS)) * g


def _modulate(x, g, shift, scale):
    return _rmsnorm(x, g) * (1.0 + scale) + shift


def _silu(x):
    return x * jax.nn.sigmoid(x)


def _dot(a, b):
    return jnp.dot(a, b, preferred_element_type=F32)


def _dot_nt(a, b):
    return lax.dot_general(a, b, (((1,), (1,)), ((), ())), preferred_element_type=F32)


def _dot_tn(a, b):
    return lax.dot_general(a, b, (((0,), (0,)), ((), ())), preferred_element_type=F32)


def _split_bf16(x):
    hi = x.astype(BF16)
    lo = (x - hi.astype(F32)).astype(BF16)
    return hi, lo


def _mod_kernel(cond_ref, w_ref, b_ref, o_ref):
    s = _silu(cond_ref[...])
    o_ref[0] = _dot(s.astype(BF16), w_ref[0].astype(BF16)) + b_ref[0]


def _adaln(cond, w_mod, b_mod):
    depth = w_mod.shape[0]
    return pl.pallas_call(
        _mod_kernel,
        grid=(depth, 6 * D // MOD_TN),
        in_specs=[
            pl.BlockSpec((COND_ROWS, D), lambda i, j: (0, 0)),
            pl.BlockSpec((1, D, MOD_TN), lambda i, j: (i, 0, j)),
            pl.BlockSpec((1, 1, MOD_TN), lambda i, j: (i, 0, j)),
        ],
        out_specs=pl.BlockSpec((1, COND_ROWS, MOD_TN), lambda i, j: (i, 0, j)),
        out_shape=jax.ShapeDtypeStruct((depth, COND_ROWS, 6 * D), F32),
        compiler_params=pltpu.CompilerParams(
            dimension_semantics=("arbitrary", "arbitrary"), vmem_limit_bytes=VMEM_LIMIT),
        name="adaln",
    )(cond, w_mod, b_mod.reshape(depth, 1, 6 * D))


_ROW_SPEC = pl.BlockSpec((1, D), lambda i: (0, 0))
_STEP_MOD_SPEC = pl.BlockSpec((1, 6, 1, D), lambda i, *_: (_cond_of_tile(i * STEP_TILES), 0, 0, 0))
_STEP_SPEC = pl.BlockSpec((STEP_ROWS, D), lambda i, *_: (i, 0))
_SUB_ROWS = tuple(slice(s * TM, (s + 1) * TM) for s in range(STEP_TILES))


def _load_weight_bf16(w_hbm, w16_ref, stage_ref, sem):
    n = w16_ref.shape[1] // D

    def slab(c):
        return pltpu.make_async_copy(w_hbm.at[:, pl.ds(c * D, D)], stage_ref.at[c % 2], sem.at[c % 2])

    slab(0).start()
    for c in range(n):
        if c + 1 < n:
            slab(c + 1).start()
        slab(c).wait()
        w16_ref[:, c * D:(c + 1) * D] = stage_ref[c % 2].astype(BF16)


def _weight_scratch(n_slabs):
    n_stage = min(n_slabs, 2)
    return [pltpu.VMEM((D, n_slabs * D), BF16), pltpu.VMEM((n_stage, D, D), F32),
            pltpu.SemaphoreType.DMA((n_stage,))]


_ANY_SPEC = pl.BlockSpec(memory_space=pl.ANY)


def _token_tile(x_refs, rows=slice(None), prompt_steps=PROMPT_TILES):
    if len(x_refs) == 1:
        return x_refs[0][rows]
    return jnp.where(pl.program_id(0) < prompt_steps, x_refs[0][rows], x_refs[1][rows])


def _token_specs(x_parts, tiles_per_step=1):
    rows = tiles_per_step * TM
    if len(x_parts) == 1:
        return [pl.BlockSpec((rows, D), lambda i: (i, 0))]
    prompt_steps = T_PROMPT // rows
    return [pl.BlockSpec((rows, D), lambda i: (jnp.minimum(i, prompt_steps - 1), 0)),
            pl.BlockSpec((rows, D), lambda i: (jnp.maximum(i - prompt_steps, 0), 0))]


def _hgrn_proj_kernel(xp_ref, xs_ref, mod_ref, g_ref, w_hbm, q_ref, v_ref, ff_ref, fb_ref, gg_ref,
                      w_ref, stage_ref, sem):
    @pl.when(pl.program_id(0) == 0)
    def _():
        _load_weight_bf16(w_hbm, w_ref, stage_ref, sem)

    h = [_modulate(_token_tile((xp_ref, xs_ref), rows, PROMPT_STEPS),
                   g_ref[...], mod_ref[0, 0], mod_ref[0, 1]).astype(BF16) for rows in _SUB_ROWS]
    for rows, h_tile in zip(_SUB_ROWS, h):
        q_ref[rows] = (_dot(h_tile, w_ref[:, 0 * D:1 * D]) * (DK ** -0.5)).astype(BF16)
        v_ref[rows] = _dot(h_tile, w_ref[:, 1 * D:2 * D]).astype(BF16)
        ff_ref[rows] = _dot(h_tile, w_ref[:, 2 * D:3 * D])
        fb_ref[rows] = _dot(h_tile, w_ref[:, 3 * D:4 * D])
        gg_ref[rows] = _dot(h_tile, w_ref[:, 4 * D:5 * D]).astype(BF16)


def _hgrn_proj(x_parts, mod, norm_g, w_in):
    bf = jax.ShapeDtypeStruct((T, D), BF16)
    f32 = jax.ShapeDtypeStruct((T, D), F32)
    return pl.pallas_call(
        _hgrn_proj_kernel,
        grid=(N_STEPS,),
        in_specs=_token_specs(x_parts, STEP_TILES) + [_STEP_MOD_SPEC, _ROW_SPEC, _ANY_SPEC],
        out_specs=[_STEP_SPEC] * 5,
        out_shape=[bf, bf, f32, f32, bf],
        scratch_shapes=_weight_scratch(5),
        compiler_params=_cparams(),
        name="hgrn_proj",
    )(*x_parts, mod, norm_g, w_in)


GLA_TILES = 4
GLA_ROWS = GLA_TILES * TM
N_GLA_STEPS = N_TILES // GLA_TILES
GLA_PROMPT_STEPS = PROMPT_TILES // GLA_TILES
STEPS_PER_SAMPLE = SAMPLE_LEN // GLA_ROWS
_GLA_SPEC = pl.BlockSpec((GLA_ROWS, D), lambda i: (i, 0))
_GLA_TILE_ROWS = tuple(slice(s * TM, (s + 1) * TM) for s in range(GLA_TILES))


def _gla_step(block, q_ref, v_ref, f_ref, lbl_ref, s0_ref, st_ref, st_tiles_ref, ma_ref, o_ref,
              *, reverse, finish=None):
    is_prompt = block < GLA_PROMPT_STEPS
    rel = block - GLA_PROMPT_STEPS
    edge = (STEPS_PER_SAMPLE - 1) if reverse else 0
    sample_start = jnp.logical_and(rel >= 0, rel % STEPS_PER_SAMPLE == edge)

    @pl.when(is_prompt)
    def _():
        st_ref[...] = jnp.zeros_like(st_ref)

    @pl.when(sample_start)
    def _():
        for h in range(HEADS):
            st_ref[h] = s0_ref[0, 0, h].T

    logits = lbl_ref[...]
    e = jnp.exp(logits - jnp.max(logits, axis=0, keepdims=True))
    lb = e[0:1] / jnp.sum(e, axis=0, keepdims=True)

    ref_idx = (CHUNK - 1 - CHUNK // 2) if reverse else CHUNK // 2
    last_idx = 0 if reverse else CHUNK - 1
    shift = CHUNK.bit_length() - 1

    row = lax.broadcasted_iota(jnp.int32, (TM, TM), 0)
    col = lax.broadcasted_iota(jnp.int32, (TM, TM), 1)
    same = (row >> shift) == (col >> shift)
    ref_row = ((row >> shift) << shift) + ref_idx
    keep = jnp.logical_and(same, (row <= col) if reverse else (row >= col))

    @pl.when(pl.program_id(0) == 0)
    def _():
        at_ref = jnp.logical_and(same, (ref_row <= col) if reverse else (ref_row >= col))
        ma_ref[...] = (keep.astype(F32) - at_ref.astype(F32)).astype(BF16)

    ma = ma_ref[...]

    rr = lax.broadcasted_iota(jnp.int32, (2 * CHUNKS_PER_TILE, TM), 0)
    cc = lax.broadcasted_iota(jnp.int32, (2 * CHUNKS_PER_TILE, TM), 1)
    chunk = rr & (CHUNKS_PER_TILE - 1)
    target = chunk * CHUNK + jnp.where(rr < CHUNKS_PER_TILE, ref_idx, last_idx)
    covered = (cc >= target) if reverse else (cc <= target)
    rm = jnp.logical_and((cc >> shift) == chunk, covered).astype(BF16)
    ma_rm = jnp.concatenate([ma, rm, jnp.zeros_like(rm)], axis=0)

    chunk_rows = [slice(c * CHUNK, (c + 1) * CHUNK) for c in range(CHUNKS_PER_TILE)]
    order = range(CHUNKS_PER_TILE - 1, -1, -1) if reverse else range(CHUNKS_PER_TILE)
    pair_w = 2 * DK
    n_pairs = HEADS // 2
    tile_order = _GLA_TILE_ROWS[::-1] if reverse else _GLA_TILE_ROWS

    def per_chunk_scale(x, scale):
        return jnp.concatenate([x[rs] * scale[c:c + 1] for c, rs in enumerate(chunk_rows)], axis=0)

    def pair_cols(p):
        return slice(p * pair_w, (p + 1) * pair_w)

    def stage_decay(k, p):
        lbp = lb[:, pair_cols(p)]
        f = lbp + (1.0 - lbp) * jax.nn.sigmoid(f_ref[tile_order[k], pair_cols(p)])
        lf_hi, lf_lo = _split_bf16(jnp.log(f))
        both = _dot(ma_rm, lf_hi) + _dot(ma_rm, lf_lo)
        z, marks = both[:TM], both[TM:TM + 2 * CHUNKS_PER_TILE]
        return 1.0 - f, z, marks

    def stage_scores(k, p, kk, z, marks):
        ref, last = marks[:CHUNKS_PER_TILE], marks[CHUNKS_PER_TILE:]
        qa32 = q_ref[tile_order[k], pair_cols(p)].astype(F32) * jnp.exp(z)
        kb32 = kk * jnp.exp(-z)
        qa = qa32.astype(BF16)
        kb = kb32.astype(BF16)
        qe = per_chunk_scale(qa32, jnp.exp(ref)).astype(BF16)
        kd = per_chunk_scale(kb32, jnp.exp(last - ref)).astype(BF16)
        v = v_ref[tile_order[k], pair_cols(p)]
        scores, incr = [], []
        for hh in range(2):
            ls = slice(hh * DK, (hh + 1) * DK)
            scores.append(_dot_nt(qa[:, ls], kb[:, ls]))
            kd_h = kd[:, ls]
            kd_blocks = jnp.concatenate(
                [jnp.concatenate([part for part in (jnp.zeros((c * CHUNK, DK), BF16), kd_h[rs],
                                                    jnp.zeros((TM - (c + 1) * CHUNK, DK), BF16))
                                  if part.shape[0]], axis=0)
                 for c, rs in enumerate(chunk_rows)], axis=1)
            incr.append(_dot_tn(v[:, ls], kd_blocks))
        return scores, incr, qe, jnp.exp(last), v

    def stage_output(k, p, scores, incr, qe, decay, v):
        base = tile_order[k].start
        for hh in range(2):
            h = 2 * p + hh
            ls = slice(hh * DK, (hh + 1) * DK)
            hs = slice(h * DK, (h + 1) * DK)
            o_intra = _dot(jnp.where(keep, scores[hh], 0.0).astype(BF16), v[:, ls])
            st = st_ref[h]
            if k > 0:
                st = jnp.where(is_prompt, 0.0, st)
            for c in order:
                rs = chunk_rows[c]
                o_ref[base + rs.start:base + rs.stop, hs] = (
                    o_intra[rs] + _dot_nt(qe[rs, ls], st.astype(BF16)))
                st = st * decay[c:c + 1, ls] + incr[hh][:, c * DK:(c + 1) * DK]
            st_ref[h] = st
            if k < GLA_TILES - 1:
                st_tiles_ref[k, h] = st
        if finish is not None:
            return (tile_order[k], p)

    def stage_finish(k, p, rows, pair):
        finish(rows, pair)

    units = [(k, p) for k in range(GLA_TILES) for p in range(n_pairs)]
    stages = [stage_decay, stage_scores, stage_output] + ([stage_finish] if finish is not None else [])
    carried = [dict() for _ in stages]
    for step in range(len(units) + len(stages) - 1):
        for depth, stage in (enumerate(stages) if reverse else reversed(list(enumerate(stages)))):
            u = step - depth
            if 0 <= u < len(units):
                out = stage(*units[u], *(carried[depth].pop(u) if depth else ()))
                if depth + 1 < len(stages):
                    carried[depth + 1][u] = out


def _store_prompt_states(block, st_ref, st_tiles_ref, snew_ref, *, reverse, sfwd_ref=None):
    @pl.when(block < GLA_PROMPT_STEPS)
    def _():
        for k in range(GLA_TILES):
            tile = GLA_TILES - 1 - k if reverse else k
            for h in range(HEADS):
                state = (st_tiles_ref[k, h] if k < GLA_TILES - 1 else st_ref[h]).T
                if sfwd_ref is None:
                    snew_ref[tile, h] = state
                else:
                    snew_ref[tile, 1, h] = state
        if sfwd_ref is not None:
            snew_ref[:, 0] = sfwd_ref[...]


def _gla_fwd_kernel(q_ref, v_ref, f_ref, lbl_ref, s0_ref, o_ref, snew_ref, st_ref, st_tiles_ref, ma_ref):
    block = pl.program_id(0)
    _gla_step(block, q_ref, v_ref, f_ref, lbl_ref, s0_ref, st_ref, st_tiles_ref, ma_ref, o_ref,
              reverse=False)
    _store_prompt_states(block, st_ref, st_tiles_ref, snew_ref, reverse=False)


def _gla_bwd_kernel(q_ref, v_ref, f_ref, lbl_ref, s0_ref, of_ref, gg_ref, ng_ref, sfwd_ref,
                    a_ref, snew_ref, st_ref, st_tiles_ref, ma_ref, ob_ref):
    block = N_GLA_STEPS - 1 - pl.program_id(0)

    def finish(rows, p):
        for h in (2 * p, 2 * p + 1):
            hs = slice(h * DV, (h + 1) * DV)
            o = of_ref[rows, hs] + ob_ref[rows, hs]
            a_ref[rows, hs] = (_rmsnorm(o, ng_ref[...]) * _silu(gg_ref[rows, hs].astype(F32))).astype(BF16)

    _gla_step(block, q_ref, v_ref, f_ref, lbl_ref, s0_ref, st_ref, st_tiles_ref, ma_ref, ob_ref,
              reverse=True, finish=finish)
    _store_prompt_states(block, st_ref, st_tiles_ref, snew_ref, reverse=True, sfwd_ref=sfwd_ref)


def _state_specs(direction, block_of_step):
    sample = lambda i: jnp.clip((block_of_step(i) - GLA_PROMPT_STEPS) // STEPS_PER_SAMPLE, 0, N_SAMPLE_SEQ - 1)
    s0 = pl.BlockSpec((1, 1, HEADS, DK, DV), lambda i: (sample(i), direction, 0, 0, 0))
    snew = pl.BlockSpec((GLA_TILES, HEADS, DK, DV),
                        lambda i: (jnp.minimum(block_of_step(i), GLA_PROMPT_STEPS - 1), 0, 0, 0))
    return s0, snew


_GLA_STATE_SCRATCH = [pltpu.VMEM((HEADS, DV, DK), F32), pltpu.VMEM((GLA_TILES - 1, HEADS, DV, DK), F32),
                      pltpu.VMEM((TM, TM), BF16)]


def _gla_fwd(q, v, ff, lb_logits, s0):
    s0_spec, snew_spec = _state_specs(0, lambda i: i)
    return pl.pallas_call(
        _gla_fwd_kernel,
        grid=(N_GLA_STEPS,),
        in_specs=[_GLA_SPEC, _GLA_SPEC, _GLA_SPEC,
                  pl.BlockSpec(lb_logits.shape, lambda i: (0, 0)), s0_spec],
        out_specs=[_GLA_SPEC, snew_spec],
        out_shape=[jax.ShapeDtypeStruct((T, D), F32),
                   jax.ShapeDtypeStruct((N_PROMPT_SEQ, HEADS, DK, DV), F32)],
        scratch_shapes=_GLA_STATE_SCRATCH,
        compiler_params=_cparams(),
        name="gla_fwd",
    )(q, v, ff, lb_logits, s0)


def _gla_bwd(q, v, fb, lb_logits, s0, o_f, gg, head_norm_g, s_fwd):
    rev = lambda i: N_GLA_STEPS - 1 - i
    block = pl.BlockSpec((GLA_ROWS, D), lambda i: (rev(i), 0))
    s0_spec, sfwd_spec = _state_specs(1, rev)
    prompt_block = lambda i: jnp.minimum(rev(i), GLA_PROMPT_STEPS - 1)
    return pl.pallas_call(
        _gla_bwd_kernel,
        grid=(N_GLA_STEPS,),
        in_specs=[block, block, block, pl.BlockSpec(lb_logits.shape, lambda i: (0, 0)), s0_spec,
                  block, block, pl.BlockSpec((1, DV), lambda i: (0, 0)), sfwd_spec],
        out_specs=[block, pl.BlockSpec((GLA_TILES, 2, HEADS, DK, DV), lambda i: (prompt_block(i), 0, 0, 0, 0))],
        out_shape=[jax.ShapeDtypeStruct((T, D), BF16),
                   jax.ShapeDtypeStruct((N_PROMPT_SEQ, 2, HEADS, DK, DV), F32)],
        scratch_shapes=_GLA_STATE_SCRATCH + [pltpu.VMEM((GLA_ROWS, D), F32)],
        compiler_params=_cparams(),
        name="gla_bwd",
    )(q, v, fb, lb_logits, s0, o_f, gg, head_norm_g, s_fwd)


def _conv_mixer_kernel(x_ref, mod_ref, g_ref, w_hbm, cw_ref, a_ref, w_ref, stage_ref, sem):
    @pl.when(pl.program_id(0) == 0)
    def _():
        _load_weight_bf16(w_hbm, w_ref, stage_ref, sem)

    seg = jnp.where(pl.program_id(0) < PROMPT_TILES // CONV_TILES, PROMPT_LEN, GRID_W)
    pos = lax.broadcasted_iota(jnp.int32, (TM, 1), 0) & (seg - 1)

    def project(rows):
        h = _modulate(x_ref[rows], g_ref[...], mod_ref[0, 0], mod_ref[0, 1]).astype(BF16)
        return [_dot(h, w_ref[:, k * D:(k + 1) * D]) for k in range(3)]

    def convolve(rows, bg, cg, x_in):
        u = cg * x_in
        prev = jnp.where(pos == 0, 0.0, pltpu.roll(u, 1, axis=0))
        nxt = jnp.where(pos == seg - 1, 0.0, pltpu.roll(u, TM - 1, axis=0))
        conv = cw_ref[0:1] * prev + cw_ref[1:2] * u + cw_ref[2:3] * nxt
        a_ref[rows] = (bg * conv).astype(BF16)

    tiles = [slice(s * TM, (s + 1) * TM) for s in range(CONV_TILES)]
    projected = [project(rows) for rows in tiles]
    for rows, parts in zip(tiles, projected):
        convolve(rows, *parts)


def _conv_mixer(x, mod, norm_g, w_in, conv_w):
    block = pl.BlockSpec((CONV_TILES * TM, D), lambda i: (i, 0))
    return pl.pallas_call(
        _conv_mixer_kernel,
        grid=(N_TILES // CONV_TILES,),
        in_specs=[block, pl.BlockSpec((1, 6, 1, D), lambda i: (_cond_of_tile(i * CONV_TILES), 0, 0, 0)),
                  _ROW_SPEC, _ANY_SPEC, pl.BlockSpec((3, D), lambda i: (0, 0))],
        out_specs=block,
        out_shape=jax.ShapeDtypeStruct((T, D), BF16),
        scratch_shapes=_weight_scratch(3),
        compiler_params=_cparams(),
        name="conv_mixer",
    )(x, mod, norm_g, w_in, conv_w)


ROUTE_ROWS = 32


def _first_member(vals, target):
    idx = jnp.full_like(target, float(EPG - 1))
    for j in range(EPG - 2, -1, -1):
        idx = jnp.where(vals[j] == target, float(j), idx)
    return idx


def _router_logits(h2, rwt_ref):
    h_hi, h_lo = _split_bf16(h2)
    w_hi, w_lo = _split_bf16(rwt_ref[...])
    with_hi = _dot_nt(jnp.concatenate([w_hi, w_lo], axis=0), h_hi)
    return with_hi[:N_EXPERTS] + (_dot_nt(w_hi, h_lo) + with_hi[N_EXPERTS:])


def _select_experts(logits, rb_ref):
    scores = jax.nn.sigmoid(logits)
    sel = scores + rb_ref[...]
    member = [sel[j * N_GROUPS:(j + 1) * N_GROUPS] for j in range(EPG)]
    m1 = functools.reduce(jnp.maximum, member)
    i1 = _first_member(member, m1)
    rest = [jnp.where(i1 == float(j), -jnp.inf, member[j]) for j in range(EPG)]
    m2 = functools.reduce(jnp.maximum, rest)
    i2 = _first_member(rest, m2)
    group_score = m1 + m2

    best, grp, first, second = group_score[0:1], jnp.zeros((1, TM), F32), i1[0:1], i2[0:1]
    for g in range(1, N_GROUPS):
        upd = group_score[g:g + 1] > best
        best = jnp.where(upd, group_score[g:g + 1], best)
        grp = jnp.where(upd, float(g), grp)
        first = jnp.where(upd, i1[g:g + 1], first)
        second = jnp.where(upd, i2[g:g + 1], second)
    a = jnp.minimum(first, second)
    b = jnp.maximum(first, second)
    row = lax.broadcasted_iota(jnp.int32, (N_EXPERTS, TM), 0).astype(F32)
    s_lo = jnp.sum(jnp.where(row == a * N_GROUPS + grp, scores, 0.0), axis=0, keepdims=True)
    s_hi = jnp.sum(jnp.where(row == b * N_GROUPS + grp, scores, 0.0), axis=0, keepdims=True)
    tot = s_lo + s_hi
    pair = jnp.where(a == 0.0, jnp.where(b == 1.0, 0.0, jnp.where(b == 2.0, 3.0, 4.0)),
                     jnp.where(a == 1.0, jnp.where(b == 2.0, 5.0, 1.0), 2.0))
    return grp * N_PAIRS + pair, s_lo / tot, s_hi / tot


def _post_mixer_kernel(*refs, n_x):
    a_ref, x_refs = refs[0], refs[1:1 + n_x]
    (mod_ref, g2_ref, wo_hbm, rwt_ref, rb_ref,
     x1_ref, gates_ref, bucket_ref, rank_ref, cnt_ref, carry_ref, earlier_ref, wo_ref, stage_ref, sem) = refs[1 + n_x:]

    @pl.when(pl.program_id(0) == 0)
    def _():
        _load_weight_bf16(wo_hbm, wo_ref, stage_ref, sem)
        carry_ref[...] = jnp.zeros_like(carry_ref)
        s_idx = lax.broadcasted_iota(jnp.int32, (TM, TM), 0)
        t_idx = lax.broadcasted_iota(jnp.int32, (TM, TM), 1)
        earlier_ref[...] = (s_idx < t_idx).astype(BF16)

    def stage_mix(s):
        rows = slice(s * TM, (s + 1) * TM)
        x1 = (_token_tile(x_refs, rows, PROMPT_TILES // ROUTE_TILES)
              + mod_ref[0, 2] * _dot(a_ref[rows], wo_ref[...]))
        x1_ref[rows] = x1
        return _modulate(x1, g2_ref[...], mod_ref[0, 3], mod_ref[0, 4])

    def stage_rank(s, bucket, g_lo, g_hi):
        onehot = lax.broadcasted_iota(jnp.int32, (ROUTE_ROWS, TM), 0).astype(F32) == bucket
        before = _dot(onehot.astype(BF16), earlier_ref[...])
        oh = onehot.astype(F32)
        carry = carry_ref[...]
        rank = jnp.sum(oh * (before + carry[:, 0:1]), axis=0, keepdims=True)
        carry_ref[...] = carry + jnp.sum(oh, axis=1, keepdims=True)
        bucket_ref[s] = bucket.astype(jnp.int32)
        rank_ref[s] = rank.astype(jnp.int32)
        grow = lax.broadcasted_iota(jnp.int32, (LANES, TM), 0)
        gates_ref[s * TM:(s + 1) * TM] = jnp.where(grow == 0, g_lo, jnp.where(grow == 1, g_hi, 0.0)).T

    tiles = range(ROUTE_TILES)
    h2 = [stage_mix(s) for s in tiles]
    logits = [_router_logits(h2[s], rwt_ref) for s in tiles]
    picks = [_select_experts(logits[s], rb_ref) for s in tiles]
    for s in tiles:
        stage_rank(s, *picks[s])
    cnt_ref[...] = carry_ref[...]


def _post_mixer(a, x_parts, mod, norm_g2, w_out, router_wt, router_bt):
    rows = ROUTE_TILES * TM
    idx_spec = pl.BlockSpec((ROUTE_TILES, 1, TM), lambda i: (i, 0, 0))
    return pl.pallas_call(
        functools.partial(_post_mixer_kernel, n_x=len(x_parts)),
        grid=(N_TILES // ROUTE_TILES,),
        in_specs=[pl.BlockSpec((rows, D), lambda i: (i, 0))] + _token_specs(x_parts, ROUTE_TILES) + [
            pl.BlockSpec((1, 6, 1, D), lambda i: (_cond_of_tile(i * ROUTE_TILES), 0, 0, 0)), _ROW_SPEC,
            _ANY_SPEC,
            pl.BlockSpec((N_EXPERTS, D), lambda i: (0, 0)),
            pl.BlockSpec((N_EXPERTS, 1), lambda i: (0, 0))],
        out_specs=[pl.BlockSpec((rows, D), lambda i: (i, 0)), pl.BlockSpec((rows, LANES), lambda i: (i, 0)),
                   idx_spec, idx_spec, pl.BlockSpec((ROUTE_ROWS, LANES), lambda i: (0, 0))],
        out_shape=[jax.ShapeDtypeStruct((T, D), F32), jax.ShapeDtypeStruct((T, LANES), F32),
                   jax.ShapeDtypeStruct((N_TILES, 1, TM), jnp.int32),
                   jax.ShapeDtypeStruct((N_TILES, 1, TM), jnp.int32),
                   jax.ShapeDtypeStruct((ROUTE_ROWS, LANES), F32)],
        scratch_shapes=[pltpu.VMEM((ROUTE_ROWS, LANES), F32), pltpu.VMEM((TM, TM), BF16)] + _weight_scratch(1),
        compiler_params=_cparams(),
        name="post_mixer",
    )(a, *x_parts, mod, norm_g2, w_out, router_wt, router_bt)


_MOVE_SPEC = pl.BlockSpec((MOVE_ROWS, D), lambda i, *_: (i, 0))
_MOVE_MOD_SPEC = pl.BlockSpec((1, 6, 1, D), lambda i, *_: (_cond_of_tile(i * MOVE_TILES), 0, 0, 0))


def _smem_step_spec(step_of_step):
    return pl.BlockSpec((MOVE_TILES, 1, TM), lambda i, *_: (step_of_step(i), 0, 0), memory_space=pltpu.SMEM)


def _start_step_rows(make_copy):
    for tile in range(MOVE_TILES):
        for r in range(TM):
            make_copy(tile, r).start(priority=r % 2)


def _dispatch_kernel(zero_from_ref, slot_ref, x1_ref, mod_ref, g2_ref, gates_ref,
                     xs_ref, buf_ref, zero_ref, sems, zero_sem):
    i = pl.program_id(0)
    cur = i % 2

    @pl.when(i == 0)
    def _():
        zero_ref[...] = jnp.zeros_like(zero_ref)

        def for_each_chunk(act):
            def tile(j, carry):
                def chunk(c, inner):
                    row = pl.multiple_of(j * TM_E + c * ZERO_ROWS, ZERO_ROWS)
                    act(pltpu.make_async_copy(zero_ref, xs_ref.at[pl.ds(row, ZERO_ROWS)], zero_sem))
                    return inner
                return lax.fori_loop(zero_from_ref[j], TM_E // ZERO_ROWS, chunk, carry)
            lax.fori_loop(0, N_TILES_E, tile, 0)

        for_each_chunk(lambda copy: copy.start())
        for_each_chunk(lambda copy: copy.wait())

    def wait_step(b):
        pltpu.make_async_copy(buf_ref.at[b], xs_ref.at[pl.ds(0, MOVE_ROWS)], sems.at[b]).wait()

    h2 = _modulate(x1_ref[...], g2_ref[...], mod_ref[0, 3], mod_ref[0, 4])
    gates = gates_ref[...]

    def send(b):
        buf_ref[b, :, :D] = h2
        buf_ref[b, :, D:] = gates
        _start_step_rows(lambda tile, r: pltpu.make_async_copy(
            buf_ref.at[b, pl.ds(tile * TM + r, 1)], xs_ref.at[pl.ds(slot_ref[tile, 0, r], 1)], sems.at[b]))

        @pl.when(i > 0)
        def _():
            wait_step(1 - b)

        @pl.when(i == N_MOVE_STEPS - 1)
        def _():
            wait_step(b)

    for b in range(2):
        pl.when(cur == b)(functools.partial(send, b))


def _dispatch(zero_from, slot, x1, mod, norm_g2, gates):
    return pl.pallas_call(
        _dispatch_kernel,
        grid_spec=pltpu.PrefetchScalarGridSpec(
            num_scalar_prefetch=1,
            grid=(N_MOVE_STEPS,),
            in_specs=[_smem_step_spec(lambda i: i), _MOVE_SPEC, _MOVE_MOD_SPEC,
                      pl.BlockSpec((1, D), lambda i, *_: (0, 0)),
                      pl.BlockSpec((MOVE_ROWS, LANES), lambda i, *_: (i, 0))],
            out_specs=pl.BlockSpec(memory_space=pl.ANY),
            scratch_shapes=[pltpu.VMEM((2, MOVE_ROWS, XW), F32), pltpu.VMEM((ZERO_ROWS, XW), F32),
                            pltpu.SemaphoreType.DMA((2,)), pltpu.SemaphoreType.DMA(())],
        ),
        out_shape=jax.ShapeDtypeStruct((S_ROWS, XW), F32),
        compiler_params=_cparams(),
        name="moe_dispatch",
    )(zero_from, slot, x1, mod, norm_g2, gates)


_W_SHAPES = ((D, D_FF), (D, D_FF), (D_FF, D))


def _expert_kernel(ea_ref, eb_ref, next_a_ref, next_b_ref, a_higher_ref, nrows_ref, xblock_ref,
                   x_ref, wg_hbm, wu_hbm, wd_hbm, y_ref, *scratch, layer):
    stage, w16, sems = scratch[0:3], scratch[3:6], scratch[6]
    j = pl.program_id(0)
    n = nrows_ref[j]
    prev = jnp.maximum(j - 1, 0)

    def fetch(slot, expert):
        return [pltpu.make_async_copy(w.at[layer, expert], s.at[slot], sems.at[slot])
                for w, s in zip((wg_hbm, wu_hbm, wd_hbm), stage)]

    for slot, e_ref, next_ref in ((0, ea_ref, next_a_ref), (1, eb_ref, next_b_ref)):
        @pl.when(j == 0)
        def _(slot=slot, e_ref=e_ref):
            for copy in fetch(slot, e_ref[0]):
                copy.start()

        @pl.when(jnp.logical_or(j == 0, e_ref[j] != e_ref[prev]))
        def _(slot=slot, e_ref=e_ref, next_ref=next_ref):
            for copy in fetch(slot, e_ref[j]):
                copy.wait()
            for s, d in zip(stage, w16):
                d[slot] = s[slot].astype(BF16)

            @pl.when(next_ref[j] >= 0)
            def _():
                for copy in fetch(slot, next_ref[j]):
                    copy.start(priority=1)

    def run(rows):
        valid = lax.broadcasted_iota(jnp.int32, (rows, 1), 0) < n
        xe = jnp.where(valid, x_ref[:rows], 0.0)
        x = xe[:, :D].astype(BF16)
        g_lo, g_hi = xe[:, D:D + 1], xe[:, D + 1:D + 2]
        a_higher = a_higher_ref[j] != 0

        def ffn(slot, gate):
            wg_ref, wu_ref, wd_ref = w16
            act = _silu(_dot(x, wg_ref[slot])) * _dot(x, wu_ref[slot]) * gate
            return _dot(act.astype(BF16), wd_ref[slot])

        y_ref[:rows] = (ffn(0, jnp.where(a_higher, g_hi, g_lo))
                        + ffn(1, jnp.where(a_higher, g_lo, g_hi)))
        if rows < TM_E:
            y_ref[rows:] = jnp.zeros((TM_E - rows, D), F32)

    half = TM_E // 2
    pl.when(n > half)(functools.partial(run, TM_E))
    pl.when(jnp.logical_and(n > 0, n <= half))(functools.partial(run, half))

    @pl.when(n == 0)
    def _():
        y_ref[...] = jnp.zeros_like(y_ref)


def _experts(tile_ea, tile_eb, tile_next_a, tile_next_b, tile_a_higher, tile_rows, tile_xblock, xs,
             w_gate, w_up, w_down, layer):
    any_spec = pl.BlockSpec(memory_space=pl.ANY)
    return pl.pallas_call(
        functools.partial(_expert_kernel, layer=layer),
        grid_spec=pltpu.PrefetchScalarGridSpec(
            num_scalar_prefetch=7,
            grid=(N_TILES_E,),
            in_specs=[pl.BlockSpec((TM_E, XW), lambda j, *plan: (plan[-1][j], 0)), any_spec, any_spec, any_spec],
            out_specs=pl.BlockSpec((TM_E, D), lambda j, *_: (j, 0)),
            scratch_shapes=[pltpu.VMEM((2,) + s, F32) for s in _W_SHAPES]
                           + [pltpu.VMEM((2,) + s, BF16) for s in _W_SHAPES]
                           + [pltpu.SemaphoreType.DMA((2,))],
        ),
        out_shape=jax.ShapeDtypeStruct((S_ROWS, D), F32),
        compiler_params=_cparams(),
        name="moe_experts",
    )(tile_ea, tile_eb, tile_next_a, tile_next_b, tile_a_higher, tile_rows, tile_xblock, xs,
      w_gate, w_up, w_down)


def _combine_kernel(slot_ref, next_slot_ref, x1_ref, mod_ref, gf_ref, y_ref, *rest, final):
    out_refs, (buf_ref, sems) = rest[:-2], rest[-2:]
    i = pl.program_id(0)
    cur = i % 2

    def gather(slots, b):
        _start_step_rows(lambda tile, r: pltpu.make_async_copy(
            y_ref.at[pl.ds(slots[tile, 0, r], 1)], buf_ref.at[b, pl.ds(tile * TM + r, 1)], sems.at[b]))

    @pl.when(i == 0)
    def _():
        gather(slot_ref, 0)

    def combine(b):
        @pl.when(i + 1 < N_MOVE_STEPS)
        def _():
            gather(next_slot_ref, 1 - b)

        pltpu.make_async_copy(y_ref.at[pl.ds(0, MOVE_ROWS)], buf_ref.at[b], sems.at[b]).wait()
        x2 = x1_ref[...] + mod_ref[0, 5] * buf_ref[b]
        if not final:
            out_refs[0][...] = x2
            return
        y = _rmsnorm(x2, gf_ref[...])
        yp_ref, ys_ref = out_refs

        @pl.when(i < T_PROMPT // MOVE_ROWS)
        def _():
            yp_ref[...] = y

        @pl.when(i >= T_PROMPT // MOVE_ROWS)
        def _():
            ys_ref[...] = y

    for b in range(2):
        pl.when(cur == b)(functools.partial(combine, b))


def _combine(slot, x1, mod, final_g, y_sorted, *, final):
    if final:
        out_specs = _token_specs((None, None), MOVE_TILES)
        out_shape = [jax.ShapeDtypeStruct((T_PROMPT, D), F32), jax.ShapeDtypeStruct((T_SAMPLE, D), F32)]
    else:
        out_specs = [_MOVE_SPEC]
        out_shape = [jax.ShapeDtypeStruct((T, D), F32)]
    return pl.pallas_call(
        functools.partial(_combine_kernel, final=final),
        grid=(N_MOVE_STEPS,),
        in_specs=[_smem_step_spec(lambda i: i), _smem_step_spec(lambda i: jnp.minimum(i + 1, N_MOVE_STEPS - 1)),
                  _MOVE_SPEC, _MOVE_MOD_SPEC, _ROW_SPEC, pl.BlockSpec(memory_space=pl.ANY)],
        out_specs=out_specs,
        scratch_shapes=[pltpu.VMEM((2, MOVE_ROWS, D), F32), pltpu.SemaphoreType.DMA((2,))],
        out_shape=out_shape,
        compiler_params=_cparams(),
        name="moe_combine_final" if final else "moe_combine",
    )(slot, slot, x1, mod, final_g, y_sorted)


def _plan_kernel(cnt_ref, bucket_ref, rank_ref, slot_ref,
                 ea_ref, eb_ref, next_a_ref, next_b_ref, a_higher_ref, rows_ref, xblock_ref, zero_from_ref):
    slot = rank_ref[...]
    bucket = bucket_ref[...]
    tile0 = jnp.int32(0)
    last = [jnp.int32(0)] * 3
    for b in range(N_BUCKETS):
        n = cnt_ref[b]
        tiles = (n + (TM_E - 1)) // TM_E
        pair = b % N_PAIRS
        per_tile = ((b // N_PAIRS) * EPG + _SLOT_A[pair], (b // N_PAIRS) * EPG + _SLOT_B[pair],
                    _SLOT_A_IS_HIGHER[pair])

        def fill(k, carry, n=n, tile0=tile0, per_tile=per_tile):
            j = tile0 + k
            live = jnp.minimum(n - k * TM_E, TM_E)
            rows_ref[j] = live
            xblock_ref[j] = j
            zero_from_ref[j] = live // ZERO_ROWS
            for ref, value in zip((ea_ref, eb_ref, a_higher_ref), per_tile):
                ref[j] = jnp.int32(value)
            return carry

        lax.fori_loop(0, tiles, fill, 0)
        slot = slot + jnp.where(bucket == b, tile0 * TM_E, 0)
        last = [jnp.where(tiles > 0, value, old) for value, old in zip(per_tile, last)]
        tile0 = tile0 + tiles

    def unused(j, carry):
        rows_ref[j] = jnp.int32(0)
        xblock_ref[j] = jnp.maximum(tile0 - 1, 0)
        zero_from_ref[j] = jnp.int32(0)
        for ref, value in zip((ea_ref, eb_ref, a_higher_ref), last):
            ref[j] = value
        return carry

    lax.fori_loop(tile0, N_TILES_E, unused, 0)

    for e_ref, next_ref in ((ea_ref, next_a_ref), (eb_ref, next_b_ref)):
        next_ref[N_TILES_E - 1] = jnp.int32(-1)

        def backward(k, carry, e_ref=e_ref, next_ref=next_ref):
            j = N_TILES_E - 2 - k
            next_ref[j] = jnp.where(e_ref[j] == e_ref[j + 1], next_ref[j + 1], e_ref[j + 1])
            return carry

        lax.fori_loop(0, N_TILES_E - 1, backward, 0)
    slot_ref[...] = slot


def _plan(counts, bucket, rank):
    whole = pl.BlockSpec(bucket.shape, lambda i, *_: (0, 0, 0))
    smem = pl.BlockSpec(memory_space=pltpu.SMEM)
    per_tile = jax.ShapeDtypeStruct((N_TILES_E,), jnp.int32)
    slot, *tile_plan = pl.pallas_call(
        _plan_kernel,
        grid_spec=pltpu.PrefetchScalarGridSpec(
            num_scalar_prefetch=1, grid=(1,), in_specs=[whole, whole], out_specs=[whole] + [smem] * 8),
        out_shape=[jax.ShapeDtypeStruct(bucket.shape, jnp.int32)] + [per_tile] * 8,
        compiler_params=_cparams(),
        name="moe_plan",
    )(counts[:, 0].astype(jnp.int32), bucket, rank)
    return slot, tile_plan


def _moe(a, x_parts, mod, norm_g2, w_out, router_wt, router_bt, w_gate, w_up, w_down, layer, final_g,
         *, final):
    x1, gates, bucket, rank, counts = _post_mixer(a, x_parts, mod, norm_g2, w_out, router_wt, router_bt)
    slot, (*expert_plan, zero_from) = _plan(counts, bucket, rank)
    xs = _dispatch(zero_from, slot, x1, mod, norm_g2, gates)
    ys = _experts(*expert_plan, xs, w_gate, w_up, w_down, layer)
    return _combine(slot, x1, mod, final_g, ys, final=final)


def kernel(x_prompt, x_sample, state_hgrn, c, c_ctx, w_mod, b_mod, norm_mix_g, norm_ffn_g, norm_final_g,
           hgrn_w_in, hgrn_lb_logits, hgrn_norm_g, hgrn_w_out, conv_w_in, conv_w, conv_w_out,
           router_w, router_b, moe_w_gate, moe_w_up, moe_w_down):
    x_parts = (x_prompt.reshape(T_PROMPT, D), x_sample.reshape(T_SAMPLE, D))
    cond = jnp.concatenate([c_ctx[None], c, jnp.zeros((COND_ROWS - N_COND, D), F32)], axis=0)
    mods = _adaln(cond, w_mod, b_mod)[:, :N_COND].reshape(2, N_COND, 6, 1, D)
    member_major = lambda w: w.reshape(N_GROUPS, EPG, -1).transpose(1, 0, 2).reshape(N_EXPERTS, -1)
    rwt = member_major(router_w.T)
    rbt = member_major(router_b.reshape(N_EXPERTS, 1))
    final_g = norm_final_g.reshape(1, D)

    q, v, ff, fb, gg = _hgrn_proj(x_parts, mods[0], norm_mix_g[0:1], hgrn_w_in[0])
    s0 = state_hgrn[:, 0]
    o_f, s_f = _gla_fwd(q, v, ff, hgrn_lb_logits, s0)
    a, new_state = _gla_bwd(q, v, fb, hgrn_lb_logits, s0, o_f, gg, hgrn_norm_g[0:1], s_f)
    (x,) = _moe(a, x_parts, mods[0], norm_ffn_g[0:1], hgrn_w_out[0], rwt, rbt,
                moe_w_gate, moe_w_up, moe_w_down, 0, final_g, final=False)

    a = _conv_mixer(x, mods[1], norm_mix_g[1:2], conv_w_in[0], conv_w[0])
    y_p, y_s = _moe(a, (x,), mods[1], norm_ffn_g[1:2], conv_w_out[0], rwt, rbt,
                    moe_w_gate, moe_w_up, moe_w_down, 1, final_g, final=True)

    return (y_p.reshape(N_PROMPT_SEQ, PROMPT_LEN, D), y_s.reshape(N_SAMPLE_SEQ, SAMPLE_LEN, D),
            new_state[:, None])
```

```python
import functools

import jax
import jax.numpy as jnp
from jax import lax
from jax.experimental import pallas as pl
from jax.experimental.pallas import tpu as pltpu

F32 = jnp.float32
BF16 = jnp.bfloat16

D = 1024
N_PROMPT_SEQ = 16
PROMPT_LEN = 256
N_SAMPLE_SEQ = 2
SAMPLE_LEN = 4096
GRID_W = 64
T_PROMPT = N_PROMPT_SEQ * PROMPT_LEN
T_SAMPLE = N_SAMPLE_SEQ * SAMPLE_LEN
T = T_PROMPT + T_SAMPLE
N_COND = 1 + N_SAMPLE_SEQ
COND_ROWS = 8

HEADS = 8
DK = 128
DV = 128
CHUNK = 64
N_EXPERTS = 16
N_GROUPS = 4
EPG = 4
N_PAIRS = 6
N_BUCKETS = N_GROUPS * N_PAIRS
D_FF = 512
EPS = 1e-6

TM = 256
N_TILES = T // TM
PROMPT_TILES = T_PROMPT // TM
TILES_PER_SAMPLE = SAMPLE_LEN // TM
CHUNKS_PER_TILE = TM // CHUNK
STEP_TILES = 2
STEP_ROWS = STEP_TILES * TM
N_STEPS = N_TILES // STEP_TILES
PROMPT_STEPS = PROMPT_TILES // STEP_TILES
MOVE_TILES = 2
MOVE_ROWS = MOVE_TILES * TM
N_MOVE_STEPS = N_TILES // MOVE_TILES
ROUTE_TILES = 4
CONV_TILES = 4
TM_E = 288
N_TILES_E = T // TM_E + N_BUCKETS
S_ROWS = N_TILES_E * TM_E
ZERO_ROWS = 32
XW = D + 128
LANES = 128
MOD_TN = 3072

V7X_VMEM_BYTES = 64 * 1024 * 1024
VMEM_LIMIT = V7X_VMEM_BYTES * 7 // 8

_SLOT_A = (0, 3, 3, 0, 0, 1)
_SLOT_B = (1, 1, 2, 2, 3, 2)
_SLOT_A_IS_HIGHER = tuple(int(a > b) for a, b in zip(_SLOT_A, _SLOT_B))


def _cparams():
    return pltpu.CompilerParams(dimension_semantics=("arbitrary",), vmem_limit_bytes=VMEM_LIMIT)


def _cond_of_tile(t):
    return jnp.where(t < PROMPT_TILES, 0, 1 + (t - PROMPT_TILES) // TILES_PER_SAMPLE)


def _rmsnorm(x, g):
    ms = jnp.mean(x * x, axis=-1, keepdims=True)
    return (x * lax.rsqrt(ms + EPS)) * g


def _modulate(x, g, shift, scale):
    return _rmsnorm(x, g) * (1.0 + scale) + shift


def _silu(x):
    return x * jax.nn.sigmoid(x)


def _dot(a, b):
    return jnp.dot(a, b, preferred_element_type=F32)


def _dot_nt(a, b):
    return lax.dot_general(a, b, (((1,), (1,)), ((), ())), preferred_element_type=F32)


def _dot_tn(a, b):
    return lax.dot_general(a, b, (((0,), (0,)), ((), ())), preferred_element_type=F32)


def _split_bf16(x):
    hi = x.astype(BF16)
    lo = (x - hi.astype(F32)).astype(BF16)
    return hi, lo


def _mod_kernel(cond_ref, w_ref, b_ref, o_ref):
    s = _silu(cond_ref[...])
    o_ref[0] = _dot(s.astype(BF16), w_ref[0].astype(BF16)) + b_ref[0]


def _adaln(cond, w_mod, b_mod):
    depth = w_mod.shape[0]
    return pl.pallas_call(
        _mod_kernel,
        grid=(depth, 6 * D // MOD_TN),
        in_specs=[
            pl.BlockSpec((COND_ROWS, D), lambda i, j: (0, 0)),
            pl.BlockSpec((1, D, MOD_TN), lambda i, j: (i, 0, j)),
            pl.BlockSpec((1, 1, MOD_TN), lambda i, j: (i, 0, j)),
        ],
        out_specs=pl.BlockSpec((1, COND_ROWS, MOD_TN), lambda i, j: (i, 0, j)),
        out_shape=jax.ShapeDtypeStruct((depth, COND_ROWS, 6 * D), F32),
        compiler_params=pltpu.CompilerParams(
            dimension_semantics=("arbitrary", "arbitrary"), vmem_limit_bytes=VMEM_LIMIT),
        name="adaln",
    )(cond, w_mod, b_mod.reshape(depth, 1, 6 * D))


_ROW_SPEC = pl.BlockSpec((1, D), lambda i: (0, 0))
_STEP_MOD_SPEC = pl.BlockSpec((1, 6, 1, D), lambda i, *_: (_cond_of_tile(i * STEP_TILES), 0, 0, 0))
_STEP_SPEC = pl.BlockSpec((STEP_ROWS, D), lambda i, *_: (i, 0))
_SUB_ROWS = tuple(slice(s * TM, (s + 1) * TM) for s in range(STEP_TILES))


def _load_weight_bf16(w_hbm, w16_ref, stage_ref, sem):
    n = w16_ref.shape[1] // D

    def slab(c):
        return pltpu.make_async_copy(w_hbm.at[:, pl.ds(c * D, D)], stage_ref.at[c % 2], sem.at[c % 2])

    slab(0).start()
    for c in range(n):
        if c + 1 < n:
            slab(c + 1).start()
        slab(c).wait()
        w16_ref[:, c * D:(c + 1) * D] = stage_ref[c % 2].astype(BF16)


def _weight_scratch(n_slabs):
    n_stage = min(n_slabs, 2)
    return [pltpu.VMEM((D, n_slabs * D), BF16), pltpu.VMEM((n_stage, D, D), F32),
            pltpu.SemaphoreType.DMA((n_stage,))]


_ANY_SPEC = pl.BlockSpec(memory_space=pl.ANY)


def _token_tile(x_refs, rows=slice(None), prompt_steps=PROMPT_TILES):
    if len(x_refs) == 1:
        return x_refs[0][rows]
    return jnp.where(pl.program_id(0) < prompt_steps, x_refs[0][rows], x_refs[1][rows])


def _token_specs(x_parts, tiles_per_step=1):
    rows = tiles_per_step * TM
    if len(x_parts) == 1:
        return [pl.BlockSpec((rows, D), lambda i: (i, 0))]
    prompt_steps = T_PROMPT // rows
    return [pl.BlockSpec((rows, D), lambda i: (jnp.minimum(i, prompt_steps - 1), 0)),
            pl.BlockSpec((rows, D), lambda i: (jnp.maximum(i - prompt_steps, 0), 0))]


def _hgrn_proj_kernel(xp_ref, xs_ref, mod_ref, g_ref, w_hbm, q_ref, v_ref, ff_ref, fb_ref, gg_ref,
                      w_ref, stage_ref, sem):
    @pl.when(pl.program_id(0) == 0)
    def _():
        _load_weight_bf16(w_hbm, w_ref, stage_ref, sem)

    h = [_modulate(_token_tile((xp_ref, xs_ref), rows, PROMPT_STEPS),
                   g_ref[...], mod_ref[0, 0], mod_ref[0, 1]).astype(BF16) for rows in _SUB_ROWS]
    for rows, h_tile in zip(_SUB_ROWS, h):
        q_ref[rows] = (_dot(h_tile, w_ref[:, 0 * D:1 * D]) * (DK ** -0.5)).astype(BF16)
        v_ref[rows] = _dot(h_tile, w_ref[:, 1 * D:2 * D]).astype(BF16)
        ff_ref[rows] = _dot(h_tile, w_ref[:, 2 * D:3 * D])
        fb_ref[rows] = _dot(h_tile, w_ref[:, 3 * D:4 * D])
        gg_ref[rows] = _dot(h_tile, w_ref[:, 4 * D:5 * D]).astype(BF16)


def _hgrn_proj(x_parts, mod, norm_g, w_in):
    bf = jax.ShapeDtypeStruct((T, D), BF16)
    f32 = jax.ShapeDtypeStruct((T, D), F32)
    return pl.pallas_call(
        _hgrn_proj_kernel,
        grid=(N_STEPS,),
        in_specs=_token_specs(x_parts, STEP_TILES) + [_STEP_MOD_SPEC, _ROW_SPEC, _ANY_SPEC],
        out_specs=[_STEP_SPEC] * 5,
        out_shape=[bf, bf, f32, f32, bf],
        scratch_shapes=_weight_scratch(5),
        compiler_params=_cparams(),
        name="hgrn_proj",
    )(*x_parts, mod, norm_g, w_in)


GLA_TILES = 4
GLA_ROWS = GLA_TILES * TM
N_GLA_STEPS = N_TILES // GLA_TILES
GLA_PROMPT_STEPS = PROMPT_TILES // GLA_TILES
STEPS_PER_SAMPLE = SAMPLE_LEN // GLA_ROWS
_GLA_SPEC = pl.BlockSpec((GLA_ROWS, D), lambda i: (i, 0))
_GLA_TILE_ROWS = tuple(slice(s * TM, (s + 1) * TM) for s in range(GLA_TILES))


def _gla_step(block, q_ref, v_ref, f_ref, lbl_ref, s0_ref, st_ref, st_tiles_ref, ma_ref, o_ref,
              *, reverse, finish=None):
    is_prompt = block < GLA_PROMPT_STEPS
    rel = block - GLA_PROMPT_STEPS
    edge = (STEPS_PER_SAMPLE - 1) if reverse else 0
    sample_start = jnp.logical_and(rel >= 0, rel % STEPS_PER_SAMPLE == edge)

    @pl.when(is_prompt)
    def _():
        st_ref[...] = jnp.zeros_like(st_ref)

    @pl.when(sample_start)
    def _():
        for h in range(HEADS):
            st_ref[h] = s0_ref[0, 0, h].T

    logits = lbl_ref[...]
    e = jnp.exp(logits - jnp.max(logits, axis=0, keepdims=True))
    lb = e[0:1] / jnp.sum(e, axis=0, keepdims=True)

    ref_idx = (CHUNK - 1 - CHUNK // 2) if reverse else CHUNK // 2
    last_idx = 0 if reverse else CHUNK - 1
    shift = CHUNK.bit_length() - 1

    row = lax.broadcasted_iota(jnp.int32, (TM, TM), 0)
    col = lax.broadcasted_iota(jnp.int32, (TM, TM), 1)
    same = (row >> shift) == (col >> shift)
    ref_row = ((row >> shift) << shift) + ref_idx
    keep = jnp.logical_and(same, (row <= col) if reverse else (row >= col))

    @pl.when(pl.program_id(0) == 0)
    def _():
        at_ref = jnp.logical_and(same, (ref_row <= col) if reverse else (ref_row >= col))
        ma_ref[...] = (keep.astype(F32) - at_ref.astype(F32)).astype(BF16)

    ma = ma_ref[...]

    rr = lax.broadcasted_iota(jnp.int32, (2 * CHUNKS_PER_TILE, TM), 0)
    cc = lax.broadcasted_iota(jnp.int32, (2 * CHUNKS_PER_TILE, TM), 1)
    chunk = rr & (CHUNKS_PER_TILE - 1)
    target = chunk * CHUNK + jnp.where(rr < CHUNKS_PER_TILE, ref_idx, last_idx)
    covered = (cc >= target) if reverse else (cc <= target)
    rm = jnp.logical_and((cc >> shift) == chunk, covered).astype(BF16)
    ma_rm = jnp.concatenate([ma, rm, jnp.zeros_like(rm)], axis=0)

    chunk_rows = [slice(c * CHUNK, (c + 1) * CHUNK) for c in range(CHUNKS_PER_TILE)]
    order = range(CHUNKS_PER_TILE - 1, -1, -1) if reverse else range(CHUNKS_PER_TILE)
    pair_w = 2 * DK
    n_pairs = HEADS // 2
    tile_order = _GLA_TILE_ROWS[::-1] if reverse else _GLA_TILE_ROWS

    def per_chunk_scale(x, scale):
        return jnp.concatenate([x[rs] * scale[c:c + 1] for c, rs in enumerate(chunk_rows)], axis=0)

    def pair_cols(p):
        return slice(p * pair_w, (p + 1) * pair_w)

    def stage_decay(k, p):
        lbp = lb[:, pair_cols(p)]
        f = lbp + (1.0 - lbp) * jax.nn.sigmoid(f_ref[tile_order[k], pair_cols(p)])
        lf_hi, lf_lo = _split_bf16(jnp.log(f))
        both = _dot(ma_rm, lf_hi) + _dot(ma_rm, lf_lo)
        z, marks = both[:TM], both[TM:TM + 2 * CHUNKS_PER_TILE]
        return 1.0 - f, z, marks

    def stage_scores(k, p, kk, z, marks):
        ref, last = marks[:CHUNKS_PER_TILE], marks[CHUNKS_PER_TILE:]
        qa32 = q_ref[tile_order[k], pair_cols(p)].astype(F32) * jnp.exp(z)
        kb32 = kk * jnp.exp(-z)
        qa = qa32.astype(BF16)
        kb = kb32.astype(BF16)
        qe = per_chunk_scale(qa32, jnp.exp(ref)).astype(BF16)
        kd = per_chunk_scale(kb32, jnp.exp(last - ref)).astype(BF16)
        v = v_ref[tile_order[k], pair_cols(p)]
        scores, incr = [], []
        for hh in range(2):
            ls = slice(hh * DK, (hh + 1) * DK)
            scores.append(_dot_nt(qa[:, ls], kb[:, ls]))
            kd_h = kd[:, ls]
            kd_blocks = jnp.concatenate(
                [jnp.concatenate([part for part in (jnp.zeros((c * CHUNK, DK), BF16), kd_h[rs],
                                                    jnp.zeros((TM - (c + 1) * CHUNK, DK), BF16))
                                  if part.shape[0]], axis=0)
                 for c, rs in enumerate(chunk_rows)], axis=1)
            incr.append(_dot_tn(v[:, ls], kd_blocks))
        return scores, incr, qe, jnp.exp(last), v

    def stage_output(k, p, scores, incr, qe, decay, v):
        base = tile_order[k].start
        for hh in range(2):
            h = 2 * p + hh
            ls = slice(hh * DK, (hh + 1) * DK)
            hs = slice(h * DK, (h + 1) * DK)
            o_intra = _dot(jnp.where(keep, scores[hh], 0.0).astype(BF16), v[:, ls])
            st = st_ref[h]
            if k > 0:
                st = jnp.where(is_prompt, 0.0, st)
            for c in order:
                rs = chunk_rows[c]
                o_ref[base + rs.start:base + rs.stop, hs] = (
                    o_intra[rs] + _dot_nt(qe[rs, ls], st.astype(BF16)))
                st = st * decay[c:c + 1, ls] + incr[hh][:, c * DK:(c + 1) * DK]
            st_ref[h] = st
            if k < GLA_TILES - 1:
                st_tiles_ref[k, h] = st
        if finish is not None:
            return (tile_order[k], p)

    def stage_finish(k, p, rows, pair):
        finish(rows, pair)

    units = [(k, p) for k in range(GLA_TILES) for p in range(n_pairs)]
    stages = [stage_decay, stage_scores, stage_output] + ([stage_finish] if finish is not None else [])
    carried = [dict() for _ in stages]
    for step in range(len(units) + len(stages) - 1):
        for depth, stage in (enumerate(stages) if reverse else reversed(list(enumerate(stages)))):
            u = step - depth
            if 0 <= u < len(units):
                out = stage(*units[u], *(carried[depth].pop(u) if depth else ()))
                if depth + 1 < len(stages):
                    carried[depth + 1][u] = out


def _store_prompt_states(block, st_ref, st_tiles_ref, snew_ref, *, reverse, sfwd_ref=None):
    @pl.when(block < GLA_PROMPT_STEPS)
    def _():
        for k in range(GLA_TILES):
            tile = GLA_TILES - 1 - k if reverse else k
            for h in range(HEADS):
                state = (st_tiles_ref[k, h] if k < GLA_TILES - 1 else st_ref[h]).T
                if sfwd_ref is None:
                    snew_ref[tile, h] = state
                else:
                    snew_ref[tile, 1, h] = state
        if sfwd_ref is not None:
            snew_ref[:, 0] = sfwd_ref[...]


def _gla_fwd_kernel(q_ref, v_ref, f_ref, lbl_ref, s0_ref, o_ref, snew_ref, st_ref, st_tiles_ref, ma_ref):
    block = pl.program_id(0)
    _gla_step(block, q_ref, v_ref, f_ref, lbl_ref, s0_ref, st_ref, st_tiles_ref, ma_ref, o_ref,
              reverse=False)
    _store_prompt_states(block, st_ref, st_tiles_ref, snew_ref, reverse=False)


def _gla_bwd_kernel(q_ref, v_ref, f_ref, lbl_ref, s0_ref, of_ref, gg_ref, ng_ref, sfwd_ref,
                    a_ref, snew_ref, st_ref, st_tiles_ref, ma_ref, ob_ref):
    block = N_GLA_STEPS - 1 - pl.program_id(0)

    def finish(rows, p):
        for h in (2 * p, 2 * p + 1):
            hs = slice(h * DV, (h + 1) * DV)
            o = of_ref[rows, hs] + ob_ref[rows, hs]
            a_ref[rows, hs] = (_rmsnorm(o, ng_ref[...]) * _silu(gg_ref[rows, hs].astype(F32))).astype(BF16)

    _gla_step(block, q_ref, v_ref, f_ref, lbl_ref, s0_ref, st_ref, st_tiles_ref, ma_ref, ob_ref,
              reverse=True, finish=finish)
    _store_prompt_states(block, st_ref, st_tiles_ref, snew_ref, reverse=True, sfwd_ref=sfwd_ref)


def _state_specs(direction, block_of_step):
    sample = lambda i: jnp.clip((block_of_step(i) - GLA_PROMPT_STEPS) // STEPS_PER_SAMPLE, 0, N_SAMPLE_SEQ - 1)
    s0 = pl.BlockSpec((1, 1, HEADS, DK, DV), lambda i: (sample(i), direction, 0, 0, 0))
    snew = pl.BlockSpec((GLA_TILES, HEADS, DK, DV),
                        lambda i: (jnp.minimum(block_of_step(i), GLA_PROMPT_STEPS - 1), 0, 0, 0))
    return s0, snew


_GLA_STATE_SCRATCH = [pltpu.VMEM((HEADS, DV, DK), F32), pltpu.VMEM((GLA_TILES - 1, HEADS, DV, DK), F32),
                      pltpu.VMEM((TM, TM), BF16)]


def _gla_fwd(q, v, ff, lb_logits, s0):
    s0_spec, snew_spec = _state_specs(0, lambda i: i)
    return pl.pallas_call(
        _gla_fwd_kernel,
        grid=(N_GLA_STEPS,),
        in_specs=[_GLA_SPEC, _GLA_SPEC, _GLA_SPEC,
                  pl.BlockSpec(lb_logits.shape, lambda i: (0, 0)), s0_spec],
        out_specs=[_GLA_SPEC, snew_spec],
        out_shape=[jax.ShapeDtypeStruct((T, D), F32),
                   jax.ShapeDtypeStruct((N_PROMPT_SEQ, HEADS, DK, DV), F32)],
        scratch_shapes=_GLA_STATE_SCRATCH,
        compiler_params=_cparams(),
        name="gla_fwd",
    )(q, v, ff, lb_logits, s0)


def _gla_bwd(q, v, fb, lb_logits, s0, o_f, gg, head_norm_g, s_fwd):
    rev = lambda i: N_GLA_STEPS - 1 - i
    block = pl.BlockSpec((GLA_ROWS, D), lambda i: (rev(i), 0))
    s0_spec, sfwd_spec = _state_specs(1, rev)
    prompt_block = lambda i: jnp.minimum(rev(i), GLA_PROMPT_STEPS - 1)
    return pl.pallas_call(
        _gla_bwd_kernel,
        grid=(N_GLA_STEPS,),
        in_specs=[block, block, block, pl.BlockSpec(lb_logits.shape, lambda i: (0, 0)), s0_spec,
                  block, block, pl.BlockSpec((1, DV), lambda i: (0, 0)), sfwd_spec],
        out_specs=[block, pl.BlockSpec((GLA_TILES, 2, HEADS, DK, DV), lambda i: (prompt_block(i), 0, 0, 0, 0))],
        out_shape=[jax.ShapeDtypeStruct((T, D), BF16),
                   jax.ShapeDtypeStruct((N_PROMPT_SEQ, 2, HEADS, DK, DV), F32)],
        scratch_shapes=_GLA_STATE_SCRATCH + [pltpu.VMEM((GLA_ROWS, D), F32)],
        compiler_params=_cparams(),
        name="gla_bwd",
    )(q, v, fb, lb_logits, s0, o_f, gg, head_norm_g, s_fwd)


def _conv_mixer_kernel(x_ref, mod_ref, g_ref, w_hbm, cw_ref, a_ref, w_ref, stage_ref, sem):
    @pl.when(pl.program_id(0) == 0)
    def _():
        _load_weight_bf16(w_hbm, w_ref, stage_ref, sem)

    seg = jnp.where(pl.program_id(0) < PROMPT_TILES // CONV_TILES, PROMPT_LEN, GRID_W)
    pos = lax.broadcasted_iota(jnp.int32, (TM, 1), 0) & (seg - 1)

    def project(rows):
        h = _modulate(x_ref[rows], g_ref[...], mod_ref[0, 0], mod_ref[0, 1]).astype(BF16)
        return [_dot(h, w_ref[:, k * D:(k + 1) * D]) for k in range(3)]

    def convolve(rows, bg, cg, x_in):
        u = cg * x_in
        prev = jnp.where(pos == 0, 0.0, pltpu.roll(u, 1, axis=0))
        nxt = jnp.where(pos == seg - 1, 0.0, pltpu.roll(u, TM - 1, axis=0))
        conv = cw_ref[0:1] * prev + cw_ref[1:2] * u + cw_ref[2:3] * nxt
        a_ref[rows] = (bg * conv).astype(BF16)

    tiles = [slice(s * TM, (s + 1) * TM) for s in range(CONV_TILES)]
    projected = [project(rows) for rows in tiles]
    for rows, parts in zip(tiles, projected):
        convolve(rows, *parts)


def _conv_mixer(x, mod, norm_g, w_in, conv_w):
    block = pl.BlockSpec((CONV_TILES * TM, D), lambda i: (i, 0))
    return pl.pallas_call(
        _conv_mixer_kernel,
        grid=(N_TILES // CONV_TILES,),
        in_specs=[block, pl.BlockSpec((1, 6, 1, D), lambda i: (_cond_of_tile(i * CONV_TILES), 0, 0, 0)),
                  _ROW_SPEC, _ANY_SPEC, pl.BlockSpec((3, D), lambda i: (0, 0))],
        out_specs=block,
        out_shape=jax.ShapeDtypeStruct((T, D), BF16),
        scratch_shapes=_weight_scratch(3),
        compiler_params=_cparams(),
        name="conv_mixer",
    )(x, mod, norm_g, w_in, conv_w)


ROUTE_ROWS = 32


def _first_member(vals, target):
    idx = jnp.full_like(target, float(EPG - 1))
    for j in range(EPG - 2, -1, -1):
        idx = jnp.where(vals[j] == target, float(j), idx)
    return idx


def _router_logits(h2, rwt_ref):
    h_hi, h_lo = _split_bf16(h2)
    w_hi, w_lo = _split_bf16(rwt_ref[...])
    with_hi = _dot_nt(jnp.concatenate([w_hi, w_lo], axis=0), h_hi)
    return with_hi[:N_EXPERTS] + (_dot_nt(w_hi, h_lo) + with_hi[N_EXPERTS:])


def _select_experts(logits, rb_ref):
    scores = jax.nn.sigmoid(logits)
    sel = scores + rb_ref[...]
    member = [sel[j * N_GROUPS:(j + 1) * N_GROUPS] for j in range(EPG)]
    m1 = functools.reduce(jnp.maximum, member)
    i1 = _first_member(member, m1)
    rest = [jnp.where(i1 == float(j), -jnp.inf, member[j]) for j in range(EPG)]
    m2 = functools.reduce(jnp.maximum, rest)
    i2 = _first_member(rest, m2)
    group_score = m1 + m2

    best, grp, first, second = group_score[0:1], jnp.zeros((1, TM), F32), i1[0:1], i2[0:1]
    for g in range(1, N_GROUPS):
        upd = group_score[g:g + 1] > best
        best = jnp.where(upd, group_score[g:g + 1], best)
        grp = jnp.where(upd, float(g), grp)
        first = jnp.where(upd, i1[g:g + 1], first)
        second = jnp.where(upd, i2[g:g + 1], second)
    a = jnp.minimum(first, second)
    b = jnp.maximum(first, second)
    row = lax.broadcasted_iota(jnp.int32, (N_EXPERTS, TM), 0).astype(F32)
    s_lo = jnp.sum(jnp.where(row == a * N_GROUPS + grp, scores, 0.0), axis=0, keepdims=True)
    s_hi = jnp.sum(jnp.where(row == b * N_GROUPS + grp, scores, 0.0), axis=0, keepdims=True)
    tot = s_lo + s_hi
    pair = jnp.where(a == 0.0, jnp.where(b == 1.0, 0.0, jnp.where(b == 2.0, 3.0, 4.0)),
                     jnp.where(a == 1.0, jnp.where(b == 2.0, 5.0, 1.0), 2.0))
    return grp * N_PAIRS + pair, s_lo / tot, s_hi / tot


def _post_mixer_kernel(*refs, n_x):
    a_ref, x_refs = refs[0], refs[1:1 + n_x]
    (mod_ref, g2_ref, wo_hbm, rwt_ref, rb_ref,
     x1_ref, gates_ref, bucket_ref, rank_ref, cnt_ref, carry_ref, earlier_ref, wo_ref, stage_ref, sem) = refs[1 + n_x:]

    @pl.when(pl.program_id(0) == 0)
    def _():
        _load_weight_bf16(wo_hbm, wo_ref, stage_ref, sem)
        carry_ref[...] = jnp.zeros_like(carry_ref)
        s_idx = lax.broadcasted_iota(jnp.int32, (TM, TM), 0)
        t_idx = lax.broadcasted_iota(jnp.int32, (TM, TM), 1)
        earlier_ref[...] = (s_idx < t_idx).astype(BF16)

    def stage_mix(s):
        rows = slice(s * TM, (s + 1) * TM)
        x1 = (_token_tile(x_refs, rows, PROMPT_TILES // ROUTE_TILES)
              + mod_ref[0, 2] * _dot(a_ref[rows], wo_ref[...]))
        x1_ref[rows] = x1
        return _modulate(x1, g2_ref[...], mod_ref[0, 3], mod_ref[0, 4])

    def bucket_onehot(bucket):
        return lax.broadcasted_iota(jnp.int32, (ROUTE_ROWS, TM), 0).astype(F32) == bucket

    def stage_rank(s, onehot, before, bucket, g_lo, g_hi):
        oh = onehot.astype(F32)
        carry = carry_ref[...]
        rank = jnp.sum(oh * (before + carry[:, 0:1]), axis=0, keepdims=True)
        carry_ref[...] = carry + jnp.sum(oh, axis=1, keepdims=True)
        bucket_ref[s] = bucket.astype(jnp.int32)
        rank_ref[s] = rank.astype(jnp.int32)
        grow = lax.broadcasted_iota(jnp.int32, (LANES, TM), 0)
        gates_ref[s * TM:(s + 1) * TM] = jnp.where(grow == 0, g_lo, jnp.where(grow == 1, g_hi, 0.0)).T

    tiles = range(ROUTE_TILES)
    h2 = [stage_mix(s) for s in tiles]
    logits = [_router_logits(h2[s], rwt_ref) for s in tiles]
    picks = [_select_experts(logits[s], rb_ref) for s in tiles]
    onehots = [bucket_onehot(picks[s][0]) for s in tiles]
    before = _dot(jnp.concatenate([oh.astype(BF16) for oh in onehots], axis=0), earlier_ref[...])
    for s in tiles:
        stage_rank(s, onehots[s], before[s * ROUTE_ROWS:(s + 1) * ROUTE_ROWS], *picks[s])
    cnt_ref[...] = carry_ref[...]


def _post_mixer(a, x_parts, mod, norm_g2, w_out, router_wt, router_bt):
    rows = ROUTE_TILES * TM
    idx_spec = pl.BlockSpec((ROUTE_TILES, 1, TM), lambda i: (i, 0, 0))
    return pl.pallas_call(
        functools.partial(_post_mixer_kernel, n_x=len(x_parts)),
        grid=(N_TILES // ROUTE_TILES,),
        in_specs=[pl.BlockSpec((rows, D), lambda i: (i, 0))] + _token_specs(x_parts, ROUTE_TILES) + [
            pl.BlockSpec((1, 6, 1, D), lambda i: (_cond_of_tile(i * ROUTE_TILES), 0, 0, 0)), _ROW_SPEC,
            _ANY_SPEC,
            pl.BlockSpec((N_EXPERTS, D), lambda i: (0, 0)),
            pl.BlockSpec((N_EXPERTS, 1), lambda i: (0, 0))],
        out_specs=[pl.BlockSpec((rows, D), lambda i: (i, 0)), pl.BlockSpec((rows, LANES), lambda i: (i, 0)),
                   idx_spec, idx_spec, pl.BlockSpec((ROUTE_ROWS, LANES), lambda i: (0, 0))],
        out_shape=[jax.ShapeDtypeStruct((T, D), F32), jax.ShapeDtypeStruct((T, LANES), F32),
                   jax.ShapeDtypeStruct((N_TILES, 1, TM), jnp.int32),
                   jax.ShapeDtypeStruct((N_TILES, 1, TM), jnp.int32),
                   jax.ShapeDtypeStruct((ROUTE_ROWS, LANES), F32)],
        scratch_shapes=[pltpu.VMEM((ROUTE_ROWS, LANES), F32), pltpu.VMEM((TM, TM), BF16)] + _weight_scratch(1),
        compiler_params=_cparams(),
        name="post_mixer",
    )(a, *x_parts, mod, norm_g2, w_out, router_wt, router_bt)


_MOVE_SPEC = pl.BlockSpec((MOVE_ROWS, D), lambda i, *_: (i, 0))
_MOVE_MOD_SPEC = pl.BlockSpec((1, 6, 1, D), lambda i, *_: (_cond_of_tile(i * MOVE_TILES), 0, 0, 0))


def _smem_step_spec(step_of_step):
    return pl.BlockSpec((MOVE_TILES, 1, TM), lambda i, *_: (step_of_step(i), 0, 0), memory_space=pltpu.SMEM)


def _start_step_rows(make_copy):
    for tile in range(MOVE_TILES):
        for r in range(TM):
            make_copy(tile, r).start(priority=r % 2)


def _dispatch_kernel(zero_from_ref, slot_ref, x1_ref, mod_ref, g2_ref, gates_ref,
                     xs_ref, buf_ref, zero_ref, sems, zero_sem):
    i = pl.program_id(0)
    cur = i % 2

    @pl.when(i == 0)
    def _():
        zero_ref[...] = jnp.zeros_like(zero_ref)

        def for_each_chunk(act):
            def tile(j, carry):
                def chunk(c, inner):
                    row = pl.multiple_of(j * TM_E + c * ZERO_ROWS, ZERO_ROWS)
                    act(pltpu.make_async_copy(zero_ref, xs_ref.at[pl.ds(row, ZERO_ROWS)], zero_sem))
                    return inner
                return lax.fori_loop(zero_from_ref[j], TM_E // ZERO_ROWS, chunk, carry)
            lax.fori_loop(0, N_TILES_E, tile, 0)

        for_each_chunk(lambda copy: copy.start())
        for_each_chunk(lambda copy: copy.wait())

    def wait_step(b):
        pltpu.make_async_copy(buf_ref.at[b], xs_ref.at[pl.ds(0, MOVE_ROWS)], sems.at[b]).wait()

    h2 = _modulate(x1_ref[...], g2_ref[...], mod_ref[0, 3], mod_ref[0, 4])
    gates = gates_ref[...]

    def send(b):
        buf_ref[b, :, :D] = h2
        buf_ref[b, :, D:] = gates
        _start_step_rows(lambda tile, r: pltpu.make_async_copy(
            buf_ref.at[b, pl.ds(tile * TM + r, 1)], xs_ref.at[pl.ds(slot_ref[tile, 0, r], 1)], sems.at[b]))

        @pl.when(i > 0)
        def _():
            wait_step(1 - b)

        @pl.when(i == N_MOVE_STEPS - 1)
        def _():
            wait_step(b)

    for b in range(2):
        pl.when(cur == b)(functools.partial(send, b))


def _dispatch(zero_from, slot, x1, mod, norm_g2, gates):
    return pl.pallas_call(
        _dispatch_kernel,
        grid_spec=pltpu.PrefetchScalarGridSpec(
            num_scalar_prefetch=1,
            grid=(N_MOVE_STEPS,),
            in_specs=[_smem_step_spec(lambda i: i), _MOVE_SPEC, _MOVE_MOD_SPEC,
                      pl.BlockSpec((1, D), lambda i, *_: (0, 0)),
                      pl.BlockSpec((MOVE_ROWS, LANES), lambda i, *_: (i, 0))],
            out_specs=pl.BlockSpec(memory_space=pl.ANY),
            scratch_shapes=[pltpu.VMEM((2, MOVE_ROWS, XW), F32), pltpu.VMEM((ZERO_ROWS, XW), F32),
                            pltpu.SemaphoreType.DMA((2,)), pltpu.SemaphoreType.DMA(())],
        ),
        out_shape=jax.ShapeDtypeStruct((S_ROWS, XW), F32),
        compiler_params=_cparams(),
        name="moe_dispatch",
    )(zero_from, slot, x1, mod, norm_g2, gates)


_W_SHAPES = ((D, D_FF), (D, D_FF), (D_FF, D))


def _expert_kernel(ea_ref, eb_ref, next_a_ref, next_b_ref, a_higher_ref, nrows_ref, xblock_ref,
                   x_ref, wg_hbm, wu_hbm, wd_hbm, y_ref, *scratch, layer):
    stage, w16, sems = scratch[0:3], scratch[3:6], scratch[6]
    j = pl.program_id(0)
    n = nrows_ref[j]
    prev = jnp.maximum(j - 1, 0)

    def fetch(slot, expert):
        return [pltpu.make_async_copy(w.at[layer, expert], s.at[slot], sems.at[slot])
                for w, s in zip((wg_hbm, wu_hbm, wd_hbm), stage)]

    for slot, e_ref, next_ref in ((0, ea_ref, next_a_ref), (1, eb_ref, next_b_ref)):
        @pl.when(j == 0)
        def _(slot=slot, e_ref=e_ref):
            for copy in fetch(slot, e_ref[0]):
                copy.start()

        @pl.when(jnp.logical_or(j == 0, e_ref[j] != e_ref[prev]))
        def _(slot=slot, e_ref=e_ref, next_ref=next_ref):
            for copy in fetch(slot, e_ref[j]):
                copy.wait()
            for s, d in zip(stage, w16):
                d[slot] = s[slot].astype(BF16)

            @pl.when(next_ref[j] >= 0)
            def _():
                for copy in fetch(slot, next_ref[j]):
                    copy.start(priority=1)

    def run(rows):
        valid = lax.broadcasted_iota(jnp.int32, (rows, 1), 0) < n
        xe = jnp.where(valid, x_ref[:rows], 0.0)
        x = xe[:, :D].astype(BF16)
        g_lo, g_hi = xe[:, D:D + 1], xe[:, D + 1:D + 2]
        a_higher = a_higher_ref[j] != 0

        def ffn(slot, gate):
            wg_ref, wu_ref, wd_ref = w16
            act = _silu(_dot(x, wg_ref[slot])) * _dot(x, wu_ref[slot]) * gate
            return _dot(act.astype(BF16), wd_ref[slot])

        y_ref[:rows] = (ffn(0, jnp.where(a_higher, g_hi, g_lo))
                        + ffn(1, jnp.where(a_higher, g_lo, g_hi)))
        if rows < TM_E:
            y_ref[rows:] = jnp.zeros((TM_E - rows, D), F32)

    half = TM_E // 2
    pl.when(n > half)(functools.partial(run, TM_E))
    pl.when(jnp.logical_and(n > 0, n <= half))(functools.partial(run, half))

    @pl.when(n == 0)
    def _():
        y_ref[...] = jnp.zeros_like(y_ref)


def _experts(tile_ea, tile_eb, tile_next_a, tile_next_b, tile_a_higher, tile_rows, tile_xblock, xs,
             w_gate, w_up, w_down, layer):
    any_spec = pl.BlockSpec(memory_space=pl.ANY)
    return pl.pallas_call(
        functools.partial(_expert_kernel, layer=layer),
        grid_spec=pltpu.PrefetchScalarGridSpec(
            num_scalar_prefetch=7,
            grid=(N_TILES_E,),
            in_specs=[pl.BlockSpec((TM_E, XW), lambda j, *plan: (plan[-1][j], 0)), any_spec, any_spec, any_spec],
            out_specs=pl.BlockSpec((TM_E, D), lambda j, *_: (j, 0)),
            scratch_shapes=[pltpu.VMEM((2,) + s, F32) for s in _W_SHAPES]
                           + [pltpu.VMEM((2,) + s, BF16) for s in _W_SHAPES]
                           + [pltpu.SemaphoreType.DMA((2,))],
        ),
        out_shape=jax.ShapeDtypeStruct((S_ROWS, D), F32),
        compiler_params=_cparams(),
        name="moe_experts",
    )(tile_ea, tile_eb, tile_next_a, tile_next_b, tile_a_higher, tile_rows, tile_xblock, xs,
      w_gate, w_up, w_down)


def _combine_kernel(slot_ref, next_slot_ref, x1_ref, mod_ref, gf_ref, y_ref, *rest, final):
    out_refs, (buf_ref, sems) = rest[:-2], rest[-2:]
    i = pl.program_id(0)
    cur = i % 2

    def gather(slots, b):
        _start_step_rows(lambda tile, r: pltpu.make_async_copy(
            y_ref.at[pl.ds(slots[tile, 0, r], 1)], buf_ref.at[b, pl.ds(tile * TM + r, 1)], sems.at[b]))

    @pl.when(i == 0)
    def _():
        gather(slot_ref, 0)

    def combine(b):
        @pl.when(i + 1 < N_MOVE_STEPS)
        def _():
            gather(next_slot_ref, 1 - b)

        pltpu.make_async_copy(y_ref.at[pl.ds(0, MOVE_ROWS)], buf_ref.at[b], sems.at[b]).wait()
        x2 = x1_ref[...] + mod_ref[0, 5] * buf_ref[b]
        if not final:
            out_refs[0][...] = x2
            return
        y = _rmsnorm(x2, gf_ref[...])
        yp_ref, ys_ref = out_refs

        @pl.when(i < T_PROMPT // MOVE_ROWS)
        def _():
            yp_ref[...] = y

        @pl.when(i >= T_PROMPT // MOVE_ROWS)
        def _():
            ys_ref[...] = y

    for b in range(2):
        pl.when(cur == b)(functools.partial(combine, b))


def _combine(slot, x1, mod, final_g, y_sorted, *, final):
    if final:
        out_specs = _token_specs((None, None), MOVE_TILES)
        out_shape = [jax.ShapeDtypeStruct((T_PROMPT, D), F32), jax.ShapeDtypeStruct((T_SAMPLE, D), F32)]
    else:
        out_specs = [_MOVE_SPEC]
        out_shape = [jax.ShapeDtypeStruct((T, D), F32)]
    return pl.pallas_call(
        functools.partial(_combine_kernel, final=final),
        grid=(N_MOVE_STEPS,),
        in_specs=[_smem_step_spec(lambda i: i), _smem_step_spec(lambda i: jnp.minimum(i + 1, N_MOVE_STEPS - 1)),
                  _MOVE_SPEC, _MOVE_MOD_SPEC, _ROW_SPEC, pl.BlockSpec(memory_space=pl.ANY)],
        out_specs=out_specs,
        scratch_shapes=[pltpu.VMEM((2, MOVE_ROWS, D), F32), pltpu.SemaphoreType.DMA((2,))],
        out_shape=out_shape,
        compiler_params=_cparams(),
        name="moe_combine_final" if final else "moe_combine",
    )(slot, slot, x1, mod, final_g, y_sorted)


def _plan_kernel(cnt_ref, bucket_ref, rank_ref, slot_ref,
                 ea_ref, eb_ref, next_a_ref, next_b_ref, a_higher_ref, rows_ref, xblock_ref, zero_from_ref):
    slot = rank_ref[...]
    bucket = bucket_ref[...]
    tile0 = jnp.int32(0)
    last = [jnp.int32(0)] * 3
    for b in range(N_BUCKETS):
        n = cnt_ref[b]
        tiles = (n + (TM_E - 1)) // TM_E
        pair = b % N_PAIRS
        per_tile = ((b // N_PAIRS) * EPG + _SLOT_A[pair], (b // N_PAIRS) * EPG + _SLOT_B[pair],
                    _SLOT_A_IS_HIGHER[pair])

        def fill(k, carry, n=n, tile0=tile0, per_tile=per_tile):
            j = tile0 + k
            live = jnp.minimum(n - k * TM_E, TM_E)
            rows_ref[j] = live
            xblock_ref[j] = j
            zero_from_ref[j] = live // ZERO_ROWS
            for ref, value in zip((ea_ref, eb_ref, a_higher_ref), per_tile):
                ref[j] = jnp.int32(value)
            return carry

        lax.fori_loop(0, tiles, fill, 0)
        slot = slot + jnp.where(bucket == b, tile0 * TM_E, 0)
        last = [jnp.where(tiles > 0, value, old) for value, old in zip(per_tile, last)]
        tile0 = tile0 + tiles

    def unused(j, carry):
        rows_ref[j] = jnp.int32(0)
        xblock_ref[j] = jnp.maximum(tile0 - 1, 0)
        zero_from_ref[j] = jnp.int32(0)
        for ref, value in zip((ea_ref, eb_ref, a_higher_ref), last):
            ref[j] = value
        return carry

    lax.fori_loop(tile0, N_TILES_E, unused, 0)

    for e_ref, next_ref in ((ea_ref, next_a_ref), (eb_ref, next_b_ref)):
        next_ref[N_TILES_E - 1] = jnp.int32(-1)

        def backward(k, carry, e_ref=e_ref, next_ref=next_ref):
            j = N_TILES_E - 2 - k
            next_ref[j] = jnp.where(e_ref[j] == e_ref[j + 1], next_ref[j + 1], e_ref[j + 1])
            return carry

        lax.fori_loop(0, N_TILES_E - 1, backward, 0)
    slot_ref[...] = slot


def _plan(counts, bucket, rank):
    whole = pl.BlockSpec(bucket.shape, lambda i, *_: (0, 0, 0))
    smem = pl.BlockSpec(memory_space=pltpu.SMEM)
    per_tile = jax.ShapeDtypeStruct((N_TILES_E,), jnp.int32)
    slot, *tile_plan = pl.pallas_call(
        _plan_kernel,
        grid_spec=pltpu.PrefetchScalarGridSpec(
            num_scalar_prefetch=1, grid=(1,), in_specs=[whole, whole], out_specs=[whole] + [smem] * 8),
        out_shape=[jax.ShapeDtypeStruct(bucket.shape, jnp.int32)] + [per_tile] * 8,
        compiler_params=_cparams(),
        name="moe_plan",
    )(counts[:, 0].astype(jnp.int32), bucket, rank)
    return slot, tile_plan


def _moe(a, x_parts, mod, norm_g2, w_out, router_wt, router_bt, w_gate, w_up, w_down, layer, final_g,
         *, final):
    x1, gates, bucket, rank, counts = _post_mixer(a, x_parts, mod, norm_g2, w_out, router_wt, router_bt)
    slot, (*expert_plan, zero_from) = _plan(counts, bucket, rank)
    xs = _dispatch(zero_from, slot, x1, mod, norm_g2, gates)
    ys = _experts(*expert_plan, xs, w_gate, w_up, w_down, layer)
    return _combine(slot, x1, mod, final_g, ys, final=final)


def kernel(x_prompt, x_sample, state_hgrn, c, c_ctx, w_mod, b_mod, norm_mix_g, norm_ffn_g, norm_final_g,
           hgrn_w_in, hgrn_lb_logits, hgrn_norm_g, hgrn_w_out, conv_w_in, conv_w, conv_w_out,
           router_w, router_b, moe_w_gate, moe_w_up, moe_w_down):
    x_parts = (x_prompt.reshape(T_PROMPT, D), x_sample.reshape(T_SAMPLE, D))
    cond = jnp.concatenate([c_ctx[None], c, jnp.zeros((COND_ROWS - N_COND, D), F32)], axis=0)
    mods = _adaln(cond, w_mod, b_mod)[:, :N_COND].reshape(2, N_COND, 6, 1, D)
    member_major = lambda w: w.reshape(N_GROUPS, EPG, -1).transpose(1, 0, 2).reshape(N_EXPERTS, -1)
    rwt = member_major(router_w.T)
    rbt = member_major(router_b.reshape(N_EXPERTS, 1))
    final_g = norm_final_g.reshape(1, D)

    q, v, ff, fb, gg = _hgrn_proj(x_parts, mods[0], norm_mix_g[0:1], hgrn_w_in[0])
    s0 = state_hgrn[:, 0]
    o_f, s_f = _gla_fwd(q, v, ff, hgrn_lb_logits, s0)
    a, new_state = _gla_bwd(q, v, fb, hgrn_lb_logits, s0, o_f, gg, hgrn_norm_g[0:1], s_f)
    (x,) = _moe(a, x_parts, mods[0], norm_ffn_g[0:1], hgrn_w_out[0], rwt, rbt,
                moe_w_gate, moe_w_up, moe_w_down, 0, final_g, final=False)

    a = _conv_mixer(x, mods[1], norm_mix_g[1:2], conv_w_in[0], conv_w[0])
    y_p, y_s = _moe(a, (x,), mods[1], norm_ffn_g[1:2], conv_w_out[0], rwt, rbt,
                    moe_w_gate, moe_w_up, moe_w_down, 1, final_g, final=True)

    return (y_p.reshape(N_PROMPT_SEQ, PROMPT_LEN, D), y_s.reshape(N_SAMPLE_SEQ, SAMPLE_LEN, D),
            new_state[:, None])
```

```python
import functools

import jax
import jax.numpy as jnp
from jax import lax
from jax.experimental import pallas as pl
from jax.experimental.pallas import tpu as pltpu

F32 = jnp.float32
BF16 = jnp.bfloat16

D = 1024
N_PROMPT_SEQ = 16
PROMPT_LEN = 256
N_SAMPLE_SEQ = 2
SAMPLE_LEN = 4096
GRID_W = 64
T_PROMPT = N_PROMPT_SEQ * PROMPT_LEN
T_SAMPLE = N_SAMPLE_SEQ * SAMPLE_LEN
T = T_PROMPT + T_SAMPLE
N_COND = 1 + N_SAMPLE_SEQ
COND_ROWS = 8

HEADS = 8
DK = 128
DV = 128
CHUNK = 64
N_EXPERTS = 16
N_GROUPS = 4
EPG = 4
N_PAIRS = 6
N_BUCKETS = N_GROUPS * N_PAIRS
D_FF = 512
EPS = 1e-6

TM = 256
N_TILES = T // TM
PROMPT_TILES = T_PROMPT // TM
TILES_PER_SAMPLE = SAMPLE_LEN // TM
CHUNKS_PER_TILE = TM // CHUNK
STEP_TILES = 2
STEP_ROWS = STEP_TILES * TM
N_STEPS = N_TILES // STEP_TILES
PROMPT_STEPS = PROMPT_TILES // STEP_TILES
MOVE_TILES = 2
MOVE_ROWS = MOVE_TILES * TM
N_MOVE_STEPS = N_TILES // MOVE_TILES
ROUTE_TILES = 4
CONV_TILES = 4
TM_E = 288
N_TILES_E = T // TM_E + N_BUCKETS
S_ROWS = N_TILES_E * TM_E
ZERO_ROWS = 32
XW = D + 128
LANES = 128
MOD_TN = 3072

V7X_VMEM_BYTES = 64 * 1024 * 1024
VMEM_LIMIT = V7X_VMEM_BYTES * 7 // 8

_SLOT_A = (0, 3, 3, 0, 0, 1)
_SLOT_B = (1, 1, 2, 2, 3, 2)
_SLOT_A_IS_HIGHER = tuple(int(a > b) for a, b in zip(_SLOT_A, _SLOT_B))


def _cparams():
    return pltpu.CompilerParams(dimension_semantics=("arbitrary",), vmem_limit_bytes=VMEM_LIMIT)


def _cond_of_tile(t):
    return jnp.where(t < PROMPT_TILES, 0, 1 + (t - PROMPT_TILES) // TILES_PER_SAMPLE)


def _rmsnorm(x, g):
    ms = jnp.mean(x * x, axis=-1, keepdims=True)
    return (x * lax.rsqrt(ms + EPS)) * g


def _modulate(x, g, shift, scale):
    return _rmsnorm(x, g) * (1.0 + scale) + shift


def _silu(x):
    return x * jax.nn.sigmoid(x)


def _dot(a, b):
    return jnp.dot(a, b, preferred_element_type=F32)


def _dot_nt(a, b):
    return lax.dot_general(a, b, (((1,), (1,)), ((), ())), preferred_element_type=F32)


def _dot_tn(a, b):
    return lax.dot_general(a, b, (((0,), (0,)), ((), ())), preferred_element_type=F32)


def _split_bf16(x):
    hi = x.astype(BF16)
    lo = (x - hi.astype(F32)).astype(BF16)
    return hi, lo


def _mod_kernel(cond_ref, w_ref, b_ref, o_ref):
    s = _silu(cond_ref[...])
    o_ref[0] = _dot(s.astype(BF16), w_ref[0].astype(BF16)) + b_ref[0]


def _adaln(cond, w_mod, b_mod):
    depth = w_mod.shape[0]
    return pl.pallas_call(
        _mod_kernel,
        grid=(depth, 6 * D // MOD_TN),
        in_specs=[
            pl.BlockSpec((COND_ROWS, D), lambda i, j: (0, 0)),
            pl.BlockSpec((1, D, MOD_TN), lambda i, j: (i, 0, j)),
            pl.BlockSpec((1, 1, MOD_TN), lambda i, j: (i, 0, j)),
        ],
        out_specs=pl.BlockSpec((1, COND_ROWS, MOD_TN), lambda i, j: (i, 0, j)),
        out_shape=jax.ShapeDtypeStruct((depth, COND_ROWS, 6 * D), F32),
        compiler_params=pltpu.CompilerParams(
            dimension_semantics=("arbitrary", "arbitrary"), vmem_limit_bytes=VMEM_LIMIT),
        name="adaln",
    )(cond, w_mod, b_mod.reshape(depth, 1, 6 * D))


_ROW_SPEC = pl.BlockSpec((1, D), lambda i: (0, 0))
_STEP_MOD_SPEC = pl.BlockSpec((1, 6, 1, D), lambda i, *_: (_cond_of_tile(i * STEP_TILES), 0, 0, 0))
_STEP_SPEC = pl.BlockSpec((STEP_ROWS, D), lambda i, *_: (i, 0))
_SUB_ROWS = tuple(slice(s * TM, (s + 1) * TM) for s in range(STEP_TILES))


def _load_weight_bf16(w_hbm, w16_ref, stage_ref, sem):
    n = w16_ref.shape[1] // D

    def slab(c):
        return pltpu.make_async_copy(w_hbm.at[:, pl.ds(c * D, D)], stage_ref.at[c % 2], sem.at[c % 2])

    slab(0).start()
    for c in range(n):
        if c + 1 < n:
            slab(c + 1).start()
        slab(c).wait()
        w16_ref[:, c * D:(c + 1) * D] = stage_ref[c % 2].astype(BF16)


def _weight_scratch(n_slabs):
    n_stage = min(n_slabs, 2)
    return [pltpu.VMEM((D, n_slabs * D), BF16), pltpu.VMEM((n_stage, D, D), F32),
            pltpu.SemaphoreType.DMA((n_stage,))]


_ANY_SPEC = pl.BlockSpec(memory_space=pl.ANY)


def _token_tile(x_refs, rows=slice(None), prompt_steps=PROMPT_TILES):
    if len(x_refs) == 1:
        return x_refs[0][rows]
    return jnp.where(pl.program_id(0) < prompt_steps, x_refs[0][rows], x_refs[1][rows])


def _token_specs(x_parts, tiles_per_step=1):
    rows = tiles_per_step * TM
    if len(x_parts) == 1:
        return [pl.BlockSpec((rows, D), lambda i: (i, 0))]
    prompt_steps = T_PROMPT // rows
    return [pl.BlockSpec((rows, D), lambda i: (jnp.minimum(i, prompt_steps - 1), 0)),
            pl.BlockSpec((rows, D), lambda i: (jnp.maximum(i - prompt_steps, 0), 0))]


def _hgrn_proj_kernel(xp_ref, xs_ref, mod_ref, g_ref, w_hbm, q_ref, v_ref, ff_ref, fb_ref, gg_ref,
                      w_ref, stage_ref, sem):
    @pl.when(pl.program_id(0) == 0)
    def _():
        _load_weight_bf16(w_hbm, w_ref, stage_ref, sem)

    h = [_modulate(_token_tile((xp_ref, xs_ref), rows, PROMPT_STEPS),
                   g_ref[...], mod_ref[0, 0], mod_ref[0, 1]).astype(BF16) for rows in _SUB_ROWS]
    for rows, h_tile in zip(_SUB_ROWS, h):
        q_ref[rows] = (_dot(h_tile, w_ref[:, 0 * D:1 * D]) * (DK ** -0.5)).astype(BF16)
        v_ref[rows] = _dot(h_tile, w_ref[:, 1 * D:2 * D]).astype(BF16)
        ff_ref[rows] = _dot(h_tile, w_ref[:, 2 * D:3 * D])
        fb_ref[rows] = _dot(h_tile, w_ref[:, 3 * D:4 * D])
        gg_ref[rows] = _dot(h_tile, w_ref[:, 4 * D:5 * D]).astype(BF16)


def _hgrn_proj(x_parts, mod, norm_g, w_in):
    bf = jax.ShapeDtypeStruct((T, D), BF16)
    f32 = jax.ShapeDtypeStruct((T, D), F32)
    return pl.pallas_call(
        _hgrn_proj_kernel,
        grid=(N_STEPS,),
        in_specs=_token_specs(x_parts, STEP_TILES) + [_STEP_MOD_SPEC, _ROW_SPEC, _ANY_SPEC],
        out_specs=[_STEP_SPEC] * 5,
        out_shape=[bf, bf, f32, f32, bf],
        scratch_shapes=_weight_scratch(5),
        compiler_params=_cparams(),
        name="hgrn_proj",
    )(*x_parts, mod, norm_g, w_in)


GLA_TILES = 4
GLA_ROWS = GLA_TILES * TM
N_GLA_STEPS = N_TILES // GLA_TILES
GLA_PROMPT_STEPS = PROMPT_TILES // GLA_TILES
STEPS_PER_SAMPLE = SAMPLE_LEN // GLA_ROWS
_GLA_SPEC = pl.BlockSpec((GLA_ROWS, D), lambda i: (i, 0))
_GLA_TILE_ROWS = tuple(slice(s * TM, (s + 1) * TM) for s in range(GLA_TILES))


def _gla_step(block, q_ref, v_ref, f_ref, lbl_ref, s0_ref, st_ref, st_tiles_ref, ma_ref, o_ref,
              *, reverse, finish=None):
    is_prompt = block < GLA_PROMPT_STEPS
    rel = block - GLA_PROMPT_STEPS
    edge = (STEPS_PER_SAMPLE - 1) if reverse else 0
    sample_start = jnp.logical_and(rel >= 0, rel % STEPS_PER_SAMPLE == edge)

    @pl.when(is_prompt)
    def _():
        st_ref[...] = jnp.zeros_like(st_ref)

    @pl.when(sample_start)
    def _():
        for h in range(HEADS):
            st_ref[h] = s0_ref[0, 0, h].T

    logits = lbl_ref[...]
    e = jnp.exp(logits - jnp.max(logits, axis=0, keepdims=True))
    lb = e[0:1] / jnp.sum(e, axis=0, keepdims=True)

    ref_idx = (CHUNK - 1 - CHUNK // 2) if reverse else CHUNK // 2
    last_idx = 0 if reverse else CHUNK - 1
    shift = CHUNK.bit_length() - 1

    row = lax.broadcasted_iota(jnp.int32, (TM, TM), 0)
    col = lax.broadcasted_iota(jnp.int32, (TM, TM), 1)
    same = (row >> shift) == (col >> shift)
    ref_row = ((row >> shift) << shift) + ref_idx
    keep = jnp.logical_and(same, (row <= col) if reverse else (row >= col))

    @pl.when(pl.program_id(0) == 0)
    def _():
        at_ref = jnp.logical_and(same, (ref_row <= col) if reverse else (ref_row >= col))
        ma_ref[...] = (keep.astype(F32) - at_ref.astype(F32)).astype(BF16)

    ma = ma_ref[...]

    rr = lax.broadcasted_iota(jnp.int32, (2 * CHUNKS_PER_TILE, TM), 0)
    cc = lax.broadcasted_iota(jnp.int32, (2 * CHUNKS_PER_TILE, TM), 1)
    chunk = rr & (CHUNKS_PER_TILE - 1)
    target = chunk * CHUNK + jnp.where(rr < CHUNKS_PER_TILE, ref_idx, last_idx)
    covered = (cc >= target) if reverse else (cc <= target)
    rm = jnp.logical_and((cc >> shift) == chunk, covered).astype(BF16)
    ma_rm = jnp.concatenate([ma, rm, jnp.zeros_like(rm)], axis=0)

    chunk_rows = [slice(c * CHUNK, (c + 1) * CHUNK) for c in range(CHUNKS_PER_TILE)]
    order = range(CHUNKS_PER_TILE - 1, -1, -1) if reverse else range(CHUNKS_PER_TILE)
    pair_w = 2 * DK
    n_pairs = HEADS // 2
    tile_order = _GLA_TILE_ROWS[::-1] if reverse else _GLA_TILE_ROWS

    def per_chunk_scale(x, scale):
        return jnp.concatenate([x[rs] * scale[c:c + 1] for c, rs in enumerate(chunk_rows)], axis=0)

    def pair_cols(p):
        return slice(p * pair_w, (p + 1) * pair_w)

    def stage_decay(k, p):
        lbp = lb[:, pair_cols(p)]
        f = lbp + (1.0 - lbp) * jax.nn.sigmoid(f_ref[tile_order[k], pair_cols(p)])
        lf_hi, lf_lo = _split_bf16(jnp.log(f))
        both = _dot(ma_rm, lf_hi) + _dot(ma_rm, lf_lo)
        z, marks = both[:TM], both[TM:TM + 2 * CHUNKS_PER_TILE]
        return 1.0 - f, z, marks

    def stage_scores(k, p, kk, z, marks):
        ref, last = marks[:CHUNKS_PER_TILE], marks[CHUNKS_PER_TILE:]
        qa32 = q_ref[tile_order[k], pair_cols(p)].astype(F32) * jnp.exp(z)
        kb32 = kk * jnp.exp(-z)
        qa = qa32.astype(BF16)
        kb = kb32.astype(BF16)
        qe = per_chunk_scale(qa32, jnp.exp(ref)).astype(BF16)
        kd = per_chunk_scale(kb32, jnp.exp(last - ref)).astype(BF16)
        v = v_ref[tile_order[k], pair_cols(p)]
        scores, incr = [], []
        for hh in range(2):
            ls = slice(hh * DK, (hh + 1) * DK)
            scores.append(_dot_nt(qa[:, ls], kb[:, ls]))
            kd_h = kd[:, ls]
            kd_blocks = jnp.concatenate(
                [jnp.concatenate([part for part in (jnp.zeros((c * CHUNK, DK), BF16), kd_h[rs],
                                                    jnp.zeros((TM - (c + 1) * CHUNK, DK), BF16))
                                  if part.shape[0]], axis=0)
                 for c, rs in enumerate(chunk_rows)], axis=1)
            incr.append(_dot_tn(v[:, ls], kd_blocks))
        return scores, incr, qe, jnp.exp(last), v

    def stage_output(k, p, scores, incr, qe, decay, v):
        base = tile_order[k].start
        for hh in range(2):
            h = 2 * p + hh
            ls = slice(hh * DK, (hh + 1) * DK)
            hs = slice(h * DK, (h + 1) * DK)
            o_intra = _dot(jnp.where(keep, scores[hh], 0.0).astype(BF16), v[:, ls])
            st = st_ref[h]
            if k > 0:
                st = jnp.where(is_prompt, 0.0, st)
            for c in order:
                rs = chunk_rows[c]
                o_ref[base + rs.start:base + rs.stop, hs] = (
                    o_intra[rs] + _dot_nt(qe[rs, ls], st.astype(BF16)))
                st = st * decay[c:c + 1, ls] + incr[hh][:, c * DK:(c + 1) * DK]
            st_ref[h] = st
            if k < GLA_TILES - 1:
                st_tiles_ref[k, h] = st
        if finish is not None:
            return (tile_order[k], p)

    def stage_finish(k, p, rows, pair):
        finish(rows, pair)

    units = [(k, p) for k in range(GLA_TILES) for p in range(n_pairs)]
    stages = [stage_decay, stage_scores, stage_output] + ([stage_finish] if finish is not None else [])
    carried = [dict() for _ in stages]
    for step in range(len(units) + len(stages) - 1):
        for depth, stage in (enumerate(stages) if reverse else reversed(list(enumerate(stages)))):
            u = step - depth
            if 0 <= u < len(units):
                out = stage(*units[u], *(carried[depth].pop(u) if depth else ()))
                if depth + 1 < len(stages):
                    carried[depth + 1][u] = out


def _store_prompt_states(block, st_ref, st_tiles_ref, snew_ref, *, reverse, sfwd_ref=None):
    @pl.when(block < GLA_PROMPT_STEPS)
    def _():
        for k in range(GLA_TILES):
            tile = GLA_TILES - 1 - k if reverse else k
            for h in range(HEADS):
                state = (st_tiles_ref[k, h] if k < GLA_TILES - 1 else st_ref[h]).T
                if sfwd_ref is None:
                    snew_ref[tile, h] = state
                else:
                    snew_ref[tile, 1, h] = state
        if sfwd_ref is not None:
            snew_ref[:, 0] = sfwd_ref[...]


def _gla_fwd_kernel(q_ref, v_ref, f_ref, lbl_ref, s0_ref, o_ref, snew_ref, st_ref, st_tiles_ref, ma_ref):
    block = pl.program_id(0)
    _gla_step(block, q_ref, v_ref, f_ref, lbl_ref, s0_ref, st_ref, st_tiles_ref, ma_ref, o_ref,
              reverse=False)
    _store_prompt_states(block, st_ref, st_tiles_ref, snew_ref, reverse=False)


def _gla_bwd_kernel(q_ref, v_ref, f_ref, lbl_ref, s0_ref, of_ref, gg_ref, ng_ref, sfwd_ref,
                    a_ref, snew_ref, st_ref, st_tiles_ref, ma_ref, ob_ref):
    block = N_GLA_STEPS - 1 - pl.program_id(0)

    def finish(rows, p):
        for h in (2 * p, 2 * p + 1):
            hs = slice(h * DV, (h + 1) * DV)
            o = of_ref[rows, hs] + ob_ref[rows, hs]
            a_ref[rows, hs] = (_rmsnorm(o, ng_ref[...]) * _silu(gg_ref[rows, hs].astype(F32))).astype(BF16)

    _gla_step(block, q_ref, v_ref, f_ref, lbl_ref, s0_ref, st_ref, st_tiles_ref, ma_ref, ob_ref,
              reverse=True, finish=finish)
    _store_prompt_states(block, st_ref, st_tiles_ref, snew_ref, reverse=True, sfwd_ref=sfwd_ref)


def _state_specs(direction, block_of_step):
    sample = lambda i: jnp.clip((block_of_step(i) - GLA_PROMPT_STEPS) // STEPS_PER_SAMPLE, 0, N_SAMPLE_SEQ - 1)
    s0 = pl.BlockSpec((1, 1, HEADS, DK, DV), lambda i: (sample(i), direction, 0, 0, 0))
    snew = pl.BlockSpec((GLA_TILES, HEADS, DK, DV),
                        lambda i: (jnp.minimum(block_of_step(i), GLA_PROMPT_STEPS - 1), 0, 0, 0))
    return s0, snew


_GLA_STATE_SCRATCH = [pltpu.VMEM((HEADS, DV, DK), F32), pltpu.VMEM((GLA_TILES - 1, HEADS, DV, DK), F32),
                      pltpu.VMEM((TM, TM), BF16)]


def _gla_fwd(q, v, ff, lb_logits, s0):
    s0_spec, snew_spec = _state_specs(0, lambda i: i)
    return pl.pallas_call(
        _gla_fwd_kernel,
        grid=(N_GLA_STEPS,),
        in_specs=[_GLA_SPEC, _GLA_SPEC, _GLA_SPEC,
                  pl.BlockSpec(lb_logits.shape, lambda i: (0, 0)), s0_spec],
        out_specs=[_GLA_SPEC, snew_spec],
        out_shape=[jax.ShapeDtypeStruct((T, D), F32),
                   jax.ShapeDtypeStruct((N_PROMPT_SEQ, HEADS, DK, DV), F32)],
        scratch_shapes=_GLA_STATE_SCRATCH,
        compiler_params=_cparams(),
        name="gla_fwd",
    )(q, v, ff, lb_logits, s0)


def _gla_bwd(q, v, fb, lb_logits, s0, o_f, gg, head_norm_g, s_fwd):
    rev = lambda i: N_GLA_STEPS - 1 - i
    block = pl.BlockSpec((GLA_ROWS, D), lambda i: (rev(i), 0))
    s0_spec, sfwd_spec = _state_specs(1, rev)
    prompt_block = lambda i: jnp.minimum(rev(i), GLA_PROMPT_STEPS - 1)
    return pl.pallas_call(
        _gla_bwd_kernel,
        grid=(N_GLA_STEPS,),
        in_specs=[block, block, block, pl.BlockSpec(lb_logits.shape, lambda i: (0, 0)), s0_spec,
                  block, block, pl.BlockSpec((1, DV), lambda i: (0, 0)), sfwd_spec],
        out_specs=[block, pl.BlockSpec((GLA_TILES, 2, HEADS, DK, DV), lambda i: (prompt_block(i), 0, 0, 0, 0))],
        out_shape=[jax.ShapeDtypeStruct((T, D), BF16),
                   jax.ShapeDtypeStruct((N_PROMPT_SEQ, 2, HEADS, DK, DV), F32)],
        scratch_shapes=_GLA_STATE_SCRATCH + [pltpu.VMEM((GLA_ROWS, D), F32)],
        compiler_params=_cparams(),
        name="gla_bwd",
    )(q, v, fb, lb_logits, s0, o_f, gg, head_norm_g, s_fwd)


def _conv_mixer_kernel(x_ref, mod_ref, g_ref, w_hbm, cw_ref, a_ref, w_ref, stage_ref, sem):
    @pl.when(pl.program_id(0) == 0)
    def _():
        _load_weight_bf16(w_hbm, w_ref, stage_ref, sem)

    seg = jnp.where(pl.program_id(0) < PROMPT_TILES // CONV_TILES, PROMPT_LEN, GRID_W)
    pos = lax.broadcasted_iota(jnp.int32, (TM, 1), 0) & (seg - 1)

    def project(rows):
        h = _modulate(x_ref[rows], g_ref[...], mod_ref[0, 0], mod_ref[0, 1]).astype(BF16)
        return [_dot(h, w_ref[:, k * D:(k + 1) * D]) for k in range(3)]

    def convolve(rows, bg, cg, x_in):
        u = cg * x_in
        prev = jnp.where(pos == 0, 0.0, pltpu.roll(u, 1, axis=0))
        nxt = jnp.where(pos == seg - 1, 0.0, pltpu.roll(u, TM - 1, axis=0))
        conv = cw_ref[0:1] * prev + cw_ref[1:2] * u + cw_ref[2:3] * nxt
        a_ref[rows] = (bg * conv).astype(BF16)

    tiles = [slice(s * TM, (s + 1) * TM) for s in range(CONV_TILES)]
    projected = [project(rows) for rows in tiles]
    for rows, parts in zip(tiles, projected):
        convolve(rows, *parts)


def _conv_mixer(x, mod, norm_g, w_in, conv_w):
    block = pl.BlockSpec((CONV_TILES * TM, D), lambda i: (i, 0))
    return pl.pallas_call(
        _conv_mixer_kernel,
        grid=(N_TILES // CONV_TILES,),
        in_specs=[block, pl.BlockSpec((1, 6, 1, D), lambda i: (_cond_of_tile(i * CONV_TILES), 0, 0, 0)),
                  _ROW_SPEC, _ANY_SPEC, pl.BlockSpec((3, D), lambda i: (0, 0))],
        out_specs=block,
        out_shape=jax.ShapeDtypeStruct((T, D), BF16),
        scratch_shapes=_weight_scratch(3),
        compiler_params=_cparams(),
        name="conv_mixer",
    )(x, mod, norm_g, w_in, conv_w)


ROUTE_ROWS = 32


def _first_member(vals, target):
    idx = jnp.full_like(target, float(EPG - 1))
    for j in range(EPG - 2, -1, -1):
        idx = jnp.where(vals[j] == target, float(j), idx)
    return idx


def _router_logits(h2, rwt_ref):
    h_hi, h_lo = _split_bf16(h2)
    w_hi, w_lo = _split_bf16(rwt_ref[...])
    with_hi = _dot_nt(jnp.concatenate([w_hi, w_lo], axis=0), h_hi)
    return with_hi[:N_EXPERTS] + (_dot_nt(w_hi, h_lo) + with_hi[N_EXPERTS:])


def _select_experts(logits, rb_ref):
    scores = jax.nn.sigmoid(logits)
    sel = scores + rb_ref[...]
    member = [sel[j * N_GROUPS:(j + 1) * N_GROUPS] for j in range(EPG)]
    m1 = functools.reduce(jnp.maximum, member)
    i1 = _first_member(member, m1)
    rest = [jnp.where(i1 == float(j), -jnp.inf, member[j]) for j in range(EPG)]
    m2 = functools.reduce(jnp.maximum, rest)
    i2 = _first_member(rest, m2)
    group_score = m1 + m2

    best, grp, first, second = group_score[0:1], jnp.zeros((1, TM), F32), i1[0:1], i2[0:1]
    for g in range(1, N_GROUPS):
        upd = group_score[g:g + 1] > best
        best = jnp.where(upd, group_score[g:g + 1], best)
        grp = jnp.where(upd, float(g), grp)
        first = jnp.where(upd, i1[g:g + 1], first)
        second = jnp.where(upd, i2[g:g + 1], second)
    a = jnp.minimum(first, second)
    b = jnp.maximum(first, second)
    row = lax.broadcasted_iota(jnp.int32, (N_EXPERTS, TM), 0).astype(F32)
    s_lo = jnp.sum(jnp.where(row == a * N_GROUPS + grp, scores, 0.0), axis=0, keepdims=True)
    s_hi = jnp.sum(jnp.where(row == b * N_GROUPS + grp, scores, 0.0), axis=0, keepdims=True)
    tot = s_lo + s_hi
    pair = jnp.where(a == 0.0, jnp.where(b == 1.0, 0.0, jnp.where(b == 2.0, 3.0, 4.0)),
                     jnp.where(a == 1.0, jnp.where(b == 2.0, 5.0, 1.0), 2.0))
    return grp * N_PAIRS + pair, s_lo / tot, s_hi / tot


def _post_mixer_kernel(*refs, n_x):
    a_ref, x_refs = refs[0], refs[1:1 + n_x]
    (mod_ref, g2_ref, wo_hbm, rwt_ref, rb_ref,
     x1_ref, gates_ref, bucket_ref, rank_ref, cnt_ref, carry_ref, earlier_ref, wo_ref, stage_ref, sem) = refs[1 + n_x:]

    @pl.when(pl.program_id(0) == 0)
    def _():
        _load_weight_bf16(wo_hbm, wo_ref, stage_ref, sem)
        carry_ref[...] = jnp.zeros_like(carry_ref)
        s_idx = lax.broadcasted_iota(jnp.int32, (TM, TM), 0)
        t_idx = lax.broadcasted_iota(jnp.int32, (TM, TM), 1)
        earlier_ref[...] = (s_idx < t_idx).astype(BF16)

    def stage_mix(s):
        rows = slice(s * TM, (s + 1) * TM)
        x1 = (_token_tile(x_refs, rows, PROMPT_TILES // ROUTE_TILES)
              + mod_ref[0, 2] * _dot(a_ref[rows], wo_ref[...]))
        x1_ref[rows] = x1
        return _modulate(x1, g2_ref[...], mod_ref[0, 3], mod_ref[0, 4])

    def stage_rank(s, bucket, g_lo, g_hi):
        onehot = lax.broadcasted_iota(jnp.int32, (ROUTE_ROWS, TM), 0).astype(F32) == bucket
        before = _dot(onehot.astype(BF16), earlier_ref[...])
        oh = onehot.astype(F32)
        carry = carry_ref[...]
        rank = jnp.sum(oh * (before + carry[:, 0:1]), axis=0, keepdims=True)
        carry_ref[...] = carry + jnp.sum(oh, axis=1, keepdims=True)
        bucket_ref[s] = bucket.astype(jnp.int32)
        rank_ref[s] = rank.astype(jnp.int32)
        grow = lax.broadcasted_iota(jnp.int32, (LANES, TM), 0)
        gates_ref[s * TM:(s + 1) * TM] = jnp.where(grow == 0, g_lo, jnp.where(grow == 1, g_hi, 0.0)).T

    tiles = range(ROUTE_TILES)
    h2 = [stage_mix(s) for s in tiles]
    logits = [_router_logits(h2[s], rwt_ref) for s in tiles]
    picks = [_select_experts(logits[s], rb_ref) for s in tiles]
    for s in tiles:
        stage_rank(s, *picks[s])
    cnt_ref[...] = carry_ref[...]


def _post_mixer(a, x_parts, mod, norm_g2, w_out, router_wt, router_bt):
    rows = ROUTE_TILES * TM
    idx_spec = pl.BlockSpec((ROUTE_TILES, 1, TM), lambda i: (i, 0, 0))
    return pl.pallas_call(
        functools.partial(_post_mixer_kernel, n_x=len(x_parts)),
        grid=(N_TILES // ROUTE_TILES,),
        in_specs=[pl.BlockSpec((rows, D), lambda i: (i, 0))] + _token_specs(x_parts, ROUTE_TILES) + [
            pl.BlockSpec((1, 6, 1, D), lambda i: (_cond_of_tile(i * ROUTE_TILES), 0, 0, 0)), _ROW_SPEC,
            _ANY_SPEC,
            pl.BlockSpec((N_EXPERTS, D), lambda i: (0, 0)),
            pl.BlockSpec((N_EXPERTS, 1), lambda i: (0, 0))],
        out_specs=[pl.BlockSpec((rows, D), lambda i: (i, 0)), pl.BlockSpec((rows, LANES), lambda i: (i, 0)),
                   idx_spec, idx_spec, pl.BlockSpec((ROUTE_ROWS, LANES), lambda i: (0, 0))],
        out_shape=[jax.ShapeDtypeStruct((T, D), F32), jax.ShapeDtypeStruct((T, LANES), F32),
                   jax.ShapeDtypeStruct((N_TILES, 1, TM), jnp.int32),
                   jax.ShapeDtypeStruct((N_TILES, 1, TM), jnp.int32),
                   jax.ShapeDtypeStruct((ROUTE_ROWS, LANES), F32)],
        scratch_shapes=[pltpu.VMEM((ROUTE_ROWS, LANES), F32), pltpu.VMEM((TM, TM), BF16)] + _weight_scratch(1),
        compiler_params=_cparams(),
        name="post_mixer",
    )(a, *x_parts, mod, norm_g2, w_out, router_wt, router_bt)


_MOVE_SPEC = pl.BlockSpec((MOVE_ROWS, D), lambda i, *_: (i, 0))
_MOVE_MOD_SPEC = pl.BlockSpec((1, 6, 1, D), lambda i, *_: (_cond_of_tile(i * MOVE_TILES), 0, 0, 0))


def _smem_step_spec(step_of_step):
    return pl.BlockSpec((MOVE_TILES, 1, TM), lambda i, *_: (step_of_step(i), 0, 0), memory_space=pltpu.SMEM)


def _start_step_rows(make_copy):
    for tile in range(MOVE_TILES):
        for r in range(TM):
            make_copy(tile, r).start(priority=r % 2)


def _dispatch_kernel(zero_from_ref, slot_ref, x1_ref, mod_ref, g2_ref, gates_ref,
                     xs_ref, buf_ref, zero_ref, sems, zero_sem):
    i = pl.program_id(0)
    cur = i % 2

    @pl.when(i == 0)
    def _():
        zero_ref[...] = jnp.zeros_like(zero_ref)

        def for_each_chunk(act):
            def tile(j, carry):
                def chunk(c, inner):
                    row = pl.multiple_of(j * TM_E + c * ZERO_ROWS, ZERO_ROWS)
                    act(pltpu.make_async_copy(zero_ref, xs_ref.at[pl.ds(row, ZERO_ROWS)], zero_sem))
                    return inner
                return lax.fori_loop(zero_from_ref[j], TM_E // ZERO_ROWS, chunk, carry)
            lax.fori_loop(0, N_TILES_E, tile, 0)

        for_each_chunk(lambda copy: copy.start())
        for_each_chunk(lambda copy: copy.wait())

    def wait_step(b):
        pltpu.make_async_copy(buf_ref.at[b], xs_ref.at[pl.ds(0, MOVE_ROWS)], sems.at[b]).wait()

    h2 = _modulate(x1_ref[...], g2_ref[...], mod_ref[0, 3], mod_ref[0, 4])
    gates = gates_ref[...]

    def send(b):
        buf_ref[b, :, :D] = h2
        buf_ref[b, :, D:] = gates
        _start_step_rows(lambda tile, r: pltpu.make_async_copy(
            buf_ref.at[b, pl.ds(tile * TM + r, 1)], xs_ref.at[pl.ds(slot_ref[tile, 0, r], 1)], sems.at[b]))

        @pl.when(i > 0)
        def _():
            wait_step(1 - b)

        @pl.when(i == N_MOVE_STEPS - 1)
        def _():
            wait_step(b)

    for b in range(2):
        pl.when(cur == b)(functools.partial(send, b))


def _dispatch(zero_from, slot, x1, mod, norm_g2, gates):
    return pl.pallas_call(
        _dispatch_kernel,
        grid_spec=pltpu.PrefetchScalarGridSpec(
            num_scalar_prefetch=1,
            grid=(N_MOVE_STEPS,),
            in_specs=[_smem_step_spec(lambda i: i), _MOVE_SPEC, _MOVE_MOD_SPEC,
                      pl.BlockSpec((1, D), lambda i, *_: (0, 0)),
                      pl.BlockSpec((MOVE_ROWS, LANES), lambda i, *_: (i, 0))],
            out_specs=pl.BlockSpec(memory_space=pl.ANY),
            scratch_shapes=[pltpu.VMEM((2, MOVE_ROWS, XW), F32), pltpu.VMEM((ZERO_ROWS, XW), F32),
                            pltpu.SemaphoreType.DMA((2,)), pltpu.SemaphoreType.DMA(())],
        ),
        out_shape=jax.ShapeDtypeStruct((S_ROWS, XW), F32),
        compiler_params=_cparams(),
        name="moe_dispatch",
    )(zero_from, slot, x1, mod, norm_g2, gates)


_W_SHAPES = ((D, D_FF), (D, D_FF), (D_FF, D))


def _expert_kernel(ea_ref, eb_ref, next_a_ref, next_b_ref, a_higher_ref, nrows_ref, xblock_ref,
                   x_ref, wg_hbm, wu_hbm, wd_hbm, y_ref, *scratch, layer):
    stage, w16, sems = scratch[0:3], scratch[3:6], scratch[6]
    j = pl.program_id(0)
    n = nrows_ref[j]
    prev = jnp.maximum(j - 1, 0)

    def fetch(slot, expert):
        return [pltpu.make_async_copy(w.at[layer, expert], s.at[slot], sems.at[slot])
                for w, s in zip((wg_hbm, wu_hbm, wd_hbm), stage)]

    for slot, e_ref, next_ref in ((0, ea_ref, next_a_ref), (1, eb_ref, next_b_ref)):
        @pl.when(j == 0)
        def _(slot=slot, e_ref=e_ref):
            for copy in fetch(slot, e_ref[0]):
                copy.start()

        @pl.when(jnp.logical_or(j == 0, e_ref[j] != e_ref[prev]))
        def _(slot=slot, e_ref=e_ref, next_ref=next_ref):
            for copy in fetch(slot, e_ref[j]):
                copy.wait()
            for s, d in zip(stage, w16):
                d[slot] = s[slot].astype(BF16)

            @pl.when(next_ref[j] >= 0)
            def _():
                for copy in fetch(slot, next_ref[j]):
                    copy.start(priority=1)

    def run(rows):
        valid = lax.broadcasted_iota(jnp.int32, (rows, 1), 0) < n
        xe = jnp.where(valid, x_ref[:rows], 0.0)
        x = xe[:, :D].astype(BF16)
        g_lo, g_hi = xe[:, D:D + 1], xe[:, D + 1:D + 2]
        a_higher = a_higher_ref[j] != 0

        def ffn(slot, gate):
            wg_ref, wu_ref, wd_ref = w16
            act = _silu(_dot(x, wg_ref[slot])) * _dot(x, wu_ref[slot]) * gate
            return _dot(act.astype(BF16), wd_ref[slot])

        y_ref[:rows] = (ffn(0, jnp.where(a_higher, g_hi, g_lo))
                        + ffn(1, jnp.where(a_higher, g_lo, g_hi)))
        if rows < TM_E:
            y_ref[rows:] = jnp.zeros((TM_E - rows, D), F32)

    half = TM_E // 2
    pl.when(n > half)(functools.partial(run, TM_E))
    pl.when(jnp.logical_and(n > 0, n <= half))(functools.partial(run, half))

    @pl.when(n == 0)
    def _():
        y_ref[...] = jnp.zeros_like(y_ref)


def _experts(tile_ea, tile_eb, tile_next_a, tile_next_b, tile_a_higher, tile_rows, tile_xblock, xs,
             w_gate, w_up, w_down, layer):
    any_spec = pl.BlockSpec(memory_space=pl.ANY)
    return pl.pallas_call(
        functools.partial(_expert_kernel, layer=layer),
        grid_spec=pltpu.PrefetchScalarGridSpec(
            num_scalar_prefetch=7,
            grid=(N_TILES_E,),
            in_specs=[pl.BlockSpec((TM_E, XW), lambda j, *plan: (plan[-1][j], 0)), any_spec, any_spec, any_spec],
            out_specs=pl.BlockSpec((TM_E, D), lambda j, *_: (j, 0)),
            scratch_shapes=[pltpu.VMEM((2,) + s, F32) for s in _W_SHAPES]
                           + [pltpu.VMEM((2,) + s, BF16) for s in _W_SHAPES]
                           + [pltpu.SemaphoreType.DMA((2,))],
        ),
        out_shape=jax.ShapeDtypeStruct((S_ROWS, D), F32),
        compiler_params=_cparams(),
        name="moe_experts",
    )(tile_ea, tile_eb, tile_next_a, tile_next_b, tile_a_higher, tile_rows, tile_xblock, xs,
      w_gate, w_up, w_down)


def _combine_kernel(slot_ref, next_slot_ref, x1_ref, mod_ref, gf_ref, y_ref, *rest, final):
    out_refs, (buf_ref, sems) = rest[:-2], rest[-2:]
    i = pl.program_id(0)
    cur = i % 2

    def gather(slots, b):
        _start_step_rows(lambda tile, r: pltpu.make_async_copy(
            y_ref.at[pl.ds(slots[tile, 0, r], 1)], buf_ref.at[b, pl.ds(tile * TM + r, 1)], sems.at[b]))

    @pl.when(i == 0)
    def _():
        gather(slot_ref, 0)

    def combine(b):
        @pl.when(i + 1 < N_MOVE_STEPS)
        def _():
            gather(next_slot_ref, 1 - b)

        pltpu.make_async_copy(y_ref.at[pl.ds(0, MOVE_ROWS)], buf_ref.at[b], sems.at[b]).wait()
        x2 = x1_ref[...] + mod_ref[0, 5] * buf_ref[b]
        if not final:
            out_refs[0][...] = x2
            return
        y = _rmsnorm(x2, gf_ref[...])
        yp_ref, ys_ref = out_refs

        @pl.when(i < T_PROMPT // MOVE_ROWS)
        def _():
            yp_ref[...] = y

        @pl.when(i >= T_PROMPT // MOVE_ROWS)
        def _():
            ys_ref[...] = y

    for b in range(2):
        pl.when(cur == b)(functools.partial(combine, b))


def _combine(slot, x1, mod, final_g, y_sorted, *, final):
    if final:
        out_specs = _token_specs((None, None), MOVE_TILES)
        out_shape = [jax.ShapeDtypeStruct((T_PROMPT, D), F32), jax.ShapeDtypeStruct((T_SAMPLE, D), F32)]
    else:
        out_specs = [_MOVE_SPEC]
        out_shape = [jax.ShapeDtypeStruct((T, D), F32)]
    return pl.pallas_call(
        functools.partial(_combine_kernel, final=final),
        grid=(N_MOVE_STEPS,),
        in_specs=[_smem_step_spec(lambda i: i), _smem_step_spec(lambda i: jnp.minimum(i + 1, N_MOVE_STEPS - 1)),
                  _MOVE_SPEC, _MOVE_MOD_SPEC, _ROW_SPEC, pl.BlockSpec(memory_space=pl.ANY)],
        out_specs=out_specs,
        scratch_shapes=[pltpu.VMEM((2, MOVE_ROWS, D), F32), pltpu.SemaphoreType.DMA((2,))],
        out_shape=out_shape,
        compiler_params=_cparams(),
        name="moe_combine_final" if final else "moe_combine",
    )(slot, slot, x1, mod, final_g, y_sorted)


CONV_ROWS = CONV_TILES * TM
_GATHER_GROUPS = 3 * CONV_TILES


def _combine_conv_kernel(slot_ref, next_slot_ref, x1_ref, mod0_ref, mod_ref, g_ref, w_hbm, cw_ref, y_ref,
                         x2_ref, a_ref, w_ref, stage_ref, sem, row_sem):
    i = pl.program_id(0)

    def gather(slots, lo, hi):
        for r in range(lo, hi):
            pltpu.make_async_copy(y_ref.at[pl.ds(slots[r // TM, 0, r % TM], 1)],
                                  stage_ref.at[0, pl.ds(r, 1)], row_sem.at[0]).start(priority=r % 2)

    def wait_rows():
        pltpu.make_async_copy(y_ref.at[pl.ds(0, CONV_ROWS)], stage_ref.at[0], row_sem.at[0]).wait()

    @pl.when(i == 0)
    def _():
        _load_weight_bf16(w_hbm, w_ref, stage_ref, sem)
        gather(slot_ref, 0, CONV_ROWS)

    wait_rows()
    x2_ref[...] = x1_ref[...] + mod0_ref[0, 5] * stage_ref[0]

    seg = jnp.where(i < PROMPT_TILES // CONV_TILES, PROMPT_LEN, GRID_W)
    pos = lax.broadcasted_iota(jnp.int32, (TM, 1), 0) & (seg - 1)
    tiles = [slice(s * TM, (s + 1) * TM) for s in range(CONV_TILES)]
    groups = [g * CONV_ROWS // _GATHER_GROUPS for g in range(_GATHER_GROUPS + 1)]

    def project(s):
        h = _modulate(x2_ref[tiles[s]], g_ref[...], mod_ref[0, 0], mod_ref[0, 1]).astype(BF16)
        parts = []
        for k in range(3):
            gather(next_slot_ref, groups[3 * s + k], groups[3 * s + k + 1])
            parts.append(_dot(h, w_ref[:, k * D:(k + 1) * D]))
        return parts

    def convolve(rows, bg, cg, x_in):
        u = cg * x_in
        prev = jnp.where(pos == 0, 0.0, pltpu.roll(u, 1, axis=0))
        nxt = jnp.where(pos == seg - 1, 0.0, pltpu.roll(u, TM - 1, axis=0))
        conv = cw_ref[0:1] * prev + cw_ref[1:2] * u + cw_ref[2:3] * nxt
        a_ref[rows] = (bg * conv).astype(BF16)

    parts = project(0)
    for s in range(1, CONV_TILES):
        nxt_parts = project(s)
        convolve(tiles[s - 1], *parts)
        parts = nxt_parts
    convolve(tiles[-1], *parts)

    @pl.when(i == N_TILES // CONV_TILES - 1)
    def _():
        wait_rows()


def _combine_conv(slot, x1, mod0, y_sorted, mod, norm_g, w_in, conv_w):
    n_steps = N_TILES // CONV_TILES
    block = pl.BlockSpec((CONV_ROWS, D), lambda i: (i, 0))
    mod_spec = pl.BlockSpec((1, 6, 1, D), lambda i: (_cond_of_tile(i * CONV_TILES), 0, 0, 0))
    slots = lambda step: pl.BlockSpec((CONV_TILES, 1, TM), lambda i: (step(i), 0, 0), memory_space=pltpu.SMEM)
    return pl.pallas_call(
        _combine_conv_kernel,
        grid=(n_steps,),
        in_specs=[slots(lambda i: i), slots(lambda i: jnp.minimum(i + 1, n_steps - 1)), block, mod_spec, mod_spec,
                  _ROW_SPEC, _ANY_SPEC, pl.BlockSpec((3, D), lambda i: (0, 0)), _ANY_SPEC],
        out_specs=[block, block],
        out_shape=[jax.ShapeDtypeStruct((T, D), F32), jax.ShapeDtypeStruct((T, D), BF16)],
        scratch_shapes=_weight_scratch(3) + [pltpu.SemaphoreType.DMA((1,))],
        compiler_params=_cparams(),
        name="moe_combine_conv",
    )(slot, slot, x1, mod0, mod, norm_g, w_in, conv_w, y_sorted)


def _plan_kernel(cnt_ref, bucket_ref, rank_ref, slot_ref,
                 ea_ref, eb_ref, next_a_ref, next_b_ref, a_higher_ref, rows_ref, xblock_ref, zero_from_ref):
    slot = rank_ref[...]
    bucket = bucket_ref[...]
    tile0 = jnp.int32(0)
    last = [jnp.int32(0)] * 3
    for b in range(N_BUCKETS):
        n = cnt_ref[b]
        tiles = (n + (TM_E - 1)) // TM_E
        pair = b % N_PAIRS
        per_tile = ((b // N_PAIRS) * EPG + _SLOT_A[pair], (b // N_PAIRS) * EPG + _SLOT_B[pair],
                    _SLOT_A_IS_HIGHER[pair])

        def fill(k, carry, n=n, tile0=tile0, per_tile=per_tile):
            j = tile0 + k
            live = jnp.minimum(n - k * TM_E, TM_E)
            rows_ref[j] = live
            xblock_ref[j] = j
            zero_from_ref[j] = live // ZERO_ROWS
            for ref, value in zip((ea_ref, eb_ref, a_higher_ref), per_tile):
                ref[j] = jnp.int32(value)
            return carry

        lax.fori_loop(0, tiles, fill, 0)
        slot = slot + jnp.where(bucket == b, tile0 * TM_E, 0)
        last = [jnp.where(tiles > 0, value, old) for value, old in zip(per_tile, last)]
        tile0 = tile0 + tiles

    def unused(j, carry):
        rows_ref[j] = jnp.int32(0)
        xblock_ref[j] = jnp.maximum(tile0 - 1, 0)
        zero_from_ref[j] = jnp.int32(0)
        for ref, value in zip((ea_ref, eb_ref, a_higher_ref), last):
            ref[j] = value
        return carry

    lax.fori_loop(tile0, N_TILES_E, unused, 0)

    for e_ref, next_ref in ((ea_ref, next_a_ref), (eb_ref, next_b_ref)):
        next_ref[N_TILES_E - 1] = jnp.int32(-1)

        def backward(k, carry, e_ref=e_ref, next_ref=next_ref):
            j = N_TILES_E - 2 - k
            next_ref[j] = jnp.where(e_ref[j] == e_ref[j + 1], next_ref[j + 1], e_ref[j + 1])
            return carry

        lax.fori_loop(0, N_TILES_E - 1, backward, 0)
    slot_ref[...] = slot


def _plan(counts, bucket, rank):
    whole = pl.BlockSpec(bucket.shape, lambda i, *_: (0, 0, 0))
    smem = pl.BlockSpec(memory_space=pltpu.SMEM)
    per_tile = jax.ShapeDtypeStruct((N_TILES_E,), jnp.int32)
    slot, *tile_plan = pl.pallas_call(
        _plan_kernel,
        grid_spec=pltpu.PrefetchScalarGridSpec(
            num_scalar_prefetch=1, grid=(1,), in_specs=[whole, whole], out_specs=[whole] + [smem] * 8),
        out_shape=[jax.ShapeDtypeStruct(bucket.shape, jnp.int32)] + [per_tile] * 8,
        compiler_params=_cparams(),
        name="moe_plan",
    )(counts[:, 0].astype(jnp.int32), bucket, rank)
    return slot, tile_plan


def _moe(a, x_parts, mod, norm_g2, w_out, router_wt, router_bt, w_gate, w_up, w_down, layer, final_g,
         *, final):
    x1, gates, bucket, rank, counts = _post_mixer(a, x_parts, mod, norm_g2, w_out, router_wt, router_bt)
    slot, (*expert_plan, zero_from) = _plan(counts, bucket, rank)
    xs = _dispatch(zero_from, slot, x1, mod, norm_g2, gates)
    ys = _experts(*expert_plan, xs, w_gate, w_up, w_down, layer)
    if not final:
        return slot, x1, ys
    return _combine(slot, x1, mod, final_g, ys, final=True)


def kernel(x_prompt, x_sample, state_hgrn, c, c_ctx, w_mod, b_mod, norm_mix_g, norm_ffn_g, norm_final_g,
           hgrn_w_in, hgrn_lb_logits, hgrn_norm_g, hgrn_w_out, conv_w_in, conv_w, conv_w_out,
           router_w, router_b, moe_w_gate, moe_w_up, moe_w_down):
    x_parts = (x_prompt.reshape(T_PROMPT, D), x_sample.reshape(T_SAMPLE, D))
    cond = jnp.concatenate([c_ctx[None], c, jnp.zeros((COND_ROWS - N_COND, D), F32)], axis=0)
    mods = _adaln(cond, w_mod, b_mod)[:, :N_COND].reshape(2, N_COND, 6, 1, D)
    member_major = lambda w: w.reshape(N_GROUPS, EPG, -1).transpose(1, 0, 2).reshape(N_EXPERTS, -1)
    rwt = member_major(router_w.T)
    rbt = member_major(router_b.reshape(N_EXPERTS, 1))
    final_g = norm_final_g.reshape(1, D)

    q, v, ff, fb, gg = _hgrn_proj(x_parts, mods[0], norm_mix_g[0:1], hgrn_w_in[0])
    s0 = state_hgrn[:, 0]
    o_f, s_f = _gla_fwd(q, v, ff, hgrn_lb_logits, s0)
    a, new_state = _gla_bwd(q, v, fb, hgrn_lb_logits, s0, o_f, gg, hgrn_norm_g[0:1], s_f)
    slot, x1, ys = _moe(a, x_parts, mods[0], norm_ffn_g[0:1], hgrn_w_out[0], rwt, rbt,
                        moe_w_gate, moe_w_up, moe_w_down, 0, final_g, final=False)

    x, a = _combine_conv(slot, x1, mods[0], ys, mods[1], norm_mix_g[1:2], conv_w_in[0], conv_w[0])
    y_p, y_s = _moe(a, (x,), mods[1], norm_ffn_g[1:2], conv_w_out[0], rwt, rbt,
                    moe_w_gate, moe_w_up, moe_w_down, 1, final_g, final=True)

    return (y_p.reshape(N_PROMPT_SEQ, PROMPT_LEN, D), y_s.reshape(N_SAMPLE_SEQ, SAMPLE_LEN, D),
            new_state[:, None])
```

```python
import functools

import jax
import jax.numpy as jnp
from jax import lax
from jax.experimental import pallas as pl
from jax.experimental.pallas import tpu as pltpu

F32 = jnp.float32
BF16 = jnp.bfloat16

D = 1024
N_PROMPT_SEQ = 16
PROMPT_LEN = 256
N_SAMPLE_SEQ = 2
SAMPLE_LEN = 4096
GRID_W = 64
T_PROMPT = N_PROMPT_SEQ * PROMPT_LEN
T_SAMPLE = N_SAMPLE_SEQ * SAMPLE_LEN
T = T_PROMPT + T_SAMPLE
N_COND = 1 + N_SAMPLE_SEQ
COND_ROWS = 8

HEADS = 8
DK = 128
DV = 128
CHUNK = 64
N_EXPERTS = 16
N_GROUPS = 4
EPG = 4
N_PAIRS = 6
N_BUCKETS = N_GROUPS * N_PAIRS
D_FF = 512
EPS = 1e-6

TM = 256
N_TILES = T // TM
PROMPT_TILES = T_PROMPT // TM
TILES_PER_SAMPLE = SAMPLE_LEN // TM
CHUNKS_PER_TILE = TM // CHUNK
STEP_TILES = 2
STEP_ROWS = STEP_TILES * TM
N_STEPS = N_TILES // STEP_TILES
PROMPT_STEPS = PROMPT_TILES // STEP_TILES
MOVE_TILES = 2
MOVE_ROWS = MOVE_TILES * TM
N_MOVE_STEPS = N_TILES // MOVE_TILES
ROUTE_TILES = 4
CONV_TILES = 4
TM_E = 288
N_TILES_E = T // TM_E + N_BUCKETS
S_ROWS = N_TILES_E * TM_E
ZERO_ROWS = 32
XW = D + 128
LANES = 128
MOD_TN = 3072

V7X_VMEM_BYTES = 64 * 1024 * 1024
VMEM_LIMIT = V7X_VMEM_BYTES * 7 // 8

_SLOT_A = (0, 3, 3, 0, 0, 1)
_SLOT_B = (1, 1, 2, 2, 3, 2)
_SLOT_A_IS_HIGHER = tuple(int(a > b) for a, b in zip(_SLOT_A, _SLOT_B))


def _cparams():
    return pltpu.CompilerParams(dimension_semantics=("arbitrary",), vmem_limit_bytes=VMEM_LIMIT)


def _cond_of_tile(t):
    return jnp.where(t < PROMPT_TILES, 0, 1 + (t - PROMPT_TILES) // TILES_PER_SAMPLE)


def _rmsnorm(x, g):
    ms = jnp.mean(x * x, axis=-1, keepdims=True)
    return (x * lax.rsqrt(ms + EPS)) * g


def _modulate(x, g, shift, scale):
    return _rmsnorm(x, g) * (1.0 + scale) + shift


def _silu(x):
    return x * jax.nn.sigmoid(x)


def _dot(a, b):
    return jnp.dot(a, b, preferred_element_type=F32)


def _dot_nt(a, b):
    return lax.dot_general(a, b, (((1,), (1,)), ((), ())), preferred_element_type=F32)


def _dot_tn(a, b):
    return lax.dot_general(a, b, (((0,), (0,)), ((), ())), preferred_element_type=F32)


def _split_bf16(x):
    hi = x.astype(BF16)
    lo = (x - hi.astype(F32)).astype(BF16)
    return hi, lo


def _mod_kernel(cond_ref, w_ref, b_ref, o_ref):
    s = _silu(cond_ref[...])
    o_ref[0] = _dot(s.astype(BF16), w_ref[0].astype(BF16)) + b_ref[0]


def _adaln(cond, w_mod, b_mod):
    depth = w_mod.shape[0]
    return pl.pallas_call(
        _mod_kernel,
        grid=(depth, 6 * D // MOD_TN),
        in_specs=[
            pl.BlockSpec((COND_ROWS, D), lambda i, j: (0, 0)),
            pl.BlockSpec((1, D, MOD_TN), lambda i, j: (i, 0, j)),
            pl.BlockSpec((1, 1, MOD_TN), lambda i, j: (i, 0, j)),
        ],
        out_specs=pl.BlockSpec((1, COND_ROWS, MOD_TN), lambda i, j: (i, 0, j)),
        out_shape=jax.ShapeDtypeStruct((depth, COND_ROWS, 6 * D), F32),
        compiler_params=pltpu.CompilerParams(
            dimension_semantics=("arbitrary", "arbitrary"), vmem_limit_bytes=VMEM_LIMIT),
        name="adaln",
    )(cond, w_mod, b_mod.reshape(depth, 1, 6 * D))


_ROW_SPEC = pl.BlockSpec((1, D), lambda i: (0, 0))
_STEP_MOD_SPEC = pl.BlockSpec((1, 6, 1, D), lambda i, *_: (_cond_of_tile(i * STEP_TILES), 0, 0, 0))
_STEP_SPEC = pl.BlockSpec((STEP_ROWS, D), lambda i, *_: (i, 0))
_SUB_ROWS = tuple(slice(s * TM, (s + 1) * TM) for s in range(STEP_TILES))


def _load_weight_bf16(w_hbm, w16_ref, stage_ref, sem):
    n = w16_ref.shape[1] // D

    def slab(c):
        return pltpu.make_async_copy(w_hbm.at[:, pl.ds(c * D, D)], stage_ref.at[c % 2], sem.at[c % 2])

    slab(0).start()
    for c in range(n):
        if c + 1 < n:
            slab(c + 1).start()
        slab(c).wait()
        w16_ref[:, c * D:(c + 1) * D] = stage_ref[c % 2].astype(BF16)


def _weight_scratch(n_slabs):
    n_stage = min(n_slabs, 2)
    return [pltpu.VMEM((D, n_slabs * D), BF16), pltpu.VMEM((n_stage, D, D), F32),
            pltpu.SemaphoreType.DMA((n_stage,))]


_ANY_SPEC = pl.BlockSpec(memory_space=pl.ANY)


def _token_tile(x_refs, rows=slice(None), prompt_steps=PROMPT_TILES):
    if len(x_refs) == 1:
        return x_refs[0][rows]
    return jnp.where(pl.program_id(0) < prompt_steps, x_refs[0][rows], x_refs[1][rows])


def _token_specs(x_parts, tiles_per_step=1):
    rows = tiles_per_step * TM
    if len(x_parts) == 1:
        return [pl.BlockSpec((rows, D), lambda i: (i, 0))]
    prompt_steps = T_PROMPT // rows
    return [pl.BlockSpec((rows, D), lambda i: (jnp.minimum(i, prompt_steps - 1), 0)),
            pl.BlockSpec((rows, D), lambda i: (jnp.maximum(i - prompt_steps, 0), 0))]


def _hgrn_proj_kernel(xp_ref, xs_ref, mod_ref, g_ref, w_hbm, q_ref, v_ref, ff_ref, fb_ref, gg_ref,
                      w_ref, stage_ref, sem):
    @pl.when(pl.program_id(0) == 0)
    def _():
        _load_weight_bf16(w_hbm, w_ref, stage_ref, sem)

    h = [_modulate(_token_tile((xp_ref, xs_ref), rows, PROMPT_STEPS),
                   g_ref[...], mod_ref[0, 0], mod_ref[0, 1]).astype(BF16) for rows in _SUB_ROWS]
    for rows, h_tile in zip(_SUB_ROWS, h):
        q_ref[rows] = (_dot(h_tile, w_ref[:, 0 * D:1 * D]) * (DK ** -0.5)).astype(BF16)
        v_ref[rows] = _dot(h_tile, w_ref[:, 1 * D:2 * D]).astype(BF16)
        ff_ref[rows] = _dot(h_tile, w_ref[:, 2 * D:3 * D])
        fb_ref[rows] = _dot(h_tile, w_ref[:, 3 * D:4 * D])
        gg_ref[rows] = _dot(h_tile, w_ref[:, 4 * D:5 * D]).astype(BF16)


def _hgrn_proj(x_parts, mod, norm_g, w_in):
    bf = jax.ShapeDtypeStruct((T, D), BF16)
    f32 = jax.ShapeDtypeStruct((T, D), F32)
    return pl.pallas_call(
        _hgrn_proj_kernel,
        grid=(N_STEPS,),
        in_specs=_token_specs(x_parts, STEP_TILES) + [_STEP_MOD_SPEC, _ROW_SPEC, _ANY_SPEC],
        out_specs=[_STEP_SPEC] * 5,
        out_shape=[bf, bf, f32, f32, bf],
        scratch_shapes=_weight_scratch(5),
        compiler_params=_cparams(),
        name="hgrn_proj",
    )(*x_parts, mod, norm_g, w_in)


GLA_TILES = 4
GLA_ROWS = GLA_TILES * TM
N_GLA_STEPS = N_TILES // GLA_TILES
GLA_PROMPT_STEPS = PROMPT_TILES // GLA_TILES
STEPS_PER_SAMPLE = SAMPLE_LEN // GLA_ROWS
_GLA_SPEC = pl.BlockSpec((GLA_ROWS, D), lambda i: (i, 0))
_GLA_TILE_ROWS = tuple(slice(s * TM, (s + 1) * TM) for s in range(GLA_TILES))


def _gla_step(block, q_ref, v_ref, f_ref, lbl_ref, s0_ref, st_ref, st_tiles_ref, ma_ref, o_ref,
              *, reverse, finish=None):
    is_prompt = block < GLA_PROMPT_STEPS
    rel = block - GLA_PROMPT_STEPS
    edge = (STEPS_PER_SAMPLE - 1) if reverse else 0
    sample_start = jnp.logical_and(rel >= 0, rel % STEPS_PER_SAMPLE == edge)

    @pl.when(is_prompt)
    def _():
        st_ref[...] = jnp.zeros_like(st_ref)

    @pl.when(sample_start)
    def _():
        for h in range(HEADS):
            st_ref[h] = s0_ref[0, 0, h].T

    logits = lbl_ref[...]
    e = jnp.exp(logits - jnp.max(logits, axis=0, keepdims=True))
    lb = e[0:1] / jnp.sum(e, axis=0, keepdims=True)

    ref_idx = (CHUNK - 1 - CHUNK // 2) if reverse else CHUNK // 2
    last_idx = 0 if reverse else CHUNK - 1
    shift = CHUNK.bit_length() - 1

    row = lax.broadcasted_iota(jnp.int32, (TM, TM), 0)
    col = lax.broadcasted_iota(jnp.int32, (TM, TM), 1)
    same = (row >> shift) == (col >> shift)
    ref_row = ((row >> shift) << shift) + ref_idx
    keep = jnp.logical_and(same, (row <= col) if reverse else (row >= col))

    @pl.when(pl.program_id(0) == 0)
    def _():
        at_ref = jnp.logical_and(same, (ref_row <= col) if reverse else (ref_row >= col))
        ma_ref[...] = (keep.astype(F32) - at_ref.astype(F32)).astype(BF16)

    ma = ma_ref[...]

    rr = lax.broadcasted_iota(jnp.int32, (2 * CHUNKS_PER_TILE, TM), 0)
    cc = lax.broadcasted_iota(jnp.int32, (2 * CHUNKS_PER_TILE, TM), 1)
    chunk = rr & (CHUNKS_PER_TILE - 1)
    target = chunk * CHUNK + jnp.where(rr < CHUNKS_PER_TILE, ref_idx, last_idx)
    covered = (cc >= target) if reverse else (cc <= target)
    rm = jnp.logical_and((cc >> shift) == chunk, covered).astype(BF16)
    ma_rm = jnp.concatenate([ma, rm, jnp.zeros_like(rm)], axis=0)

    chunk_rows = [slice(c * CHUNK, (c + 1) * CHUNK) for c in range(CHUNKS_PER_TILE)]
    order = range(CHUNKS_PER_TILE - 1, -1, -1) if reverse else range(CHUNKS_PER_TILE)
    pair_w = 2 * DK
    n_pairs = HEADS // 2
    tile_order = _GLA_TILE_ROWS[::-1] if reverse else _GLA_TILE_ROWS

    def per_chunk_scale(x, scale):
        return jnp.concatenate([x[rs] * scale[c:c + 1] for c, rs in enumerate(chunk_rows)], axis=0)

    def pair_cols(p):
        return slice(p * pair_w, (p + 1) * pair_w)

    def stage_decay(k, p):
        lbp = lb[:, pair_cols(p)]
        f = lbp + (1.0 - lbp) * jax.nn.sigmoid(f_ref[tile_order[k], pair_cols(p)])
        lf_hi, lf_lo = _split_bf16(jnp.log(f))
        both = _dot(ma_rm, lf_hi) + _dot(ma_rm, lf_lo)
        z, marks = both[:TM], both[TM:TM + 2 * CHUNKS_PER_TILE]
        return 1.0 - f, z, marks

    def stage_scores(k, p, kk, z, marks):
        ref, last = marks[:CHUNKS_PER_TILE], marks[CHUNKS_PER_TILE:]
        qa32 = q_ref[tile_order[k], pair_cols(p)].astype(F32) * jnp.exp(z)
        kb32 = kk * jnp.exp(-z)
        qa = qa32.astype(BF16)
        kb = kb32.astype(BF16)
        qe = per_chunk_scale(qa32, jnp.exp(ref)).astype(BF16)
        kd = per_chunk_scale(kb32, jnp.exp(last - ref)).astype(BF16)
        v = v_ref[tile_order[k], pair_cols(p)]
        scores, incr = [], []
        for hh in range(2):
            ls = slice(hh * DK, (hh + 1) * DK)
            scores.append(_dot_nt(qa[:, ls], kb[:, ls]))
            kd_h = kd[:, ls]
            kd_blocks = jnp.concatenate(
                [jnp.concatenate([part for part in (jnp.zeros((c * CHUNK, DK), BF16), kd_h[rs],
                                                    jnp.zeros((TM - (c + 1) * CHUNK, DK), BF16))
                                  if part.shape[0]], axis=0)
                 for c, rs in enumerate(chunk_rows)], axis=1)
            incr.append(_dot_tn(v[:, ls], kd_blocks))
        return scores, incr, qe, jnp.exp(last), v

    def stage_output(k, p, scores, incr, qe, decay, v):
        base = tile_order[k].start
        for hh in range(2):
            h = 2 * p + hh
            ls = slice(hh * DK, (hh + 1) * DK)
            hs = slice(h * DK, (h + 1) * DK)
            o_intra = _dot(jnp.where(keep, scores[hh], 0.0).astype(BF16), v[:, ls])
            st = st_ref[h]
            if k > 0:
                st = jnp.where(is_prompt, 0.0, st)
            for c in order:
                rs = chunk_rows[c]
                o_ref[base + rs.start:base + rs.stop, hs] = (
                    o_intra[rs] + _dot_nt(qe[rs, ls], st.astype(BF16)))
                st = st * decay[c:c + 1, ls] + incr[hh][:, c * DK:(c + 1) * DK]
            st_ref[h] = st
            if k < GLA_TILES - 1:
                st_tiles_ref[k, h] = st
        if finish is not None:
            return (tile_order[k], p)

    def stage_finish(k, p, rows, pair):
        finish(rows, pair)

    units = [(k, p) for k in range(GLA_TILES) for p in range(n_pairs)]
    stages = [stage_decay, stage_scores, stage_output] + ([stage_finish] if finish is not None else [])
    carried = [dict() for _ in stages]
    for step in range(len(units) + len(stages) - 1):
        for depth, stage in (enumerate(stages) if reverse else reversed(list(enumerate(stages)))):
            u = step - depth
            if 0 <= u < len(units):
                out = stage(*units[u], *(carried[depth].pop(u) if depth else ()))
                if depth + 1 < len(stages):
                    carried[depth + 1][u] = out


def _store_prompt_states(block, st_ref, st_tiles_ref, snew_ref, *, reverse, sfwd_ref=None):
    @pl.when(block < GLA_PROMPT_STEPS)
    def _():
        for k in range(GLA_TILES):
            tile = GLA_TILES - 1 - k if reverse else k
            for h in range(HEADS):
                state = (st_tiles_ref[k, h] if k < GLA_TILES - 1 else st_ref[h]).T
                if sfwd_ref is None:
                    snew_ref[tile, h] = state
                else:
                    snew_ref[tile, 1, h] = state
        if sfwd_ref is not None:
            snew_ref[:, 0] = sfwd_ref[...]


def _gla_fwd_kernel(q_ref, v_ref, f_ref, lbl_ref, s0_ref, o_ref, snew_ref, st_ref, st_tiles_ref, ma_ref):
    block = pl.program_id(0)
    _gla_step(block, q_ref, v_ref, f_ref, lbl_ref, s0_ref, st_ref, st_tiles_ref, ma_ref, o_ref,
              reverse=False)
    _store_prompt_states(block, st_ref, st_tiles_ref, snew_ref, reverse=False)


def _gla_bwd_kernel(q_ref, v_ref, f_ref, lbl_ref, s0_ref, of_ref, gg_ref, ng_ref, sfwd_ref,
                    a_ref, snew_ref, st_ref, st_tiles_ref, ma_ref, ob_ref):
    block = N_GLA_STEPS - 1 - pl.program_id(0)

    def finish(rows, p):
        for h in (2 * p, 2 * p + 1):
            hs = slice(h * DV, (h + 1) * DV)
            o = of_ref[rows, hs] + ob_ref[rows, hs]
            a_ref[rows, hs] = (_rmsnorm(o, ng_ref[...]) * _silu(gg_ref[rows, hs].astype(F32))).astype(BF16)

    _gla_step(block, q_ref, v_ref, f_ref, lbl_ref, s0_ref, st_ref, st_tiles_ref, ma_ref, ob_ref,
              reverse=True, finish=finish)
    _store_prompt_states(block, st_ref, st_tiles_ref, snew_ref, reverse=True, sfwd_ref=sfwd_ref)


def _state_specs(direction, block_of_step):
    sample = lambda i: jnp.clip((block_of_step(i) - GLA_PROMPT_STEPS) // STEPS_PER_SAMPLE, 0, N_SAMPLE_SEQ - 1)
    s0 = pl.BlockSpec((1, 1, HEADS, DK, DV), lambda i: (sample(i), direction, 0, 0, 0))
    snew = pl.BlockSpec((GLA_TILES, HEADS, DK, DV),
                        lambda i: (jnp.minimum(block_of_step(i), GLA_PROMPT_STEPS - 1), 0, 0, 0))
    return s0, snew


_GLA_STATE_SCRATCH = [pltpu.VMEM((HEADS, DV, DK), F32), pltpu.VMEM((GLA_TILES - 1, HEADS, DV, DK), F32),
                      pltpu.VMEM((TM, TM), BF16)]


def _gla_fwd(q, v, ff, lb_logits, s0):
    s0_spec, snew_spec = _state_specs(0, lambda i: i)
    return pl.pallas_call(
        _gla_fwd_kernel,
        grid=(N_GLA_STEPS,),
        in_specs=[_GLA_SPEC, _GLA_SPEC, _GLA_SPEC,
                  pl.BlockSpec(lb_logits.shape, lambda i: (0, 0)), s0_spec],
        out_specs=[_GLA_SPEC, snew_spec],
        out_shape=[jax.ShapeDtypeStruct((T, D), F32),
                   jax.ShapeDtypeStruct((N_PROMPT_SEQ, HEADS, DK, DV), F32)],
        scratch_shapes=_GLA_STATE_SCRATCH,
        compiler_params=_cparams(),
        name="gla_fwd",
    )(q, v, ff, lb_logits, s0)


def _gla_bwd(q, v, fb, lb_logits, s0, o_f, gg, head_norm_g, s_fwd):
    rev = lambda i: N_GLA_STEPS - 1 - i
    block = pl.BlockSpec((GLA_ROWS, D), lambda i: (rev(i), 0))
    s0_spec, sfwd_spec = _state_specs(1, rev)
    prompt_block = lambda i: jnp.minimum(rev(i), GLA_PROMPT_STEPS - 1)
    return pl.pallas_call(
        _gla_bwd_kernel,
        grid=(N_GLA_STEPS,),
        in_specs=[block, block, block, pl.BlockSpec(lb_logits.shape, lambda i: (0, 0)), s0_spec,
                  block, block, pl.BlockSpec((1, DV), lambda i: (0, 0)), sfwd_spec],
        out_specs=[block, pl.BlockSpec((GLA_TILES, 2, HEADS, DK, DV), lambda i: (prompt_block(i), 0, 0, 0, 0))],
        out_shape=[jax.ShapeDtypeStruct((T, D), BF16),
                   jax.ShapeDtypeStruct((N_PROMPT_SEQ, 2, HEADS, DK, DV), F32)],
        scratch_shapes=_GLA_STATE_SCRATCH + [pltpu.VMEM((GLA_ROWS, D), F32)],
        compiler_params=_cparams(),
        name="gla_bwd",
    )(q, v, fb, lb_logits, s0, o_f, gg, head_norm_g, s_fwd)


def _conv_mixer_kernel(x_ref, mod_ref, g_ref, w_hbm, cw_ref, a_ref, w_ref, stage_ref, sem):
    @pl.when(pl.program_id(0) == 0)
    def _():
        _load_weight_bf16(w_hbm, w_ref, stage_ref, sem)

    seg = jnp.where(pl.program_id(0) < PROMPT_TILES // CONV_TILES, PROMPT_LEN, GRID_W)
    pos = lax.broadcasted_iota(jnp.int32, (TM, 1), 0) & (seg - 1)

    def project(rows):
        h = _modulate(x_ref[rows], g_ref[...], mod_ref[0, 0], mod_ref[0, 1]).astype(BF16)
        return [_dot(h, w_ref[:, k * D:(k + 1) * D]) for k in range(3)]

    def convolve(rows, bg, cg, x_in):
        u = cg * x_in
        prev = jnp.where(pos == 0, 0.0, pltpu.roll(u, 1, axis=0))
        nxt = jnp.where(pos == seg - 1, 0.0, pltpu.roll(u, TM - 1, axis=0))
        conv = cw_ref[0:1] * prev + cw_ref[1:2] * u + cw_ref[2:3] * nxt
        a_ref[rows] = (bg * conv).astype(BF16)

    tiles = [slice(s * TM, (s + 1) * TM) for s in range(CONV_TILES)]
    projected = [project(rows) for rows in tiles]
    for rows, parts in zip(tiles, projected):
        convolve(rows, *parts)


def _conv_mixer(x, mod, norm_g, w_in, conv_w):
    block = pl.BlockSpec((CONV_TILES * TM, D), lambda i: (i, 0))
    return pl.pallas_call(
        _conv_mixer_kernel,
        grid=(N_TILES // CONV_TILES,),
        in_specs=[block, pl.BlockSpec((1, 6, 1, D), lambda i: (_cond_of_tile(i * CONV_TILES), 0, 0, 0)),
                  _ROW_SPEC, _ANY_SPEC, pl.BlockSpec((3, D), lambda i: (0, 0))],
        out_specs=block,
        out_shape=jax.ShapeDtypeStruct((T, D), BF16),
        scratch_shapes=_weight_scratch(3),
        compiler_params=_cparams(),
        name="conv_mixer",
    )(x, mod, norm_g, w_in, conv_w)


ROUTE_ROWS = 32


def _first_member(vals, target):
    idx = jnp.full_like(target, float(EPG - 1))
    for j in range(EPG - 2, -1, -1):
        idx = jnp.where(vals[j] == target, float(j), idx)
    return idx


def _router_logits(h2, rwt_ref):
    h_hi, h_lo = _split_bf16(h2)
    w_hi, w_lo = _split_bf16(rwt_ref[...])
    with_hi = _dot_nt(jnp.concatenate([w_hi, w_lo], axis=0), h_hi)
    return with_hi[:N_EXPERTS] + (_dot_nt(w_hi, h_lo) + with_hi[N_EXPERTS:])


def _select_experts(logits, rb_ref):
    scores = jax.nn.sigmoid(logits)
    sel = scores + rb_ref[...]
    member = [sel[j * N_GROUPS:(j + 1) * N_GROUPS] for j in range(EPG)]
    m1 = functools.reduce(jnp.maximum, member)
    i1 = _first_member(member, m1)
    rest = [jnp.where(i1 == float(j), -jnp.inf, member[j]) for j in range(EPG)]
    m2 = functools.reduce(jnp.maximum, rest)
    i2 = _first_member(rest, m2)
    group_score = m1 + m2

    best, grp, first, second = group_score[0:1], jnp.zeros((1, TM), F32), i1[0:1], i2[0:1]
    for g in range(1, N_GROUPS):
        upd = group_score[g:g + 1] > best
        best = jnp.where(upd, group_score[g:g + 1], best)
        grp = jnp.where(upd, float(g), grp)
        first = jnp.where(upd, i1[g:g + 1], first)
        second = jnp.where(upd, i2[g:g + 1], second)
    a = jnp.minimum(first, second)
    b = jnp.maximum(first, second)
    row = lax.broadcasted_iota(jnp.int32, (N_EXPERTS, TM), 0).astype(F32)
    s_lo = jnp.sum(jnp.where(row == a * N_GROUPS + grp, scores, 0.0), axis=0, keepdims=True)
    s_hi = jnp.sum(jnp.where(row == b * N_GROUPS + grp, scores, 0.0), axis=0, keepdims=True)
    tot = s_lo + s_hi
    pair = jnp.where(a == 0.0, jnp.where(b == 1.0, 0.0, jnp.where(b == 2.0, 3.0, 4.0)),
                     jnp.where(a == 1.0, jnp.where(b == 2.0, 5.0, 1.0), 2.0))
    return grp * N_PAIRS + pair, s_lo / tot, s_hi / tot


def _post_mixer_kernel(*refs, n_x):
    a_ref, x_refs = refs[0], refs[1:1 + n_x]
    (mod_ref, g2_ref, wo_hbm, rwt_ref, rb_ref,
     x1_ref, gates_ref, bucket_ref, rank_ref, cnt_ref, carry_ref, earlier_ref, wo_ref, stage_ref, sem) = refs[1 + n_x:]

    @pl.when(pl.program_id(0) == 0)
    def _():
        _load_weight_bf16(wo_hbm, wo_ref, stage_ref, sem)
        carry_ref[...] = jnp.zeros_like(carry_ref)
        s_idx = lax.broadcasted_iota(jnp.int32, (TM, TM), 0)
        t_idx = lax.broadcasted_iota(jnp.int32, (TM, TM), 1)
        earlier_ref[...] = (s_idx < t_idx).astype(BF16)

    def stage_mix(s):
        rows = slice(s * TM, (s + 1) * TM)
        x1 = (_token_tile(x_refs, rows, PROMPT_TILES // ROUTE_TILES)
              + mod_ref[0, 2] * _dot(a_ref[rows], wo_ref[...]))
        x1_ref[rows] = x1
        return _modulate(x1, g2_ref[...], mod_ref[0, 3], mod_ref[0, 4])

    def stage_rank(s, bucket, g_lo, g_hi):
        onehot = lax.broadcasted_iota(jnp.int32, (ROUTE_ROWS, TM), 0).astype(F32) == bucket
        before = _dot(onehot.astype(BF16), earlier_ref[...])
        oh = onehot.astype(F32)
        carry = carry_ref[...]
        rank = jnp.sum(oh * (before + carry[:, 0:1]), axis=0, keepdims=True)
        carry_ref[...] = carry + jnp.sum(oh, axis=1, keepdims=True)
        bucket_ref[s] = bucket.astype(jnp.int32)
        rank_ref[s] = rank.astype(jnp.int32)
        grow = lax.broadcasted_iota(jnp.int32, (LANES, TM), 0)
        gates_ref[s * TM:(s + 1) * TM] = jnp.where(grow == 0, g_lo, jnp.where(grow == 1, g_hi, 0.0)).T

    tiles = range(ROUTE_TILES)
    h2 = [stage_mix(s) for s in tiles]
    logits = [_router_logits(h2[s], rwt_ref) for s in tiles]
    picks = [_select_experts(logits[s], rb_ref) for s in tiles]
    for s in tiles:
        stage_rank(s, *picks[s])
    cnt_ref[...] = carry_ref[...]


def _post_mixer(a, x_parts, mod, norm_g2, w_out, router_wt, router_bt):
    rows = ROUTE_TILES * TM
    idx_spec = pl.BlockSpec((ROUTE_TILES, 1, TM), lambda i: (i, 0, 0))
    return pl.pallas_call(
        functools.partial(_post_mixer_kernel, n_x=len(x_parts)),
        grid=(N_TILES // ROUTE_TILES,),
        in_specs=[pl.BlockSpec((rows, D), lambda i: (i, 0))] + _token_specs(x_parts, ROUTE_TILES) + [
            pl.BlockSpec((1, 6, 1, D), lambda i: (_cond_of_tile(i * ROUTE_TILES), 0, 0, 0)), _ROW_SPEC,
            _ANY_SPEC,
            pl.BlockSpec((N_EXPERTS, D), lambda i: (0, 0)),
            pl.BlockSpec((N_EXPERTS, 1), lambda i: (0, 0))],
        out_specs=[pl.BlockSpec((rows, D), lambda i: (i, 0)), pl.BlockSpec((rows, LANES), lambda i: (i, 0)),
                   idx_spec, idx_spec, pl.BlockSpec((ROUTE_ROWS, LANES), lambda i: (0, 0))],
        out_shape=[jax.ShapeDtypeStruct((T, D), F32), jax.ShapeDtypeStruct((T, LANES), F32),
                   jax.ShapeDtypeStruct((N_TILES, 1, TM), jnp.int32),
                   jax.ShapeDtypeStruct((N_TILES, 1, TM), jnp.int32),
                   jax.ShapeDtypeStruct((ROUTE_ROWS, LANES), F32)],
        scratch_shapes=[pltpu.VMEM((ROUTE_ROWS, LANES), F32), pltpu.VMEM((TM, TM), BF16)] + _weight_scratch(1),
        compiler_params=_cparams(),
        name="post_mixer",
    )(a, *x_parts, mod, norm_g2, w_out, router_wt, router_bt)


_MOVE_SPEC = pl.BlockSpec((MOVE_ROWS, D), lambda i, *_: (i, 0))
_MOVE_MOD_SPEC = pl.BlockSpec((1, 6, 1, D), lambda i, *_: (_cond_of_tile(i * MOVE_TILES), 0, 0, 0))


def _smem_step_spec(step_of_step):
    return pl.BlockSpec((MOVE_TILES, 1, TM), lambda i, *_: (step_of_step(i), 0, 0), memory_space=pltpu.SMEM)


def _start_step_rows(make_copy):
    for tile in range(MOVE_TILES):
        for r in range(TM):
            make_copy(tile, r).start(priority=r % 2)


def _dispatch_kernel(zero_from_ref, slot_ref, x1_ref, mod_ref, g2_ref, gates_ref,
                     xs_ref, buf_ref, zero_ref, sems, zero_sem):
    i = pl.program_id(0)
    cur = i % 2

    @pl.when(i == 0)
    def _():
        zero_ref[...] = jnp.zeros_like(zero_ref)

        def for_each_chunk(act):
            def tile(j, carry):
                def chunk(c, inner):
                    row = pl.multiple_of(j * TM_E + c * ZERO_ROWS, ZERO_ROWS)
                    act(pltpu.make_async_copy(zero_ref, xs_ref.at[pl.ds(row, ZERO_ROWS)], zero_sem))
                    return inner
                return lax.fori_loop(zero_from_ref[j], TM_E // ZERO_ROWS, chunk, carry)
            lax.fori_loop(0, N_TILES_E, tile, 0)

        for_each_chunk(lambda copy: copy.start())
        for_each_chunk(lambda copy: copy.wait())

    def wait_step(b):
        pltpu.make_async_copy(buf_ref.at[b], xs_ref.at[pl.ds(0, MOVE_ROWS)], sems.at[b]).wait()

    h2 = _modulate(x1_ref[...], g2_ref[...], mod_ref[0, 3], mod_ref[0, 4])
    gates = gates_ref[...]

    def send(b):
        buf_ref[b, :, :D] = h2
        buf_ref[b, :, D:] = gates
        _start_step_rows(lambda tile, r: pltpu.make_async_copy(
            buf_ref.at[b, pl.ds(tile * TM + r, 1)], xs_ref.at[pl.ds(slot_ref[tile, 0, r], 1)], sems.at[b]))

        @pl.when(i > 0)
        def _():
            wait_step(1 - b)

        @pl.when(i == N_MOVE_STEPS - 1)
        def _():
            wait_step(b)

    for b in range(2):
        pl.when(cur == b)(functools.partial(send, b))


def _dispatch(zero_from, slot, x1, mod, norm_g2, gates):
    return pl.pallas_call(
        _dispatch_kernel,
        grid_spec=pltpu.PrefetchScalarGridSpec(
            num_scalar_prefetch=1,
            grid=(N_MOVE_STEPS,),
            in_specs=[_smem_step_spec(lambda i: i), _MOVE_SPEC, _MOVE_MOD_SPEC,
                      pl.BlockSpec((1, D), lambda i, *_: (0, 0)),
                      pl.BlockSpec((MOVE_ROWS, LANES), lambda i, *_: (i, 0))],
            out_specs=pl.BlockSpec(memory_space=pl.ANY),
            scratch_shapes=[pltpu.VMEM((2, MOVE_ROWS, XW), F32), pltpu.VMEM((ZERO_ROWS, XW), F32),
                            pltpu.SemaphoreType.DMA((2,)), pltpu.SemaphoreType.DMA(())],
        ),
        out_shape=jax.ShapeDtypeStruct((S_ROWS, XW), F32),
        compiler_params=_cparams(),
        name="moe_dispatch",
    )(zero_from, slot, x1, mod, norm_g2, gates)


_W_SHAPES = ((D, D_FF), (D, D_FF), (D_FF, D))


def _expert_kernel(ea_ref, eb_ref, next_a_ref, next_b_ref, a_higher_ref, nrows_ref, xblock_ref,
                   x_ref, wg_hbm, wu_hbm, wd_hbm, y_ref, *scratch, layer):
    stage, w16, sems = scratch[0:3], scratch[3:6], scratch[6]
    j = pl.program_id(0)
    n = nrows_ref[j]
    prev = jnp.maximum(j - 1, 0)

    def fetch(slot, expert):
        return [pltpu.make_async_copy(w.at[layer, expert], s.at[slot], sems.at[slot])
                for w, s in zip((wg_hbm, wu_hbm, wd_hbm), stage)]

    for slot, e_ref, next_ref in ((0, ea_ref, next_a_ref), (1, eb_ref, next_b_ref)):
        @pl.when(j == 0)
        def _(slot=slot, e_ref=e_ref):
            for copy in fetch(slot, e_ref[0]):
                copy.start()

        @pl.when(jnp.logical_or(j == 0, e_ref[j] != e_ref[prev]))
        def _(slot=slot, e_ref=e_ref, next_ref=next_ref):
            for copy in fetch(slot, e_ref[j]):
                copy.wait()
            for s, d in zip(stage, w16):
                d[slot] = s[slot].astype(BF16)

            @pl.when(next_ref[j] >= 0)
            def _():
                for copy in fetch(slot, next_ref[j]):
                    copy.start(priority=1)

    def run(rows):
        valid = lax.broadcasted_iota(jnp.int32, (rows, 1), 0) < n
        xe = jnp.where(valid, x_ref[:rows], 0.0)
        x = xe[:, :D].astype(BF16)
        g_lo, g_hi = xe[:, D:D + 1], xe[:, D + 1:D + 2]
        a_higher = a_higher_ref[j] != 0

        def ffn(slot, gate):
            wg_ref, wu_ref, wd_ref = w16
            act = _silu(_dot(x, wg_ref[slot])) * _dot(x, wu_ref[slot]) * gate
            return _dot(act.astype(BF16), wd_ref[slot])

        y_ref[:rows] = (ffn(0, jnp.where(a_higher, g_hi, g_lo))
                        + ffn(1, jnp.where(a_higher, g_lo, g_hi)))
        if rows < TM_E:
            y_ref[rows:] = jnp.zeros((TM_E - rows, D), F32)

    half = TM_E // 2
    pl.when(n > half)(functools.partial(run, TM_E))
    pl.when(jnp.logical_and(n > 0, n <= half))(functools.partial(run, half))

    @pl.when(n == 0)
    def _():
        y_ref[...] = jnp.zeros_like(y_ref)


def _experts(tile_ea, tile_eb, tile_next_a, tile_next_b, tile_a_higher, tile_rows, tile_xblock, xs,
             w_gate, w_up, w_down, layer):
    any_spec = pl.BlockSpec(memory_space=pl.ANY)
    return pl.pallas_call(
        functools.partial(_expert_kernel, layer=layer),
        grid_spec=pltpu.PrefetchScalarGridSpec(
            num_scalar_prefetch=7,
            grid=(N_TILES_E,),
            in_specs=[pl.BlockSpec((TM_E, XW), lambda j, *plan: (plan[-1][j], 0)), any_spec, any_spec, any_spec],
            out_specs=pl.BlockSpec((TM_E, D), lambda j, *_: (j, 0)),
            scratch_shapes=[pltpu.VMEM((2,) + s, F32) for s in _W_SHAPES]
                           + [pltpu.VMEM((2,) + s, BF16) for s in _W_SHAPES]
                           + [pltpu.SemaphoreType.DMA((2,))],
        ),
        out_shape=jax.ShapeDtypeStruct((S_ROWS, D), F32),
        compiler_params=_cparams(),
        name="moe_experts",
    )(tile_ea, tile_eb, tile_next_a, tile_next_b, tile_a_higher, tile_rows, tile_xblock, xs,
      w_gate, w_up, w_down)


def _combine_kernel(slot_ref, next_slot_ref, x1_ref, mod_ref, gf_ref, y_ref, *rest, final):
    out_refs, (buf_ref, sems) = rest[:-2], rest[-2:]
    i = pl.program_id(0)
    cur = i % 2

    def gather(slots, b):
        _start_step_rows(lambda tile, r: pltpu.make_async_copy(
            y_ref.at[pl.ds(slots[tile, 0, r], 1)], buf_ref.at[b, pl.ds(tile * TM + r, 1)], sems.at[b]))

    @pl.when(i == 0)
    def _():
        gather(slot_ref, 0)

    def combine(b):
        @pl.when(i + 1 < N_MOVE_STEPS)
        def _():
            gather(next_slot_ref, 1 - b)

        pltpu.make_async_copy(y_ref.at[pl.ds(0, MOVE_ROWS)], buf_ref.at[b], sems.at[b]).wait()
        x2 = x1_ref[...] + mod_ref[0, 5] * buf_ref[b]
        if not final:
            out_refs[0][...] = x2
            return
        y = _rmsnorm(x2, gf_ref[...])
        yp_ref, ys_ref = out_refs

        @pl.when(i < T_PROMPT // MOVE_ROWS)
        def _():
            yp_ref[...] = y

        @pl.when(i >= T_PROMPT // MOVE_ROWS)
        def _():
            ys_ref[...] = y

    for b in range(2):
        pl.when(cur == b)(functools.partial(combine, b))


def _combine(slot, x1, mod, final_g, y_sorted, *, final):
    if final:
        out_specs = _token_specs((None, None), MOVE_TILES)
        out_shape = [jax.ShapeDtypeStruct((T_PROMPT, D), F32), jax.ShapeDtypeStruct((T_SAMPLE, D), F32)]
    else:
        out_specs = [_MOVE_SPEC]
        out_shape = [jax.ShapeDtypeStruct((T, D), F32)]
    return pl.pallas_call(
        functools.partial(_combine_kernel, final=final),
        grid=(N_MOVE_STEPS,),
        in_specs=[_smem_step_spec(lambda i: i), _smem_step_spec(lambda i: jnp.minimum(i + 1, N_MOVE_STEPS - 1)),
                  _MOVE_SPEC, _MOVE_MOD_SPEC, _ROW_SPEC, pl.BlockSpec(memory_space=pl.ANY)],
        out_specs=out_specs,
        scratch_shapes=[pltpu.VMEM((2, MOVE_ROWS, D), F32), pltpu.SemaphoreType.DMA((2,))],
        out_shape=out_shape,
        compiler_params=_cparams(),
        name="moe_combine_final" if final else "moe_combine",
    )(slot, slot, x1, mod, final_g, y_sorted)


CONV_ROWS = CONV_TILES * TM
_HALF_D = D // 2
_GATHER_GROUPS = 6 * CONV_TILES


def _combine_conv_kernel(slot_ref, next_slot_ref, x1_ref, mod0_ref, mod_ref, g_ref, w_hbm, cw_ref, y_ref,
                         x2_ref, a_ref, w_ref, stage_ref, sem, row_sem):
    i = pl.program_id(0)

    def gather(slots, lo, hi):
        for r in range(lo, hi):
            pltpu.make_async_copy(y_ref.at[pl.ds(slots[r // TM, 0, r % TM], 1)],
                                  stage_ref.at[0, pl.ds(r, 1)], row_sem.at[0]).start(priority=r % 2)

    def wait_rows():
        pltpu.make_async_copy(y_ref.at[pl.ds(0, CONV_ROWS)], stage_ref.at[0], row_sem.at[0]).wait()

    @pl.when(i == 0)
    def _():
        _load_weight_bf16(w_hbm, w_ref, stage_ref, sem)
        gather(slot_ref, 0, CONV_ROWS)

    wait_rows()
    x2_ref[...] = x1_ref[...] + mod0_ref[0, 5] * stage_ref[0]

    seg = jnp.where(i < PROMPT_TILES // CONV_TILES, PROMPT_LEN, GRID_W)
    pos = lax.broadcasted_iota(jnp.int32, (TM, 1), 0) & (seg - 1)
    units = [(slice(s * TM, (s + 1) * TM), slice(j * _HALF_D, (j + 1) * _HALF_D))
             for s in range(CONV_TILES) for j in range(2)]
    groups = [g * CONV_ROWS // _GATHER_GROUPS for g in range(_GATHER_GROUPS + 1)]
    hs = {}

    def project(u):
        rows, cols = units[u]
        if cols.start == 0:
            hs[rows.start] = _modulate(x2_ref[rows], g_ref[...], mod_ref[0, 0], mod_ref[0, 1]).astype(BF16)
        parts = []
        for k in range(3):
            gather(next_slot_ref, groups[3 * u + k], groups[3 * u + k + 1])
            parts.append(_dot(hs[rows.start], w_ref[:, k * D + cols.start:k * D + cols.stop]))
        return parts

    def convolve(u, bg, cg, x_in):
        rows, cols = units[u]
        v = cg * x_in
        prev = jnp.where(pos == 0, 0.0, pltpu.roll(v, 1, axis=0))
        nxt = jnp.where(pos == seg - 1, 0.0, pltpu.roll(v, TM - 1, axis=0))
        conv = cw_ref[0:1, cols] * prev + cw_ref[1:2, cols] * v + cw_ref[2:3, cols] * nxt
        a_ref[rows, cols] = (bg * conv).astype(BF16)

    parts = project(0)
    for u in range(1, len(units)):
        nxt_parts = project(u)
        convolve(u - 1, *parts)
        parts = nxt_parts
    convolve(len(units) - 1, *parts)

    @pl.when(i == N_TILES // CONV_TILES - 1)
    def _():
        wait_rows()


def _combine_conv(slot, x1, mod0, y_sorted, mod, norm_g, w_in, conv_w):
    n_steps = N_TILES // CONV_TILES
    block = pl.BlockSpec((CONV_ROWS, D), lambda i: (i, 0))
    mod_spec = pl.BlockSpec((1, 6, 1, D), lambda i: (_cond_of_tile(i * CONV_TILES), 0, 0, 0))
    slots = lambda step: pl.BlockSpec((CONV_TILES, 1, TM), lambda i: (step(i), 0, 0), memory_space=pltpu.SMEM)
    return pl.pallas_call(
        _combine_conv_kernel,
        grid=(n_steps,),
        in_specs=[slots(lambda i: i), slots(lambda i: jnp.minimum(i + 1, n_steps - 1)), block, mod_spec, mod_spec,
                  _ROW_SPEC, _ANY_SPEC, pl.BlockSpec((3, D), lambda i: (0, 0)), _ANY_SPEC],
        out_specs=[block, block],
        out_shape=[jax.ShapeDtypeStruct((T, D), F32), jax.ShapeDtypeStruct((T, D), BF16)],
        scratch_shapes=_weight_scratch(3) + [pltpu.SemaphoreType.DMA((1,))],
        compiler_params=_cparams(),
        name="moe_combine_conv",
    )(slot, slot, x1, mod0, mod, norm_g, w_in, conv_w, y_sorted)


def _plan_kernel(cnt_ref, bucket_ref, rank_ref, slot_ref,
                 ea_ref, eb_ref, next_a_ref, next_b_ref, a_higher_ref, rows_ref, xblock_ref, zero_from_ref):
    slot = rank_ref[...]
    bucket = bucket_ref[...]
    tile0 = jnp.int32(0)
    last = [jnp.int32(0)] * 3
    for b in range(N_BUCKETS):
        n = cnt_ref[b]
        tiles = (n + (TM_E - 1)) // TM_E
        pair = b % N_PAIRS
        per_tile = ((b // N_PAIRS) * EPG + _SLOT_A[pair], (b // N_PAIRS) * EPG + _SLOT_B[pair],
                    _SLOT_A_IS_HIGHER[pair])

        def fill(k, carry, n=n, tile0=tile0, per_tile=per_tile):
            j = tile0 + k
            live = jnp.minimum(n - k * TM_E, TM_E)
            rows_ref[j] = live
            xblock_ref[j] = j
            zero_from_ref[j] = live // ZERO_ROWS
            for ref, value in zip((ea_ref, eb_ref, a_higher_ref), per_tile):
                ref[j] = jnp.int32(value)
            return carry

        lax.fori_loop(0, tiles, fill, 0)
        slot = slot + jnp.where(bucket == b, tile0 * TM_E, 0)
        last = [jnp.where(tiles > 0, value, old) for value, old in zip(per_tile, last)]
        tile0 = tile0 + tiles

    def unused(j, carry):
        rows_ref[j] = jnp.int32(0)
        xblock_ref[j] = jnp.maximum(tile0 - 1, 0)
        zero_from_ref[j] = jnp.int32(0)
        for ref, value in zip((ea_ref, eb_ref, a_higher_ref), last):
            ref[j] = value
        return carry

    lax.fori_loop(tile0, N_TILES_E, unused, 0)

    for e_ref, next_ref in ((ea_ref, next_a_ref), (eb_ref, next_b_ref)):
        next_ref[N_TILES_E - 1] = jnp.int32(-1)

        def backward(k, carry, e_ref=e_ref, next_ref=next_ref):
            j = N_TILES_E - 2 - k
            next_ref[j] = jnp.where(e_ref[j] == e_ref[j + 1], next_ref[j + 1], e_ref[j + 1])
            return carry

        lax.fori_loop(0, N_TILES_E - 1, backward, 0)
    slot_ref[...] = slot


def _plan(counts, bucket, rank):
    whole = pl.BlockSpec(bucket.shape, lambda i, *_: (0, 0, 0))
    smem = pl.BlockSpec(memory_space=pltpu.SMEM)
    per_tile = jax.ShapeDtypeStruct((N_TILES_E,), jnp.int32)
    slot, *tile_plan = pl.pallas_call(
        _plan_kernel,
        grid_spec=pltpu.PrefetchScalarGridSpec(
            num_scalar_prefetch=1, grid=(1,), in_specs=[whole, whole], out_specs=[whole] + [smem] * 8),
        out_shape=[jax.ShapeDtypeStruct(bucket.shape, jnp.int32)] + [per_tile] * 8,
        compiler_params=_cparams(),
        name="moe_plan",
    )(counts[:, 0].astype(jnp.int32), bucket, rank)
    return slot, tile_plan


def _moe(a, x_parts, mod, norm_g2, w_out, router_wt, router_bt, w_gate, w_up, w_down, layer, final_g,
         *, final):
    x1, gates, bucket, rank, counts = _post_mixer(a, x_parts, mod, norm_g2, w_out, router_wt, router_bt)
    slot, (*expert_plan, zero_from) = _plan(counts, bucket, rank)
    xs = _dispatch(zero_from, slot, x1, mod, norm_g2, gates)
    ys = _experts(*expert_plan, xs, w_gate, w_up, w_down, layer)
    if not final:
        return slot, x1, ys
    return _combine(slot, x1, mod, final_g, ys, final=True)


def kernel(x_prompt, x_sample, state_hgrn, c, c_ctx, w_mod, b_mod, norm_mix_g, norm_ffn_g, norm_final_g,
           hgrn_w_in, hgrn_lb_logits, hgrn_norm_g, hgrn_w_out, conv_w_in, conv_w, conv_w_out,
           router_w, router_b, moe_w_gate, moe_w_up, moe_w_down):
    x_parts = (x_prompt.reshape(T_PROMPT, D), x_sample.reshape(T_SAMPLE, D))
    cond = jnp.concatenate([c_ctx[None], c, jnp.zeros((COND_ROWS - N_COND, D), F32)], axis=0)
    mods = _adaln(cond, w_mod, b_mod)[:, :N_COND].reshape(2, N_COND, 6, 1, D)
    member_major = lambda w: w.reshape(N_GROUPS, EPG, -1).transpose(1, 0, 2).reshape(N_EXPERTS, -1)
    rwt = member_major(router_w.T)
    rbt = member_major(router_b.reshape(N_EXPERTS, 1))
    final_g = norm_final_g.reshape(1, D)

    q, v, ff, fb, gg = _hgrn_proj(x_parts, mods[0], norm_mix_g[0:1], hgrn_w_in[0])
    s0 = state_hgrn[:, 0]
    o_f, s_f = _gla_fwd(q, v, ff, hgrn_lb_logits, s0)
    a, new_state = _gla_bwd(q, v, fb, hgrn_lb_logits, s0, o_f, gg, hgrn_norm_g[0:1], s_f)
    slot, x1, ys = _moe(a, x_parts, mods[0], norm_ffn_g[0:1], hgrn_w_out[0], rwt, rbt,
                        moe_w_gate, moe_w_up, moe_w_down, 0, final_g, final=False)

    x, a = _combine_conv(slot, x1, mods[0], ys, mods[1], norm_mix_g[1:2], conv_w_in[0], conv_w[0])
    y_p, y_s = _moe(a, (x,), mods[1], norm_ffn_g[1:2], conv_w_out[0], rwt, rbt,
                    moe_w_gate, moe_w_up, moe_w_down, 1, final_g, final=True)

    return (y_p.reshape(N_PROMPT_SEQ, PROMPT_LEN, D), y_s.reshape(N_SAMPLE_SEQ, SAMPLE_LEN, D),
            new_state[:, None])
```

```python
import functools

import jax
import jax.numpy as jnp
from jax import lax
from jax.experimental import pallas as pl
from jax.experimental.pallas import tpu as pltpu

F32 = jnp.float32
BF16 = jnp.bfloat16

D = 1024
N_PROMPT_SEQ = 16
PROMPT_LEN = 256
N_SAMPLE_SEQ = 2
SAMPLE_LEN = 4096
GRID_W = 64
T_PROMPT = N_PROMPT_SEQ * PROMPT_LEN
T_SAMPLE = N_SAMPLE_SEQ * SAMPLE_LEN
T = T_PROMPT + T_SAMPLE
N_COND = 1 + N_SAMPLE_SEQ
COND_ROWS = 8

HEADS = 8
DK = 128
DV = 128
CHUNK = 64
N_EXPERTS = 16
N_GROUPS = 4
EPG = 4
N_PAIRS = 6
N_BUCKETS = N_GROUPS * N_PAIRS
D_FF = 512
EPS = 1e-6

TM = 256
N_TILES = T // TM
PROMPT_TILES = T_PROMPT // TM
TILES_PER_SAMPLE = SAMPLE_LEN // TM
CHUNKS_PER_TILE = TM // CHUNK
STEP_TILES = 2
STEP_ROWS = STEP_TILES * TM
N_STEPS = N_TILES // STEP_TILES
PROMPT_STEPS = PROMPT_TILES // STEP_TILES
MOVE_TILES = 2
MOVE_ROWS = MOVE_TILES * TM
N_MOVE_STEPS = N_TILES // MOVE_TILES
ROUTE_TILES = 4
CONV_TILES = 4
TM_E = 288
N_TILES_E = T // TM_E + N_BUCKETS
S_ROWS = N_TILES_E * TM_E
ZERO_ROWS = 32
XW = D + 128
LANES = 128
MOD_TN = 3072

V7X_VMEM_BYTES = 64 * 1024 * 1024
VMEM_LIMIT = V7X_VMEM_BYTES * 7 // 8

_SLOT_A = (0, 3, 3, 0, 0, 1)
_SLOT_B = (1, 1, 2, 2, 3, 2)
_SLOT_A_IS_HIGHER = tuple(int(a > b) for a, b in zip(_SLOT_A, _SLOT_B))


def _cparams():
    return pltpu.CompilerParams(dimension_semantics=("arbitrary",), vmem_limit_bytes=VMEM_LIMIT)


def _cond_of_tile(t):
    return jnp.where(t < PROMPT_TILES, 0, 1 + (t - PROMPT_TILES) // TILES_PER_SAMPLE)


def _rmsnorm(x, g):
    ms = jnp.mean(x * x, axis=-1, keepdims=True)
    return (x * lax.rsqrt(ms + EPS)) * g


def _modulate(x, g, shift, scale):
    return _rmsnorm(x, g) * (1.0 + scale) + shift


def _silu(x):
    return x * jax.nn.sigmoid(x)


def _dot(a, b):
    return jnp.dot(a, b, preferred_element_type=F32)


def _dot_nt(a, b):
    return lax.dot_general(a, b, (((1,), (1,)), ((), ())), preferred_element_type=F32)


def _dot_tn(a, b):
    return lax.dot_general(a, b, (((0,), (0,)), ((), ())), preferred_element_type=F32)


def _split_bf16(x):
    hi = x.astype(BF16)
    lo = (x - hi.astype(F32)).astype(BF16)
    return hi, lo


def _mod_kernel(cond_ref, w_ref, b_ref, o_ref):
    s = _silu(cond_ref[...])
    o_ref[0] = _dot(s.astype(BF16), w_ref[0].astype(BF16)) + b_ref[0]


def _adaln(cond, w_mod, b_mod):
    depth = w_mod.shape[0]
    return pl.pallas_call(
        _mod_kernel,
        grid=(depth, 6 * D // MOD_TN),
        in_specs=[
            pl.BlockSpec((COND_ROWS, D), lambda i, j: (0, 0)),
            pl.BlockSpec((1, D, MOD_TN), lambda i, j: (i, 0, j)),
            pl.BlockSpec((1, 1, MOD_TN), lambda i, j: (i, 0, j)),
        ],
        out_specs=pl.BlockSpec((1, COND_ROWS, MOD_TN), lambda i, j: (i, 0, j)),
        out_shape=jax.ShapeDtypeStruct((depth, COND_ROWS, 6 * D), F32),
        compiler_params=pltpu.CompilerParams(
            dimension_semantics=("arbitrary", "arbitrary"), vmem_limit_bytes=VMEM_LIMIT),
        name="adaln",
    )(cond, w_mod, b_mod.reshape(depth, 1, 6 * D))


_ROW_SPEC = pl.BlockSpec((1, D), lambda i: (0, 0))
_STEP_MOD_SPEC = pl.BlockSpec((1, 6, 1, D), lambda i, *_: (_cond_of_tile(i * STEP_TILES), 0, 0, 0))
_STEP_SPEC = pl.BlockSpec((STEP_ROWS, D), lambda i, *_: (i, 0))
_SUB_ROWS = tuple(slice(s * TM, (s + 1) * TM) for s in range(STEP_TILES))


def _load_weight_bf16(w_hbm, w16_ref, stage_ref, sem):
    n = w16_ref.shape[1] // D

    def slab(c):
        return pltpu.make_async_copy(w_hbm.at[:, pl.ds(c * D, D)], stage_ref.at[c % 2], sem.at[c % 2])

    slab(0).start()
    for c in range(n):
        if c + 1 < n:
            slab(c + 1).start()
        slab(c).wait()
        w16_ref[:, c * D:(c + 1) * D] = stage_ref[c % 2].astype(BF16)


def _weight_scratch(n_slabs):
    n_stage = min(n_slabs, 2)
    return [pltpu.VMEM((D, n_slabs * D), BF16), pltpu.VMEM((n_stage, D, D), F32),
            pltpu.SemaphoreType.DMA((n_stage,))]


_ANY_SPEC = pl.BlockSpec(memory_space=pl.ANY)


def _token_tile(x_refs, rows=slice(None), prompt_steps=PROMPT_TILES):
    if len(x_refs) == 1:
        return x_refs[0][rows]
    return jnp.where(pl.program_id(0) < prompt_steps, x_refs[0][rows], x_refs[1][rows])


def _token_specs(x_parts, tiles_per_step=1):
    rows = tiles_per_step * TM
    if len(x_parts) == 1:
        return [pl.BlockSpec((rows, D), lambda i: (i, 0))]
    prompt_steps = T_PROMPT // rows
    return [pl.BlockSpec((rows, D), lambda i: (jnp.minimum(i, prompt_steps - 1), 0)),
            pl.BlockSpec((rows, D), lambda i: (jnp.maximum(i - prompt_steps, 0), 0))]


def _hgrn_proj_kernel(xp_ref, xs_ref, mod_ref, g_ref, w_hbm, q_ref, v_ref, ff_ref, fb_ref, gg_ref,
                      w_ref, stage_ref, sem):
    @pl.when(pl.program_id(0) == 0)
    def _():
        _load_weight_bf16(w_hbm, w_ref, stage_ref, sem)

    h = [_modulate(_token_tile((xp_ref, xs_ref), rows, PROMPT_STEPS),
                   g_ref[...], mod_ref[0, 0], mod_ref[0, 1]).astype(BF16) for rows in _SUB_ROWS]
    for rows, h_tile in zip(_SUB_ROWS, h):
        q_ref[rows] = (_dot(h_tile, w_ref[:, 0 * D:1 * D]) * (DK ** -0.5)).astype(BF16)
        v_ref[rows] = _dot(h_tile, w_ref[:, 1 * D:2 * D]).astype(BF16)
        ff_ref[rows] = _dot(h_tile, w_ref[:, 2 * D:3 * D])
        fb_ref[rows] = _dot(h_tile, w_ref[:, 3 * D:4 * D])
        gg_ref[rows] = _dot(h_tile, w_ref[:, 4 * D:5 * D]).astype(BF16)


def _hgrn_proj(x_parts, mod, norm_g, w_in):
    bf = jax.ShapeDtypeStruct((T, D), BF16)
    f32 = jax.ShapeDtypeStruct((T, D), F32)
    return pl.pallas_call(
        _hgrn_proj_kernel,
        grid=(N_STEPS,),
        in_specs=_token_specs(x_parts, STEP_TILES) + [_STEP_MOD_SPEC, _ROW_SPEC, _ANY_SPEC],
        out_specs=[_STEP_SPEC] * 5,
        out_shape=[bf, bf, f32, f32, bf],
        scratch_shapes=_weight_scratch(5),
        compiler_params=_cparams(),
        name="hgrn_proj",
    )(*x_parts, mod, norm_g, w_in)


GLA_TILES = 4
GLA_ROWS = GLA_TILES * TM
N_GLA_STEPS = N_TILES // GLA_TILES
GLA_PROMPT_STEPS = PROMPT_TILES // GLA_TILES
STEPS_PER_SAMPLE = SAMPLE_LEN // GLA_ROWS
_GLA_SPEC = pl.BlockSpec((GLA_ROWS, D), lambda i: (i, 0))
_GLA_TILE_ROWS = tuple(slice(s * TM, (s + 1) * TM) for s in range(GLA_TILES))


def _gla_step(block, q_ref, v_ref, f_ref, lbl_ref, s0_ref, st_ref, st_tiles_ref, ma_ref, o_ref,
              *, reverse, finish=None):
    is_prompt = block < GLA_PROMPT_STEPS
    rel = block - GLA_PROMPT_STEPS
    edge = (STEPS_PER_SAMPLE - 1) if reverse else 0
    sample_start = jnp.logical_and(rel >= 0, rel % STEPS_PER_SAMPLE == edge)

    @pl.when(is_prompt)
    def _():
        st_ref[...] = jnp.zeros_like(st_ref)

    @pl.when(sample_start)
    def _():
        for h in range(HEADS):
            st_ref[h] = s0_ref[0, 0, h].T

    logits = lbl_ref[...]
    e = jnp.exp(logits - jnp.max(logits, axis=0, keepdims=True))
    lb = e[0:1] / jnp.sum(e, axis=0, keepdims=True)

    ref_idx = (CHUNK - 1 - CHUNK // 2) if reverse else CHUNK // 2
    last_idx = 0 if reverse else CHUNK - 1
    shift = CHUNK.bit_length() - 1

    row = lax.broadcasted_iota(jnp.int32, (TM, TM), 0)
    col = lax.broadcasted_iota(jnp.int32, (TM, TM), 1)
    same = (row >> shift) == (col >> shift)
    ref_row = ((row >> shift) << shift) + ref_idx
    keep = jnp.logical_and(same, (row <= col) if reverse else (row >= col))

    @pl.when(pl.program_id(0) == 0)
    def _():
        at_ref = jnp.logical_and(same, (ref_row <= col) if reverse else (ref_row >= col))
        ma_ref[...] = (keep.astype(F32) - at_ref.astype(F32)).astype(BF16)

    ma = ma_ref[...]

    rr = lax.broadcasted_iota(jnp.int32, (2 * CHUNKS_PER_TILE, TM), 0)
    cc = lax.broadcasted_iota(jnp.int32, (2 * CHUNKS_PER_TILE, TM), 1)
    chunk = rr & (CHUNKS_PER_TILE - 1)
    target = chunk * CHUNK + jnp.where(rr < CHUNKS_PER_TILE, ref_idx, last_idx)
    covered = (cc >= target) if reverse else (cc <= target)
    rm = jnp.logical_and((cc >> shift) == chunk, covered).astype(BF16)
    ma_rm = jnp.concatenate([ma, rm, jnp.zeros_like(rm)], axis=0)

    chunk_rows = [slice(c * CHUNK, (c + 1) * CHUNK) for c in range(CHUNKS_PER_TILE)]
    order = range(CHUNKS_PER_TILE - 1, -1, -1) if reverse else range(CHUNKS_PER_TILE)
    pair_w = 2 * DK
    n_pairs = HEADS // 2
    tile_order = _GLA_TILE_ROWS[::-1] if reverse else _GLA_TILE_ROWS

    def per_chunk_scale(x, scale):
        return jnp.concatenate([x[rs] * scale[c:c + 1] for c, rs in enumerate(chunk_rows)], axis=0)

    def pair_cols(p):
        return slice(p * pair_w, (p + 1) * pair_w)

    def stage_decay(k, p):
        lbp = lb[:, pair_cols(p)]
        f = lbp + (1.0 - lbp) * jax.nn.sigmoid(f_ref[tile_order[k], pair_cols(p)])
        lf_hi, lf_lo = _split_bf16(jnp.log(f))
        both = _dot(ma_rm, lf_hi) + _dot(ma_rm, lf_lo)
        z, marks = both[:TM], both[TM:TM + 2 * CHUNKS_PER_TILE]
        return 1.0 - f, z, marks

    def stage_scores(k, p, kk, z, marks):
        ref, last = marks[:CHUNKS_PER_TILE], marks[CHUNKS_PER_TILE:]
        qa32 = q_ref[tile_order[k], pair_cols(p)].astype(F32) * jnp.exp(z)
        kb32 = kk * jnp.exp(-z)
        qa = qa32.astype(BF16)
        kb = kb32.astype(BF16)
        qe = per_chunk_scale(qa32, jnp.exp(ref)).astype(BF16)
        kd = per_chunk_scale(kb32, jnp.exp(last - ref)).astype(BF16)
        v = v_ref[tile_order[k], pair_cols(p)]
        scores, incr = [], []
        for hh in range(2):
            ls = slice(hh * DK, (hh + 1) * DK)
            scores.append(_dot_nt(qa[:, ls], kb[:, ls]))
            kd_h = kd[:, ls]
            kd_blocks = jnp.concatenate(
                [jnp.concatenate([part for part in (jnp.zeros((c * CHUNK, DK), BF16), kd_h[rs],
                                                    jnp.zeros((TM - (c + 1) * CHUNK, DK), BF16))
                                  if part.shape[0]], axis=0)
                 for c, rs in enumerate(chunk_rows)], axis=1)
            incr.append(_dot_tn(v[:, ls], kd_blocks))
        return scores, incr, qe, jnp.exp(last), v

    def stage_output(k, p, scores, incr, qe, decay, v):
        base = tile_order[k].start
        for hh in range(2):
            h = 2 * p + hh
            ls = slice(hh * DK, (hh + 1) * DK)
            hs = slice(h * DK, (h + 1) * DK)
            o_intra = _dot(jnp.where(keep, scores[hh], 0.0).astype(BF16), v[:, ls])
            st = st_ref[h]
            if k > 0:
                st = jnp.where(is_prompt, 0.0, st)
            for c in order:
                rs = chunk_rows[c]
                o_ref[base + rs.start:base + rs.stop, hs] = (
                    o_intra[rs] + _dot_nt(qe[rs, ls], st.astype(BF16)))
                st = st * decay[c:c + 1, ls] + incr[hh][:, c * DK:(c + 1) * DK]
            st_ref[h] = st
            if k < GLA_TILES - 1:
                st_tiles_ref[k, h] = st
        if finish is not None:
            return (tile_order[k], p)

    def stage_finish(k, p, rows, pair):
        finish(rows, pair)

    units = [(k, p) for k in range(GLA_TILES) for p in range(n_pairs)]
    stages = [stage_decay, stage_scores, stage_output] + ([stage_finish] if finish is not None else [])
    carried = [dict() for _ in stages]
    for step in range(len(units) + len(stages) - 1):
        for depth, stage in (enumerate(stages) if reverse else reversed(list(enumerate(stages)))):
            u = step - depth
            if 0 <= u < len(units):
                out = stage(*units[u], *(carried[depth].pop(u) if depth else ()))
                if depth + 1 < len(stages):
                    carried[depth + 1][u] = out


def _store_prompt_states(block, st_ref, st_tiles_ref, snew_ref, *, reverse, sfwd_ref=None):
    @pl.when(block < GLA_PROMPT_STEPS)
    def _():
        for k in range(GLA_TILES):
            tile = GLA_TILES - 1 - k if reverse else k
            for h in range(HEADS):
                state = (st_tiles_ref[k, h] if k < GLA_TILES - 1 else st_ref[h]).T
                if sfwd_ref is None:
                    snew_ref[tile, h] = state
                else:
                    snew_ref[tile, 1, h] = state
        if sfwd_ref is not None:
            snew_ref[:, 0] = sfwd_ref[...]


def _gla_fwd_kernel(q_ref, v_ref, f_ref, lbl_ref, s0_ref, o_ref, snew_ref, st_ref, st_tiles_ref, ma_ref):
    block = pl.program_id(0)
    _gla_step(block, q_ref, v_ref, f_ref, lbl_ref, s0_ref, st_ref, st_tiles_ref, ma_ref, o_ref,
              reverse=False)
    _store_prompt_states(block, st_ref, st_tiles_ref, snew_ref, reverse=False)


def _gla_bwd_kernel(q_ref, v_ref, f_ref, lbl_ref, s0_ref, of_ref, gg_ref, ng_ref, sfwd_ref,
                    a_ref, snew_ref, st_ref, st_tiles_ref, ma_ref, ob_ref):
    block = N_GLA_STEPS - 1 - pl.program_id(0)

    def finish(rows, p):
        for h in (2 * p, 2 * p + 1):
            hs = slice(h * DV, (h + 1) * DV)
            o = of_ref[rows, hs] + ob_ref[rows, hs]
            a_ref[rows, hs] = (_rmsnorm(o, ng_ref[...]) * _silu(gg_ref[rows, hs].astype(F32))).astype(BF16)

    _gla_step(block, q_ref, v_ref, f_ref, lbl_ref, s0_ref, st_ref, st_tiles_ref, ma_ref, ob_ref,
              reverse=True, finish=finish)
    _store_prompt_states(block, st_ref, st_tiles_ref, snew_ref, reverse=True, sfwd_ref=sfwd_ref)


def _state_specs(direction, block_of_step):
    sample = lambda i: jnp.clip((block_of_step(i) - GLA_PROMPT_STEPS) // STEPS_PER_SAMPLE, 0, N_SAMPLE_SEQ - 1)
    s0 = pl.BlockSpec((1, 1, HEADS, DK, DV), lambda i: (sample(i), direction, 0, 0, 0))
    snew = pl.BlockSpec((GLA_TILES, HEADS, DK, DV),
                        lambda i: (jnp.minimum(block_of_step(i), GLA_PROMPT_STEPS - 1), 0, 0, 0))
    return s0, snew


_GLA_STATE_SCRATCH = [pltpu.VMEM((HEADS, DV, DK), F32), pltpu.VMEM((GLA_TILES - 1, HEADS, DV, DK), F32),
                      pltpu.VMEM((TM, TM), BF16)]


def _gla_fwd(q, v, ff, lb_logits, s0):
    s0_spec, snew_spec = _state_specs(0, lambda i: i)
    return pl.pallas_call(
        _gla_fwd_kernel,
        grid=(N_GLA_STEPS,),
        in_specs=[_GLA_SPEC, _GLA_SPEC, _GLA_SPEC,
                  pl.BlockSpec(lb_logits.shape, lambda i: (0, 0)), s0_spec],
        out_specs=[_GLA_SPEC, snew_spec],
        out_shape=[jax.ShapeDtypeStruct((T, D), F32),
                   jax.ShapeDtypeStruct((N_PROMPT_SEQ, HEADS, DK, DV), F32)],
        scratch_shapes=_GLA_STATE_SCRATCH,
        compiler_params=_cparams(),
        name="gla_fwd",
    )(q, v, ff, lb_logits, s0)


def _gla_bwd(q, v, fb, lb_logits, s0, o_f, gg, head_norm_g, s_fwd):
    rev = lambda i: N_GLA_STEPS - 1 - i
    block = pl.BlockSpec((GLA_ROWS, D), lambda i: (rev(i), 0))
    s0_spec, sfwd_spec = _state_specs(1, rev)
    prompt_block = lambda i: jnp.minimum(rev(i), GLA_PROMPT_STEPS - 1)
    return pl.pallas_call(
        _gla_bwd_kernel,
        grid=(N_GLA_STEPS,),
        in_specs=[block, block, block, pl.BlockSpec(lb_logits.shape, lambda i: (0, 0)), s0_spec,
                  block, block, pl.BlockSpec((1, DV), lambda i: (0, 0)), sfwd_spec],
        out_specs=[block, pl.BlockSpec((GLA_TILES, 2, HEADS, DK, DV), lambda i: (prompt_block(i), 0, 0, 0, 0))],
        out_shape=[jax.ShapeDtypeStruct((T, D), BF16),
                   jax.ShapeDtypeStruct((N_PROMPT_SEQ, 2, HEADS, DK, DV), F32)],
        scratch_shapes=_GLA_STATE_SCRATCH + [pltpu.VMEM((GLA_ROWS, D), F32)],
        compiler_params=_cparams(),
        name="gla_bwd",
    )(q, v, fb, lb_logits, s0, o_f, gg, head_norm_g, s_fwd)


def _conv_mixer_kernel(x_ref, mod_ref, g_ref, w_hbm, cw_ref, a_ref, w_ref, stage_ref, sem):
    @pl.when(pl.program_id(0) == 0)
    def _():
        _load_weight_bf16(w_hbm, w_ref, stage_ref, sem)

    seg = jnp.where(pl.program_id(0) < PROMPT_TILES // CONV_TILES, PROMPT_LEN, GRID_W)
    pos = lax.broadcasted_iota(jnp.int32, (TM, 1), 0) & (seg - 1)

    def project(rows):
        h = _modulate(x_ref[rows], g_ref[...], mod_ref[0, 0], mod_ref[0, 1]).astype(BF16)
        return [_dot(h, w_ref[:, k * D:(k + 1) * D]) for k in range(3)]

    def convolve(rows, bg, cg, x_in):
        u = cg * x_in
        prev = jnp.where(pos == 0, 0.0, pltpu.roll(u, 1, axis=0))
        nxt = jnp.where(pos == seg - 1, 0.0, pltpu.roll(u, TM - 1, axis=0))
        conv = cw_ref[0:1] * prev + cw_ref[1:2] * u + cw_ref[2:3] * nxt
        a_ref[rows] = (bg * conv).astype(BF16)

    tiles = [slice(s * TM, (s + 1) * TM) for s in range(CONV_TILES)]
    projected = [project(rows) for rows in tiles]
    for rows, parts in zip(tiles, projected):
        convolve(rows, *parts)


def _conv_mixer(x, mod, norm_g, w_in, conv_w):
    block = pl.BlockSpec((CONV_TILES * TM, D), lambda i: (i, 0))
    return pl.pallas_call(
        _conv_mixer_kernel,
        grid=(N_TILES // CONV_TILES,),
        in_specs=[block, pl.BlockSpec((1, 6, 1, D), lambda i: (_cond_of_tile(i * CONV_TILES), 0, 0, 0)),
                  _ROW_SPEC, _ANY_SPEC, pl.BlockSpec((3, D), lambda i: (0, 0))],
        out_specs=block,
        out_shape=jax.ShapeDtypeStruct((T, D), BF16),
        scratch_shapes=_weight_scratch(3),
        compiler_params=_cparams(),
        name="conv_mixer",
    )(x, mod, norm_g, w_in, conv_w)


ROUTE_ROWS = 32


def _first_member(vals, target):
    idx = jnp.full_like(target, float(EPG - 1))
    for j in range(EPG - 2, -1, -1):
        idx = jnp.where(vals[j] == target, float(j), idx)
    return idx


def _router_logits(h2, rwt_ref):
    h_hi, h_lo = _split_bf16(h2)
    w_hi, w_lo = _split_bf16(rwt_ref[...])
    with_hi = _dot_nt(jnp.concatenate([w_hi, w_lo], axis=0), h_hi)
    return with_hi[:N_EXPERTS] + (_dot_nt(w_hi, h_lo) + with_hi[N_EXPERTS:])


def _select_experts(logits, rb_ref):
    scores = jax.nn.sigmoid(logits)
    sel = scores + rb_ref[...]
    member = [sel[j * N_GROUPS:(j + 1) * N_GROUPS] for j in range(EPG)]
    m1 = functools.reduce(jnp.maximum, member)
    i1 = _first_member(member, m1)
    rest = [jnp.where(i1 == float(j), -jnp.inf, member[j]) for j in range(EPG)]
    m2 = functools.reduce(jnp.maximum, rest)
    i2 = _first_member(rest, m2)
    group_score = m1 + m2

    best, grp, first, second = group_score[0:1], jnp.zeros((1, TM), F32), i1[0:1], i2[0:1]
    for g in range(1, N_GROUPS):
        upd = group_score[g:g + 1] > best
        best = jnp.where(upd, group_score[g:g + 1], best)
        grp = jnp.where(upd, float(g), grp)
        first = jnp.where(upd, i1[g:g + 1], first)
        second = jnp.where(upd, i2[g:g + 1], second)
    a = jnp.minimum(first, second)
    b = jnp.maximum(first, second)
    row = lax.broadcasted_iota(jnp.int32, (N_EXPERTS, TM), 0).astype(F32)
    s_lo = jnp.sum(jnp.where(row == a * N_GROUPS + grp, scores, 0.0), axis=0, keepdims=True)
    s_hi = jnp.sum(jnp.where(row == b * N_GROUPS + grp, scores, 0.0), axis=0, keepdims=True)
    tot = s_lo + s_hi
    pair = jnp.where(a == 0.0, jnp.where(b == 1.0, 0.0, jnp.where(b == 2.0, 3.0, 4.0)),
                     jnp.where(a == 1.0, jnp.where(b == 2.0, 5.0, 1.0), 2.0))
    return grp * N_PAIRS + pair, s_lo / tot, s_hi / tot


def _post_mixer_kernel(*refs, n_x):
    a_ref, x_refs = refs[0], refs[1:1 + n_x]
    (mod_ref, g2_ref, wo_hbm, rwt_ref, rb_ref,
     x1_ref, gates_ref, bucket_ref, rank_ref, cnt_ref, carry_ref, earlier_ref, wo_ref, stage_ref, sem) = refs[1 + n_x:]

    @pl.when(pl.program_id(0) == 0)
    def _():
        _load_weight_bf16(wo_hbm, wo_ref, stage_ref, sem)
        carry_ref[...] = jnp.zeros_like(carry_ref)
        s_idx = lax.broadcasted_iota(jnp.int32, (TM, TM), 0)
        t_idx = lax.broadcasted_iota(jnp.int32, (TM, TM), 1)
        earlier_ref[...] = (s_idx < t_idx).astype(BF16)

    def stage_mix(s):
        rows = slice(s * TM, (s + 1) * TM)
        x1 = (_token_tile(x_refs, rows, PROMPT_TILES // ROUTE_TILES)
              + mod_ref[0, 2] * _dot(a_ref[rows], wo_ref[...]))
        x1_ref[rows] = x1
        return _modulate(x1, g2_ref[...], mod_ref[0, 3], mod_ref[0, 4])

    def stage_rank(s, bucket, g_lo, g_hi):
        onehot = lax.broadcasted_iota(jnp.int32, (ROUTE_ROWS, TM), 0).astype(F32) == bucket
        before = _dot(onehot.astype(BF16), earlier_ref[...])
        oh = onehot.astype(F32)
        carry = carry_ref[...]
        rank = jnp.sum(oh * (before + carry[:, 0:1]), axis=0, keepdims=True)
        carry_ref[...] = carry + jnp.sum(oh, axis=1, keepdims=True)
        bucket_ref[s] = bucket.astype(jnp.int32)
        rank_ref[s] = rank.astype(jnp.int32)
        grow = lax.broadcasted_iota(jnp.int32, (LANES, TM), 0)
        gates_ref[s * TM:(s + 1) * TM] = jnp.where(grow == 0, g_lo, jnp.where(grow == 1, g_hi, 0.0)).T

    tiles = range(ROUTE_TILES)
    h2 = [stage_mix(s) for s in tiles]
    logits = [_router_logits(h2[s], rwt_ref) for s in tiles]
    picks = [_select_experts(logits[s], rb_ref) for s in tiles]
    for s in tiles:
        stage_rank(s, *picks[s])
    cnt_ref[...] = carry_ref[...]


def _post_mixer(a, x_parts, mod, norm_g2, w_out, router_wt, router_bt):
    rows = ROUTE_TILES * TM
    idx_spec = pl.BlockSpec((ROUTE_TILES, 1, TM), lambda i: (i, 0, 0))
    return pl.pallas_call(
        functools.partial(_post_mixer_kernel, n_x=len(x_parts)),
        grid=(N_TILES // ROUTE_TILES,),
        in_specs=[pl.BlockSpec((rows, D), lambda i: (i, 0))] + _token_specs(x_parts, ROUTE_TILES) + [
            pl.BlockSpec((1, 6, 1, D), lambda i: (_cond_of_tile(i * ROUTE_TILES), 0, 0, 0)), _ROW_SPEC,
            _ANY_SPEC,
            pl.BlockSpec((N_EXPERTS, D), lambda i: (0, 0)),
            pl.BlockSpec((N_EXPERTS, 1), lambda i: (0, 0))],
        out_specs=[pl.BlockSpec((rows, D), lambda i: (i, 0)), pl.BlockSpec((rows, LANES), lambda i: (i, 0)),
                   idx_spec, idx_spec, pl.BlockSpec((ROUTE_ROWS, LANES), lambda i: (0, 0))],
        out_shape=[jax.ShapeDtypeStruct((T, D), F32), jax.ShapeDtypeStruct((T, LANES), F32),
                   jax.ShapeDtypeStruct((N_TILES, 1, TM), jnp.int32),
                   jax.ShapeDtypeStruct((N_TILES, 1, TM), jnp.int32),
                   jax.ShapeDtypeStruct((ROUTE_ROWS, LANES), F32)],
        scratch_shapes=[pltpu.VMEM((ROUTE_ROWS, LANES), F32), pltpu.VMEM((TM, TM), BF16)] + _weight_scratch(1),
        compiler_params=_cparams(),
        name="post_mixer",
    )(a, *x_parts, mod, norm_g2, w_out, router_wt, router_bt)


_MOVE_SPEC = pl.BlockSpec((MOVE_ROWS, D), lambda i, *_: (i, 0))
_MOVE_MOD_SPEC = pl.BlockSpec((1, 6, 1, D), lambda i, *_: (_cond_of_tile(i * MOVE_TILES), 0, 0, 0))


def _smem_step_spec(step_of_step):
    return pl.BlockSpec((MOVE_TILES, 1, TM), lambda i, *_: (step_of_step(i), 0, 0), memory_space=pltpu.SMEM)


def _start_step_rows(make_copy):
    for tile in range(MOVE_TILES):
        for r in range(TM):
            make_copy(tile, r).start(priority=r % 2)


def _dispatch_kernel(zero_from_ref, slot_ref, x1_ref, mod_ref, g2_ref, gates_ref,
                     xs_ref, buf_ref, zero_ref, sems, zero_sem):
    i = pl.program_id(0)
    cur = i % 2

    @pl.when(i == 0)
    def _():
        zero_ref[...] = jnp.zeros_like(zero_ref)

        def for_each_chunk(act):
            def tile(j, carry):
                def chunk(c, inner):
                    row = pl.multiple_of(j * TM_E + c * ZERO_ROWS, ZERO_ROWS)
                    act(pltpu.make_async_copy(zero_ref, xs_ref.at[pl.ds(row, ZERO_ROWS)], zero_sem))
                    return inner
                return lax.fori_loop(zero_from_ref[j], TM_E // ZERO_ROWS, chunk, carry)
            lax.fori_loop(0, N_TILES_E, tile, 0)

        for_each_chunk(lambda copy: copy.start())
        for_each_chunk(lambda copy: copy.wait())

    def wait_step(b):
        pltpu.make_async_copy(buf_ref.at[b], xs_ref.at[pl.ds(0, MOVE_ROWS)], sems.at[b]).wait()

    h2 = _modulate(x1_ref[...], g2_ref[...], mod_ref[0, 3], mod_ref[0, 4])
    gates = gates_ref[...]

    def send(b):
        buf_ref[b, :, :D] = h2
        buf_ref[b, :, D:] = gates
        _start_step_rows(lambda tile, r: pltpu.make_async_copy(
            buf_ref.at[b, pl.ds(tile * TM + r, 1)], xs_ref.at[pl.ds(slot_ref[tile, 0, r], 1)], sems.at[b]))

        @pl.when(i > 0)
        def _():
            wait_step(1 - b)

        @pl.when(i == N_MOVE_STEPS - 1)
        def _():
            wait_step(b)

    for b in range(2):
        pl.when(cur == b)(functools.partial(send, b))


def _dispatch(zero_from, slot, x1, mod, norm_g2, gates):
    return pl.pallas_call(
        _dispatch_kernel,
        grid_spec=pltpu.PrefetchScalarGridSpec(
            num_scalar_prefetch=1,
            grid=(N_MOVE_STEPS,),
            in_specs=[_smem_step_spec(lambda i: i), _MOVE_SPEC, _MOVE_MOD_SPEC,
                      pl.BlockSpec((1, D), lambda i, *_: (0, 0)),
                      pl.BlockSpec((MOVE_ROWS, LANES), lambda i, *_: (i, 0))],
            out_specs=pl.BlockSpec(memory_space=pl.ANY),
            scratch_shapes=[pltpu.VMEM((2, MOVE_ROWS, XW), F32), pltpu.VMEM((ZERO_ROWS, XW), F32),
                            pltpu.SemaphoreType.DMA((2,)), pltpu.SemaphoreType.DMA(())],
        ),
        out_shape=jax.ShapeDtypeStruct((S_ROWS, XW), F32),
        compiler_params=_cparams(),
        name="moe_dispatch",
    )(zero_from, slot, x1, mod, norm_g2, gates)


_W_SHAPES = ((D, D_FF), (D, D_FF), (D_FF, D))


def _expert_kernel(ea_ref, eb_ref, next_a_ref, next_b_ref, a_higher_ref, nrows_ref, xblock_ref,
                   x_ref, wg_hbm, wu_hbm, wd_hbm, y_ref, *scratch, layer):
    stage, w16, sems = scratch[0:3], scratch[3:6], scratch[6]
    j = pl.program_id(0)
    n = nrows_ref[j]
    prev = jnp.maximum(j - 1, 0)

    def fetch(slot, expert):
        return [pltpu.make_async_copy(w.at[layer, expert], s.at[slot], sems.at[slot])
                for w, s in zip((wg_hbm, wu_hbm, wd_hbm), stage)]

    for slot, e_ref, next_ref in ((0, ea_ref, next_a_ref), (1, eb_ref, next_b_ref)):
        @pl.when(j == 0)
        def _(slot=slot, e_ref=e_ref):
            for copy in fetch(slot, e_ref[0]):
                copy.start()

        @pl.when(jnp.logical_or(j == 0, e_ref[j] != e_ref[prev]))
        def _(slot=slot, e_ref=e_ref, next_ref=next_ref):
            for copy in fetch(slot, e_ref[j]):
                copy.wait()
            for s, d in zip(stage, w16):
                d[slot] = s[slot].astype(BF16)

            @pl.when(next_ref[j] >= 0)
            def _():
                for copy in fetch(slot, next_ref[j]):
                    copy.start(priority=1)

    def run(rows):
        valid = lax.broadcasted_iota(jnp.int32, (rows, 1), 0) < n
        xe = jnp.where(valid, x_ref[:rows], 0.0)
        x = xe[:, :D].astype(BF16)
        g_lo, g_hi = xe[:, D:D + 1], xe[:, D + 1:D + 2]
        a_higher = a_higher_ref[j] != 0

        def ffn(slot, gate):
            wg_ref, wu_ref, wd_ref = w16
            act = _silu(_dot(x, wg_ref[slot])) * _dot(x, wu_ref[slot]) * gate
            return _dot(act.astype(BF16), wd_ref[slot])

        y_ref[:rows] = (ffn(0, jnp.where(a_higher, g_hi, g_lo))
                        + ffn(1, jnp.where(a_higher, g_lo, g_hi)))
        if rows < TM_E:
            y_ref[rows:] = jnp.zeros((TM_E - rows, D), F32)

    half = TM_E // 2
    pl.when(n > half)(functools.partial(run, TM_E))
    pl.when(jnp.logical_and(n > 0, n <= half))(functools.partial(run, half))

    @pl.when(n == 0)
    def _():
        y_ref[...] = jnp.zeros_like(y_ref)


def _experts(tile_ea, tile_eb, tile_next_a, tile_next_b, tile_a_higher, tile_rows, tile_xblock, xs,
             w_gate, w_up, w_down, layer):
    any_spec = pl.BlockSpec(memory_space=pl.ANY)
    return pl.pallas_call(
        functools.partial(_expert_kernel, layer=layer),
        grid_spec=pltpu.PrefetchScalarGridSpec(
            num_scalar_prefetch=7,
            grid=(N_TILES_E,),
            in_specs=[pl.BlockSpec((TM_E, XW), lambda j, *plan: (plan[-1][j], 0)), any_spec, any_spec, any_spec],
            out_specs=pl.BlockSpec((TM_E, D), lambda j, *_: (j, 0)),
            scratch_shapes=[pltpu.VMEM((2,) + s, F32) for s in _W_SHAPES]
                           + [pltpu.VMEM((2,) + s, BF16) for s in _W_SHAPES]
                           + [pltpu.SemaphoreType.DMA((2,))],
        ),
        out_shape=jax.ShapeDtypeStruct((S_ROWS, D), F32),
        compiler_params=_cparams(),
        name="moe_experts",
    )(tile_ea, tile_eb, tile_next_a, tile_next_b, tile_a_higher, tile_rows, tile_xblock, xs,
      w_gate, w_up, w_down)


def _combine_kernel(slot_ref, next_slot_ref, x1_ref, mod_ref, gf_ref, y_ref, *rest, final):
    out_refs, (buf_ref, sems) = rest[:-2], rest[-2:]
    i = pl.program_id(0)
    cur = i % 2

    def gather(slots, b):
        _start_step_rows(lambda tile, r: pltpu.make_async_copy(
            y_ref.at[pl.ds(slots[tile, 0, r], 1)], buf_ref.at[b, pl.ds(tile * TM + r, 1)], sems.at[b]))

    @pl.when(i == 0)
    def _():
        gather(slot_ref, 0)

    def combine(b):
        @pl.when(i + 1 < N_MOVE_STEPS)
        def _():
            gather(next_slot_ref, 1 - b)

        pltpu.make_async_copy(y_ref.at[pl.ds(0, MOVE_ROWS)], buf_ref.at[b], sems.at[b]).wait()
        x2 = x1_ref[...] + mod_ref[0, 5] * buf_ref[b]
        if not final:
            out_refs[0][...] = x2
            return
        y = _rmsnorm(x2, gf_ref[...])
        yp_ref, ys_ref = out_refs

        @pl.when(i < T_PROMPT // MOVE_ROWS)
        def _():
            yp_ref[...] = y

        @pl.when(i >= T_PROMPT // MOVE_ROWS)
        def _():
            ys_ref[...] = y

    for b in range(2):
        pl.when(cur == b)(functools.partial(combine, b))


def _combine(slot, x1, mod, final_g, y_sorted, *, final):
    if final:
        out_specs = _token_specs((None, None), MOVE_TILES)
        out_shape = [jax.ShapeDtypeStruct((T_PROMPT, D), F32), jax.ShapeDtypeStruct((T_SAMPLE, D), F32)]
    else:
        out_specs = [_MOVE_SPEC]
        out_shape = [jax.ShapeDtypeStruct((T, D), F32)]
    return pl.pallas_call(
        functools.partial(_combine_kernel, final=final),
        grid=(N_MOVE_STEPS,),
        in_specs=[_smem_step_spec(lambda i: i), _smem_step_spec(lambda i: jnp.minimum(i + 1, N_MOVE_STEPS - 1)),
                  _MOVE_SPEC, _MOVE_MOD_SPEC, _ROW_SPEC, pl.BlockSpec(memory_space=pl.ANY)],
        out_specs=out_specs,
        scratch_shapes=[pltpu.VMEM((2, MOVE_ROWS, D), F32), pltpu.SemaphoreType.DMA((2,))],
        out_shape=out_shape,
        compiler_params=_cparams(),
        name="moe_combine_final" if final else "moe_combine",
    )(slot, slot, x1, mod, final_g, y_sorted)


CONV_ROWS = CONV_TILES * TM
_GATHER_GROUPS = 3 * CONV_TILES


def _combine_conv_kernel(slot_ref, next_slot_ref, x1_ref, mod0_ref, mod_ref, g_ref, w_hbm, cw_ref, y_ref,
                         x2_ref, a_ref, w_ref, stage_ref, sem, row_sem):
    i = pl.program_id(0)

    def gather(slots, lo, hi):
        for r in range(lo, hi):
            pltpu.make_async_copy(y_ref.at[pl.ds(slots[r // TM, 0, r % TM], 1)],
                                  stage_ref.at[0, pl.ds(r, 1)], row_sem.at[0]).start(priority=1)

    def wait_rows():
        pltpu.make_async_copy(y_ref.at[pl.ds(0, CONV_ROWS)], stage_ref.at[0], row_sem.at[0]).wait()

    @pl.when(i == 0)
    def _():
        _load_weight_bf16(w_hbm, w_ref, stage_ref, sem)
        gather(slot_ref, 0, CONV_ROWS)

    wait_rows()
    x2_ref[...] = x1_ref[...] + mod0_ref[0, 5] * stage_ref[0]

    seg = jnp.where(i < PROMPT_TILES // CONV_TILES, PROMPT_LEN, GRID_W)
    pos = lax.broadcasted_iota(jnp.int32, (TM, 1), 0) & (seg - 1)
    tiles = [slice(s * TM, (s + 1) * TM) for s in range(CONV_TILES)]
    groups = [g * CONV_ROWS // _GATHER_GROUPS for g in range(_GATHER_GROUPS + 1)]

    def project(s):
        h = _modulate(x2_ref[tiles[s]], g_ref[...], mod_ref[0, 0], mod_ref[0, 1]).astype(BF16)
        parts = []
        for k in range(3):
            gather(next_slot_ref, groups[3 * s + k], groups[3 * s + k + 1])
            parts.append(_dot(h, w_ref[:, k * D:(k + 1) * D]))
        return parts

    def convolve(rows, bg, cg, x_in):
        u = cg * x_in
        prev = jnp.where(pos == 0, 0.0, pltpu.roll(u, 1, axis=0))
        nxt = jnp.where(pos == seg - 1, 0.0, pltpu.roll(u, TM - 1, axis=0))
        conv = cw_ref[0:1] * prev + cw_ref[1:2] * u + cw_ref[2:3] * nxt
        a_ref[rows] = (bg * conv).astype(BF16)

    parts = project(0)
    for s in range(1, CONV_TILES):
        nxt_parts = project(s)
        convolve(tiles[s - 1], *parts)
        parts = nxt_parts
    convolve(tiles[-1], *parts)

    @pl.when(i == N_TILES // CONV_TILES - 1)
    def _():
        wait_rows()


def _combine_conv(slot, x1, mod0, y_sorted, mod, norm_g, w_in, conv_w):
    n_steps = N_TILES // CONV_TILES
    block = pl.BlockSpec((CONV_ROWS, D), lambda i: (i, 0))
    mod_spec = pl.BlockSpec((1, 6, 1, D), lambda i: (_cond_of_tile(i * CONV_TILES), 0, 0, 0))
    slots = lambda step: pl.BlockSpec((CONV_TILES, 1, TM), lambda i: (step(i), 0, 0), memory_space=pltpu.SMEM)
    return pl.pallas_call(
        _combine_conv_kernel,
        grid=(n_steps,),
        in_specs=[slots(lambda i: i), slots(lambda i: jnp.minimum(i + 1, n_steps - 1)), block, mod_spec, mod_spec,
                  _ROW_SPEC, _ANY_SPEC, pl.BlockSpec((3, D), lambda i: (0, 0)), _ANY_SPEC],
        out_specs=[block, block],
        out_shape=[jax.ShapeDtypeStruct((T, D), F32), jax.ShapeDtypeStruct((T, D), BF16)],
        scratch_shapes=_weight_scratch(3) + [pltpu.SemaphoreType.DMA((1,))],
        compiler_params=_cparams(),
        name="moe_combine_conv",
    )(slot, slot, x1, mod0, mod, norm_g, w_in, conv_w, y_sorted)


def _plan_kernel(cnt_ref, bucket_ref, rank_ref, slot_ref,
                 ea_ref, eb_ref, next_a_ref, next_b_ref, a_higher_ref, rows_ref, xblock_ref, zero_from_ref):
    slot = rank_ref[...]
    bucket = bucket_ref[...]
    tile0 = jnp.int32(0)
    last = [jnp.int32(0)] * 3
    for b in range(N_BUCKETS):
        n = cnt_ref[b]
        tiles = (n + (TM_E - 1)) // TM_E
        pair = b % N_PAIRS
        per_tile = ((b // N_PAIRS) * EPG + _SLOT_A[pair], (b // N_PAIRS) * EPG + _SLOT_B[pair],
                    _SLOT_A_IS_HIGHER[pair])

        def fill(k, carry, n=n, tile0=tile0, per_tile=per_tile):
            j = tile0 + k
            live = jnp.minimum(n - k * TM_E, TM_E)
            rows_ref[j] = live
            xblock_ref[j] = j
            zero_from_ref[j] = live // ZERO_ROWS
            for ref, value in zip((ea_ref, eb_ref, a_higher_ref), per_tile):
                ref[j] = jnp.int32(value)
            return carry

        lax.fori_loop(0, tiles, fill, 0)
        slot = slot + jnp.where(bucket == b, tile0 * TM_E, 0)
        last = [jnp.where(tiles > 0, value, old) for value, old in zip(per_tile, last)]
        tile0 = tile0 + tiles

    def unused(j, carry):
        rows_ref[j] = jnp.int32(0)
        xblock_ref[j] = jnp.maximum(tile0 - 1, 0)
        zero_from_ref[j] = jnp.int32(0)
        for ref, value in zip((ea_ref, eb_ref, a_higher_ref), last):
            ref[j] = value
        return carry

    lax.fori_loop(tile0, N_TILES_E, unused, 0)

    for e_ref, next_ref in ((ea_ref, next_a_ref), (eb_ref, next_b_ref)):
        next_ref[N_TILES_E - 1] = jnp.int32(-1)

        def backward(k, carry, e_ref=e_ref, next_ref=next_ref):
            j = N_TILES_E - 2 - k
            next_ref[j] = jnp.where(e_ref[j] == e_ref[j + 1], next_ref[j + 1], e_ref[j + 1])
            return carry

        lax.fori_loop(0, N_TILES_E - 1, backward, 0)
    slot_ref[...] = slot


def _plan(counts, bucket, rank):
    whole = pl.BlockSpec(bucket.shape, lambda i, *_: (0, 0, 0))
    smem = pl.BlockSpec(memory_space=pltpu.SMEM)
    per_tile = jax.ShapeDtypeStruct((N_TILES_E,), jnp.int32)
    slot, *tile_plan = pl.pallas_call(
        _plan_kernel,
        grid_spec=pltpu.PrefetchScalarGridSpec(
            num_scalar_prefetch=1, grid=(1,), in_specs=[whole, whole], out_specs=[whole] + [smem] * 8),
        out_shape=[jax.ShapeDtypeStruct(bucket.shape, jnp.int32)] + [per_tile] * 8,
        compiler_params=_cparams(),
        name="moe_plan",
    )(counts[:, 0].astype(jnp.int32), bucket, rank)
    return slot, tile_plan


def _moe(a, x_parts, mod, norm_g2, w_out, router_wt, router_bt, w_gate, w_up, w_down, layer, final_g,
         *, final):
    x1, gates, bucket, rank, counts = _post_mixer(a, x_parts, mod, norm_g2, w_out, router_wt, router_bt)
    slot, (*expert_plan, zero_from) = _plan(counts, bucket, rank)
    xs = _dispatch(zero_from, slot, x1, mod, norm_g2, gates)
    ys = _experts(*expert_plan, xs, w_gate, w_up, w_down, layer)
    if not final:
        return slot, x1, ys
    return _combine(slot, x1, mod, final_g, ys, final=True)


def kernel(x_prompt, x_sample, state_hgrn, c, c_ctx, w_mod, b_mod, norm_mix_g, norm_ffn_g, norm_final_g,
           hgrn_w_in, hgrn_lb_logits, hgrn_norm_g, hgrn_w_out, conv_w_in, conv_w, conv_w_out,
           router_w, router_b, moe_w_gate, moe_w_up, moe_w_down):
    x_parts = (x_prompt.reshape(T_PROMPT, D), x_sample.reshape(T_SAMPLE, D))
    cond = jnp.concatenate([c_ctx[None], c, jnp.zeros((COND_ROWS - N_COND, D), F32)], axis=0)
    mods = _adaln(cond, w_mod, b_mod)[:, :N_COND].reshape(2, N_COND, 6, 1, D)
    member_major = lambda w: w.reshape(N_GROUPS, EPG, -1).transpose(1, 0, 2).reshape(N_EXPERTS, -1)
    rwt = member_major(router_w.T)
    rbt = member_major(router_b.reshape(N_EXPERTS, 1))
    final_g = norm_final_g.reshape(1, D)

    q, v, ff, fb, gg = _hgrn_proj(x_parts, mods[0], norm_mix_g[0:1], hgrn_w_in[0])
    s0 = state_hgrn[:, 0]
    o_f, s_f = _gla_fwd(q, v, ff, hgrn_lb_logits, s0)
    a, new_state = _gla_bwd(q, v, fb, hgrn_lb_logits, s0, o_f, gg, hgrn_norm_g[0:1], s_f)
    slot, x1, ys = _moe(a, x_parts, mods[0], norm_ffn_g[0:1], hgrn_w_out[0], rwt, rbt,
                        moe_w_gate, moe_w_up, moe_w_down, 0, final_g, final=False)

    x, a = _combine_conv(slot, x1, mods[0], ys, mods[1], norm_mix_g[1:2], conv_w_in[0], conv_w[0])
    y_p, y_s = _moe(a, (x,), mods[1], norm_ffn_g[1:2], conv_w_out[0], rwt, rbt,
                    moe_w_gate, moe_w_up, moe_w_down, 1, final_g, final=True)

    return (y_p.reshape(N_PROMPT_SEQ, PROMPT_LEN, D), y_s.reshape(N_SAMPLE_SEQ, SAMPLE_LEN, D),
            new_state[:, None])
```
